```python
import jax, jax.numpy as jnp
from jax import lax
import numpy as np

D_MODEL = 1024
BATCH = 8
SEQ = 8192
DEPTH = 2

HEAD_DIM = 128
HEADS_PER_GROUP = 4
DILATION_GROUPS = ((128, 1), (512, 4), (2048, 16))
N_GROUPS = 3
ATT_WIDTH = HEADS_PER_GROUP * HEAD_DIM
QKV_WIDTH = N_GROUPS * ATT_WIDTH
BAND_BLOCK = 128
LRU_WIDTH = D_MODEL
LRU_BLOCKS = 16
LRU_BLOCK_DIM = LRU_WIDTH // LRU_BLOCKS
CONV_WIDTH = 4
LRU_C = 8.0
N_BRANCHES = 2
NORM_EPS = 1e-6
NEG_INF = -1e30
SPLIT_SIZES = (QKV_WIDTH, QKV_WIDTH, QKV_WIDTH, ATT_WIDTH, LRU_WIDTH, LRU_WIDTH, N_BRANCHES * D_MODEL)
IN_WIDTH = 3 * QKV_WIDTH + ATT_WIDTH + 2 * LRU_WIDTH + N_BRANCHES * D_MODEL

kernel_name = "hybrid_dilated_attn_rglru_block"


def rms_norm(x, gain):
    xf = x.astype(jnp.float32)
    y = xf * lax.rsqrt(jnp.mean(xf * xf, axis=-1, keepdims=True) + NORM_EPS)
    return (y * gain.astype(jnp.float32)).astype(x.dtype)


def dilated_window_group(q, k, v, window, dilation):
    B, S, H, Dh = q.shape
    span = window // dilation
    L = S // dilation
    n_blk = -(-L // BAND_BLOCK)
    Lp = n_blk * BAND_BLOCK

    def to_sub(t):
        t = t.astype(jnp.float32).reshape(B, L, dilation, H, Dh).transpose(0, 2, 1, 3, 4)
        return jnp.pad(t, ((0, 0), (0, 0), (0, Lp - L), (0, 0), (0, 0)))

    def band(t):
        t = jnp.pad(to_sub(t), ((0, 0), (0, 0), (BAND_BLOCK, 0), (0, 0), (0, 0)))
        t = t.reshape(B, dilation, n_blk + 1, BAND_BLOCK, H, Dh)
        return jnp.concatenate([t[:, :, :-1], t[:, :, 1:]], axis=3)

    qs = to_sub(q).reshape(B, dilation, n_blk, BAND_BLOCK, H, Dh)
    kb = band(k)
    vb = band(v)
    s = jnp.einsum('brnqhd,brnkhd->brnhqk', qs, kb) * (Dh ** -0.5)
    qi = jnp.arange(BAND_BLOCK)[:, None] + BAND_BLOCK
    ki = jnp.arange(2 * BAND_BLOCK)[None, :]
    dist = qi - ki
    key_pos = jnp.arange(n_blk)[:, None, None] * BAND_BLOCK + ki[None] - BAND_BLOCK
    valid = (dist >= 0) & (dist <= span) & (key_pos >= 0)
    s = jnp.where(valid[None, None, :, None], s, NEG_INF)
    m = jnp.max(s, axis=-1)
    p = jnp.exp(s - m[..., None])
    l = jnp.sum(p, axis=-1)
    m = jnp.swapaxes(m, -1, -2)
    l = jnp.swapaxes(l, -1, -2)
    o = jnp.einsum('brnhqk,brnkhd->brnqhd', p, vb) / l[..., None]

    def from_sub(t):
        t = t.reshape((B, dilation, Lp) + t.shape[4:])[:, :, :L]
        t = jnp.swapaxes(t, 1, 2)
        return t.reshape((B, S) + t.shape[3:])

    return from_sub(o), from_sub(m), from_sub(l)


def dilated_attention(q, k, v):
    B, S, _ = q.shape
    shp = (B, S, N_GROUPS, HEADS_PER_GROUP, HEAD_DIM)
    qg, kg, vg = q.reshape(shp), k.reshape(shp), v.reshape(shp)
    outs, maxes, dens = [], [], []
    for g, (window, dilation) in enumerate(DILATION_GROUPS):
        o, m, l = dilated_window_group(qg[:, :, g], kg[:, :, g], vg[:, :, g], window, dilation)
        outs.append(o)
        maxes.append(m)
        dens.append(l)
    o = jnp.stack(outs, 0)
    m = jnp.stack(maxes, 0)
    l = jnp.stack(dens, 0)
    wgt = l * jnp.exp(m - jnp.max(m, axis=0, keepdims=True))
    o = jnp.sum(wgt[..., None] * o, axis=0) / jnp.sum(wgt, axis=0)[..., None]
    return o.reshape(B, S, ATT_WIDTH).astype(q.dtype)


def causal_depthwise_conv(u, w, b):
    S = u.shape[1]
    up = jnp.pad(u, ((0, 0), (CONV_WIDTH - 1, 0), (0, 0)))
    y = b
    for j in range(CONV_WIDTH):
        y = y + up[:, CONV_WIDTH - 1 - j: CONV_WIDTH - 1 - j + S] * w[j]
    return y


def rg_lru(u, w_rg, b_rg, w_ig, b_ig, lru_lambda):
    B, S, _ = u.shape
    uf = u.astype(jnp.float32)
    ub = uf.reshape(B, S, LRU_BLOCKS, LRU_BLOCK_DIM)
    r = jax.nn.sigmoid(jnp.einsum('bshi,hij->bshj', ub, w_rg.astype(jnp.float32)).reshape(B, S, LRU_WIDTH) + b_rg.astype(jnp.float32))
    i = jax.nn.sigmoid(jnp.einsum('bshi,hij->bshj', ub, w_ig.astype(jnp.float32)).reshape(B, S, LRU_WIDTH) + b_ig.astype(jnp.float32))
    log_a = -LRU_C * r * jax.nn.softplus(-lru_lambda.astype(jnp.float32))
    a = jnp.exp(log_a)
    xin = jnp.sqrt(-jnp.expm1(2.0 * log_a)) * (i * uf)

    def combine(left, right):
        a1, b1 = left
        a2, b2 = right
        return a1 * a2, a2 * b1 + b2

    _, h = lax.associative_scan(combine, (a, xin), axis=1)
    return h.astype(u.dtype)


def hybrid_layer(x, c, w_mod, b_mod, g_pre, w_in, conv_w, conv_b, w_rg, b_rg, w_ig, b_ig,
                 lru_lambda, w_pa, w_pb, w_o, g_post):
    mod = jax.nn.silu(c) @ w_mod + b_mod
    shift, scale, gate = jnp.split(mod[:, None, :], 3, axis=-1)
    h = rms_norm(x, g_pre) * (1 + scale) + shift
    proj = h @ w_in
    split_at = [int(v) for v in np.cumsum(SPLIT_SIZES)[:-1]]
    q, k, v, g_att, u, g_lru, merge = jnp.split(proj, split_at, axis=-1)
    y_a = (dilated_attention(q, k, v) * jax.nn.silu(g_att)) @ w_pa
    u = causal_depthwise_conv(u, conv_w, conv_b)
    y_b = (rg_lru(u, w_rg, b_rg, w_ig, b_ig, lru_lambda) * jax.nn.silu(g_lru)) @ w_pb
    gate_a, gate_b = jnp.split(jax.nn.sigmoid(merge), N_BRANCHES, axis=-1)
    out = (gate_a * y_a + gate_b * y_b) @ w_o
    return x + gate * rms_norm(out, g_post)


def _fwd_setup_inputs(seed: int = 0) -> dict:
    key = jax.random.key(seed)
    ks = jax.random.split(key, 20)
    f32 = jnp.float32
    nrm = lambda k, shape, s: jax.random.normal(k, shape, f32) * s
    u = jax.random.uniform(ks[13], (DEPTH, LRU_WIDTH), f32, 0.9, 0.999)
    a_base = u ** (1.0 / LRU_C)
    lru_lambda = jnp.log(a_base) - jnp.log1p(-a_base)
    return {
        "x": nrm(ks[0], (BATCH, SEQ, D_MODEL), 1.0),
        "c": nrm(ks[1], (BATCH, D_MODEL), 1.0),
        "w_mod": nrm(ks[2], (DEPTH, D_MODEL, 3 * D_MODEL), 0.5 * D_MODEL ** -0.5),
        "b_mod": nrm(ks[3], (DEPTH, 3 * D_MODEL), 0.01),
        "g_pre": 1.0 + nrm(ks[4], (DEPTH, D_MODEL), 0.05),
        "w_in": nrm(ks[5], (DEPTH, D_MODEL, IN_WIDTH), D_MODEL ** -0.5),
        "conv_w": nrm(ks[6], (DEPTH, CONV_WIDTH, LRU_WIDTH), CONV_WIDTH ** -0.5),
        "conv_b": nrm(ks[7], (DEPTH, LRU_WIDTH), 0.01),
        "w_rg": nrm(ks[8], (DEPTH, LRU_BLOCKS, LRU_BLOCK_DIM, LRU_BLOCK_DIM), LRU_BLOCK_DIM ** -0.5),
        "b_rg": nrm(ks[9], (DEPTH, LRU_WIDTH), 0.01),
        "w_ig": nrm(ks[10], (DEPTH, LRU_BLOCKS, LRU_BLOCK_DIM, LRU_BLOCK_DIM), LRU_BLOCK_DIM ** -0.5),
        "b_ig": nrm(ks[11], (DEPTH, LRU_WIDTH), 0.01),
        "lru_lambda": lru_lambda,
        "w_pa": nrm(ks[14], (DEPTH, ATT_WIDTH, D_MODEL), ATT_WIDTH ** -0.5),
        "w_pb": nrm(ks[15], (DEPTH, LRU_WIDTH, D_MODEL), LRU_WIDTH ** -0.5),
        "w_o": nrm(ks[16], (DEPTH, D_MODEL, D_MODEL), D_MODEL ** -0.5),
        "g_post": 1.0 + nrm(ks[17], (DEPTH, D_MODEL), 0.05),
    }


def _fwd_reference(x, c, w_mod, b_mod, g_pre, w_in, conv_w, conv_b, w_rg, b_rg, w_ig, b_ig,
              lru_lambda, w_pa, w_pb, w_o, g_post):
    for layer in range(DEPTH):
        x = hybrid_layer(x, c, w_mod[layer], b_mod[layer], g_pre[layer], w_in[layer],
                         conv_w[layer], conv_b[layer], w_rg[layer], b_rg[layer],
                         w_ig[layer], b_ig[layer], lru_lambda[layer], w_pa[layer],
                         w_pb[layer], w_o[layer], g_post[layer])
    return x


import jax as _jax
import jax.numpy as _jnp

TWIN_FORMAT = 'train_step'
FWD_PARAMS = ['x', 'c', 'w_mod', 'b_mod', 'g_pre', 'w_in', 'conv_w', 'conv_b', 'w_rg', 'b_rg', 'w_ig', 'b_ig', 'lru_lambda', 'w_pa', 'w_pb', 'w_o', 'g_post']
TWIN_WEIGHTS = ['w_mod', 'b_mod', 'g_pre', 'w_in', 'conv_w', 'conv_b', 'w_rg', 'b_rg', 'w_ig', 'b_ig', 'lru_lambda', 'w_pa', 'w_pb', 'w_o', 'g_post']
TWIN_DIFF_INPUT = 'x'
TWIN_INPUTS = ['x', 'c', 'w_mod', 'b_mod', 'g_pre', 'w_in', 'conv_w', 'conv_b', 'w_rg', 'b_rg', 'w_ig', 'b_ig', 'lru_lambda', 'w_pa', 'w_pb', 'w_o', 'g_post', 'loss_target', 'm_w_mod', 'm_b_mod', 'm_g_pre', 'm_w_in', 'm_conv_w', 'm_conv_b', 'm_w_rg', 'm_b_rg', 'm_w_ig', 'm_b_ig', 'm_lru_lambda', 'm_w_pa', 'm_w_pb', 'm_w_o', 'm_g_post', 'v_w_mod', 'v_b_mod', 'v_g_pre', 'v_w_in', 'v_conv_w', 'v_conv_b', 'v_w_rg', 'v_b_rg', 'v_w_ig', 'v_b_ig', 'v_lru_lambda', 'v_w_pa', 'v_w_pb', 'v_w_o', 'v_g_post']
TWIN_OUTPUTS = ['loss', 'grad_x', 'grad_w_mod', 'grad_b_mod', 'grad_g_pre', 'grad_w_in', 'grad_conv_w', 'grad_conv_b', 'grad_w_rg', 'grad_b_rg', 'grad_w_ig', 'grad_b_ig', 'grad_lru_lambda', 'grad_w_pa', 'grad_w_pb', 'grad_w_o', 'grad_g_post', 'delta_w_mod', 'delta_b_mod', 'delta_g_pre', 'delta_w_in', 'delta_conv_w', 'delta_conv_b', 'delta_w_rg', 'delta_b_rg', 'delta_w_ig', 'delta_b_ig', 'delta_lru_lambda', 'delta_w_pa', 'delta_w_pb', 'delta_w_o', 'delta_g_post', 'new_m_w_mod', 'new_m_b_mod', 'new_m_g_pre', 'new_m_w_in', 'new_m_conv_w', 'new_m_conv_b', 'new_m_w_rg', 'new_m_b_rg', 'new_m_w_ig', 'new_m_b_ig', 'new_m_lru_lambda', 'new_m_w_pa', 'new_m_w_pb', 'new_m_w_o', 'new_m_g_post', 'new_v_w_mod', 'new_v_b_mod', 'new_v_g_pre', 'new_v_w_in', 'new_v_conv_w', 'new_v_conv_b', 'new_v_w_rg', 'new_v_b_rg', 'new_v_w_ig', 'new_v_b_ig', 'new_v_lru_lambda', 'new_v_w_pa', 'new_v_w_pb', 'new_v_w_o', 'new_v_g_post']
TWIN_LEAF_KINDS = {'loss': 'loss', 'grad_x': 'grad_x', 'grad_w_mod': 'grad_w', 'grad_b_mod': 'grad_w', 'grad_g_pre': 'grad_w', 'grad_w_in': 'grad_w', 'grad_conv_w': 'grad_w', 'grad_conv_b': 'grad_w', 'grad_w_rg': 'grad_w', 'grad_b_rg': 'grad_w', 'grad_w_ig': 'grad_w', 'grad_b_ig': 'grad_w', 'grad_lru_lambda': 'grad_w', 'grad_w_pa': 'grad_w', 'grad_w_pb': 'grad_w', 'grad_w_o': 'grad_w', 'grad_g_post': 'grad_w', 'delta_w_mod': 'delta_w', 'delta_b_mod': 'delta_w', 'delta_g_pre': 'delta_w', 'delta_w_in': 'delta_w', 'delta_conv_w': 'delta_w', 'delta_conv_b': 'delta_w', 'delta_w_rg': 'delta_w', 'delta_b_rg': 'delta_w', 'delta_w_ig': 'delta_w', 'delta_b_ig': 'delta_w', 'delta_lru_lambda': 'delta_w', 'delta_w_pa': 'delta_w', 'delta_w_pb': 'delta_w', 'delta_w_o': 'delta_w', 'delta_g_post': 'delta_w', 'new_m_w_mod': 'new_m', 'new_m_b_mod': 'new_m', 'new_m_g_pre': 'new_m', 'new_m_w_in': 'new_m', 'new_m_conv_w': 'new_m', 'new_m_conv_b': 'new_m', 'new_m_w_rg': 'new_m', 'new_m_b_rg': 'new_m', 'new_m_w_ig': 'new_m', 'new_m_b_ig': 'new_m', 'new_m_lru_lambda': 'new_m', 'new_m_w_pa': 'new_m', 'new_m_w_pb': 'new_m', 'new_m_w_o': 'new_m', 'new_m_g_post': 'new_m', 'new_v_w_mod': 'new_v', 'new_v_b_mod': 'new_v', 'new_v_g_pre': 'new_v', 'new_v_w_in': 'new_v', 'new_v_conv_w': 'new_v', 'new_v_conv_b': 'new_v', 'new_v_w_rg': 'new_v', 'new_v_b_rg': 'new_v', 'new_v_w_ig': 'new_v', 'new_v_b_ig': 'new_v', 'new_v_lru_lambda': 'new_v', 'new_v_w_pa': 'new_v', 'new_v_w_pb': 'new_v', 'new_v_w_o': 'new_v', 'new_v_g_post': 'new_v'}


def _forward(args):
    return _fwd_reference(*[args[k] for k in FWD_PARAMS])


def _output_shape():
    def fwd():
        inp = _fwd_setup_inputs(0)
        return _fwd_reference(*[inp[k] for k in FWD_PARAMS])
    out = _jax.eval_shape(fwd)
    return out.shape, out.dtype

N_MICROBATCH = 1
ADAM_LR = 0.001
ADAM_B1 = 0.9
ADAM_B2 = 0.999
ADAM_EPS = 1e-08
ADAM_WD = 0.01
ADAM_STEP = 10
PER_EXAMPLE_BATCH_AXIS = {'x': 0, 'c': 0, 'loss_target': 0}
SHARED_INPUTS = []
_WEIGHT_DTYPES = {'w_mod': _jnp.float32, 'b_mod': _jnp.float32, 'g_pre': _jnp.float32, 'w_in': _jnp.float32, 'conv_w': _jnp.float32, 'conv_b': _jnp.float32, 'w_rg': _jnp.float32, 'b_rg': _jnp.float32, 'w_ig': _jnp.float32, 'b_ig': _jnp.float32, 'lru_lambda': _jnp.float32, 'w_pa': _jnp.float32, 'w_pb': _jnp.float32, 'w_o': _jnp.float32, 'g_post': _jnp.float32}
MOMENT_SCALE = {'w_mod': 2.656929e+00, 'b_mod': 5.519504e+00, 'g_pre': 1.976406e-01, 'w_in': 1.730019e-01, 'conv_w': 4.822728e-01, 'conv_b': 1.622689e+00, 'w_rg': 5.975715e-02, 'b_rg': 9.537356e-02, 'w_ig': 1.333091e-01, 'b_ig': 2.123152e-01, 'lru_lambda': 2.602018e-01, 'w_pa': 4.341376e-02, 'w_pb': 5.350676e-01, 'w_o': 5.180427e-01, 'g_post': 6.758964e+00}


def _to_microbatches(a, axis):
    t = _jnp.moveaxis(a, axis, 0)
    t = t.reshape((N_MICROBATCH, t.shape[0] // N_MICROBATCH) + t.shape[1:])
    return _jnp.moveaxis(t, 1, axis + 1)


def setup_inputs(seed: int = 0) -> dict:
    inp = _fwd_setup_inputs(seed)
    key = _jax.random.fold_in(_jax.random.key(seed), 7919)
    shape, _ = _output_shape()
    out = dict(inp)
    out["loss_target"] = _jax.random.normal(_jax.random.fold_in(key, 0), shape, _jnp.float32)
    for i, name in enumerate(TWIN_WEIGHTS):
        w = inp[name].astype(_jnp.float32)
        if MOMENT_SCALE is None:
            s = _jnp.sqrt(_jnp.mean(_jnp.square(w)) + 1e-30)
        else:
            s = MOMENT_SCALE[name]
        km, kv = _jax.random.split(_jax.random.fold_in(key, i + 1))
        out[name] = w
        out["m_" + name] = s * _jax.random.normal(km, w.shape, _jnp.float32)
        out["v_" + name] = (s * s) * _jax.random.uniform(kv, w.shape, _jnp.float32, 0.5, 1.5)
    if N_MICROBATCH > 1:
        for name, axis in PER_EXAMPLE_BATCH_AXIS.items():
            out[name] = _to_microbatches(out[name], axis)
    return {'x': out['x'], 'c': out['c'], 'w_mod': out['w_mod'], 'b_mod': out['b_mod'], 'g_pre': out['g_pre'], 'w_in': out['w_in'], 'conv_w': out['conv_w'], 'conv_b': out['conv_b'], 'w_rg': out['w_rg'], 'b_rg': out['b_rg'], 'w_ig': out['w_ig'], 'b_ig': out['b_ig'], 'lru_lambda': out['lru_lambda'], 'w_pa': out['w_pa'], 'w_pb': out['w_pb'], 'w_o': out['w_o'], 'g_post': out['g_post'], 'loss_target': out['loss_target'], 'm_w_mod': out['m_w_mod'], 'm_b_mod': out['m_b_mod'], 'm_g_pre': out['m_g_pre'], 'm_w_in': out['m_w_in'], 'm_conv_w': out['m_conv_w'], 'm_conv_b': out['m_conv_b'], 'm_w_rg': out['m_w_rg'], 'm_b_rg': out['m_b_rg'], 'm_w_ig': out['m_w_ig'], 'm_b_ig': out['m_b_ig'], 'm_lru_lambda': out['m_lru_lambda'], 'm_w_pa': out['m_w_pa'], 'm_w_pb': out['m_w_pb'], 'm_w_o': out['m_w_o'], 'm_g_post': out['m_g_post'], 'v_w_mod': out['v_w_mod'], 'v_b_mod': out['v_b_mod'], 'v_g_pre': out['v_g_pre'], 'v_w_in': out['v_w_in'], 'v_conv_w': out['v_conv_w'], 'v_conv_b': out['v_conv_b'], 'v_w_rg': out['v_w_rg'], 'v_b_rg': out['v_b_rg'], 'v_w_ig': out['v_w_ig'], 'v_b_ig': out['v_b_ig'], 'v_lru_lambda': out['v_lru_lambda'], 'v_w_pa': out['v_w_pa'], 'v_w_pb': out['v_w_pb'], 'v_w_o': out['v_w_o'], 'v_g_post': out['v_g_post']}


def _loss(weights, diff, rest, loss_target):
    with _jax.named_scope("forward"):
        args = {**rest, TWIN_DIFF_INPUT: diff, **{k: w.astype(_WEIGHT_DTYPES[k]) for k, w in weights.items()}}
        y = _forward(args)
    with _jax.named_scope("loss_head"):
        err = _jnp.square(y.astype(_jnp.float32) - loss_target)
        return 0.5 * _jnp.sum(_jnp.mean(err, axis=-1)) if err.ndim else 0.5 * err


def _adamw(w, g, m, v):
    m = ADAM_B1 * m + (1.0 - ADAM_B1) * g
    v = ADAM_B2 * v + (1.0 - ADAM_B2) * _jnp.square(g)
    m_hat = m / (1.0 - ADAM_B1 ** ADAM_STEP)
    v_hat = v / (1.0 - ADAM_B2 ** ADAM_STEP)
    delta = -ADAM_LR * (m_hat / (_jnp.sqrt(v_hat) + ADAM_EPS) + ADAM_WD * w)
    return delta, m, v


def reference(x, c, w_mod, b_mod, g_pre, w_in, conv_w, conv_b, w_rg, b_rg, w_ig, b_ig, lru_lambda, w_pa, w_pb, w_o, g_post, loss_target, m_w_mod, m_b_mod, m_g_pre, m_w_in, m_conv_w, m_conv_b, m_w_rg, m_b_rg, m_w_ig, m_b_ig, m_lru_lambda, m_w_pa, m_w_pb, m_w_o, m_g_post, v_w_mod, v_b_mod, v_g_pre, v_w_in, v_conv_w, v_conv_b, v_w_rg, v_b_rg, v_w_ig, v_b_ig, v_lru_lambda, v_w_pa, v_w_pb, v_w_o, v_g_post):
    given = dict(x=x, c=c, w_mod=w_mod, b_mod=b_mod, g_pre=g_pre, w_in=w_in, conv_w=conv_w, conv_b=conv_b, w_rg=w_rg, b_rg=b_rg, w_ig=w_ig, b_ig=b_ig, lru_lambda=lru_lambda, w_pa=w_pa, w_pb=w_pb, w_o=w_o, g_post=g_post, loss_target=loss_target, m_w_mod=m_w_mod, m_b_mod=m_b_mod, m_g_pre=m_g_pre, m_w_in=m_w_in, m_conv_w=m_conv_w, m_conv_b=m_conv_b, m_w_rg=m_w_rg, m_b_rg=m_b_rg, m_w_ig=m_w_ig, m_b_ig=m_b_ig, m_lru_lambda=m_lru_lambda, m_w_pa=m_w_pa, m_w_pb=m_w_pb, m_w_o=m_w_o, m_g_post=m_g_post, v_w_mod=v_w_mod, v_b_mod=v_b_mod, v_g_pre=v_g_pre, v_w_in=v_w_in, v_conv_w=v_conv_w, v_conv_b=v_conv_b, v_w_rg=v_w_rg, v_b_rg=v_b_rg, v_w_ig=v_w_ig, v_b_ig=v_b_ig, v_lru_lambda=v_lru_lambda, v_w_pa=v_w_pa, v_w_pb=v_w_pb, v_w_o=v_w_o, v_g_post=v_g_post)
    weights = {n: given[n] for n in TWIN_WEIGHTS}
    shared = {n: given[n] for n in SHARED_INPUTS}
    per_example = {n: given[n] for n in ['x', 'c']}
    grad_fn = _jax.value_and_grad(_loss, argnums=(0, 1))

    def one_microbatch(ex, loss_target):
        ex = dict(ex)
        diff = ex.pop(TWIN_DIFF_INPUT)
        return grad_fn(weights, diff, {**shared, **ex}, loss_target)

    if N_MICROBATCH == 1:
        loss, (grad_w, grad_x) = one_microbatch(per_example, given["loss_target"])
    else:
        def body(carry, xs):
            loss_sum, grad_sum = carry
            l_k, (gw_k, gx_k) = one_microbatch(xs[0], xs[1])
            with _jax.named_scope("update"):
                return (loss_sum + l_k, _jax.tree.map(_jnp.add, grad_sum, gw_k)), gx_k

        init = (_jnp.zeros((), _jnp.float32), _jax.tree.map(_jnp.zeros_like, weights))
        (loss, grad_w), grad_x = _jax.lax.scan(body, init, (per_example, given["loss_target"]))
    with _jax.named_scope("update"):
        delta_w, new_m, new_v = {}, {}, {}
        for n in TWIN_WEIGHTS:
            delta_w[n], new_m[n], new_v[n] = _adamw(weights[n], grad_w[n], given["m_" + n], given["v_" + n])
    return (loss, grad_x, *[grad_w[n] for n in TWIN_WEIGHTS], *[delta_w[n] for n in TWIN_WEIGHTS],
            *[new_m[n] for n in TWIN_WEIGHTS], *[new_v[n] for n in TWIN_WEIGHTS])
```

```python
import functools

import jax
import jax.numpy as jnp
from jax import lax
from jax.experimental import pallas as pl
from jax.experimental.pallas import tpu as pltpu

_F32 = jnp.float32
_MXU = jnp.bfloat16
_VMEM_LIMIT = 56 * 1024 * 1024
_MESH = pl.DeviceIdType.MESH

D = 1024
HEAD = 128
HEADS = 4
ATT_W = 512
QKV_W = 1536
IN_W = 9216
DILATIONS = (1, 4, 16)
BAND = 128
QBLK = BAND * 16
NORM_EPS = 1e-6
NEG_INF = -1e30
LRU_C = 8.0
N_CHIPS = 4
CB_GATT = 4608 // 512
CB_U, CB_GLRU, CB_MA, CB_MB = 5, 6, 7, 8

ADAM_LR, ADAM_B1, ADAM_B2, ADAM_EPS, ADAM_WD, ADAM_STEP = 0.001, 0.9, 0.999, 1e-08, 0.01, 10


def _params(ngrid):
    return pltpu.CompilerParams(dimension_semantics=("arbitrary",) * ngrid, vmem_limit_bytes=_VMEM_LIMIT)


def _sigmoid(v):
    return 1.0 / (1.0 + jnp.exp(-v))


_GROUPS = {
    "c": [(0, 0, 1)],
    "xy": [(1, 0, 0), (0, 1, 0), (1, 1, 0)],
    "xyc": [(0, 0, 1), (0, 1, 0), (0, 1, 1), (1, 0, 0), (1, 0, 1), (1, 1, 0), (1, 1, 1)],
}


def _exchange(name, srcs, group, scatter):
    rels = _GROUPS[group]
    gsize = len(rels) + 1
    n = len(srcs)
    blks = [s.shape[1:] if scatter else s.shape for s in srcs]

    def body(*refs):
        src_refs, out_refs = refs[:n], refs[n:2 * n]
        send_sems, recv_sems, loc_sems = refs[2 * n:]
        x, y, c = lax.axis_index("x"), lax.axis_index("y"), lax.axis_index("c")

        def rank(px, py, pc):
            if group == "c":
                return pc
            if group == "xy":
                return 2 * px + py
            return 4 * px + 2 * py + pc

        me = rank(x, y, c)
        copies = []
        for a in range(n):
            def part(r, a=a):
                return src_refs[a].at[r] if scatter else src_refs[a]
            loc = pltpu.make_async_copy(part(me), out_refs[a].at[me], loc_sems.at[a])
            loc.start()
            copies.append(loc)
            for k, (dx, dy, dc) in enumerate(rels):
                px = 1 - x if dx else x
                py = 1 - y if dy else y
                pc = 1 - c if dc else c
                cp = pltpu.make_async_remote_copy(
                    src_ref=part(rank(px, py, pc)), dst_ref=out_refs[a].at[me],
                    send_sem=send_sems.at[a * len(rels) + k], recv_sem=recv_sems.at[a * len(rels) + k],
                    device_id=(px, py, pc), device_id_type=_MESH)
                cp.start()
                copies.append(cp)
        for cp in copies:
            cp.wait()

    any_spec = pl.BlockSpec(memory_space=pl.ANY)
    return pl.pallas_call(
        body, name=name,
        out_shape=[jax.ShapeDtypeStruct((gsize,) + tuple(b), s.dtype) for b, s in zip(blks, srcs)],
        in_specs=[any_spec] * n, out_specs=[any_spec] * n,
        scratch_shapes=[pltpu.SemaphoreType.DMA((n * len(rels),)), pltpu.SemaphoreType.DMA((n * len(rels),)),
                        pltpu.SemaphoreType.DMA((n,))],
    )(*srcs)


def _mm(name, a, b, out_sds, *, grid, a_spec, b_spec, o_spec, dims, acc_shape, into=None):
    nk = grid[2]

    def body(*refs):
        a_ref, b_ref = refs[0], refs[1]
        o_ref, acc = refs[-2], refs[-1]
        k = pl.program_id(2)

        @pl.when(k == 0)
        def _():
            acc[...] = jnp.zeros_like(acc)

        acc[...] += lax.dot_general(a_ref[...].astype(_MXU), b_ref[...].astype(_MXU), dims,
                                    preferred_element_type=_F32)

        @pl.when(k == nk - 1)
        def _():
            o_ref[...] = acc[...].astype(o_ref.dtype)

    in_specs = [a_spec, b_spec]
    args = [a, b]
    aliases = {}
    if into is not None:
        in_specs.append(pl.BlockSpec(memory_space=pl.ANY))
        args.append(into)
        aliases = {2: 0}
    return pl.pallas_call(
        body, name=name, grid=grid, in_specs=in_specs, out_specs=o_spec, out_shape=out_sds,
        scratch_shapes=[pltpu.VMEM(acc_shape, _F32)], input_output_aliases=aliases,
        compiler_params=_params(3))(*args)


_NN = (((1,), (0,)), ((), ()))
_NT = (((1,), (1,)), ((), ()))
_TN = (((0,), (0,)), ((), ()))


def _rowwise(name, body, *, grid, ins, outs, scratch=()):
    return pl.pallas_call(
        body, name=name, grid=(grid,), in_specs=[s for _, s in ins], out_specs=[s for _, s in outs],
        out_shape=[o for o, _ in outs], scratch_shapes=list(scratch),
        compiler_params=_params(1))(*[a for a, _ in ins])


def _rows(tb, w, cb=0, n=None):
    if n is None:
        return pl.BlockSpec((tb, w), lambda i: (i, cb))
    return pl.BlockSpec((tb, w), lambda i: (n - 1 - i, cb))


def _vec(shape):
    return pl.BlockSpec(shape, lambda i: (0,) * len(shape))


def _halo_prev(tb, w, cb=0, n=None):
    if n is None:
        return pl.BlockSpec((8, w), lambda i: (jnp.maximum(i * (tb // 8) - 1, 0), cb))
    return pl.BlockSpec((8, w), lambda i: (jnp.maximum((n - 1 - i) * (tb // 8) - 1, 0), cb))


def _halo_next(tb, w, n, cb=0):
    return pl.BlockSpec((8, w), lambda i: (jnp.minimum((i + 1) * (tb // 8), n * (tb // 8) - 1), cb))


def _sds(shape, dtype=_F32):
    return jax.ShapeDtypeStruct(shape, dtype)


def _cast(name, a, tb):
    rows, cols = a.shape

    def body(a_ref, o_ref):
        o_ref[...] = a_ref[...].astype(o_ref.dtype)

    return _rowwise(name, body, grid=rows // tb, ins=[(a, _rows(tb, cols))],
                    outs=[(_sds((rows, cols), _MXU), _rows(tb, cols))])[0]


def _sum_lead(name, a, tb):
    g, rows, cols = a.shape

    def body(a_ref, o_ref):
        acc = a_ref[0]
        for k in range(1, g):
            acc = acc + a_ref[k]
        o_ref[...] = acc

    return _rowwise(name, body, grid=rows // tb,
                    ins=[(a, pl.BlockSpec((g, tb, cols), lambda i: (0, i, 0)))],
                    outs=[(_sds((rows, cols)), _rows(tb, cols))])[0]


def _adamw(name, w, g, m, v, tb):
    rows, cols = w.shape
    c1 = 1.0 - ADAM_B1 ** ADAM_STEP
    c2 = 1.0 - ADAM_B2 ** ADAM_STEP

    def body(w_ref, g_ref, m_ref, v_ref, d_ref, nm_ref, nv_ref):
        gv = g_ref[...]
        nm = ADAM_B1 * m_ref[...] + (1.0 - ADAM_B1) * gv
        nv = ADAM_B2 * v_ref[...] + (1.0 - ADAM_B2) * (gv * gv)
        d_ref[...] = -ADAM_LR * ((nm / c1) / (jnp.sqrt(nv / c2) + ADAM_EPS) + ADAM_WD * w_ref[...])
        nm_ref[...] = nm
        nv_ref[...] = nv

    spec = _rows(tb, cols)
    return _rowwise(name, body, grid=rows // tb, ins=[(w, spec), (g, spec), (m, spec), (v, spec)],
                    outs=[(_sds((rows, cols)), spec)] * 3)


def _mod_fwd(c_all, w_mod, b_cols):
    cols = w_mod.shape[2]

    def body(c_ref, w_ref, b_ref, o_ref):
        cv = c_ref[...]
        sc = (cv * _sigmoid(cv)).astype(_MXU)
        o_ref[...] = jnp.dot(sc, w_ref[...].astype(_MXU), preferred_element_type=_F32) + b_ref[...]

    return pl.pallas_call(
        body, name="mod_fwd", grid=(2,),
        in_specs=[pl.BlockSpec((8, D), lambda l: (0, 0)), pl.BlockSpec((None, D, cols), lambda l: (l, 0, 0)),
                  pl.BlockSpec((None, 1, cols), lambda l: (l, 0, 0))],
        out_specs=pl.BlockSpec((None, 8, cols), lambda l: (l, 0, 0)),
        out_shape=_sds((2, 8, cols)), compiler_params=_params(1))(c_all, w_mod, b_cols)


def _mod_bwd(c_all_t, dm):
    cols = dm.shape[2]

    def body(c_ref, d_ref, o_ref):
        cv = c_ref[...]
        sc = (cv * _sigmoid(cv)).astype(_MXU)
        o_ref[...] = jnp.dot(sc, d_ref[...].astype(_MXU), preferred_element_type=_F32)

    return pl.pallas_call(
        body, name="mod_bwd", grid=(2,),
        in_specs=[pl.BlockSpec((D, 8), lambda l: (0, 0)), pl.BlockSpec((None, 8, cols), lambda l: (l, 0, 0))],
        out_specs=pl.BlockSpec((None, D, cols), lambda l: (l, 0, 0)),
        out_shape=_sds((2, D, cols)), compiler_params=_params(1))(c_all_t, dm)


def _prenorm_fwd(x, g_pre, shift, scale):
    s = x.shape[0]
    tb = 512

    def body(x_ref, g_ref, sh_ref, sc_ref, h_ref):
        xv = x_ref[...]
        rstd = lax.rsqrt(jnp.mean(xv * xv, axis=-1, keepdims=True) + NORM_EPS)
        h_ref[...] = ((xv * rstd) * g_ref[...] * (1.0 + sc_ref[...]) + sh_ref[...]).astype(h_ref.dtype)

    v = _vec((1, D))
    return _rowwise("prenorm_fwd", body, grid=s // tb,
                    ins=[(x, _rows(tb, D)), (g_pre, v), (shift, v), (scale, v)],
                    outs=[(_sds((s, D), _MXU), _rows(tb, D))])[0]


def _shift_down(cur, halo, j, tb):
    ext = jnp.concatenate([halo, cur], axis=0)
    return pltpu.roll(ext, j, 0)[8:8 + tb]


def _shift_up(cur, halo, j, tb):
    ext = jnp.concatenate([cur, halo], axis=0)
    return pltpu.roll(ext, tb + 8 - j, 0)[0:tb]


def _conv_fwd(proj, conv_w, conv_b):
    s = proj.shape[0]
    tb = 256

    def body(u_ref, hp_ref, w_ref, b_ref, o_ref):
        i = pl.program_id(0)
        u = u_ref[...]
        halo = jnp.where(i > 0, hp_ref[...], 0.0)
        acc = b_ref[...] + u * w_ref[0:1, :]
        for j in range(1, 4):
            acc = acc + _shift_down(u, halo, j, tb) * w_ref[j:j + 1, :]
        o_ref[...] = acc

    return _rowwise("conv_fwd", body, grid=s // tb,
                    ins=[(proj, _rows(tb, D, CB_U)), (proj, _halo_prev(tb, D, CB_U)),
                         (conv_w, _vec((4, D))), (conv_b, _vec((1, D)))],
                    outs=[(_sds((s, D)), _rows(tb, D))])[0]


def _lru_gates(pre_r, pre_i, uc, b_rg, b_ig, lam):
    r = _sigmoid(pre_r + b_rg)
    ig = _sigmoid(pre_i + b_ig)
    nl = -lam
    sp = jnp.maximum(nl, 0.0) + jnp.log(1.0 + jnp.exp(-jnp.abs(nl)))
    la = -LRU_C * r * sp
    a = jnp.exp(la)
    y2 = 2.0 * la
    one_m_a2 = jnp.where(jnp.abs(y2) < 1e-2, -(y2 + 0.5 * y2 * y2 + (1.0 / 6.0) * y2 * y2 * y2),
                         1.0 - jnp.exp(y2))
    sq = jnp.sqrt(one_m_a2)
    return r, ig, sp, a, sq


def _scan_fwd(pre, uc, b_rg, b_ig, lam):
    s = uc.shape[0]
    tb = 256

    def body(pr_ref, pi_ref, uc_ref, brg_ref, big_ref, lam_ref, h_ref, carry):
        i = pl.program_id(0)

        @pl.when(i == 0)
        def _():
            carry[...] = jnp.zeros_like(carry)

        ucv = uc_ref[...]
        _, ig, _, a, sq = _lru_gates(pr_ref[...], pi_ref[...], ucv, brg_ref[...], big_ref[...], lam_ref[...])
        av = a
        bv = sq * (ig * ucv)
        row = lax.broadcasted_iota(jnp.int32, (tb, 1), 0)
        sh = 1
        while sh < tb:
            m = row >= sh
            b_sh = pltpu.roll(bv, sh, 0)
            a_sh = pltpu.roll(av, sh, 0)
            bv = jnp.where(m, av * b_sh + bv, bv)
            av = jnp.where(m, av * a_sh, av)
            sh *= 2
        hv = bv + av * carry[7:8, :]
        h_ref[...] = hv
        carry[...] = hv[tb - 8:tb]

    v = _vec((1, D))
    return _rowwise("scan_fwd", body, grid=s // tb,
                    ins=[(pre, _rows(tb, D, 0)), (pre, _rows(tb, D, 1)), (uc, _rows(tb, D)),
                         (b_rg, v), (b_ig, v), (lam, v)],
                    outs=[(_sds((s, D)), _rows(tb, D))],
                    scratch=[pltpu.VMEM((8, D), _F32)])[0]


def _gating_fwd(og, mg, lg, proj, h_lru):
    s = h_lru.shape[0]
    tb = 256

    def body(o0, o1, o2, m0, m1, m2, l0, l1, l2, ga_ref, h_ref, gl_ref, o_ref, lse_ref, aa_ref, ba_ref):
        ms = [m0[...], m1[...], m2[...]]
        mx = jnp.maximum(jnp.maximum(ms[0], ms[1]), ms[2])
        ws = [l[...] * jnp.exp(m - mx) for l, m in zip((l0, l1, l2), ms)]
        den = ws[0] + ws[1] + ws[2]
        o = (ws[0] * o0[...] + ws[1] * o1[...] + ws[2] * o2[...]) / den
        o_ref[...] = o
        lse_ref[...] = mx + jnp.log(den)
        ga = ga_ref[...]
        aa_ref[...] = (o * (ga * _sigmoid(ga))).astype(aa_ref.dtype)
        gl = gl_ref[...]
        ba_ref[...] = (h_ref[...] * (gl * _sigmoid(gl))).astype(ba_ref.dtype)

    r5 = _rows(tb, ATT_W)
    return _rowwise("gating_fwd", body, grid=s // tb,
                    ins=[(a, r5) for a in og] + [(a, r5) for a in mg] + [(a, r5) for a in lg]
                    + [(proj, _rows(tb, ATT_W, CB_GATT)), (h_lru, _rows(tb, D)), (proj, _rows(tb, D, CB_GLRU))],
                    outs=[(_sds((s, ATT_W)), r5), (_sds((s, ATT_W)), r5), (_sds((s, ATT_W), _MXU), r5),
                          (_sds((s, D), _MXU), _rows(tb, D))])


def _merge_fwd(y_a, y_b, proj):
    s = y_a.shape[0]
    tb = 512

    def body(ya_ref, yb_ref, ma_ref, mb_ref, z_ref):
        z_ref[...] = (_sigmoid(ma_ref[...]) * ya_ref[...] + _sigmoid(mb_ref[...]) * yb_ref[...]).astype(z_ref.dtype)

    return _rowwise("merge_fwd", body, grid=s // tb,
                    ins=[(y_a, _rows(tb, D)), (y_b, _rows(tb, D)), (proj, _rows(tb, D, CB_MA)),
                         (proj, _rows(tb, D, CB_MB))],
                    outs=[(_sds((s, D), _MXU), _rows(tb, D))])[0]


def _post_fwd(x, out, gate, g_post):
    s = x.shape[0]
    tb = 512

    def body(x_ref, o_ref, gt_ref, gp_ref, y_ref):
        ov = o_ref[...]
        rstd = lax.rsqrt(jnp.mean(ov * ov, axis=-1, keepdims=True) + NORM_EPS)
        y_ref[...] = x_ref[...] + gt_ref[...] * ((ov * rstd) * gp_ref[...])

    v = _vec((1, D))
    return _rowwise("post_fwd", body, grid=s // tb,
                    ins=[(x, _rows(tb, D)), (out, _rows(tb, D)), (gate, v), (g_post, v)],
                    outs=[(_sds((s, D)), _rows(tb, D))])[0]


def _loss_head(y, target):
    s = y.shape[0]
    tb = 512

    def body(y_ref, t_ref, dy_ref, acc_ref):
        i = pl.program_id(0)

        @pl.when(i == 0)
        def _():
            acc_ref[...] = jnp.zeros_like(acc_ref)

        err = y_ref[...] - t_ref[...]
        dy_ref[...] = err * (1.0 / D)
        acc_ref[...] += jnp.sum(err * err, axis=0, keepdims=True)

    return _rowwise("loss_head", body, grid=s // tb,
                    ins=[(y, _rows(tb, D)), (target, _rows(tb, D))],
                    outs=[(_sds((s, D)), _rows(tb, D)), (_sds((1, D)), _vec((1, D)))])


def _accumulate(i, ref, val):
    @pl.when(i == 0)
    def _():
        ref[...] = val

    @pl.when(i > 0)
    def _():
        ref[...] += val


def _post_bwd(dx, out, gate, g_post):
    s = dx.shape[0]
    tb = 512

    def body(dx_ref, o_ref, gt_ref, gp_ref, do_ref, dgt_ref, dgp_ref):
        i = pl.program_id(0)
        ov = o_ref[...]
        dxv = dx_ref[...]
        rstd = lax.rsqrt(jnp.mean(ov * ov, axis=-1, keepdims=True) + NORM_EPS)
        nv = ov * rstd
        _accumulate(i, dgt_ref, jnp.sum(dxv * nv, axis=0, keepdims=True) * gp_ref[...])
        _accumulate(i, dgp_ref, jnp.sum(dxv * nv, axis=0, keepdims=True) * gt_ref[...])
        dn = dxv * (gt_ref[...] * gp_ref[...])
        do_ref[...] = (rstd * (dn - nv * jnp.mean(dn * nv, axis=-1, keepdims=True))).astype(do_ref.dtype)

    v = _vec((1, D))
    return _rowwise("post_bwd", body, grid=s // tb,
                    ins=[(dx, _rows(tb, D)), (out, _rows(tb, D)), (gate, v), (g_post, v)],
                    outs=[(_sds((s, D), _MXU), _rows(tb, D)), (_sds((1, D)), v), (_sds((1, D)), v)])


def _merge_bwd(dz, y_a, y_b, proj):
    s = dz.shape[0]
    tb = 256

    def body(dz_ref, ya_ref, yb_ref, ma_ref, mb_ref, dya_ref, dyb_ref, dma_ref, dmb_ref):
        dzv = dz_ref[...]
        ga = _sigmoid(ma_ref[...])
        gb = _sigmoid(mb_ref[...])
        dya_ref[...] = (dzv * ga).astype(dya_ref.dtype)
        dyb_ref[...] = (dzv * gb).astype(dyb_ref.dtype)
        dma_ref[...] = (dzv * ya_ref[...] * ga * (1.0 - ga)).astype(dma_ref.dtype)
        dmb_ref[...] = (dzv * yb_ref[...] * gb * (1.0 - gb)).astype(dmb_ref.dtype)

    r = _rows(tb, D)
    return _rowwise("merge_bwd", body, grid=s // tb,
                    ins=[(dz, r), (y_a, r), (y_b, r), (proj, _rows(tb, D, CB_MA)), (proj, _rows(tb, D, CB_MB))],
                    outs=[(_sds((s, D), _MXU), r)] * 4)


def _gating_bwd(d_aa, o, proj, d_ba, h_lru):
    s = o.shape[0]
    tb = 256

    def body(daa_ref, o_ref, ga_ref, dba_ref, h_ref, gl_ref, do_ref, dga_ref, dh_ref, dgl_ref):
        ga = ga_ref[...]
        sa = _sigmoid(ga)
        daa = daa_ref[...]
        do_ref[...] = daa * (ga * sa)
        dga_ref[...] = (daa * o_ref[...] * (sa * (1.0 + ga * (1.0 - sa)))).astype(dga_ref.dtype)
        gl = gl_ref[...]
        sl = _sigmoid(gl)
        dba = dba_ref[...]
        dh_ref[...] = dba * (gl * sl)
        dgl_ref[...] = (dba * h_ref[...] * (sl * (1.0 + gl * (1.0 - sl)))).astype(dgl_ref.dtype)

    r5, r10 = _rows(tb, ATT_W), _rows(tb, D)
    return _rowwise("gating_bwd", body, grid=s // tb,
                    ins=[(d_aa, r5), (o, r5), (proj, _rows(tb, ATT_W, CB_GATT)), (d_ba, r10), (h_lru, r10),
                         (proj, _rows(tb, D, CB_GLRU))],
                    outs=[(_sds((s, ATT_W)), r5), (_sds((s, ATT_W), _MXU), r5), (_sds((s, D)), r10),
                          (_sds((s, D), _MXU), r10)])


def _scan_bwd(dh, pre, uc, h_lru, b_rg, b_ig, lam):
    s = uc.shape[0]
    tb = 256
    n = s // tb

    def body(dh_ref, pr_ref, pi_ref, uc_ref, h_ref, hp_ref, brg_ref, big_ref, lam_ref,
             dpre_ref, duc_ref, dbrg_ref, dbig_ref, dlam_ref, carry):
        i = pl.program_id(0)

        @pl.when(i == 0)
        def _():
            carry[...] = jnp.zeros_like(carry)

        ucv = uc_ref[...]
        r, ig, sp, a, sq = _lru_gates(pr_ref[...], pi_ref[...], ucv, brg_ref[...], big_ref[...], lam_ref[...])
        row = lax.broadcasted_iota(jnp.int32, (tb, 1), 0)
        cv = jnp.where(row == tb - 1, 1.0, pltpu.roll(a, tb - 1, 0))
        gv = dh_ref[...]
        sh = 1
        while sh < tb:
            m = row < tb - sh
            g_sh = pltpu.roll(gv, tb - sh, 0)
            c_sh = pltpu.roll(cv, tb - sh, 0)
            gv = jnp.where(m, gv + cv * g_sh, gv)
            cv = jnp.where(m, cv * c_sh, cv)
            sh *= 2
        gv = gv + cv * carry[0:1, :]
        carry[...] = (a * gv)[0:8]

        halo = jnp.where(i < n - 1, hp_ref[...], 0.0)
        h_prev = _shift_down(h_ref[...], halo, 1, tb)
        d_a = gv * h_prev
        d_sq = gv * (ig * ucv)
        d_i = gv * sq * ucv
        duc_ref[...] = gv * sq * ig
        d_la = d_a * a - d_sq * (a * a) / sq
        d_r = d_la * (-LRU_C * sp)
        d_pre_r = d_r * r * (1.0 - r)
        d_pre_i = d_i * ig * (1.0 - ig)
        dpre_ref[:, 0:D] = d_pre_r.astype(dpre_ref.dtype)
        dpre_ref[:, D:2 * D] = d_pre_i.astype(dpre_ref.dtype)
        _accumulate(i, dbrg_ref, jnp.sum(d_pre_r, axis=0, keepdims=True))
        _accumulate(i, dbig_ref, jnp.sum(d_pre_i, axis=0, keepdims=True))
        lamv = lam_ref[...]
        _accumulate(i, dlam_ref, jnp.sum(d_la * (-LRU_C * r), axis=0, keepdims=True) * (-_sigmoid(-lamv)))

    v = _vec((1, D))
    rv = _rows(tb, D, 0, n)
    return _rowwise("scan_bwd", body, grid=n,
                    ins=[(dh, rv), (pre, _rows(tb, D, 0, n)), (pre, _rows(tb, D, 1, n)), (uc, rv), (h_lru, rv),
                         (h_lru, _halo_prev(tb, D, 0, n)), (b_rg, v), (b_ig, v), (lam, v)],
                    outs=[(_sds((s, 2 * D), _MXU), _rows(tb, 2 * D, 0, n)), (_sds((s, D)), rv),
                          (_sds((1, D)), v), (_sds((1, D)), v), (_sds((1, D)), v)],
                    scratch=[pltpu.VMEM((8, D), _F32)])


def _conv_bwd(duc_a, duc_b, proj, conv_w):
    s = duc_a.shape[0]
    tb = 256
    n = s // tb

    def body(da_ref, db_ref, dan_ref, dbn_ref, u_ref, up_ref, w_ref, du_ref, dw_ref, dbias_ref):
        i = pl.program_id(0)
        duc = da_ref[...] + db_ref[...]
        nxt = jnp.where(i < n - 1, dan_ref[...] + dbn_ref[...], 0.0)
        u = u_ref[...]
        halo = jnp.where(i > 0, up_ref[...], 0.0)
        du = duc * w_ref[0:1, :]
        dws = [jnp.sum(duc * u, axis=0, keepdims=True)]
        for j in range(1, 4):
            du = du + _shift_up(duc, nxt, j, tb) * w_ref[j:j + 1, :]
            dws.append(jnp.sum(duc * _shift_down(u, halo, j, tb), axis=0, keepdims=True))
        du_ref[...] = du.astype(du_ref.dtype)
        for j in range(4):
            _accumulate(i, dw_ref.at[j:j + 1, :], dws[j])
        _accumulate(i, dbias_ref, jnp.sum(duc, axis=0, keepdims=True))

    r = _rows(tb, D)
    return _rowwise("conv_bwd", body, grid=n,
                    ins=[(duc_a, r), (duc_b, r), (duc_a, _halo_next(tb, D, n)), (duc_b, _halo_next(tb, D, n)),
                         (proj, _rows(tb, D, CB_U)), (proj, _halo_prev(tb, D, CB_U)), (conv_w, _vec((4, D)))],
                    outs=[(_sds((s, D), _MXU), r), (_sds((4, D)), _vec((4, D))), (_sds((1, D)), _vec((1, D)))])


def _prenorm_bwd(dh, x, dx_out, g_pre, scale):
    s = x.shape[0]
    tb = 512

    def body(dh_ref, x_ref, dxo_ref, g_ref, sc_ref, dx_ref, dsh_ref, dsc_ref, dg_ref):
        i = pl.program_id(0)
        xv = x_ref[...]
        dhv = dh_ref[...]
        rstd = lax.rsqrt(jnp.mean(xv * xv, axis=-1, keepdims=True) + NORM_EPS)
        xn = xv * rstd
        one_sc = 1.0 + sc_ref[...]
        s1 = jnp.sum(dhv * xn, axis=0, keepdims=True)
        _accumulate(i, dsh_ref, jnp.sum(dhv, axis=0, keepdims=True))
        _accumulate(i, dsc_ref, s1 * g_ref[...])
        _accumulate(i, dg_ref, s1 * one_sc)
        dxn = dhv * (g_ref[...] * one_sc)
        dx_ref[...] = dxo_ref[...] + rstd * (dxn - xn * jnp.mean(dxn * xn, axis=-1, keepdims=True))

    v = _vec((1, D))
    r = _rows(tb, D)
    return _rowwise("prenorm_bwd", body, grid=s // tb,
                    ins=[(dh, r), (x, r), (dx_out, r), (g_pre, v), (scale, v)],
                    outs=[(_sds((s, D)), r), (_sds((1, D)), v), (_sds((1, D)), v), (_sds((1, D)), v)])


def _band_tiles(dil):
    tiles = []
    for rho in range(dil):
        for b in range(16 // dil):
            qs = rho + dil * BAND * b
            tiles.append((qs, QBLK + qs - dil * BAND, b))
    return tiles


def _strided(start, size, dil):
    return pl.ds(start, size, stride=dil) if dil > 1 else pl.ds(start, size)


def _band_mask(i, b):
    qi = lax.broadcasted_iota(jnp.int32, (BAND, 2 * BAND), 0)
    ki = lax.broadcasted_iota(jnp.int32, (BAND, 2 * BAND), 1)
    valid = (ki >= qi) & (ki <= qi + BAND)
    if b == 0:
        valid = valid & ((ki >= BAND) | (i > 0))
    return valid


def _attn_fwd(proj, g):
    s = proj.shape[0]
    dil = DILATIONS[g]
    n = s // QBLK
    scale = HEAD ** -0.5
    tiles = _band_tiles(dil)

    def body(q_ref, kp_ref, kc_ref, vp_ref, vc_ref, o_ref, m_ref, l_ref, kbuf, vbuf):
        i = pl.program_id(1)
        kbuf[0:QBLK, :] = kp_ref[...]
        kbuf[QBLK:2 * QBLK, :] = kc_ref[...]
        vbuf[0:QBLK, :] = vp_ref[...]
        vbuf[QBLK:2 * QBLK, :] = vc_ref[...]
        for qs, ks, b in tiles:
            q = q_ref[_strided(qs, BAND, dil), :].astype(_MXU)
            kk = kbuf[_strided(ks, 2 * BAND, dil), :].astype(_MXU)
            vv = vbuf[_strided(ks, 2 * BAND, dil), :].astype(_MXU)
            sc = lax.dot_general(q, kk, _NT, preferred_element_type=_F32) * scale
            sc = jnp.where(_band_mask(i, b), sc, NEG_INF)
            m = jnp.max(sc, axis=-1, keepdims=True)
            p = jnp.exp(sc - m)
            l = jnp.sum(p, axis=-1, keepdims=True)
            o = jnp.dot(p.astype(_MXU), vv, preferred_element_type=_F32) / l
            o_ref[_strided(qs, BAND, dil), :] = o
            m_ref[_strided(qs, BAND, dil), :] = jnp.broadcast_to(m, (BAND, HEAD))
            l_ref[_strided(qs, BAND, dil), :] = jnp.broadcast_to(l, (BAND, HEAD))

    blk = (QBLK, HEAD)
    cq, ck, cv = g * HEADS, 12 + g * HEADS, 24 + g * HEADS
    out_spec = pl.BlockSpec(blk, lambda j, i: (i, j))
    return pl.pallas_call(
        body, name="attn_fwd_d%d" % dil, grid=(HEADS, n),
        in_specs=[pl.BlockSpec(blk, lambda j, i: (i, cq + j)),
                  pl.BlockSpec(blk, lambda j, i: (jnp.maximum(i - 1, 0), ck + j)),
                  pl.BlockSpec(blk, lambda j, i: (i, ck + j)),
                  pl.BlockSpec(blk, lambda j, i: (jnp.maximum(i - 1, 0), cv + j)),
                  pl.BlockSpec(blk, lambda j, i: (i, cv + j))],
        out_specs=[out_spec] * 3, out_shape=[_sds((s, ATT_W))] * 3,
        scratch_shapes=[pltpu.VMEM((2 * QBLK, HEAD), _F32)] * 2,
        compiler_params=_params(2))(proj, proj, proj, proj, proj)


def _attn_bwd(proj, d_o, o, lse, g):
    s = proj.shape[0]
    dil = DILATIONS[g]
    n = s // QBLK
    scale = HEAD ** -0.5
    tiles = _band_tiles(dil)

    def body(q_ref, kp_ref, kc_ref, vp_ref, vc_ref, do_ref, o_ref, lse_ref, dq_ref, dk_ref, dv_ref,
             kbuf, vbuf, dkbuf, dvbuf, dqbuf):
        i = pl.program_id(1)

        @pl.when(i == 0)
        def _():
            dkbuf[0:QBLK, :] = jnp.zeros((QBLK, HEAD), _F32)
            dvbuf[0:QBLK, :] = jnp.zeros((QBLK, HEAD), _F32)

        @pl.when(i < n)
        def _():
            kbuf[0:QBLK, :] = kp_ref[...]
            kbuf[QBLK:2 * QBLK, :] = kc_ref[...]
            vbuf[0:QBLK, :] = vp_ref[...]
            vbuf[QBLK:2 * QBLK, :] = vc_ref[...]
            dkbuf[QBLK:2 * QBLK, :] = jnp.zeros((QBLK, HEAD), _F32)
            dvbuf[QBLK:2 * QBLK, :] = jnp.zeros((QBLK, HEAD), _F32)
            for qs, ks, b in tiles:
                qsl = _strided(qs, BAND, dil)
                ksl = _strided(ks, 2 * BAND, dil)
                q = q_ref[qsl, :].astype(_MXU)
                kk = kbuf[ksl, :].astype(_MXU)
                vv = vbuf[ksl, :].astype(_MXU)
                dov = do_ref[qsl, :]
                dd = jnp.sum(dov * o_ref[qsl, :], axis=-1, keepdims=True)
                lse_t = lse_ref[qsl, :][:, 0:1]
                sc = lax.dot_general(q, kk, _NT, preferred_element_type=_F32) * scale
                p = jnp.where(_band_mask(i, b), jnp.exp(sc - lse_t), 0.0)
                dob = dov.astype(_MXU)
                dp = lax.dot_general(dob, vv, _NT, preferred_element_type=_F32)
                ds = (p * (dp - dd) * scale).astype(_MXU)
                dqbuf[qsl, :] = jnp.dot(ds, kk, preferred_element_type=_F32)
                dkbuf[ksl, :] += lax.dot_general(ds, q, _TN, preferred_element_type=_F32)
                dvbuf[ksl, :] += lax.dot_general(p.astype(_MXU), dob, _TN, preferred_element_type=_F32)
            dq_ref[...] = dqbuf[...].astype(dq_ref.dtype)

        dk_ref[...] = dkbuf[0:QBLK, :].astype(dk_ref.dtype)
        dv_ref[...] = dvbuf[0:QBLK, :].astype(dv_ref.dtype)
        dkbuf[0:QBLK, :] = dkbuf[QBLK:2 * QBLK, :]
        dvbuf[0:QBLK, :] = dvbuf[QBLK:2 * QBLK, :]

    blk = (QBLK, HEAD)
    cq, ck, cv = g * HEADS, 12 + g * HEADS, 24 + g * HEADS

    def cur(i):
        return jnp.minimum(i, n - 1)

    def prev(i):
        return jnp.maximum(jnp.minimum(i, n - 1) - 1, 0)

    own = pl.BlockSpec(blk, lambda j, i: (cur(i), j))
    late = pl.BlockSpec(blk, lambda j, i: (jnp.maximum(i - 1, 0), j))
    return pl.pallas_call(
        body, name="attn_bwd_d%d" % dil, grid=(HEADS, n + 1),
        in_specs=[pl.BlockSpec(blk, lambda j, i: (cur(i), cq + j)),
                  pl.BlockSpec(blk, lambda j, i: (prev(i), ck + j)),
                  pl.BlockSpec(blk, lambda j, i: (cur(i), ck + j)),
                  pl.BlockSpec(blk, lambda j, i: (prev(i), cv + j)),
                  pl.BlockSpec(blk, lambda j, i: (cur(i), cv + j)),
                  own, own, own],
        out_specs=[own, late, late], out_shape=[_sds((s, ATT_W), _MXU)] * 3,
        scratch_shapes=[pltpu.VMEM((2 * QBLK, HEAD), _F32)] * 4 + [pltpu.VMEM((QBLK, HEAD), _F32)],
        compiler_params=_params(2))(proj, proj, proj, proj, proj, d_o, o, lse)


def _block_diag(w):
    eye = jnp.eye(16, dtype=w.dtype)
    return jnp.einsum("hij,hg->higj", w, eye).reshape(D, D)


def _diag_blocks(gd):
    g4 = gd.reshape(16, 64, 16, 64)
    return jnp.stack([g4[h, :, h, :] for h in range(16)], axis=0)


def _layer_fwd(l, x, p, gw):
    s = x.shape[0]
    nm = s // 1024
    h = _prenorm_fwd(x, p["g_pre"], p["shift"], p["scale"])
    proj = _mm("proj", h, gw["w_in"], _sds((s, IN_W)), grid=(nm, 12, 1),
               a_spec=pl.BlockSpec((1024, D), lambda m, n, k: (m, 0)),
               b_spec=pl.BlockSpec((None, None, D, 768), lambda m, n, k: (n // 3, l, 0, n % 3)),
               o_spec=pl.BlockSpec((1024, 768), lambda m, n, k: (m, n)), dims=_NN, acc_shape=(1024, 768))
    og, mg, lg = [], [], []
    for g in range(3):
        o_g, m_g, l_g = _attn_fwd(proj, g)
        og.append(o_g)
        mg.append(m_g)
        lg.append(l_g)
    uc = _conv_fwd(proj, p["conv_w"], p["conv_b"])
    pre = _mm("lru_gates", uc, p["w_gates"], _sds((s, 2 * D)), grid=(nm, 2, 1),
              a_spec=pl.BlockSpec((1024, D), lambda m, n, k: (m, 0)),
              b_spec=pl.BlockSpec((D, D), lambda m, n, k: (0, n)),
              o_spec=pl.BlockSpec((1024, D), lambda m, n, k: (m, n)), dims=_NN, acc_shape=(1024, D))
    h_lru = _scan_fwd(pre, uc, p["b_rg"], p["b_ig"], p["lam"])
    o, lse, a_att, b_act = _gating_fwd(og, mg, lg, proj, h_lru)
    y_a = _mm("proj_a", a_att, gw["w_pa"], _sds((s, D)), grid=(nm, 4, 1),
              a_spec=pl.BlockSpec((1024, ATT_W), lambda m, n, k: (m, 0)),
              b_spec=pl.BlockSpec((None, None, ATT_W, 256), lambda m, n, k: (n, l, 0, 0)),
              o_spec=pl.BlockSpec((1024, 256), lambda m, n, k: (m, n)), dims=_NN, acc_shape=(1024, 256))
    rows_w = pl.BlockSpec((None, None, 256, D), lambda m, n, k: (k, l, 0, 0))
    a_k = pl.BlockSpec((1024, 256), lambda m, n, k: (m, k))
    o_full = pl.BlockSpec((1024, D), lambda m, n, k: (m, 0))
    y_b = _mm("proj_b", b_act, gw["w_pb"], _sds((s, D)), grid=(nm, 1, 4), a_spec=a_k, b_spec=rows_w,
              o_spec=o_full, dims=_NN, acc_shape=(1024, D))
    z = _merge_fwd(y_a, y_b, proj)
    out = _mm("proj_o", z, gw["w_o"], _sds((s, D)), grid=(nm, 1, 4), a_spec=a_k, b_spec=rows_w,
              o_spec=o_full, dims=_NN, acc_shape=(1024, D))
    x_new = _post_fwd(x, out, p["gate"], p["g_post"])
    saved = dict(x=x, h=h, proj=proj, o=o, lse=lse, uc=uc, pre=pre, h_lru=h_lru, a_att=a_att, b_act=b_act,
                 y_a=y_a, y_b=y_b, z=z, out=out)
    return x_new, saved


def _layer_bwd(l, dx, p, gw, sv, big):
    s = dx.shape[0]
    nm = s // 1024
    nt = s // 1024
    proj = sv["proj"]
    d_out, d_gate, d_gpost = _post_bwd(dx, sv["out"], p["gate"], p["g_post"])

    def wgrad_rows(name, a, b, into):
        return _mm(name, a, b, _sds((2, N_CHIPS, 256, D)), grid=(4, 1, nt),
                   a_spec=pl.BlockSpec((1024, 256), lambda m, n, k: (k, m)),
                   b_spec=pl.BlockSpec((1024, D), lambda m, n, k: (k, 0)),
                   o_spec=pl.BlockSpec((None, None, 256, D), lambda m, n, k: (l, m, 0, 0)),
                   dims=_TN, acc_shape=(256, D), into=into)

    def dgrad_rows(name, a, w):
        return _mm(name, a, w, _sds((s, D)), grid=(nm, 4, 1),
                   a_spec=pl.BlockSpec((1024, D), lambda m, n, k: (m, 0)),
                   b_spec=pl.BlockSpec((None, None, 256, D), lambda m, n, k: (n, l, 0, 0)),
                   o_spec=pl.BlockSpec((1024, 256), lambda m, n, k: (m, n)), dims=_NT, acc_shape=(1024, 256))

    dz = dgrad_rows("d_z", d_out, gw["w_o"])
    big = dict(big)
    big["w_o"] = wgrad_rows("g_w_o", sv["z"], d_out, big.get("w_o"))
    dy_a, dy_b, d_ma, d_mb = _merge_bwd(dz, sv["y_a"], sv["y_b"], proj)
    d_aa = _mm("d_a_att", dy_a, gw["w_pa"], _sds((s, ATT_W)), grid=(nm, 1, 4),
               a_spec=pl.BlockSpec((1024, 256), lambda m, n, k: (m, k)),
               b_spec=pl.BlockSpec((None, None, ATT_W, 256), lambda m, n, k: (k, l, 0, 0)),
               o_spec=pl.BlockSpec((1024, ATT_W), lambda m, n, k: (m, 0)), dims=_NT, acc_shape=(1024, ATT_W))
    big["w_pa"] = _mm("g_w_pa", sv["a_att"], dy_a, _sds((2, N_CHIPS, ATT_W, 256)), grid=(1, 4, nt),
                      a_spec=pl.BlockSpec((1024, ATT_W), lambda m, n, k: (k, 0)),
                      b_spec=pl.BlockSpec((1024, 256), lambda m, n, k: (k, n)),
                      o_spec=pl.BlockSpec((None, None, ATT_W, 256), lambda m, n, k: (l, n, 0, 0)),
                      dims=_TN, acc_shape=(ATT_W, 256), into=big.get("w_pa"))
    d_ba = dgrad_rows("d_b_act", dy_b, gw["w_pb"])
    big["w_pb"] = wgrad_rows("g_w_pb", sv["b_act"], dy_b, big.get("w_pb"))
    d_o, d_gatt, dh_lru, d_glru = _gating_bwd(d_aa, sv["o"], proj, d_ba, sv["h_lru"])
    d_pre, duc_dir, d_brg, d_big, d_lam = _scan_bwd(dh_lru, sv["pre"], sv["uc"], sv["h_lru"],
                                                   p["b_rg"], p["b_ig"], p["lam"])
    duc_mm = _mm("d_uc", d_pre, p["w_gates"], _sds((s, D)), grid=(nm, 1, 2),
                 a_spec=pl.BlockSpec((1024, D), lambda m, n, k: (m, k)),
                 b_spec=pl.BlockSpec((D, D), lambda m, n, k: (0, k)),
                 o_spec=pl.BlockSpec((1024, D), lambda m, n, k: (m, 0)), dims=_NT, acc_shape=(1024, D))
    g_gates = _mm("g_w_gates", sv["uc"], d_pre, _sds((D, 2 * D)), grid=(1, 2, s // 512),
                  a_spec=pl.BlockSpec((512, D), lambda m, n, k: (k, 0)),
                  b_spec=pl.BlockSpec((512, D), lambda m, n, k: (k, n)),
                  o_spec=pl.BlockSpec((D, D), lambda m, n, k: (0, n)), dims=_TN, acc_shape=(D, D))
    d_u, g_convw, g_convb = _conv_bwd(duc_dir, duc_mm, proj, p["conv_w"])
    dqkv = [_attn_bwd(proj, d_o, sv["o"], sv["lse"], g) for g in range(3)]
    dproj = jnp.concatenate([dqkv[g][t] for t in range(3) for g in range(3)]
                            + [d_gatt, d_u, d_glru, d_ma, d_mb], axis=1)
    dh = _mm("d_h", dproj, gw["w_in"], _sds((s, D)), grid=(nm, 1, 12),
             a_spec=pl.BlockSpec((1024, 768), lambda m, n, k: (m, k)),
             b_spec=pl.BlockSpec((None, None, D, 768), lambda m, n, k: (k // 3, l, 0, k % 3)),
             o_spec=pl.BlockSpec((1024, D), lambda m, n, k: (m, 0)), dims=_NT, acc_shape=(1024, D))
    big["w_in"] = _mm("g_w_in", sv["h"], dproj, _sds((2, N_CHIPS, D, 2304)), grid=(1, 12, s // 512),
                      a_spec=pl.BlockSpec((512, D), lambda m, n, k: (k, 0)),
                      b_spec=pl.BlockSpec((512, 768), lambda m, n, k: (k, n)),
                      o_spec=pl.BlockSpec((None, None, D, 768), lambda m, n, k: (l, n // 3, 0, n % 3)),
                      dims=_TN, acc_shape=(D, 768), into=big.get("w_in"))
    dx_in, d_shift, d_scale, d_gpre = _prenorm_bwd(dh, sv["x"], dx, p["g_pre"], p["scale"])
    small = dict(dmod=jnp.concatenate([d_shift, d_scale, d_gate], axis=1), g_pre=d_gpre, conv_w=g_convw,
                 conv_b=g_convb, w_rg=_diag_blocks(g_gates[:, 0:D]), b_rg=d_brg,
                 w_ig=_diag_blocks(g_gates[:, D:2 * D]), b_ig=d_big, lam=d_lam, g_post=d_gpost)
    return dx_in, small, big


def _local_step(x, target, small_p, gw):
    saved = []
    h = x
    for l in range(2):
        h, sv = _layer_fwd(l, h, small_p[l], gw)
        saved.append(sv)
    dy, sq = _loss_head(h, target)
    loss = 0.5 * jnp.sum(sq) / D
    big = {}
    smalls = [None, None]
    dx = dy
    for l in (1, 0):
        dx, smalls[l], big = _layer_bwd(l, dx, small_p[l], gw, saved[l], big)
    return loss, dx, smalls, big


_SMALL_ROWS = 8 + 16 + 8 + 128 + 128


def _pack_small(smalls):
    dmod = jnp.concatenate([smalls[0]["dmod"].reshape(3, D), smalls[1]["dmod"].reshape(3, D),
                            jnp.zeros((2, D), _F32)], axis=0)
    vecs = jnp.concatenate([smalls[l][k] for k in ("g_pre", "conv_b", "b_rg", "b_ig", "lam", "g_post")
                            for l in range(2)] + [jnp.zeros((4, D), _F32)], axis=0)
    convw = jnp.concatenate([smalls[0]["conv_w"], smalls[1]["conv_w"]], axis=0)
    wrg = jnp.stack([smalls[0]["w_rg"], smalls[1]["w_rg"]]).reshape(128, D)
    wig = jnp.stack([smalls[0]["w_ig"], smalls[1]["w_ig"]]).reshape(128, D)
    return jnp.concatenate([dmod, vecs, convw, wrg, wig], axis=0)


def kernel(x, c, w_mod, b_mod, g_pre, w_in, conv_w, conv_b, w_rg, b_rg, w_ig, b_ig, lru_lambda, w_pa, w_pb, w_o, g_post, loss_target, m_w_mod, m_b_mod, m_g_pre, m_w_in, m_conv_w, m_conv_b, m_w_rg, m_b_rg, m_w_ig, m_b_ig, m_lru_lambda, m_w_pa, m_w_pb, m_w_o, m_g_post, v_w_mod, v_b_mod, v_g_pre, v_w_in, v_conv_w, v_conv_b, v_w_rg, v_b_rg, v_w_ig, v_b_ig, v_lru_lambda, v_w_pa, v_w_pb, v_w_o, v_g_post):
    xi, yi, ci = lax.axis_index("x"), lax.axis_index("y"), lax.axis_index("c")
    chip = 2 * xi + yi
    dev = 4 * xi + 2 * yi + ci
    mcols = w_mod.shape[2]

    pack1 = jnp.concatenate([jnp.broadcast_to(c, (8, D)),
                             jnp.pad(conv_w.reshape(8, 256), ((0, 0), (0, D - 256)))], axis=0)
    g1 = _exchange("gather_cond", [pack1], "xyc", False)[0]
    c_all = g1[:, 0, :]
    conv_w_full = jnp.transpose(g1[0::2, 8:16, 0:256], (1, 0, 2)).reshape(2, 4, D)

    b_cols = lax.dynamic_slice(b_mod, (0, chip * mcols), (2, mcols)).reshape(2, 1, mcols)
    mod_loc = _mod_fwd(c_all, w_mod, b_cols)
    g2 = _exchange("gather_mod", [mod_loc.reshape(16, mcols)], "xyc", False)[0]
    mod_full = jnp.transpose(g2[0::2], (1, 0, 2)).reshape(2, 8, 3 * D)
    mod_me = lax.dynamic_index_in_dim(mod_full, dev, axis=1, keepdims=False)

    wb = [_cast("cast_w_in", w_in.reshape(2 * D, 2304), 256).reshape(2, D, 2304),
          _cast("cast_w_pa", w_pa.reshape(2 * ATT_W, 256), 256).reshape(2, ATT_W, 256),
          _cast("cast_w_pb", w_pb.reshape(512, D), 256).reshape(2, 256, D),
          _cast("cast_w_o", w_o.reshape(512, D), 256).reshape(2, 256, D)]
    gl = _exchange("gather_weights", wb, "xy", False)
    gw = dict(w_in=gl[0], w_pa=gl[1], w_pb=gl[2], w_o=gl[3])

    small_p = []
    for l in range(2):
        gates = jnp.concatenate([_block_diag(w_rg[l]), _block_diag(w_ig[l])], axis=1).astype(_MXU)
        small_p.append(dict(
            shift=mod_me[l:l + 1, 0:D], scale=mod_me[l:l + 1, D:2 * D], gate=mod_me[l:l + 1, 2 * D:3 * D],
            g_pre=g_pre[l:l + 1], conv_w=conv_w_full[l], conv_b=conv_b[l:l + 1], w_gates=gates,
            b_rg=b_rg[l:l + 1], b_ig=b_ig[l:l + 1], lam=lru_lambda[l:l + 1], g_post=g_post[l:l + 1]))

    loss_loc, dx, smalls, big = _local_step(x[0], loss_target[0], small_p, gw)
    loss = lax.psum(loss_loc, ("x", "y", "c"))
    grad_x = dx[None]

    names = ("w_in", "w_pa", "w_pb", "w_o")
    pair = _exchange("reduce_pair", [big[k] for k in names], "c", True)
    t1 = [_sum_lead("sum_pair_" + k, a.reshape(2, 4 * a.shape[2], a.shape[3]), 128).reshape(a.shape[1:])
          for k, a in zip(names, pair)]
    quad = _exchange("reduce_chips", t1, "xy", True)
    t3 = [_sum_lead("sum_chips_" + k, a, 128) for k, a in zip(names, quad)]
    both = _exchange("gather_layers", t3, "c", False)
    g_big = dict(zip(names, both))

    g3 = _exchange("gather_small", [_pack_small(smalls)], "xyc", False)[0]
    tot = _sum_lead("sum_small", g3, 96)
    dmod_all = g3[:, 0:6, :].reshape(8, 2, 3 * D)
    dm_cols = jnp.transpose(lax.dynamic_slice(dmod_all, (0, 0, chip * mcols), (8, 2, mcols)), (1, 0, 2))
    g_w_mod = _mod_bwd(jnp.transpose(c_all), dm_cols)
    vec = tot[8:20].reshape(6, 2, D)
    g_conv_w_full = tot[24:32].reshape(2, 4, D)
    grads = dict(
        w_mod=g_w_mod, b_mod=tot[0:6].reshape(2, 3 * D), g_pre=vec[0], w_in=g_big["w_in"],
        conv_w=lax.dynamic_slice(g_conv_w_full, (0, 0, chip * 256), (2, 4, 256)), conv_b=vec[1],
        w_rg=tot[32:160].reshape(2, 16, 64, 64), b_rg=vec[2], w_ig=tot[160:288].reshape(2, 16, 64, 64),
        b_ig=vec[3], lru_lambda=vec[4], w_pa=g_big["w_pa"], w_pb=g_big["w_pb"], w_o=g_big["w_o"],
        g_post=vec[5])

    weights = dict(w_mod=w_mod, b_mod=b_mod, g_pre=g_pre, w_in=w_in, conv_w=conv_w, conv_b=conv_b, w_rg=w_rg,
                   b_rg=b_rg, w_ig=w_ig, b_ig=b_ig, lru_lambda=lru_lambda, w_pa=w_pa, w_pb=w_pb, w_o=w_o,
                   g_post=g_post)
    ms = dict(w_mod=m_w_mod, b_mod=m_b_mod, g_pre=m_g_pre, w_in=m_w_in, conv_w=m_conv_w, conv_b=m_conv_b,
              w_rg=m_w_rg, b_rg=m_b_rg, w_ig=m_w_ig, b_ig=m_b_ig, lru_lambda=m_lru_lambda, w_pa=m_w_pa,
              w_pb=m_w_pb, w_o=m_w_o, g_post=m_g_post)
    vs = dict(w_mod=v_w_mod, b_mod=v_b_mod, g_pre=v_g_pre, w_in=v_w_in, conv_w=v_conv_w, conv_b=v_conv_b,
              w_rg=v_w_rg, b_rg=v_b_rg, w_ig=v_w_ig, b_ig=v_b_ig, lru_lambda=v_lru_lambda, w_pa=v_w_pa,
              w_pb=v_w_pb, w_o=v_w_o, g_post=v_g_post)
    flat = dict(w_mod=(2 * D, mcols, 256), b_mod=(2, 3 * D, 2), g_pre=(2, D, 2), w_in=(2 * D, 2304, 256),
                conv_w=(8, 256, 8), conv_b=(2, D, 2), w_rg=(128, D, 128), b_rg=(2, D, 2), w_ig=(128, D, 128),
                b_ig=(2, D, 2), lru_lambda=(2, D, 2), w_pa=(2 * ATT_W, 256, 256), w_pb=(512, D, 256),
                w_o=(512, D, 256), g_post=(2, D, 2))
    order = ("w_mod", "b_mod", "g_pre", "w_in", "conv_w", "conv_b", "w_rg", "b_rg", "w_ig", "b_ig",
             "lru_lambda", "w_pa", "w_pb", "w_o", "g_post")
    deltas, new_m, new_v = [], [], []
    for k in order:
        rows, cols, tb = flat[k]
        shp = weights[k].shape
        d, nm_, nv_ = _adamw("adamw_" + k, weights[k].reshape(rows, cols), grads[k].reshape(rows, cols),
                             ms[k].reshape(rows, cols), vs[k].reshape(rows, cols), tb)
        deltas.append(d.reshape(shp))
        new_m.append(nm_.reshape(shp))
        new_v.append(nv_.reshape(shp))
    return (loss, grad_x, *[grads[k].reshape(weights[k].shape) for k in order], *deltas, *new_m, *new_v)
```

```python
import functools

import jax
import jax.numpy as jnp
from jax import lax
from jax.experimental import pallas as pl
from jax.experimental.pallas import tpu as pltpu

_F32 = jnp.float32
_MXU = jnp.bfloat16
_VMEM_LIMIT = 56 * 1024 * 1024
_MESH = pl.DeviceIdType.MESH

D = 1024
HEAD = 128
HEADS = 4
ATT_W = 512
QKV_W = 1536
IN_W = 9216
DILATIONS = (1, 4, 16)
BAND = 128
QBLK = BAND * 16
NORM_EPS = 1e-6
NEG_INF = -1e30
LRU_C = 8.0
N_CHIPS = 4
CB_GATT = 4608 // 512
CB_U, CB_GLRU, CB_MA, CB_MB = 5, 6, 7, 8

ADAM_LR, ADAM_B1, ADAM_B2, ADAM_EPS, ADAM_WD, ADAM_STEP = 0.001, 0.9, 0.999, 1e-08, 0.01, 10


def _params(ngrid):
    return pltpu.CompilerParams(dimension_semantics=("arbitrary",) * ngrid, vmem_limit_bytes=_VMEM_LIMIT)


def _sigmoid(v):
    return 1.0 / (1.0 + jnp.exp(-v))


_GROUPS = {
    "c": [(0, 0, 1)],
    "xy": [(1, 0, 0), (0, 1, 0), (1, 1, 0)],
    "xyc": [(0, 0, 1), (0, 1, 0), (0, 1, 1), (1, 0, 0), (1, 0, 1), (1, 1, 0), (1, 1, 1)],
}


def _rank(group, px, py, pc):
    if group == "c":
        return pc
    if group == "xy":
        return 2 * px + py
    return 4 * px + 2 * py + pc


def _flip(rel, x, y, c):
    dx, dy, dc = rel
    return (1 - x if dx else x, 1 - y if dy else y, 1 - c if dc else c)


def _pieces(ref, nchunk):
    step = ref.shape[0] // nchunk
    return [ref.at[pl.ds(q * step, step)] for q in range(nchunk)]


def _exchange(name, srcs, group, scatter, *, local=True, nchunks=None):
    rels = _GROUPS[group]
    gsize = len(rels) + 1
    n = len(srcs)
    nchunks = nchunks or [1] * n
    blks = [s.shape[1:] if scatter else s.shape for s in srcs]
    slotted = local or gsize > 2
    base = [sum(nchunks[:a]) for a in range(n)]
    tot = sum(nchunks)

    def body(*refs):
        src_refs, out_refs = refs[:n], refs[n:2 * n]
        send_sems, recv_sems, loc_sems = refs[2 * n:]
        x, y, c = lax.axis_index("x"), lax.axis_index("y"), lax.axis_index("c")
        me = _rank(group, x, y, c)
        copies = []
        for a in range(n):
            def part(r, a=a):
                return src_refs[a].at[r] if scatter else src_refs[a]
            dst = out_refs[a].at[me] if slotted else out_refs[a]
            if local:
                for q, (s_, d_) in enumerate(zip(_pieces(part(me), nchunks[a]), _pieces(dst, nchunks[a]))):
                    loc = pltpu.make_async_copy(s_, d_, loc_sems.at[base[a] + q])
                    loc.start()
                    copies.append(loc)
            for k, rel in enumerate(rels):
                peer = _flip(rel, x, y, c)
                for q, (s_, d_) in enumerate(zip(_pieces(part(_rank(group, *peer)), nchunks[a]),
                                                 _pieces(dst, nchunks[a]))):
                    cp = pltpu.make_async_remote_copy(
                        src_ref=s_, dst_ref=d_, send_sem=send_sems.at[(base[a] + q) * len(rels) + k],
                        recv_sem=recv_sems.at[(base[a] + q) * len(rels) + k],
                        device_id=peer, device_id_type=_MESH)
                    cp.start()
                    copies.append(cp)
        for cp in copies:
            cp.wait()

    any_spec = pl.BlockSpec(memory_space=pl.ANY)
    lead = (gsize,) if slotted else ()
    return pl.pallas_call(
        body, name=name,
        out_shape=[jax.ShapeDtypeStruct(lead + tuple(b), s.dtype) for b, s in zip(blks, srcs)],
        in_specs=[any_spec] * n, out_specs=[any_spec] * n,
        scratch_shapes=[pltpu.SemaphoreType.DMA((tot * len(rels),)), pltpu.SemaphoreType.DMA((tot * len(rels),)),
                        pltpu.SemaphoreType.DMA((tot,))],
    )(*srcs)


def _pair_fill(name, arrs, nchunks):
    n = len(arrs)
    base = [sum(nchunks[:a]) for a in range(n)]
    tot = sum(nchunks)

    def body(*refs):
        out_refs = refs[n:2 * n]
        send_sems, recv_sems = refs[2 * n:]
        x, y, c = lax.axis_index("x"), lax.axis_index("y"), lax.axis_index("c")
        copies = []
        for a in range(n):
            for q, blk in enumerate(_pieces(out_refs[a].at[c], nchunks[a])):
                cp = pltpu.make_async_remote_copy(
                    src_ref=blk, dst_ref=blk, send_sem=send_sems.at[base[a] + q], recv_sem=recv_sems.at[base[a] + q],
                    device_id=(x, y, 1 - c), device_id_type=_MESH)
                cp.start()
                copies.append(cp)
        for cp in copies:
            cp.wait()

    any_spec = pl.BlockSpec(memory_space=pl.ANY)
    return pl.pallas_call(
        body, name=name, out_shape=[jax.ShapeDtypeStruct(a.shape, a.dtype) for a in arrs],
        in_specs=[any_spec] * n, out_specs=[any_spec] * n, input_output_aliases={a: a for a in range(n)},
        scratch_shapes=[pltpu.SemaphoreType.DMA((tot,)), pltpu.SemaphoreType.DMA((tot,))],
    )(*arrs)


def _gather_weights(wb, nchunks):
    n = len(wb)
    rels = _GROUPS["xy"]
    base = [sum(nchunks[:a]) for a in range(n)]
    tot = sum(nchunks)

    def body(*refs):
        src_refs, out_refs = refs[:n], refs[n:2 * n]
        ici_send, ici_recv, d2d_send, d2d_recv, loc_sems = refs[2 * n:]
        x, y, c = lax.axis_index("x"), lax.axis_index("y"), lax.axis_index("c")
        me = 2 * x + y
        waits = []
        for a in range(n):
            for l in range(2):
                for q, (s_, d_) in enumerate(zip(_pieces(src_refs[a].at[l], nchunks[a]),
                                                 _pieces(out_refs[a].at[me, l], nchunks[a]))):
                    loc = pltpu.make_async_copy(s_, d_, loc_sems.at[(base[a] + q) * 2 + l])
                    loc.start()
                    waits.append(loc)
        first = []
        for a in range(n):
            for k, rel in enumerate(rels):
                px, py, _ = _flip(rel, x, y, c)
                for q, (s_, d_) in enumerate(zip(_pieces(src_refs[a].at[c], nchunks[a]),
                                                 _pieces(out_refs[a].at[me, c], nchunks[a]))):
                    sem = (base[a] + q) * 3 + k
                    cp = pltpu.make_async_remote_copy(src_ref=s_, dst_ref=d_, send_sem=ici_send.at[sem],
                                                      recv_sem=ici_recv.at[sem], device_id=(px, py, c),
                                                      device_id_type=_MESH)
                    cp.start()
                    first.append(cp)
        second = []
        for a in range(n):
            for k, rel in enumerate(rels):
                px, py, _ = _flip(rel, x, y, c)
                for q, blk in enumerate(_pieces(out_refs[a].at[2 * px + py, c], nchunks[a])):
                    sem = (base[a] + q) * 3 + k
                    landed = pltpu.make_async_remote_copy(src_ref=blk, dst_ref=blk, send_sem=ici_send.at[sem],
                                                          recv_sem=ici_recv.at[sem], device_id=(px, py, c),
                                                          device_id_type=_MESH)
                    landed.wait_recv()
                    cp = pltpu.make_async_remote_copy(src_ref=blk, dst_ref=blk, send_sem=d2d_send.at[sem],
                                                      recv_sem=d2d_recv.at[sem], device_id=(x, y, 1 - c),
                                                      device_id_type=_MESH)
                    cp.start()
                    second.append(cp)
        for cp in first:
            cp.wait_send()
        for cp in second:
            cp.wait_send()
        for a in range(n):
            for k, rel in enumerate(rels):
                px, py, _ = _flip(rel, x, y, c)
                for q, blk in enumerate(_pieces(out_refs[a].at[2 * px + py, 1 - c], nchunks[a])):
                    sem = (base[a] + q) * 3 + k
                    pltpu.make_async_remote_copy(src_ref=blk, dst_ref=blk, send_sem=d2d_send.at[sem],
                                                 recv_sem=d2d_recv.at[sem], device_id=(x, y, 1 - c),
                                                 device_id_type=_MESH).wait_recv()
        for cp in waits:
            cp.wait()

    any_spec = pl.BlockSpec(memory_space=pl.ANY)
    return pl.pallas_call(
        body, name="gather_weights",
        out_shape=[jax.ShapeDtypeStruct((N_CHIPS,) + a.shape, a.dtype) for a in wb],
        in_specs=[any_spec] * n, out_specs=[any_spec] * n,
        scratch_shapes=[pltpu.SemaphoreType.DMA((tot * 3,))] * 4 + [pltpu.SemaphoreType.DMA((tot * 2,))],
    )(*wb)


def _mm(name, a, b, out_sds, *, grid, a_spec, b_spec, o_spec, dims, acc_shape, into=None):
    nk = grid[2]

    def body(*refs):
        a_ref, b_ref = refs[0], refs[1]
        o_ref, acc = refs[-2], refs[-1]
        k = pl.program_id(2)
        part = lax.dot_general(a_ref[...].astype(_MXU), b_ref[...].astype(_MXU), dims,
                               preferred_element_type=_F32)
        if nk == 1:
            o_ref[...] = part.astype(o_ref.dtype)
            return

        @pl.when(k == 0)
        def _():
            acc[...] = part

        @pl.when(k > 0)
        def _():
            acc[...] += part

        @pl.when(k == nk - 1)
        def _():
            o_ref[...] = acc[...].astype(o_ref.dtype)

    if nk == 1:
        acc_shape = (8, 128)
    in_specs = [a_spec, b_spec]
    args = [a, b]
    aliases = {}
    if into is not None:
        in_specs.append(pl.BlockSpec(memory_space=pl.ANY))
        args.append(into)
        aliases = {2: 0}
    return pl.pallas_call(
        body, name=name, grid=grid, in_specs=in_specs, out_specs=o_spec, out_shape=out_sds,
        scratch_shapes=[pltpu.VMEM(acc_shape, _F32)], input_output_aliases=aliases,
        compiler_params=_params(3))(*args)


_NN = (((1,), (0,)), ((), ()))
_NT = (((1,), (1,)), ((), ()))
_TN = (((0,), (0,)), ((), ()))


def _rowwise(name, body, *, grid, ins, outs, scratch=()):
    return pl.pallas_call(
        body, name=name, grid=(grid,), in_specs=[s for _, s in ins], out_specs=[s for _, s in outs],
        out_shape=[o for o, _ in outs], scratch_shapes=list(scratch),
        compiler_params=_params(1))(*[a for a, _ in ins])


def _rows(tb, w, cb=0, n=None):
    if n is None:
        return pl.BlockSpec((tb, w), lambda i: (i, cb))
    return pl.BlockSpec((tb, w), lambda i: (n - 1 - i, cb))


def _vec(shape):
    return pl.BlockSpec(shape, lambda i: (0,) * len(shape))


def _halo_prev(tb, w, cb=0, n=None):
    if n is None:
        return pl.BlockSpec((8, w), lambda i: (jnp.maximum(i * (tb // 8) - 1, 0), cb))
    return pl.BlockSpec((8, w), lambda i: (jnp.maximum((n - 1 - i) * (tb // 8) - 1, 0), cb))


def _halo_next(tb, w, n, cb=0):
    return pl.BlockSpec((8, w), lambda i: (jnp.minimum((i + 1) * (tb // 8), n * (tb // 8) - 1), cb))


def _sds(shape, dtype=_F32):
    return jax.ShapeDtypeStruct(shape, dtype)


def _cast(name, a, tb):
    rows, cols = a.shape

    def body(a_ref, o_ref):
        o_ref[...] = a_ref[...].astype(o_ref.dtype)

    return _rowwise(name, body, grid=rows // tb, ins=[(a, _rows(tb, cols))],
                    outs=[(_sds((rows, cols), _MXU), _rows(tb, cols))])[0]


def _sum_lead(name, a, tb):
    g, rows, cols = a.shape

    def body(a_ref, o_ref):
        acc = a_ref[0]
        for k in range(1, g):
            acc = acc + a_ref[k]
        o_ref[...] = acc

    return _rowwise(name, body, grid=rows // tb,
                    ins=[(a, pl.BlockSpec((g, tb, cols), lambda i: (0, i, 0)))],
                    outs=[(_sds((rows, cols)), _rows(tb, cols))])[0]


def _sum_pair(name, mine, theirs, core, tb):
    _, nj, rows, cols = mine.shape

    def body(s_ref, a_ref, b_ref, o_ref, ob_ref):
        t = a_ref[...] + b_ref[...]
        o_ref[...] = t
        ob_ref[...] = t.astype(ob_ref.dtype)

    blk = pl.BlockSpec((None, tb, cols), lambda j, i, s: (j, i, 0))
    grid_spec = pltpu.PrefetchScalarGridSpec(
        num_scalar_prefetch=1, grid=(nj, rows // tb),
        in_specs=[pl.BlockSpec((None, None, tb, cols), lambda j, i, s: (s[0], j, i, 0)), blk],
        out_specs=[blk, blk])
    return pl.pallas_call(body, name=name, grid_spec=grid_spec,
                          out_shape=[_sds((nj, rows, cols)), _sds((nj, rows, cols), _MXU)],
                          compiler_params=_params(2))(core, mine, theirs)


def _sum_chips(name, mine, theirs, where, tb):
    _, rows, cols = mine.shape

    def body(s_ref, a_ref, b1_ref, b2_ref, b3_ref, o_ref):
        o_ref[...] = ((a_ref[...] + b1_ref[...].astype(_F32)) + b2_ref[...].astype(_F32)) + b3_ref[...].astype(_F32)

    def slot(k):
        return pl.BlockSpec((None, tb, cols), lambda i, s: (jnp.bitwise_xor(s[0], k), i, 0))

    grid_spec = pltpu.PrefetchScalarGridSpec(
        num_scalar_prefetch=1, grid=(rows // tb,),
        in_specs=[slot(0), slot(1), slot(2), slot(3)],
        out_specs=pl.BlockSpec((None, tb, cols), lambda i, s: (s[1], i, 0)))
    return pl.pallas_call(body, name=name, grid_spec=grid_spec, out_shape=_sds((2, rows, cols)),
                          compiler_params=_params(1))(where, mine, theirs, theirs, theirs)


def _adamw(name, w, g, m, v, tb):
    rows, cols = w.shape
    c1 = 1.0 - ADAM_B1 ** ADAM_STEP
    c2 = 1.0 - ADAM_B2 ** ADAM_STEP

    def body(w_ref, g_ref, m_ref, v_ref, d_ref, nm_ref, nv_ref):
        gv = g_ref[...]
        nm = ADAM_B1 * m_ref[...] + (1.0 - ADAM_B1) * gv
        nv = ADAM_B2 * v_ref[...] + (1.0 - ADAM_B2) * (gv * gv)
        d_ref[...] = -ADAM_LR * ((nm / c1) / (jnp.sqrt(nv / c2) + ADAM_EPS) + ADAM_WD * w_ref[...])
        nm_ref[...] = nm
        nv_ref[...] = nv

    spec = _rows(tb, cols)
    return _rowwise(name, body, grid=rows // tb, ins=[(w, spec), (g, spec), (m, spec), (v, spec)],
                    outs=[(_sds((rows, cols)), spec)] * 3)


def _mod_fwd(c_all, w_mod, b_cols):
    cols = w_mod.shape[2]

    def body(c_ref, w_ref, b_ref, o_ref):
        cv = c_ref[...]
        sc = (cv * _sigmoid(cv)).astype(_MXU)
        o_ref[...] = jnp.dot(sc, w_ref[...].astype(_MXU), preferred_element_type=_F32) + b_ref[...]

    return pl.pallas_call(
        body, name="mod_fwd", grid=(2,),
        in_specs=[pl.BlockSpec((8, D), lambda l: (0, 0)), pl.BlockSpec((None, D, cols), lambda l: (l, 0, 0)),
                  pl.BlockSpec((None, 1, cols), lambda l: (l, 0, 0))],
        out_specs=pl.BlockSpec((None, 8, cols), lambda l: (l, 0, 0)),
        out_shape=_sds((2, 8, cols)), compiler_params=_params(1))(c_all, w_mod, b_cols)


def _mod_bwd(c_all_t, dm):
    cols = dm.shape[2]

    def body(c_ref, d_ref, o_ref):
        cv = c_ref[...]
        sc = (cv * _sigmoid(cv)).astype(_MXU)
        o_ref[...] = jnp.dot(sc, d_ref[...].astype(_MXU), preferred_element_type=_F32)

    return pl.pallas_call(
        body, name="mod_bwd", grid=(2,),
        in_specs=[pl.BlockSpec((D, 8), lambda l: (0, 0)), pl.BlockSpec((None, 8, cols), lambda l: (l, 0, 0))],
        out_specs=pl.BlockSpec((None, D, cols), lambda l: (l, 0, 0)),
        out_shape=_sds((2, D, cols)), compiler_params=_params(1))(c_all_t, dm)


def _prenorm_fwd(x, g_pre, shift, scale):
    s = x.shape[0]
    tb = 512

    def body(x_ref, g_ref, sh_ref, sc_ref, h_ref, ht_ref):
        xv = x_ref[...]
        rstd = lax.rsqrt(jnp.mean(xv * xv, axis=-1, keepdims=True) + NORM_EPS)
        hv = (xv * rstd) * g_ref[...] * (1.0 + sc_ref[...]) + sh_ref[...]
        h_ref[...] = hv.astype(h_ref.dtype)
        ht_ref[...] = hv.T.astype(ht_ref.dtype)

    v = _vec((1, D))
    return _rowwise("prenorm_fwd", body, grid=s // tb,
                    ins=[(x, _rows(tb, D)), (g_pre, v), (shift, v), (scale, v)],
                    outs=[(_sds((s, D), _MXU), _rows(tb, D)),
                          (_sds((D, s), _MXU), pl.BlockSpec((D, tb), lambda i: (0, i)))])


def _shift_down(cur, halo, j, tb):
    ext = jnp.concatenate([halo, cur], axis=0)
    return pltpu.roll(ext, j, 0)[8:8 + tb]


def _shift_up(cur, halo, j, tb):
    ext = jnp.concatenate([cur, halo], axis=0)
    return pltpu.roll(ext, tb + 8 - j, 0)[0:tb]


def _conv_fwd(proj, conv_w, conv_b):
    s = proj.shape[0]
    tb = 256

    def body(u_ref, hp_ref, w_ref, b_ref, o_ref):
        i = pl.program_id(0)
        u = u_ref[...]
        halo = jnp.where(i > 0, hp_ref[...], 0.0)
        acc = b_ref[...] + u * w_ref[0:1, :]
        for j in range(1, 4):
            acc = acc + _shift_down(u, halo, j, tb) * w_ref[j:j + 1, :]
        o_ref[...] = acc

    return _rowwise("conv_fwd", body, grid=s // tb,
                    ins=[(proj, _rows(tb, D, CB_U)), (proj, _halo_prev(tb, D, CB_U)),
                         (conv_w, _vec((4, D))), (conv_b, _vec((1, D)))],
                    outs=[(_sds((s, D)), _rows(tb, D))])[0]


def _lru_gates(pre_r, pre_i, uc, b_rg, b_ig, lam):
    r = _sigmoid(pre_r + b_rg)
    ig = _sigmoid(pre_i + b_ig)
    nl = -lam
    sp = jnp.maximum(nl, 0.0) + jnp.log(1.0 + jnp.exp(-jnp.abs(nl)))
    la = -LRU_C * r * sp
    a = jnp.exp(la)
    y2 = 2.0 * la
    one_m_a2 = jnp.where(jnp.abs(y2) < 1e-2, -(y2 + 0.5 * y2 * y2 + (1.0 / 6.0) * y2 * y2 * y2),
                         1.0 - jnp.exp(y2))
    sq = jnp.sqrt(one_m_a2)
    return r, ig, sp, a, sq


def _scan_fwd(pre, uc, b_rg, b_ig, lam):
    s = uc.shape[0]
    tb = 256

    def body(pr_ref, pi_ref, uc_ref, brg_ref, big_ref, lam_ref, h_ref, carry):
        i = pl.program_id(0)

        @pl.when(i == 0)
        def _():
            carry[...] = jnp.zeros_like(carry)

        ucv = uc_ref[...]
        _, ig, _, a, sq = _lru_gates(pr_ref[...], pi_ref[...], ucv, brg_ref[...], big_ref[...], lam_ref[...])
        av = a
        bv = sq * (ig * ucv)
        row = lax.broadcasted_iota(jnp.int32, (tb, 1), 0)
        sh = 1
        while sh < tb:
            m = row >= sh
            b_sh = pltpu.roll(bv, sh, 0)
            a_sh = pltpu.roll(av, sh, 0)
            bv = jnp.where(m, av * b_sh + bv, bv)
            av = jnp.where(m, av * a_sh, av)
            sh *= 2
        hv = bv + av * carry[7:8, :]
        h_ref[...] = hv
        carry[...] = hv[tb - 8:tb]

    v = _vec((1, D))
    return _rowwise("scan_fwd", body, grid=s // tb,
                    ins=[(pre, _rows(tb, D, 0)), (pre, _rows(tb, D, 1)), (uc, _rows(tb, D)),
                         (b_rg, v), (b_ig, v), (lam, v)],
                    outs=[(_sds((s, D)), _rows(tb, D))],
                    scratch=[pltpu.VMEM((8, D), _F32)])[0]


def _gating_fwd(og, mg, lg, proj, h_lru):
    s = h_lru.shape[0]
    tb = 256

    def body(o0, o1, o2, m0, m1, m2, l0, l1, l2, ga_ref, h_ref, gl_ref, o_ref, lse_ref, aa_ref, ba_ref):
        ms = [m0[...], m1[...], m2[...]]
        mx = jnp.maximum(jnp.maximum(ms[0], ms[1]), ms[2])
        ws = [l[...] * jnp.exp(m - mx) for l, m in zip((l0, l1, l2), ms)]
        den = ws[0] + ws[1] + ws[2]
        o = (ws[0] * o0[...] + ws[1] * o1[...] + ws[2] * o2[...]) / den
        o_ref[...] = o
        lse_ref[...] = mx + jnp.log(den)
        ga = ga_ref[...]
        aa_ref[...] = (o * (ga * _sigmoid(ga))).astype(aa_ref.dtype)
        gl = gl_ref[...]
        ba_ref[...] = (h_ref[...] * (gl * _sigmoid(gl))).astype(ba_ref.dtype)

    r5 = _rows(tb, ATT_W)
    return _rowwise("gating_fwd", body, grid=s // tb,
                    ins=[(a, r5) for a in og] + [(a, r5) for a in mg] + [(a, r5) for a in lg]
                    + [(proj, _rows(tb, ATT_W, CB_GATT)), (h_lru, _rows(tb, D)), (proj, _rows(tb, D, CB_GLRU))],
                    outs=[(_sds((s, ATT_W)), r5), (_sds((s, ATT_W)), r5), (_sds((s, ATT_W), _MXU), r5),
                          (_sds((s, D), _MXU), _rows(tb, D))])


def _merge_fwd(y_a, y_b, proj):
    s = y_a.shape[0]
    tb = 512

    def body(ya_ref, yb_ref, ma_ref, mb_ref, z_ref):
        z_ref[...] = (_sigmoid(ma_ref[...]) * ya_ref[...] + _sigmoid(mb_ref[...]) * yb_ref[...]).astype(z_ref.dtype)

    return _rowwise("merge_fwd", body, grid=s // tb,
                    ins=[(y_a, _rows(tb, D)), (y_b, _rows(tb, D)), (proj, _rows(tb, D, CB_MA)),
                         (proj, _rows(tb, D, CB_MB))],
                    outs=[(_sds((s, D), _MXU), _rows(tb, D))])[0]


def _post_fwd(x, out, gate, g_post):
    s = x.shape[0]
    tb = 512

    def body(x_ref, o_ref, gt_ref, gp_ref, y_ref):
        ov = o_ref[...]
        rstd = lax.rsqrt(jnp.mean(ov * ov, axis=-1, keepdims=True) + NORM_EPS)
        y_ref[...] = x_ref[...] + gt_ref[...] * ((ov * rstd) * gp_ref[...])

    v = _vec((1, D))
    return _rowwise("post_fwd", body, grid=s // tb,
                    ins=[(x, _rows(tb, D)), (out, _rows(tb, D)), (gate, v), (g_post, v)],
                    outs=[(_sds((s, D)), _rows(tb, D))])[0]


def _loss_head(y, target):
    s = y.shape[0]
    tb = 512

    def body(y_ref, t_ref, dy_ref, acc_ref):
        i = pl.program_id(0)

        @pl.when(i == 0)
        def _():
            acc_ref[...] = jnp.zeros_like(acc_ref)

        err = y_ref[...] - t_ref[...]
        dy_ref[...] = err * (1.0 / D)
        acc_ref[...] += jnp.sum(err * err, axis=0, keepdims=True)

    return _rowwise("loss_head", body, grid=s // tb,
                    ins=[(y, _rows(tb, D)), (target, _rows(tb, D))],
                    outs=[(_sds((s, D)), _rows(tb, D)), (_sds((1, D)), _vec((1, D)))])


def _accumulate(i, ref, val):
    @pl.when(i == 0)
    def _():
        ref[...] = val

    @pl.when(i > 0)
    def _():
        ref[...] += val


def _post_bwd(dx, out, gate, g_post):
    s = dx.shape[0]
    tb = 512

    def body(dx_ref, o_ref, gt_ref, gp_ref, do_ref, dgt_ref, dgp_ref):
        i = pl.program_id(0)
        ov = o_ref[...]
        dxv = dx_ref[...]
        rstd = lax.rsqrt(jnp.mean(ov * ov, axis=-1, keepdims=True) + NORM_EPS)
        nv = ov * rstd
        _accumulate(i, dgt_ref, jnp.sum(dxv * nv, axis=0, keepdims=True) * gp_ref[...])
        _accumulate(i, dgp_ref, jnp.sum(dxv * nv, axis=0, keepdims=True) * gt_ref[...])
        dn = dxv * (gt_ref[...] * gp_ref[...])
        do_ref[...] = (rstd * (dn - nv * jnp.mean(dn * nv, axis=-1, keepdims=True))).astype(do_ref.dtype)

    v = _vec((1, D))
    return _rowwise("post_bwd", body, grid=s // tb,
                    ins=[(dx, _rows(tb, D)), (out, _rows(tb, D)), (gate, v), (g_post, v)],
                    outs=[(_sds((s, D), _MXU), _rows(tb, D)), (_sds((1, D)), v), (_sds((1, D)), v)])


def _merge_bwd(dz, y_a, y_b, proj):
    s = dz.shape[0]
    tb = 256

    def body(dz_ref, ya_ref, yb_ref, ma_ref, mb_ref, dya_ref, dyb_ref, dma_ref, dmb_ref):
        dzv = dz_ref[...]
        ga = _sigmoid(ma_ref[...])
        gb = _sigmoid(mb_ref[...])
        dya_ref[...] = (dzv * ga).astype(dya_ref.dtype)
        dyb_ref[...] = (dzv * gb).astype(dyb_ref.dtype)
        dma_ref[...] = (dzv * ya_ref[...] * ga * (1.0 - ga)).astype(dma_ref.dtype)
        dmb_ref[...] = (dzv * yb_ref[...] * gb * (1.0 - gb)).astype(dmb_ref.dtype)

    r = _rows(tb, D)
    return _rowwise("merge_bwd", body, grid=s // tb,
                    ins=[(dz, r), (y_a, r), (y_b, r), (proj, _rows(tb, D, CB_MA)), (proj, _rows(tb, D, CB_MB))],
                    outs=[(_sds((s, D), _MXU), r)] * 4)


def _gating_bwd(d_aa, o, proj, d_ba, h_lru):
    s = o.shape[0]
    tb = 256

    def body(daa_ref, o_ref, ga_ref, dba_ref, h_ref, gl_ref, do_ref, dga_ref, dh_ref, dgl_ref):
        ga = ga_ref[...]
        sa = _sigmoid(ga)
        daa = daa_ref[...]
        do_ref[...] = daa * (ga * sa)
        dga_ref[...] = (daa * o_ref[...] * (sa * (1.0 + ga * (1.0 - sa)))).astype(dga_ref.dtype)
        gl = gl_ref[...]
        sl = _sigmoid(gl)
        dba = dba_ref[...]
        dh_ref[...] = dba * (gl * sl)
        dgl_ref[...] = (dba * h_ref[...] * (sl * (1.0 + gl * (1.0 - sl)))).astype(dgl_ref.dtype)

    r5, r10 = _rows(tb, ATT_W), _rows(tb, D)
    return _rowwise("gating_bwd", body, grid=s // tb,
                    ins=[(d_aa, r5), (o, r5), (proj, _rows(tb, ATT_W, CB_GATT)), (d_ba, r10), (h_lru, r10),
                         (proj, _rows(tb, D, CB_GLRU))],
                    outs=[(_sds((s, ATT_W)), r5), (_sds((s, ATT_W), _MXU), r5), (_sds((s, D)), r10),
                          (_sds((s, D), _MXU), r10)])


def _scan_bwd(dh, pre, uc, h_lru, b_rg, b_ig, lam):
    s = uc.shape[0]
    tb = 256
    n = s // tb

    def body(dh_ref, pr_ref, pi_ref, uc_ref, h_ref, hp_ref, brg_ref, big_ref, lam_ref,
             dpre_ref, duc_ref, dbrg_ref, dbig_ref, dlam_ref, carry):
        i = pl.program_id(0)

        @pl.when(i == 0)
        def _():
            carry[...] = jnp.zeros_like(carry)

        ucv = uc_ref[...]
        r, ig, sp, a, sq = _lru_gates(pr_ref[...], pi_ref[...], ucv, brg_ref[...], big_ref[...], lam_ref[...])
        row = lax.broadcasted_iota(jnp.int32, (tb, 1), 0)
        cv = jnp.where(row == tb - 1, 1.0, pltpu.roll(a, tb - 1, 0))
        gv = dh_ref[...]
        sh = 1
        while sh < tb:
            m = row < tb - sh
            g_sh = pltpu.roll(gv, tb - sh, 0)
            c_sh = pltpu.roll(cv, tb - sh, 0)
            gv = jnp.where(m, gv + cv * g_sh, gv)
            cv = jnp.where(m, cv * c_sh, cv)
            sh *= 2
        gv = gv + cv * carry[0:1, :]
        carry[...] = (a * gv)[0:8]

        halo = jnp.where(i < n - 1, hp_ref[...], 0.0)
        h_prev = _shift_down(h_ref[...], halo, 1, tb)
        d_a = gv * h_prev
        d_sq = gv * (ig * ucv)
        d_i = gv * sq * ucv
        duc_ref[...] = gv * sq * ig
        d_la = d_a * a - d_sq * (a * a) / sq
        d_r = d_la * (-LRU_C * sp)
        d_pre_r = d_r * r * (1.0 - r)
        d_pre_i = d_i * ig * (1.0 - ig)
        dpre_ref[:, 0:D] = d_pre_r.astype(dpre_ref.dtype)
        dpre_ref[:, D:2 * D] = d_pre_i.astype(dpre_ref.dtype)
        _accumulate(i, dbrg_ref, jnp.sum(d_pre_r, axis=0, keepdims=True))
        _accumulate(i, dbig_ref, jnp.sum(d_pre_i, axis=0, keepdims=True))
        lamv = lam_ref[...]
        _accumulate(i, dlam_ref, jnp.sum(d_la * (-LRU_C * r), axis=0, keepdims=True) * (-_sigmoid(-lamv)))

    v = _vec((1, D))
    rv = _rows(tb, D, 0, n)
    return _rowwise("scan_bwd", body, grid=n,
                    ins=[(dh, rv), (pre, _rows(tb, D, 0, n)), (pre, _rows(tb, D, 1, n)), (uc, rv), (h_lru, rv),
                         (h_lru, _halo_prev(tb, D, 0, n)), (b_rg, v), (b_ig, v), (lam, v)],
                    outs=[(_sds((s, 2 * D), _MXU), _rows(tb, 2 * D, 0, n)), (_sds((s, D)), rv),
                          (_sds((1, D)), v), (_sds((1, D)), v), (_sds((1, D)), v)],
                    scratch=[pltpu.VMEM((8, D), _F32)])


def _conv_bwd(duc_a, duc_b, proj, conv_w):
    s = duc_a.shape[0]
    tb = 256
    n = s // tb

    def body(da_ref, db_ref, dan_ref, dbn_ref, u_ref, up_ref, w_ref, du_ref, dw_ref, dbias_ref):
        i = pl.program_id(0)
        duc = da_ref[...] + db_ref[...]
        nxt = jnp.where(i < n - 1, dan_ref[...] + dbn_ref[...], 0.0)
        u = u_ref[...]
        halo = jnp.where(i > 0, up_ref[...], 0.0)
        du = duc * w_ref[0:1, :]
        dws = [jnp.sum(duc * u, axis=0, keepdims=True)]
        for j in range(1, 4):
            du = du + _shift_up(duc, nxt, j, tb) * w_ref[j:j + 1, :]
            dws.append(jnp.sum(duc * _shift_down(u, halo, j, tb), axis=0, keepdims=True))
        du_ref[...] = du.astype(du_ref.dtype)
        for j in range(4):
            _accumulate(i, dw_ref.at[j:j + 1, :], dws[j])
        _accumulate(i, dbias_ref, jnp.sum(duc, axis=0, keepdims=True))

    r = _rows(tb, D)
    return _rowwise("conv_bwd", body, grid=n,
                    ins=[(duc_a, r), (duc_b, r), (duc_a, _halo_next(tb, D, n)), (duc_b, _halo_next(tb, D, n)),
                         (proj, _rows(tb, D, CB_U)), (proj, _halo_prev(tb, D, CB_U)), (conv_w, _vec((4, D)))],
                    outs=[(_sds((s, D), _MXU), r), (_sds((4, D)), _vec((4, D))), (_sds((1, D)), _vec((1, D)))])


def _prenorm_bwd(dh, x, dx_out, g_pre, scale):
    s = x.shape[0]
    tb = 512

    def body(dh_ref, x_ref, dxo_ref, g_ref, sc_ref, dx_ref, dsh_ref, dsc_ref, dg_ref):
        i = pl.program_id(0)
        xv = x_ref[...]
        dhv = dh_ref[...]
        rstd = lax.rsqrt(jnp.mean(xv * xv, axis=-1, keepdims=True) + NORM_EPS)
        xn = xv * rstd
        one_sc = 1.0 + sc_ref[...]
        s1 = jnp.sum(dhv * xn, axis=0, keepdims=True)
        _accumulate(i, dsh_ref, jnp.sum(dhv, axis=0, keepdims=True))
        _accumulate(i, dsc_ref, s1 * g_ref[...])
        _accumulate(i, dg_ref, s1 * one_sc)
        dxn = dhv * (g_ref[...] * one_sc)
        dx_ref[...] = dxo_ref[...] + rstd * (dxn - xn * jnp.mean(dxn * xn, axis=-1, keepdims=True))

    v = _vec((1, D))
    r = _rows(tb, D)
    return _rowwise("prenorm_bwd", body, grid=s // tb,
                    ins=[(dh, r), (x, r), (dx_out, r), (g_pre, v), (scale, v)],
                    outs=[(_sds((s, D)), r), (_sds((1, D)), v), (_sds((1, D)), v), (_sds((1, D)), v)])


def _band_tiles(dil):
    tiles = []
    for rho in range(dil):
        for b in range(16 // dil):
            qs = rho + dil * BAND * b
            tiles.append((qs, QBLK + qs - dil * BAND, b))
    return tiles


def _strided(start, size, dil):
    return pl.ds(start, size, stride=dil) if dil > 1 else pl.ds(start, size)


def _band_mask(i, b):
    qi = lax.broadcasted_iota(jnp.int32, (BAND, 2 * BAND), 0)
    ki = lax.broadcasted_iota(jnp.int32, (BAND, 2 * BAND), 1)
    valid = (ki >= qi) & (ki <= qi + BAND)
    if b == 0:
        valid = valid & ((ki >= BAND) | (i > 0))
    return valid


def _attn_fwd(proj, g):
    s = proj.shape[0]
    dil = DILATIONS[g]
    n = s // QBLK
    scale = HEAD ** -0.5
    tiles = _band_tiles(dil)

    def body(q_ref, kp_ref, kc_ref, vp_ref, vc_ref, o_ref, m_ref, l_ref, kbuf, vbuf):
        i = pl.program_id(1)
        kbuf[0:QBLK, :] = kp_ref[...]
        kbuf[QBLK:2 * QBLK, :] = kc_ref[...]
        vbuf[0:QBLK, :] = vp_ref[...]
        vbuf[QBLK:2 * QBLK, :] = vc_ref[...]
        for qs, ks, b in tiles:
            q = q_ref[_strided(qs, BAND, dil), :].astype(_MXU)
            kk = kbuf[_strided(ks, 2 * BAND, dil), :].astype(_MXU)
            vv = vbuf[_strided(ks, 2 * BAND, dil), :].astype(_MXU)
            sc = lax.dot_general(q, kk, _NT, preferred_element_type=_F32) * scale
            sc = jnp.where(_band_mask(i, b), sc, NEG_INF)
            m = jnp.max(sc, axis=-1, keepdims=True)
            p = jnp.exp(sc - m)
            l = jnp.sum(p, axis=-1, keepdims=True)
            o = jnp.dot(p.astype(_MXU), vv, preferred_element_type=_F32) / l
            o_ref[_strided(qs, BAND, dil), :] = o
            m_ref[_strided(qs, BAND, dil), :] = jnp.broadcast_to(m, (BAND, HEAD))
            l_ref[_strided(qs, BAND, dil), :] = jnp.broadcast_to(l, (BAND, HEAD))

    blk = (QBLK, HEAD)
    cq, ck, cv = g * HEADS, 12 + g * HEADS, 24 + g * HEADS
    out_spec = pl.BlockSpec(blk, lambda j, i: (i, j))
    return pl.pallas_call(
        body, name="attn_fwd_d%d" % dil, grid=(HEADS, n),
        in_specs=[pl.BlockSpec(blk, lambda j, i: (i, cq + j)),
                  pl.BlockSpec(blk, lambda j, i: (jnp.maximum(i - 1, 0), ck + j)),
                  pl.BlockSpec(blk, lambda j, i: (i, ck + j)),
                  pl.BlockSpec(blk, lambda j, i: (jnp.maximum(i - 1, 0), cv + j)),
                  pl.BlockSpec(blk, lambda j, i: (i, cv + j))],
        out_specs=[out_spec] * 3, out_shape=[_sds((s, ATT_W))] * 3,
        scratch_shapes=[pltpu.VMEM((2 * QBLK, HEAD), _F32)] * 2,
        compiler_params=_params(2))(proj, proj, proj, proj, proj)


def _attn_bwd(proj, d_o, o, lse, g):
    s = proj.shape[0]
    dil = DILATIONS[g]
    n = s // QBLK
    scale = HEAD ** -0.5
    tiles = _band_tiles(dil)

    def body(q_ref, kp_ref, kc_ref, vp_ref, vc_ref, do_ref, o_ref, lse_ref, dq_ref, dk_ref, dv_ref,
             kbuf, vbuf, dkbuf, dvbuf, dqbuf):
        i = pl.program_id(1)

        @pl.when(i == 0)
        def _():
            dkbuf[0:QBLK, :] = jnp.zeros((QBLK, HEAD), _F32)
            dvbuf[0:QBLK, :] = jnp.zeros((QBLK, HEAD), _F32)

        @pl.when(i < n)
        def _():
            kbuf[0:QBLK, :] = kp_ref[...]
            kbuf[QBLK:2 * QBLK, :] = kc_ref[...]
            vbuf[0:QBLK, :] = vp_ref[...]
            vbuf[QBLK:2 * QBLK, :] = vc_ref[...]
            dkbuf[QBLK:2 * QBLK, :] = jnp.zeros((QBLK, HEAD), _F32)
            dvbuf[QBLK:2 * QBLK, :] = jnp.zeros((QBLK, HEAD), _F32)
            for qs, ks, b in tiles:
                qsl = _strided(qs, BAND, dil)
                ksl = _strided(ks, 2 * BAND, dil)
                q = q_ref[qsl, :].astype(_MXU)
                kk = kbuf[ksl, :].astype(_MXU)
                vv = vbuf[ksl, :].astype(_MXU)
                dov = do_ref[qsl, :]
                dd = jnp.sum(dov * o_ref[qsl, :], axis=-1, keepdims=True)
                lse_t = lse_ref[qsl, :][:, 0:1]
                sc = lax.dot_general(q, kk, _NT, preferred_element_type=_F32) * scale
                p = jnp.where(_band_mask(i, b), jnp.exp(sc - lse_t), 0.0)
                dob = dov.astype(_MXU)
                dp = lax.dot_general(dob, vv, _NT, preferred_element_type=_F32)
                ds = (p * (dp - dd) * scale).astype(_MXU)
                dqbuf[qsl, :] = jnp.dot(ds, kk, preferred_element_type=_F32)
                dkbuf[ksl, :] += lax.dot_general(ds, q, _TN, preferred_element_type=_F32)
                dvbuf[ksl, :] += lax.dot_general(p.astype(_MXU), dob, _TN, preferred_element_type=_F32)
            dq_ref[...] = dqbuf[...].astype(dq_ref.dtype)

        dk_ref[...] = dkbuf[0:QBLK, :].astype(dk_ref.dtype)
        dv_ref[...] = dvbuf[0:QBLK, :].astype(dv_ref.dtype)
        dkbuf[0:QBLK, :] = dkbuf[QBLK:2 * QBLK, :]
        dvbuf[0:QBLK, :] = dvbuf[QBLK:2 * QBLK, :]

    blk = (QBLK, HEAD)
    cq, ck, cv = g * HEADS, 12 + g * HEADS, 24 + g * HEADS

    def cur(i):
        return jnp.minimum(i, n - 1)

    def prev(i):
        return jnp.maximum(jnp.minimum(i, n - 1) - 1, 0)

    own = pl.BlockSpec(blk, lambda j, i: (cur(i), j))
    late = pl.BlockSpec(blk, lambda j, i: (jnp.maximum(i - 1, 0), j))
    return pl.pallas_call(
        body, name="attn_bwd_d%d" % dil, grid=(HEADS, n + 1),
        in_specs=[pl.BlockSpec(blk, lambda j, i: (cur(i), cq + j)),
                  pl.BlockSpec(blk, lambda j, i: (prev(i), ck + j)),
                  pl.BlockSpec(blk, lambda j, i: (cur(i), ck + j)),
                  pl.BlockSpec(blk, lambda j, i: (prev(i), cv + j)),
                  pl.BlockSpec(blk, lambda j, i: (cur(i), cv + j)),
                  own, own, own],
        out_specs=[own, late, late], out_shape=[_sds((s, ATT_W), _MXU)] * 3,
        scratch_shapes=[pltpu.VMEM((2 * QBLK, HEAD), _F32)] * 4 + [pltpu.VMEM((QBLK, HEAD), _F32)],
        compiler_params=_params(2))(proj, proj, proj, proj, proj, d_o, o, lse)


def _block_diag(w):
    eye = jnp.eye(16, dtype=w.dtype)
    return jnp.einsum("hij,hg->higj", w, eye).reshape(D, D)


def _diag_blocks(gd):
    g4 = gd.reshape(16, 64, 16, 64)
    keep = jnp.eye(16, dtype=jnp.bool_)[:, None, :, None]
    return jnp.sum(jnp.where(keep, g4, 0.0), axis=2)


def _layer_fwd(l, x, p, gw):
    s = x.shape[0]
    nm = s // 1024
    h, h_t = _prenorm_fwd(x, p["g_pre"], p["shift"], p["scale"])
    proj = _mm("proj", h, gw["w_in"], _sds((s, IN_W)), grid=(nm, 12, 1),
               a_spec=pl.BlockSpec((1024, D), lambda m, n, k: (m, 0)),
               b_spec=pl.BlockSpec((None, None, D, 768), lambda m, n, k: (n // 3, l, 0, n % 3)),
               o_spec=pl.BlockSpec((1024, 768), lambda m, n, k: (m, n)), dims=_NN, acc_shape=(1024, 768))
    og, mg, lg = [], [], []
    for g in range(3):
        o_g, m_g, l_g = _attn_fwd(proj, g)
        og.append(o_g)
        mg.append(m_g)
        lg.append(l_g)
    uc = _conv_fwd(proj, p["conv_w"], p["conv_b"])
    pre = _mm("lru_gates", uc, p["w_gates"], _sds((s, 2 * D)), grid=(nm, 2, 1),
              a_spec=pl.BlockSpec((1024, D), lambda m, n, k: (m, 0)),
              b_spec=pl.BlockSpec((D, D), lambda m, n, k: (0, n)),
              o_spec=pl.BlockSpec((1024, D), lambda m, n, k: (m, n)), dims=_NN, acc_shape=(1024, D))
    h_lru = _scan_fwd(pre, uc, p["b_rg"], p["b_ig"], p["lam"])
    o, lse, a_att, b_act = _gating_fwd(og, mg, lg, proj, h_lru)
    y_a = _mm("proj_a", a_att, gw["w_pa"], _sds((s, D)), grid=(nm, 4, 1),
              a_spec=pl.BlockSpec((1024, ATT_W), lambda m, n, k: (m, 0)),
              b_spec=pl.BlockSpec((None, None, ATT_W, 256), lambda m, n, k: (n, l, 0, 0)),
              o_spec=pl.BlockSpec((1024, 256), lambda m, n, k: (m, n)), dims=_NN, acc_shape=(1024, 256))
    rows_w = pl.BlockSpec((None, None, 256, D), lambda m, n, k: (k, l, 0, 0))
    a_k = pl.BlockSpec((1024, 256), lambda m, n, k: (m, k))
    o_full = pl.BlockSpec((1024, D), lambda m, n, k: (m, 0))
    y_b = _mm("proj_b", b_act, gw["w_pb"], _sds((s, D)), grid=(nm, 1, 4), a_spec=a_k, b_spec=rows_w,
              o_spec=o_full, dims=_NN, acc_shape=(1024, D))
    z = _merge_fwd(y_a, y_b, proj)
    out = _mm("proj_o", z, gw["w_o"], _sds((s, D)), grid=(nm, 1, 4), a_spec=a_k, b_spec=rows_w,
              o_spec=o_full, dims=_NN, acc_shape=(1024, D))
    x_new = _post_fwd(x, out, p["gate"], p["g_post"])
    saved = dict(x=x, h_t=h_t, proj=proj, o=o, lse=lse, uc=uc, pre=pre, h_lru=h_lru, a_att=a_att, b_act=b_act,
                 y_a=y_a, y_b=y_b, z=z, out=out)
    return x_new, saved


def _layer_bwd(l, dx, p, gw, sv, big):
    s = dx.shape[0]
    nm = s // 1024
    nt = s // 1024
    proj = sv["proj"]
    d_out, d_gate, d_gpost = _post_bwd(dx, sv["out"], p["gate"], p["g_post"])

    def wgrad_rows(name, a, b, into):
        return _mm(name, a, b, _sds((2, N_CHIPS, 256, D)), grid=(4, 1, nt),
                   a_spec=pl.BlockSpec((1024, 256), lambda m, n, k: (k, m)),
                   b_spec=pl.BlockSpec((1024, D), lambda m, n, k: (k, 0)),
                   o_spec=pl.BlockSpec((None, None, 256, D), lambda m, n, k: (l, m, 0, 0)),
                   dims=_TN, acc_shape=(256, D), into=into)

    def dgrad_rows(name, a, w):
        return _mm(name, a, w, _sds((s, D)), grid=(nm, 4, 1),
                   a_spec=pl.BlockSpec((1024, D), lambda m, n, k: (m, 0)),
                   b_spec=pl.BlockSpec((None, None, 256, D), lambda m, n, k: (n, l, 0, 0)),
                   o_spec=pl.BlockSpec((1024, 256), lambda m, n, k: (m, n)), dims=_NT, acc_shape=(1024, 256))

    dz = dgrad_rows("d_z", d_out, gw["w_o"])
    big = dict(big)
    big["w_o"] = wgrad_rows("g_w_o", sv["z"], d_out, big.get("w_o"))
    dy_a, dy_b, d_ma, d_mb = _merge_bwd(dz, sv["y_a"], sv["y_b"], proj)
    d_aa = _mm("d_a_att", dy_a, gw["w_pa"], _sds((s, ATT_W)), grid=(nm, 1, 4),
               a_spec=pl.BlockSpec((1024, 256), lambda m, n, k: (m, k)),
               b_spec=pl.BlockSpec((None, None, ATT_W, 256), lambda m, n, k: (k, l, 0, 0)),
               o_spec=pl.BlockSpec((1024, ATT_W), lambda m, n, k: (m, 0)), dims=_NT, acc_shape=(1024, ATT_W))
    big["w_pa"] = _mm("g_w_pa", sv["a_att"], dy_a, _sds((2, N_CHIPS, ATT_W, 256)), grid=(1, 4, nt),
                      a_spec=pl.BlockSpec((1024, ATT_W), lambda m, n, k: (k, 0)),
                      b_spec=pl.BlockSpec((1024, 256), lambda m, n, k: (k, n)),
                      o_spec=pl.BlockSpec((None, None, ATT_W, 256), lambda m, n, k: (l, n, 0, 0)),
                      dims=_TN, acc_shape=(ATT_W, 256), into=big.get("w_pa"))
    d_ba = dgrad_rows("d_b_act", dy_b, gw["w_pb"])
    big["w_pb"] = wgrad_rows("g_w_pb", sv["b_act"], dy_b, big.get("w_pb"))
    d_o, d_gatt, dh_lru, d_glru = _gating_bwd(d_aa, sv["o"], proj, d_ba, sv["h_lru"])
    d_pre, duc_dir, d_brg, d_big, d_lam = _scan_bwd(dh_lru, sv["pre"], sv["uc"], sv["h_lru"],
                                                   p["b_rg"], p["b_ig"], p["lam"])
    duc_mm = _mm("d_uc", d_pre, p["w_gates"], _sds((s, D)), grid=(nm, 1, 2),
                 a_spec=pl.BlockSpec((1024, D), lambda m, n, k: (m, k)),
                 b_spec=pl.BlockSpec((D, D), lambda m, n, k: (0, k)),
                 o_spec=pl.BlockSpec((1024, D), lambda m, n, k: (m, 0)), dims=_NT, acc_shape=(1024, D))
    g_gates = _mm("g_w_gates", sv["uc"], d_pre, _sds((D, 2 * D)), grid=(1, 2, s // 512),
                  a_spec=pl.BlockSpec((512, D), lambda m, n, k: (k, 0)),
                  b_spec=pl.BlockSpec((512, D), lambda m, n, k: (k, n)),
                  o_spec=pl.BlockSpec((D, D), lambda m, n, k: (0, n)), dims=_TN, acc_shape=(D, D))
    d_u, g_convw, g_convb = _conv_bwd(duc_dir, duc_mm, proj, p["conv_w"])
    dqkv = [_attn_bwd(proj, d_o, sv["o"], sv["lse"], g) for g in range(3)]
    dproj = jnp.concatenate([dqkv[g][t] for t in range(3) for g in range(3)]
                            + [d_gatt, d_u, d_glru, d_ma, d_mb], axis=1)
    dh = _mm("d_h", dproj, gw["w_in"], _sds((s, D)), grid=(nm, 1, 12),
             a_spec=pl.BlockSpec((1024, 768), lambda m, n, k: (m, k)),
             b_spec=pl.BlockSpec((None, None, D, 768), lambda m, n, k: (k // 3, l, 0, k % 3)),
             o_spec=pl.BlockSpec((1024, D), lambda m, n, k: (m, 0)), dims=_NT, acc_shape=(1024, D))
    big["w_in"] = _mm("g_w_in", sv["h_t"], dproj, _sds((2, N_CHIPS, D, 2304)), grid=(1, 12, s // 512),
                      a_spec=pl.BlockSpec((D, 512), lambda m, n, k: (0, k)),
                      b_spec=pl.BlockSpec((512, 768), lambda m, n, k: (k, n)),
                      o_spec=pl.BlockSpec((None, None, D, 768), lambda m, n, k: (l, n // 3, 0, n % 3)),
                      dims=_NN, acc_shape=(D, 768), into=big.get("w_in"))
    dx_in, d_shift, d_scale, d_gpre = _prenorm_bwd(dh, sv["x"], dx, p["g_pre"], p["scale"])
    small = dict(dmod=jnp.concatenate([d_shift, d_scale, d_gate], axis=1), g_pre=d_gpre, conv_w=g_convw,
                 conv_b=g_convb, w_rg=_diag_blocks(g_gates[:, 0:D]), b_rg=d_brg,
                 w_ig=_diag_blocks(g_gates[:, D:2 * D]), b_ig=d_big, lam=d_lam, g_post=d_gpost)
    return dx_in, small, big


def _local_step(x, target, small_p, gw):
    saved = []
    h = x
    for l in range(2):
        h, sv = _layer_fwd(l, h, small_p[l], gw)
        saved.append(sv)
    dy, sq = _loss_head(h, target)
    loss = 0.5 * jnp.sum(sq) / D
    big = {}
    smalls = [None, None]
    dx = dy
    for l in (1, 0):
        dx, smalls[l], big = _layer_bwd(l, dx, small_p[l], gw, saved[l], big)
    return loss, dx, smalls, big


_SMALL_ROWS = 8 + 16 + 8 + 128 + 128


def _pack_small(smalls):
    dmod = jnp.concatenate([smalls[0]["dmod"].reshape(3, D), smalls[1]["dmod"].reshape(3, D),
                            jnp.zeros((2, D), _F32)], axis=0)
    vecs = jnp.concatenate([smalls[l][k] for k in ("g_pre", "conv_b", "b_rg", "b_ig", "lam", "g_post")
                            for l in range(2)] + [jnp.zeros((4, D), _F32)], axis=0)
    convw = jnp.concatenate([smalls[0]["conv_w"], smalls[1]["conv_w"]], axis=0)
    wrg = jnp.stack([smalls[0]["w_rg"], smalls[1]["w_rg"]]).reshape(128, D)
    wig = jnp.stack([smalls[0]["w_ig"], smalls[1]["w_ig"]]).reshape(128, D)
    return jnp.concatenate([dmod, vecs, convw, wrg, wig], axis=0)


def kernel(x, c, w_mod, b_mod, g_pre, w_in, conv_w, conv_b, w_rg, b_rg, w_ig, b_ig, lru_lambda, w_pa, w_pb, w_o, g_post, loss_target, m_w_mod, m_b_mod, m_g_pre, m_w_in, m_conv_w, m_conv_b, m_w_rg, m_b_rg, m_w_ig, m_b_ig, m_lru_lambda, m_w_pa, m_w_pb, m_w_o, m_g_post, v_w_mod, v_b_mod, v_g_pre, v_w_in, v_conv_w, v_conv_b, v_w_rg, v_b_rg, v_w_ig, v_b_ig, v_lru_lambda, v_w_pa, v_w_pb, v_w_o, v_g_post):
    xi, yi, ci = lax.axis_index("x"), lax.axis_index("y"), lax.axis_index("c")
    chip = 2 * xi + yi
    dev = 4 * xi + 2 * yi + ci
    mcols = w_mod.shape[2]

    pack1 = jnp.concatenate([jnp.broadcast_to(c, (8, D)),
                             jnp.pad(conv_w.reshape(8, 256), ((0, 0), (0, D - 256)))], axis=0)
    g1 = _exchange("gather_cond", [pack1], "xyc", False)[0]
    c_all = g1[:, 0, :]
    conv_w_full = jnp.transpose(g1[0::2, 8:16, 0:256], (1, 0, 2)).reshape(2, 4, D)

    b_cols = lax.dynamic_slice(b_mod, (0, chip * mcols), (2, mcols)).reshape(2, 1, mcols)
    mod_loc = _mod_fwd(c_all, w_mod, b_cols)
    g2 = _exchange("gather_mod", [mod_loc.reshape(16, mcols)], "xyc", False)[0]
    mod_full = jnp.transpose(g2[0::2], (1, 0, 2)).reshape(2, 8, 3 * D)
    mod_me = lax.dynamic_index_in_dim(mod_full, dev, axis=1, keepdims=False)

    wb = [_cast("cast_w_in", w_in.reshape(2 * D, 2304), 256).reshape(2, D, 2304),
          _cast("cast_w_pa", w_pa.reshape(2 * ATT_W, 256), 256).reshape(2, ATT_W, 256),
          _cast("cast_w_pb", w_pb.reshape(512, D), 256).reshape(2, 256, D),
          _cast("cast_w_o", w_o.reshape(512, D), 256).reshape(2, 256, D)]
    gl = _gather_weights(wb, [4, 1, 1, 1])
    gw = dict(w_in=gl[0], w_pa=gl[1], w_pb=gl[2], w_o=gl[3])

    small_p = []
    for l in range(2):
        gates = jnp.concatenate([_block_diag(w_rg[l]), _block_diag(w_ig[l])], axis=1).astype(_MXU)
        small_p.append(dict(
            shift=mod_me[l:l + 1, 0:D], scale=mod_me[l:l + 1, D:2 * D], gate=mod_me[l:l + 1, 2 * D:3 * D],
            g_pre=g_pre[l:l + 1], conv_w=conv_w_full[l], conv_b=conv_b[l:l + 1], w_gates=gates,
            b_rg=b_rg[l:l + 1], b_ig=b_ig[l:l + 1], lam=lru_lambda[l:l + 1], g_post=g_post[l:l + 1]))

    loss_loc, dx, smalls, big = _local_step(x[0], loss_target[0], small_p, gw)
    loss = lax.psum(loss_loc, ("x", "y", "c"))
    grad_x = dx[None]

    names = ("w_in", "w_pa", "w_pb", "w_o")
    core = jnp.reshape(ci, (1,)).astype(jnp.int32)
    where = jnp.stack([chip, ci]).astype(jnp.int32)
    pair = list(_exchange("reduce_pair", [big["w_in"].reshape(2, 16, 256, 2304)] + [big[k] for k in names[1:]],
                          "c", True, local=False, nchunks=[16, 4, 4, 4]))
    pair[0] = pair[0].reshape(N_CHIPS, D, 2304)
    t1 = [_sum_pair("sum_pair_" + k, big[k], r, core, 128) for k, r in zip(names, pair)]
    quad = _exchange("reduce_chips", [t[1] for t in t1], "xy", True, local=False, nchunks=[4, 1, 1, 1])
    t3 = [_sum_chips("sum_chips_" + k, t[0], r, where, 128) for k, t, r in zip(names, t1, quad)]
    both = _pair_fill("gather_layers", t3, [4, 1, 1, 1])
    g_big = dict(zip(names, both))

    g3 = _exchange("gather_small", [_pack_small(smalls)], "xyc", False)[0]
    tot = _sum_lead("sum_small", g3, 96)
    dmod_all = g3[:, 0:6, :].reshape(8, 2, 3 * D)
    dm_cols = jnp.transpose(lax.dynamic_slice(dmod_all, (0, 0, chip * mcols), (8, 2, mcols)), (1, 0, 2))
    g_w_mod = _mod_bwd(jnp.transpose(c_all), dm_cols)
    vec = tot[8:20].reshape(6, 2, D)
    g_conv_w_full = tot[24:32].reshape(2, 4, D)
    grads = dict(
        w_mod=g_w_mod, b_mod=tot[0:6].reshape(2, 3 * D), g_pre=vec[0], w_in=g_big["w_in"],
        conv_w=lax.dynamic_slice(g_conv_w_full, (0, 0, chip * 256), (2, 4, 256)), conv_b=vec[1],
        w_rg=tot[32:160].reshape(2, 16, 64, 64), b_rg=vec[2], w_ig=tot[160:288].reshape(2, 16, 64, 64),
        b_ig=vec[3], lru_lambda=vec[4], w_pa=g_big["w_pa"], w_pb=g_big["w_pb"], w_o=g_big["w_o"],
        g_post=vec[5])

    weights = dict(w_mod=w_mod, b_mod=b_mod, g_pre=g_pre, w_in=w_in, conv_w=conv_w, conv_b=conv_b, w_rg=w_rg,
                   b_rg=b_rg, w_ig=w_ig, b_ig=b_ig, lru_lambda=lru_lambda, w_pa=w_pa, w_pb=w_pb, w_o=w_o,
                   g_post=g_post)
    ms = dict(w_mod=m_w_mod, b_mod=m_b_mod, g_pre=m_g_pre, w_in=m_w_in, conv_w=m_conv_w, conv_b=m_conv_b,
              w_rg=m_w_rg, b_rg=m_b_rg, w_ig=m_w_ig, b_ig=m_b_ig, lru_lambda=m_lru_lambda, w_pa=m_w_pa,
              w_pb=m_w_pb, w_o=m_w_o, g_post=m_g_post)
    vs = dict(w_mod=v_w_mod, b_mod=v_b_mod, g_pre=v_g_pre, w_in=v_w_in, conv_w=v_conv_w, conv_b=v_conv_b,
              w_rg=v_w_rg, b_rg=v_b_rg, w_ig=v_w_ig, b_ig=v_b_ig, lru_lambda=v_lru_lambda, w_pa=v_w_pa,
              w_pb=v_w_pb, w_o=v_w_o, g_post=v_g_post)
    flat = dict(w_mod=(2 * D, mcols, 256), b_mod=(2, 3 * D, 2), g_pre=(2, D, 2), w_in=(2 * D, 2304, 256),
                conv_w=(8, 256, 8), conv_b=(2, D, 2), w_rg=(128, D, 128), b_rg=(2, D, 2), w_ig=(128, D, 128),
                b_ig=(2, D, 2), lru_lambda=(2, D, 2), w_pa=(2 * ATT_W, 256, 256), w_pb=(512, D, 256),
                w_o=(512, D, 256), g_post=(2, D, 2))
    order = ("w_mod", "b_mod", "g_pre", "w_in", "conv_w", "conv_b", "w_rg", "b_rg", "w_ig", "b_ig",
             "lru_lambda", "w_pa", "w_pb", "w_o", "g_post")
    deltas, new_m, new_v = [], [], []
    for k in order:
        rows, cols, tb = flat[k]
        shp = weights[k].shape
        d, nm_, nv_ = _adamw("adamw_" + k, weights[k].reshape(rows, cols), grads[k].reshape(rows, cols),
                             ms[k].reshape(rows, cols), vs[k].reshape(rows, cols), tb)
        deltas.append(d.reshape(shp))
        new_m.append(nm_.reshape(shp))
        new_v.append(nv_.reshape(shp))
    return (loss, grad_x, *[grads[k].reshape(weights[k].shape) for k in order], *deltas, *new_m, *new_v)
```

```python
import functools

import jax
import jax.numpy as jnp
from jax import lax
from jax.experimental import pallas as pl
from jax.experimental.pallas import tpu as pltpu

_F32 = jnp.float32
_MXU = jnp.bfloat16
_VMEM_LIMIT = 56 * 1024 * 1024
_MESH = pl.DeviceIdType.MESH

D = 1024
HEAD = 128
HEADS = 4
ATT_W = 512
QKV_W = 1536
IN_W = 9216
DILATIONS = (1, 4, 16)
BAND = 128
QBLK = BAND * 16
NORM_EPS = 1e-6
NEG_INF = -1e30
LRU_C = 8.0
N_CHIPS = 4
CB_GATT = 4608 // 512
CB_U, CB_GLRU, CB_MA, CB_MB = 5, 6, 7, 8

ADAM_LR, ADAM_B1, ADAM_B2, ADAM_EPS, ADAM_WD, ADAM_STEP = 0.001, 0.9, 0.999, 1e-08, 0.01, 10


def _params(ngrid):
    return pltpu.CompilerParams(dimension_semantics=("arbitrary",) * ngrid, vmem_limit_bytes=_VMEM_LIMIT)


def _sigmoid(v):
    return 1.0 / (1.0 + jnp.exp(-v))


_GROUPS = {
    "c": [(0, 0, 1)],
    "xy": [(1, 0, 0), (0, 1, 0), (1, 1, 0)],
    "xyc": [(0, 0, 1), (0, 1, 0), (0, 1, 1), (1, 0, 0), (1, 0, 1), (1, 1, 0), (1, 1, 1)],
}


def _rank(group, px, py, pc):
    if group == "c":
        return pc
    if group == "xy":
        return 2 * px + py
    return 4 * px + 2 * py + pc


def _flip(rel, x, y, c):
    dx, dy, dc = rel
    return (1 - x if dx else x, 1 - y if dy else y, 1 - c if dc else c)


def _pieces(ref, nchunk):
    step = ref.shape[0] // nchunk
    return [ref.at[pl.ds(q * step, step)] for q in range(nchunk)]


def _exchange(name, srcs, group, scatter, *, local=True, nchunks=None):
    rels = _GROUPS[group]
    gsize = len(rels) + 1
    n = len(srcs)
    nchunks = nchunks or [1] * n
    blks = [s.shape[1:] if scatter else s.shape for s in srcs]
    slotted = local or gsize > 2
    base = [sum(nchunks[:a]) for a in range(n)]
    tot = sum(nchunks)

    def body(*refs):
        src_refs, out_refs = refs[:n], refs[n:2 * n]
        send_sems, recv_sems, loc_sems = refs[2 * n:]
        x, y, c = lax.axis_index("x"), lax.axis_index("y"), lax.axis_index("c")
        me = _rank(group, x, y, c)
        copies = []
        for a in range(n):
            def part(r, a=a):
                return src_refs[a].at[r] if scatter else src_refs[a]
            dst = out_refs[a].at[me] if slotted else out_refs[a]
            if local:
                for q, (s_, d_) in enumerate(zip(_pieces(part(me), nchunks[a]), _pieces(dst, nchunks[a]))):
                    loc = pltpu.make_async_copy(s_, d_, loc_sems.at[base[a] + q])
                    loc.start()
                    copies.append(loc)
            for k, rel in enumerate(rels):
                peer = _flip(rel, x, y, c)
                for q, (s_, d_) in enumerate(zip(_pieces(part(_rank(group, *peer)), nchunks[a]),
                                                 _pieces(dst, nchunks[a]))):
                    cp = pltpu.make_async_remote_copy(
                        src_ref=s_, dst_ref=d_, send_sem=send_sems.at[(base[a] + q) * len(rels) + k],
                        recv_sem=recv_sems.at[(base[a] + q) * len(rels) + k],
                        device_id=peer, device_id_type=_MESH)
                    cp.start()
                    copies.append(cp)
        for cp in copies:
            cp.wait()

    any_spec = pl.BlockSpec(memory_space=pl.ANY)
    lead = (gsize,) if slotted else ()
    return pl.pallas_call(
        body, name=name,
        out_shape=[jax.ShapeDtypeStruct(lead + tuple(b), s.dtype) for b, s in zip(blks, srcs)],
        in_specs=[any_spec] * n, out_specs=[any_spec] * n,
        scratch_shapes=[pltpu.SemaphoreType.DMA((tot * len(rels),)), pltpu.SemaphoreType.DMA((tot * len(rels),)),
                        pltpu.SemaphoreType.DMA((tot,))],
    )(*srcs)


def _pair_fill(name, arrs, nchunks):
    n = len(arrs)
    base = [sum(nchunks[:a]) for a in range(n)]
    tot = sum(nchunks)

    def body(*refs):
        out_refs = refs[n:2 * n]
        send_sems, recv_sems = refs[2 * n:]
        x, y, c = lax.axis_index("x"), lax.axis_index("y"), lax.axis_index("c")
        copies = []
        for a in range(n):
            for q, blk in enumerate(_pieces(out_refs[a].at[c], nchunks[a])):
                cp = pltpu.make_async_remote_copy(
                    src_ref=blk, dst_ref=blk, send_sem=send_sems.at[base[a] + q], recv_sem=recv_sems.at[base[a] + q],
                    device_id=(x, y, 1 - c), device_id_type=_MESH)
                cp.start()
                copies.append(cp)
        for cp in copies:
            cp.wait()

    any_spec = pl.BlockSpec(memory_space=pl.ANY)
    return pl.pallas_call(
        body, name=name, out_shape=[jax.ShapeDtypeStruct(a.shape, a.dtype) for a in arrs],
        in_specs=[any_spec] * n, out_specs=[any_spec] * n, input_output_aliases={a: a for a in range(n)},
        scratch_shapes=[pltpu.SemaphoreType.DMA((tot,)), pltpu.SemaphoreType.DMA((tot,))],
    )(*arrs)


def _gather_weights(wb, nchunks):
    n = len(wb)
    rels = _GROUPS["xy"]
    base = [sum(nchunks[:a]) for a in range(n)]
    tot = sum(nchunks)

    def body(*refs):
        src_refs, out_refs = refs[:n], refs[n:2 * n]
        ici_send, ici_recv, d2d_send, d2d_recv, loc_sems = refs[2 * n:]
        x, y, c = lax.axis_index("x"), lax.axis_index("y"), lax.axis_index("c")
        me = 2 * x + y
        waits = []
        for a in range(n):
            for l in range(2):
                for q, (s_, d_) in enumerate(zip(_pieces(src_refs[a].at[l], nchunks[a]),
                                                 _pieces(out_refs[a].at[me, l], nchunks[a]))):
                    loc = pltpu.make_async_copy(s_, d_, loc_sems.at[(base[a] + q) * 2 + l])
                    loc.start()
                    waits.append(loc)
        first = []
        for a in range(n):
            for k, rel in enumerate(rels):
                px, py, _ = _flip(rel, x, y, c)
                for q, (s_, d_) in enumerate(zip(_pieces(src_refs[a].at[c], nchunks[a]),
                                                 _pieces(out_refs[a].at[me, c], nchunks[a]))):
                    sem = (base[a] + q) * 3 + k
                    cp = pltpu.make_async_remote_copy(src_ref=s_, dst_ref=d_, send_sem=ici_send.at[sem],
                                                      recv_sem=ici_recv.at[sem], device_id=(px, py, c),
                                                      device_id_type=_MESH)
                    cp.start()
                    first.append(cp)
        second = []
        for a in range(n):
            for k, rel in enumerate(rels):
                px, py, _ = _flip(rel, x, y, c)
                for q, blk in enumerate(_pieces(out_refs[a].at[2 * px + py, c], nchunks[a])):
                    sem = (base[a] + q) * 3 + k
                    landed = pltpu.make_async_remote_copy(src_ref=blk, dst_ref=blk, send_sem=ici_send.at[sem],
                                                          recv_sem=ici_recv.at[sem], device_id=(px, py, c),
                                                          device_id_type=_MESH)
                    landed.wait_recv()
                    cp = pltpu.make_async_remote_copy(src_ref=blk, dst_ref=blk, send_sem=d2d_send.at[sem],
                                                      recv_sem=d2d_recv.at[sem], device_id=(x, y, 1 - c),
                                                      device_id_type=_MESH)
                    cp.start()
                    second.append(cp)
        for cp in first:
            cp.wait_send()
        for cp in second:
            cp.wait_send()
        for a in range(n):
            for k, rel in enumerate(rels):
                px, py, _ = _flip(rel, x, y, c)
                for q, blk in enumerate(_pieces(out_refs[a].at[2 * px + py, 1 - c], nchunks[a])):
                    sem = (base[a] + q) * 3 + k
                    pltpu.make_async_remote_copy(src_ref=blk, dst_ref=blk, send_sem=d2d_send.at[sem],
                                                 recv_sem=d2d_recv.at[sem], device_id=(x, y, 1 - c),
                                                 device_id_type=_MESH).wait_recv()
        for cp in waits:
            cp.wait()

    any_spec = pl.BlockSpec(memory_space=pl.ANY)
    return pl.pallas_call(
        body, name="gather_weights",
        out_shape=[jax.ShapeDtypeStruct((N_CHIPS,) + a.shape, a.dtype) for a in wb],
        in_specs=[any_spec] * n, out_specs=[any_spec] * n,
        scratch_shapes=[pltpu.SemaphoreType.DMA((tot * 3,))] * 4 + [pltpu.SemaphoreType.DMA((tot * 2,))],
    )(*wb)


def _mm(name, a, b, out_sds, *, grid, a_spec, b_spec, o_spec, dims, acc_shape, into=None):
    nk = grid[2]

    def body(*refs):
        a_ref, b_ref = refs[0], refs[1]
        o_ref, acc = refs[-2], refs[-1]
        k = pl.program_id(2)
        part = lax.dot_general(a_ref[...].astype(_MXU), b_ref[...].astype(_MXU), dims,
                               preferred_element_type=_F32)
        if nk == 1:
            o_ref[...] = part.astype(o_ref.dtype)
            return

        @pl.when(k == 0)
        def _():
            acc[...] = part

        @pl.when(k > 0)
        def _():
            acc[...] += part

        @pl.when(k == nk - 1)
        def _():
            o_ref[...] = acc[...].astype(o_ref.dtype)

    if nk == 1:
        acc_shape = (8, 128)
    in_specs = [a_spec, b_spec]
    args = [a, b]
    aliases = {}
    if into is not None:
        in_specs.append(pl.BlockSpec(memory_space=pl.ANY))
        args.append(into)
        aliases = {2: 0}
    return pl.pallas_call(
        body, name=name, grid=grid, in_specs=in_specs, out_specs=o_spec, out_shape=out_sds,
        scratch_shapes=[pltpu.VMEM(acc_shape, _F32)], input_output_aliases=aliases,
        compiler_params=_params(3))(*args)


_NN = (((1,), (0,)), ((), ()))
_NT = (((1,), (1,)), ((), ()))
_TN = (((0,), (0,)), ((), ()))


def _rowwise(name, body, *, grid, ins, outs, scratch=()):
    return pl.pallas_call(
        body, name=name, grid=(grid,), in_specs=[s for _, s in ins], out_specs=[s for _, s in outs],
        out_shape=[o for o, _ in outs], scratch_shapes=list(scratch),
        compiler_params=_params(1))(*[a for a, _ in ins])


def _rows(tb, w, cb=0, n=None):
    if n is None:
        return pl.BlockSpec((tb, w), lambda i: (i, cb))
    return pl.BlockSpec((tb, w), lambda i: (n - 1 - i, cb))


def _vec(shape):
    return pl.BlockSpec(shape, lambda i: (0,) * len(shape))


def _halo_prev(tb, w, cb=0, n=None):
    if n is None:
        return pl.BlockSpec((8, w), lambda i: (jnp.maximum(i * (tb // 8) - 1, 0), cb))
    return pl.BlockSpec((8, w), lambda i: (jnp.maximum((n - 1 - i) * (tb // 8) - 1, 0), cb))


def _halo_next(tb, w, n, cb=0):
    return pl.BlockSpec((8, w), lambda i: (jnp.minimum((i + 1) * (tb // 8), n * (tb // 8) - 1), cb))


def _sds(shape, dtype=_F32):
    return jax.ShapeDtypeStruct(shape, dtype)


def _cast(name, a, tb):
    rows, cols = a.shape

    def body(a_ref, o_ref):
        o_ref[...] = a_ref[...].astype(o_ref.dtype)

    return _rowwise(name, body, grid=rows // tb, ins=[(a, _rows(tb, cols))],
                    outs=[(_sds((rows, cols), _MXU), _rows(tb, cols))])[0]


def _sum_lead(name, a, tb):
    g, rows, cols = a.shape

    def body(a_ref, o_ref):
        acc = a_ref[0]
        for k in range(1, g):
            acc = acc + a_ref[k]
        o_ref[...] = acc

    return _rowwise(name, body, grid=rows // tb,
                    ins=[(a, pl.BlockSpec((g, tb, cols), lambda i: (0, i, 0)))],
                    outs=[(_sds((rows, cols)), _rows(tb, cols))])[0]


def _sum_pair(name, mine, theirs, core, tb):
    _, nj, rows, cols = mine.shape

    def body(s_ref, a_ref, b_ref, o_ref, ob_ref):
        t = a_ref[...] + b_ref[...]
        o_ref[...] = t
        ob_ref[...] = t.astype(ob_ref.dtype)

    blk = pl.BlockSpec((None, tb, cols), lambda j, i, s: (j, i, 0))
    grid_spec = pltpu.PrefetchScalarGridSpec(
        num_scalar_prefetch=1, grid=(nj, rows // tb),
        in_specs=[pl.BlockSpec((None, None, tb, cols), lambda j, i, s: (s[0], j, i, 0)), blk],
        out_specs=[blk, blk])
    return pl.pallas_call(body, name=name, grid_spec=grid_spec,
                          out_shape=[_sds((nj, rows, cols)), _sds((nj, rows, cols), _MXU)],
                          compiler_params=_params(2))(core, mine, theirs)


def _sum_chips(name, mine, theirs, where, tb):
    _, rows, cols = mine.shape

    def body(s_ref, a_ref, b1_ref, b2_ref, b3_ref, o_ref):
        o_ref[...] = ((a_ref[...] + b1_ref[...].astype(_F32)) + b2_ref[...].astype(_F32)) + b3_ref[...].astype(_F32)

    def slot(k):
        return pl.BlockSpec((None, tb, cols), lambda i, s: (jnp.bitwise_xor(s[0], k), i, 0))

    grid_spec = pltpu.PrefetchScalarGridSpec(
        num_scalar_prefetch=1, grid=(rows // tb,),
        in_specs=[slot(0), slot(1), slot(2), slot(3)],
        out_specs=pl.BlockSpec((None, tb, cols), lambda i, s: (s[1], i, 0)))
    return pl.pallas_call(body, name=name, grid_spec=grid_spec, out_shape=_sds((2, rows, cols)),
                          compiler_params=_params(1))(where, mine, theirs, theirs, theirs)


def _adamw(name, w, g, m, v, tb):
    rows, cols = w.shape
    c1 = 1.0 - ADAM_B1 ** ADAM_STEP
    c2 = 1.0 - ADAM_B2 ** ADAM_STEP

    def body(w_ref, g_ref, m_ref, v_ref, d_ref, nm_ref, nv_ref):
        gv = g_ref[...]
        nm = ADAM_B1 * m_ref[...] + (1.0 - ADAM_B1) * gv
        nv = ADAM_B2 * v_ref[...] + (1.0 - ADAM_B2) * (gv * gv)
        d_ref[...] = -ADAM_LR * ((nm / c1) / (jnp.sqrt(nv / c2) + ADAM_EPS) + ADAM_WD * w_ref[...])
        nm_ref[...] = nm
        nv_ref[...] = nv

    spec = _rows(tb, cols)
    return _rowwise(name, body, grid=rows // tb, ins=[(w, spec), (g, spec), (m, spec), (v, spec)],
                    outs=[(_sds((rows, cols)), spec)] * 3)


def _mod_fwd(c_all, w_mod, b_cols):
    cols = w_mod.shape[2]

    def body(c_ref, w_ref, b_ref, o_ref):
        cv = c_ref[...]
        sc = (cv * _sigmoid(cv)).astype(_MXU)
        o_ref[...] = jnp.dot(sc, w_ref[...].astype(_MXU), preferred_element_type=_F32) + b_ref[...]

    return pl.pallas_call(
        body, name="mod_fwd", grid=(2,),
        in_specs=[pl.BlockSpec((8, D), lambda l: (0, 0)), pl.BlockSpec((None, D, cols), lambda l: (l, 0, 0)),
                  pl.BlockSpec((None, 1, cols), lambda l: (l, 0, 0))],
        out_specs=pl.BlockSpec((None, 8, cols), lambda l: (l, 0, 0)),
        out_shape=_sds((2, 8, cols)), compiler_params=_params(1))(c_all, w_mod, b_cols)


def _mod_bwd(c_all_t, dm):
    cols = dm.shape[2]

    def body(c_ref, d_ref, o_ref):
        cv = c_ref[...]
        sc = (cv * _sigmoid(cv)).astype(_MXU)
        o_ref[...] = jnp.dot(sc, d_ref[...].astype(_MXU), preferred_element_type=_F32)

    return pl.pallas_call(
        body, name="mod_bwd", grid=(2,),
        in_specs=[pl.BlockSpec((D, 8), lambda l: (0, 0)), pl.BlockSpec((None, 8, cols), lambda l: (l, 0, 0))],
        out_specs=pl.BlockSpec((None, D, cols), lambda l: (l, 0, 0)),
        out_shape=_sds((2, D, cols)), compiler_params=_params(1))(c_all_t, dm)


def _prenorm_fwd(x, g_pre, shift, scale):
    s = x.shape[0]
    tb = 512

    def body(x_ref, g_ref, sh_ref, sc_ref, h_ref, ht_ref):
        xv = x_ref[...]
        rstd = lax.rsqrt(jnp.mean(xv * xv, axis=-1, keepdims=True) + NORM_EPS)
        hv = (xv * rstd) * g_ref[...] * (1.0 + sc_ref[...]) + sh_ref[...]
        h_ref[...] = hv.astype(h_ref.dtype)
        ht_ref[...] = hv.T.astype(ht_ref.dtype)

    v = _vec((1, D))
    return _rowwise("prenorm_fwd", body, grid=s // tb,
                    ins=[(x, _rows(tb, D)), (g_pre, v), (shift, v), (scale, v)],
                    outs=[(_sds((s, D), _MXU), _rows(tb, D)),
                          (_sds((D, s), _MXU), pl.BlockSpec((D, tb), lambda i: (0, i)))])


def _shift_down(cur, halo, j, tb):
    ext = jnp.concatenate([halo, cur], axis=0)
    return pltpu.roll(ext, j, 0)[8:8 + tb]


def _shift_up(cur, halo, j, tb):
    ext = jnp.concatenate([cur, halo], axis=0)
    return pltpu.roll(ext, tb + 8 - j, 0)[0:tb]


def _conv_fwd(proj, conv_w, conv_b):
    s = proj.shape[0]
    tb = 512

    def body(u_ref, hp_ref, w_ref, b_ref, o_ref):
        i = pl.program_id(0)
        u = u_ref[...]
        halo = jnp.where(i > 0, hp_ref[...], 0.0)
        acc = b_ref[...] + u * w_ref[0:1, :]
        for j in range(1, 4):
            acc = acc + _shift_down(u, halo, j, tb) * w_ref[j:j + 1, :]
        o_ref[...] = acc

    return _rowwise("conv_fwd", body, grid=s // tb,
                    ins=[(proj, _rows(tb, D, CB_U)), (proj, _halo_prev(tb, D, CB_U)),
                         (conv_w, _vec((4, D))), (conv_b, _vec((1, D)))],
                    outs=[(_sds((s, D)), _rows(tb, D))])[0]


def _lru_gates(pre_r, pre_i, uc, b_rg, b_ig, lam):
    r = _sigmoid(pre_r + b_rg)
    ig = _sigmoid(pre_i + b_ig)
    nl = -lam
    sp = jnp.maximum(nl, 0.0) + jnp.log(1.0 + jnp.exp(-jnp.abs(nl)))
    la = -LRU_C * r * sp
    a = jnp.exp(la)
    y2 = 2.0 * la
    one_m_a2 = jnp.where(jnp.abs(y2) < 1e-2, -(y2 + 0.5 * y2 * y2 + (1.0 / 6.0) * y2 * y2 * y2),
                         1.0 - jnp.exp(y2))
    sq = jnp.sqrt(one_m_a2)
    return r, ig, sp, a, sq


def _scan_fwd(pre, uc, b_rg, b_ig, lam):
    s = uc.shape[0]
    tb = 256

    def body(pr_ref, pi_ref, uc_ref, brg_ref, big_ref, lam_ref, h_ref, carry):
        i = pl.program_id(0)

        @pl.when(i == 0)
        def _():
            carry[...] = jnp.zeros_like(carry)

        ucv = uc_ref[...]
        _, ig, _, a, sq = _lru_gates(pr_ref[...], pi_ref[...], ucv, brg_ref[...], big_ref[...], lam_ref[...])
        av = a
        bv = sq * (ig * ucv)
        row = lax.broadcasted_iota(jnp.int32, (tb, 1), 0)
        sh = 1
        while sh < tb:
            m = row >= sh
            b_sh = pltpu.roll(bv, sh, 0)
            a_sh = pltpu.roll(av, sh, 0)
            bv = jnp.where(m, av * b_sh + bv, bv)
            av = jnp.where(m, av * a_sh, av)
            sh *= 2
        hv = bv + av * carry[7:8, :]
        h_ref[...] = hv
        carry[...] = hv[tb - 8:tb]

    v = _vec((1, D))
    return _rowwise("scan_fwd", body, grid=s // tb,
                    ins=[(pre, _rows(tb, D, 0)), (pre, _rows(tb, D, 1)), (uc, _rows(tb, D)),
                         (b_rg, v), (b_ig, v), (lam, v)],
                    outs=[(_sds((s, D)), _rows(tb, D))],
                    scratch=[pltpu.VMEM((8, D), _F32)])[0]


def _gating_fwd(og, mg, lg, proj, h_lru):
    s = h_lru.shape[0]
    tb = 512

    def body(o0, o1, o2, m0, m1, m2, l0, l1, l2, ga_ref, h_ref, gl_ref, o_ref, lse_ref, aa_ref, ba_ref):
        ms = [m0[...], m1[...], m2[...]]
        mx = jnp.maximum(jnp.maximum(ms[0], ms[1]), ms[2])
        ws = [l[...] * jnp.exp(m - mx) for l, m in zip((l0, l1, l2), ms)]
        den = ws[0] + ws[1] + ws[2]
        o = (ws[0] * o0[...] + ws[1] * o1[...] + ws[2] * o2[...]) / den
        o_ref[...] = o
        lse_ref[...] = mx + jnp.log(den)
        ga = ga_ref[...]
        aa_ref[...] = (o * (ga * _sigmoid(ga))).astype(aa_ref.dtype)
        gl = gl_ref[...]
        ba_ref[...] = (h_ref[...] * (gl * _sigmoid(gl))).astype(ba_ref.dtype)

    r5 = _rows(tb, ATT_W)
    return _rowwise("gating_fwd", body, grid=s // tb,
                    ins=[(a, r5) for a in og] + [(a, r5) for a in mg] + [(a, r5) for a in lg]
                    + [(proj, _rows(tb, ATT_W, CB_GATT)), (h_lru, _rows(tb, D)), (proj, _rows(tb, D, CB_GLRU))],
                    outs=[(_sds((s, ATT_W)), r5), (_sds((s, ATT_W)), r5), (_sds((s, ATT_W), _MXU), r5),
                          (_sds((s, D), _MXU), _rows(tb, D))])


def _weight_specs(l):
    return [pl.BlockSpec((N_CHIPS, None, ATT_W, 256), lambda i: (0, l, 0, 0)),
            pl.BlockSpec((N_CHIPS, None, 256, D), lambda i: (0, l, 0, 0)),
            pl.BlockSpec((N_CHIPS, None, 256, D), lambda i: (0, l, 0, 0))]


def _tail_fwd(l, a_att, b_act, proj, x, gate, g_post, gw):
    s = x.shape[0]
    tb = 512

    def body(aa_ref, ba_ref, ma_ref, mb_ref, x_ref, gt_ref, gp_ref, wpa_ref, wpb_ref, wo_ref,
             ya_ref, yb_ref, z_ref, out_ref, xn_ref):
        aa = aa_ref[...]
        for j in range(N_CHIPS):
            ya_ref[:, j * 256:(j + 1) * 256] = jnp.dot(aa, wpa_ref[j], preferred_element_type=_F32)
        yb = jnp.dot(ba_ref[...], wpb_ref[...].reshape(D, D), preferred_element_type=_F32)
        yb_ref[...] = yb
        z = (_sigmoid(ma_ref[...]) * ya_ref[...] + _sigmoid(mb_ref[...]) * yb).astype(z_ref.dtype)
        z_ref[...] = z
        ov = jnp.dot(z, wo_ref[...].reshape(D, D), preferred_element_type=_F32)
        out_ref[...] = ov
        rstd = lax.rsqrt(jnp.mean(ov * ov, axis=-1, keepdims=True) + NORM_EPS)
        xn_ref[...] = x_ref[...] + gt_ref[...] * ((ov * rstd) * gp_ref[...])

    v = _vec((1, D))
    r = _rows(tb, D)
    return _rowwise("tail_fwd", body, grid=s // tb,
                    ins=[(a_att, _rows(tb, ATT_W)), (b_act, r), (proj, _rows(tb, D, CB_MA)),
                         (proj, _rows(tb, D, CB_MB)), (x, r), (gate, v), (g_post, v)]
                    + list(zip((gw["w_pa"], gw["w_pb"], gw["w_o"]), _weight_specs(l))),
                    outs=[(_sds((s, D)), r), (_sds((s, D)), r), (_sds((s, D), _MXU), r), (_sds((s, D)), r),
                          (_sds((s, D)), r)])


def _loss_head(y, target):
    s = y.shape[0]
    tb = 512

    def body(y_ref, t_ref, dy_ref, acc_ref):
        i = pl.program_id(0)

        @pl.when(i == 0)
        def _():
            acc_ref[...] = jnp.zeros_like(acc_ref)

        err = y_ref[...] - t_ref[...]
        dy_ref[...] = err * (1.0 / D)
        acc_ref[...] += jnp.sum(err * err, axis=0, keepdims=True)

    return _rowwise("loss_head", body, grid=s // tb,
                    ins=[(y, _rows(tb, D)), (target, _rows(tb, D))],
                    outs=[(_sds((s, D)), _rows(tb, D)), (_sds((1, D)), _vec((1, D)))])


def _accumulate(i, ref, val):
    @pl.when(i == 0)
    def _():
        ref[...] = val

    @pl.when(i > 0)
    def _():
        ref[...] += val


def _tail_bwd(l, dx, out, y_a, y_b, proj, o, h_lru, gate, g_post, gw):
    s = dx.shape[0]
    tb = 256

    def body(dx_ref, out_ref, ya_ref, yb_ref, ma_ref, mb_ref, o_ref, ga_ref, h_ref, gl_ref, gt_ref, gp_ref,
             wpa_ref, wpb_ref, wo_ref,
             dout_ref, dya_ref, dyb_ref, dga_ref, rest_ref, do_ref, dh_ref, dgt_ref, dgp_ref):
        i = pl.program_id(0)
        ov = out_ref[...]
        dxv = dx_ref[...]
        rstd = lax.rsqrt(jnp.mean(ov * ov, axis=-1, keepdims=True) + NORM_EPS)
        nv = ov * rstd
        s_dn = jnp.sum(dxv * nv, axis=0, keepdims=True)
        _accumulate(i, dgt_ref, s_dn * gp_ref[...])
        _accumulate(i, dgp_ref, s_dn * gt_ref[...])
        dn = dxv * (gt_ref[...] * gp_ref[...])
        d_out = (rstd * (dn - nv * jnp.mean(dn * nv, axis=-1, keepdims=True))).astype(_MXU)
        dout_ref[...] = d_out
        dz = lax.dot_general(d_out, wo_ref[...].reshape(D, D), _NT, preferred_element_type=_F32)
        ga = _sigmoid(ma_ref[...])
        gb = _sigmoid(mb_ref[...])
        dya = (dz * ga).astype(_MXU)
        dyb = (dz * gb).astype(_MXU)
        dya_ref[...] = dya
        dyb_ref[...] = dyb
        rest_ref[:, D:2 * D] = (dz * ya_ref[...] * ga * (1.0 - ga)).astype(rest_ref.dtype)
        rest_ref[:, 2 * D:3 * D] = (dz * yb_ref[...] * gb * (1.0 - gb)).astype(rest_ref.dtype)
        daa = lax.dot_general(dya[:, 0:256], wpa_ref[0], _NT, preferred_element_type=_F32)
        for j in range(1, N_CHIPS):
            daa = daa + lax.dot_general(dya[:, j * 256:(j + 1) * 256], wpa_ref[j], _NT, preferred_element_type=_F32)
        dba = lax.dot_general(dyb, wpb_ref[...].reshape(D, D), _NT, preferred_element_type=_F32)
        gav = ga_ref[...]
        sa = _sigmoid(gav)
        do_ref[...] = daa * (gav * sa)
        dga_ref[...] = (daa * o_ref[...] * (sa * (1.0 + gav * (1.0 - sa)))).astype(dga_ref.dtype)
        gl = gl_ref[...]
        sl = _sigmoid(gl)
        dh_ref[...] = dba * (gl * sl)
        rest_ref[:, 0:D] = (dba * h_ref[...] * (sl * (1.0 + gl * (1.0 - sl)))).astype(rest_ref.dtype)

    v = _vec((1, D))
    r5, r10 = _rows(tb, ATT_W), _rows(tb, D)
    return _rowwise("tail_bwd", body, grid=s // tb,
                    ins=[(dx, r10), (out, r10), (y_a, r10), (y_b, r10), (proj, _rows(tb, D, CB_MA)),
                         (proj, _rows(tb, D, CB_MB)), (o, r5), (proj, _rows(tb, ATT_W, CB_GATT)), (h_lru, r10),
                         (proj, _rows(tb, D, CB_GLRU)), (gate, v), (g_post, v)]
                    + list(zip((gw["w_pa"], gw["w_pb"], gw["w_o"]), _weight_specs(l))),
                    outs=[(_sds((s, D), _MXU), r10), (_sds((s, D), _MXU), r10), (_sds((s, D), _MXU), r10),
                          (_sds((s, ATT_W), _MXU), r5), (_sds((s, 3 * D), _MXU), _rows(tb, 3 * D)),
                          (_sds((s, ATT_W)), r5), (_sds((s, D)), r10), (_sds((1, D)), v), (_sds((1, D)), v)])


def _scan_bwd(dh, pre, uc, h_lru, b_rg, b_ig, lam):
    s = uc.shape[0]
    tb = 256
    n = s // tb

    def body(dh_ref, pr_ref, pi_ref, uc_ref, h_ref, hp_ref, brg_ref, big_ref, lam_ref,
             dpre_ref, duc_ref, dbrg_ref, dbig_ref, dlam_ref, carry):
        i = pl.program_id(0)

        @pl.when(i == 0)
        def _():
            carry[...] = jnp.zeros_like(carry)

        ucv = uc_ref[...]
        r, ig, sp, a, sq = _lru_gates(pr_ref[...], pi_ref[...], ucv, brg_ref[...], big_ref[...], lam_ref[...])
        row = lax.broadcasted_iota(jnp.int32, (tb, 1), 0)
        cv = jnp.where(row == tb - 1, 1.0, pltpu.roll(a, tb - 1, 0))
        gv = dh_ref[...]
        sh = 1
        while sh < tb:
            m = row < tb - sh
            g_sh = pltpu.roll(gv, tb - sh, 0)
            c_sh = pltpu.roll(cv, tb - sh, 0)
            gv = jnp.where(m, gv + cv * g_sh, gv)
            cv = jnp.where(m, cv * c_sh, cv)
            sh *= 2
        gv = gv + cv * carry[0:1, :]
        carry[...] = (a * gv)[0:8]

        halo = jnp.where(i < n - 1, hp_ref[...], 0.0)
        h_prev = _shift_down(h_ref[...], halo, 1, tb)
        d_a = gv * h_prev
        d_sq = gv * (ig * ucv)
        d_i = gv * sq * ucv
        duc_ref[...] = gv * sq * ig
        d_la = d_a * a - d_sq * (a * a) / sq
        d_r = d_la * (-LRU_C * sp)
        d_pre_r = d_r * r * (1.0 - r)
        d_pre_i = d_i * ig * (1.0 - ig)
        dpre_ref[:, 0:D] = d_pre_r.astype(dpre_ref.dtype)
        dpre_ref[:, D:2 * D] = d_pre_i.astype(dpre_ref.dtype)
        _accumulate(i, dbrg_ref, jnp.sum(d_pre_r, axis=0, keepdims=True))
        _accumulate(i, dbig_ref, jnp.sum(d_pre_i, axis=0, keepdims=True))
        lamv = lam_ref[...]
        _accumulate(i, dlam_ref, jnp.sum(d_la * (-LRU_C * r), axis=0, keepdims=True) * (-_sigmoid(-lamv)))

    v = _vec((1, D))
    rv = _rows(tb, D, 0, n)
    return _rowwise("scan_bwd", body, grid=n,
                    ins=[(dh, rv), (pre, _rows(tb, D, 0, n)), (pre, _rows(tb, D, 1, n)), (uc, rv), (h_lru, rv),
                         (h_lru, _halo_prev(tb, D, 0, n)), (b_rg, v), (b_ig, v), (lam, v)],
                    outs=[(_sds((s, 2 * D), _MXU), _rows(tb, 2 * D, 0, n)), (_sds((s, D)), rv),
                          (_sds((1, D)), v), (_sds((1, D)), v), (_sds((1, D)), v)],
                    scratch=[pltpu.VMEM((8, D), _F32)])


def _conv_bwd(duc_a, duc_b, proj, conv_w):
    s = duc_a.shape[0]
    tb = 512
    n = s // tb

    def body(da_ref, db_ref, dan_ref, dbn_ref, u_ref, up_ref, w_ref, du_ref, dw_ref, dbias_ref):
        i = pl.program_id(0)
        duc = da_ref[...] + db_ref[...]
        nxt = jnp.where(i < n - 1, dan_ref[...] + dbn_ref[...], 0.0)
        u = u_ref[...]
        halo = jnp.where(i > 0, up_ref[...], 0.0)
        du = duc * w_ref[0:1, :]
        dws = [jnp.sum(duc * u, axis=0, keepdims=True)]
        for j in range(1, 4):
            du = du + _shift_up(duc, nxt, j, tb) * w_ref[j:j + 1, :]
            dws.append(jnp.sum(duc * _shift_down(u, halo, j, tb), axis=0, keepdims=True))
        du_ref[...] = du.astype(du_ref.dtype)
        for j in range(4):
            _accumulate(i, dw_ref.at[j:j + 1, :], dws[j])
        _accumulate(i, dbias_ref, jnp.sum(duc, axis=0, keepdims=True))

    r = _rows(tb, D)
    return _rowwise("conv_bwd", body, grid=n,
                    ins=[(duc_a, r), (duc_b, r), (duc_a, _halo_next(tb, D, n)), (duc_b, _halo_next(tb, D, n)),
                         (proj, _rows(tb, D, CB_U)), (proj, _halo_prev(tb, D, CB_U)), (conv_w, _vec((4, D)))],
                    outs=[(_sds((s, D), _MXU), r), (_sds((4, D)), _vec((4, D))), (_sds((1, D)), _vec((1, D)))])


def _prenorm_bwd(dh, x, dx_out, g_pre, scale):
    s = x.shape[0]
    tb = 512

    def body(dh_ref, x_ref, dxo_ref, g_ref, sc_ref, dx_ref, dsh_ref, dsc_ref, dg_ref):
        i = pl.program_id(0)
        xv = x_ref[...]
        dhv = dh_ref[...]
        rstd = lax.rsqrt(jnp.mean(xv * xv, axis=-1, keepdims=True) + NORM_EPS)
        xn = xv * rstd
        one_sc = 1.0 + sc_ref[...]
        s1 = jnp.sum(dhv * xn, axis=0, keepdims=True)
        _accumulate(i, dsh_ref, jnp.sum(dhv, axis=0, keepdims=True))
        _accumulate(i, dsc_ref, s1 * g_ref[...])
        _accumulate(i, dg_ref, s1 * one_sc)
        dxn = dhv * (g_ref[...] * one_sc)
        dx_ref[...] = dxo_ref[...] + rstd * (dxn - xn * jnp.mean(dxn * xn, axis=-1, keepdims=True))

    v = _vec((1, D))
    r = _rows(tb, D)
    return _rowwise("prenorm_bwd", body, grid=s // tb,
                    ins=[(dh, r), (x, r), (dx_out, r), (g_pre, v), (scale, v)],
                    outs=[(_sds((s, D)), r), (_sds((1, D)), v), (_sds((1, D)), v), (_sds((1, D)), v)])


def _band_tiles(dil):
    tiles = []
    for rho in range(dil):
        for b in range(16 // dil):
            qs = rho + dil * BAND * b
            tiles.append((qs, QBLK + qs - dil * BAND, b))
    return tiles


def _strided(start, size, dil):
    return pl.ds(start, size, stride=dil) if dil > 1 else pl.ds(start, size)


def _band_mask(i, b):
    qi = lax.broadcasted_iota(jnp.int32, (BAND, 2 * BAND), 0)
    ki = lax.broadcasted_iota(jnp.int32, (BAND, 2 * BAND), 1)
    valid = (ki >= qi) & (ki <= qi + BAND)
    if b == 0:
        valid = valid & ((ki >= BAND) | (i > 0))
    return valid


def _attn_fwd(proj, g):
    s = proj.shape[0]
    dil = DILATIONS[g]
    n = s // QBLK
    scale = HEAD ** -0.5
    tiles = _band_tiles(dil)

    def body(q_ref, kp_ref, kc_ref, vp_ref, vc_ref, o_ref, m_ref, l_ref, kbuf, vbuf):
        i = pl.program_id(1)
        kbuf[0:QBLK, :] = kp_ref[...]
        kbuf[QBLK:2 * QBLK, :] = kc_ref[...]
        vbuf[0:QBLK, :] = vp_ref[...]
        vbuf[QBLK:2 * QBLK, :] = vc_ref[...]
        for qs, ks, b in tiles:
            q = q_ref[_strided(qs, BAND, dil), :].astype(_MXU)
            kk = kbuf[_strided(ks, 2 * BAND, dil), :].astype(_MXU)
            vv = vbuf[_strided(ks, 2 * BAND, dil), :].astype(_MXU)
            sc = lax.dot_general(q, kk, _NT, preferred_element_type=_F32) * scale
            sc = jnp.where(_band_mask(i, b), sc, NEG_INF)
            m = jnp.max(sc, axis=-1, keepdims=True)
            p = jnp.exp(sc - m)
            l = jnp.sum(p, axis=-1, keepdims=True)
            o = jnp.dot(p.astype(_MXU), vv, preferred_element_type=_F32) / l
            o_ref[_strided(qs, BAND, dil), :] = o
            m_ref[_strided(qs, BAND, dil), :] = jnp.broadcast_to(m, (BAND, HEAD))
            l_ref[_strided(qs, BAND, dil), :] = jnp.broadcast_to(l, (BAND, HEAD))

    blk = (QBLK, HEAD)
    cq, ck, cv = g * HEADS, 12 + g * HEADS, 24 + g * HEADS
    out_spec = pl.BlockSpec(blk, lambda j, i: (i, j))
    return pl.pallas_call(
        body, name="attn_fwd_d%d" % dil, grid=(HEADS, n),
        in_specs=[pl.BlockSpec(blk, lambda j, i: (i, cq + j)),
                  pl.BlockSpec(blk, lambda j, i: (jnp.maximum(i - 1, 0), ck + j)),
                  pl.BlockSpec(blk, lambda j, i: (i, ck + j)),
                  pl.BlockSpec(blk, lambda j, i: (jnp.maximum(i - 1, 0), cv + j)),
                  pl.BlockSpec(blk, lambda j, i: (i, cv + j))],
        out_specs=[out_spec] * 3, out_shape=[_sds((s, ATT_W))] * 3,
        scratch_shapes=[pltpu.VMEM((2 * QBLK, HEAD), _F32)] * 2,
        compiler_params=_params(2))(proj, proj, proj, proj, proj)


def _attn_bwd(proj, d_o, o, lse, g):
    s = proj.shape[0]
    dil = DILATIONS[g]
    n = s // QBLK
    scale = HEAD ** -0.5
    tiles = _band_tiles(dil)

    def body(q_ref, kp_ref, kc_ref, vp_ref, vc_ref, do_ref, o_ref, lse_ref, dq_ref, dk_ref, dv_ref,
             kbuf, vbuf, dkbuf, dvbuf, dqbuf):
        i = pl.program_id(1)

        @pl.when(i == 0)
        def _():
            dkbuf[0:QBLK, :] = jnp.zeros((QBLK, HEAD), _F32)
            dvbuf[0:QBLK, :] = jnp.zeros((QBLK, HEAD), _F32)

        @pl.when(i < n)
        def _():
            kbuf[0:QBLK, :] = kp_ref[...]
            kbuf[QBLK:2 * QBLK, :] = kc_ref[...]
            vbuf[0:QBLK, :] = vp_ref[...]
            vbuf[QBLK:2 * QBLK, :] = vc_ref[...]
            dkbuf[QBLK:2 * QBLK, :] = jnp.zeros((QBLK, HEAD), _F32)
            dvbuf[QBLK:2 * QBLK, :] = jnp.zeros((QBLK, HEAD), _F32)
            for qs, ks, b in tiles:
                qsl = _strided(qs, BAND, dil)
                ksl = _strided(ks, 2 * BAND, dil)
                q = q_ref[qsl, :].astype(_MXU)
                kk = kbuf[ksl, :].astype(_MXU)
                vv = vbuf[ksl, :].astype(_MXU)
                dov = do_ref[qsl, :]
                dd = jnp.sum(dov * o_ref[qsl, :], axis=-1, keepdims=True)
                lse_t = lse_ref[qsl, :][:, 0:1]
                sc = lax.dot_general(q, kk, _NT, preferred_element_type=_F32) * scale
                p = jnp.where(_band_mask(i, b), jnp.exp(sc - lse_t), 0.0)
                dob = dov.astype(_MXU)
                dp = lax.dot_general(dob, vv, _NT, preferred_element_type=_F32)
                ds = (p * (dp - dd) * scale).astype(_MXU)
                dqbuf[qsl, :] = jnp.dot(ds, kk, preferred_element_type=_F32)
                dkbuf[ksl, :] += lax.dot_general(ds, q, _TN, preferred_element_type=_F32)
                dvbuf[ksl, :] += lax.dot_general(p.astype(_MXU), dob, _TN, preferred_element_type=_F32)
            dq_ref[...] = dqbuf[...].astype(dq_ref.dtype)

        dk_ref[...] = dkbuf[0:QBLK, :].astype(dk_ref.dtype)
        dv_ref[...] = dvbuf[0:QBLK, :].astype(dv_ref.dtype)
        dkbuf[0:QBLK, :] = dkbuf[QBLK:2 * QBLK, :]
        dvbuf[0:QBLK, :] = dvbuf[QBLK:2 * QBLK, :]

    blk = (QBLK, HEAD)
    cq, ck, cv = g * HEADS, 12 + g * HEADS, 24 + g * HEADS

    def cur(i):
        return jnp.minimum(i, n - 1)

    def prev(i):
        return jnp.maximum(jnp.minimum(i, n - 1) - 1, 0)

    own = pl.BlockSpec(blk, lambda j, i: (cur(i), j))
    late = pl.BlockSpec(blk, lambda j, i: (jnp.maximum(i - 1, 0), j))
    return pl.pallas_call(
        body, name="attn_bwd_d%d" % dil, grid=(HEADS, n + 1),
        in_specs=[pl.BlockSpec(blk, lambda j, i: (cur(i), cq + j)),
                  pl.BlockSpec(blk, lambda j, i: (prev(i), ck + j)),
                  pl.BlockSpec(blk, lambda j, i: (cur(i), ck + j)),
                  pl.BlockSpec(blk, lambda j, i: (prev(i), cv + j)),
                  pl.BlockSpec(blk, lambda j, i: (cur(i), cv + j)),
                  own, own, own],
        out_specs=[own, late, late], out_shape=[_sds((s, ATT_W), _MXU)] * 3,
        scratch_shapes=[pltpu.VMEM((2 * QBLK, HEAD), _F32)] * 4 + [pltpu.VMEM((QBLK, HEAD), _F32)],
        compiler_params=_params(2))(proj, proj, proj, proj, proj, d_o, o, lse)


def _block_diag(w):
    eye = jnp.eye(16, dtype=w.dtype)
    return jnp.einsum("hij,hg->higj", w, eye).reshape(D, D)


def _diag_blocks(gd):
    g4 = gd.reshape(16, 64, 16, 64)
    keep = jnp.eye(16, dtype=jnp.bool_)[:, None, :, None]
    return jnp.sum(jnp.where(keep, g4, 0.0), axis=2)


def _layer_fwd(l, x, p, gw):
    s = x.shape[0]
    nm = s // 1024
    h, h_t = _prenorm_fwd(x, p["g_pre"], p["shift"], p["scale"])
    proj = _mm("proj", h, gw["w_in"], _sds((s, IN_W)), grid=(nm, N_CHIPS, 1),
               a_spec=pl.BlockSpec((1024, D), lambda m, n, k: (m, 0)),
               b_spec=pl.BlockSpec((None, None, D, 2304), lambda m, n, k: (n, l, 0, 0)),
               o_spec=pl.BlockSpec((1024, 2304), lambda m, n, k: (m, n)), dims=_NN, acc_shape=(1024, 2304))
    og, mg, lg = [], [], []
    for g in range(3):
        o_g, m_g, l_g = _attn_fwd(proj, g)
        og.append(o_g)
        mg.append(m_g)
        lg.append(l_g)
    uc = _conv_fwd(proj, p["conv_w"], p["conv_b"])
    pre = _mm("lru_gates", uc, p["w_gates"], _sds((s, 2 * D)), grid=(nm, 1, 1),
              a_spec=pl.BlockSpec((1024, D), lambda m, n, k: (m, 0)),
              b_spec=pl.BlockSpec((D, 2 * D), lambda m, n, k: (0, 0)),
              o_spec=pl.BlockSpec((1024, 2 * D), lambda m, n, k: (m, 0)), dims=_NN, acc_shape=(1024, 2 * D))
    h_lru = _scan_fwd(pre, uc, p["b_rg"], p["b_ig"], p["lam"])
    o, lse, a_att, b_act = _gating_fwd(og, mg, lg, proj, h_lru)
    y_a, y_b, z, out, x_new = _tail_fwd(l, a_att, b_act, proj, x, p["gate"], p["g_post"], gw)
    saved = dict(x=x, h_t=h_t, proj=proj, o=o, lse=lse, uc=uc, pre=pre, h_lru=h_lru, a_att=a_att, b_act=b_act,
                 y_a=y_a, y_b=y_b, z=z, out=out)
    return x_new, saved


def _layer_bwd(l, dx, p, gw, sv, big):
    s = dx.shape[0]
    nm = s // 1024
    nt = s // 2048
    proj = sv["proj"]
    d_out, dy_a, dy_b, d_gatt, d_rest, d_o, dh_lru, d_gate, d_gpost = _tail_bwd(
        l, dx, sv["out"], sv["y_a"], sv["y_b"], proj, sv["o"], sv["h_lru"], p["gate"], p["g_post"], gw)

    def wgrad_rows(name, a, b, into):
        return _mm(name, a, b, _sds((2, N_CHIPS, 256, D)), grid=(4, 1, nt),
                   a_spec=pl.BlockSpec((2048, 256), lambda m, n, k: (k, m)),
                   b_spec=pl.BlockSpec((2048, D), lambda m, n, k: (k, 0)),
                   o_spec=pl.BlockSpec((None, None, 256, D), lambda m, n, k: (l, m, 0, 0)),
                   dims=_TN, acc_shape=(256, D), into=into)

    big = dict(big)
    big["w_o"] = wgrad_rows("g_w_o", sv["z"], d_out, big.get("w_o"))
    big["w_pa"] = _mm("g_w_pa", sv["a_att"], dy_a, _sds((2, N_CHIPS, ATT_W, 256)), grid=(1, 4, nt),
                      a_spec=pl.BlockSpec((2048, ATT_W), lambda m, n, k: (k, 0)),
                      b_spec=pl.BlockSpec((2048, 256), lambda m, n, k: (k, n)),
                      o_spec=pl.BlockSpec((None, None, ATT_W, 256), lambda m, n, k: (l, n, 0, 0)),
                      dims=_TN, acc_shape=(ATT_W, 256), into=big.get("w_pa"))
    big["w_pb"] = wgrad_rows("g_w_pb", sv["b_act"], dy_b, big.get("w_pb"))
    d_pre, duc_dir, d_brg, d_big, d_lam = _scan_bwd(dh_lru, sv["pre"], sv["uc"], sv["h_lru"],
                                                   p["b_rg"], p["b_ig"], p["lam"])
    duc_mm = _mm("d_uc", d_pre, p["w_gates"], _sds((s, D)), grid=(nm, 1, 1),
                 a_spec=pl.BlockSpec((1024, 2 * D), lambda m, n, k: (m, 0)),
                 b_spec=pl.BlockSpec((D, 2 * D), lambda m, n, k: (0, 0)),
                 o_spec=pl.BlockSpec((1024, D), lambda m, n, k: (m, 0)), dims=_NT, acc_shape=(1024, D))
    g_gates = _mm("g_w_gates", sv["uc"], d_pre, _sds((D, 2 * D)), grid=(1, 2, s // 1024),
                  a_spec=pl.BlockSpec((1024, D), lambda m, n, k: (k, 0)),
                  b_spec=pl.BlockSpec((1024, D), lambda m, n, k: (k, n)),
                  o_spec=pl.BlockSpec((D, D), lambda m, n, k: (0, n)), dims=_TN, acc_shape=(D, D))
    d_u, g_convw, g_convb = _conv_bwd(duc_dir, duc_mm, proj, p["conv_w"])
    dqkv = [_attn_bwd(proj, d_o, sv["o"], sv["lse"], g) for g in range(3)]
    dproj = jnp.concatenate([dqkv[g][t] for t in range(3) for g in range(3)] + [d_gatt, d_u, d_rest], axis=1)
    dh = _mm("d_h", dproj, gw["w_in"], _sds((s, D)), grid=(nm, 1, N_CHIPS),
             a_spec=pl.BlockSpec((1024, 2304), lambda m, n, k: (m, k)),
             b_spec=pl.BlockSpec((None, None, D, 2304), lambda m, n, k: (k, l, 0, 0)),
             o_spec=pl.BlockSpec((1024, D), lambda m, n, k: (m, 0)), dims=_NT, acc_shape=(1024, D))
    big["w_in"] = _mm("g_w_in", sv["h_t"], dproj, _sds((2, N_CHIPS, D, 2304)), grid=(1, N_CHIPS, s // 1024),
                      a_spec=pl.BlockSpec((D, 1024), lambda m, n, k: (0, k)),
                      b_spec=pl.BlockSpec((1024, 2304), lambda m, n, k: (k, n)),
                      o_spec=pl.BlockSpec((None, None, D, 2304), lambda m, n, k: (l, n, 0, 0)),
                      dims=_NN, acc_shape=(D, 2304), into=big.get("w_in"))
    dx_in, d_shift, d_scale, d_gpre = _prenorm_bwd(dh, sv["x"], dx, p["g_pre"], p["scale"])
    small = dict(dmod=jnp.concatenate([d_shift, d_scale, d_gate], axis=1), g_pre=d_gpre, conv_w=g_convw,
                 conv_b=g_convb, w_rg=_diag_blocks(g_gates[:, 0:D]), b_rg=d_brg,
                 w_ig=_diag_blocks(g_gates[:, D:2 * D]), b_ig=d_big, lam=d_lam, g_post=d_gpost)
    return dx_in, small, big


def _local_step(x, target, small_p, gw):
    saved = []
    h = x
    for l in range(2):
        h, sv = _layer_fwd(l, h, small_p[l], gw)
        saved.append(sv)
    dy, sq = _loss_head(h, target)
    loss = 0.5 * jnp.sum(sq) / D
    big = {}
    smalls = [None, None]
    dx = dy
    for l in (1, 0):
        dx, smalls[l], big = _layer_bwd(l, dx, small_p[l], gw, saved[l], big)
    return loss, dx, smalls, big


_SMALL_ROWS = 8 + 16 + 8 + 128 + 128


def _pack_small(smalls):
    dmod = jnp.concatenate([smalls[0]["dmod"].reshape(3, D), smalls[1]["dmod"].reshape(3, D),
                            jnp.zeros((2, D), _F32)], axis=0)
    vecs = jnp.concatenate([smalls[l][k] for k in ("g_pre", "conv_b", "b_rg", "b_ig", "lam", "g_post")
                            for l in range(2)] + [jnp.zeros((4, D), _F32)], axis=0)
    convw = jnp.concatenate([smalls[0]["conv_w"], smalls[1]["conv_w"]], axis=0)
    wrg = jnp.stack([smalls[0]["w_rg"], smalls[1]["w_rg"]]).reshape(128, D)
    wig = jnp.stack([smalls[0]["w_ig"], smalls[1]["w_ig"]]).reshape(128, D)
    return jnp.concatenate([dmod, vecs, convw, wrg, wig], axis=0)


def kernel(x, c, w_mod, b_mod, g_pre, w_in, conv_w, conv_b, w_rg, b_rg, w_ig, b_ig, lru_lambda, w_pa, w_pb, w_o, g_post, loss_target, m_w_mod, m_b_mod, m_g_pre, m_w_in, m_conv_w, m_conv_b, m_w_rg, m_b_rg, m_w_ig, m_b_ig, m_lru_lambda, m_w_pa, m_w_pb, m_w_o, m_g_post, v_w_mod, v_b_mod, v_g_pre, v_w_in, v_conv_w, v_conv_b, v_w_rg, v_b_rg, v_w_ig, v_b_ig, v_lru_lambda, v_w_pa, v_w_pb, v_w_o, v_g_post):
    xi, yi, ci = lax.axis_index("x"), lax.axis_index("y"), lax.axis_index("c")
    chip = 2 * xi + yi
    dev = 4 * xi + 2 * yi + ci
    mcols = w_mod.shape[2]

    pack1 = jnp.concatenate([jnp.broadcast_to(c, (8, D)),
                             jnp.pad(conv_w.reshape(8, 256), ((0, 0), (0, D - 256)))], axis=0)
    g1 = _exchange("gather_cond", [pack1], "xyc", False)[0]
    c_all = g1[:, 0, :]
    conv_w_full = jnp.transpose(g1[0::2, 8:16, 0:256], (1, 0, 2)).reshape(2, 4, D)

    b_cols = lax.dynamic_slice(b_mod, (0, chip * mcols), (2, mcols)).reshape(2, 1, mcols)
    mod_loc = _mod_fwd(c_all, w_mod, b_cols)
    g2 = _exchange("gather_mod", [mod_loc.reshape(16, mcols)], "xyc", False)[0]
    mod_full = jnp.transpose(g2[0::2], (1, 0, 2)).reshape(2, 8, 3 * D)
    mod_me = lax.dynamic_index_in_dim(mod_full, dev, axis=1, keepdims=False)

    wb = [_cast("cast_w_in", w_in.reshape(2 * D, 2304), 256).reshape(2, D, 2304),
          _cast("cast_w_pa", w_pa.reshape(2 * ATT_W, 256), 256).reshape(2, ATT_W, 256),
          _cast("cast_w_pb", w_pb.reshape(512, D), 256).reshape(2, 256, D),
          _cast("cast_w_o", w_o.reshape(512, D), 256).reshape(2, 256, D)]
    gl = _gather_weights(wb, [4, 1, 1, 1])
    gw = dict(w_in=gl[0], w_pa=gl[1], w_pb=gl[2], w_o=gl[3])

    small_p = []
    for l in range(2):
        gates = jnp.concatenate([_block_diag(w_rg[l]), _block_diag(w_ig[l])], axis=1).astype(_MXU)
        small_p.append(dict(
            shift=mod_me[l:l + 1, 0:D], scale=mod_me[l:l + 1, D:2 * D], gate=mod_me[l:l + 1, 2 * D:3 * D],
            g_pre=g_pre[l:l + 1], conv_w=conv_w_full[l], conv_b=conv_b[l:l + 1], w_gates=gates,
            b_rg=b_rg[l:l + 1], b_ig=b_ig[l:l + 1], lam=lru_lambda[l:l + 1], g_post=g_post[l:l + 1]))

    loss_loc, dx, smalls, big = _local_step(x[0], loss_target[0], small_p, gw)
    loss = lax.psum(loss_loc, ("x", "y", "c"))
    grad_x = dx[None]

    names = ("w_in", "w_pa", "w_pb", "w_o")
    core = jnp.reshape(ci, (1,)).astype(jnp.int32)
    where = jnp.stack([chip, ci]).astype(jnp.int32)
    pair = list(_exchange("reduce_pair", [big["w_in"].reshape(2, 16, 256, 2304)] + [big[k] for k in names[1:]],
                          "c", True, local=False, nchunks=[16, 4, 4, 4]))
    pair[0] = pair[0].reshape(N_CHIPS, D, 2304)
    t1 = [_sum_pair("sum_pair_" + k, big[k], r, core, 128) for k, r in zip(names, pair)]
    quad = _exchange("reduce_chips", [t[1] for t in t1], "xy", True, local=False, nchunks=[4, 1, 1, 1])
    t3 = [_sum_chips("sum_chips_" + k, t[0], r, where, 128) for k, t, r in zip(names, t1, quad)]
    both = _pair_fill("gather_layers", t3, [4, 1, 1, 1])
    g_big = dict(zip(names, both))

    g3 = _exchange("gather_small", [_pack_small(smalls)], "xyc", False)[0]
    tot = _sum_lead("sum_small", g3, 96)
    dmod_all = g3[:, 0:6, :].reshape(8, 2, 3 * D)
    dm_cols = jnp.transpose(lax.dynamic_slice(dmod_all, (0, 0, chip * mcols), (8, 2, mcols)), (1, 0, 2))
    g_w_mod = _mod_bwd(jnp.transpose(c_all), dm_cols)
    vec = tot[8:20].reshape(6, 2, D)
    g_conv_w_full = tot[24:32].reshape(2, 4, D)
    grads = dict(
        w_mod=g_w_mod, b_mod=tot[0:6].reshape(2, 3 * D), g_pre=vec[0], w_in=g_big["w_in"],
        conv_w=lax.dynamic_slice(g_conv_w_full, (0, 0, chip * 256), (2, 4, 256)), conv_b=vec[1],
        w_rg=tot[32:160].reshape(2, 16, 64, 64), b_rg=vec[2], w_ig=tot[160:288].reshape(2, 16, 64, 64),
        b_ig=vec[3], lru_lambda=vec[4], w_pa=g_big["w_pa"], w_pb=g_big["w_pb"], w_o=g_big["w_o"],
        g_post=vec[5])

    weights = dict(w_mod=w_mod, b_mod=b_mod, g_pre=g_pre, w_in=w_in, conv_w=conv_w, conv_b=conv_b, w_rg=w_rg,
                   b_rg=b_rg, w_ig=w_ig, b_ig=b_ig, lru_lambda=lru_lambda, w_pa=w_pa, w_pb=w_pb, w_o=w_o,
                   g_post=g_post)
    ms = dict(w_mod=m_w_mod, b_mod=m_b_mod, g_pre=m_g_pre, w_in=m_w_in, conv_w=m_conv_w, conv_b=m_conv_b,
              w_rg=m_w_rg, b_rg=m_b_rg, w_ig=m_w_ig, b_ig=m_b_ig, lru_lambda=m_lru_lambda, w_pa=m_w_pa,
              w_pb=m_w_pb, w_o=m_w_o, g_post=m_g_post)
    vs = dict(w_mod=v_w_mod, b_mod=v_b_mod, g_pre=v_g_pre, w_in=v_w_in, conv_w=v_conv_w, conv_b=v_conv_b,
              w_rg=v_w_rg, b_rg=v_b_rg, w_ig=v_w_ig, b_ig=v_b_ig, lru_lambda=v_lru_lambda, w_pa=v_w_pa,
              w_pb=v_w_pb, w_o=v_w_o, g_post=v_g_post)
    flat = dict(w_mod=(2 * D, mcols, 256), b_mod=(2, 3 * D, 2), g_pre=(2, D, 2), w_in=(2 * D, 2304, 256),
                conv_w=(8, 256, 8), conv_b=(2, D, 2), w_rg=(128, D, 128), b_rg=(2, D, 2), w_ig=(128, D, 128),
                b_ig=(2, D, 2), lru_lambda=(2, D, 2), w_pa=(2 * ATT_W, 256, 256), w_pb=(512, D, 256),
                w_o=(512, D, 256), g_post=(2, D, 2))
    order = ("w_mod", "b_mod", "g_pre", "w_in", "conv_w", "conv_b", "w_rg", "b_rg", "w_ig", "b_ig",
             "lru_lambda", "w_pa", "w_pb", "w_o", "g_post")
    deltas, new_m, new_v = [], [], []
    for k in order:
        rows, cols, tb = flat[k]
        shp = weights[k].shape
        d, nm_, nv_ = _adamw("adamw_" + k, weights[k].reshape(rows, cols), grads[k].reshape(rows, cols),
                             ms[k].reshape(rows, cols), vs[k].reshape(rows, cols), tb)
        deltas.append(d.reshape(shp))
        new_m.append(nm_.reshape(shp))
        new_v.append(nv_.reshape(shp))
    return (loss, grad_x, *[grads[k].reshape(weights[k].shape) for k in order], *deltas, *new_m, *new_v)
```

```python
import functools

import jax
import jax.numpy as jnp
from jax import lax
from jax.experimental import pallas as pl
from jax.experimental.pallas import tpu as pltpu

_F32 = jnp.float32
_MXU = jnp.bfloat16
_VMEM_LIMIT = 56 * 1024 * 1024
_MESH = pl.DeviceIdType.MESH

D = 1024
HEAD = 128
HEADS = 4
ATT_W = 512
QKV_W = 1536
IN_W = 9216
DILATIONS = (1, 4, 16)
BAND = 128
QBLK = BAND * 16
NORM_EPS = 1e-6
NEG_INF = -1e30
LRU_C = 8.0
N_CHIPS = 4
CB_GATT = 4608 // 512
CB_U, CB_GLRU, CB_MA, CB_MB = 5, 6, 7, 8

ADAM_LR, ADAM_B1, ADAM_B2, ADAM_EPS, ADAM_WD, ADAM_STEP = 0.001, 0.9, 0.999, 1e-08, 0.01, 10


def _params(ngrid):
    return pltpu.CompilerParams(dimension_semantics=("arbitrary",) * ngrid, vmem_limit_bytes=_VMEM_LIMIT)


def _sigmoid(v):
    return 1.0 / (1.0 + jnp.exp(-v))


_GROUPS = {
    "c": [(0, 0, 1)],
    "xy": [(1, 0, 0), (0, 1, 0), (1, 1, 0)],
    "xyc": [(0, 0, 1), (0, 1, 0), (0, 1, 1), (1, 0, 0), (1, 0, 1), (1, 1, 0), (1, 1, 1)],
}


def _rank(group, px, py, pc):
    if group == "c":
        return pc
    if group == "xy":
        return 2 * px + py
    return 4 * px + 2 * py + pc


def _flip(rel, x, y, c):
    dx, dy, dc = rel
    return (1 - x if dx else x, 1 - y if dy else y, 1 - c if dc else c)


def _pieces(ref, nchunk):
    step = ref.shape[0] // nchunk
    return [ref.at[pl.ds(q * step, step)] for q in range(nchunk)]


def _exchange(name, srcs, group, scatter, *, local=True, nchunks=None):
    rels = _GROUPS[group]
    gsize = len(rels) + 1
    n = len(srcs)
    nchunks = nchunks or [1] * n
    blks = [s.shape[1:] if scatter else s.shape for s in srcs]
    slotted = local or gsize > 2
    base = [sum(nchunks[:a]) for a in range(n)]
    tot = sum(nchunks)

    def body(*refs):
        src_refs, out_refs = refs[:n], refs[n:2 * n]
        send_sems, recv_sems, loc_sems = refs[2 * n:]
        x, y, c = lax.axis_index("x"), lax.axis_index("y"), lax.axis_index("c")
        me = _rank(group, x, y, c)
        copies = []
        for a in range(n):
            def part(r, a=a):
                return src_refs[a].at[r] if scatter else src_refs[a]
            dst = out_refs[a].at[me] if slotted else out_refs[a]
            if local:
                for q, (s_, d_) in enumerate(zip(_pieces(part(me), nchunks[a]), _pieces(dst, nchunks[a]))):
                    loc = pltpu.make_async_copy(s_, d_, loc_sems.at[base[a] + q])
                    loc.start()
                    copies.append(loc)
            for k, rel in enumerate(rels):
                peer = _flip(rel, x, y, c)
                for q, (s_, d_) in enumerate(zip(_pieces(part(_rank(group, *peer)), nchunks[a]),
                                                 _pieces(dst, nchunks[a]))):
                    cp = pltpu.make_async_remote_copy(
                        src_ref=s_, dst_ref=d_, send_sem=send_sems.at[(base[a] + q) * len(rels) + k],
                        recv_sem=recv_sems.at[(base[a] + q) * len(rels) + k],
                        device_id=peer, device_id_type=_MESH)
                    cp.start()
                    copies.append(cp)
        for cp in copies:
            cp.wait()

    any_spec = pl.BlockSpec(memory_space=pl.ANY)
    lead = (gsize,) if slotted else ()
    return pl.pallas_call(
        body, name=name,
        out_shape=[jax.ShapeDtypeStruct(lead + tuple(b), s.dtype) for b, s in zip(blks, srcs)],
        in_specs=[any_spec] * n, out_specs=[any_spec] * n,
        scratch_shapes=[pltpu.SemaphoreType.DMA((tot * len(rels),)), pltpu.SemaphoreType.DMA((tot * len(rels),)),
                        pltpu.SemaphoreType.DMA((tot,))],
    )(*srcs)


def _pair_fill(name, arrs, nchunks):
    n = len(arrs)
    base = [sum(nchunks[:a]) for a in range(n)]
    tot = sum(nchunks)

    def body(*refs):
        out_refs = refs[n:2 * n]
        send_sems, recv_sems = refs[2 * n:]
        x, y, c = lax.axis_index("x"), lax.axis_index("y"), lax.axis_index("c")
        copies = []
        for a in range(n):
            for q, blk in enumerate(_pieces(out_refs[a].at[c], nchunks[a])):
                cp = pltpu.make_async_remote_copy(
                    src_ref=blk, dst_ref=blk, send_sem=send_sems.at[base[a] + q], recv_sem=recv_sems.at[base[a] + q],
                    device_id=(x, y, 1 - c), device_id_type=_MESH)
                cp.start()
                copies.append(cp)
        for cp in copies:
            cp.wait()

    any_spec = pl.BlockSpec(memory_space=pl.ANY)
    return pl.pallas_call(
        body, name=name, out_shape=[jax.ShapeDtypeStruct(a.shape, a.dtype) for a in arrs],
        in_specs=[any_spec] * n, out_specs=[any_spec] * n, input_output_aliases={a: a for a in range(n)},
        scratch_shapes=[pltpu.SemaphoreType.DMA((tot,)), pltpu.SemaphoreType.DMA((tot,))],
    )(*arrs)


def _gather_weights(wb, nchunks):
    n = len(wb)
    rels = _GROUPS["xy"]
    base = [sum(nchunks[:a]) for a in range(n)]
    tot = sum(nchunks)

    def body(*refs):
        src_refs, out_refs = refs[:n], refs[n:2 * n]
        ici_send, ici_recv, d2d_send, d2d_recv, loc_sems = refs[2 * n:]
        x, y, c = lax.axis_index("x"), lax.axis_index("y"), lax.axis_index("c")
        me = 2 * x + y
        waits = []
        for a in range(n):
            for l in range(2):
                for q, (s_, d_) in enumerate(zip(_pieces(src_refs[a].at[l], nchunks[a]),
                                                 _pieces(out_refs[a].at[me, l], nchunks[a]))):
                    loc = pltpu.make_async_copy(s_, d_, loc_sems.at[(base[a] + q) * 2 + l])
                    loc.start()
                    waits.append(loc)
        first = []
        for a in range(n):
            for k, rel in enumerate(rels):
                px, py, _ = _flip(rel, x, y, c)
                for q, (s_, d_) in enumerate(zip(_pieces(src_refs[a].at[c], nchunks[a]),
                                                 _pieces(out_refs[a].at[me, c], nchunks[a]))):
                    sem = (base[a] + q) * 3 + k
                    cp = pltpu.make_async_remote_copy(src_ref=s_, dst_ref=d_, send_sem=ici_send.at[sem],
                                                      recv_sem=ici_recv.at[sem], device_id=(px, py, c),
                                                      device_id_type=_MESH)
                    cp.start()
                    first.append(cp)
        second = []
        for a in range(n):
            for k, rel in enumerate(rels):
                px, py, _ = _flip(rel, x, y, c)
                for q, blk in enumerate(_pieces(out_refs[a].at[2 * px + py, c], nchunks[a])):
                    sem = (base[a] + q) * 3 + k
                    landed = pltpu.make_async_remote_copy(src_ref=blk, dst_ref=blk, send_sem=ici_send.at[sem],
                                                          recv_sem=ici_recv.at[sem], device_id=(px, py, c),
                                                          device_id_type=_MESH)
                    landed.wait_recv()
                    cp = pltpu.make_async_remote_copy(src_ref=blk, dst_ref=blk, send_sem=d2d_send.at[sem],
                                                      recv_sem=d2d_recv.at[sem], device_id=(x, y, 1 - c),
                                                      device_id_type=_MESH)
                    cp.start()
                    second.append(cp)
        for cp in first:
            cp.wait_send()
        for cp in second:
            cp.wait_send()
        for a in range(n):
            for k, rel in enumerate(rels):
                px, py, _ = _flip(rel, x, y, c)
                for q, blk in enumerate(_pieces(out_refs[a].at[2 * px + py, 1 - c], nchunks[a])):
                    sem = (base[a] + q) * 3 + k
                    pltpu.make_async_remote_copy(src_ref=blk, dst_ref=blk, send_sem=d2d_send.at[sem],
                                                 recv_sem=d2d_recv.at[sem], device_id=(x, y, 1 - c),
                                                 device_id_type=_MESH).wait_recv()
        for cp in waits:
            cp.wait()

    any_spec = pl.BlockSpec(memory_space=pl.ANY)
    return pl.pallas_call(
        body, name="gather_weights",
        out_shape=[jax.ShapeDtypeStruct((N_CHIPS,) + a.shape, a.dtype) for a in wb],
        in_specs=[any_spec] * n, out_specs=[any_spec] * n,
        scratch_shapes=[pltpu.SemaphoreType.DMA((tot * 3,))] * 4 + [pltpu.SemaphoreType.DMA((tot * 2,))],
    )(*wb)


def _mm(name, a, b, out_sds, *, grid, a_spec, b_spec, o_spec, dims, acc_shape, into=None):
    nk = grid[2]

    def body(*refs):
        a_ref, b_ref = refs[0], refs[1]
        o_ref, acc = refs[-2], refs[-1]
        k = pl.program_id(2)
        part = lax.dot_general(a_ref[...].astype(_MXU), b_ref[...].astype(_MXU), dims,
                               preferred_element_type=_F32)
        if nk == 1:
            o_ref[...] = part.astype(o_ref.dtype)
            return

        @pl.when(k == 0)
        def _():
            acc[...] = part

        @pl.when(k > 0)
        def _():
            acc[...] += part

        @pl.when(k == nk - 1)
        def _():
            o_ref[...] = acc[...].astype(o_ref.dtype)

    if nk == 1:
        acc_shape = (8, 128)
    in_specs = [a_spec, b_spec]
    args = [a, b]
    aliases = {}
    if into is not None:
        in_specs.append(pl.BlockSpec(memory_space=pl.ANY))
        args.append(into)
        aliases = {2: 0}
    return pl.pallas_call(
        body, name=name, grid=grid, in_specs=in_specs, out_specs=o_spec, out_shape=out_sds,
        scratch_shapes=[pltpu.VMEM(acc_shape, _F32)], input_output_aliases=aliases,
        compiler_params=_params(3))(*args)


_NN = (((1,), (0,)), ((), ()))
_NT = (((1,), (1,)), ((), ()))
_TN = (((0,), (0,)), ((), ()))


def _rowwise(name, body, *, grid, ins, outs, scratch=()):
    return pl.pallas_call(
        body, name=name, grid=(grid,), in_specs=[s for _, s in ins], out_specs=[s for _, s in outs],
        out_shape=[o for o, _ in outs], scratch_shapes=list(scratch),
        compiler_params=_params(1))(*[a for a, _ in ins])


def _rows(tb, w, cb=0, n=None):
    if n is None:
        return pl.BlockSpec((tb, w), lambda i: (i, cb))
    return pl.BlockSpec((tb, w), lambda i: (n - 1 - i, cb))


def _vec(shape):
    return pl.BlockSpec(shape, lambda i: (0,) * len(shape))


def _halo_prev(tb, w, cb=0, n=None, rows=8):
    if n is None:
        return pl.BlockSpec((rows, w), lambda i: (jnp.maximum(i * (tb // rows) - 1, 0), cb))
    return pl.BlockSpec((rows, w), lambda i: (jnp.maximum((n - 1 - i) * (tb // rows) - 1, 0), cb))


def _halo_next(tb, w, n, cb=0):
    return pl.BlockSpec((8, w), lambda i: (jnp.minimum((i + 1) * (tb // 8), n * (tb // 8) - 1), cb))


def _sds(shape, dtype=_F32):
    return jax.ShapeDtypeStruct(shape, dtype)


def _cast(name, a, tb):
    rows, cols = a.shape

    def body(a_ref, o_ref):
        o_ref[...] = a_ref[...].astype(o_ref.dtype)

    return _rowwise(name, body, grid=rows // tb, ins=[(a, _rows(tb, cols))],
                    outs=[(_sds((rows, cols), _MXU), _rows(tb, cols))])[0]


def _sum_lead(name, a, tb):
    g, rows, cols = a.shape

    def body(a_ref, o_ref):
        acc = a_ref[0]
        for k in range(1, g):
            acc = acc + a_ref[k]
        o_ref[...] = acc

    return _rowwise(name, body, grid=rows // tb,
                    ins=[(a, pl.BlockSpec((g, tb, cols), lambda i: (0, i, 0)))],
                    outs=[(_sds((rows, cols)), _rows(tb, cols))])[0]


def _sum_pair(name, mine, theirs, core, tb):
    _, nj, rows, cols = mine.shape

    def body(s_ref, a_ref, b_ref, o_ref, ob_ref):
        t = a_ref[...] + b_ref[...]
        o_ref[...] = t
        ob_ref[...] = t.astype(ob_ref.dtype)

    blk = pl.BlockSpec((None, tb, cols), lambda j, i, s: (j, i, 0))
    grid_spec = pltpu.PrefetchScalarGridSpec(
        num_scalar_prefetch=1, grid=(nj, rows // tb),
        in_specs=[pl.BlockSpec((None, None, tb, cols), lambda j, i, s: (s[0], j, i, 0)), blk],
        out_specs=[blk, blk])
    return pl.pallas_call(body, name=name, grid_spec=grid_spec,
                          out_shape=[_sds((nj, rows, cols)), _sds((nj, rows, cols), _MXU)],
                          compiler_params=_params(2))(core, mine, theirs)


def _sum_chips(name, mine, theirs, where, tb):
    _, rows, cols = mine.shape

    def body(s_ref, a_ref, b1_ref, b2_ref, b3_ref, o_ref):
        o_ref[...] = ((a_ref[...] + b1_ref[...].astype(_F32)) + b2_ref[...].astype(_F32)) + b3_ref[...].astype(_F32)

    def slot(k):
        return pl.BlockSpec((None, tb, cols), lambda i, s: (jnp.bitwise_xor(s[0], k), i, 0))

    grid_spec = pltpu.PrefetchScalarGridSpec(
        num_scalar_prefetch=1, grid=(rows // tb,),
        in_specs=[slot(0), slot(1), slot(2), slot(3)],
        out_specs=pl.BlockSpec((None, tb, cols), lambda i, s: (s[1], i, 0)))
    return pl.pallas_call(body, name=name, grid_spec=grid_spec, out_shape=_sds((2, rows, cols)),
                          compiler_params=_params(1))(where, mine, theirs, theirs, theirs)


def _adamw(name, w, g, m, v, tb):
    rows, cols = w.shape
    c1 = 1.0 - ADAM_B1 ** ADAM_STEP
    c2 = 1.0 - ADAM_B2 ** ADAM_STEP

    def body(w_ref, g_ref, m_ref, v_ref, d_ref, nm_ref, nv_ref):
        gv = g_ref[...]
        nm = ADAM_B1 * m_ref[...] + (1.0 - ADAM_B1) * gv
        nv = ADAM_B2 * v_ref[...] + (1.0 - ADAM_B2) * (gv * gv)
        d_ref[...] = -ADAM_LR * ((nm / c1) / (jnp.sqrt(nv / c2) + ADAM_EPS) + ADAM_WD * w_ref[...])
        nm_ref[...] = nm
        nv_ref[...] = nv

    spec = _rows(tb, cols)
    return _rowwise(name, body, grid=rows // tb, ins=[(w, spec), (g, spec), (m, spec), (v, spec)],
                    outs=[(_sds((rows, cols)), spec)] * 3)


def _mod_fwd(c_all, w_mod, b_cols):
    cols = w_mod.shape[2]

    def body(c_ref, w_ref, b_ref, o_ref):
        cv = c_ref[...]
        sc = (cv * _sigmoid(cv)).astype(_MXU)
        o_ref[...] = jnp.dot(sc, w_ref[...].astype(_MXU), preferred_element_type=_F32) + b_ref[...]

    return pl.pallas_call(
        body, name="mod_fwd", grid=(2,),
        in_specs=[pl.BlockSpec((8, D), lambda l: (0, 0)), pl.BlockSpec((None, D, cols), lambda l: (l, 0, 0)),
                  pl.BlockSpec((None, 1, cols), lambda l: (l, 0, 0))],
        out_specs=pl.BlockSpec((None, 8, cols), lambda l: (l, 0, 0)),
        out_shape=_sds((2, 8, cols)), compiler_params=_params(1))(c_all, w_mod, b_cols)


def _mod_bwd(c_all_t, dm):
    cols = dm.shape[2]

    def body(c_ref, d_ref, o_ref):
        cv = c_ref[...]
        sc = (cv * _sigmoid(cv)).astype(_MXU)
        o_ref[...] = jnp.dot(sc, d_ref[...].astype(_MXU), preferred_element_type=_F32)

    return pl.pallas_call(
        body, name="mod_bwd", grid=(2,),
        in_specs=[pl.BlockSpec((D, 8), lambda l: (0, 0)), pl.BlockSpec((None, 8, cols), lambda l: (l, 0, 0))],
        out_specs=pl.BlockSpec((None, D, cols), lambda l: (l, 0, 0)),
        out_shape=_sds((2, D, cols)), compiler_params=_params(1))(c_all_t, dm)


def _prenorm_fwd(x, g_pre, shift, scale):
    s = x.shape[0]
    tb = 512

    def body(x_ref, g_ref, sh_ref, sc_ref, h_ref, ht_ref):
        xv = x_ref[...]
        rstd = lax.rsqrt(jnp.mean(xv * xv, axis=-1, keepdims=True) + NORM_EPS)
        hv = (xv * rstd) * g_ref[...] * (1.0 + sc_ref[...]) + sh_ref[...]
        h_ref[...] = hv.astype(h_ref.dtype)
        ht_ref[...] = hv.T.astype(ht_ref.dtype)

    v = _vec((1, D))
    return _rowwise("prenorm_fwd", body, grid=s // tb,
                    ins=[(x, _rows(tb, D)), (g_pre, v), (shift, v), (scale, v)],
                    outs=[(_sds((s, D), _MXU), _rows(tb, D)),
                          (_sds((D, s), _MXU), pl.BlockSpec((D, tb), lambda i: (0, i)))])


def _shift_down(cur, halo, j, tb):
    ext = jnp.concatenate([halo, cur], axis=0)
    return pltpu.roll(ext, j, 0)[8:8 + tb]


def _shift_up(cur, halo, j, tb):
    ext = jnp.concatenate([cur, halo], axis=0)
    return pltpu.roll(ext, tb + 8 - j, 0)[0:tb]


def _conv_fwd(proj, conv_w, conv_b):
    s = proj.shape[0]
    tb = 512

    def body(u_ref, hp_ref, w_ref, b_ref, o_ref):
        i = pl.program_id(0)
        u = u_ref[...].astype(_F32)
        halo = jnp.where(i > 0, hp_ref[...].astype(_F32)[8:16], 0.0)
        acc = b_ref[...] + u * w_ref[0:1, :]
        for j in range(1, 4):
            acc = acc + _shift_down(u, halo, j, tb) * w_ref[j:j + 1, :]
        o_ref[...] = acc

    return _rowwise("conv_fwd", body, grid=s // tb,
                    ins=[(proj, _rows(tb, D, CB_U)), (proj, _halo_prev(tb, D, CB_U, rows=16)),
                         (conv_w, _vec((4, D))), (conv_b, _vec((1, D)))],
                    outs=[(_sds((s, D)), _rows(tb, D))])[0]


def _lru_gates(pre_r, pre_i, uc, b_rg, b_ig, lam):
    r = _sigmoid(pre_r + b_rg)
    ig = _sigmoid(pre_i + b_ig)
    nl = -lam
    sp = jnp.maximum(nl, 0.0) + jnp.log(1.0 + jnp.exp(-jnp.abs(nl)))
    la = -LRU_C * r * sp
    a = jnp.exp(la)
    y2 = 2.0 * la
    one_m_a2 = jnp.where(jnp.abs(y2) < 1e-2, -(y2 + 0.5 * y2 * y2 + (1.0 / 6.0) * y2 * y2 * y2),
                         1.0 - jnp.exp(y2))
    sq = jnp.sqrt(one_m_a2)
    return r, ig, sp, a, sq


def _scan_fwd(pre, uc, b_rg, b_ig, lam):
    s = uc.shape[0]
    tb = 256

    def body(pr_ref, pi_ref, uc_ref, brg_ref, big_ref, lam_ref, h_ref, carry):
        i = pl.program_id(0)

        @pl.when(i == 0)
        def _():
            carry[...] = jnp.zeros_like(carry)

        ucv = uc_ref[...]
        _, ig, _, a, sq = _lru_gates(pr_ref[...], pi_ref[...], ucv, brg_ref[...], big_ref[...], lam_ref[...])
        av = a
        bv = sq * (ig * ucv)
        row = lax.broadcasted_iota(jnp.int32, (tb, 1), 0)
        sh = 1
        while sh < tb:
            m = row >= sh
            b_sh = pltpu.roll(bv, sh, 0)
            a_sh = pltpu.roll(av, sh, 0)
            bv = jnp.where(m, av * b_sh + bv, bv)
            av = jnp.where(m, av * a_sh, av)
            sh *= 2
        hv = bv + av * carry[7:8, :]
        h_ref[...] = hv
        carry[...] = hv[tb - 8:tb]

    v = _vec((1, D))
    return _rowwise("scan_fwd", body, grid=s // tb,
                    ins=[(pre, _rows(tb, D, 0)), (pre, _rows(tb, D, 1)), (uc, _rows(tb, D)),
                         (b_rg, v), (b_ig, v), (lam, v)],
                    outs=[(_sds((s, D)), _rows(tb, D))],
                    scratch=[pltpu.VMEM((8, D), _F32)])[0]


def _weight_specs(l):
    return [pl.BlockSpec((N_CHIPS, None, ATT_W, 256), lambda i: (0, l, 0, 0)),
            pl.BlockSpec((N_CHIPS, None, 256, D), lambda i: (0, l, 0, 0)),
            pl.BlockSpec((N_CHIPS, None, 256, D), lambda i: (0, l, 0, 0))]


def _tail_fwd(l, o, h_lru, proj, x, gate, g_post, gw):
    s = x.shape[0]
    tb = 512

    def body(o_ref, h_ref, ga_ref, gl_ref, ma_ref, mb_ref, x_ref, gt_ref, gp_ref, wpa_ref, wpb_ref, wo_ref,
             aa_ref, ba_ref, ya_ref, yb_ref, z_ref, out_ref, xn_ref):
        ga = ga_ref[...].astype(_F32)
        aa = (o_ref[...] * (ga * _sigmoid(ga))).astype(_MXU)
        aa_ref[...] = aa
        gl = gl_ref[...].astype(_F32)
        ba = (h_ref[...] * (gl * _sigmoid(gl))).astype(_MXU)
        ba_ref[...] = ba
        for j in range(N_CHIPS):
            ya_ref[:, j * 256:(j + 1) * 256] = jnp.dot(aa, wpa_ref[j], preferred_element_type=_F32)
        yb = jnp.dot(ba, wpb_ref[...].reshape(D, D), preferred_element_type=_F32)
        yb_ref[...] = yb
        z = (_sigmoid(ma_ref[...].astype(_F32)) * ya_ref[...]
             + _sigmoid(mb_ref[...].astype(_F32)) * yb).astype(z_ref.dtype)
        z_ref[...] = z
        ov = jnp.dot(z, wo_ref[...].reshape(D, D), preferred_element_type=_F32)
        out_ref[...] = ov
        rstd = lax.rsqrt(jnp.mean(ov * ov, axis=-1, keepdims=True) + NORM_EPS)
        xn_ref[...] = x_ref[...] + gt_ref[...] * ((ov * rstd) * gp_ref[...])

    v = _vec((1, D))
    r = _rows(tb, D)
    r5 = _rows(tb, ATT_W)
    return _rowwise("tail_fwd", body, grid=s // tb,
                    ins=[(o, r5), (h_lru, r), (proj, _rows(tb, ATT_W, CB_GATT)), (proj, _rows(tb, D, CB_GLRU)),
                         (proj, _rows(tb, D, CB_MA)), (proj, _rows(tb, D, CB_MB)), (x, r), (gate, v), (g_post, v)]
                    + list(zip((gw["w_pa"], gw["w_pb"], gw["w_o"]), _weight_specs(l))),
                    outs=[(_sds((s, ATT_W), _MXU), r5), (_sds((s, D), _MXU), r), (_sds((s, D)), r),
                          (_sds((s, D)), r), (_sds((s, D), _MXU), r), (_sds((s, D)), r), (_sds((s, D)), r)])


def _loss_head(y, target):
    s = y.shape[0]
    tb = 512

    def body(y_ref, t_ref, dy_ref, acc_ref):
        i = pl.program_id(0)

        @pl.when(i == 0)
        def _():
            acc_ref[...] = jnp.zeros_like(acc_ref)

        err = y_ref[...] - t_ref[...]
        dy_ref[...] = err * (1.0 / D)
        acc_ref[...] += jnp.sum(err * err, axis=0, keepdims=True)

    return _rowwise("loss_head", body, grid=s // tb,
                    ins=[(y, _rows(tb, D)), (target, _rows(tb, D))],
                    outs=[(_sds((s, D)), _rows(tb, D)), (_sds((1, D)), _vec((1, D)))])


def _accumulate(i, ref, val):
    @pl.when(i == 0)
    def _():
        ref[...] = val

    @pl.when(i > 0)
    def _():
        ref[...] += val


def _tail_bwd(l, dx, out, y_a, y_b, proj, o, h_lru, gate, g_post, gw):
    s = dx.shape[0]
    tb = 256

    def body(dx_ref, out_ref, ya_ref, yb_ref, ma_ref, mb_ref, o_ref, ga_ref, h_ref, gl_ref, gt_ref, gp_ref,
             wpa_ref, wpb_ref, wo_ref,
             dout_ref, dya_ref, dyb_ref, dga_ref, rest_ref, do_ref, dh_ref, dgt_ref, dgp_ref):
        i = pl.program_id(0)
        ov = out_ref[...]
        dxv = dx_ref[...]
        rstd = lax.rsqrt(jnp.mean(ov * ov, axis=-1, keepdims=True) + NORM_EPS)
        nv = ov * rstd
        s_dn = jnp.sum(dxv * nv, axis=0, keepdims=True)
        _accumulate(i, dgt_ref, s_dn * gp_ref[...])
        _accumulate(i, dgp_ref, s_dn * gt_ref[...])
        dn = dxv * (gt_ref[...] * gp_ref[...])
        d_out = (rstd * (dn - nv * jnp.mean(dn * nv, axis=-1, keepdims=True))).astype(_MXU)
        dout_ref[...] = d_out
        dz = lax.dot_general(d_out, wo_ref[...].reshape(D, D), _NT, preferred_element_type=_F32)
        ga = _sigmoid(ma_ref[...].astype(_F32))
        gb = _sigmoid(mb_ref[...].astype(_F32))
        dya = (dz * ga).astype(_MXU)
        dyb = (dz * gb).astype(_MXU)
        dya_ref[...] = dya
        dyb_ref[...] = dyb
        rest_ref[:, D:2 * D] = (dz * ya_ref[...] * ga * (1.0 - ga)).astype(rest_ref.dtype)
        rest_ref[:, 2 * D:3 * D] = (dz * yb_ref[...] * gb * (1.0 - gb)).astype(rest_ref.dtype)
        daa = lax.dot_general(dya[:, 0:256], wpa_ref[0], _NT, preferred_element_type=_F32)
        for j in range(1, N_CHIPS):
            daa = daa + lax.dot_general(dya[:, j * 256:(j + 1) * 256], wpa_ref[j], _NT, preferred_element_type=_F32)
        dba = lax.dot_general(dyb, wpb_ref[...].reshape(D, D), _NT, preferred_element_type=_F32)
        gav = ga_ref[...].astype(_F32)
        sa = _sigmoid(gav)
        do_ref[...] = daa * (gav * sa)
        dga_ref[...] = (daa * o_ref[...] * (sa * (1.0 + gav * (1.0 - sa)))).astype(dga_ref.dtype)
        gl = gl_ref[...].astype(_F32)
        sl = _sigmoid(gl)
        dh_ref[...] = dba * (gl * sl)
        rest_ref[:, 0:D] = (dba * h_ref[...] * (sl * (1.0 + gl * (1.0 - sl)))).astype(rest_ref.dtype)

    v = _vec((1, D))
    r5, r10 = _rows(tb, ATT_W), _rows(tb, D)
    return _rowwise("tail_bwd", body, grid=s // tb,
                    ins=[(dx, r10), (out, r10), (y_a, r10), (y_b, r10), (proj, _rows(tb, D, CB_MA)),
                         (proj, _rows(tb, D, CB_MB)), (o, r5), (proj, _rows(tb, ATT_W, CB_GATT)), (h_lru, r10),
                         (proj, _rows(tb, D, CB_GLRU)), (gate, v), (g_post, v)]
                    + list(zip((gw["w_pa"], gw["w_pb"], gw["w_o"]), _weight_specs(l))),
                    outs=[(_sds((s, D), _MXU), r10), (_sds((s, D), _MXU), r10), (_sds((s, D), _MXU), r10),
                          (_sds((s, ATT_W), _MXU), r5), (_sds((s, 3 * D), _MXU), _rows(tb, 3 * D)),
                          (_sds((s, ATT_W)), r5), (_sds((s, D)), r10), (_sds((1, D)), v), (_sds((1, D)), v)])


def _scan_bwd(dh, pre, uc, h_lru, b_rg, b_ig, lam):
    s = uc.shape[0]
    tb = 256
    n = s // tb

    def body(dh_ref, pr_ref, pi_ref, uc_ref, h_ref, hp_ref, brg_ref, big_ref, lam_ref,
             dpre_ref, duc_ref, dbrg_ref, dbig_ref, dlam_ref, carry):
        i = pl.program_id(0)

        @pl.when(i == 0)
        def _():
            carry[...] = jnp.zeros_like(carry)

        ucv = uc_ref[...]
        r, ig, sp, a, sq = _lru_gates(pr_ref[...], pi_ref[...], ucv, brg_ref[...], big_ref[...], lam_ref[...])
        row = lax.broadcasted_iota(jnp.int32, (tb, 1), 0)
        cv = jnp.where(row == tb - 1, 1.0, pltpu.roll(a, tb - 1, 0))
        gv = dh_ref[...]
        sh = 1
        while sh < tb:
            m = row < tb - sh
            g_sh = pltpu.roll(gv, tb - sh, 0)
            c_sh = pltpu.roll(cv, tb - sh, 0)
            gv = jnp.where(m, gv + cv * g_sh, gv)
            cv = jnp.where(m, cv * c_sh, cv)
            sh *= 2
        gv = gv + cv * carry[0:1, :]
        carry[...] = (a * gv)[0:8]

        halo = jnp.where(i < n - 1, hp_ref[...], 0.0)
        h_prev = _shift_down(h_ref[...], halo, 1, tb)
        d_a = gv * h_prev
        d_sq = gv * (ig * ucv)
        d_i = gv * sq * ucv
        duc_ref[...] = gv * sq * ig
        d_la = d_a * a - d_sq * (a * a) / sq
        d_r = d_la * (-LRU_C * sp)
        d_pre_r = d_r * r * (1.0 - r)
        d_pre_i = d_i * ig * (1.0 - ig)
        dpre_ref[:, 0:D] = d_pre_r.astype(dpre_ref.dtype)
        dpre_ref[:, D:2 * D] = d_pre_i.astype(dpre_ref.dtype)
        _accumulate(i, dbrg_ref, jnp.sum(d_pre_r, axis=0, keepdims=True))
        _accumulate(i, dbig_ref, jnp.sum(d_pre_i, axis=0, keepdims=True))
        lamv = lam_ref[...]
        _accumulate(i, dlam_ref, jnp.sum(d_la * (-LRU_C * r), axis=0, keepdims=True) * (-_sigmoid(-lamv)))

    v = _vec((1, D))
    rv = _rows(tb, D, 0, n)
    return _rowwise("scan_bwd", body, grid=n,
                    ins=[(dh, rv), (pre, _rows(tb, D, 0, n)), (pre, _rows(tb, D, 1, n)), (uc, rv), (h_lru, rv),
                         (h_lru, _halo_prev(tb, D, 0, n)), (b_rg, v), (b_ig, v), (lam, v)],
                    outs=[(_sds((s, 2 * D), _MXU), _rows(tb, 2 * D, 0, n)), (_sds((s, D)), rv),
                          (_sds((1, D)), v), (_sds((1, D)), v), (_sds((1, D)), v)],
                    scratch=[pltpu.VMEM((8, D), _F32)])


def _conv_bwd(duc_a, duc_b, proj, conv_w):
    s = duc_a.shape[0]
    tb = 512
    n = s // tb

    def body(da_ref, db_ref, dan_ref, dbn_ref, u_ref, up_ref, w_ref, du_ref, dw_ref, dbias_ref):
        i = pl.program_id(0)
        duc = da_ref[...] + db_ref[...]
        nxt = jnp.where(i < n - 1, dan_ref[...] + dbn_ref[...], 0.0)
        u = u_ref[...].astype(_F32)
        halo = jnp.where(i > 0, up_ref[...].astype(_F32)[8:16], 0.0)
        du = duc * w_ref[0:1, :]
        dws = [jnp.sum(duc * u, axis=0, keepdims=True)]
        for j in range(1, 4):
            du = du + _shift_up(duc, nxt, j, tb) * w_ref[j:j + 1, :]
            dws.append(jnp.sum(duc * _shift_down(u, halo, j, tb), axis=0, keepdims=True))
        du_ref[...] = du.astype(du_ref.dtype)
        for j in range(4):
            _accumulate(i, dw_ref.at[j:j + 1, :], dws[j])
        _accumulate(i, dbias_ref, jnp.sum(duc, axis=0, keepdims=True))

    r = _rows(tb, D)
    return _rowwise("conv_bwd", body, grid=n,
                    ins=[(duc_a, r), (duc_b, r), (duc_a, _halo_next(tb, D, n)), (duc_b, _halo_next(tb, D, n)),
                         (proj, _rows(tb, D, CB_U)), (proj, _halo_prev(tb, D, CB_U, rows=16)),
                         (conv_w, _vec((4, D)))],
                    outs=[(_sds((s, D), _MXU), r), (_sds((4, D)), _vec((4, D))), (_sds((1, D)), _vec((1, D)))])


def _prenorm_bwd(dh, x, dx_out, g_pre, scale):
    s = x.shape[0]
    tb = 512

    def body(dh_ref, x_ref, dxo_ref, g_ref, sc_ref, dx_ref, dsh_ref, dsc_ref, dg_ref):
        i = pl.program_id(0)
        xv = x_ref[...]
        dhv = dh_ref[...]
        rstd = lax.rsqrt(jnp.mean(xv * xv, axis=-1, keepdims=True) + NORM_EPS)
        xn = xv * rstd
        one_sc = 1.0 + sc_ref[...]
        s1 = jnp.sum(dhv * xn, axis=0, keepdims=True)
        _accumulate(i, dsh_ref, jnp.sum(dhv, axis=0, keepdims=True))
        _accumulate(i, dsc_ref, s1 * g_ref[...])
        _accumulate(i, dg_ref, s1 * one_sc)
        dxn = dhv * (g_ref[...] * one_sc)
        dx_ref[...] = dxo_ref[...] + rstd * (dxn - xn * jnp.mean(dxn * xn, axis=-1, keepdims=True))

    v = _vec((1, D))
    r = _rows(tb, D)
    return _rowwise("prenorm_bwd", body, grid=s // tb,
                    ins=[(dh, r), (x, r), (dx_out, r), (g_pre, v), (scale, v)],
                    outs=[(_sds((s, D)), r), (_sds((1, D)), v), (_sds((1, D)), v), (_sds((1, D)), v)])


def _band_tiles(dil):
    tiles = []
    for rho in range(dil):
        for b in range(16 // dil):
            qs = rho + dil * BAND * b
            tiles.append((qs, QBLK + qs - dil * BAND, b))
    return tiles


def _strided(start, size, dil):
    return pl.ds(start, size, stride=dil) if dil > 1 else pl.ds(start, size)


def _band_mask(i, b):
    qi = lax.broadcasted_iota(jnp.int32, (BAND, 2 * BAND), 0)
    ki = lax.broadcasted_iota(jnp.int32, (BAND, 2 * BAND), 1)
    valid = (ki >= qi) & (ki <= qi + BAND)
    if b == 0:
        valid = valid & ((ki >= BAND) | (i > 0))
    return valid


def _attn_fwd(proj):
    s = proj.shape[0]
    n = s // QBLK
    scale = HEAD ** -0.5

    def body(*refs):
        q_refs, kp_refs, kc_refs, vp_refs, vc_refs = (refs[3 * t:3 * t + 3] for t in range(5))
        o_ref, lse_ref, qbuf, kbuf, vbuf = refs[15:20]
        accs, maxs, dens = refs[20:23], refs[23:26], refs[26:29]
        i = pl.program_id(1)
        for g, dil in enumerate(DILATIONS):
            qbuf[...] = q_refs[g][...].astype(_F32)
            kbuf[0:QBLK, :] = kp_refs[g][...].astype(_F32)
            kbuf[QBLK:2 * QBLK, :] = kc_refs[g][...].astype(_F32)
            vbuf[0:QBLK, :] = vp_refs[g][...].astype(_F32)
            vbuf[QBLK:2 * QBLK, :] = vc_refs[g][...].astype(_F32)
            for qs, ks, b in _band_tiles(dil):
                qsl = _strided(qs, BAND, dil)
                q = qbuf[qsl, :].astype(_MXU)
                kk = kbuf[_strided(ks, 2 * BAND, dil), :].astype(_MXU)
                vv = vbuf[_strided(ks, 2 * BAND, dil), :].astype(_MXU)
                sc = lax.dot_general(q, kk, _NT, preferred_element_type=_F32) * scale
                sc = jnp.where(_band_mask(i, b), sc, NEG_INF)
                m = jnp.max(sc, axis=-1, keepdims=True)
                p = jnp.exp(sc - m)
                accs[g][qsl, :] = jnp.dot(p.astype(_MXU), vv, preferred_element_type=_F32)
                maxs[g][qsl, :] = jnp.broadcast_to(m, (BAND, HEAD))
                dens[g][qsl, :] = jnp.broadcast_to(jnp.sum(p, axis=-1, keepdims=True), (BAND, HEAD))
        ms = [r[...] for r in maxs]
        mx = jnp.maximum(jnp.maximum(ms[0], ms[1]), ms[2])
        ws = [jnp.exp(m - mx) for m in ms]
        den = ws[0] * dens[0][...] + ws[1] * dens[1][...] + ws[2] * dens[2][...]
        o_ref[...] = (ws[0] * accs[0][...] + ws[1] * accs[1][...] + ws[2] * accs[2][...]) / den
        lse_ref[...] = mx + jnp.log(den)

    blk = (QBLK, HEAD)

    def spec(first_col, lag):
        specs = []
        for g in range(3):
            col = first_col + g * HEADS
            if lag:
                specs.append(pl.BlockSpec(blk, lambda j, i, col=col: (jnp.maximum(i - 1, 0), col + j)))
            else:
                specs.append(pl.BlockSpec(blk, lambda j, i, col=col: (i, col + j)))
        return specs

    out_spec = pl.BlockSpec(blk, lambda j, i: (i, j))
    return pl.pallas_call(
        body, name="attn_fwd", grid=(HEADS, n),
        in_specs=spec(0, False) + spec(12, True) + spec(12, False) + spec(24, True) + spec(24, False),
        out_specs=[out_spec] * 2, out_shape=[_sds((s, ATT_W))] * 2,
        scratch_shapes=[pltpu.VMEM(blk, _F32)] + [pltpu.VMEM((2 * QBLK, HEAD), _F32)] * 2
        + [pltpu.VMEM(blk, _F32)] * 9,
        compiler_params=_params(2))(*([proj] * 15))


def _attn_bwd(proj, d_o, o, lse, g):
    s = proj.shape[0]
    dil = DILATIONS[g]
    n = s // QBLK
    scale = HEAD ** -0.5
    tiles = _band_tiles(dil)

    def body(q_ref, kp_ref, kc_ref, vp_ref, vc_ref, do_ref, o_ref, lse_ref, dq_ref, dk_ref, dv_ref,
             kbuf, vbuf, dkbuf, dvbuf, dqbuf, qbuf):
        i = pl.program_id(1)

        @pl.when(i == 0)
        def _():
            dkbuf[0:QBLK, :] = jnp.zeros((QBLK, HEAD), _F32)
            dvbuf[0:QBLK, :] = jnp.zeros((QBLK, HEAD), _F32)

        @pl.when(i < n)
        def _():
            qbuf[...] = q_ref[...].astype(_F32)
            kbuf[0:QBLK, :] = kp_ref[...].astype(_F32)
            kbuf[QBLK:2 * QBLK, :] = kc_ref[...].astype(_F32)
            vbuf[0:QBLK, :] = vp_ref[...].astype(_F32)
            vbuf[QBLK:2 * QBLK, :] = vc_ref[...].astype(_F32)
            dkbuf[QBLK:2 * QBLK, :] = jnp.zeros((QBLK, HEAD), _F32)
            dvbuf[QBLK:2 * QBLK, :] = jnp.zeros((QBLK, HEAD), _F32)
            for qs, ks, b in tiles:
                qsl = _strided(qs, BAND, dil)
                ksl = _strided(ks, 2 * BAND, dil)
                q = qbuf[qsl, :].astype(_MXU)
                kk = kbuf[ksl, :].astype(_MXU)
                vv = vbuf[ksl, :].astype(_MXU)
                dov = do_ref[qsl, :]
                dd = jnp.sum(dov * o_ref[qsl, :], axis=-1, keepdims=True)
                lse_t = lse_ref[qsl, :][:, 0:1]
                sc = lax.dot_general(q, kk, _NT, preferred_element_type=_F32) * scale
                p = jnp.where(_band_mask(i, b), jnp.exp(sc - lse_t), 0.0)
                dob = dov.astype(_MXU)
                dp = lax.dot_general(dob, vv, _NT, preferred_element_type=_F32)
                ds = (p * (dp - dd) * scale).astype(_MXU)
                dqbuf[qsl, :] = jnp.dot(ds, kk, preferred_element_type=_F32)
                dkbuf[ksl, :] += lax.dot_general(ds, q, _TN, preferred_element_type=_F32)
                dvbuf[ksl, :] += lax.dot_general(p.astype(_MXU), dob, _TN, preferred_element_type=_F32)
            dq_ref[...] = dqbuf[...].astype(dq_ref.dtype)

        dk_ref[...] = dkbuf[0:QBLK, :].astype(dk_ref.dtype)
        dv_ref[...] = dvbuf[0:QBLK, :].astype(dv_ref.dtype)
        dkbuf[0:QBLK, :] = dkbuf[QBLK:2 * QBLK, :]
        dvbuf[0:QBLK, :] = dvbuf[QBLK:2 * QBLK, :]

    blk = (QBLK, HEAD)
    cq, ck, cv = g * HEADS, 12 + g * HEADS, 24 + g * HEADS

    def cur(i):
        return jnp.minimum(i, n - 1)

    def prev(i):
        return jnp.maximum(jnp.minimum(i, n - 1) - 1, 0)

    own = pl.BlockSpec(blk, lambda j, i: (cur(i), j))
    late = pl.BlockSpec(blk, lambda j, i: (jnp.maximum(i - 1, 0), j))
    return pl.pallas_call(
        body, name="attn_bwd_d%d" % dil, grid=(HEADS, n + 1),
        in_specs=[pl.BlockSpec(blk, lambda j, i: (cur(i), cq + j)),
                  pl.BlockSpec(blk, lambda j, i: (prev(i), ck + j)),
                  pl.BlockSpec(blk, lambda j, i: (cur(i), ck + j)),
                  pl.BlockSpec(blk, lambda j, i: (prev(i), cv + j)),
                  pl.BlockSpec(blk, lambda j, i: (cur(i), cv + j)),
                  own, own, own],
        out_specs=[own, late, late], out_shape=[_sds((s, ATT_W), _MXU)] * 3,
        scratch_shapes=[pltpu.VMEM((2 * QBLK, HEAD), _F32)] * 4 + [pltpu.VMEM((QBLK, HEAD), _F32)] * 2,
        compiler_params=_params(2))(proj, proj, proj, proj, proj, d_o, o, lse)


def _block_diag(w):
    eye = jnp.eye(16, dtype=w.dtype)
    return jnp.einsum("hij,hg->higj", w, eye).reshape(D, D)


def _diag_blocks(gd):
    g4 = gd.reshape(16, 64, 16, 64)
    keep = jnp.eye(16, dtype=jnp.bool_)[:, None, :, None]
    return jnp.sum(jnp.where(keep, g4, 0.0), axis=2)


def _layer_fwd(l, x, p, gw):
    s = x.shape[0]
    nm = s // 1024
    h, h_t = _prenorm_fwd(x, p["g_pre"], p["shift"], p["scale"])
    proj = _mm("proj", h, gw["w_in"], _sds((s, IN_W), _MXU), grid=(nm, N_CHIPS, 1),
               a_spec=pl.BlockSpec((1024, D), lambda m, n, k: (m, 0)),
               b_spec=pl.BlockSpec((None, None, D, 2304), lambda m, n, k: (n, l, 0, 0)),
               o_spec=pl.BlockSpec((1024, 2304), lambda m, n, k: (m, n)), dims=_NN, acc_shape=(1024, 2304))
    o, lse = _attn_fwd(proj)
    uc = _conv_fwd(proj, p["conv_w"], p["conv_b"])
    pre = _mm("lru_gates", uc, p["w_gates"], _sds((s, 2 * D)), grid=(nm, 1, 1),
              a_spec=pl.BlockSpec((1024, D), lambda m, n, k: (m, 0)),
              b_spec=pl.BlockSpec((D, 2 * D), lambda m, n, k: (0, 0)),
              o_spec=pl.BlockSpec((1024, 2 * D), lambda m, n, k: (m, 0)), dims=_NN, acc_shape=(1024, 2 * D))
    h_lru = _scan_fwd(pre, uc, p["b_rg"], p["b_ig"], p["lam"])
    a_att, b_act, y_a, y_b, z, out, x_new = _tail_fwd(l, o, h_lru, proj, x, p["gate"], p["g_post"], gw)
    saved = dict(x=x, h_t=h_t, proj=proj, o=o, lse=lse, uc=uc, pre=pre, h_lru=h_lru, a_att=a_att, b_act=b_act,
                 y_a=y_a, y_b=y_b, z=z, out=out)
    return x_new, saved


def _layer_bwd(l, dx, p, gw, sv, big):
    s = dx.shape[0]
    nm = s // 1024
    nt = s // 2048
    proj = sv["proj"]
    d_out, dy_a, dy_b, d_gatt, d_rest, d_o, dh_lru, d_gate, d_gpost = _tail_bwd(
        l, dx, sv["out"], sv["y_a"], sv["y_b"], proj, sv["o"], sv["h_lru"], p["gate"], p["g_post"], gw)

    def wgrad_rows(name, a, b, into):
        return _mm(name, a, b, _sds((2, N_CHIPS, 256, D)), grid=(4, 1, nt),
                   a_spec=pl.BlockSpec((2048, 256), lambda m, n, k: (k, m)),
                   b_spec=pl.BlockSpec((2048, D), lambda m, n, k: (k, 0)),
                   o_spec=pl.BlockSpec((None, None, 256, D), lambda m, n, k: (l, m, 0, 0)),
                   dims=_TN, acc_shape=(256, D), into=into)

    big = dict(big)
    big["w_o"] = wgrad_rows("g_w_o", sv["z"], d_out, big.get("w_o"))
    big["w_pa"] = _mm("g_w_pa", sv["a_att"], dy_a, _sds((2, N_CHIPS, ATT_W, 256)), grid=(1, 4, nt),
                      a_spec=pl.BlockSpec((2048, ATT_W), lambda m, n, k: (k, 0)),
                      b_spec=pl.BlockSpec((2048, 256), lambda m, n, k: (k, n)),
                      o_spec=pl.BlockSpec((None, None, ATT_W, 256), lambda m, n, k: (l, n, 0, 0)),
                      dims=_TN, acc_shape=(ATT_W, 256), into=big.get("w_pa"))
    big["w_pb"] = wgrad_rows("g_w_pb", sv["b_act"], dy_b, big.get("w_pb"))
    d_pre, duc_dir, d_brg, d_big, d_lam = _scan_bwd(dh_lru, sv["pre"], sv["uc"], sv["h_lru"],
                                                   p["b_rg"], p["b_ig"], p["lam"])
    duc_mm = _mm("d_uc", d_pre, p["w_gates"], _sds((s, D)), grid=(nm, 1, 1),
                 a_spec=pl.BlockSpec((1024, 2 * D), lambda m, n, k: (m, 0)),
                 b_spec=pl.BlockSpec((D, 2 * D), lambda m, n, k: (0, 0)),
                 o_spec=pl.BlockSpec((1024, D), lambda m, n, k: (m, 0)), dims=_NT, acc_shape=(1024, D))
    g_gates = _mm("g_w_gates", sv["uc"], d_pre, _sds((D, 2 * D)), grid=(1, 2, s // 1024),
                  a_spec=pl.BlockSpec((1024, D), lambda m, n, k: (k, 0)),
                  b_spec=pl.BlockSpec((1024, D), lambda m, n, k: (k, n)),
                  o_spec=pl.BlockSpec((D, D), lambda m, n, k: (0, n)), dims=_TN, acc_shape=(D, D))
    d_u, g_convw, g_convb = _conv_bwd(duc_dir, duc_mm, proj, p["conv_w"])
    dqkv = [_attn_bwd(proj, d_o, sv["o"], sv["lse"], g) for g in range(3)]
    dproj = jnp.concatenate([dqkv[g][t] for t in range(3) for g in range(3)] + [d_gatt, d_u, d_rest], axis=1)
    dh = _mm("d_h", dproj, gw["w_in"], _sds((s, D)), grid=(nm, 1, N_CHIPS),
             a_spec=pl.BlockSpec((1024, 2304), lambda m, n, k: (m, k)),
             b_spec=pl.BlockSpec((None, None, D, 2304), lambda m, n, k: (k, l, 0, 0)),
             o_spec=pl.BlockSpec((1024, D), lambda m, n, k: (m, 0)), dims=_NT, acc_shape=(1024, D))
    big["w_in"] = _mm("g_w_in", sv["h_t"], dproj, _sds((2, N_CHIPS, D, 2304)), grid=(1, N_CHIPS, s // 1024),
                      a_spec=pl.BlockSpec((D, 1024), lambda m, n, k: (0, k)),
                      b_spec=pl.BlockSpec((1024, 2304), lambda m, n, k: (k, n)),
                      o_spec=pl.BlockSpec((None, None, D, 2304), lambda m, n, k: (l, n, 0, 0)),
                      dims=_NN, acc_shape=(D, 2304), into=big.get("w_in"))
    dx_in, d_shift, d_scale, d_gpre = _prenorm_bwd(dh, sv["x"], dx, p["g_pre"], p["scale"])
    small = dict(dmod=jnp.concatenate([d_shift, d_scale, d_gate], axis=1), g_pre=d_gpre, conv_w=g_convw,
                 conv_b=g_convb, w_rg=_diag_blocks(g_gates[:, 0:D]), b_rg=d_brg,
                 w_ig=_diag_blocks(g_gates[:, D:2 * D]), b_ig=d_big, lam=d_lam, g_post=d_gpost)
    return dx_in, small, big


def _local_step(x, target, small_p, gw):
    saved = []
    h = x
    for l in range(2):
        h, sv = _layer_fwd(l, h, small_p[l], gw)
        saved.append(sv)
    dy, sq = _loss_head(h, target)
    loss = 0.5 * jnp.sum(sq) / D
    big = {}
    smalls = [None, None]
    dx = dy
    for l in (1, 0):
        dx, smalls[l], big = _layer_bwd(l, dx, small_p[l], gw, saved[l], big)
    return loss, dx, smalls, big


_SMALL_ROWS = 8 + 16 + 8 + 128 + 128


def _pack_small(smalls):
    dmod = jnp.concatenate([smalls[0]["dmod"].reshape(3, D), smalls[1]["dmod"].reshape(3, D),
                            jnp.zeros((2, D), _F32)], axis=0)
    vecs = jnp.concatenate([smalls[l][k] for k in ("g_pre", "conv_b", "b_rg", "b_ig", "lam", "g_post")
                            for l in range(2)] + [jnp.zeros((4, D), _F32)], axis=0)
    convw = jnp.concatenate([smalls[0]["conv_w"], smalls[1]["conv_w"]], axis=0)
    wrg = jnp.stack([smalls[0]["w_rg"], smalls[1]["w_rg"]]).reshape(128, D)
    wig = jnp.stack([smalls[0]["w_ig"], smalls[1]["w_ig"]]).reshape(128, D)
    return jnp.concatenate([dmod, vecs, convw, wrg, wig], axis=0)


def kernel(x, c, w_mod, b_mod, g_pre, w_in, conv_w, conv_b, w_rg, b_rg, w_ig, b_ig, lru_lambda, w_pa, w_pb, w_o, g_post, loss_target, m_w_mod, m_b_mod, m_g_pre, m_w_in, m_conv_w, m_conv_b, m_w_rg, m_b_rg, m_w_ig, m_b_ig, m_lru_lambda, m_w_pa, m_w_pb, m_w_o, m_g_post, v_w_mod, v_b_mod, v_g_pre, v_w_in, v_conv_w, v_conv_b, v_w_rg, v_b_rg, v_w_ig, v_b_ig, v_lru_lambda, v_w_pa, v_w_pb, v_w_o, v_g_post):
    xi, yi, ci = lax.axis_index("x"), lax.axis_index("y"), lax.axis_index("c")
    chip = 2 * xi + yi
    dev = 4 * xi + 2 * yi + ci
    mcols = w_mod.shape[2]

    pack1 = jnp.concatenate([jnp.broadcast_to(c, (8, D)),
                             jnp.pad(conv_w.reshape(8, 256), ((0, 0), (0, D - 256)))], axis=0)
    g1 = _exchange("gather_cond", [pack1], "xyc", False)[0]
    c_all = g1[:, 0, :]
    conv_w_full = jnp.transpose(g1[0::2, 8:16, 0:256], (1, 0, 2)).reshape(2, 4, D)

    b_cols = lax.dynamic_slice(b_mod, (0, chip * mcols), (2, mcols)).reshape(2, 1, mcols)
    mod_loc = _mod_fwd(c_all, w_mod, b_cols)
    g2 = _exchange("gather_mod", [mod_loc.reshape(16, mcols)], "xyc", False)[0]
    mod_full = jnp.transpose(g2[0::2], (1, 0, 2)).reshape(2, 8, 3 * D)
    mod_me = lax.dynamic_index_in_dim(mod_full, dev, axis=1, keepdims=False)

    wb = [_cast("cast_w_in", w_in.reshape(2 * D, 2304), 256).reshape(2, D, 2304),
          _cast("cast_w_pa", w_pa.reshape(2 * ATT_W, 256), 256).reshape(2, ATT_W, 256),
          _cast("cast_w_pb", w_pb.reshape(512, D), 256).reshape(2, 256, D),
          _cast("cast_w_o", w_o.reshape(512, D), 256).reshape(2, 256, D)]
    gl = _gather_weights(wb, [4, 1, 1, 1])
    gw = dict(w_in=gl[0], w_pa=gl[1], w_pb=gl[2], w_o=gl[3])

    small_p = []
    for l in range(2):
        gates = jnp.concatenate([_block_diag(w_rg[l]), _block_diag(w_ig[l])], axis=1).astype(_MXU)
        small_p.append(dict(
            shift=mod_me[l:l + 1, 0:D], scale=mod_me[l:l + 1, D:2 * D], gate=mod_me[l:l + 1, 2 * D:3 * D],
            g_pre=g_pre[l:l + 1], conv_w=conv_w_full[l], conv_b=conv_b[l:l + 1], w_gates=gates,
            b_rg=b_rg[l:l + 1], b_ig=b_ig[l:l + 1], lam=lru_lambda[l:l + 1], g_post=g_post[l:l + 1]))

    loss_loc, dx, smalls, big = _local_step(x[0], loss_target[0], small_p, gw)
    loss = lax.psum(loss_loc, ("x", "y", "c"))
    grad_x = dx[None]

    names = ("w_in", "w_pa", "w_pb", "w_o")
    core = jnp.reshape(ci, (1,)).astype(jnp.int32)
    where = jnp.stack([chip, ci]).astype(jnp.int32)
    pair = list(_exchange("reduce_pair", [big["w_in"].reshape(2, 16, 256, 2304)] + [big[k] for k in names[1:]],
                          "c", True, local=False, nchunks=[16, 4, 4, 4]))
    pair[0] = pair[0].reshape(N_CHIPS, D, 2304)
    t1 = [_sum_pair("sum_pair_" + k, big[k], r, core, 128) for k, r in zip(names, pair)]
    quad = _exchange("reduce_chips", [t[1] for t in t1], "xy", True, local=False, nchunks=[4, 1, 1, 1])
    t3 = [_sum_chips("sum_chips_" + k, t[0], r, where, 128) for k, t, r in zip(names, t1, quad)]
    both = _pair_fill("gather_layers", t3, [4, 1, 1, 1])
    g_big = dict(zip(names, both))

    g3 = _exchange("gather_small", [_pack_small(smalls)], "xyc", False)[0]
    tot = _sum_lead("sum_small", g3, 96)
    dmod_all = g3[:, 0:6, :].reshape(8, 2, 3 * D)
    dm_cols = jnp.transpose(lax.dynamic_slice(dmod_all, (0, 0, chip * mcols), (8, 2, mcols)), (1, 0, 2))
    g_w_mod = _mod_bwd(jnp.transpose(c_all), dm_cols)
    vec = tot[8:20].reshape(6, 2, D)
    g_conv_w_full = tot[24:32].reshape(2, 4, D)
    grads = dict(
        w_mod=g_w_mod, b_mod=tot[0:6].reshape(2, 3 * D), g_pre=vec[0], w_in=g_big["w_in"],
        conv_w=lax.dynamic_slice(g_conv_w_full, (0, 0, chip * 256), (2, 4, 256)), conv_b=vec[1],
        w_rg=tot[32:160].reshape(2, 16, 64, 64), b_rg=vec[2], w_ig=tot[160:288].reshape(2, 16, 64, 64),
        b_ig=vec[3], lru_lambda=vec[4], w_pa=g_big["w_pa"], w_pb=g_big["w_pb"], w_o=g_big["w_o"],
        g_post=vec[5])

    weights = dict(w_mod=w_mod, b_mod=b_mod, g_pre=g_pre, w_in=w_in, conv_w=conv_w, conv_b=conv_b, w_rg=w_rg,
                   b_rg=b_rg, w_ig=w_ig, b_ig=b_ig, lru_lambda=lru_lambda, w_pa=w_pa, w_pb=w_pb, w_o=w_o,
                   g_post=g_post)
    ms = dict(w_mod=m_w_mod, b_mod=m_b_mod, g_pre=m_g_pre, w_in=m_w_in, conv_w=m_conv_w, conv_b=m_conv_b,
              w_rg=m_w_rg, b_rg=m_b_rg, w_ig=m_w_ig, b_ig=m_b_ig, lru_lambda=m_lru_lambda, w_pa=m_w_pa,
              w_pb=m_w_pb, w_o=m_w_o, g_post=m_g_post)
    vs = dict(w_mod=v_w_mod, b_mod=v_b_mod, g_pre=v_g_pre, w_in=v_w_in, conv_w=v_conv_w, conv_b=v_conv_b,
              w_rg=v_w_rg, b_rg=v_b_rg, w_ig=v_w_ig, b_ig=v_b_ig, lru_lambda=v_lru_lambda, w_pa=v_w_pa,
              w_pb=v_w_pb, w_o=v_w_o, g_post=v_g_post)
    flat = dict(w_mod=(2 * D, mcols, 256), b_mod=(2, 3 * D, 2), g_pre=(2, D, 2), w_in=(2 * D, 2304, 256),
                conv_w=(8, 256, 8), conv_b=(2, D, 2), w_rg=(128, D, 128), b_rg=(2, D, 2), w_ig=(128, D, 128),
                b_ig=(2, D, 2), lru_lambda=(2, D, 2), w_pa=(2 * ATT_W, 256, 256), w_pb=(512, D, 256),
                w_o=(512, D, 256), g_post=(2, D, 2))
    order = ("w_mod", "b_mod", "g_pre", "w_in", "conv_w", "conv_b", "w_rg", "b_rg", "w_ig", "b_ig",
             "lru_lambda", "w_pa", "w_pb", "w_o", "g_post")
    deltas, new_m, new_v = [], [], []
    for k in order:
        rows, cols, tb = flat[k]
        shp = weights[k].shape
        d, nm_, nv_ = _adamw("adamw_" + k, weights[k].reshape(rows, cols), grads[k].reshape(rows, cols),
                             ms[k].reshape(rows, cols), vs[k].reshape(rows, cols), tb)
        deltas.append(d.reshape(shp))
        new_m.append(nm_.reshape(shp))
        new_v.append(nv_.reshape(shp))
    return (loss, grad_x, *[grads[k].reshape(weights[k].shape) for k in order], *deltas, *new_m, *new_v)
```

```python
import functools

import jax
import jax.numpy as jnp
from jax import lax
from jax.experimental import pallas as pl
from jax.experimental.pallas import tpu as pltpu

_F32 = jnp.float32
_MXU = jnp.bfloat16
_VMEM_LIMIT = 56 * 1024 * 1024
_MESH = pl.DeviceIdType.MESH

D = 1024
HEAD = 128
HEADS = 4
ATT_W = 512
QKV_W = 1536
IN_W = 9216
DILATIONS = (1, 4, 16)
BAND = 128
QBLK = BAND * 16
NORM_EPS = 1e-6
NEG_INF = -1e30
LRU_C = 8.0
N_CHIPS = 4
CB_GATT = 4608 // 512
CB_U, CB_GLRU, CB_MA, CB_MB = 5, 6, 7, 8
R_U, R_GLRU, R_MA, R_MB, R_END = 512, 1536, 2560, 3584, 4608

ADAM_LR, ADAM_B1, ADAM_B2, ADAM_EPS, ADAM_WD, ADAM_STEP = 0.001, 0.9, 0.999, 1e-08, 0.01, 10


def _params(ngrid):
    return pltpu.CompilerParams(dimension_semantics=("arbitrary",) * ngrid, vmem_limit_bytes=_VMEM_LIMIT)


def _sigmoid(v):
    return 0.5 * jnp.tanh(0.5 * v) + 0.5


_GROUPS = {
    "c": [(0, 0, 1)],
    "xy": [(1, 0, 0), (0, 1, 0), (1, 1, 0)],
    "xyc": [(0, 0, 1), (0, 1, 0), (0, 1, 1), (1, 0, 0), (1, 0, 1), (1, 1, 0), (1, 1, 1)],
}


def _rank(group, px, py, pc):
    if group == "c":
        return pc
    if group == "xy":
        return 2 * px + py
    return 4 * px + 2 * py + pc


def _flip(rel, x, y, c):
    dx, dy, dc = rel
    return (1 - x if dx else x, 1 - y if dy else y, 1 - c if dc else c)


def _pieces(ref, nchunk):
    step = ref.shape[0] // nchunk
    return [ref.at[pl.ds(q * step, step)] for q in range(nchunk)]


def _exchange(name, srcs, group, scatter, *, local=True, nchunks=None):
    rels = _GROUPS[group]
    gsize = len(rels) + 1
    n = len(srcs)
    nchunks = nchunks or [1] * n
    blks = [s.shape[1:] if scatter else s.shape for s in srcs]
    slotted = local or gsize > 2
    base = [sum(nchunks[:a]) for a in range(n)]
    tot = sum(nchunks)

    def body(*refs):
        src_refs, out_refs = refs[:n], refs[n:2 * n]
        send_sems, recv_sems, loc_sems = refs[2 * n:]
        x, y, c = lax.axis_index("x"), lax.axis_index("y"), lax.axis_index("c")
        me = _rank(group, x, y, c)
        copies = []
        for a in range(n):
            def part(r, a=a):
                return src_refs[a].at[r] if scatter else src_refs[a]
            dst = out_refs[a].at[me] if slotted else out_refs[a]
            if local:
                for q, (s_, d_) in enumerate(zip(_pieces(part(me), nchunks[a]), _pieces(dst, nchunks[a]))):
                    loc = pltpu.make_async_copy(s_, d_, loc_sems.at[base[a] + q])
                    loc.start()
                    copies.append(loc)
            for k, rel in enumerate(rels):
                peer = _flip(rel, x, y, c)
                for q, (s_, d_) in enumerate(zip(_pieces(part(_rank(group, *peer)), nchunks[a]),
                                                 _pieces(dst, nchunks[a]))):
                    cp = pltpu.make_async_remote_copy(
                        src_ref=s_, dst_ref=d_, send_sem=send_sems.at[(base[a] + q) * len(rels) + k],
                        recv_sem=recv_sems.at[(base[a] + q) * len(rels) + k],
                        device_id=peer, device_id_type=_MESH)
                    cp.start()
                    copies.append(cp)
        for cp in copies:
            cp.wait()

    any_spec = pl.BlockSpec(memory_space=pl.ANY)
    lead = (gsize,) if slotted else ()
    return pl.pallas_call(
        body, name=name,
        out_shape=[jax.ShapeDtypeStruct(lead + tuple(b), s.dtype) for b, s in zip(blks, srcs)],
        in_specs=[any_spec] * n, out_specs=[any_spec] * n,
        scratch_shapes=[pltpu.SemaphoreType.DMA((tot * len(rels),)), pltpu.SemaphoreType.DMA((tot * len(rels),)),
                        pltpu.SemaphoreType.DMA((tot,))],
    )(*srcs)


def _pair_fill(name, arrs, nchunks):
    n = len(arrs)
    base = [sum(nchunks[:a]) for a in range(n)]
    tot = sum(nchunks)

    def body(*refs):
        out_refs = refs[n:2 * n]
        send_sems, recv_sems = refs[2 * n:]
        x, y, c = lax.axis_index("x"), lax.axis_index("y"), lax.axis_index("c")
        copies = []
        for a in range(n):
            for q, blk in enumerate(_pieces(out_refs[a].at[c], nchunks[a])):
                cp = pltpu.make_async_remote_copy(
                    src_ref=blk, dst_ref=blk, send_sem=send_sems.at[base[a] + q], recv_sem=recv_sems.at[base[a] + q],
                    device_id=(x, y, 1 - c), device_id_type=_MESH)
                cp.start()
                copies.append(cp)
        for cp in copies:
            cp.wait()

    any_spec = pl.BlockSpec(memory_space=pl.ANY)
    return pl.pallas_call(
        body, name=name, out_shape=[jax.ShapeDtypeStruct(a.shape, a.dtype) for a in arrs],
        in_specs=[any_spec] * n, out_specs=[any_spec] * n, input_output_aliases={a: a for a in range(n)},
        scratch_shapes=[pltpu.SemaphoreType.DMA((tot,)), pltpu.SemaphoreType.DMA((tot,))],
    )(*arrs)


def _gather_weights(wb, nchunks):
    n = len(wb)
    rels = _GROUPS["xy"]
    base = [sum(nchunks[:a]) for a in range(n)]
    tot = sum(nchunks)

    def body(*refs):
        src_refs, out_refs = refs[:n], refs[n:2 * n]
        ici_send, ici_recv, d2d_send, d2d_recv, loc_sems = refs[2 * n:]
        x, y, c = lax.axis_index("x"), lax.axis_index("y"), lax.axis_index("c")
        me = 2 * x + y
        waits = []
        for a in range(n):
            for l in range(2):
                for q, (s_, d_) in enumerate(zip(_pieces(src_refs[a].at[l], nchunks[a]),
                                                 _pieces(out_refs[a].at[me, l], nchunks[a]))):
                    loc = pltpu.make_async_copy(s_, d_, loc_sems.at[(base[a] + q) * 2 + l])
                    loc.start()
                    waits.append(loc)
        first = []
        for a in range(n):
            for k, rel in enumerate(rels):
                px, py, _ = _flip(rel, x, y, c)
                for q, (s_, d_) in enumerate(zip(_pieces(src_refs[a].at[c], nchunks[a]),
                                                 _pieces(out_refs[a].at[me, c], nchunks[a]))):
                    sem = (base[a] + q) * 3 + k
                    cp = pltpu.make_async_remote_copy(src_ref=s_, dst_ref=d_, send_sem=ici_send.at[sem],
                                                      recv_sem=ici_recv.at[sem], device_id=(px, py, c),
                                                      device_id_type=_MESH)
                    cp.start()
                    first.append(cp)
        second = []
        for a in range(n):
            for k, rel in enumerate(rels):
                px, py, _ = _flip(rel, x, y, c)
                for q, blk in enumerate(_pieces(out_refs[a].at[2 * px + py, c], nchunks[a])):
                    sem = (base[a] + q) * 3 + k
                    landed = pltpu.make_async_remote_copy(src_ref=blk, dst_ref=blk, send_sem=ici_send.at[sem],
                                                          recv_sem=ici_recv.at[sem], device_id=(px, py, c),
                                                          device_id_type=_MESH)
                    landed.wait_recv()
                    cp = pltpu.make_async_remote_copy(src_ref=blk, dst_ref=blk, send_sem=d2d_send.at[sem],
                                                      recv_sem=d2d_recv.at[sem], device_id=(x, y, 1 - c),
                                                      device_id_type=_MESH)
                    cp.start()
                    second.append(cp)
        for cp in first:
            cp.wait_send()
        for cp in second:
            cp.wait_send()
        for a in range(n):
            for k, rel in enumerate(rels):
                px, py, _ = _flip(rel, x, y, c)
                for q, blk in enumerate(_pieces(out_refs[a].at[2 * px + py, 1 - c], nchunks[a])):
                    sem = (base[a] + q) * 3 + k
                    pltpu.make_async_remote_copy(src_ref=blk, dst_ref=blk, send_sem=d2d_send.at[sem],
                                                 recv_sem=d2d_recv.at[sem], device_id=(x, y, 1 - c),
                                                 device_id_type=_MESH).wait_recv()
        for cp in waits:
            cp.wait()

    any_spec = pl.BlockSpec(memory_space=pl.ANY)
    return pl.pallas_call(
        body, name="gather_weights",
        out_shape=[jax.ShapeDtypeStruct((N_CHIPS,) + a.shape, a.dtype) for a in wb],
        in_specs=[any_spec] * n, out_specs=[any_spec] * n,
        scratch_shapes=[pltpu.SemaphoreType.DMA((tot * 3,))] * 4 + [pltpu.SemaphoreType.DMA((tot * 2,))],
    )(*wb)


def _mm(name, a, b, out_sds, *, grid, a_spec, b_spec, o_spec, dims, acc_shape, into=None):
    nk = grid[2]

    def body(*refs):
        a_ref, b_ref = refs[0], refs[1]
        o_ref, acc = refs[-2], refs[-1]
        k = pl.program_id(2)
        part = lax.dot_general(a_ref[...].astype(_MXU), b_ref[...].astype(_MXU), dims,
                               preferred_element_type=_F32)
        if nk == 1:
            o_ref[...] = part.astype(o_ref.dtype)
            return

        @pl.when(k == 0)
        def _():
            acc[...] = part

        @pl.when(k > 0)
        def _():
            acc[...] += part

        @pl.when(k == nk - 1)
        def _():
            o_ref[...] = acc[...].astype(o_ref.dtype)

    if nk == 1:
        acc_shape = (8, 128)
    in_specs = [a_spec, b_spec]
    args = [a, b]
    aliases = {}
    if into is not None:
        in_specs.append(pl.BlockSpec(memory_space=pl.ANY))
        args.append(into)
        aliases = {2: 0}
    return pl.pallas_call(
        body, name=name, grid=grid, in_specs=in_specs, out_specs=o_spec, out_shape=out_sds,
        scratch_shapes=[pltpu.VMEM(acc_shape, _F32)], input_output_aliases=aliases,
        compiler_params=_params(3))(*args)


_NN = (((1,), (0,)), ((), ()))
_NT = (((1,), (1,)), ((), ()))
_TN = (((0,), (0,)), ((), ()))


def _rowwise(name, body, *, grid, ins, outs, scratch=()):
    return pl.pallas_call(
        body, name=name, grid=(grid,), in_specs=[s for _, s in ins], out_specs=[s for _, s in outs],
        out_shape=[o for o, _ in outs], scratch_shapes=list(scratch),
        compiler_params=_params(1))(*[a for a, _ in ins])


def _rows(tb, w, cb=0, n=None):
    if n is None:
        return pl.BlockSpec((tb, w), lambda i: (i, cb))
    return pl.BlockSpec((tb, w), lambda i: (n - 1 - i, cb))


def _vec(shape):
    return pl.BlockSpec(shape, lambda i: (0,) * len(shape))


def _halo_prev(tb, w, cb=0, n=None, rows=8):
    if n is None:
        return pl.BlockSpec((rows, w), lambda i: (jnp.maximum(i * (tb // rows) - 1, 0), cb))
    return pl.BlockSpec((rows, w), lambda i: (jnp.maximum((n - 1 - i) * (tb // rows) - 1, 0), cb))


def _halo_next(tb, w, n, cb=0):
    return pl.BlockSpec((8, w), lambda i: (jnp.minimum((i + 1) * (tb // 8), n * (tb // 8) - 1), cb))


def _sds(shape, dtype=_F32):
    return jax.ShapeDtypeStruct(shape, dtype)


def _cast(name, a, tb):
    rows, cols = a.shape

    def body(a_ref, o_ref):
        o_ref[...] = a_ref[...].astype(o_ref.dtype)

    return _rowwise(name, body, grid=rows // tb, ins=[(a, _rows(tb, cols))],
                    outs=[(_sds((rows, cols), _MXU), _rows(tb, cols))])[0]


def _sum_lead(name, a, tb):
    g, rows, cols = a.shape

    def body(a_ref, o_ref):
        acc = a_ref[0]
        for k in range(1, g):
            acc = acc + a_ref[k]
        o_ref[...] = acc

    return _rowwise(name, body, grid=rows // tb,
                    ins=[(a, pl.BlockSpec((g, tb, cols), lambda i: (0, i, 0)))],
                    outs=[(_sds((rows, cols)), _rows(tb, cols))])[0]


def _sum_pair(name, mine, theirs, core, tb):
    _, nj, rows, cols = mine.shape

    def body(s_ref, a_ref, b_ref, o_ref, ob_ref):
        t = a_ref[...] + b_ref[...]
        o_ref[...] = t
        ob_ref[...] = t.astype(ob_ref.dtype)

    blk = pl.BlockSpec((None, tb, cols), lambda j, i, s: (j, i, 0))
    grid_spec = pltpu.PrefetchScalarGridSpec(
        num_scalar_prefetch=1, grid=(nj, rows // tb),
        in_specs=[pl.BlockSpec((None, None, tb, cols), lambda j, i, s: (s[0], j, i, 0)), blk],
        out_specs=[blk, blk])
    return pl.pallas_call(body, name=name, grid_spec=grid_spec,
                          out_shape=[_sds((nj, rows, cols)), _sds((nj, rows, cols), _MXU)],
                          compiler_params=_params(2))(core, mine, theirs)


def _sum_chips(name, mine, theirs, where, tb):
    _, rows, cols = mine.shape

    def body(s_ref, a_ref, b1_ref, b2_ref, b3_ref, o_ref):
        o_ref[...] = ((a_ref[...] + b1_ref[...].astype(_F32)) + b2_ref[...].astype(_F32)) + b3_ref[...].astype(_F32)

    def slot(k):
        return pl.BlockSpec((None, tb, cols), lambda i, s: (jnp.bitwise_xor(s[0], k), i, 0))

    grid_spec = pltpu.PrefetchScalarGridSpec(
        num_scalar_prefetch=1, grid=(rows // tb,),
        in_specs=[slot(0), slot(1), slot(2), slot(3)],
        out_specs=pl.BlockSpec((None, tb, cols), lambda i, s: (s[1], i, 0)))
    return pl.pallas_call(body, name=name, grid_spec=grid_spec, out_shape=_sds((2, rows, cols)),
                          compiler_params=_params(1))(where, mine, theirs, theirs, theirs)


def _adamw(name, w, g, m, v, tb):
    rows, cols = w.shape
    c1 = 1.0 - ADAM_B1 ** ADAM_STEP
    c2 = 1.0 - ADAM_B2 ** ADAM_STEP

    def body(w_ref, g_ref, m_ref, v_ref, d_ref, nm_ref, nv_ref):
        gv = g_ref[...]
        nm = ADAM_B1 * m_ref[...] + (1.0 - ADAM_B1) * gv
        nv = ADAM_B2 * v_ref[...] + (1.0 - ADAM_B2) * (gv * gv)
        d_ref[...] = -ADAM_LR * ((nm / c1) / (jnp.sqrt(nv / c2) + ADAM_EPS) + ADAM_WD * w_ref[...])
        nm_ref[...] = nm
        nv_ref[...] = nv

    spec = _rows(tb, cols)
    return _rowwise(name, body, grid=rows // tb, ins=[(w, spec), (g, spec), (m, spec), (v, spec)],
                    outs=[(_sds((rows, cols)), spec)] * 3)


def _mod_fwd(c_all, w_mod, b_cols):
    cols = w_mod.shape[2]

    def body(c_ref, w_ref, b_ref, o_ref):
        cv = c_ref[...]
        sc = (cv * _sigmoid(cv)).astype(_MXU)
        o_ref[...] = jnp.dot(sc, w_ref[...].astype(_MXU), preferred_element_type=_F32) + b_ref[...]

    return pl.pallas_call(
        body, name="mod_fwd", grid=(2,),
        in_specs=[pl.BlockSpec((8, D), lambda l: (0, 0)), pl.BlockSpec((None, D, cols), lambda l: (l, 0, 0)),
                  pl.BlockSpec((None, 1, cols), lambda l: (l, 0, 0))],
        out_specs=pl.BlockSpec((None, 8, cols), lambda l: (l, 0, 0)),
        out_shape=_sds((2, 8, cols)), compiler_params=_params(1))(c_all, w_mod, b_cols)


def _mod_bwd(c_all_t, dm):
    cols = dm.shape[2]

    def body(c_ref, d_ref, o_ref):
        cv = c_ref[...]
        sc = (cv * _sigmoid(cv)).astype(_MXU)
        o_ref[...] = jnp.dot(sc, d_ref[...].astype(_MXU), preferred_element_type=_F32)

    return pl.pallas_call(
        body, name="mod_bwd", grid=(2,),
        in_specs=[pl.BlockSpec((D, 8), lambda l: (0, 0)), pl.BlockSpec((None, 8, cols), lambda l: (l, 0, 0))],
        out_specs=pl.BlockSpec((None, D, cols), lambda l: (l, 0, 0)),
        out_shape=_sds((2, D, cols)), compiler_params=_params(1))(c_all_t, dm)


def _prenorm_fwd(x, g_pre, shift, scale):
    s = x.shape[0]
    tb = 512

    def body(x_ref, g_ref, sh_ref, sc_ref, h_ref, ht_ref):
        xv = x_ref[...]
        rstd = lax.rsqrt(jnp.mean(xv * xv, axis=-1, keepdims=True) + NORM_EPS)
        hv = (xv * rstd) * g_ref[...] * (1.0 + sc_ref[...]) + sh_ref[...]
        h_ref[...] = hv.astype(h_ref.dtype)
        ht_ref[...] = hv.T.astype(ht_ref.dtype)

    v = _vec((1, D))
    return _rowwise("prenorm_fwd", body, grid=s // tb,
                    ins=[(x, _rows(tb, D)), (g_pre, v), (shift, v), (scale, v)],
                    outs=[(_sds((s, D), _MXU), _rows(tb, D)),
                          (_sds((D, s), _MXU), pl.BlockSpec((D, tb), lambda i: (0, i)))])


def _shift_down(cur, halo, j, tb):
    ext = jnp.concatenate([halo, cur], axis=0)
    return pltpu.roll(ext, j, 0)[8:8 + tb]


def _shift_up(cur, halo, j, tb):
    ext = jnp.concatenate([cur, halo], axis=0)
    return pltpu.roll(ext, tb + 8 - j, 0)[0:tb]


def _conv_fwd(proj, conv_w, conv_b):
    s = proj.shape[0]
    tb = 512

    def body(u_ref, hp_ref, w_ref, b_ref, o_ref):
        i = pl.program_id(0)
        u = u_ref[...].astype(_F32)
        halo = jnp.where(i > 0, hp_ref[...].astype(_F32)[8:16], 0.0)
        acc = b_ref[...] + u * w_ref[0:1, :]
        for j in range(1, 4):
            acc = acc + _shift_down(u, halo, j, tb) * w_ref[j:j + 1, :]
        o_ref[...] = acc

    return _rowwise("conv_fwd", body, grid=s // tb,
                    ins=[(proj, _rows(tb, D, CB_U)), (proj, _halo_prev(tb, D, CB_U, rows=16)),
                         (conv_w, _vec((4, D))), (conv_b, _vec((1, D)))],
                    outs=[(_sds((s, D)), _rows(tb, D))])[0]


def _lru_gates(pre_r, pre_i, uc, b_rg, b_ig, lam):
    r = _sigmoid(pre_r + b_rg)
    ig = _sigmoid(pre_i + b_ig)
    nl = -lam
    sp = jnp.maximum(nl, 0.0) + jnp.log(1.0 + jnp.exp(-jnp.abs(nl)))
    la = -LRU_C * r * sp
    a = jnp.exp(la)
    sq = jnp.sqrt(-jnp.tanh(la) * (a * a + 1.0))
    return r, ig, sp, a, sq


def _scan_fwd(pre, uc, b_rg, b_ig, lam):
    s = uc.shape[0]
    tb = 256

    def body(pr_ref, pi_ref, uc_ref, brg_ref, big_ref, lam_ref, h_ref, carry, a_s, b_s):
        i = pl.program_id(0)

        @pl.when(i == 0)
        def _():
            carry[...] = jnp.zeros_like(carry)

        ucv = uc_ref[...]
        _, ig, _, a, sq = _lru_gates(pr_ref[...], pi_ref[...], ucv, brg_ref[...], big_ref[...], lam_ref[...])
        av = a
        bv = sq * (ig * ucv)
        av = av.reshape(tb // 8, 8, D)
        bv = bv.reshape(tb // 8, 8, D)
        row8 = lax.broadcasted_iota(jnp.int32, (1, 8, 1), 1)
        for sh in (1, 2, 4):
            m = row8 >= sh
            b_sh = pltpu.roll(bv, sh, 1)
            a_sh = pltpu.roll(av, sh, 1)
            bv = jnp.where(m, av * b_sh + bv, bv)
            av = jnp.where(m, av * a_sh, av)
        a_s[...] = av.reshape(tb, D)
        b_s[...] = bv.reshape(tb, D)

        def tile(t, state):
            rows = pl.ds(pl.multiple_of(t * 8, 8), 8)
            hv = b_s[rows, :] + a_s[rows, :] * state
            h_ref[rows, :] = hv
            return jnp.broadcast_to(hv[7:8, :], (8, D))

        carry[...] = lax.fori_loop(0, tb // 8, tile, jnp.broadcast_to(carry[7:8, :], (8, D)), unroll=4)

    v = _vec((1, D))
    return _rowwise("scan_fwd", body, grid=s // tb,
                    ins=[(pre, _rows(tb, D, 0)), (pre, _rows(tb, D, 1)), (uc, _rows(tb, D)),
                         (b_rg, v), (b_ig, v), (lam, v)],
                    outs=[(_sds((s, D)), _rows(tb, D))],
                    scratch=[pltpu.VMEM((8, D), _F32), pltpu.VMEM((tb, D), _F32), pltpu.VMEM((tb, D), _F32)])[0]


def _weight_specs(l):
    return [pl.BlockSpec((N_CHIPS, None, ATT_W, 256), lambda i: (0, l, 0, 0)),
            pl.BlockSpec((N_CHIPS, None, 256, D), lambda i: (0, l, 0, 0)),
            pl.BlockSpec((N_CHIPS, None, 256, D), lambda i: (0, l, 0, 0))]


def _tail_fwd(l, o, h_lru, proj, x, gate, g_post, gw):
    s = x.shape[0]
    tb = 512

    def body(o_ref, h_ref, ga_ref, gl_ref, ma_ref, mb_ref, x_ref, gt_ref, gp_ref, wpa_ref, wpb_ref, wo_ref,
             aa_ref, ba_ref, ya_ref, yb_ref, z_ref, out_ref, xn_ref):
        ga = ga_ref[...].astype(_F32)
        aa = (o_ref[...] * (ga * _sigmoid(ga))).astype(_MXU)
        aa_ref[...] = aa
        gl = gl_ref[...].astype(_F32)
        ba = (h_ref[...] * (gl * _sigmoid(gl))).astype(_MXU)
        ba_ref[...] = ba
        for j in range(N_CHIPS):
            ya_ref[:, j * 256:(j + 1) * 256] = jnp.dot(aa, wpa_ref[j], preferred_element_type=_F32)
        yb = jnp.dot(ba, wpb_ref[...].reshape(D, D), preferred_element_type=_F32)
        yb_ref[...] = yb
        z = (_sigmoid(ma_ref[...].astype(_F32)) * ya_ref[...]
             + _sigmoid(mb_ref[...].astype(_F32)) * yb).astype(z_ref.dtype)
        z_ref[...] = z
        ov = jnp.dot(z, wo_ref[...].reshape(D, D), preferred_element_type=_F32)
        out_ref[...] = ov
        rstd = lax.rsqrt(jnp.mean(ov * ov, axis=-1, keepdims=True) + NORM_EPS)
        xn_ref[...] = x_ref[...] + gt_ref[...] * ((ov * rstd) * gp_ref[...])

    v = _vec((1, D))
    r = _rows(tb, D)
    r5 = _rows(tb, ATT_W)
    return _rowwise("tail_fwd", body, grid=s // tb,
                    ins=[(o, r5), (h_lru, r), (proj, _rows(tb, ATT_W, CB_GATT)), (proj, _rows(tb, D, CB_GLRU)),
                         (proj, _rows(tb, D, CB_MA)), (proj, _rows(tb, D, CB_MB)), (x, r), (gate, v), (g_post, v)]
                    + list(zip((gw["w_pa"], gw["w_pb"], gw["w_o"]), _weight_specs(l))),
                    outs=[(_sds((s, ATT_W), _MXU), r5), (_sds((s, D), _MXU), r), (_sds((s, D)), r),
                          (_sds((s, D)), r), (_sds((s, D), _MXU), r), (_sds((s, D)), r), (_sds((s, D)), r)])


def _loss_head(y, target):
    s = y.shape[0]
    tb = 512

    def body(y_ref, t_ref, dy_ref, acc_ref):
        i = pl.program_id(0)

        @pl.when(i == 0)
        def _():
            acc_ref[...] = jnp.zeros_like(acc_ref)

        err = y_ref[...] - t_ref[...]
        dy_ref[...] = err * (1.0 / D)
        acc_ref[...] += jnp.sum(err * err, axis=0, keepdims=True)

    return _rowwise("loss_head", body, grid=s // tb,
                    ins=[(y, _rows(tb, D)), (target, _rows(tb, D))],
                    outs=[(_sds((s, D)), _rows(tb, D)), (_sds((1, D)), _vec((1, D)))])


def _accumulate(i, ref, val):
    @pl.when(i == 0)
    def _():
        ref[...] = val

    @pl.when(i > 0)
    def _():
        ref[...] += val


def _tail_bwd(l, dx, out, y_a, y_b, proj, o, h_lru, gate, g_post, gw):
    s = dx.shape[0]
    tb = 256

    def body(dx_ref, out_ref, ya_ref, yb_ref, ma_ref, mb_ref, o_ref, ga_ref, h_ref, gl_ref, gt_ref, gp_ref,
             wpa_ref, wpb_ref, wo_ref,
             dout_ref, dya_ref, dyb_ref, rest_ref, do_ref, dh_ref, dgt_ref, dgp_ref):
        i = pl.program_id(0)
        ov = out_ref[...]
        dxv = dx_ref[...]
        rstd = lax.rsqrt(jnp.mean(ov * ov, axis=-1, keepdims=True) + NORM_EPS)
        nv = ov * rstd
        s_dn = jnp.sum(dxv * nv, axis=0, keepdims=True)
        _accumulate(i, dgt_ref, s_dn * gp_ref[...])
        _accumulate(i, dgp_ref, s_dn * gt_ref[...])
        dn = dxv * (gt_ref[...] * gp_ref[...])
        d_out = (rstd * (dn - nv * jnp.mean(dn * nv, axis=-1, keepdims=True))).astype(_MXU)
        dout_ref[...] = d_out
        dz = lax.dot_general(d_out, wo_ref[...].reshape(D, D), _NT, preferred_element_type=_F32)
        ga = _sigmoid(ma_ref[...].astype(_F32))
        gb = _sigmoid(mb_ref[...].astype(_F32))
        dya = (dz * ga).astype(_MXU)
        dyb = (dz * gb).astype(_MXU)
        dya_ref[...] = dya
        dyb_ref[...] = dyb
        rest_ref[:, R_MA:R_MB] = (dz * ya_ref[...] * ga * (1.0 - ga)).astype(rest_ref.dtype)
        rest_ref[:, R_MB:R_END] = (dz * yb_ref[...] * gb * (1.0 - gb)).astype(rest_ref.dtype)
        daa = lax.dot_general(dya[:, 0:256], wpa_ref[0], _NT, preferred_element_type=_F32)
        for j in range(1, N_CHIPS):
            daa = daa + lax.dot_general(dya[:, j * 256:(j + 1) * 256], wpa_ref[j], _NT, preferred_element_type=_F32)
        dba = lax.dot_general(dyb, wpb_ref[...].reshape(D, D), _NT, preferred_element_type=_F32)
        gav = ga_ref[...].astype(_F32)
        sa = _sigmoid(gav)
        do_ref[...] = daa * (gav * sa)
        rest_ref[:, 0:R_U] = (daa * o_ref[...] * (sa * (1.0 + gav * (1.0 - sa)))).astype(rest_ref.dtype)
        gl = gl_ref[...].astype(_F32)
        sl = _sigmoid(gl)
        dh_ref[...] = dba * (gl * sl)
        rest_ref[:, R_GLRU:R_MA] = (dba * h_ref[...] * (sl * (1.0 + gl * (1.0 - sl)))).astype(rest_ref.dtype)

    v = _vec((1, D))
    r5, r10 = _rows(tb, ATT_W), _rows(tb, D)
    return _rowwise("tail_bwd", body, grid=s // tb,
                    ins=[(dx, r10), (out, r10), (y_a, r10), (y_b, r10), (proj, _rows(tb, D, CB_MA)),
                         (proj, _rows(tb, D, CB_MB)), (o, r5), (proj, _rows(tb, ATT_W, CB_GATT)), (h_lru, r10),
                         (proj, _rows(tb, D, CB_GLRU)), (gate, v), (g_post, v)]
                    + list(zip((gw["w_pa"], gw["w_pb"], gw["w_o"]), _weight_specs(l))),
                    outs=[(_sds((s, D), _MXU), r10), (_sds((s, D), _MXU), r10), (_sds((s, D), _MXU), r10),
                          (_sds((s, R_END), _MXU), _rows(tb, R_END)),
                          (_sds((s, ATT_W)), r5), (_sds((s, D)), r10), (_sds((1, D)), v), (_sds((1, D)), v)])


def _scan_bwd(dh, pre, uc, h_lru, b_rg, b_ig, lam):
    s = uc.shape[0]
    tb = 256
    n = s // tb

    def body(dh_ref, pr_ref, pi_ref, uc_ref, h_ref, hp_ref, brg_ref, big_ref, lam_ref,
             dpre_ref, duc_ref, dbrg_ref, dbig_ref, dlam_ref, carry, c_s, g_s):
        i = pl.program_id(0)

        @pl.when(i == 0)
        def _():
            carry[...] = jnp.zeros_like(carry)

        ucv = uc_ref[...]
        r, ig, sp, a, sq = _lru_gates(pr_ref[...], pi_ref[...], ucv, brg_ref[...], big_ref[...], lam_ref[...])
        row = lax.broadcasted_iota(jnp.int32, (tb, 1), 0)
        cv = jnp.where(row == tb - 1, 1.0, pltpu.roll(a, tb - 1, 0))
        gv = dh_ref[...]
        cv = cv.reshape(tb // 8, 8, D)
        gv = gv.reshape(tb // 8, 8, D)
        row8 = lax.broadcasted_iota(jnp.int32, (1, 8, 1), 1)
        for sh in (1, 2, 4):
            m = row8 < 8 - sh
            g_sh = pltpu.roll(gv, 8 - sh, 1)
            c_sh = pltpu.roll(cv, 8 - sh, 1)
            gv = jnp.where(m, gv + cv * g_sh, gv)
            cv = jnp.where(m, cv * c_sh, cv)
        c_s[...] = cv.reshape(tb, D)
        g_s[...] = gv.reshape(tb, D)

        def tile(k, state):
            rows = pl.ds(pl.multiple_of((tb // 8 - 1 - k) * 8, 8), 8)
            gt = g_s[rows, :] + c_s[rows, :] * state
            g_s[rows, :] = gt
            return jnp.broadcast_to(gt[0:1, :], (8, D))

        lax.fori_loop(0, tb // 8, tile, jnp.broadcast_to(carry[0:1, :], (8, D)), unroll=4)
        gv = g_s[...]
        carry[...] = (a * gv)[0:8]

        halo = jnp.where(i < n - 1, hp_ref[...], 0.0)
        h_prev = _shift_down(h_ref[...], halo, 1, tb)
        d_a = gv * h_prev
        d_sq = gv * (ig * ucv)
        d_i = gv * sq * ucv
        duc_ref[...] = gv * sq * ig
        d_la = d_a * a - d_sq * (a * a) / sq
        d_r = d_la * (-LRU_C * sp)
        d_pre_r = d_r * r * (1.0 - r)
        d_pre_i = d_i * ig * (1.0 - ig)
        dpre_ref[:, 0:D] = d_pre_r.astype(dpre_ref.dtype)
        dpre_ref[:, D:2 * D] = d_pre_i.astype(dpre_ref.dtype)
        _accumulate(i, dbrg_ref, jnp.sum(d_pre_r, axis=0, keepdims=True))
        _accumulate(i, dbig_ref, jnp.sum(d_pre_i, axis=0, keepdims=True))
        lamv = lam_ref[...]
        _accumulate(i, dlam_ref, jnp.sum(d_la * (-LRU_C * r), axis=0, keepdims=True) * (-_sigmoid(-lamv)))

    v = _vec((1, D))
    rv = _rows(tb, D, 0, n)
    return _rowwise("scan_bwd", body, grid=n,
                    ins=[(dh, rv), (pre, _rows(tb, D, 0, n)), (pre, _rows(tb, D, 1, n)), (uc, rv), (h_lru, rv),
                         (h_lru, _halo_prev(tb, D, 0, n)), (b_rg, v), (b_ig, v), (lam, v)],
                    outs=[(_sds((s, 2 * D), _MXU), _rows(tb, 2 * D, 0, n)), (_sds((s, D)), rv),
                          (_sds((1, D)), v), (_sds((1, D)), v), (_sds((1, D)), v)],
                    scratch=[pltpu.VMEM((8, D), _F32), pltpu.VMEM((tb, D), _F32), pltpu.VMEM((tb, D), _F32)])


def _conv_bwd(duc_a, duc_b, proj, conv_w, rest):
    s = duc_a.shape[0]
    tb = 512
    n = s // tb
    hw = D // 2

    def body(da_ref, db_ref, dan_ref, dbn_ref, u_ref, up_ref, w_ref, rest_in, du_ref, dw_ref, dbias_ref):
        i = pl.program_id(1)
        duc = da_ref[...] + db_ref[...]
        nxt = jnp.where(i < n - 1, dan_ref[...] + dbn_ref[...], 0.0)
        u = u_ref[...].astype(_F32)
        halo = jnp.where(i > 0, up_ref[...].astype(_F32)[8:16], 0.0)
        du = duc * w_ref[0:1, :]
        dws = [jnp.sum(duc * u, axis=0, keepdims=True)]
        for j in range(1, 4):
            du = du + _shift_up(duc, nxt, j, tb) * w_ref[j:j + 1, :]
            dws.append(jnp.sum(duc * _shift_down(u, halo, j, tb), axis=0, keepdims=True))
        du_ref[...] = du.astype(du_ref.dtype)
        for j in range(4):
            _accumulate(i, dw_ref.at[j:j + 1, :], dws[j])
        _accumulate(i, dbias_ref, jnp.sum(duc, axis=0, keepdims=True))

    r = pl.BlockSpec((tb, hw), lambda h, i: (i, h))
    nxt_spec = pl.BlockSpec((8, hw), lambda h, i: (jnp.minimum((i + 1) * (tb // 8), n * (tb // 8) - 1), h))
    return pl.pallas_call(
        body, name="conv_bwd", grid=(2, n),
        in_specs=[r, r, nxt_spec, nxt_spec,
                  pl.BlockSpec((tb, hw), lambda h, i: (i, 2 * CB_U + h)),
                  pl.BlockSpec((16, hw), lambda h, i: (jnp.maximum(i * (tb // 16) - 1, 0), 2 * CB_U + h)),
                  pl.BlockSpec((4, hw), lambda h, i: (0, h)), pl.BlockSpec(memory_space=pl.ANY)],
        out_specs=[pl.BlockSpec((tb, hw), lambda h, i: (i, R_U // hw + h)),
                   pl.BlockSpec((4, hw), lambda h, i: (0, h)), pl.BlockSpec((1, hw), lambda h, i: (0, h))],
        out_shape=[_sds(rest.shape, rest.dtype), _sds((4, D)), _sds((1, D))],
        input_output_aliases={7: 0}, compiler_params=_params(2),
    )(duc_a, duc_b, duc_a, duc_b, proj, proj, conv_w, rest)


def _prenorm_bwd(dh, x, dx_out, g_pre, scale):
    s = x.shape[0]
    tb = 512

    def body(dh_ref, x_ref, dxo_ref, g_ref, sc_ref, dx_ref, dsh_ref, dsc_ref, dg_ref):
        i = pl.program_id(0)
        xv = x_ref[...]
        dhv = dh_ref[...]
        rstd = lax.rsqrt(jnp.mean(xv * xv, axis=-1, keepdims=True) + NORM_EPS)
        xn = xv * rstd
        one_sc = 1.0 + sc_ref[...]
        s1 = jnp.sum(dhv * xn, axis=0, keepdims=True)
        _accumulate(i, dsh_ref, jnp.sum(dhv, axis=0, keepdims=True))
        _accumulate(i, dsc_ref, s1 * g_ref[...])
        _accumulate(i, dg_ref, s1 * one_sc)
        dxn = dhv * (g_ref[...] * one_sc)
        dx_ref[...] = dxo_ref[...] + rstd * (dxn - xn * jnp.mean(dxn * xn, axis=-1, keepdims=True))

    v = _vec((1, D))
    r = _rows(tb, D)
    return _rowwise("prenorm_bwd", body, grid=s // tb,
                    ins=[(dh, r), (x, r), (dx_out, r), (g_pre, v), (scale, v)],
                    outs=[(_sds((s, D)), r), (_sds((1, D)), v), (_sds((1, D)), v), (_sds((1, D)), v)])


def _band_tiles(dil):
    tiles = []
    for rho in range(dil):
        for b in range(16 // dil):
            qs = rho + dil * BAND * b
            tiles.append((qs, QBLK + qs - dil * BAND, b))
    return tiles


def _strided(start, size, dil):
    return pl.ds(start, size, stride=dil) if dil > 1 else pl.ds(start, size)


def _band_mask(i, b):
    qi = lax.broadcasted_iota(jnp.int32, (BAND, 2 * BAND), 0)
    ki = lax.broadcasted_iota(jnp.int32, (BAND, 2 * BAND), 1)
    valid = (ki >= qi) & (ki <= qi + BAND)
    if b == 0:
        valid = valid & ((ki >= BAND) | (i > 0))
    return valid


def _attn_fwd(proj):
    s = proj.shape[0]
    n = s // QBLK
    scale = HEAD ** -0.5

    def body(*refs):
        q_refs, kp_refs, kc_refs, vp_refs, vc_refs = (refs[3 * t:3 * t + 3] for t in range(5))
        o_ref, lse_ref, qbuf, kbuf, vbuf = refs[15:20]
        accs, maxs, dens = refs[20:23], refs[23:26], refs[26:29]
        i = pl.program_id(1)
        for g, dil in enumerate(DILATIONS):
            qbuf[...] = q_refs[g][...].astype(_F32)
            kbuf[0:QBLK, :] = kp_refs[g][...].astype(_F32)
            kbuf[QBLK:2 * QBLK, :] = kc_refs[g][...].astype(_F32)
            vbuf[0:QBLK, :] = vp_refs[g][...].astype(_F32)
            vbuf[QBLK:2 * QBLK, :] = vc_refs[g][...].astype(_F32)
            for qs, ks, b in _band_tiles(dil):
                qsl = _strided(qs, BAND, dil)
                q = qbuf[qsl, :].astype(_MXU)
                kk = kbuf[_strided(ks, 2 * BAND, dil), :].astype(_MXU)
                vv = vbuf[_strided(ks, 2 * BAND, dil), :].astype(_MXU)
                sc = lax.dot_general(q, kk, _NT, preferred_element_type=_F32) * scale
                sc = jnp.where(_band_mask(i, b), sc, NEG_INF)
                m = jnp.max(sc, axis=-1, keepdims=True)
                p = jnp.exp(sc - m)
                accs[g][qsl, :] = jnp.dot(p.astype(_MXU), vv, preferred_element_type=_F32)
                maxs[g][qsl, :] = jnp.broadcast_to(m, (BAND, HEAD))
                dens[g][qsl, :] = jnp.broadcast_to(jnp.sum(p, axis=-1, keepdims=True), (BAND, HEAD))
        ms = [r[...] for r in maxs]
        mx = jnp.maximum(jnp.maximum(ms[0], ms[1]), ms[2])
        ws = [jnp.exp(m - mx) for m in ms]
        den = ws[0] * dens[0][...] + ws[1] * dens[1][...] + ws[2] * dens[2][...]
        o_ref[...] = (ws[0] * accs[0][...] + ws[1] * accs[1][...] + ws[2] * accs[2][...]) / den
        lse_ref[...] = mx + jnp.log(den)

    blk = (QBLK, HEAD)

    def spec(first_col, lag):
        specs = []
        for g in range(3):
            col = first_col + g * HEADS
            if lag:
                specs.append(pl.BlockSpec(blk, lambda j, i, col=col: (jnp.maximum(i - 1, 0), col + j)))
            else:
                specs.append(pl.BlockSpec(blk, lambda j, i, col=col: (i, col + j)))
        return specs

    out_spec = pl.BlockSpec(blk, lambda j, i: (i, j))
    return pl.pallas_call(
        body, name="attn_fwd", grid=(HEADS, n),
        in_specs=spec(0, False) + spec(12, True) + spec(12, False) + spec(24, True) + spec(24, False),
        out_specs=[out_spec] * 2, out_shape=[_sds((s, ATT_W))] * 2,
        scratch_shapes=[pltpu.VMEM(blk, _F32)] + [pltpu.VMEM((2 * QBLK, HEAD), _F32)] * 2
        + [pltpu.VMEM(blk, _F32)] * 9,
        compiler_params=_params(2))(*([proj] * 15))


def _attn_bwd(proj, d_o, o, lse, g, into):
    s = proj.shape[0]
    dil = DILATIONS[g]
    n = s // QBLK
    scale = HEAD ** -0.5
    tiles = _band_tiles(dil)

    def body(*refs):
        q_ref, kp_ref, kc_ref, vp_ref, vc_ref, do_ref, o_ref, lse_ref = refs[0:8]
        dq_ref, dk_ref, dv_ref, kbuf, vbuf, dkbuf, dvbuf, dqbuf, qbuf = refs[-9:]
        i = pl.program_id(1)

        @pl.when(i == 0)
        def _():
            dkbuf[0:QBLK, :] = jnp.zeros((QBLK, HEAD), _F32)
            dvbuf[0:QBLK, :] = jnp.zeros((QBLK, HEAD), _F32)

        @pl.when(i < n)
        def _():
            qbuf[...] = q_ref[...].astype(_F32)
            kbuf[0:QBLK, :] = kp_ref[...].astype(_F32)
            kbuf[QBLK:2 * QBLK, :] = kc_ref[...].astype(_F32)
            vbuf[0:QBLK, :] = vp_ref[...].astype(_F32)
            vbuf[QBLK:2 * QBLK, :] = vc_ref[...].astype(_F32)
            dkbuf[QBLK:2 * QBLK, :] = jnp.zeros((QBLK, HEAD), _F32)
            dvbuf[QBLK:2 * QBLK, :] = jnp.zeros((QBLK, HEAD), _F32)
            for qs, ks, b in tiles:
                qsl = _strided(qs, BAND, dil)
                ksl = _strided(ks, 2 * BAND, dil)
                q = qbuf[qsl, :].astype(_MXU)
                kk = kbuf[ksl, :].astype(_MXU)
                vv = vbuf[ksl, :].astype(_MXU)
                dov = do_ref[qsl, :]
                dd = jnp.sum(dov * o_ref[qsl, :], axis=-1, keepdims=True)
                lse_t = lse_ref[qsl, :][:, 0:1]
                sc = lax.dot_general(q, kk, _NT, preferred_element_type=_F32) * scale
                p = jnp.where(_band_mask(i, b), jnp.exp(sc - lse_t), 0.0)
                dob = dov.astype(_MXU)
                dp = lax.dot_general(dob, vv, _NT, preferred_element_type=_F32)
                ds = (p * (dp - dd) * scale).astype(_MXU)
                dqbuf[qsl, :] = jnp.dot(ds, kk, preferred_element_type=_F32)
                dkbuf[ksl, :] += lax.dot_general(ds, q, _TN, preferred_element_type=_F32)
                dvbuf[ksl, :] += lax.dot_general(p.astype(_MXU), dob, _TN, preferred_element_type=_F32)
            dq_ref[...] = dqbuf[...].astype(dq_ref.dtype)

        dk_ref[...] = dkbuf[0:QBLK, :].astype(dk_ref.dtype)
        dv_ref[...] = dvbuf[0:QBLK, :].astype(dv_ref.dtype)
        dkbuf[0:QBLK, :] = dkbuf[QBLK:2 * QBLK, :]
        dvbuf[0:QBLK, :] = dvbuf[QBLK:2 * QBLK, :]

    blk = (QBLK, HEAD)
    cq, ck, cv = g * HEADS, 12 + g * HEADS, 24 + g * HEADS

    def cur(i):
        return jnp.minimum(i, n - 1)

    def prev(i):
        return jnp.maximum(jnp.minimum(i, n - 1) - 1, 0)

    own = pl.BlockSpec(blk, lambda j, i: (cur(i), j))
    own_out = pl.BlockSpec(blk, lambda j, i: (cur(i), cq + j))
    late_out = pl.BlockSpec(blk, lambda j, i: (jnp.maximum(i - 1, 0), cq + j))
    extra = [] if into is None else list(into)
    return pl.pallas_call(
        body, name="attn_bwd_d%d" % dil, grid=(HEADS, n + 1),
        in_specs=[pl.BlockSpec(blk, lambda j, i: (cur(i), cq + j)),
                  pl.BlockSpec(blk, lambda j, i: (prev(i), ck + j)),
                  pl.BlockSpec(blk, lambda j, i: (cur(i), ck + j)),
                  pl.BlockSpec(blk, lambda j, i: (prev(i), cv + j)),
                  pl.BlockSpec(blk, lambda j, i: (cur(i), cv + j)),
                  own, own, own] + [pl.BlockSpec(memory_space=pl.ANY)] * len(extra),
        out_specs=[own_out, late_out, late_out], out_shape=[_sds((s, QKV_W), _MXU)] * 3,
        input_output_aliases={8 + t: t for t in range(len(extra))},
        scratch_shapes=[pltpu.VMEM((2 * QBLK, HEAD), _F32)] * 4 + [pltpu.VMEM((QBLK, HEAD), _F32)] * 2,
        compiler_params=_params(2))(proj, proj, proj, proj, proj, d_o, o, lse, *extra)


def _block_diag(w):
    eye = jnp.eye(16, dtype=w.dtype)
    return jnp.einsum("hij,hg->higj", w, eye).reshape(D, D)


def _diag_blocks(gd):
    g4 = gd.reshape(16, 64, 16, 64)
    keep = jnp.eye(16, dtype=jnp.bool_)[:, None, :, None]
    return jnp.sum(jnp.where(keep, g4, 0.0), axis=2)


_PARTS = ((0, 2), (2, 2), (4, 2), (6, 6))
_CHUNK = 768


def _d_h(l, parts, w_in):
    s = parts[0].shape[0]
    nk = IN_W // _CHUNK

    def body(p0, p1, p2, p3, w_ref, o_ref, acc):
        k = pl.program_id(2)

        @pl.when(k == 0)
        def _():
            acc[...] = jnp.zeros_like(acc)

        for p_ref, (first, cnt) in zip((p0, p1, p2, p3), _PARTS):
            @pl.when((k >= first) & (k < first + cnt))
            def _(p_ref=p_ref):
                acc[...] += lax.dot_general(p_ref[...].astype(_MXU), w_ref[...], _NT, preferred_element_type=_F32)

        @pl.when(k == nk - 1)
        def _():
            o_ref[...] = acc[...]

    def part_spec(first, cnt):
        return pl.BlockSpec((1024, _CHUNK), lambda m, n, k: (m, jnp.clip(k - first, 0, cnt - 1)))

    return pl.pallas_call(
        body, name="d_h", grid=(s // 1024, 1, nk),
        in_specs=[part_spec(*p) for p in _PARTS]
        + [pl.BlockSpec((None, None, D, _CHUNK), lambda m, n, k: (k // 3, l, 0, k % 3))],
        out_specs=pl.BlockSpec((1024, D), lambda m, n, k: (m, 0)), out_shape=_sds((s, D)),
        scratch_shapes=[pltpu.VMEM((1024, D), _F32)], compiler_params=_params(3))(*parts, w_in)


def _g_w_in(l, h_t, parts, into):
    s = h_t.shape[1]
    nk = s // 1024

    def body(*refs):
        h_ref, p_refs = refs[0], refs[1:5]
        o_ref, acc = refs[-2], refs[-1]
        n = pl.program_id(1)
        k = pl.program_id(2)

        @pl.when(k == 0)
        def _():
            acc[...] = jnp.zeros_like(acc)

        for p_ref, (first, cnt) in zip(p_refs, _PARTS):
            @pl.when((n >= first) & (n < first + cnt))
            def _(p_ref=p_ref):
                acc[...] += jnp.dot(h_ref[...], p_ref[...].astype(_MXU), preferred_element_type=_F32)

        @pl.when(k == nk - 1)
        def _():
            o_ref[...] = acc[...]

    def part_spec(first, cnt):
        def index(m, n, k):
            row = jnp.where(n < first, 0, jnp.where(n >= first + cnt, nk - 1, k))
            return (row, jnp.clip(n - first, 0, cnt - 1))
        return pl.BlockSpec((1024, _CHUNK), index)

    extra = [] if into is None else [into]
    return pl.pallas_call(
        body, name="g_w_in", grid=(1, IN_W // _CHUNK, nk),
        in_specs=[pl.BlockSpec((D, 1024), lambda m, n, k: (0, k))] + [part_spec(*p) for p in _PARTS]
        + [pl.BlockSpec(memory_space=pl.ANY)] * len(extra),
        out_specs=pl.BlockSpec((None, None, D, _CHUNK), lambda m, n, k: (l, n // 3, 0, n % 3)),
        out_shape=_sds((2, N_CHIPS, D, 2304)), input_output_aliases={5: 0} if extra else {},
        scratch_shapes=[pltpu.VMEM((D, _CHUNK), _F32)], compiler_params=_params(3))(h_t, *parts, *extra)


def _layer_fwd(l, x, p, gw):
    s = x.shape[0]
    nm = s // 1024
    h, h_t = _prenorm_fwd(x, p["g_pre"], p["shift"], p["scale"])
    proj = _mm("proj", h, gw["w_in"], _sds((s, IN_W), _MXU), grid=(nm, N_CHIPS, 1),
               a_spec=pl.BlockSpec((1024, D), lambda m, n, k: (m, 0)),
               b_spec=pl.BlockSpec((None, None, D, 2304), lambda m, n, k: (n, l, 0, 0)),
               o_spec=pl.BlockSpec((1024, 2304), lambda m, n, k: (m, n)), dims=_NN, acc_shape=(1024, 2304))
    o, lse = _attn_fwd(proj)
    uc = _conv_fwd(proj, p["conv_w"], p["conv_b"])
    pre = _mm("lru_gates", uc, p["w_gates"], _sds((s, 2 * D)), grid=(nm, 1, 1),
              a_spec=pl.BlockSpec((1024, D), lambda m, n, k: (m, 0)),
              b_spec=pl.BlockSpec((D, 2 * D), lambda m, n, k: (0, 0)),
              o_spec=pl.BlockSpec((1024, 2 * D), lambda m, n, k: (m, 0)), dims=_NN, acc_shape=(1024, 2 * D))
    h_lru = _scan_fwd(pre, uc, p["b_rg"], p["b_ig"], p["lam"])
    a_att, b_act, y_a, y_b, z, out, x_new = _tail_fwd(l, o, h_lru, proj, x, p["gate"], p["g_post"], gw)
    saved = dict(x=x, h_t=h_t, proj=proj, o=o, lse=lse, uc=uc, pre=pre, h_lru=h_lru, a_att=a_att, b_act=b_act,
                 y_a=y_a, y_b=y_b, z=z, out=out)
    return x_new, saved


def _layer_bwd(l, dx, p, gw, sv, big):
    s = dx.shape[0]
    nm = s // 1024
    nt = s // 2048
    proj = sv["proj"]
    d_out, dy_a, dy_b, d_rest, d_o, dh_lru, d_gate, d_gpost = _tail_bwd(
        l, dx, sv["out"], sv["y_a"], sv["y_b"], proj, sv["o"], sv["h_lru"], p["gate"], p["g_post"], gw)

    def wgrad_rows(name, a, b, into):
        return _mm(name, a, b, _sds((2, N_CHIPS, 256, D)), grid=(4, 1, nt),
                   a_spec=pl.BlockSpec((2048, 256), lambda m, n, k: (k, m)),
                   b_spec=pl.BlockSpec((2048, D), lambda m, n, k: (k, 0)),
                   o_spec=pl.BlockSpec((None, None, 256, D), lambda m, n, k: (l, m, 0, 0)),
                   dims=_TN, acc_shape=(256, D), into=into)

    big = dict(big)
    big["w_o"] = wgrad_rows("g_w_o", sv["z"], d_out, big.get("w_o"))
    big["w_pa"] = _mm("g_w_pa", sv["a_att"], dy_a, _sds((2, N_CHIPS, ATT_W, 256)), grid=(1, 4, nt),
                      a_spec=pl.BlockSpec((2048, ATT_W), lambda m, n, k: (k, 0)),
                      b_spec=pl.BlockSpec((2048, 256), lambda m, n, k: (k, n)),
                      o_spec=pl.BlockSpec((None, None, ATT_W, 256), lambda m, n, k: (l, n, 0, 0)),
                      dims=_TN, acc_shape=(ATT_W, 256), into=big.get("w_pa"))
    big["w_pb"] = wgrad_rows("g_w_pb", sv["b_act"], dy_b, big.get("w_pb"))
    d_pre, duc_dir, d_brg, d_big, d_lam = _scan_bwd(dh_lru, sv["pre"], sv["uc"], sv["h_lru"],
                                                   p["b_rg"], p["b_ig"], p["lam"])
    duc_mm = _mm("d_uc", d_pre, p["w_gates"], _sds((s, D)), grid=(nm, 1, 1),
                 a_spec=pl.BlockSpec((1024, 2 * D), lambda m, n, k: (m, 0)),
                 b_spec=pl.BlockSpec((D, 2 * D), lambda m, n, k: (0, 0)),
                 o_spec=pl.BlockSpec((1024, D), lambda m, n, k: (m, 0)), dims=_NT, acc_shape=(1024, D))
    g_gates = _mm("g_w_gates", sv["uc"], d_pre, _sds((D, 2 * D)), grid=(1, 2, s // 1024),
                  a_spec=pl.BlockSpec((1024, D), lambda m, n, k: (k, 0)),
                  b_spec=pl.BlockSpec((1024, D), lambda m, n, k: (k, n)),
                  o_spec=pl.BlockSpec((D, D), lambda m, n, k: (0, n)), dims=_TN, acc_shape=(D, D))
    d_rest, g_convw, g_convb = _conv_bwd(duc_dir, duc_mm, proj, p["conv_w"], d_rest)
    dqkv = None
    for g in range(3):
        dqkv = _attn_bwd(proj, d_o, sv["o"], sv["lse"], g, dqkv)
    parts = (dqkv[0], dqkv[1], dqkv[2], d_rest)
    dh = _d_h(l, parts, gw["w_in"])
    big["w_in"] = _g_w_in(l, sv["h_t"], parts, big.get("w_in"))
    dx_in, d_shift, d_scale, d_gpre = _prenorm_bwd(dh, sv["x"], dx, p["g_pre"], p["scale"])
    small = dict(dmod=jnp.concatenate([d_shift, d_scale, d_gate], axis=1), g_pre=d_gpre, conv_w=g_convw,
                 conv_b=g_convb, w_rg=_diag_blocks(g_gates[:, 0:D]), b_rg=d_brg,
                 w_ig=_diag_blocks(g_gates[:, D:2 * D]), b_ig=d_big, lam=d_lam, g_post=d_gpost)
    return dx_in, small, big


def _local_step(x, target, small_p, gw):
    saved = []
    h = x
    for l in range(2):
        h, sv = _layer_fwd(l, h, small_p[l], gw)
        saved.append(sv)
    dy, sq = _loss_head(h, target)
    loss = 0.5 * jnp.sum(sq) / D
    big = {}
    smalls = [None, None]
    dx = dy
    for l in (1, 0):
        dx, smalls[l], big = _layer_bwd(l, dx, small_p[l], gw, saved[l], big)
    return loss, dx, smalls, big


_SMALL_ROWS = 8 + 16 + 8 + 128 + 128


def _pack_small(smalls):
    dmod = jnp.concatenate([smalls[0]["dmod"].reshape(3, D), smalls[1]["dmod"].reshape(3, D),
                            jnp.zeros((2, D), _F32)], axis=0)
    vecs = jnp.concatenate([smalls[l][k] for k in ("g_pre", "conv_b", "b_rg", "b_ig", "lam", "g_post")
                            for l in range(2)] + [jnp.zeros((4, D), _F32)], axis=0)
    convw = jnp.concatenate([smalls[0]["conv_w"], smalls[1]["conv_w"]], axis=0)
    wrg = jnp.stack([smalls[0]["w_rg"], smalls[1]["w_rg"]]).reshape(128, D)
    wig = jnp.stack([smalls[0]["w_ig"], smalls[1]["w_ig"]]).reshape(128, D)
    return jnp.concatenate([dmod, vecs, convw, wrg, wig], axis=0)


def kernel(x, c, w_mod, b_mod, g_pre, w_in, conv_w, conv_b, w_rg, b_rg, w_ig, b_ig, lru_lambda, w_pa, w_pb, w_o, g_post, loss_target, m_w_mod, m_b_mod, m_g_pre, m_w_in, m_conv_w, m_conv_b, m_w_rg, m_b_rg, m_w_ig, m_b_ig, m_lru_lambda, m_w_pa, m_w_pb, m_w_o, m_g_post, v_w_mod, v_b_mod, v_g_pre, v_w_in, v_conv_w, v_conv_b, v_w_rg, v_b_rg, v_w_ig, v_b_ig, v_lru_lambda, v_w_pa, v_w_pb, v_w_o, v_g_post):
    xi, yi, ci = lax.axis_index("x"), lax.axis_index("y"), lax.axis_index("c")
    chip = 2 * xi + yi
    dev = 4 * xi + 2 * yi + ci
    mcols = w_mod.shape[2]

    pack1 = jnp.concatenate([jnp.broadcast_to(c, (8, D)),
                             jnp.pad(conv_w.reshape(8, 256), ((0, 0), (0, D - 256)))], axis=0)
    g1 = _exchange("gather_cond", [pack1], "xyc", False)[0]
    c_all = g1[:, 0, :]
    conv_w_full = jnp.transpose(g1[0::2, 8:16, 0:256], (1, 0, 2)).reshape(2, 4, D)

    b_cols = lax.dynamic_slice(b_mod, (0, chip * mcols), (2, mcols)).reshape(2, 1, mcols)
    mod_loc = _mod_fwd(c_all, w_mod, b_cols)
    g2 = _exchange("gather_mod", [mod_loc.reshape(16, mcols)], "xyc", False)[0]
    mod_full = jnp.transpose(g2[0::2], (1, 0, 2)).reshape(2, 8, 3 * D)
    mod_me = lax.dynamic_index_in_dim(mod_full, dev, axis=1, keepdims=False)

    wb = [_cast("cast_w_in", w_in.reshape(2 * D, 2304), 256).reshape(2, D, 2304),
          _cast("cast_w_pa", w_pa.reshape(2 * ATT_W, 256), 256).reshape(2, ATT_W, 256),
          _cast("cast_w_pb", w_pb.reshape(512, D), 256).reshape(2, 256, D),
          _cast("cast_w_o", w_o.reshape(512, D), 256).reshape(2, 256, D)]
    gl = _gather_weights(wb, [4, 1, 1, 1])
    gw = dict(w_in=gl[0], w_pa=gl[1], w_pb=gl[2], w_o=gl[3])

    small_p = []
    for l in range(2):
        gates = jnp.concatenate([_block_diag(w_rg[l]), _block_diag(w_ig[l])], axis=1).astype(_MXU)
        small_p.append(dict(
            shift=mod_me[l:l + 1, 0:D], scale=mod_me[l:l + 1, D:2 * D], gate=mod_me[l:l + 1, 2 * D:3 * D],
            g_pre=g_pre[l:l + 1], conv_w=conv_w_full[l], conv_b=conv_b[l:l + 1], w_gates=gates,
            b_rg=b_rg[l:l + 1], b_ig=b_ig[l:l + 1], lam=lru_lambda[l:l + 1], g_post=g_post[l:l + 1]))

    loss_loc, dx, smalls, big = _local_step(x[0], loss_target[0], small_p, gw)
    loss = lax.psum(loss_loc, ("x", "y", "c"))
    grad_x = dx[None]

    names = ("w_in", "w_pa", "w_pb", "w_o")
    core = jnp.reshape(ci, (1,)).astype(jnp.int32)
    where = jnp.stack([chip, ci]).astype(jnp.int32)
    pair = list(_exchange("reduce_pair", [big["w_in"].reshape(2, 16, 256, 2304)] + [big[k] for k in names[1:]],
                          "c", True, local=False, nchunks=[16, 4, 4, 4]))
    pair[0] = pair[0].reshape(N_CHIPS, D, 2304)
    t1 = [_sum_pair("sum_pair_" + k, big[k], r, core, 128) for k, r in zip(names, pair)]
    quad = _exchange("reduce_chips", [t[1] for t in t1], "xy", True, local=False, nchunks=[4, 1, 1, 1])
    t3 = [_sum_chips("sum_chips_" + k, t[0], r, where, 128) for k, t, r in zip(names, t1, quad)]
    both = _pair_fill("gather_layers", t3, [4, 1, 1, 1])
    g_big = dict(zip(names, both))

    g3 = _exchange("gather_small", [_pack_small(smalls)], "xyc", False)[0]
    tot = _sum_lead("sum_small", g3, 96)
    dmod_all = g3[:, 0:6, :].reshape(8, 2, 3 * D)
    dm_cols = jnp.transpose(lax.dynamic_slice(dmod_all, (0, 0, chip * mcols), (8, 2, mcols)), (1, 0, 2))
    g_w_mod = _mod_bwd(jnp.transpose(c_all), dm_cols)
    vec = tot[8:20].reshape(6, 2, D)
    g_conv_w_full = tot[24:32].reshape(2, 4, D)
    grads = dict(
        w_mod=g_w_mod, b_mod=tot[0:6].reshape(2, 3 * D), g_pre=vec[0], w_in=g_big["w_in"],
        conv_w=lax.dynamic_slice(g_conv_w_full, (0, 0, chip * 256), (2, 4, 256)), conv_b=vec[1],
        w_rg=tot[32:160].reshape(2, 16, 64, 64), b_rg=vec[2], w_ig=tot[160:288].reshape(2, 16, 64, 64),
        b_ig=vec[3], lru_lambda=vec[4], w_pa=g_big["w_pa"], w_pb=g_big["w_pb"], w_o=g_big["w_o"],
        g_post=vec[5])

    weights = dict(w_mod=w_mod, b_mod=b_mod, g_pre=g_pre, w_in=w_in, conv_w=conv_w, conv_b=conv_b, w_rg=w_rg,
                   b_rg=b_rg, w_ig=w_ig, b_ig=b_ig, lru_lambda=lru_lambda, w_pa=w_pa, w_pb=w_pb, w_o=w_o,
                   g_post=g_post)
    ms = dict(w_mod=m_w_mod, b_mod=m_b_mod, g_pre=m_g_pre, w_in=m_w_in, conv_w=m_conv_w, conv_b=m_conv_b,
              w_rg=m_w_rg, b_rg=m_b_rg, w_ig=m_w_ig, b_ig=m_b_ig, lru_lambda=m_lru_lambda, w_pa=m_w_pa,
              w_pb=m_w_pb, w_o=m_w_o, g_post=m_g_post)
    vs = dict(w_mod=v_w_mod, b_mod=v_b_mod, g_pre=v_g_pre, w_in=v_w_in, conv_w=v_conv_w, conv_b=v_conv_b,
              w_rg=v_w_rg, b_rg=v_b_rg, w_ig=v_w_ig, b_ig=v_b_ig, lru_lambda=v_lru_lambda, w_pa=v_w_pa,
              w_pb=v_w_pb, w_o=v_w_o, g_post=v_g_post)
    flat = dict(w_mod=(2 * D, mcols, 256), b_mod=(2, 3 * D, 2), g_pre=(2, D, 2), w_in=(2 * D, 2304, 256),
                conv_w=(8, 256, 8), conv_b=(2, D, 2), w_rg=(128, D, 128), b_rg=(2, D, 2), w_ig=(128, D, 128),
                b_ig=(2, D, 2), lru_lambda=(2, D, 2), w_pa=(2 * ATT_W, 256, 256), w_pb=(512, D, 256),
                w_o=(512, D, 256), g_post=(2, D, 2))
    order = ("w_mod", "b_mod", "g_pre", "w_in", "conv_w", "conv_b", "w_rg", "b_rg", "w_ig", "b_ig",
             "lru_lambda", "w_pa", "w_pb", "w_o", "g_post")
    deltas, new_m, new_v = [], [], []
    for k in order:
        rows, cols, tb = flat[k]
        shp = weights[k].shape
        d, nm_, nv_ = _adamw("adamw_" + k, weights[k].reshape(rows, cols), grads[k].reshape(rows, cols),
                             ms[k].reshape(rows, cols), vs[k].reshape(rows, cols), tb)
        deltas.append(d.reshape(shp))
        new_m.append(nm_.reshape(shp))
        new_v.append(nv_.reshape(shp))
    return (loss, grad_x, *[grads[k].reshape(weights[k].shape) for k in order], *deltas, *new_m, *new_v)
```

```python
import functools

import jax
import jax.numpy as jnp
from jax import lax
from jax.experimental import pallas as pl
from jax.experimental.pallas import tpu as pltpu

_F32 = jnp.float32
_MXU = jnp.bfloat16
_VMEM_LIMIT = 56 * 1024 * 1024
_MESH = pl.DeviceIdType.MESH

D = 1024
HEAD = 128
HEADS = 4
ATT_W = 512
QKV_W = 1536
IN_W = 9216
DILATIONS = (1, 4, 16)
BAND = 128
QBLK = BAND * 16
NORM_EPS = 1e-6
NEG_INF = -1e30
LRU_C = 8.0
N_CHIPS = 4
CB_GATT = 4608 // 512
CB_U, CB_GLRU, CB_MA, CB_MB = 5, 6, 7, 8
R_U, R_GLRU, R_MA, R_MB, R_END = 512, 1536, 2560, 3584, 4608

ADAM_LR, ADAM_B1, ADAM_B2, ADAM_EPS, ADAM_WD, ADAM_STEP = 0.001, 0.9, 0.999, 1e-08, 0.01, 10


def _params(ngrid):
    return pltpu.CompilerParams(dimension_semantics=("arbitrary",) * ngrid, vmem_limit_bytes=_VMEM_LIMIT)


def _sigmoid(v):
    return 0.5 * jnp.tanh(0.5 * v) + 0.5


_GROUPS = {
    "c": [(0, 0, 1)],
    "xy": [(1, 0, 0), (0, 1, 0), (1, 1, 0)],
    "xyc": [(0, 0, 1), (0, 1, 0), (0, 1, 1), (1, 0, 0), (1, 0, 1), (1, 1, 0), (1, 1, 1)],
}


def _rank(group, px, py, pc):
    if group == "c":
        return pc
    if group == "xy":
        return 2 * px + py
    return 4 * px + 2 * py + pc


def _flip(rel, x, y, c):
    dx, dy, dc = rel
    return (1 - x if dx else x, 1 - y if dy else y, 1 - c if dc else c)


def _pieces(ref, nchunk):
    step = ref.shape[0] // nchunk
    return [ref.at[pl.ds(q * step, step)] for q in range(nchunk)]


def _exchange(name, srcs, group, scatter, *, local=True, nchunks=None):
    rels = _GROUPS[group]
    gsize = len(rels) + 1
    n = len(srcs)
    nchunks = nchunks or [1] * n
    blks = [s.shape[1:] if scatter else s.shape for s in srcs]
    slotted = local or gsize > 2
    base = [sum(nchunks[:a]) for a in range(n)]
    tot = sum(nchunks)

    def body(*refs):
        src_refs, out_refs = refs[:n], refs[n:2 * n]
        send_sems, recv_sems, loc_sems = refs[2 * n:]
        x, y, c = lax.axis_index("x"), lax.axis_index("y"), lax.axis_index("c")
        me = _rank(group, x, y, c)
        copies = []
        for a in range(n):
            def part(r, a=a):
                return src_refs[a].at[r] if scatter else src_refs[a]
            dst = out_refs[a].at[me] if slotted else out_refs[a]
            if local:
                for q, (s_, d_) in enumerate(zip(_pieces(part(me), nchunks[a]), _pieces(dst, nchunks[a]))):
                    loc = pltpu.make_async_copy(s_, d_, loc_sems.at[base[a] + q])
                    loc.start()
                    copies.append(loc)
            for k, rel in enumerate(rels):
                peer = _flip(rel, x, y, c)
                for q, (s_, d_) in enumerate(zip(_pieces(part(_rank(group, *peer)), nchunks[a]),
                                                 _pieces(dst, nchunks[a]))):
                    cp = pltpu.make_async_remote_copy(
                        src_ref=s_, dst_ref=d_, send_sem=send_sems.at[(base[a] + q) * len(rels) + k],
                        recv_sem=recv_sems.at[(base[a] + q) * len(rels) + k],
                        device_id=peer, device_id_type=_MESH)
                    cp.start()
                    copies.append(cp)
        for cp in copies:
            cp.wait()

    any_spec = pl.BlockSpec(memory_space=pl.ANY)
    lead = (gsize,) if slotted else ()
    return pl.pallas_call(
        body, name=name,
        out_shape=[jax.ShapeDtypeStruct(lead + tuple(b), s.dtype) for b, s in zip(blks, srcs)],
        in_specs=[any_spec] * n, out_specs=[any_spec] * n,
        scratch_shapes=[pltpu.SemaphoreType.DMA((tot * len(rels),)), pltpu.SemaphoreType.DMA((tot * len(rels),)),
                        pltpu.SemaphoreType.DMA((tot,))],
    )(*srcs)


def _pair_fill(name, arrs, nchunks):
    n = len(arrs)
    base = [sum(nchunks[:a]) for a in range(n)]
    tot = sum(nchunks)

    def body(*refs):
        out_refs = refs[n:2 * n]
        send_sems, recv_sems = refs[2 * n:]
        x, y, c = lax.axis_index("x"), lax.axis_index("y"), lax.axis_index("c")
        copies = []
        for a in range(n):
            for q, blk in enumerate(_pieces(out_refs[a].at[c], nchunks[a])):
                cp = pltpu.make_async_remote_copy(
                    src_ref=blk, dst_ref=blk, send_sem=send_sems.at[base[a] + q], recv_sem=recv_sems.at[base[a] + q],
                    device_id=(x, y, 1 - c), device_id_type=_MESH)
                cp.start()
                copies.append(cp)
        for cp in copies:
            cp.wait()

    any_spec = pl.BlockSpec(memory_space=pl.ANY)
    return pl.pallas_call(
        body, name=name, out_shape=[jax.ShapeDtypeStruct(a.shape, a.dtype) for a in arrs],
        in_specs=[any_spec] * n, out_specs=[any_spec] * n, input_output_aliases={a: a for a in range(n)},
        scratch_shapes=[pltpu.SemaphoreType.DMA((tot,)), pltpu.SemaphoreType.DMA((tot,))],
    )(*arrs)


def _gather_weights(wb, nchunks):
    n = len(wb)
    rels = _GROUPS["xy"]
    base = [sum(nchunks[:a]) for a in range(n)]
    tot = sum(nchunks)

    def body(*refs):
        src_refs, out_refs = refs[:n], refs[n:2 * n]
        ici_send, ici_recv, d2d_send, d2d_recv, loc_sems = refs[2 * n:]
        x, y, c = lax.axis_index("x"), lax.axis_index("y"), lax.axis_index("c")
        me = 2 * x + y
        waits = []
        for a in range(n):
            for l in range(2):
                for q, (s_, d_) in enumerate(zip(_pieces(src_refs[a].at[l], nchunks[a]),
                                                 _pieces(out_refs[a].at[me, l], nchunks[a]))):
                    loc = pltpu.make_async_copy(s_, d_, loc_sems.at[(base[a] + q) * 2 + l])
                    loc.start()
                    waits.append(loc)
        first = []
        for a in range(n):
            for k, rel in enumerate(rels):
                px, py, _ = _flip(rel, x, y, c)
                for q, (s_, d_) in enumerate(zip(_pieces(src_refs[a].at[c], nchunks[a]),
                                                 _pieces(out_refs[a].at[me, c], nchunks[a]))):
                    sem = (base[a] + q) * 3 + k
                    cp = pltpu.make_async_remote_copy(src_ref=s_, dst_ref=d_, send_sem=ici_send.at[sem],
                                                      recv_sem=ici_recv.at[sem], device_id=(px, py, c),
                                                      device_id_type=_MESH)
                    cp.start()
                    first.append(cp)
        second = []
        for a in range(n):
            for k, rel in enumerate(rels):
                px, py, _ = _flip(rel, x, y, c)
                for q, blk in enumerate(_pieces(out_refs[a].at[2 * px + py, c], nchunks[a])):
                    sem = (base[a] + q) * 3 + k
                    landed = pltpu.make_async_remote_copy(src_ref=blk, dst_ref=blk, send_sem=ici_send.at[sem],
                                                          recv_sem=ici_recv.at[sem], device_id=(px, py, c),
                                                          device_id_type=_MESH)
                    landed.wait_recv()
                    cp = pltpu.make_async_remote_copy(src_ref=blk, dst_ref=blk, send_sem=d2d_send.at[sem],
                                                      recv_sem=d2d_recv.at[sem], device_id=(x, y, 1 - c),
                                                      device_id_type=_MESH)
                    cp.start()
                    second.append(cp)
        for cp in first:
            cp.wait_send()
        for cp in second:
            cp.wait_send()
        for a in range(n):
            for k, rel in enumerate(rels):
                px, py, _ = _flip(rel, x, y, c)
                for q, blk in enumerate(_pieces(out_refs[a].at[2 * px + py, 1 - c], nchunks[a])):
                    sem = (base[a] + q) * 3 + k
                    pltpu.make_async_remote_copy(src_ref=blk, dst_ref=blk, send_sem=d2d_send.at[sem],
                                                 recv_sem=d2d_recv.at[sem], device_id=(x, y, 1 - c),
                                                 device_id_type=_MESH).wait_recv()
        for cp in waits:
            cp.wait()

    any_spec = pl.BlockSpec(memory_space=pl.ANY)
    return pl.pallas_call(
        body, name="gather_weights",
        out_shape=[jax.ShapeDtypeStruct((N_CHIPS,) + a.shape, a.dtype) for a in wb],
        in_specs=[any_spec] * n, out_specs=[any_spec] * n,
        scratch_shapes=[pltpu.SemaphoreType.DMA((tot * 3,))] * 4 + [pltpu.SemaphoreType.DMA((tot * 2,))],
    )(*wb)


def _mm(name, a, b, out_sds, *, grid, a_spec, b_spec, o_spec, dims, acc_shape, into=None):
    nk = grid[2]

    def body(*refs):
        a_ref, b_ref = refs[0], refs[1]
        o_ref, acc = refs[-2], refs[-1]
        k = pl.program_id(2)
        part = lax.dot_general(a_ref[...].astype(_MXU), b_ref[...].astype(_MXU), dims,
                               preferred_element_type=_F32)
        if nk == 1:
            o_ref[...] = part.astype(o_ref.dtype)
            return

        @pl.when(k == 0)
        def _():
            acc[...] = part

        @pl.when(k > 0)
        def _():
            acc[...] += part

        @pl.when(k == nk - 1)
        def _():
            o_ref[...] = acc[...].astype(o_ref.dtype)

    if nk == 1:
        acc_shape = (8, 128)
    in_specs = [a_spec, b_spec]
    args = [a, b]
    aliases = {}
    if into is not None:
        in_specs.append(pl.BlockSpec(memory_space=pl.ANY))
        args.append(into)
        aliases = {2: 0}
    return pl.pallas_call(
        body, name=name, grid=grid, in_specs=in_specs, out_specs=o_spec, out_shape=out_sds,
        scratch_shapes=[pltpu.VMEM(acc_shape, _F32)], input_output_aliases=aliases,
        compiler_params=_params(3))(*args)


_NN = (((1,), (0,)), ((), ()))
_NT = (((1,), (1,)), ((), ()))
_TN = (((0,), (0,)), ((), ()))


def _rowwise(name, body, *, grid, ins, outs, scratch=()):
    return pl.pallas_call(
        body, name=name, grid=(grid,), in_specs=[s for _, s in ins], out_specs=[s for _, s in outs],
        out_shape=[o for o, _ in outs], scratch_shapes=list(scratch),
        compiler_params=_params(1))(*[a for a, _ in ins])


def _rows(tb, w, cb=0, n=None):
    if n is None:
        return pl.BlockSpec((tb, w), lambda i: (i, cb))
    return pl.BlockSpec((tb, w), lambda i: (n - 1 - i, cb))


def _vec(shape):
    return pl.BlockSpec(shape, lambda i: (0,) * len(shape))


def _halo_prev(tb, w, cb=0, n=None, rows=8):
    if n is None:
        return pl.BlockSpec((rows, w), lambda i: (jnp.maximum(i * (tb // rows) - 1, 0), cb))
    return pl.BlockSpec((rows, w), lambda i: (jnp.maximum((n - 1 - i) * (tb // rows) - 1, 0), cb))


def _halo_next(tb, w, n, cb=0):
    return pl.BlockSpec((8, w), lambda i: (jnp.minimum((i + 1) * (tb // 8), n * (tb // 8) - 1), cb))


def _sds(shape, dtype=_F32):
    return jax.ShapeDtypeStruct(shape, dtype)


def _cast(name, a, tb):
    rows, cols = a.shape

    def body(a_ref, o_ref):
        o_ref[...] = a_ref[...].astype(o_ref.dtype)

    return _rowwise(name, body, grid=rows // tb, ins=[(a, _rows(tb, cols))],
                    outs=[(_sds((rows, cols), _MXU), _rows(tb, cols))])[0]


def _sum_lead(name, a, tb):
    g, rows, cols = a.shape

    def body(a_ref, o_ref):
        acc = a_ref[0]
        for k in range(1, g):
            acc = acc + a_ref[k]
        o_ref[...] = acc

    return _rowwise(name, body, grid=rows // tb,
                    ins=[(a, pl.BlockSpec((g, tb, cols), lambda i: (0, i, 0)))],
                    outs=[(_sds((rows, cols)), _rows(tb, cols))])[0]


def _sum_pair(name, mine, theirs, core, tb):
    _, nj, rows, cols = mine.shape

    def body(s_ref, a_ref, b_ref, o_ref, ob_ref):
        t = a_ref[...] + b_ref[...]
        o_ref[...] = t
        ob_ref[...] = t.astype(ob_ref.dtype)

    blk = pl.BlockSpec((None, tb, cols), lambda j, i, s: (j, i, 0))
    grid_spec = pltpu.PrefetchScalarGridSpec(
        num_scalar_prefetch=1, grid=(nj, rows // tb),
        in_specs=[pl.BlockSpec((None, None, tb, cols), lambda j, i, s: (s[0], j, i, 0)), blk],
        out_specs=[blk, blk])
    return pl.pallas_call(body, name=name, grid_spec=grid_spec,
                          out_shape=[_sds((nj, rows, cols)), _sds((nj, rows, cols), _MXU)],
                          compiler_params=_params(2))(core, mine, theirs)


def _sum_chips(name, mine, theirs, where, tb):
    _, rows, cols = mine.shape

    def body(s_ref, a_ref, b1_ref, b2_ref, b3_ref, o_ref):
        o_ref[...] = ((a_ref[...] + b1_ref[...].astype(_F32)) + b2_ref[...].astype(_F32)) + b3_ref[...].astype(_F32)

    def slot(k):
        return pl.BlockSpec((None, tb, cols), lambda i, s: (jnp.bitwise_xor(s[0], k), i, 0))

    grid_spec = pltpu.PrefetchScalarGridSpec(
        num_scalar_prefetch=1, grid=(rows // tb,),
        in_specs=[slot(0), slot(1), slot(2), slot(3)],
        out_specs=pl.BlockSpec((None, tb, cols), lambda i, s: (s[1], i, 0)))
    return pl.pallas_call(body, name=name, grid_spec=grid_spec, out_shape=_sds((2, rows, cols)),
                          compiler_params=_params(1))(where, mine, theirs, theirs, theirs)


def _adamw(name, w, g, m, v, tb):
    rows, cols = w.shape
    c1 = 1.0 - ADAM_B1 ** ADAM_STEP
    c2 = 1.0 - ADAM_B2 ** ADAM_STEP

    def body(w_ref, g_ref, m_ref, v_ref, d_ref, nm_ref, nv_ref):
        gv = g_ref[...]
        nm = ADAM_B1 * m_ref[...] + (1.0 - ADAM_B1) * gv
        nv = ADAM_B2 * v_ref[...] + (1.0 - ADAM_B2) * (gv * gv)
        d_ref[...] = -ADAM_LR * ((nm / c1) / (jnp.sqrt(nv / c2) + ADAM_EPS) + ADAM_WD * w_ref[...])
        nm_ref[...] = nm
        nv_ref[...] = nv

    spec = _rows(tb, cols)
    return _rowwise(name, body, grid=rows // tb, ins=[(w, spec), (g, spec), (m, spec), (v, spec)],
                    outs=[(_sds((rows, cols)), spec)] * 3)


def _mod_fwd(c_all, w_mod, b_cols):
    cols = w_mod.shape[2]

    def body(c_ref, w_ref, b_ref, o_ref):
        cv = c_ref[...]
        sc = (cv * _sigmoid(cv)).astype(_MXU)
        o_ref[...] = jnp.dot(sc, w_ref[...].astype(_MXU), preferred_element_type=_F32) + b_ref[...]

    return pl.pallas_call(
        body, name="mod_fwd", grid=(2,),
        in_specs=[pl.BlockSpec((8, D), lambda l: (0, 0)), pl.BlockSpec((None, D, cols), lambda l: (l, 0, 0)),
                  pl.BlockSpec((None, 1, cols), lambda l: (l, 0, 0))],
        out_specs=pl.BlockSpec((None, 8, cols), lambda l: (l, 0, 0)),
        out_shape=_sds((2, 8, cols)), compiler_params=_params(1))(c_all, w_mod, b_cols)


def _mod_bwd(c_all_t, dm):
    cols = dm.shape[2]

    def body(c_ref, d_ref, o_ref):
        cv = c_ref[...]
        sc = (cv * _sigmoid(cv)).astype(_MXU)
        o_ref[...] = jnp.dot(sc, d_ref[...].astype(_MXU), preferred_element_type=_F32)

    return pl.pallas_call(
        body, name="mod_bwd", grid=(2,),
        in_specs=[pl.BlockSpec((D, 8), lambda l: (0, 0)), pl.BlockSpec((None, 8, cols), lambda l: (l, 0, 0))],
        out_specs=pl.BlockSpec((None, D, cols), lambda l: (l, 0, 0)),
        out_shape=_sds((2, D, cols)), compiler_params=_params(1))(c_all_t, dm)


def _prenorm_fwd(x, g_pre, shift, scale):
    s = x.shape[0]
    tb = 512

    def body(x_ref, g_ref, sh_ref, sc_ref, h_ref, ht_ref):
        xv = x_ref[...]
        rstd = lax.rsqrt(jnp.mean(xv * xv, axis=-1, keepdims=True) + NORM_EPS)
        hv = (xv * rstd) * g_ref[...] * (1.0 + sc_ref[...]) + sh_ref[...]
        h_ref[...] = hv.astype(h_ref.dtype)
        ht_ref[...] = hv.T.astype(ht_ref.dtype)

    v = _vec((1, D))
    return _rowwise("prenorm_fwd", body, grid=s // tb,
                    ins=[(x, _rows(tb, D)), (g_pre, v), (shift, v), (scale, v)],
                    outs=[(_sds((s, D), _MXU), _rows(tb, D)),
                          (_sds((D, s), _MXU), pl.BlockSpec((D, tb), lambda i: (0, i)))])


def _shift_down(cur, halo, j, tb):
    ext = jnp.concatenate([halo, cur], axis=0)
    return pltpu.roll(ext, j, 0)[8:8 + tb]


def _shift_up(cur, halo, j, tb):
    ext = jnp.concatenate([cur, halo], axis=0)
    return pltpu.roll(ext, tb + 8 - j, 0)[0:tb]


def _conv_fwd(proj, conv_w, conv_b):
    s = proj.shape[0]
    tb = 512

    def body(u_ref, hp_ref, w_ref, b_ref, o_ref):
        i = pl.program_id(0)
        u = u_ref[...].astype(_F32)
        halo = jnp.where(i > 0, hp_ref[...].astype(_F32)[8:16], 0.0)
        acc = b_ref[...] + u * w_ref[0:1, :]
        for j in range(1, 4):
            acc = acc + _shift_down(u, halo, j, tb) * w_ref[j:j + 1, :]
        o_ref[...] = acc

    return _rowwise("conv_fwd", body, grid=s // tb,
                    ins=[(proj, _rows(tb, D, CB_U)), (proj, _halo_prev(tb, D, CB_U, rows=16)),
                         (conv_w, _vec((4, D))), (conv_b, _vec((1, D)))],
                    outs=[(_sds((s, D)), _rows(tb, D))])[0]


def _lru_gates(pre_r, pre_i, uc, b_rg, b_ig, lam):
    r = _sigmoid(pre_r + b_rg)
    ig = _sigmoid(pre_i + b_ig)
    nl = -lam
    sp = jnp.maximum(nl, 0.0) + jnp.log(1.0 + jnp.exp(-jnp.abs(nl)))
    la = -LRU_C * r * sp
    a = jnp.exp(la)
    one_m_a2 = -jnp.tanh(la) * (a * a + 1.0)
    inv_sq = lax.rsqrt(jnp.maximum(one_m_a2, 1e-30))
    return r, ig, sp, a, one_m_a2 * inv_sq, inv_sq


GATE_TILES = 8


def _gate_tiles(w_rg, w_ig):
    eye = jnp.eye(2, dtype=w_rg.dtype)

    def tiles(w):
        return jnp.einsum("cpij,pq->cpiqj", w.reshape(GATE_TILES, 2, 64, 64), eye).reshape(GATE_TILES, 128, 128)

    return jnp.concatenate([tiles(w_rg), tiles(w_ig)], axis=2)


def _gate_tile_grads(gw):
    keep = jnp.eye(2, dtype=jnp.bool_)[None, :, None, :, None]

    def blocks(t):
        t5 = t.reshape(GATE_TILES, 2, 64, 2, 64)
        return jnp.sum(jnp.where(keep, t5, 0.0), axis=3).reshape(16, 64, 64)

    return blocks(gw[:, :, 0:128]), blocks(gw[:, :, 128:256])


def _gate_preacts(ucv, wt_ref):
    ucb = ucv.astype(_MXU)
    ps = [jnp.dot(ucb[:, 128 * c:128 * (c + 1)], wt_ref[c], preferred_element_type=_F32) for c in range(GATE_TILES)]
    pre_r = jnp.concatenate([p[:, 0:128] for p in ps], axis=1)
    pre_i = jnp.concatenate([p[:, 128:256] for p in ps], axis=1)
    return pre_r, pre_i


def _scan_fwd(uc, wt, b_rg, b_ig, lam):
    s = uc.shape[0]
    tb = 256

    def body(uc_ref, wt_ref, brg_ref, big_ref, lam_ref, h_ref, carry, a_s, b_s):
        i = pl.program_id(0)

        @pl.when(i == 0)
        def _():
            carry[...] = jnp.zeros_like(carry)

        ucv = uc_ref[...]
        pre_r, pre_i = _gate_preacts(ucv, wt_ref)
        _, ig, _, a, sq, _ = _lru_gates(pre_r, pre_i, ucv, brg_ref[...], big_ref[...], lam_ref[...])
        av = a
        bv = sq * (ig * ucv)
        av = av.reshape(tb // 8, 8, D)
        bv = bv.reshape(tb // 8, 8, D)
        row8 = lax.broadcasted_iota(jnp.int32, (1, 8, 1), 1)
        for sh in (1, 2, 4):
            m = row8 >= sh
            b_sh = pltpu.roll(bv, sh, 1)
            a_sh = pltpu.roll(av, sh, 1)
            bv = jnp.where(m, av * b_sh + bv, bv)
            av = jnp.where(m, av * a_sh, av)
        a_s[...] = av.reshape(tb, D)
        b_s[...] = bv.reshape(tb, D)

        def tile(t, state):
            rows = pl.ds(pl.multiple_of(t * 8, 8), 8)
            hv = b_s[rows, :] + a_s[rows, :] * state
            h_ref[rows, :] = hv
            return jnp.broadcast_to(hv[7:8, :], (8, D))

        carry[...] = lax.fori_loop(0, tb // 8, tile, jnp.broadcast_to(carry[7:8, :], (8, D)), unroll=4)

    v = _vec((1, D))
    return _rowwise("scan_fwd", body, grid=s // tb,
                    ins=[(uc, _rows(tb, D)), (wt, _vec((GATE_TILES, 128, 256))), (b_rg, v), (b_ig, v), (lam, v)],
                    outs=[(_sds((s, D)), _rows(tb, D))],
                    scratch=[pltpu.VMEM((8, D), _F32), pltpu.VMEM((tb, D), _F32), pltpu.VMEM((tb, D), _F32)])[0]


def _weight_specs(l):
    return [pl.BlockSpec((N_CHIPS, None, ATT_W, 256), lambda i: (0, l, 0, 0)),
            pl.BlockSpec((N_CHIPS, None, 256, D), lambda i: (0, l, 0, 0)),
            pl.BlockSpec((N_CHIPS, None, 256, D), lambda i: (0, l, 0, 0))]


def _tail_fwd(l, o, h_lru, proj, x, gate, g_post, gw):
    s = x.shape[0]
    tb = 512

    def body(o_ref, h_ref, ga_ref, gl_ref, ma_ref, mb_ref, x_ref, gt_ref, gp_ref, wpa_ref, wpb_ref, wo_ref,
             aa_ref, ba_ref, ya_ref, yb_ref, z_ref, out_ref, xn_ref):
        ga = ga_ref[...].astype(_F32)
        aa = (o_ref[...] * (ga * _sigmoid(ga))).astype(_MXU)
        aa_ref[...] = aa
        gl = gl_ref[...].astype(_F32)
        ba = (h_ref[...] * (gl * _sigmoid(gl))).astype(_MXU)
        ba_ref[...] = ba
        ya = jnp.concatenate([jnp.dot(aa, wpa_ref[j], preferred_element_type=_F32) for j in range(N_CHIPS)], axis=1)
        ya_ref[...] = ya.astype(ya_ref.dtype)
        yb = jnp.dot(ba, wpb_ref[...].reshape(D, D), preferred_element_type=_F32)
        yb_ref[...] = yb.astype(yb_ref.dtype)
        z = (_sigmoid(ma_ref[...].astype(_F32)) * ya
             + _sigmoid(mb_ref[...].astype(_F32)) * yb).astype(z_ref.dtype)
        z_ref[...] = z
        ov = jnp.dot(z, wo_ref[...].reshape(D, D), preferred_element_type=_F32)
        out_ref[...] = ov
        rstd = lax.rsqrt(jnp.mean(ov * ov, axis=-1, keepdims=True) + NORM_EPS)
        xn_ref[...] = x_ref[...] + gt_ref[...] * ((ov * rstd) * gp_ref[...])

    v = _vec((1, D))
    r = _rows(tb, D)
    r5 = _rows(tb, ATT_W)
    return _rowwise("tail_fwd", body, grid=s // tb,
                    ins=[(o, r5), (h_lru, r), (proj, _rows(tb, ATT_W, CB_GATT)), (proj, _rows(tb, D, CB_GLRU)),
                         (proj, _rows(tb, D, CB_MA)), (proj, _rows(tb, D, CB_MB)), (x, r), (gate, v), (g_post, v)]
                    + list(zip((gw["w_pa"], gw["w_pb"], gw["w_o"]), _weight_specs(l))),
                    outs=[(_sds((s, ATT_W), _MXU), r5), (_sds((s, D), _MXU), r), (_sds((s, D), _MXU), r),
                          (_sds((s, D), _MXU), r), (_sds((s, D), _MXU), r), (_sds((s, D)), r), (_sds((s, D)), r)])


def _loss_head(y, target):
    s = y.shape[0]
    tb = 512

    def body(y_ref, t_ref, dy_ref, acc_ref):
        i = pl.program_id(0)

        @pl.when(i == 0)
        def _():
            acc_ref[...] = jnp.zeros_like(acc_ref)

        err = y_ref[...] - t_ref[...]
        dy_ref[...] = err * (1.0 / D)
        acc_ref[...] += jnp.sum(err * err, axis=0, keepdims=True)

    return _rowwise("loss_head", body, grid=s // tb,
                    ins=[(y, _rows(tb, D)), (target, _rows(tb, D))],
                    outs=[(_sds((s, D)), _rows(tb, D)), (_sds((1, D)), _vec((1, D)))])


def _zero_first(i, *refs):
    @pl.when(i == 0)
    def _():
        for ref in refs:
            ref[...] = jnp.zeros_like(ref)


def _tail_bwd(l, dx, out, y_a, y_b, proj, o, h_lru, gate, g_post, gw):
    s = dx.shape[0]
    tb = 256

    def body(dx_ref, out_ref, ya_ref, yb_ref, ma_ref, mb_ref, o_ref, ga_ref, h_ref, gl_ref, gt_ref, gp_ref,
             wpa_ref, wpb_ref, wo_ref,
             dout_ref, dya_ref, dyb_ref, rest_ref, do_ref, dh_ref, dgt_ref, dgp_ref):
        i = pl.program_id(0)
        ov = out_ref[...]
        dxv = dx_ref[...]
        rstd = lax.rsqrt(jnp.mean(ov * ov, axis=-1, keepdims=True) + NORM_EPS)
        nv = ov * rstd
        s_dn = jnp.sum(dxv * nv, axis=0, keepdims=True)
        _zero_first(i, dgt_ref, dgp_ref)
        dgt_ref[...] += s_dn * gp_ref[...]
        dgp_ref[...] += s_dn * gt_ref[...]
        dn = dxv * (gt_ref[...] * gp_ref[...])
        d_out = (rstd * (dn - nv * jnp.mean(dn * nv, axis=-1, keepdims=True))).astype(_MXU)
        dout_ref[...] = d_out
        dz = lax.dot_general(d_out, wo_ref[...].reshape(D, D), _NT, preferred_element_type=_F32)
        ga = _sigmoid(ma_ref[...].astype(_F32))
        gb = _sigmoid(mb_ref[...].astype(_F32))
        dya = (dz * ga).astype(_MXU)
        dyb = (dz * gb).astype(_MXU)
        dya_ref[...] = dya
        dyb_ref[...] = dyb
        rest_ref[:, R_MA:R_MB] = (dz * ya_ref[...].astype(_F32) * ga * (1.0 - ga)).astype(rest_ref.dtype)
        rest_ref[:, R_MB:R_END] = (dz * yb_ref[...].astype(_F32) * gb * (1.0 - gb)).astype(rest_ref.dtype)
        daa = lax.dot_general(dya[:, 0:256], wpa_ref[0], _NT, preferred_element_type=_F32)
        for j in range(1, N_CHIPS):
            daa = daa + lax.dot_general(dya[:, j * 256:(j + 1) * 256], wpa_ref[j], _NT, preferred_element_type=_F32)
        dba = lax.dot_general(dyb, wpb_ref[...].reshape(D, D), _NT, preferred_element_type=_F32)
        gav = ga_ref[...].astype(_F32)
        sa = _sigmoid(gav)
        do_ref[...] = daa * (gav * sa)
        rest_ref[:, 0:R_U] = (daa * o_ref[...] * (sa * (1.0 + gav * (1.0 - sa)))).astype(rest_ref.dtype)
        gl = gl_ref[...].astype(_F32)
        sl = _sigmoid(gl)
        dh_ref[...] = dba * (gl * sl)
        rest_ref[:, R_GLRU:R_MA] = (dba * h_ref[...] * (sl * (1.0 + gl * (1.0 - sl)))).astype(rest_ref.dtype)

    v = _vec((1, D))
    r5, r10 = _rows(tb, ATT_W), _rows(tb, D)
    return _rowwise("tail_bwd", body, grid=s // tb,
                    ins=[(dx, r10), (out, r10), (y_a, r10), (y_b, r10), (proj, _rows(tb, D, CB_MA)),
                         (proj, _rows(tb, D, CB_MB)), (o, r5), (proj, _rows(tb, ATT_W, CB_GATT)), (h_lru, r10),
                         (proj, _rows(tb, D, CB_GLRU)), (gate, v), (g_post, v)]
                    + list(zip((gw["w_pa"], gw["w_pb"], gw["w_o"]), _weight_specs(l))),
                    outs=[(_sds((s, D), _MXU), r10), (_sds((s, D), _MXU), r10), (_sds((s, D), _MXU), r10),
                          (_sds((s, R_END), _MXU), _rows(tb, R_END)),
                          (_sds((s, ATT_W)), r5), (_sds((s, D)), r10), (_sds((1, D)), v), (_sds((1, D)), v)])


def _scan_bwd(dh, uc, h_lru, wt, b_rg, b_ig, lam):
    s = uc.shape[0]
    tb = 256
    n = s // tb

    def body(dh_ref, uc_ref, h_ref, hp_ref, wt_ref, brg_ref, big_ref, lam_ref,
             duc_ref, dwt_ref, dbrg_ref, dbig_ref, dlam_ref, carry, c_s, g_s):
        i = pl.program_id(0)

        @pl.when(i == 0)
        def _():
            carry[...] = jnp.zeros_like(carry)
            for acc_ref in (dwt_ref, dbrg_ref, dbig_ref, dlam_ref):
                acc_ref[...] = jnp.zeros_like(acc_ref)

        ucv = uc_ref[...]
        pre_r, pre_i = _gate_preacts(ucv, wt_ref)
        r, ig, sp, a, sq, inv_sq =_lru_gates(pre_r, pre_i, ucv, brg_ref[...], big_ref[...], lam_ref[...])
        row = lax.broadcasted_iota(jnp.int32, (tb, 1), 0)
        cv = jnp.where(row == tb - 1, 1.0, pltpu.roll(a, tb - 1, 0))
        gv = dh_ref[...]
        cv = cv.reshape(tb // 8, 8, D)
        gv = gv.reshape(tb // 8, 8, D)
        row8 = lax.broadcasted_iota(jnp.int32, (1, 8, 1), 1)
        for sh in (1, 2, 4):
            m = row8 < 8 - sh
            g_sh = pltpu.roll(gv, 8 - sh, 1)
            c_sh = pltpu.roll(cv, 8 - sh, 1)
            gv = jnp.where(m, gv + cv * g_sh, gv)
            cv = jnp.where(m, cv * c_sh, cv)
        c_s[...] = cv.reshape(tb, D)
        g_s[...] = gv.reshape(tb, D)

        def tile(k, state):
            rows = pl.ds(pl.multiple_of((tb // 8 - 1 - k) * 8, 8), 8)
            gt = g_s[rows, :] + c_s[rows, :] * state
            g_s[rows, :] = gt
            return jnp.broadcast_to(gt[0:1, :], (8, D))

        lax.fori_loop(0, tb // 8, tile, jnp.broadcast_to(carry[0:1, :], (8, D)), unroll=4)
        gv = g_s[...]
        carry[...] = (a * gv)[0:8]

        halo = jnp.where(i < n - 1, hp_ref[...], 0.0)
        h_prev = _shift_down(h_ref[...], halo, 1, tb)
        d_a = gv * h_prev
        d_sq = gv * (ig * ucv)
        d_i = gv * sq * ucv
        d_la = d_a * a - d_sq * (a * a) * inv_sq
        d_r = d_la * (-LRU_C * sp)
        d_pre_r = d_r * r * (1.0 - r)
        d_pre_i = d_i * ig * (1.0 - ig)
        ucb = ucv.astype(_MXU)
        dpr = d_pre_r.astype(_MXU)
        dpi = d_pre_i.astype(_MXU)
        back = []
        for c in range(GATE_TILES):
            lanes = slice(128 * c, 128 * (c + 1))
            dp = jnp.concatenate([dpr[:, lanes], dpi[:, lanes]], axis=1)
            back.append(lax.dot_general(dp, wt_ref[c], _NT, preferred_element_type=_F32))
            dwt_ref[c] += lax.dot_general(ucb[:, lanes], dp, _TN, preferred_element_type=_F32)
        duc_ref[...] = gv * sq * ig + jnp.concatenate(back, axis=1)
        dbrg_ref[...] += jnp.sum(d_pre_r, axis=0, keepdims=True)
        dbig_ref[...] += jnp.sum(d_pre_i, axis=0, keepdims=True)
        lamv = lam_ref[...]
        dlam_ref[...] += jnp.sum(d_la * (-LRU_C * r), axis=0, keepdims=True) * (-_sigmoid(-lamv))

    v = _vec((1, D))
    rv = _rows(tb, D, 0, n)
    return _rowwise("scan_bwd", body, grid=n,
                    ins=[(dh, rv), (uc, rv), (h_lru, rv), (h_lru, _halo_prev(tb, D, 0, n)),
                         (wt, _vec((GATE_TILES, 128, 256))), (b_rg, v), (b_ig, v), (lam, v)],
                    outs=[(_sds((s, D)), rv), (_sds((GATE_TILES, 128, 256)), _vec((GATE_TILES, 128, 256))),
                          (_sds((1, D)), v), (_sds((1, D)), v), (_sds((1, D)), v)],
                    scratch=[pltpu.VMEM((8, D), _F32), pltpu.VMEM((tb, D), _F32), pltpu.VMEM((tb, D), _F32)])


def _conv_bwd(duc_a, proj, conv_w, rest):
    s = duc_a.shape[0]
    tb = 512
    n = s // tb
    hw = D // 2

    def body(da_ref, dan_ref, u_ref, up_ref, w_ref, rest_in, du_ref, dw_ref, dbias_ref):
        i = pl.program_id(1)
        duc = da_ref[...]
        nxt = jnp.where(i < n - 1, dan_ref[...], 0.0)
        u = u_ref[...].astype(_F32)
        halo = jnp.where(i > 0, up_ref[...].astype(_F32)[8:16], 0.0)
        du = duc * w_ref[0:1, :]
        dws = [jnp.sum(duc * u, axis=0, keepdims=True)]
        for j in range(1, 4):
            du = du + _shift_up(duc, nxt, j, tb) * w_ref[j:j + 1, :]
            dws.append(jnp.sum(duc * _shift_down(u, halo, j, tb), axis=0, keepdims=True))
        du_ref[...] = du.astype(du_ref.dtype)
        _zero_first(i, dw_ref, dbias_ref)
        for j in range(4):
            dw_ref[j:j + 1, :] += dws[j]
        dbias_ref[...] += jnp.sum(duc, axis=0, keepdims=True)

    r = pl.BlockSpec((tb, hw), lambda h, i: (i, h))
    nxt_spec = pl.BlockSpec((8, hw), lambda h, i: (jnp.minimum((i + 1) * (tb // 8), n * (tb // 8) - 1), h))
    return pl.pallas_call(
        body, name="conv_bwd", grid=(2, n),
        in_specs=[r, nxt_spec,
                  pl.BlockSpec((tb, hw), lambda h, i: (i, 2 * CB_U + h)),
                  pl.BlockSpec((16, hw), lambda h, i: (jnp.maximum(i * (tb // 16) - 1, 0), 2 * CB_U + h)),
                  pl.BlockSpec((4, hw), lambda h, i: (0, h)), pl.BlockSpec(memory_space=pl.ANY)],
        out_specs=[pl.BlockSpec((tb, hw), lambda h, i: (i, R_U // hw + h)),
                   pl.BlockSpec((4, hw), lambda h, i: (0, h)), pl.BlockSpec((1, hw), lambda h, i: (0, h))],
        out_shape=[_sds(rest.shape, rest.dtype), _sds((4, D)), _sds((1, D))],
        input_output_aliases={5: 0}, compiler_params=_params(2),
    )(duc_a, duc_a, proj, proj, conv_w, rest)


def _prenorm_bwd(dh, x, dx_out, g_pre, scale):
    s = x.shape[0]
    tb = 512

    def body(dh_ref, x_ref, dxo_ref, g_ref, sc_ref, dx_ref, dsh_ref, dsc_ref, dg_ref):
        i = pl.program_id(0)
        xv = x_ref[...]
        dhv = dh_ref[...]
        rstd = lax.rsqrt(jnp.mean(xv * xv, axis=-1, keepdims=True) + NORM_EPS)
        xn = xv * rstd
        one_sc = 1.0 + sc_ref[...]
        s1 = jnp.sum(dhv * xn, axis=0, keepdims=True)
        _zero_first(i, dsh_ref, dsc_ref, dg_ref)
        dsh_ref[...] += jnp.sum(dhv, axis=0, keepdims=True)
        dsc_ref[...] += s1 * g_ref[...]
        dg_ref[...] += s1 * one_sc
        dxn = dhv * (g_ref[...] * one_sc)
        dx_ref[...] = dxo_ref[...] + rstd * (dxn - xn * jnp.mean(dxn * xn, axis=-1, keepdims=True))

    v = _vec((1, D))
    r = _rows(tb, D)
    return _rowwise("prenorm_bwd", body, grid=s // tb,
                    ins=[(dh, r), (x, r), (dx_out, r), (g_pre, v), (scale, v)],
                    outs=[(_sds((s, D)), r), (_sds((1, D)), v), (_sds((1, D)), v), (_sds((1, D)), v)])


def _band_tiles(dil):
    tiles = []
    for rho in range(dil):
        for b in range(16 // dil):
            qs = rho + dil * BAND * b
            tiles.append((qs, QBLK + qs - dil * BAND, b))
    return tiles


def _strided(start, size, dil):
    return pl.ds(start, size, stride=dil) if dil > 1 else pl.ds(start, size)


def _band_mask(i, b):
    qi = lax.broadcasted_iota(jnp.int32, (BAND, 2 * BAND), 0)
    ki = lax.broadcasted_iota(jnp.int32, (BAND, 2 * BAND), 1)
    valid = (ki >= qi) & (ki <= qi + BAND)
    if b == 0:
        valid = valid & ((ki >= BAND) | (i > 0))
    return valid


def _attn_fwd(proj):
    s = proj.shape[0]
    n = s // QBLK
    scale = HEAD ** -0.5

    def body(*refs):
        q_refs, kp_refs, kc_refs, vp_refs, vc_refs = (refs[3 * t:3 * t + 3] for t in range(5))
        o_ref, lse_ref, qbuf, kbuf, vbuf = refs[15:20]
        accs, maxs, dens = refs[20:23], refs[23:26], refs[26:29]
        i = pl.program_id(1)
        for g, dil in enumerate(DILATIONS):
            qbuf[...] = q_refs[g][...].astype(_F32)
            kbuf[0:QBLK, :] = kp_refs[g][...].astype(_F32)
            kbuf[QBLK:2 * QBLK, :] = kc_refs[g][...].astype(_F32)
            vbuf[0:QBLK, :] = vp_refs[g][...].astype(_F32)
            vbuf[QBLK:2 * QBLK, :] = vc_refs[g][...].astype(_F32)
            for qs, ks, b in _band_tiles(dil):
                qsl = _strided(qs, BAND, dil)
                q = qbuf[qsl, :].astype(_MXU)
                kk = kbuf[_strided(ks, 2 * BAND, dil), :].astype(_MXU)
                vv = vbuf[_strided(ks, 2 * BAND, dil), :].astype(_MXU)
                sc = lax.dot_general(q, kk, _NT, preferred_element_type=_F32) * scale
                sc = jnp.where(_band_mask(i, b), sc, NEG_INF)
                m = jnp.max(sc, axis=-1, keepdims=True)
                p = jnp.exp(sc - m)
                accs[g][qsl, :] = jnp.dot(p.astype(_MXU), vv, preferred_element_type=_F32)
                maxs[g][qsl, :] = jnp.broadcast_to(m, (BAND, HEAD))
                dens[g][qsl, :] = jnp.broadcast_to(jnp.sum(p, axis=-1, keepdims=True), (BAND, HEAD))
        ms = [r[...] for r in maxs]
        mx = jnp.maximum(jnp.maximum(ms[0], ms[1]), ms[2])
        ws = [jnp.exp(m - mx) for m in ms]
        den = ws[0] * dens[0][...] + ws[1] * dens[1][...] + ws[2] * dens[2][...]
        o_ref[...] = (ws[0] * accs[0][...] + ws[1] * accs[1][...] + ws[2] * accs[2][...]) / den
        lse_ref[...] = mx + jnp.log(den)

    blk = (QBLK, HEAD)

    def spec(first_col, lag):
        specs = []
        for g in range(3):
            col = first_col + g * HEADS
            if lag:
                specs.append(pl.BlockSpec(blk, lambda j, i, col=col: (jnp.maximum(i - 1, 0), col + j)))
            else:
                specs.append(pl.BlockSpec(blk, lambda j, i, col=col: (i, col + j)))
        return specs

    out_spec = pl.BlockSpec(blk, lambda j, i: (i, j))
    return pl.pallas_call(
        body, name="attn_fwd", grid=(HEADS, n),
        in_specs=spec(0, False) + spec(12, True) + spec(12, False) + spec(24, True) + spec(24, False),
        out_specs=[out_spec] * 2, out_shape=[_sds((s, ATT_W))] * 2,
        scratch_shapes=[pltpu.VMEM(blk, _F32)] + [pltpu.VMEM((2 * QBLK, HEAD), _F32)] * 2
        + [pltpu.VMEM(blk, _F32)] * 9,
        compiler_params=_params(2))(*([proj] * 15))


def _attn_bwd(proj, d_o, o, lse, g, into):
    s = proj.shape[0]
    dil = DILATIONS[g]
    n = s // QBLK
    scale = HEAD ** -0.5
    tiles = _band_tiles(dil)

    def body(*refs):
        q_ref, kp_ref, kc_ref, vp_ref, vc_ref, do_ref, o_ref, lse_ref = refs[0:8]
        dq_ref, dk_ref, dv_ref, kbuf, vbuf, dkbuf, dvbuf, dqbuf, qbuf = refs[-9:]
        i = pl.program_id(1)

        @pl.when(i == 0)
        def _():
            dkbuf[0:QBLK, :] = jnp.zeros((QBLK, HEAD), _F32)
            dvbuf[0:QBLK, :] = jnp.zeros((QBLK, HEAD), _F32)

        @pl.when(i < n)
        def _():
            qbuf[...] = q_ref[...].astype(_F32)
            kbuf[0:QBLK, :] = kp_ref[...].astype(_F32)
            kbuf[QBLK:2 * QBLK, :] = kc_ref[...].astype(_F32)
            vbuf[0:QBLK, :] = vp_ref[...].astype(_F32)
            vbuf[QBLK:2 * QBLK, :] = vc_ref[...].astype(_F32)
            dkbuf[QBLK:2 * QBLK, :] = jnp.zeros((QBLK, HEAD), _F32)
            dvbuf[QBLK:2 * QBLK, :] = jnp.zeros((QBLK, HEAD), _F32)
            for qs, ks, b in tiles:
                qsl = _strided(qs, BAND, dil)
                ksl = _strided(ks, 2 * BAND, dil)
                q = qbuf[qsl, :].astype(_MXU)
                kk = kbuf[ksl, :].astype(_MXU)
                vv = vbuf[ksl, :].astype(_MXU)
                dov = do_ref[qsl, :]
                dd = jnp.sum(dov * o_ref[qsl, :], axis=-1, keepdims=True)
                lse_t = lse_ref[qsl, :][:, 0:1]
                sc = lax.dot_general(q, kk, _NT, preferred_element_type=_F32) * scale
                p = jnp.where(_band_mask(i, b), jnp.exp(sc - lse_t), 0.0)
                dob = dov.astype(_MXU)
                dp = lax.dot_general(dob, vv, _NT, preferred_element_type=_F32)
                ds = (p * (dp - dd) * scale).astype(_MXU)
                dqbuf[qsl, :] = jnp.dot(ds, kk, preferred_element_type=_F32)
                dkbuf[ksl, :] += lax.dot_general(ds, q, _TN, preferred_element_type=_F32)
                dvbuf[ksl, :] += lax.dot_general(p.astype(_MXU), dob, _TN, preferred_element_type=_F32)
            dq_ref[...] = dqbuf[...].astype(dq_ref.dtype)

        dk_ref[...] = dkbuf[0:QBLK, :].astype(dk_ref.dtype)
        dv_ref[...] = dvbuf[0:QBLK, :].astype(dv_ref.dtype)
        dkbuf[0:QBLK, :] = dkbuf[QBLK:2 * QBLK, :]
        dvbuf[0:QBLK, :] = dvbuf[QBLK:2 * QBLK, :]

    blk = (QBLK, HEAD)
    cq, ck, cv = g * HEADS, 12 + g * HEADS, 24 + g * HEADS

    def cur(i):
        return jnp.minimum(i, n - 1)

    def prev(i):
        return jnp.maximum(jnp.minimum(i, n - 1) - 1, 0)

    own = pl.BlockSpec(blk, lambda j, i: (cur(i), j))
    own_out = pl.BlockSpec(blk, lambda j, i: (cur(i), cq + j))
    late_out = pl.BlockSpec(blk, lambda j, i: (jnp.maximum(i - 1, 0), cq + j))
    extra = [] if into is None else list(into)
    return pl.pallas_call(
        body, name="attn_bwd_d%d" % dil, grid=(HEADS, n + 1),
        in_specs=[pl.BlockSpec(blk, lambda j, i: (cur(i), cq + j)),
                  pl.BlockSpec(blk, lambda j, i: (prev(i), ck + j)),
                  pl.BlockSpec(blk, lambda j, i: (cur(i), ck + j)),
                  pl.BlockSpec(blk, lambda j, i: (prev(i), cv + j)),
                  pl.BlockSpec(blk, lambda j, i: (cur(i), cv + j)),
                  own, own, own] + [pl.BlockSpec(memory_space=pl.ANY)] * len(extra),
        out_specs=[own_out, late_out, late_out], out_shape=[_sds((s, QKV_W), _MXU)] * 3,
        input_output_aliases={8 + t: t for t in range(len(extra))},
        scratch_shapes=[pltpu.VMEM((2 * QBLK, HEAD), _F32)] * 4 + [pltpu.VMEM((QBLK, HEAD), _F32)] * 2,
        compiler_params=_params(2))(proj, proj, proj, proj, proj, d_o, o, lse, *extra)


_PARTS = ((0, 2), (2, 2), (4, 2), (6, 6))
_CHUNK = 768


def _d_h(l, parts, w_in):
    s = parts[0].shape[0]
    nk = IN_W // _CHUNK

    def body(p0, p1, p2, p3, w_ref, o_ref, acc):
        k = pl.program_id(2)

        @pl.when(k == 0)
        def _():
            acc[...] = jnp.zeros_like(acc)

        for p_ref, (first, cnt) in zip((p0, p1, p2, p3), _PARTS):
            @pl.when((k >= first) & (k < first + cnt))
            def _(p_ref=p_ref):
                acc[...] += lax.dot_general(p_ref[...].astype(_MXU), w_ref[...], _NT, preferred_element_type=_F32)

        @pl.when(k == nk - 1)
        def _():
            o_ref[...] = acc[...]

    def part_spec(first, cnt):
        return pl.BlockSpec((1024, _CHUNK), lambda m, n, k: (m, jnp.clip(k - first, 0, cnt - 1)))

    return pl.pallas_call(
        body, name="d_h", grid=(s // 1024, 1, nk),
        in_specs=[part_spec(*p) for p in _PARTS]
        + [pl.BlockSpec((None, None, D, _CHUNK), lambda m, n, k: (k // 3, l, 0, k % 3))],
        out_specs=pl.BlockSpec((1024, D), lambda m, n, k: (m, 0)), out_shape=_sds((s, D)),
        scratch_shapes=[pltpu.VMEM((1024, D), _F32)], compiler_params=_params(3))(*parts, w_in)


def _g_w_in(l, h_t, parts, into):
    s = h_t.shape[1]
    nk = s // 1024

    def body(*refs):
        h_ref, p_refs = refs[0], refs[1:5]
        o_ref, acc = refs[-2], refs[-1]
        n = pl.program_id(1)
        k = pl.program_id(2)

        @pl.when(k == 0)
        def _():
            acc[...] = jnp.zeros_like(acc)

        for p_ref, (first, cnt) in zip(p_refs, _PARTS):
            @pl.when((n >= first) & (n < first + cnt))
            def _(p_ref=p_ref):
                acc[...] += jnp.dot(h_ref[...], p_ref[...].astype(_MXU), preferred_element_type=_F32)

        @pl.when(k == nk - 1)
        def _():
            o_ref[...] = acc[...]

    def part_spec(first, cnt):
        def index(m, n, k):
            row = jnp.where(n < first, 0, jnp.where(n >= first + cnt, nk - 1, k))
            return (row, jnp.clip(n - first, 0, cnt - 1))
        return pl.BlockSpec((1024, _CHUNK), index)

    extra = [] if into is None else [into]
    return pl.pallas_call(
        body, name="g_w_in", grid=(1, IN_W // _CHUNK, nk),
        in_specs=[pl.BlockSpec((D, 1024), lambda m, n, k: (0, k))] + [part_spec(*p) for p in _PARTS]
        + [pl.BlockSpec(memory_space=pl.ANY)] * len(extra),
        out_specs=pl.BlockSpec((None, None, D, _CHUNK), lambda m, n, k: (l, n // 3, 0, n % 3)),
        out_shape=_sds((2, N_CHIPS, D, 2304)), input_output_aliases={5: 0} if extra else {},
        scratch_shapes=[pltpu.VMEM((D, _CHUNK), _F32)], compiler_params=_params(3))(h_t, *parts, *extra)


def _layer_fwd(l, x, p, gw):
    s = x.shape[0]
    nm = s // 1024
    h, h_t = _prenorm_fwd(x, p["g_pre"], p["shift"], p["scale"])
    proj = _mm("proj", h, gw["w_in"], _sds((s, IN_W), _MXU), grid=(nm, N_CHIPS, 1),
               a_spec=pl.BlockSpec((1024, D), lambda m, n, k: (m, 0)),
               b_spec=pl.BlockSpec((None, None, D, 2304), lambda m, n, k: (n, l, 0, 0)),
               o_spec=pl.BlockSpec((1024, 2304), lambda m, n, k: (m, n)), dims=_NN, acc_shape=(1024, 2304))
    o, lse = _attn_fwd(proj)
    uc = _conv_fwd(proj, p["conv_w"], p["conv_b"])
    h_lru = _scan_fwd(uc, p["wt"], p["b_rg"], p["b_ig"], p["lam"])
    a_att, b_act, y_a, y_b, z, out, x_new = _tail_fwd(l, o, h_lru, proj, x, p["gate"], p["g_post"], gw)
    saved = dict(x=x, h_t=h_t, proj=proj, o=o, lse=lse, uc=uc, h_lru=h_lru, a_att=a_att, b_act=b_act,
                 y_a=y_a, y_b=y_b, z=z, out=out)
    return x_new, saved


def _layer_bwd(l, dx, p, gw, sv, big):
    s = dx.shape[0]
    nm = s // 1024
    nt = s // 2048
    proj = sv["proj"]
    d_out, dy_a, dy_b, d_rest, d_o, dh_lru, d_gate, d_gpost = _tail_bwd(
        l, dx, sv["out"], sv["y_a"], sv["y_b"], proj, sv["o"], sv["h_lru"], p["gate"], p["g_post"], gw)

    def wgrad_rows(name, a, b, into):
        return _mm(name, a, b, _sds((2, N_CHIPS, 256, D)), grid=(4, 1, nt),
                   a_spec=pl.BlockSpec((2048, 256), lambda m, n, k: (k, m)),
                   b_spec=pl.BlockSpec((2048, D), lambda m, n, k: (k, 0)),
                   o_spec=pl.BlockSpec((None, None, 256, D), lambda m, n, k: (l, m, 0, 0)),
                   dims=_TN, acc_shape=(256, D), into=into)

    big = dict(big)
    big["w_o"] = wgrad_rows("g_w_o", sv["z"], d_out, big.get("w_o"))
    big["w_pa"] = _mm("g_w_pa", sv["a_att"], dy_a, _sds((2, N_CHIPS, ATT_W, 256)), grid=(1, 4, nt),
                      a_spec=pl.BlockSpec((2048, ATT_W), lambda m, n, k: (k, 0)),
                      b_spec=pl.BlockSpec((2048, 256), lambda m, n, k: (k, n)),
                      o_spec=pl.BlockSpec((None, None, ATT_W, 256), lambda m, n, k: (l, n, 0, 0)),
                      dims=_TN, acc_shape=(ATT_W, 256), into=big.get("w_pa"))
    big["w_pb"] = wgrad_rows("g_w_pb", sv["b_act"], dy_b, big.get("w_pb"))
    duc, g_wt, d_brg, d_big, d_lam = _scan_bwd(dh_lru, sv["uc"], sv["h_lru"], p["wt"],
                                               p["b_rg"], p["b_ig"], p["lam"])
    g_wrg, g_wig = _gate_tile_grads(g_wt)
    d_rest, g_convw, g_convb = _conv_bwd(duc, proj, p["conv_w"], d_rest)
    dqkv = None
    for g in range(3):
        dqkv = _attn_bwd(proj, d_o, sv["o"], sv["lse"], g, dqkv)
    parts = (dqkv[0], dqkv[1], dqkv[2], d_rest)
    dh = _d_h(l, parts, gw["w_in"])
    big["w_in"] = _g_w_in(l, sv["h_t"], parts, big.get("w_in"))
    dx_in, d_shift, d_scale, d_gpre = _prenorm_bwd(dh, sv["x"], dx, p["g_pre"], p["scale"])
    small = dict(dmod=jnp.concatenate([d_shift, d_scale, d_gate], axis=1), g_pre=d_gpre, conv_w=g_convw,
                 conv_b=g_convb, w_rg=g_wrg, b_rg=d_brg, w_ig=g_wig, b_ig=d_big, lam=d_lam, g_post=d_gpost)
    return dx_in, small, big


def _local_step(x, target, small_p, gw):
    saved = []
    h = x
    for l in range(2):
        h, sv = _layer_fwd(l, h, small_p[l], gw)
        saved.append(sv)
    dy, sq = _loss_head(h, target)
    loss = 0.5 * jnp.sum(sq) / D
    big = {}
    smalls = [None, None]
    dx = dy
    for l in (1, 0):
        dx, smalls[l], big = _layer_bwd(l, dx, small_p[l], gw, saved[l], big)
    return loss, dx, smalls, big


_SMALL_ROWS = 8 + 16 + 8 + 128 + 128


def _pack_small(smalls):
    dmod = jnp.concatenate([smalls[0]["dmod"].reshape(3, D), smalls[1]["dmod"].reshape(3, D),
                            jnp.zeros((2, D), _F32)], axis=0)
    vecs = jnp.concatenate([smalls[l][k] for k in ("g_pre", "conv_b", "b_rg", "b_ig", "lam", "g_post")
                            for l in range(2)] + [jnp.zeros((4, D), _F32)], axis=0)
    convw = jnp.concatenate([smalls[0]["conv_w"], smalls[1]["conv_w"]], axis=0)
    wrg = jnp.stack([smalls[0]["w_rg"], smalls[1]["w_rg"]]).reshape(128, D)
    wig = jnp.stack([smalls[0]["w_ig"], smalls[1]["w_ig"]]).reshape(128, D)
    return jnp.concatenate([dmod, vecs, convw, wrg, wig], axis=0)


def kernel(x, c, w_mod, b_mod, g_pre, w_in, conv_w, conv_b, w_rg, b_rg, w_ig, b_ig, lru_lambda, w_pa, w_pb, w_o, g_post, loss_target, m_w_mod, m_b_mod, m_g_pre, m_w_in, m_conv_w, m_conv_b, m_w_rg, m_b_rg, m_w_ig, m_b_ig, m_lru_lambda, m_w_pa, m_w_pb, m_w_o, m_g_post, v_w_mod, v_b_mod, v_g_pre, v_w_in, v_conv_w, v_conv_b, v_w_rg, v_b_rg, v_w_ig, v_b_ig, v_lru_lambda, v_w_pa, v_w_pb, v_w_o, v_g_post):
    xi, yi, ci = lax.axis_index("x"), lax.axis_index("y"), lax.axis_index("c")
    chip = 2 * xi + yi
    dev = 4 * xi + 2 * yi + ci
    mcols = w_mod.shape[2]

    pack1 = jnp.concatenate([jnp.broadcast_to(c, (8, D)),
                             jnp.pad(conv_w.reshape(8, 256), ((0, 0), (0, D - 256)))], axis=0)
    g1 = _exchange("gather_cond", [pack1], "xyc", False)[0]
    c_all = g1[:, 0, :]
    conv_w_full = jnp.transpose(g1[0::2, 8:16, 0:256], (1, 0, 2)).reshape(2, 4, D)

    b_cols = lax.dynamic_slice(b_mod, (0, chip * mcols), (2, mcols)).reshape(2, 1, mcols)
    mod_loc = _mod_fwd(c_all, w_mod, b_cols)
    g2 = _exchange("gather_mod", [mod_loc.reshape(16, mcols)], "xyc", False)[0]
    mod_full = jnp.transpose(g2[0::2], (1, 0, 2)).reshape(2, 8, 3 * D)
    mod_me = lax.dynamic_index_in_dim(mod_full, dev, axis=1, keepdims=False)

    wb = [_cast("cast_w_in", w_in.reshape(2 * D, 2304), 256).reshape(2, D, 2304),
          _cast("cast_w_pa", w_pa.reshape(2 * ATT_W, 256), 256).reshape(2, ATT_W, 256),
          _cast("cast_w_pb", w_pb.reshape(512, D), 256).reshape(2, 256, D),
          _cast("cast_w_o", w_o.reshape(512, D), 256).reshape(2, 256, D)]
    gl = _gather_weights(wb, [4, 1, 1, 1])
    gw = dict(w_in=gl[0], w_pa=gl[1], w_pb=gl[2], w_o=gl[3])

    small_p = []
    for l in range(2):
        gates = _gate_tiles(w_rg[l], w_ig[l]).astype(_MXU)
        small_p.append(dict(
            shift=mod_me[l:l + 1, 0:D], scale=mod_me[l:l + 1, D:2 * D], gate=mod_me[l:l + 1, 2 * D:3 * D],
            g_pre=g_pre[l:l + 1], conv_w=conv_w_full[l], conv_b=conv_b[l:l + 1], wt=gates,
            b_rg=b_rg[l:l + 1], b_ig=b_ig[l:l + 1], lam=lru_lambda[l:l + 1], g_post=g_post[l:l + 1]))

    loss_loc, dx, smalls, big = _local_step(x[0], loss_target[0], small_p, gw)
    loss = lax.psum(loss_loc, ("x", "y", "c"))
    grad_x = dx[None]

    names = ("w_in", "w_pa", "w_pb", "w_o")
    core = jnp.reshape(ci, (1,)).astype(jnp.int32)
    where = jnp.stack([chip, ci]).astype(jnp.int32)
    pair = list(_exchange("reduce_pair", [big["w_in"].reshape(2, 16, 256, 2304)] + [big[k] for k in names[1:]],
                          "c", True, local=False, nchunks=[16, 4, 4, 4]))
    pair[0] = pair[0].reshape(N_CHIPS, D, 2304)
    t1 = [_sum_pair("sum_pair_" + k, big[k], r, core, 128) for k, r in zip(names, pair)]
    quad = _exchange("reduce_chips", [t[1] for t in t1], "xy", True, local=False, nchunks=[4, 1, 1, 1])
    t3 = [_sum_chips("sum_chips_" + k, t[0], r, where, 128) for k, t, r in zip(names, t1, quad)]
    both = _pair_fill("gather_layers", t3, [4, 1, 1, 1])
    g_big = dict(zip(names, both))

    g3 = _exchange("gather_small", [_pack_small(smalls)], "xyc", False)[0]
    tot = _sum_lead("sum_small", g3, 96)
    dmod_all = g3[:, 0:6, :].reshape(8, 2, 3 * D)
    dm_cols = jnp.transpose(lax.dynamic_slice(dmod_all, (0, 0, chip * mcols), (8, 2, mcols)), (1, 0, 2))
    g_w_mod = _mod_bwd(jnp.transpose(c_all), dm_cols)
    vec = tot[8:20].reshape(6, 2, D)
    g_conv_w_full = tot[24:32].reshape(2, 4, D)
    grads = dict(
        w_mod=g_w_mod, b_mod=tot[0:6].reshape(2, 3 * D), g_pre=vec[0], w_in=g_big["w_in"],
        conv_w=lax.dynamic_slice(g_conv_w_full, (0, 0, chip * 256), (2, 4, 256)), conv_b=vec[1],
        w_rg=tot[32:160].reshape(2, 16, 64, 64), b_rg=vec[2], w_ig=tot[160:288].reshape(2, 16, 64, 64),
        b_ig=vec[3], lru_lambda=vec[4], w_pa=g_big["w_pa"], w_pb=g_big["w_pb"], w_o=g_big["w_o"],
        g_post=vec[5])

    weights = dict(w_mod=w_mod, b_mod=b_mod, g_pre=g_pre, w_in=w_in, conv_w=conv_w, conv_b=conv_b, w_rg=w_rg,
                   b_rg=b_rg, w_ig=w_ig, b_ig=b_ig, lru_lambda=lru_lambda, w_pa=w_pa, w_pb=w_pb, w_o=w_o,
                   g_post=g_post)
    ms = dict(w_mod=m_w_mod, b_mod=m_b_mod, g_pre=m_g_pre, w_in=m_w_in, conv_w=m_conv_w, conv_b=m_conv_b,
              w_rg=m_w_rg, b_rg=m_b_rg, w_ig=m_w_ig, b_ig=m_b_ig, lru_lambda=m_lru_lambda, w_pa=m_w_pa,
              w_pb=m_w_pb, w_o=m_w_o, g_post=m_g_post)
    vs = dict(w_mod=v_w_mod, b_mod=v_b_mod, g_pre=v_g_pre, w_in=v_w_in, conv_w=v_conv_w, conv_b=v_conv_b,
              w_rg=v_w_rg, b_rg=v_b_rg, w_ig=v_w_ig, b_ig=v_b_ig, lru_lambda=v_lru_lambda, w_pa=v_w_pa,
              w_pb=v_w_pb, w_o=v_w_o, g_post=v_g_post)
    flat = dict(w_mod=(2 * D, mcols, 256), b_mod=(2, 3 * D, 2), g_pre=(2, D, 2), w_in=(2 * D, 2304, 256),
                conv_w=(8, 256, 8), conv_b=(2, D, 2), w_rg=(128, D, 128), b_rg=(2, D, 2), w_ig=(128, D, 128),
                b_ig=(2, D, 2), lru_lambda=(2, D, 2), w_pa=(2 * ATT_W, 256, 256), w_pb=(512, D, 256),
                w_o=(512, D, 256), g_post=(2, D, 2))
    order = ("w_mod", "b_mod", "g_pre", "w_in", "conv_w", "conv_b", "w_rg", "b_rg", "w_ig", "b_ig",
             "lru_lambda", "w_pa", "w_pb", "w_o", "g_post")
    deltas, new_m, new_v = [], [], []
    for k in order:
        rows, cols, tb = flat[k]
        shp = weights[k].shape
        d, nm_, nv_ = _adamw("adamw_" + k, weights[k].reshape(rows, cols), grads[k].reshape(rows, cols),
                             ms[k].reshape(rows, cols), vs[k].reshape(rows, cols), tb)
        deltas.append(d.reshape(shp))
        new_m.append(nm_.reshape(shp))
        new_v.append(nv_.reshape(shp))
    return (loss, grad_x, *[grads[k].reshape(weights[k].shape) for k in order], *deltas, *new_m, *new_v)
```

```python
import functools

import jax
import jax.numpy as jnp
from jax import lax
from jax.experimental import pallas as pl
from jax.experimental.pallas import tpu as pltpu

_F32 = jnp.float32
_MXU = jnp.bfloat16
_VMEM_LIMIT = 56 * 1024 * 1024
_MESH = pl.DeviceIdType.MESH

D = 1024
HEAD = 128
HEADS = 4
ATT_W = 512
QKV_W = 1536
IN_W = 9216
DILATIONS = (1, 4, 16)
BAND = 128
QBLK = BAND * 16
NORM_EPS = 1e-6
NEG_INF = -1e30
LRU_C = 8.0
N_CHIPS = 4
CB_GATT = 4608 // 512
CB_U, CB_GLRU, CB_MA, CB_MB = 5, 6, 7, 8
R_U, R_GLRU, R_MA, R_MB, R_END = 512, 1536, 2560, 3584, 4608

ADAM_LR, ADAM_B1, ADAM_B2, ADAM_EPS, ADAM_WD, ADAM_STEP = 0.001, 0.9, 0.999, 1e-08, 0.01, 10


def _params(ngrid):
    return pltpu.CompilerParams(dimension_semantics=("arbitrary",) * ngrid, vmem_limit_bytes=_VMEM_LIMIT)


def _sigmoid(v):
    return 0.5 * jnp.tanh(0.5 * v) + 0.5


_GROUPS = {
    "c": [(0, 0, 1)],
    "xy": [(1, 0, 0), (0, 1, 0), (1, 1, 0)],
    "xyc": [(0, 0, 1), (0, 1, 0), (0, 1, 1), (1, 0, 0), (1, 0, 1), (1, 1, 0), (1, 1, 1)],
}


def _rank(group, px, py, pc):
    if group == "c":
        return pc
    if group == "xy":
        return 2 * px + py
    return 4 * px + 2 * py + pc


def _flip(rel, x, y, c):
    dx, dy, dc = rel
    return (1 - x if dx else x, 1 - y if dy else y, 1 - c if dc else c)


def _pieces(ref, nchunk):
    step = ref.shape[0] // nchunk
    return [ref.at[pl.ds(q * step, step)] for q in range(nchunk)]


def _exchange(name, srcs, group, scatter, *, local=True, nchunks=None):
    rels = _GROUPS[group]
    gsize = len(rels) + 1
    n = len(srcs)
    nchunks = nchunks or [1] * n
    blks = [s.shape[1:] if scatter else s.shape for s in srcs]
    slotted = local or gsize > 2
    base = [sum(nchunks[:a]) for a in range(n)]
    tot = sum(nchunks)

    def body(*refs):
        src_refs, out_refs = refs[:n], refs[n:2 * n]
        send_sems, recv_sems, loc_sems = refs[2 * n:]
        x, y, c = lax.axis_index("x"), lax.axis_index("y"), lax.axis_index("c")
        me = _rank(group, x, y, c)
        copies = []
        for a in range(n):
            def part(r, a=a):
                return src_refs[a].at[r] if scatter else src_refs[a]
            dst = out_refs[a].at[me] if slotted else out_refs[a]
            if local:
                for q, (s_, d_) in enumerate(zip(_pieces(part(me), nchunks[a]), _pieces(dst, nchunks[a]))):
                    loc = pltpu.make_async_copy(s_, d_, loc_sems.at[base[a] + q])
                    loc.start()
                    copies.append(loc)
            for k, rel in enumerate(rels):
                peer = _flip(rel, x, y, c)
                for q, (s_, d_) in enumerate(zip(_pieces(part(_rank(group, *peer)), nchunks[a]),
                                                 _pieces(dst, nchunks[a]))):
                    cp = pltpu.make_async_remote_copy(
                        src_ref=s_, dst_ref=d_, send_sem=send_sems.at[(base[a] + q) * len(rels) + k],
                        recv_sem=recv_sems.at[(base[a] + q) * len(rels) + k],
                        device_id=peer, device_id_type=_MESH)
                    cp.start()
                    copies.append(cp)
        for cp in copies:
            cp.wait()

    any_spec = pl.BlockSpec(memory_space=pl.ANY)
    lead = (gsize,) if slotted else ()
    return pl.pallas_call(
        body, name=name,
        out_shape=[jax.ShapeDtypeStruct(lead + tuple(b), s.dtype) for b, s in zip(blks, srcs)],
        in_specs=[any_spec] * n, out_specs=[any_spec] * n,
        scratch_shapes=[pltpu.SemaphoreType.DMA((tot * len(rels),)), pltpu.SemaphoreType.DMA((tot * len(rels),)),
                        pltpu.SemaphoreType.DMA((tot,))],
    )(*srcs)


def _pair_fill(name, arrs, nchunks):
    n = len(arrs)
    base = [sum(nchunks[:a]) for a in range(n)]
    tot = sum(nchunks)

    def body(*refs):
        out_refs = refs[n:2 * n]
        send_sems, recv_sems = refs[2 * n:]
        x, y, c = lax.axis_index("x"), lax.axis_index("y"), lax.axis_index("c")
        copies = []
        for a in range(n):
            for q, blk in enumerate(_pieces(out_refs[a].at[c], nchunks[a])):
                cp = pltpu.make_async_remote_copy(
                    src_ref=blk, dst_ref=blk, send_sem=send_sems.at[base[a] + q], recv_sem=recv_sems.at[base[a] + q],
                    device_id=(x, y, 1 - c), device_id_type=_MESH)
                cp.start()
                copies.append(cp)
        for cp in copies:
            cp.wait()

    any_spec = pl.BlockSpec(memory_space=pl.ANY)
    return pl.pallas_call(
        body, name=name, out_shape=[jax.ShapeDtypeStruct(a.shape, a.dtype) for a in arrs],
        in_specs=[any_spec] * n, out_specs=[any_spec] * n, input_output_aliases={a: a for a in range(n)},
        scratch_shapes=[pltpu.SemaphoreType.DMA((tot,)), pltpu.SemaphoreType.DMA((tot,))],
    )(*arrs)


def _gather_weights(wb, nchunks):
    n = len(wb)
    rels = _GROUPS["xy"]
    base = [sum(nchunks[:a]) for a in range(n)]
    tot = sum(nchunks)

    def body(*refs):
        src_refs, out_refs = refs[:n], refs[n:2 * n]
        ici_send, ici_recv, d2d_send, d2d_recv, loc_sems = refs[2 * n:]
        x, y, c = lax.axis_index("x"), lax.axis_index("y"), lax.axis_index("c")
        me = 2 * x + y
        waits = []
        for a in range(n):
            for l in range(2):
                for q, (s_, d_) in enumerate(zip(_pieces(src_refs[a].at[l], nchunks[a]),
                                                 _pieces(out_refs[a].at[me, l], nchunks[a]))):
                    loc = pltpu.make_async_copy(s_, d_, loc_sems.at[(base[a] + q) * 2 + l])
                    loc.start()
                    waits.append(loc)
        first = []
        for a in range(n):
            for k, rel in enumerate(rels):
                px, py, _ = _flip(rel, x, y, c)
                for q, (s_, d_) in enumerate(zip(_pieces(src_refs[a].at[c], nchunks[a]),
                                                 _pieces(out_refs[a].at[me, c], nchunks[a]))):
                    sem = (base[a] + q) * 3 + k
                    cp = pltpu.make_async_remote_copy(src_ref=s_, dst_ref=d_, send_sem=ici_send.at[sem],
                                                      recv_sem=ici_recv.at[sem], device_id=(px, py, c),
                                                      device_id_type=_MESH)
                    cp.start()
                    first.append(cp)
        second = []
        for a in range(n):
            for k, rel in enumerate(rels):
                px, py, _ = _flip(rel, x, y, c)
                for q, blk in enumerate(_pieces(out_refs[a].at[2 * px + py, c], nchunks[a])):
                    sem = (base[a] + q) * 3 + k
                    landed = pltpu.make_async_remote_copy(src_ref=blk, dst_ref=blk, send_sem=ici_send.at[sem],
                                                          recv_sem=ici_recv.at[sem], device_id=(px, py, c),
                                                          device_id_type=_MESH)
                    landed.wait_recv()
                    cp = pltpu.make_async_remote_copy(src_ref=blk, dst_ref=blk, send_sem=d2d_send.at[sem],
                                                      recv_sem=d2d_recv.at[sem], device_id=(x, y, 1 - c),
                                                      device_id_type=_MESH)
                    cp.start()
                    second.append(cp)
        for cp in first:
            cp.wait_send()
        for cp in second:
            cp.wait_send()
        for a in range(n):
            for k, rel in enumerate(rels):
                px, py, _ = _flip(rel, x, y, c)
                for q, blk in enumerate(_pieces(out_refs[a].at[2 * px + py, 1 - c], nchunks[a])):
                    sem = (base[a] + q) * 3 + k
                    pltpu.make_async_remote_copy(src_ref=blk, dst_ref=blk, send_sem=d2d_send.at[sem],
                                                 recv_sem=d2d_recv.at[sem], device_id=(x, y, 1 - c),
                                                 device_id_type=_MESH).wait_recv()
        for cp in waits:
            cp.wait()

    any_spec = pl.BlockSpec(memory_space=pl.ANY)
    return pl.pallas_call(
        body, name="gather_weights",
        out_shape=[jax.ShapeDtypeStruct((N_CHIPS,) + a.shape, a.dtype) for a in wb],
        in_specs=[any_spec] * n, out_specs=[any_spec] * n,
        scratch_shapes=[pltpu.SemaphoreType.DMA((tot * 3,))] * 4 + [pltpu.SemaphoreType.DMA((tot * 2,))],
    )(*wb)


_HBM = pl.BlockSpec(memory_space=pltpu.HBM)
_SEM = pl.BlockSpec(memory_space=pltpu.SEMAPHORE)
_EFFECT = pltpu.SideEffectType.DATAFLOW_SIDE_EFFECTING


def _own_slot(srcs, nchunks):
    n = len(srcs)
    base = [sum(nchunks[:a]) for a in range(n)]

    def body(*refs):
        src_refs, out_refs, sems = refs[:n], refs[n:2 * n], refs[2 * n]
        me = 2 * lax.axis_index("x") + lax.axis_index("y")
        copies = []
        for a in range(n):
            for q, (s_, d_) in enumerate(zip(_pieces(src_refs[a], nchunks[a]), _pieces(out_refs[a].at[me], nchunks[a]))):
                cp = pltpu.make_async_copy(s_, d_, sems.at[base[a] + q])
                cp.start()
                copies.append(cp)
        for cp in copies:
            cp.wait()

    any_spec = pl.BlockSpec(memory_space=pl.ANY)
    return pl.pallas_call(
        body, name="own_slot", out_shape=[jax.ShapeDtypeStruct((N_CHIPS,) + a.shape, a.dtype) for a in srcs],
        in_specs=[any_spec] * n, out_specs=[any_spec] * n,
        scratch_shapes=[pltpu.SemaphoreType.DMA((sum(nchunks),))])(*srcs)


def _late_copies(src_refs, land_refs, send_sems, recv_sems, nchunks):
    x, y, c = lax.axis_index("x"), lax.axis_index("y"), lax.axis_index("c")
    me = 2 * x + y
    rels = _GROUPS["xy"]
    copies = []
    idx = 0
    for a in range(len(src_refs)):
        for rel in rels:
            px, py, _ = _flip(rel, x, y, c)
            for s_, d_ in zip(_pieces(src_refs[a], nchunks[a]), _pieces(land_refs[a].at[me], nchunks[a])):
                copies.append(pltpu.make_async_remote_copy(
                    src_ref=s_, dst_ref=d_, send_sem=send_sems.at[idx], recv_sem=recv_sems.at[idx],
                    device_id=(px, py, c), device_id_type=_MESH))
                idx += 1
    return copies


def _late_gather_start(srcs, lands, nchunks, after):
    n = len(srcs)
    tot = 3 * sum(nchunks)

    def body(*refs):
        src_refs, land_refs = refs[:n], refs[n:2 * n]
        send_sems, recv_sems = refs[2 * n + 1], refs[2 * n + 2]
        token = refs[-1]
        for cp in _late_copies(src_refs, land_refs, send_sems, recv_sems, nchunks):
            cp.start()
        token[...] = jnp.zeros_like(token)

    hbm = [pltpu.HBM(a.shape, a.dtype) for a in list(srcs) + list(lands)]
    outs = pl.pallas_call(
        body, name="late_gather_start",
        out_shape=(pltpu.SemaphoreType.DMA((tot,)), pltpu.SemaphoreType.DMA((tot,)), *hbm, _sds((8, 128))),
        in_specs=[_HBM] * (2 * n) + [pl.BlockSpec(memory_space=pl.ANY)],
        out_specs=(_SEM, _SEM, *([_HBM] * (2 * n)), pl.BlockSpec(memory_space=pltpu.VMEM)),
        input_output_aliases={i: 2 + i for i in range(2 * n)},
        compiler_params=pltpu.CompilerParams(has_side_effects=_EFFECT),
    )(*[pltpu.with_memory_space_constraint(a, pltpu.HBM) for a in list(srcs) + list(lands)], after)
    return outs[0], outs[1], outs[2:2 + n], outs[2 + n:2 + 2 * n], outs[-1]


def _late_gather_wait(send_sems, recv_sems, srcs, lands, nchunks, after):
    n = len(srcs)

    def body(*refs):
        src_refs, land_refs = refs[:n], refs[n:2 * n]
        s_sems, r_sems = refs[2 * n], refs[2 * n + 1]
        for cp in _late_copies(src_refs, land_refs, s_sems, r_sems, nchunks):
            cp.wait_send()
            cp.wait_recv()

    hbm = [pltpu.HBM(a.shape, a.dtype) for a in list(srcs) + list(lands)]
    outs = pl.pallas_call(
        body, name="late_gather_wait", out_shape=tuple(hbm),
        in_specs=[_HBM] * (2 * n) + [_SEM, _SEM, pl.BlockSpec(memory_space=pl.ANY)],
        out_specs=tuple([_HBM] * (2 * n)), input_output_aliases={i: i for i in range(2 * n)},
        compiler_params=pltpu.CompilerParams(has_side_effects=_EFFECT),
    )(*srcs, *lands, send_sems, recv_sems, after)
    return outs[n:2 * n]


def _mm(name, a, b, out_sds, *, grid, a_spec, b_spec, o_spec, dims, acc_shape, into=None):
    nk = grid[2]

    def body(*refs):
        a_ref, b_ref = refs[0], refs[1]
        o_ref, acc = refs[-2], refs[-1]
        k = pl.program_id(2)
        part = lax.dot_general(a_ref[...].astype(_MXU), b_ref[...].astype(_MXU), dims,
                               preferred_element_type=_F32)
        if nk == 1:
            o_ref[...] = part.astype(o_ref.dtype)
            return

        @pl.when(k == 0)
        def _():
            acc[...] = part

        @pl.when(k > 0)
        def _():
            acc[...] += part

        @pl.when(k == nk - 1)
        def _():
            o_ref[...] = acc[...].astype(o_ref.dtype)

    if nk == 1:
        acc_shape = (8, 128)
    in_specs = [a_spec, b_spec]
    args = [a, b]
    aliases = {}
    if into is not None:
        in_specs.append(pl.BlockSpec(memory_space=pl.ANY))
        args.append(into)
        aliases = {2: 0}
    return pl.pallas_call(
        body, name=name, grid=grid, in_specs=in_specs, out_specs=o_spec, out_shape=out_sds,
        scratch_shapes=[pltpu.VMEM(acc_shape, _F32)], input_output_aliases=aliases,
        compiler_params=_params(3))(*args)


_NN = (((1,), (0,)), ((), ()))
_NT = (((1,), (1,)), ((), ()))
_TN = (((0,), (0,)), ((), ()))


def _rowwise(name, body, *, grid, ins, outs, scratch=()):
    return pl.pallas_call(
        body, name=name, grid=(grid,), in_specs=[s for _, s in ins], out_specs=[s for _, s in outs],
        out_shape=[o for o, _ in outs], scratch_shapes=list(scratch),
        compiler_params=_params(1))(*[a for a, _ in ins])


def _rows(tb, w, cb=0, n=None):
    if n is None:
        return pl.BlockSpec((tb, w), lambda i: (i, cb))
    return pl.BlockSpec((tb, w), lambda i: (n - 1 - i, cb))


def _vec(shape):
    return pl.BlockSpec(shape, lambda i: (0,) * len(shape))


def _halo_prev(tb, w, cb=0, n=None, rows=8):
    if n is None:
        return pl.BlockSpec((rows, w), lambda i: (jnp.maximum(i * (tb // rows) - 1, 0), cb))
    return pl.BlockSpec((rows, w), lambda i: (jnp.maximum((n - 1 - i) * (tb // rows) - 1, 0), cb))


def _halo_next(tb, w, n, cb=0):
    return pl.BlockSpec((8, w), lambda i: (jnp.minimum((i + 1) * (tb // 8), n * (tb // 8) - 1), cb))


def _sds(shape, dtype=_F32):
    return jax.ShapeDtypeStruct(shape, dtype)


def _cast(name, a, tb):
    rows, cols = a.shape

    def body(a_ref, o_ref):
        o_ref[...] = a_ref[...].astype(o_ref.dtype)

    return _rowwise(name, body, grid=rows // tb, ins=[(a, _rows(tb, cols))],
                    outs=[(_sds((rows, cols), _MXU), _rows(tb, cols))])[0]


def _sum_lead(name, a, tb):
    g, rows, cols = a.shape

    def body(a_ref, o_ref):
        acc = a_ref[0]
        for k in range(1, g):
            acc = acc + a_ref[k]
        o_ref[...] = acc

    return _rowwise(name, body, grid=rows // tb,
                    ins=[(a, pl.BlockSpec((g, tb, cols), lambda i: (0, i, 0)))],
                    outs=[(_sds((rows, cols)), _rows(tb, cols))])[0]


def _sum_pair(name, mine, theirs, core, tb):
    _, nj, rows, cols = mine.shape

    def body(s_ref, a_ref, b_ref, o_ref, ob_ref):
        t = a_ref[...] + b_ref[...]
        o_ref[...] = t
        ob_ref[...] = t.astype(ob_ref.dtype)

    blk = pl.BlockSpec((None, tb, cols), lambda j, i, s: (j, i, 0))
    grid_spec = pltpu.PrefetchScalarGridSpec(
        num_scalar_prefetch=1, grid=(nj, rows // tb),
        in_specs=[pl.BlockSpec((None, None, tb, cols), lambda j, i, s: (s[0], j, i, 0)), blk],
        out_specs=[blk, blk])
    return pl.pallas_call(body, name=name, grid_spec=grid_spec,
                          out_shape=[_sds((nj, rows, cols)), _sds((nj, rows, cols), _MXU)],
                          compiler_params=_params(2))(core, mine, theirs)


def _sum_chips(name, mine, theirs, where, tb):
    _, rows, cols = mine.shape

    def body(s_ref, a_ref, b1_ref, b2_ref, b3_ref, o_ref):
        o_ref[...] = ((a_ref[...] + b1_ref[...].astype(_F32)) + b2_ref[...].astype(_F32)) + b3_ref[...].astype(_F32)

    def slot(k):
        return pl.BlockSpec((None, tb, cols), lambda i, s: (jnp.bitwise_xor(s[0], k), i, 0))

    grid_spec = pltpu.PrefetchScalarGridSpec(
        num_scalar_prefetch=1, grid=(rows // tb,),
        in_specs=[slot(0), slot(1), slot(2), slot(3)],
        out_specs=pl.BlockSpec((None, tb, cols), lambda i, s: (s[1], i, 0)))
    return pl.pallas_call(body, name=name, grid_spec=grid_spec, out_shape=_sds((2, rows, cols)),
                          compiler_params=_params(1))(where, mine, theirs, theirs, theirs)


def _adamw(name, w, g, m, v, tb):
    rows, cols = w.shape
    c1 = 1.0 - ADAM_B1 ** ADAM_STEP
    c2 = 1.0 - ADAM_B2 ** ADAM_STEP

    def body(w_ref, g_ref, m_ref, v_ref, d_ref, nm_ref, nv_ref):
        gv = g_ref[...]
        nm = ADAM_B1 * m_ref[...] + (1.0 - ADAM_B1) * gv
        nv = ADAM_B2 * v_ref[...] + (1.0 - ADAM_B2) * (gv * gv)
        d_ref[...] = -ADAM_LR * ((nm / c1) / (jnp.sqrt(nv / c2) + ADAM_EPS) + ADAM_WD * w_ref[...])
        nm_ref[...] = nm
        nv_ref[...] = nv

    spec = _rows(tb, cols)
    return _rowwise(name, body, grid=rows // tb, ins=[(w, spec), (g, spec), (m, spec), (v, spec)],
                    outs=[(_sds((rows, cols)), spec)] * 3)


def _mod_fwd(c_all, w_mod, b_cols):
    cols = w_mod.shape[2]

    def body(c_ref, w_ref, b_ref, o_ref):
        cv = c_ref[...]
        sc = (cv * _sigmoid(cv)).astype(_MXU)
        o_ref[...] = jnp.dot(sc, w_ref[...].astype(_MXU), preferred_element_type=_F32) + b_ref[...]

    return pl.pallas_call(
        body, name="mod_fwd", grid=(2,),
        in_specs=[pl.BlockSpec((8, D), lambda l: (0, 0)), pl.BlockSpec((None, D, cols), lambda l: (l, 0, 0)),
                  pl.BlockSpec((None, 1, cols), lambda l: (l, 0, 0))],
        out_specs=pl.BlockSpec((None, 8, cols), lambda l: (l, 0, 0)),
        out_shape=_sds((2, 8, cols)), compiler_params=_params(1))(c_all, w_mod, b_cols)


def _mod_bwd(c_all_t, dm):
    cols = dm.shape[2]

    def body(c_ref, d_ref, o_ref):
        cv = c_ref[...]
        sc = (cv * _sigmoid(cv)).astype(_MXU)
        o_ref[...] = jnp.dot(sc, d_ref[...].astype(_MXU), preferred_element_type=_F32)

    return pl.pallas_call(
        body, name="mod_bwd", grid=(2,),
        in_specs=[pl.BlockSpec((D, 8), lambda l: (0, 0)), pl.BlockSpec((None, 8, cols), lambda l: (l, 0, 0))],
        out_specs=pl.BlockSpec((None, D, cols), lambda l: (l, 0, 0)),
        out_shape=_sds((2, D, cols)), compiler_params=_params(1))(c_all_t, dm)


def _prenorm_fwd(x, g_pre, shift, scale):
    s = x.shape[0]
    tb = 512

    def body(x_ref, g_ref, sh_ref, sc_ref, h_ref, ht_ref):
        xv = x_ref[...]
        rstd = lax.rsqrt(jnp.mean(xv * xv, axis=-1, keepdims=True) + NORM_EPS)
        hv = (xv * rstd) * g_ref[...] * (1.0 + sc_ref[...]) + sh_ref[...]
        h_ref[...] = hv.astype(h_ref.dtype)
        ht_ref[...] = hv.T.astype(ht_ref.dtype)

    v = _vec((1, D))
    return _rowwise("prenorm_fwd", body, grid=s // tb,
                    ins=[(x, _rows(tb, D)), (g_pre, v), (shift, v), (scale, v)],
                    outs=[(_sds((s, D), _MXU), _rows(tb, D)),
                          (_sds((D, s), _MXU), pl.BlockSpec((D, tb), lambda i: (0, i)))])


def _shift_down(cur, halo, j, tb):
    ext = jnp.concatenate([halo, cur], axis=0)
    return pltpu.roll(ext, j, 0)[8:8 + tb]


def _shift_up(cur, halo, j, tb):
    ext = jnp.concatenate([cur, halo], axis=0)
    return pltpu.roll(ext, tb + 8 - j, 0)[0:tb]


def _conv_fwd(proj, conv_w, conv_b):
    s = proj.shape[0]
    tb = 512

    def body(u_ref, hp_ref, w_ref, b_ref, o_ref):
        i = pl.program_id(0)
        u = u_ref[...].astype(_F32)
        halo = jnp.where(i > 0, hp_ref[...].astype(_F32)[8:16], 0.0)
        acc = b_ref[...] + u * w_ref[0:1, :]
        for j in range(1, 4):
            acc = acc + _shift_down(u, halo, j, tb) * w_ref[j:j + 1, :]
        o_ref[...] = acc

    return _rowwise("conv_fwd", body, grid=s // tb,
                    ins=[(proj, _rows(tb, D, CB_U)), (proj, _halo_prev(tb, D, CB_U, rows=16)),
                         (conv_w, _vec((4, D))), (conv_b, _vec((1, D)))],
                    outs=[(_sds((s, D)), _rows(tb, D))])[0]


def _lru_gates(pre_r, pre_i, uc, b_rg, b_ig, lam):
    r = _sigmoid(pre_r + b_rg)
    ig = _sigmoid(pre_i + b_ig)
    nl = -lam
    sp = jnp.maximum(nl, 0.0) + jnp.log(1.0 + jnp.exp(-jnp.abs(nl)))
    la = -LRU_C * r * sp
    a = jnp.exp(la)
    one_m_a2 = -jnp.tanh(la) * (a * a + 1.0)
    inv_sq = lax.rsqrt(jnp.maximum(one_m_a2, 1e-30))
    return r, ig, sp, a, one_m_a2 * inv_sq, inv_sq


GATE_TILES = 8


def _gate_tiles(w_rg, w_ig):
    eye = jnp.eye(2, dtype=w_rg.dtype)

    def tiles(w):
        return jnp.einsum("cpij,pq->cpiqj", w.reshape(GATE_TILES, 2, 64, 64), eye).reshape(GATE_TILES, 128, 128)

    return jnp.concatenate([tiles(w_rg), tiles(w_ig)], axis=2)


def _gate_tile_grads(gw):
    keep = jnp.eye(2, dtype=jnp.bool_)[None, :, None, :, None]

    def blocks(t):
        t5 = t.reshape(GATE_TILES, 2, 64, 2, 64)
        return jnp.sum(jnp.where(keep, t5, 0.0), axis=3).reshape(16, 64, 64)

    return blocks(gw[:, :, 0:128]), blocks(gw[:, :, 128:256])


def _gate_preacts(ucv, wt_ref):
    ucb = ucv.astype(_MXU)
    ps = [jnp.dot(ucb[:, 128 * c:128 * (c + 1)], wt_ref[c], preferred_element_type=_F32) for c in range(GATE_TILES)]
    pre_r = jnp.concatenate([p[:, 0:128] for p in ps], axis=1)
    pre_i = jnp.concatenate([p[:, 128:256] for p in ps], axis=1)
    return pre_r, pre_i


def _scan_fwd(uc, wt, b_rg, b_ig, lam):
    s = uc.shape[0]
    tb = 256

    def body(uc_ref, wt_ref, brg_ref, big_ref, lam_ref, h_ref, carry, a_s, b_s):
        i = pl.program_id(0)

        @pl.when(i == 0)
        def _():
            carry[...] = jnp.zeros_like(carry)

        ucv = uc_ref[...]
        pre_r, pre_i = _gate_preacts(ucv, wt_ref)
        _, ig, _, a, sq, _ = _lru_gates(pre_r, pre_i, ucv, brg_ref[...], big_ref[...], lam_ref[...])
        av = a
        bv = sq * (ig * ucv)
        av = av.reshape(tb // 8, 8, D)
        bv = bv.reshape(tb // 8, 8, D)
        row8 = lax.broadcasted_iota(jnp.int32, (1, 8, 1), 1)
        for sh in (1, 2, 4):
            m = row8 >= sh
            b_sh = pltpu.roll(bv, sh, 1)
            a_sh = pltpu.roll(av, sh, 1)
            bv = jnp.where(m, av * b_sh + bv, bv)
            av = jnp.where(m, av * a_sh, av)
        a_s[...] = av.reshape(tb, D)
        b_s[...] = bv.reshape(tb, D)

        def tile(t, state):
            rows = pl.ds(pl.multiple_of(t * 8, 8), 8)
            hv = b_s[rows, :] + a_s[rows, :] * state
            h_ref[rows, :] = hv
            return jnp.broadcast_to(hv[7:8, :], (8, D))

        carry[...] = lax.fori_loop(0, tb // 8, tile, jnp.broadcast_to(carry[7:8, :], (8, D)), unroll=4)

    v = _vec((1, D))
    return _rowwise("scan_fwd", body, grid=s // tb,
                    ins=[(uc, _rows(tb, D)), (wt, _vec((GATE_TILES, 128, 256))), (b_rg, v), (b_ig, v), (lam, v)],
                    outs=[(_sds((s, D)), _rows(tb, D))],
                    scratch=[pltpu.VMEM((8, D), _F32), pltpu.VMEM((tb, D), _F32), pltpu.VMEM((tb, D), _F32)])[0]


def _weight_specs(l):
    return [pl.BlockSpec((N_CHIPS, None, ATT_W, 256), lambda i: (0, l, 0, 0)),
            pl.BlockSpec((N_CHIPS, None, 256, D), lambda i: (0, l, 0, 0)),
            pl.BlockSpec((N_CHIPS, None, 256, D), lambda i: (0, l, 0, 0))]


def _tail_fwd(l, o, h_lru, proj, x, gate, g_post, gw):
    s = x.shape[0]
    tb = 512

    def body(o_ref, h_ref, ga_ref, gl_ref, ma_ref, mb_ref, x_ref, gt_ref, gp_ref, wpa_ref, wpb_ref, wo_ref,
             aa_ref, ba_ref, ya_ref, yb_ref, z_ref, out_ref, xn_ref):
        ga = ga_ref[...].astype(_F32)
        aa = (o_ref[...] * (ga * _sigmoid(ga))).astype(_MXU)
        aa_ref[...] = aa
        gl = gl_ref[...].astype(_F32)
        ba = (h_ref[...] * (gl * _sigmoid(gl))).astype(_MXU)
        ba_ref[...] = ba
        ya = jnp.concatenate([jnp.dot(aa, wpa_ref[j], preferred_element_type=_F32) for j in range(N_CHIPS)], axis=1)
        ya_ref[...] = ya.astype(ya_ref.dtype)
        yb = jnp.dot(ba, wpb_ref[...].reshape(D, D), preferred_element_type=_F32)
        yb_ref[...] = yb.astype(yb_ref.dtype)
        z = (_sigmoid(ma_ref[...].astype(_F32)) * ya
             + _sigmoid(mb_ref[...].astype(_F32)) * yb).astype(z_ref.dtype)
        z_ref[...] = z
        ov = jnp.dot(z, wo_ref[...].reshape(D, D), preferred_element_type=_F32)
        out_ref[...] = ov
        rstd = lax.rsqrt(jnp.mean(ov * ov, axis=-1, keepdims=True) + NORM_EPS)
        xn_ref[...] = x_ref[...] + gt_ref[...] * ((ov * rstd) * gp_ref[...])

    v = _vec((1, D))
    r = _rows(tb, D)
    r5 = _rows(tb, ATT_W)
    return _rowwise("tail_fwd", body, grid=s // tb,
                    ins=[(o, r5), (h_lru, r), (proj, _rows(tb, ATT_W, CB_GATT)), (proj, _rows(tb, D, CB_GLRU)),
                         (proj, _rows(tb, D, CB_MA)), (proj, _rows(tb, D, CB_MB)), (x, r), (gate, v), (g_post, v)]
                    + list(zip((gw["w_pa"], gw["w_pb"], gw["w_o"]), _weight_specs(l))),
                    outs=[(_sds((s, ATT_W), _MXU), r5), (_sds((s, D), _MXU), r), (_sds((s, D), _MXU), r),
                          (_sds((s, D), _MXU), r), (_sds((s, D), _MXU), r), (_sds((s, D)), r), (_sds((s, D)), r)])


def _loss_head(y, target):
    s = y.shape[0]
    tb = 512

    def body(y_ref, t_ref, dy_ref, acc_ref):
        i = pl.program_id(0)

        @pl.when(i == 0)
        def _():
            acc_ref[...] = jnp.zeros_like(acc_ref)

        err = y_ref[...] - t_ref[...]
        dy_ref[...] = err * (1.0 / D)
        acc_ref[...] += jnp.sum(err * err, axis=0, keepdims=True)

    return _rowwise("loss_head", body, grid=s // tb,
                    ins=[(y, _rows(tb, D)), (target, _rows(tb, D))],
                    outs=[(_sds((s, D)), _rows(tb, D)), (_sds((1, D)), _vec((1, D)))])


def _zero_first(i, *refs):
    @pl.when(i == 0)
    def _():
        for ref in refs:
            ref[...] = jnp.zeros_like(ref)


def _tail_bwd(l, dx, out, y_a, y_b, proj, o, h_lru, gate, g_post, gw):
    s = dx.shape[0]
    tb = 256

    def body(dx_ref, out_ref, ya_ref, yb_ref, ma_ref, mb_ref, o_ref, ga_ref, h_ref, gl_ref, gt_ref, gp_ref,
             wpa_ref, wpb_ref, wo_ref,
             dout_ref, dya_ref, dyb_ref, rest_ref, do_ref, dh_ref, dgt_ref, dgp_ref):
        i = pl.program_id(0)
        ov = out_ref[...]
        dxv = dx_ref[...]
        rstd = lax.rsqrt(jnp.mean(ov * ov, axis=-1, keepdims=True) + NORM_EPS)
        nv = ov * rstd
        s_dn = jnp.sum(dxv * nv, axis=0, keepdims=True)
        _zero_first(i, dgt_ref, dgp_ref)
        dgt_ref[...] += s_dn * gp_ref[...]
        dgp_ref[...] += s_dn * gt_ref[...]
        dn = dxv * (gt_ref[...] * gp_ref[...])
        d_out = (rstd * (dn - nv * jnp.mean(dn * nv, axis=-1, keepdims=True))).astype(_MXU)
        dout_ref[...] = d_out
        dz = lax.dot_general(d_out, wo_ref[...].reshape(D, D), _NT, preferred_element_type=_F32)
        ga = _sigmoid(ma_ref[...].astype(_F32))
        gb = _sigmoid(mb_ref[...].astype(_F32))
        dya = (dz * ga).astype(_MXU)
        dyb = (dz * gb).astype(_MXU)
        dya_ref[...] = dya
        dyb_ref[...] = dyb
        rest_ref[:, R_MA:R_MB] = (dz * ya_ref[...].astype(_F32) * ga * (1.0 - ga)).astype(rest_ref.dtype)
        rest_ref[:, R_MB:R_END] = (dz * yb_ref[...].astype(_F32) * gb * (1.0 - gb)).astype(rest_ref.dtype)
        daa = lax.dot_general(dya[:, 0:256], wpa_ref[0], _NT, preferred_element_type=_F32)
        for j in range(1, N_CHIPS):
            daa = daa + lax.dot_general(dya[:, j * 256:(j + 1) * 256], wpa_ref[j], _NT, preferred_element_type=_F32)
        dba = lax.dot_general(dyb, wpb_ref[...].reshape(D, D), _NT, preferred_element_type=_F32)
        gav = ga_ref[...].astype(_F32)
        sa = _sigmoid(gav)
        do_ref[...] = daa * (gav * sa)
        rest_ref[:, 0:R_U] = (daa * o_ref[...] * (sa * (1.0 + gav * (1.0 - sa)))).astype(rest_ref.dtype)
        gl = gl_ref[...].astype(_F32)
        sl = _sigmoid(gl)
        dh_ref[...] = dba * (gl * sl)
        rest_ref[:, R_GLRU:R_MA] = (dba * h_ref[...] * (sl * (1.0 + gl * (1.0 - sl)))).astype(rest_ref.dtype)

    v = _vec((1, D))
    r5, r10 = _rows(tb, ATT_W), _rows(tb, D)
    return _rowwise("tail_bwd", body, grid=s // tb,
                    ins=[(dx, r10), (out, r10), (y_a, r10), (y_b, r10), (proj, _rows(tb, D, CB_MA)),
                         (proj, _rows(tb, D, CB_MB)), (o, r5), (proj, _rows(tb, ATT_W, CB_GATT)), (h_lru, r10),
                         (proj, _rows(tb, D, CB_GLRU)), (gate, v), (g_post, v)]
                    + list(zip((gw["w_pa"], gw["w_pb"], gw["w_o"]), _weight_specs(l))),
                    outs=[(_sds((s, D), _MXU), r10), (_sds((s, D), _MXU), r10), (_sds((s, D), _MXU), r10),
                          (_sds((s, R_END), _MXU), _rows(tb, R_END)),
                          (_sds((s, ATT_W)), r5), (_sds((s, D)), r10), (_sds((1, D)), v), (_sds((1, D)), v)])


def _scan_bwd(dh, uc, h_lru, wt, b_rg, b_ig, lam):
    s = uc.shape[0]
    tb = 256
    n = s // tb

    def body(dh_ref, uc_ref, h_ref, hp_ref, wt_ref, brg_ref, big_ref, lam_ref,
             duc_ref, dwt_ref, dbrg_ref, dbig_ref, dlam_ref, carry, c_s, g_s):
        i = pl.program_id(0)

        @pl.when(i == 0)
        def _():
            carry[...] = jnp.zeros_like(carry)
            for acc_ref in (dwt_ref, dbrg_ref, dbig_ref, dlam_ref):
                acc_ref[...] = jnp.zeros_like(acc_ref)

        ucv = uc_ref[...]
        pre_r, pre_i = _gate_preacts(ucv, wt_ref)
        r, ig, sp, a, sq, inv_sq =_lru_gates(pre_r, pre_i, ucv, brg_ref[...], big_ref[...], lam_ref[...])
        row = lax.broadcasted_iota(jnp.int32, (tb, 1), 0)
        cv = jnp.where(row == tb - 1, 1.0, pltpu.roll(a, tb - 1, 0))
        gv = dh_ref[...]
        cv = cv.reshape(tb // 8, 8, D)
        gv = gv.reshape(tb // 8, 8, D)
        row8 = lax.broadcasted_iota(jnp.int32, (1, 8, 1), 1)
        for sh in (1, 2, 4):
            m = row8 < 8 - sh
            g_sh = pltpu.roll(gv, 8 - sh, 1)
            c_sh = pltpu.roll(cv, 8 - sh, 1)
            gv = jnp.where(m, gv + cv * g_sh, gv)
            cv = jnp.where(m, cv * c_sh, cv)
        c_s[...] = cv.reshape(tb, D)
        g_s[...] = gv.reshape(tb, D)

        def tile(k, state):
            rows = pl.ds(pl.multiple_of((tb // 8 - 1 - k) * 8, 8), 8)
            gt = g_s[rows, :] + c_s[rows, :] * state
            g_s[rows, :] = gt
            return jnp.broadcast_to(gt[0:1, :], (8, D))

        lax.fori_loop(0, tb // 8, tile, jnp.broadcast_to(carry[0:1, :], (8, D)), unroll=4)
        gv = g_s[...]
        carry[...] = (a * gv)[0:8]

        halo = jnp.where(i < n - 1, hp_ref[...], 0.0)
        h_prev = _shift_down(h_ref[...], halo, 1, tb)
        d_a = gv * h_prev
        d_sq = gv * (ig * ucv)
        d_i = gv * sq * ucv
        d_la = d_a * a - d_sq * (a * a) * inv_sq
        d_r = d_la * (-LRU_C * sp)
        d_pre_r = d_r * r * (1.0 - r)
        d_pre_i = d_i * ig * (1.0 - ig)
        ucb = ucv.astype(_MXU)
        dpr = d_pre_r.astype(_MXU)
        dpi = d_pre_i.astype(_MXU)
        back = []
        for c in range(GATE_TILES):
            lanes = slice(128 * c, 128 * (c + 1))
            dp = jnp.concatenate([dpr[:, lanes], dpi[:, lanes]], axis=1)
            back.append(lax.dot_general(dp, wt_ref[c], _NT, preferred_element_type=_F32))
            dwt_ref[c] += lax.dot_general(ucb[:, lanes], dp, _TN, preferred_element_type=_F32)
        duc_ref[...] = gv * sq * ig + jnp.concatenate(back, axis=1)
        dbrg_ref[...] += jnp.sum(d_pre_r, axis=0, keepdims=True)
        dbig_ref[...] += jnp.sum(d_pre_i, axis=0, keepdims=True)
        lamv = lam_ref[...]
        dlam_ref[...] += jnp.sum(d_la * (-LRU_C * r), axis=0, keepdims=True) * (-_sigmoid(-lamv))

    v = _vec((1, D))
    rv = _rows(tb, D, 0, n)
    return _rowwise("scan_bwd", body, grid=n,
                    ins=[(dh, rv), (uc, rv), (h_lru, rv), (h_lru, _halo_prev(tb, D, 0, n)),
                         (wt, _vec((GATE_TILES, 128, 256))), (b_rg, v), (b_ig, v), (lam, v)],
                    outs=[(_sds((s, D)), rv), (_sds((GATE_TILES, 128, 256)), _vec((GATE_TILES, 128, 256))),
                          (_sds((1, D)), v), (_sds((1, D)), v), (_sds((1, D)), v)],
                    scratch=[pltpu.VMEM((8, D), _F32), pltpu.VMEM((tb, D), _F32), pltpu.VMEM((tb, D), _F32)])


def _conv_bwd(duc_a, proj, conv_w, rest):
    s = duc_a.shape[0]
    tb = 512
    n = s // tb
    hw = D // 2

    def body(da_ref, dan_ref, u_ref, up_ref, w_ref, rest_in, du_ref, dw_ref, dbias_ref):
        i = pl.program_id(1)
        duc = da_ref[...]
        nxt = jnp.where(i < n - 1, dan_ref[...], 0.0)
        u = u_ref[...].astype(_F32)
        halo = jnp.where(i > 0, up_ref[...].astype(_F32)[8:16], 0.0)
        du = duc * w_ref[0:1, :]
        dws = [jnp.sum(duc * u, axis=0, keepdims=True)]
        for j in range(1, 4):
            du = du + _shift_up(duc, nxt, j, tb) * w_ref[j:j + 1, :]
            dws.append(jnp.sum(duc * _shift_down(u, halo, j, tb), axis=0, keepdims=True))
        du_ref[...] = du.astype(du_ref.dtype)
        _zero_first(i, dw_ref, dbias_ref)
        for j in range(4):
            dw_ref[j:j + 1, :] += dws[j]
        dbias_ref[...] += jnp.sum(duc, axis=0, keepdims=True)

    r = pl.BlockSpec((tb, hw), lambda h, i: (i, h))
    nxt_spec = pl.BlockSpec((8, hw), lambda h, i: (jnp.minimum((i + 1) * (tb // 8), n * (tb // 8) - 1), h))
    return pl.pallas_call(
        body, name="conv_bwd", grid=(2, n),
        in_specs=[r, nxt_spec,
                  pl.BlockSpec((tb, hw), lambda h, i: (i, 2 * CB_U + h)),
                  pl.BlockSpec((16, hw), lambda h, i: (jnp.maximum(i * (tb // 16) - 1, 0), 2 * CB_U + h)),
                  pl.BlockSpec((4, hw), lambda h, i: (0, h)), pl.BlockSpec(memory_space=pl.ANY)],
        out_specs=[pl.BlockSpec((tb, hw), lambda h, i: (i, R_U // hw + h)),
                   pl.BlockSpec((4, hw), lambda h, i: (0, h)), pl.BlockSpec((1, hw), lambda h, i: (0, h))],
        out_shape=[_sds(rest.shape, rest.dtype), _sds((4, D)), _sds((1, D))],
        input_output_aliases={5: 0}, compiler_params=_params(2),
    )(duc_a, duc_a, proj, proj, conv_w, rest)


def _prenorm_bwd(dh, x, dx_out, g_pre, scale):
    s = x.shape[0]
    tb = 512

    def body(dh_ref, x_ref, dxo_ref, g_ref, sc_ref, dx_ref, dsh_ref, dsc_ref, dg_ref):
        i = pl.program_id(0)
        xv = x_ref[...]
        dhv = dh_ref[...]
        rstd = lax.rsqrt(jnp.mean(xv * xv, axis=-1, keepdims=True) + NORM_EPS)
        xn = xv * rstd
        one_sc = 1.0 + sc_ref[...]
        s1 = jnp.sum(dhv * xn, axis=0, keepdims=True)
        _zero_first(i, dsh_ref, dsc_ref, dg_ref)
        dsh_ref[...] += jnp.sum(dhv, axis=0, keepdims=True)
        dsc_ref[...] += s1 * g_ref[...]
        dg_ref[...] += s1 * one_sc
        dxn = dhv * (g_ref[...] * one_sc)
        dx_ref[...] = dxo_ref[...] + rstd * (dxn - xn * jnp.mean(dxn * xn, axis=-1, keepdims=True))

    v = _vec((1, D))
    r = _rows(tb, D)
    return _rowwise("prenorm_bwd", body, grid=s // tb,
                    ins=[(dh, r), (x, r), (dx_out, r), (g_pre, v), (scale, v)],
                    outs=[(_sds((s, D)), r), (_sds((1, D)), v), (_sds((1, D)), v), (_sds((1, D)), v)])


def _band_tiles(dil):
    tiles = []
    for rho in range(dil):
        for b in range(16 // dil):
            qs = rho + dil * BAND * b
            tiles.append((qs, QBLK + qs - dil * BAND, b))
    return tiles


def _strided(start, size, dil):
    return pl.ds(start, size, stride=dil) if dil > 1 else pl.ds(start, size)


def _band_mask(i, b):
    qi = lax.broadcasted_iota(jnp.int32, (BAND, 2 * BAND), 0)
    ki = lax.broadcasted_iota(jnp.int32, (BAND, 2 * BAND), 1)
    valid = (ki >= qi) & (ki <= qi + BAND)
    if b == 0:
        valid = valid & ((ki >= BAND) | (i > 0))
    return valid


def _attn_fwd(proj):
    s = proj.shape[0]
    n = s // QBLK
    scale = HEAD ** -0.5

    def body(*refs):
        q_refs, kp_refs, kc_refs, vp_refs, vc_refs = (refs[3 * t:3 * t + 3] for t in range(5))
        o_ref, lse_ref, qbuf, kbuf, vbuf = refs[15:20]
        accs, maxs, dens = refs[20:23], refs[23:26], refs[26:29]
        i = pl.program_id(1)
        for g, dil in enumerate(DILATIONS):
            qbuf[...] = q_refs[g][...].astype(_F32)
            kbuf[0:QBLK, :] = kp_refs[g][...].astype(_F32)
            kbuf[QBLK:2 * QBLK, :] = kc_refs[g][...].astype(_F32)
            vbuf[0:QBLK, :] = vp_refs[g][...].astype(_F32)
            vbuf[QBLK:2 * QBLK, :] = vc_refs[g][...].astype(_F32)
            for qs, ks, b in _band_tiles(dil):
                qsl = _strided(qs, BAND, dil)
                q = qbuf[qsl, :].astype(_MXU)
                kk = kbuf[_strided(ks, 2 * BAND, dil), :].astype(_MXU)
                vv = vbuf[_strided(ks, 2 * BAND, dil), :].astype(_MXU)
                sc = lax.dot_general(q, kk, _NT, preferred_element_type=_F32) * scale
                sc = jnp.where(_band_mask(i, b), sc, NEG_INF)
                m = jnp.max(sc, axis=-1, keepdims=True)
                p = jnp.exp(sc - m)
                accs[g][qsl, :] = jnp.dot(p.astype(_MXU), vv, preferred_element_type=_F32)
                maxs[g][qsl, :] = jnp.broadcast_to(m, (BAND, HEAD))
                dens[g][qsl, :] = jnp.broadcast_to(jnp.sum(p, axis=-1, keepdims=True), (BAND, HEAD))
        ms = [r[...] for r in maxs]
        mx = jnp.maximum(jnp.maximum(ms[0], ms[1]), ms[2])
        ws = [jnp.exp(m - mx) for m in ms]
        den = ws[0] * dens[0][...] + ws[1] * dens[1][...] + ws[2] * dens[2][...]
        o_ref[...] = (ws[0] * accs[0][...] + ws[1] * accs[1][...] + ws[2] * accs[2][...]) / den
        lse_ref[...] = mx + jnp.log(den)

    blk = (QBLK, HEAD)

    def spec(first_col, lag):
        specs = []
        for g in range(3):
            col = first_col + g * HEADS
            if lag:
                specs.append(pl.BlockSpec(blk, lambda j, i, col=col: (jnp.maximum(i - 1, 0), col + j)))
            else:
                specs.append(pl.BlockSpec(blk, lambda j, i, col=col: (i, col + j)))
        return specs

    out_spec = pl.BlockSpec(blk, lambda j, i: (i, j))
    return pl.pallas_call(
        body, name="attn_fwd", grid=(HEADS, n),
        in_specs=spec(0, False) + spec(12, True) + spec(12, False) + spec(24, True) + spec(24, False),
        out_specs=[out_spec] * 2, out_shape=[_sds((s, ATT_W))] * 2,
        scratch_shapes=[pltpu.VMEM(blk, _F32)] + [pltpu.VMEM((2 * QBLK, HEAD), _F32)] * 2
        + [pltpu.VMEM(blk, _F32)] * 9,
        compiler_params=_params(2))(*([proj] * 15))


def _attn_bwd(proj, d_o, o, lse, g, into):
    s = proj.shape[0]
    dil = DILATIONS[g]
    n = s // QBLK
    scale = HEAD ** -0.5
    tiles = _band_tiles(dil)

    def body(*refs):
        q_ref, kp_ref, kc_ref, vp_ref, vc_ref, do_ref, o_ref, lse_ref = refs[0:8]
        dq_ref, dk_ref, dv_ref, kbuf, vbuf, dkbuf, dvbuf, dqbuf, qbuf = refs[-9:]
        i = pl.program_id(1)

        @pl.when(i == 0)
        def _():
            dkbuf[0:QBLK, :] = jnp.zeros((QBLK, HEAD), _F32)
            dvbuf[0:QBLK, :] = jnp.zeros((QBLK, HEAD), _F32)

        @pl.when(i < n)
        def _():
            qbuf[...] = q_ref[...].astype(_F32)
            kbuf[0:QBLK, :] = kp_ref[...].astype(_F32)
            kbuf[QBLK:2 * QBLK, :] = kc_ref[...].astype(_F32)
            vbuf[0:QBLK, :] = vp_ref[...].astype(_F32)
            vbuf[QBLK:2 * QBLK, :] = vc_ref[...].astype(_F32)
            dkbuf[QBLK:2 * QBLK, :] = jnp.zeros((QBLK, HEAD), _F32)
            dvbuf[QBLK:2 * QBLK, :] = jnp.zeros((QBLK, HEAD), _F32)
            for qs, ks, b in tiles:
                qsl = _strided(qs, BAND, dil)
                ksl = _strided(ks, 2 * BAND, dil)
                q = qbuf[qsl, :].astype(_MXU)
                kk = kbuf[ksl, :].astype(_MXU)
                vv = vbuf[ksl, :].astype(_MXU)
                dov = do_ref[qsl, :]
                dd = jnp.sum(dov * o_ref[qsl, :], axis=-1, keepdims=True)
                lse_t = lse_ref[qsl, :][:, 0:1]
                sc = lax.dot_general(q, kk, _NT, preferred_element_type=_F32) * scale
                p = jnp.where(_band_mask(i, b), jnp.exp(sc - lse_t), 0.0)
                dob = dov.astype(_MXU)
                dp = lax.dot_general(dob, vv, _NT, preferred_element_type=_F32)
                ds = (p * (dp - dd) * scale).astype(_MXU)
                dqbuf[qsl, :] = jnp.dot(ds, kk, preferred_element_type=_F32)
                dkbuf[ksl, :] += lax.dot_general(ds, q, _TN, preferred_element_type=_F32)
                dvbuf[ksl, :] += lax.dot_general(p.astype(_MXU), dob, _TN, preferred_element_type=_F32)
            dq_ref[...] = dqbuf[...].astype(dq_ref.dtype)

        dk_ref[...] = dkbuf[0:QBLK, :].astype(dk_ref.dtype)
        dv_ref[...] = dvbuf[0:QBLK, :].astype(dv_ref.dtype)
        dkbuf[0:QBLK, :] = dkbuf[QBLK:2 * QBLK, :]
        dvbuf[0:QBLK, :] = dvbuf[QBLK:2 * QBLK, :]

    blk = (QBLK, HEAD)
    cq, ck, cv = g * HEADS, 12 + g * HEADS, 24 + g * HEADS

    def cur(i):
        return jnp.minimum(i, n - 1)

    def prev(i):
        return jnp.maximum(jnp.minimum(i, n - 1) - 1, 0)

    own = pl.BlockSpec(blk, lambda j, i: (cur(i), j))
    own_out = pl.BlockSpec(blk, lambda j, i: (cur(i), cq + j))
    late_out = pl.BlockSpec(blk, lambda j, i: (jnp.maximum(i - 1, 0), cq + j))
    extra = [] if into is None else list(into)
    return pl.pallas_call(
        body, name="attn_bwd_d%d" % dil, grid=(HEADS, n + 1),
        in_specs=[pl.BlockSpec(blk, lambda j, i: (cur(i), cq + j)),
                  pl.BlockSpec(blk, lambda j, i: (prev(i), ck + j)),
                  pl.BlockSpec(blk, lambda j, i: (cur(i), ck + j)),
                  pl.BlockSpec(blk, lambda j, i: (prev(i), cv + j)),
                  pl.BlockSpec(blk, lambda j, i: (cur(i), cv + j)),
                  own, own, own] + [pl.BlockSpec(memory_space=pl.ANY)] * len(extra),
        out_specs=[own_out, late_out, late_out], out_shape=[_sds((s, QKV_W), _MXU)] * 3,
        input_output_aliases={8 + t: t for t in range(len(extra))},
        scratch_shapes=[pltpu.VMEM((2 * QBLK, HEAD), _F32)] * 4 + [pltpu.VMEM((QBLK, HEAD), _F32)] * 2,
        compiler_params=_params(2))(proj, proj, proj, proj, proj, d_o, o, lse, *extra)


_PARTS = ((0, 2), (2, 2), (4, 2), (6, 6))
_CHUNK = 768


def _d_h(parts, w_in):
    s = parts[0].shape[0]
    nk = IN_W // _CHUNK

    def body(p0, p1, p2, p3, w_ref, o_ref, acc):
        k = pl.program_id(2)

        @pl.when(k == 0)
        def _():
            acc[...] = jnp.zeros_like(acc)

        for p_ref, (first, cnt) in zip((p0, p1, p2, p3), _PARTS):
            @pl.when((k >= first) & (k < first + cnt))
            def _(p_ref=p_ref):
                acc[...] += lax.dot_general(p_ref[...].astype(_MXU), w_ref[...], _NT, preferred_element_type=_F32)

        @pl.when(k == nk - 1)
        def _():
            o_ref[...] = acc[...]

    def part_spec(first, cnt):
        return pl.BlockSpec((1024, _CHUNK), lambda m, n, k: (m, jnp.clip(k - first, 0, cnt - 1)))

    return pl.pallas_call(
        body, name="d_h", grid=(s // 1024, 1, nk),
        in_specs=[part_spec(*p) for p in _PARTS]
        + [pl.BlockSpec((None, D, _CHUNK), lambda m, n, k: (k // 3, 0, k % 3))],
        out_specs=pl.BlockSpec((1024, D), lambda m, n, k: (m, 0)), out_shape=_sds((s, D)),
        scratch_shapes=[pltpu.VMEM((1024, D), _F32)], compiler_params=_params(3))(*parts, w_in)


def _g_w_in(l, h_t, parts, into):
    s = h_t.shape[1]
    nk = s // 1024

    def body(*refs):
        h_ref, p_refs = refs[0], refs[1:5]
        o_ref, acc = refs[-2], refs[-1]
        n = pl.program_id(1)
        k = pl.program_id(2)

        @pl.when(k == 0)
        def _():
            acc[...] = jnp.zeros_like(acc)

        for p_ref, (first, cnt) in zip(p_refs, _PARTS):
            @pl.when((n >= first) & (n < first + cnt))
            def _(p_ref=p_ref):
                acc[...] += jnp.dot(h_ref[...], p_ref[...].astype(_MXU), preferred_element_type=_F32)

        @pl.when(k == nk - 1)
        def _():
            o_ref[...] = acc[...]

    def part_spec(first, cnt):
        def index(m, n, k):
            row = jnp.where(n < first, 0, jnp.where(n >= first + cnt, nk - 1, k))
            return (row, jnp.clip(n - first, 0, cnt - 1))
        return pl.BlockSpec((1024, _CHUNK), index)

    extra = [] if into is None else [into]
    return pl.pallas_call(
        body, name="g_w_in", grid=(1, IN_W // _CHUNK, nk),
        in_specs=[pl.BlockSpec((D, 1024), lambda m, n, k: (0, k))] + [part_spec(*p) for p in _PARTS]
        + [pl.BlockSpec(memory_space=pl.ANY)] * len(extra),
        out_specs=pl.BlockSpec((None, None, D, _CHUNK), lambda m, n, k: (l, n // 3, 0, n % 3)),
        out_shape=_sds((2, N_CHIPS, D, 2304)), input_output_aliases={5: 0} if extra else {},
        scratch_shapes=[pltpu.VMEM((D, _CHUNK), _F32)], compiler_params=_params(3))(h_t, *parts, *extra)


def _layer_fwd(l, x, p, gw, late):
    s = x.shape[0]
    nm = s // 1024
    h, h_t = _prenorm_fwd(x, p["g_pre"], p["shift"], p["scale"])
    proj = _mm("proj", h, gw["w_in"][l], _sds((s, IN_W), _MXU), grid=(nm, N_CHIPS, 1),
               a_spec=pl.BlockSpec((1024, D), lambda m, n, k: (m, 0)),
               b_spec=pl.BlockSpec((None, D, 2304), lambda m, n, k: (n, 0, 0)),
               o_spec=pl.BlockSpec((1024, 2304), lambda m, n, k: (m, n)), dims=_NN, acc_shape=(1024, 2304))
    o, lse = _attn_fwd(proj)
    uc = _conv_fwd(proj, p["conv_w"], p["conv_b"])
    h_lru = _scan_fwd(uc, p["wt"], p["b_rg"], p["b_ig"], p["lam"])
    if late is not None:
        landed = dict(late(h_lru))
        gw["w_in"].append(landed.pop("w_in1"))
        gw.update(landed)
    a_att, b_act, y_a, y_b, z, out, x_new = _tail_fwd(l, o, h_lru, proj, x, p["gate"], p["g_post"], gw)
    saved = dict(x=x, h_t=h_t, proj=proj, o=o, lse=lse, uc=uc, h_lru=h_lru, a_att=a_att, b_act=b_act,
                 y_a=y_a, y_b=y_b, z=z, out=out)
    return x_new, saved


def _layer_bwd(l, dx, p, gw, sv, big):
    s = dx.shape[0]
    nm = s // 1024
    nt = s // 2048
    proj = sv["proj"]
    d_out, dy_a, dy_b, d_rest, d_o, dh_lru, d_gate, d_gpost = _tail_bwd(
        l, dx, sv["out"], sv["y_a"], sv["y_b"], proj, sv["o"], sv["h_lru"], p["gate"], p["g_post"], gw)

    def wgrad_rows(name, a, b, into):
        return _mm(name, a, b, _sds((2, N_CHIPS, 256, D)), grid=(4, 1, nt),
                   a_spec=pl.BlockSpec((2048, 256), lambda m, n, k: (k, m)),
                   b_spec=pl.BlockSpec((2048, D), lambda m, n, k: (k, 0)),
                   o_spec=pl.BlockSpec((None, None, 256, D), lambda m, n, k: (l, m, 0, 0)),
                   dims=_TN, acc_shape=(256, D), into=into)

    big = dict(big)
    big["w_o"] = wgrad_rows("g_w_o", sv["z"], d_out, big.get("w_o"))
    big["w_pa"] = _mm("g_w_pa", sv["a_att"], dy_a, _sds((2, N_CHIPS, ATT_W, 256)), grid=(1, 4, nt),
                      a_spec=pl.BlockSpec((2048, ATT_W), lambda m, n, k: (k, 0)),
                      b_spec=pl.BlockSpec((2048, 256), lambda m, n, k: (k, n)),
                      o_spec=pl.BlockSpec((None, None, ATT_W, 256), lambda m, n, k: (l, n, 0, 0)),
                      dims=_TN, acc_shape=(ATT_W, 256), into=big.get("w_pa"))
    big["w_pb"] = wgrad_rows("g_w_pb", sv["b_act"], dy_b, big.get("w_pb"))
    duc, g_wt, d_brg, d_big, d_lam = _scan_bwd(dh_lru, sv["uc"], sv["h_lru"], p["wt"],
                                               p["b_rg"], p["b_ig"], p["lam"])
    g_wrg, g_wig = _gate_tile_grads(g_wt)
    d_rest, g_convw, g_convb = _conv_bwd(duc, proj, p["conv_w"], d_rest)
    dqkv = None
    for g in range(3):
        dqkv = _attn_bwd(proj, d_o, sv["o"], sv["lse"], g, dqkv)
    parts = (dqkv[0], dqkv[1], dqkv[2], d_rest)
    dh = _d_h(parts, gw["w_in"][l])
    big["w_in"] = _g_w_in(l, sv["h_t"], parts, big.get("w_in"))
    dx_in, d_shift, d_scale, d_gpre = _prenorm_bwd(dh, sv["x"], dx, p["g_pre"], p["scale"])
    small = dict(dmod=jnp.concatenate([d_shift, d_scale, d_gate], axis=1), g_pre=d_gpre, conv_w=g_convw,
                 conv_b=g_convb, w_rg=g_wrg, b_rg=d_brg, w_ig=g_wig, b_ig=d_big, lam=d_lam, g_post=d_gpost)
    return dx_in, small, big


def _local_step(x, target, small_p, w_in0, late):
    saved = []
    h = x
    gw = dict(w_in=[w_in0])
    for l in range(2):
        h, sv = _layer_fwd(l, h, small_p[l], gw, late if l == 0 else None)
        saved.append(sv)
    dy, sq = _loss_head(h, target)
    loss = 0.5 * jnp.sum(sq) / D
    big = {}
    smalls = [None, None]
    dx = dy
    for l in (1, 0):
        dx, smalls[l], big = _layer_bwd(l, dx, small_p[l], gw, saved[l], big)
    return loss, dx, smalls, big


_SMALL_ROWS = 8 + 16 + 8 + 128 + 128


def _pack_small(smalls):
    dmod = jnp.concatenate([smalls[0]["dmod"].reshape(3, D), smalls[1]["dmod"].reshape(3, D),
                            jnp.zeros((2, D), _F32)], axis=0)
    vecs = jnp.concatenate([smalls[l][k] for k in ("g_pre", "conv_b", "b_rg", "b_ig", "lam", "g_post")
                            for l in range(2)] + [jnp.zeros((4, D), _F32)], axis=0)
    convw = jnp.concatenate([smalls[0]["conv_w"], smalls[1]["conv_w"]], axis=0)
    wrg = jnp.stack([smalls[0]["w_rg"], smalls[1]["w_rg"]]).reshape(128, D)
    wig = jnp.stack([smalls[0]["w_ig"], smalls[1]["w_ig"]]).reshape(128, D)
    return jnp.concatenate([dmod, vecs, convw, wrg, wig], axis=0)


def kernel(x, c, w_mod, b_mod, g_pre, w_in, conv_w, conv_b, w_rg, b_rg, w_ig, b_ig, lru_lambda, w_pa, w_pb, w_o, g_post, loss_target, m_w_mod, m_b_mod, m_g_pre, m_w_in, m_conv_w, m_conv_b, m_w_rg, m_b_rg, m_w_ig, m_b_ig, m_lru_lambda, m_w_pa, m_w_pb, m_w_o, m_g_post, v_w_mod, v_b_mod, v_g_pre, v_w_in, v_conv_w, v_conv_b, v_w_rg, v_b_rg, v_w_ig, v_b_ig, v_lru_lambda, v_w_pa, v_w_pb, v_w_o, v_g_post):
    xi, yi, ci = lax.axis_index("x"), lax.axis_index("y"), lax.axis_index("c")
    chip = 2 * xi + yi
    dev = 4 * xi + 2 * yi + ci
    mcols = w_mod.shape[2]

    pack1 = jnp.concatenate([jnp.broadcast_to(c, (8, D)),
                             jnp.pad(conv_w.reshape(8, 256), ((0, 0), (0, D - 256)))], axis=0)
    g1 = _exchange("gather_cond", [pack1], "xyc", False)[0]
    c_all = g1[:, 0, :]
    conv_w_full = jnp.transpose(g1[0::2, 8:16, 0:256], (1, 0, 2)).reshape(2, 4, D)

    b_cols = lax.dynamic_slice(b_mod, (0, chip * mcols), (2, mcols)).reshape(2, 1, mcols)
    mod_loc = _mod_fwd(c_all, w_mod, b_cols)
    g2 = _exchange("gather_mod", [mod_loc.reshape(16, mcols)], "xyc", False)[0]
    mod_full = jnp.transpose(g2[0::2], (1, 0, 2)).reshape(2, 8, 3 * D)
    mod_me = lax.dynamic_index_in_dim(mod_full, dev, axis=1, keepdims=False)

    wb_in = _cast("cast_w_in", w_in.reshape(2 * D, 2304), 256).reshape(2, D, 2304)
    late_src = [wb_in[1], _cast("cast_w_pa", w_pa.reshape(2 * ATT_W, 256), 256).reshape(2, ATT_W, 256),
                _cast("cast_w_pb", w_pb.reshape(512, D), 256).reshape(2, 256, D),
                _cast("cast_w_o", w_o.reshape(512, D), 256).reshape(2, 256, D)]
    late_chunks = [4, 2, 2, 2]
    w_in0 = _gather_weights([wb_in[0].reshape(2, D // 2, 2304)], [2])[0].reshape(N_CHIPS, D, 2304)
    lands = _own_slot(late_src, late_chunks)
    send_sems, recv_sems, src_thru, land_thru, token = _late_gather_start(late_src, lands, late_chunks, w_in0)

    def late(after):
        got = _late_gather_wait(send_sems, recv_sems, src_thru, land_thru, late_chunks, after)
        return dict(w_in1=got[0], w_pa=got[1], w_pb=got[2], w_o=got[3])

    small_p = []
    for l in range(2):
        gates = _gate_tiles(w_rg[l], w_ig[l]).astype(_MXU)
        small_p.append(dict(
            shift=mod_me[l:l + 1, 0:D], scale=mod_me[l:l + 1, D:2 * D], gate=mod_me[l:l + 1, 2 * D:3 * D],
            g_pre=g_pre[l:l + 1], conv_w=conv_w_full[l], conv_b=conv_b[l:l + 1], wt=gates,
            b_rg=b_rg[l:l + 1], b_ig=b_ig[l:l + 1], lam=lru_lambda[l:l + 1], g_post=g_post[l:l + 1]))

    small_p[0]["shift"] = small_p[0]["shift"] + token[0, 0]

    loss_loc, dx, smalls, big = _local_step(x[0], loss_target[0], small_p, w_in0, late)
    loss = lax.psum(loss_loc, ("x", "y", "c"))
    grad_x = dx[None]

    names = ("w_in", "w_pa", "w_pb", "w_o")
    core = jnp.reshape(ci, (1,)).astype(jnp.int32)
    where = jnp.stack([chip, ci]).astype(jnp.int32)
    pair = list(_exchange("reduce_pair", [big["w_in"].reshape(2, 16, 256, 2304)] + [big[k] for k in names[1:]],
                          "c", True, local=False, nchunks=[16, 4, 4, 4]))
    pair[0] = pair[0].reshape(N_CHIPS, D, 2304)
    t1 = [_sum_pair("sum_pair_" + k, big[k], r, core, 128) for k, r in zip(names, pair)]
    quad = _exchange("reduce_chips", [t[1] for t in t1], "xy", True, local=False, nchunks=[4, 1, 1, 1])
    t3 = [_sum_chips("sum_chips_" + k, t[0], r, where, 128) for k, t, r in zip(names, t1, quad)]
    both = _pair_fill("gather_layers", t3, [4, 1, 1, 1])
    g_big = dict(zip(names, both))

    g3 = _exchange("gather_small", [_pack_small(smalls)], "xyc", False)[0]
    tot = _sum_lead("sum_small", g3, 96)
    dmod_all = g3[:, 0:6, :].reshape(8, 2, 3 * D)
    dm_cols = jnp.transpose(lax.dynamic_slice(dmod_all, (0, 0, chip * mcols), (8, 2, mcols)), (1, 0, 2))
    g_w_mod = _mod_bwd(jnp.transpose(c_all), dm_cols)
    vec = tot[8:20].reshape(6, 2, D)
    g_conv_w_full = tot[24:32].reshape(2, 4, D)
    grads = dict(
        w_mod=g_w_mod, b_mod=tot[0:6].reshape(2, 3 * D), g_pre=vec[0], w_in=g_big["w_in"],
        conv_w=lax.dynamic_slice(g_conv_w_full, (0, 0, chip * 256), (2, 4, 256)), conv_b=vec[1],
        w_rg=tot[32:160].reshape(2, 16, 64, 64), b_rg=vec[2], w_ig=tot[160:288].reshape(2, 16, 64, 64),
        b_ig=vec[3], lru_lambda=vec[4], w_pa=g_big["w_pa"], w_pb=g_big["w_pb"], w_o=g_big["w_o"],
        g_post=vec[5])

    weights = dict(w_mod=w_mod, b_mod=b_mod, g_pre=g_pre, w_in=w_in, conv_w=conv_w, conv_b=conv_b, w_rg=w_rg,
                   b_rg=b_rg, w_ig=w_ig, b_ig=b_ig, lru_lambda=lru_lambda, w_pa=w_pa, w_pb=w_pb, w_o=w_o,
                   g_post=g_post)
    ms = dict(w_mod=m_w_mod, b_mod=m_b_mod, g_pre=m_g_pre, w_in=m_w_in, conv_w=m_conv_w, conv_b=m_conv_b,
              w_rg=m_w_rg, b_rg=m_b_rg, w_ig=m_w_ig, b_ig=m_b_ig, lru_lambda=m_lru_lambda, w_pa=m_w_pa,
              w_pb=m_w_pb, w_o=m_w_o, g_post=m_g_post)
    vs = dict(w_mod=v_w_mod, b_mod=v_b_mod, g_pre=v_g_pre, w_in=v_w_in, conv_w=v_conv_w, conv_b=v_conv_b,
              w_rg=v_w_rg, b_rg=v_b_rg, w_ig=v_w_ig, b_ig=v_b_ig, lru_lambda=v_lru_lambda, w_pa=v_w_pa,
              w_pb=v_w_pb, w_o=v_w_o, g_post=v_g_post)
    flat = dict(w_mod=(2 * D, mcols, 256), b_mod=(2, 3 * D, 2), g_pre=(2, D, 2), w_in=(2 * D, 2304, 256),
                conv_w=(8, 256, 8), conv_b=(2, D, 2), w_rg=(128, D, 128), b_rg=(2, D, 2), w_ig=(128, D, 128),
                b_ig=(2, D, 2), lru_lambda=(2, D, 2), w_pa=(2 * ATT_W, 256, 256), w_pb=(512, D, 256),
                w_o=(512, D, 256), g_post=(2, D, 2))
    order = ("w_mod", "b_mod", "g_pre", "w_in", "conv_w", "conv_b", "w_rg", "b_rg", "w_ig", "b_ig",
             "lru_lambda", "w_pa", "w_pb", "w_o", "g_post")
    deltas, new_m, new_v = [], [], []
    for k in order:
        rows, cols, tb = flat[k]
        shp = weights[k].shape
        d, nm_, nv_ = _adamw("adamw_" + k, weights[k].reshape(rows, cols), grads[k].reshape(rows, cols),
                             ms[k].reshape(rows, cols), vs[k].reshape(rows, cols), tb)
        deltas.append(d.reshape(shp))
        new_m.append(nm_.reshape(shp))
        new_v.append(nv_.reshape(shp))
    return (loss, grad_x, *[grads[k].reshape(weights[k].shape) for k in order], *deltas, *new_m, *new_v)
```

```python
import functools

import jax
import jax.numpy as jnp
from jax import lax
from jax.experimental import pallas as pl
from jax.experimental.pallas import tpu as pltpu

_F32 = jnp.float32
_MXU = jnp.bfloat16
_VMEM_LIMIT = 56 * 1024 * 1024
_MESH = pl.DeviceIdType.MESH

D = 1024
HEAD = 128
HEADS = 4
ATT_W = 512
QKV_W = 1536
IN_W = 9216
DILATIONS = (1, 4, 16)
BAND = 128
QBLK = BAND * 16
NORM_EPS = 1e-6
NEG_INF = -1e30
LRU_C = 8.0
N_CHIPS = 4
CB_GATT = 4608 // 512
CB_U, CB_GLRU, CB_MA, CB_MB = 5, 6, 7, 8
R_U, R_GLRU, R_MA, R_MB, R_END = 512, 1536, 2560, 3584, 4608

ADAM_LR, ADAM_B1, ADAM_B2, ADAM_EPS, ADAM_WD, ADAM_STEP = 0.001, 0.9, 0.999, 1e-08, 0.01, 10


def _params(ngrid):
    return pltpu.CompilerParams(dimension_semantics=("arbitrary",) * ngrid, vmem_limit_bytes=_VMEM_LIMIT)


def _sigmoid(v):
    return 0.5 * jnp.tanh(0.5 * v) + 0.5


_GROUPS = {
    "c": [(0, 0, 1)],
    "xy": [(1, 0, 0), (0, 1, 0), (1, 1, 0)],
    "xyc": [(0, 0, 1), (0, 1, 0), (0, 1, 1), (1, 0, 0), (1, 0, 1), (1, 1, 0), (1, 1, 1)],
}


def _rank(group, px, py, pc):
    if group == "c":
        return pc
    if group == "xy":
        return 2 * px + py
    return 4 * px + 2 * py + pc


def _flip(rel, x, y, c):
    dx, dy, dc = rel
    return (1 - x if dx else x, 1 - y if dy else y, 1 - c if dc else c)


def _pieces(ref, nchunk):
    step = ref.shape[0] // nchunk
    return [ref.at[pl.ds(q * step, step)] for q in range(nchunk)]


def _exchange(name, srcs, group, scatter, *, local=True, nchunks=None):
    rels = _GROUPS[group]
    gsize = len(rels) + 1
    n = len(srcs)
    nchunks = nchunks or [1] * n
    blks = [s.shape[1:] if scatter else s.shape for s in srcs]
    slotted = local or gsize > 2
    base = [sum(nchunks[:a]) for a in range(n)]
    tot = sum(nchunks)

    def body(*refs):
        src_refs, out_refs = refs[:n], refs[n:2 * n]
        send_sems, recv_sems, loc_sems = refs[2 * n:]
        x, y, c = lax.axis_index("x"), lax.axis_index("y"), lax.axis_index("c")
        me = _rank(group, x, y, c)
        copies = []
        for a in range(n):
            def part(r, a=a):
                return src_refs[a].at[r] if scatter else src_refs[a]
            dst = out_refs[a].at[me] if slotted else out_refs[a]
            if local:
                for q, (s_, d_) in enumerate(zip(_pieces(part(me), nchunks[a]), _pieces(dst, nchunks[a]))):
                    loc = pltpu.make_async_copy(s_, d_, loc_sems.at[base[a] + q])
                    loc.start()
                    copies.append(loc)
            for k, rel in enumerate(rels):
                peer = _flip(rel, x, y, c)
                for q, (s_, d_) in enumerate(zip(_pieces(part(_rank(group, *peer)), nchunks[a]),
                                                 _pieces(dst, nchunks[a]))):
                    cp = pltpu.make_async_remote_copy(
                        src_ref=s_, dst_ref=d_, send_sem=send_sems.at[(base[a] + q) * len(rels) + k],
                        recv_sem=recv_sems.at[(base[a] + q) * len(rels) + k],
                        device_id=peer, device_id_type=_MESH)
                    cp.start()
                    copies.append(cp)
        for cp in copies:
            cp.wait()

    any_spec = pl.BlockSpec(memory_space=pl.ANY)
    lead = (gsize,) if slotted else ()
    return pl.pallas_call(
        body, name=name,
        out_shape=[jax.ShapeDtypeStruct(lead + tuple(b), s.dtype) for b, s in zip(blks, srcs)],
        in_specs=[any_spec] * n, out_specs=[any_spec] * n,
        scratch_shapes=[pltpu.SemaphoreType.DMA((tot * len(rels),)), pltpu.SemaphoreType.DMA((tot * len(rels),)),
                        pltpu.SemaphoreType.DMA((tot,))],
    )(*srcs)


def _pair_fill(name, arrs, nchunks):
    n = len(arrs)
    base = [sum(nchunks[:a]) for a in range(n)]
    tot = sum(nchunks)

    def body(*refs):
        out_refs = refs[n:2 * n]
        send_sems, recv_sems = refs[2 * n:]
        x, y, c = lax.axis_index("x"), lax.axis_index("y"), lax.axis_index("c")
        copies = []
        for a in range(n):
            for q, blk in enumerate(_pieces(out_refs[a].at[c], nchunks[a])):
                cp = pltpu.make_async_remote_copy(
                    src_ref=blk, dst_ref=blk, send_sem=send_sems.at[base[a] + q], recv_sem=recv_sems.at[base[a] + q],
                    device_id=(x, y, 1 - c), device_id_type=_MESH)
                cp.start()
                copies.append(cp)
        for cp in copies:
            cp.wait()

    any_spec = pl.BlockSpec(memory_space=pl.ANY)
    return pl.pallas_call(
        body, name=name, out_shape=[jax.ShapeDtypeStruct(a.shape, a.dtype) for a in arrs],
        in_specs=[any_spec] * n, out_specs=[any_spec] * n, input_output_aliases={a: a for a in range(n)},
        scratch_shapes=[pltpu.SemaphoreType.DMA((tot,)), pltpu.SemaphoreType.DMA((tot,))],
    )(*arrs)


def _gather_weights(wb, nchunks):
    n = len(wb)
    rels = _GROUPS["xy"]
    base = [sum(nchunks[:a]) for a in range(n)]
    tot = sum(nchunks)

    def body(*refs):
        src_refs, out_refs = refs[:n], refs[n:2 * n]
        ici_send, ici_recv, d2d_send, d2d_recv, loc_sems = refs[2 * n:]
        x, y, c = lax.axis_index("x"), lax.axis_index("y"), lax.axis_index("c")
        me = 2 * x + y
        waits = []
        for a in range(n):
            for l in range(2):
                for q, (s_, d_) in enumerate(zip(_pieces(src_refs[a].at[l], nchunks[a]),
                                                 _pieces(out_refs[a].at[me, l], nchunks[a]))):
                    loc = pltpu.make_async_copy(s_, d_, loc_sems.at[(base[a] + q) * 2 + l])
                    loc.start()
                    waits.append(loc)
        first = []
        for a in range(n):
            for k, rel in enumerate(rels):
                px, py, _ = _flip(rel, x, y, c)
                for q, (s_, d_) in enumerate(zip(_pieces(src_refs[a].at[c], nchunks[a]),
                                                 _pieces(out_refs[a].at[me, c], nchunks[a]))):
                    sem = (base[a] + q) * 3 + k
                    cp = pltpu.make_async_remote_copy(src_ref=s_, dst_ref=d_, send_sem=ici_send.at[sem],
                                                      recv_sem=ici_recv.at[sem], device_id=(px, py, c),
                                                      device_id_type=_MESH)
                    cp.start()
                    first.append(cp)
        second = []
        for a in range(n):
            for k, rel in enumerate(rels):
                px, py, _ = _flip(rel, x, y, c)
                for q, blk in enumerate(_pieces(out_refs[a].at[2 * px + py, c], nchunks[a])):
                    sem = (base[a] + q) * 3 + k
                    landed = pltpu.make_async_remote_copy(src_ref=blk, dst_ref=blk, send_sem=ici_send.at[sem],
                                                          recv_sem=ici_recv.at[sem], device_id=(px, py, c),
                                                          device_id_type=_MESH)
                    landed.wait_recv()
                    cp = pltpu.make_async_remote_copy(src_ref=blk, dst_ref=blk, send_sem=d2d_send.at[sem],
                                                      recv_sem=d2d_recv.at[sem], device_id=(x, y, 1 - c),
                                                      device_id_type=_MESH)
                    cp.start()
                    second.append(cp)
        for cp in first:
            cp.wait_send()
        for cp in second:
            cp.wait_send()
        for a in range(n):
            for k, rel in enumerate(rels):
                px, py, _ = _flip(rel, x, y, c)
                for q, blk in enumerate(_pieces(out_refs[a].at[2 * px + py, 1 - c], nchunks[a])):
                    sem = (base[a] + q) * 3 + k
                    pltpu.make_async_remote_copy(src_ref=blk, dst_ref=blk, send_sem=d2d_send.at[sem],
                                                 recv_sem=d2d_recv.at[sem], device_id=(x, y, 1 - c),
                                                 device_id_type=_MESH).wait_recv()
        for cp in waits:
            cp.wait()

    any_spec = pl.BlockSpec(memory_space=pl.ANY)
    return pl.pallas_call(
        body, name="gather_weights",
        out_shape=[jax.ShapeDtypeStruct((N_CHIPS,) + a.shape, a.dtype) for a in wb],
        in_specs=[any_spec] * n, out_specs=[any_spec] * n,
        scratch_shapes=[pltpu.SemaphoreType.DMA((tot * 3,))] * 4 + [pltpu.SemaphoreType.DMA((tot * 2,))],
    )(*wb)


_HBM = pl.BlockSpec(memory_space=pltpu.HBM)
_SEM = pl.BlockSpec(memory_space=pltpu.SEMAPHORE)
_EFFECT = pltpu.SideEffectType.DATAFLOW_SIDE_EFFECTING


def _own_slot(name, src, chip, tb):
    rows, cols = src.shape[-2:]
    lead = src.shape[:-2]
    flat = src.reshape((-1, cols))

    def body(s_ref, a_ref, o_ref):
        o_ref[...] = a_ref[...]

    grid_spec = pltpu.PrefetchScalarGridSpec(
        num_scalar_prefetch=1, grid=(flat.shape[0] // tb,),
        in_specs=[pl.BlockSpec((tb, cols), lambda i, s: (i, 0))],
        out_specs=pl.BlockSpec((None, tb, cols), lambda i, s: (s[0], i, 0)))
    out = pl.pallas_call(body, name=name, grid_spec=grid_spec,
                         out_shape=jax.ShapeDtypeStruct((N_CHIPS,) + flat.shape, src.dtype),
                         compiler_params=_params(1))(chip, flat)
    return out.reshape((N_CHIPS,) + lead + (rows, cols))


def _late_copies(src_refs, land_refs, send_sems, recv_sems, nchunks):
    x, y, c = lax.axis_index("x"), lax.axis_index("y"), lax.axis_index("c")
    me = 2 * x + y
    rels = _GROUPS["xy"]
    copies = []
    idx = 0
    for a in range(len(src_refs)):
        for rel in rels:
            px, py, _ = _flip(rel, x, y, c)
            for s_, d_ in zip(_pieces(src_refs[a], nchunks[a]), _pieces(land_refs[a].at[me], nchunks[a])):
                copies.append(pltpu.make_async_remote_copy(
                    src_ref=s_, dst_ref=d_, send_sem=send_sems.at[idx], recv_sem=recv_sems.at[idx],
                    device_id=(px, py, c), device_id_type=_MESH))
                idx += 1
    return copies


def _late_gather_start(srcs, lands, nchunks, after):
    n = len(srcs)
    tot = 3 * sum(nchunks)
    na = len(after)

    def body(*refs):
        src_refs, land_refs = refs[:n], refs[n:2 * n]
        send_sems, recv_sems = refs[2 * n + na], refs[2 * n + na + 1]
        token = refs[-1]
        for cp in _late_copies(src_refs, land_refs, send_sems, recv_sems, nchunks):
            cp.start()
        token[...] = jnp.zeros_like(token)

    hbm = [pltpu.HBM(a.shape, a.dtype) for a in list(srcs) + list(lands)]
    outs = pl.pallas_call(
        body, name="late_gather_start",
        out_shape=(pltpu.SemaphoreType.DMA((tot,)), pltpu.SemaphoreType.DMA((tot,)), *hbm, _sds((8, 128))),
        in_specs=[_HBM] * (2 * n) + [pl.BlockSpec(memory_space=pl.ANY)] * na,
        out_specs=(_SEM, _SEM, *([_HBM] * (2 * n)), pl.BlockSpec(memory_space=pltpu.VMEM)),
        input_output_aliases={i: 2 + i for i in range(2 * n)},
        compiler_params=pltpu.CompilerParams(has_side_effects=_EFFECT),
    )(*[pltpu.with_memory_space_constraint(a, pltpu.HBM) for a in list(srcs) + list(lands)], *after)
    return outs[0], outs[1], outs[2:2 + n], outs[2 + n:2 + 2 * n], outs[-1]


def _late_gather_wait(send_sems, recv_sems, srcs, lands, nchunks, after):
    n = len(srcs)

    def body(*refs):
        src_refs, land_refs = refs[:n], refs[n:2 * n]
        s_sems, r_sems = refs[2 * n], refs[2 * n + 1]
        for cp in _late_copies(src_refs, land_refs, s_sems, r_sems, nchunks):
            cp.wait_send()
            cp.wait_recv()

    hbm = [pltpu.HBM(a.shape, a.dtype) for a in list(srcs) + list(lands)]
    outs = pl.pallas_call(
        body, name="late_gather_wait", out_shape=tuple(hbm),
        in_specs=[_HBM] * (2 * n) + [_SEM, _SEM, pl.BlockSpec(memory_space=pl.ANY)],
        out_specs=tuple([_HBM] * (2 * n)), input_output_aliases={i: i for i in range(2 * n)},
        compiler_params=pltpu.CompilerParams(has_side_effects=_EFFECT),
    )(*srcs, *lands, send_sems, recv_sems, after)
    return outs[n:2 * n]


def _mm(name, a, b, out_sds, *, grid, a_spec, b_spec, o_spec, dims, acc_shape, into=None):
    nk = grid[2]

    def body(*refs):
        a_ref, b_ref = refs[0], refs[1]
        o_ref, acc = refs[-2], refs[-1]
        k = pl.program_id(2)
        part = lax.dot_general(a_ref[...].astype(_MXU), b_ref[...].astype(_MXU), dims,
                               preferred_element_type=_F32)
        if nk == 1:
            o_ref[...] = part.astype(o_ref.dtype)
            return

        @pl.when(k == 0)
        def _():
            acc[...] = part

        @pl.when(k > 0)
        def _():
            acc[...] += part

        @pl.when(k == nk - 1)
        def _():
            o_ref[...] = acc[...].astype(o_ref.dtype)

    if nk == 1:
        acc_shape = (8, 128)
    in_specs = [a_spec, b_spec]
    args = [a, b]
    aliases = {}
    if into is not None:
        in_specs.append(pl.BlockSpec(memory_space=pl.ANY))
        args.append(into)
        aliases = {2: 0}
    return pl.pallas_call(
        body, name=name, grid=grid, in_specs=in_specs, out_specs=o_spec, out_shape=out_sds,
        scratch_shapes=[pltpu.VMEM(acc_shape, _F32)], input_output_aliases=aliases,
        compiler_params=_params(3))(*args)


_NN = (((1,), (0,)), ((), ()))
_NT = (((1,), (1,)), ((), ()))
_TN = (((0,), (0,)), ((), ()))


def _rowwise(name, body, *, grid, ins, outs, scratch=()):
    return pl.pallas_call(
        body, name=name, grid=(grid,), in_specs=[s for _, s in ins], out_specs=[s for _, s in outs],
        out_shape=[o for o, _ in outs], scratch_shapes=list(scratch),
        compiler_params=_params(1))(*[a for a, _ in ins])


def _rows(tb, w, cb=0, n=None):
    if n is None:
        return pl.BlockSpec((tb, w), lambda i: (i, cb))
    return pl.BlockSpec((tb, w), lambda i: (n - 1 - i, cb))


def _vec(shape):
    return pl.BlockSpec(shape, lambda i: (0,) * len(shape))


def _halo_prev(tb, w, cb=0, n=None, rows=8):
    if n is None:
        return pl.BlockSpec((rows, w), lambda i: (jnp.maximum(i * (tb // rows) - 1, 0), cb))
    return pl.BlockSpec((rows, w), lambda i: (jnp.maximum((n - 1 - i) * (tb // rows) - 1, 0), cb))


def _halo_next(tb, w, n, cb=0):
    return pl.BlockSpec((8, w), lambda i: (jnp.minimum((i + 1) * (tb // 8), n * (tb // 8) - 1), cb))


def _sds(shape, dtype=_F32):
    return jax.ShapeDtypeStruct(shape, dtype)


def _cast(name, a, tb):
    rows, cols = a.shape

    def body(a_ref, o_ref):
        o_ref[...] = a_ref[...].astype(o_ref.dtype)

    return _rowwise(name, body, grid=rows // tb, ins=[(a, _rows(tb, cols))],
                    outs=[(_sds((rows, cols), _MXU), _rows(tb, cols))])[0]


def _sum_lead(name, a, tb):
    g, rows, cols = a.shape

    def body(a_ref, o_ref):
        acc = a_ref[0]
        for k in range(1, g):
            acc = acc + a_ref[k]
        o_ref[...] = acc

    return _rowwise(name, body, grid=rows // tb,
                    ins=[(a, pl.BlockSpec((g, tb, cols), lambda i: (0, i, 0)))],
                    outs=[(_sds((rows, cols)), _rows(tb, cols))])[0]


def _sum_pair(name, mine, theirs, core, tb):
    _, nj, rows, cols = mine.shape

    def body(s_ref, a_ref, b_ref, o_ref, ob_ref):
        t = a_ref[...] + b_ref[...]
        o_ref[...] = t
        ob_ref[...] = t.astype(ob_ref.dtype)

    blk = pl.BlockSpec((None, tb, cols), lambda j, i, s: (j, i, 0))
    grid_spec = pltpu.PrefetchScalarGridSpec(
        num_scalar_prefetch=1, grid=(nj, rows // tb),
        in_specs=[pl.BlockSpec((None, None, tb, cols), lambda j, i, s: (s[0], j, i, 0)), blk],
        out_specs=[blk, blk])
    return pl.pallas_call(body, name=name, grid_spec=grid_spec,
                          out_shape=[_sds((nj, rows, cols)), _sds((nj, rows, cols), _MXU)],
                          compiler_params=_params(2))(core, mine, theirs)


def _sum_chips(name, mine, theirs, where, tb):
    _, rows, cols = mine.shape

    def body(s_ref, a_ref, b1_ref, b2_ref, b3_ref, o_ref):
        o_ref[...] = ((a_ref[...] + b1_ref[...].astype(_F32)) + b2_ref[...].astype(_F32)) + b3_ref[...].astype(_F32)

    def slot(k):
        return pl.BlockSpec((None, tb, cols), lambda i, s: (jnp.bitwise_xor(s[0], k), i, 0))

    grid_spec = pltpu.PrefetchScalarGridSpec(
        num_scalar_prefetch=1, grid=(rows // tb,),
        in_specs=[slot(0), slot(1), slot(2), slot(3)],
        out_specs=pl.BlockSpec((None, tb, cols), lambda i, s: (s[1], i, 0)))
    return pl.pallas_call(body, name=name, grid_spec=grid_spec, out_shape=_sds((2, rows, cols)),
                          compiler_params=_params(1))(where, mine, theirs, theirs, theirs)


def _adamw(name, w, g, m, v, tb):
    rows, cols = w.shape
    c1 = 1.0 - ADAM_B1 ** ADAM_STEP
    c2 = 1.0 - ADAM_B2 ** ADAM_STEP

    def body(w_ref, g_ref, m_ref, v_ref, d_ref, nm_ref, nv_ref):
        gv = g_ref[...]
        nm = ADAM_B1 * m_ref[...] + (1.0 - ADAM_B1) * gv
        nv = ADAM_B2 * v_ref[...] + (1.0 - ADAM_B2) * (gv * gv)
        d_ref[...] = -ADAM_LR * ((nm / c1) / (jnp.sqrt(nv / c2) + ADAM_EPS) + ADAM_WD * w_ref[...])
        nm_ref[...] = nm
        nv_ref[...] = nv

    spec = _rows(tb, cols)
    return _rowwise(name, body, grid=rows // tb, ins=[(w, spec), (g, spec), (m, spec), (v, spec)],
                    outs=[(_sds((rows, cols)), spec)] * 3)


def _mod_fwd(c_all, w_mod, b_cols):
    cols = w_mod.shape[2]

    def body(c_ref, w_ref, b_ref, o_ref):
        cv = c_ref[...]
        sc = (cv * _sigmoid(cv)).astype(_MXU)
        o_ref[...] = jnp.dot(sc, w_ref[...].astype(_MXU), preferred_element_type=_F32) + b_ref[...]

    return pl.pallas_call(
        body, name="mod_fwd", grid=(2,),
        in_specs=[pl.BlockSpec((8, D), lambda l: (0, 0)), pl.BlockSpec((None, D, cols), lambda l: (l, 0, 0)),
                  pl.BlockSpec((None, 1, cols), lambda l: (l, 0, 0))],
        out_specs=pl.BlockSpec((None, 8, cols), lambda l: (l, 0, 0)),
        out_shape=_sds((2, 8, cols)), compiler_params=_params(1))(c_all, w_mod, b_cols)


def _mod_bwd(c_all_t, dm):
    cols = dm.shape[2]

    def body(c_ref, d_ref, o_ref):
        cv = c_ref[...]
        sc = (cv * _sigmoid(cv)).astype(_MXU)
        o_ref[...] = jnp.dot(sc, d_ref[...].astype(_MXU), preferred_element_type=_F32)

    return pl.pallas_call(
        body, name="mod_bwd", grid=(2,),
        in_specs=[pl.BlockSpec((D, 8), lambda l: (0, 0)), pl.BlockSpec((None, 8, cols), lambda l: (l, 0, 0))],
        out_specs=pl.BlockSpec((None, D, cols), lambda l: (l, 0, 0)),
        out_shape=_sds((2, D, cols)), compiler_params=_params(1))(c_all_t, dm)


def _prenorm_fwd(x, g_pre, shift, scale):
    s = x.shape[0]
    tb = 512

    def body(x_ref, g_ref, sh_ref, sc_ref, h_ref, ht_ref):
        xv = x_ref[...]
        rstd = lax.rsqrt(jnp.mean(xv * xv, axis=-1, keepdims=True) + NORM_EPS)
        hv = (xv * rstd) * g_ref[...] * (1.0 + sc_ref[...]) + sh_ref[...]
        h_ref[...] = hv.astype(h_ref.dtype)
        ht_ref[...] = hv.T.astype(ht_ref.dtype)

    v = _vec((1, D))
    return _rowwise("prenorm_fwd", body, grid=s // tb,
                    ins=[(x, _rows(tb, D)), (g_pre, v), (shift, v), (scale, v)],
                    outs=[(_sds((s, D), _MXU), _rows(tb, D)),
                          (_sds((D, s), _MXU), pl.BlockSpec((D, tb), lambda i: (0, i)))])


def _shift_down(cur, halo, j, tb):
    ext = jnp.concatenate([halo, cur], axis=0)
    return pltpu.roll(ext, j, 0)[8:8 + tb]


def _shift_up(cur, halo, j, tb):
    ext = jnp.concatenate([cur, halo], axis=0)
    return pltpu.roll(ext, tb + 8 - j, 0)[0:tb]


def _conv_fwd(proj, conv_w, conv_b):
    s = proj.shape[0]
    tb = 512

    def body(u_ref, hp_ref, w_ref, b_ref, o_ref):
        i = pl.program_id(0)
        u = u_ref[...].astype(_F32)
        halo = jnp.where(i > 0, hp_ref[...].astype(_F32)[8:16], 0.0)
        acc = b_ref[...] + u * w_ref[0:1, :]
        for j in range(1, 4):
            acc = acc + _shift_down(u, halo, j, tb) * w_ref[j:j + 1, :]
        o_ref[...] = acc

    return _rowwise("conv_fwd", body, grid=s // tb,
                    ins=[(proj, _rows(tb, D, CB_U)), (proj, _halo_prev(tb, D, CB_U, rows=16)),
                         (conv_w, _vec((4, D))), (conv_b, _vec((1, D)))],
                    outs=[(_sds((s, D)), _rows(tb, D))])[0]


def _lru_gates(pre_r, pre_i, uc, b_rg, b_ig, lam):
    r = _sigmoid(pre_r + b_rg)
    ig = _sigmoid(pre_i + b_ig)
    nl = -lam
    sp = jnp.maximum(nl, 0.0) + jnp.log(1.0 + jnp.exp(-jnp.abs(nl)))
    la = -LRU_C * r * sp
    a = jnp.exp(la)
    one_m_a2 = -jnp.tanh(la) * (a * a + 1.0)
    inv_sq = lax.rsqrt(jnp.maximum(one_m_a2, 1e-30))
    return r, ig, sp, a, one_m_a2 * inv_sq, inv_sq


GATE_TILES = 8


def _gate_tiles(w_rg, w_ig):
    eye = jnp.eye(2, dtype=w_rg.dtype)

    def tiles(w):
        return jnp.einsum("cpij,pq->cpiqj", w.reshape(GATE_TILES, 2, 64, 64), eye).reshape(GATE_TILES, 128, 128)

    return jnp.concatenate([tiles(w_rg), tiles(w_ig)], axis=2)


def _gate_tile_grads(gw):
    keep = jnp.eye(2, dtype=jnp.bool_)[None, :, None, :, None]

    def blocks(t):
        t5 = t.reshape(GATE_TILES, 2, 64, 2, 64)
        return jnp.sum(jnp.where(keep, t5, 0.0), axis=3).reshape(16, 64, 64)

    return blocks(gw[:, :, 0:128]), blocks(gw[:, :, 128:256])


def _gate_preacts(ucv, wt_ref):
    ucb = ucv.astype(_MXU)
    ps = [jnp.dot(ucb[:, 128 * c:128 * (c + 1)], wt_ref[c], preferred_element_type=_F32) for c in range(GATE_TILES)]
    pre_r = jnp.concatenate([p[:, 0:128] for p in ps], axis=1)
    pre_i = jnp.concatenate([p[:, 128:256] for p in ps], axis=1)
    return pre_r, pre_i


def _scan_fwd(uc, wt, b_rg, b_ig, lam):
    s = uc.shape[0]
    tb = 256

    def body(uc_ref, wt_ref, brg_ref, big_ref, lam_ref, h_ref, carry, a_s, b_s):
        i = pl.program_id(0)

        @pl.when(i == 0)
        def _():
            carry[...] = jnp.zeros_like(carry)

        ucv = uc_ref[...]
        pre_r, pre_i = _gate_preacts(ucv, wt_ref)
        _, ig, _, a, sq, _ = _lru_gates(pre_r, pre_i, ucv, brg_ref[...], big_ref[...], lam_ref[...])
        av = a
        bv = sq * (ig * ucv)
        av = av.reshape(tb // 8, 8, D)
        bv = bv.reshape(tb // 8, 8, D)
        row8 = lax.broadcasted_iota(jnp.int32, (1, 8, 1), 1)
        for sh in (1, 2, 4):
            m = row8 >= sh
            b_sh = pltpu.roll(bv, sh, 1)
            a_sh = pltpu.roll(av, sh, 1)
            bv = jnp.where(m, av * b_sh + bv, bv)
            av = jnp.where(m, av * a_sh, av)
        a_s[...] = av.reshape(tb, D)
        b_s[...] = bv.reshape(tb, D)

        def tile(t, state):
            rows = pl.ds(pl.multiple_of(t * 8, 8), 8)
            hv = b_s[rows, :] + a_s[rows, :] * state
            h_ref[rows, :] = hv
            return jnp.broadcast_to(hv[7:8, :], (8, D))

        carry[...] = lax.fori_loop(0, tb // 8, tile, jnp.broadcast_to(carry[7:8, :], (8, D)), unroll=4)

    v = _vec((1, D))
    return _rowwise("scan_fwd", body, grid=s // tb,
                    ins=[(uc, _rows(tb, D)), (wt, _vec((GATE_TILES, 128, 256))), (b_rg, v), (b_ig, v), (lam, v)],
                    outs=[(_sds((s, D)), _rows(tb, D))],
                    scratch=[pltpu.VMEM((8, D), _F32), pltpu.VMEM((tb, D), _F32), pltpu.VMEM((tb, D), _F32)])[0]


def _weight_specs(l):
    return [pl.BlockSpec((N_CHIPS, None, ATT_W, 256), lambda i: (0, l, 0, 0)),
            pl.BlockSpec((N_CHIPS, None, 256, D), lambda i: (0, l, 0, 0)),
            pl.BlockSpec((N_CHIPS, None, 256, D), lambda i: (0, l, 0, 0))]


def _tail_fwd(l, o, h_lru, proj, x, gate, g_post, gw):
    s = x.shape[0]
    tb = 512

    def body(o_ref, h_ref, ga_ref, gl_ref, ma_ref, mb_ref, x_ref, gt_ref, gp_ref, wpa_ref, wpb_ref, wo_ref,
             aa_ref, ba_ref, ya_ref, yb_ref, z_ref, out_ref, xn_ref):
        ga = ga_ref[...].astype(_F32)
        aa = (o_ref[...] * (ga * _sigmoid(ga))).astype(_MXU)
        aa_ref[...] = aa
        gl = gl_ref[...].astype(_F32)
        ba = (h_ref[...] * (gl * _sigmoid(gl))).astype(_MXU)
        ba_ref[...] = ba
        ya = jnp.concatenate([jnp.dot(aa, wpa_ref[j], preferred_element_type=_F32) for j in range(N_CHIPS)], axis=1)
        ya_ref[...] = ya.astype(ya_ref.dtype)
        yb = jnp.dot(ba, wpb_ref[...].reshape(D, D), preferred_element_type=_F32)
        yb_ref[...] = yb.astype(yb_ref.dtype)
        z = (_sigmoid(ma_ref[...].astype(_F32)) * ya
             + _sigmoid(mb_ref[...].astype(_F32)) * yb).astype(z_ref.dtype)
        z_ref[...] = z
        ov = jnp.dot(z, wo_ref[...].reshape(D, D), preferred_element_type=_F32)
        out_ref[...] = ov
        rstd = lax.rsqrt(jnp.mean(ov * ov, axis=-1, keepdims=True) + NORM_EPS)
        xn_ref[...] = x_ref[...] + gt_ref[...] * ((ov * rstd) * gp_ref[...])

    v = _vec((1, D))
    r = _rows(tb, D)
    r5 = _rows(tb, ATT_W)
    return _rowwise("tail_fwd", body, grid=s // tb,
                    ins=[(o, r5), (h_lru, r), (proj, _rows(tb, ATT_W, CB_GATT)), (proj, _rows(tb, D, CB_GLRU)),
                         (proj, _rows(tb, D, CB_MA)), (proj, _rows(tb, D, CB_MB)), (x, r), (gate, v), (g_post, v)]
                    + list(zip((gw["w_pa"], gw["w_pb"], gw["w_o"]), _weight_specs(l))),
                    outs=[(_sds((s, ATT_W), _MXU), r5), (_sds((s, D), _MXU), r), (_sds((s, D), _MXU), r),
                          (_sds((s, D), _MXU), r), (_sds((s, D), _MXU), r), (_sds((s, D)), r), (_sds((s, D)), r)])


def _loss_head(y, target):
    s = y.shape[0]
    tb = 512

    def body(y_ref, t_ref, dy_ref, acc_ref):
        i = pl.program_id(0)

        @pl.when(i == 0)
        def _():
            acc_ref[...] = jnp.zeros_like(acc_ref)

        err = y_ref[...] - t_ref[...]
        dy_ref[...] = err * (1.0 / D)
        acc_ref[...] += jnp.sum(err * err, axis=0, keepdims=True)

    return _rowwise("loss_head", body, grid=s // tb,
                    ins=[(y, _rows(tb, D)), (target, _rows(tb, D))],
                    outs=[(_sds((s, D)), _rows(tb, D)), (_sds((1, D)), _vec((1, D)))])


def _zero_first(i, *refs):
    @pl.when(i == 0)
    def _():
        for ref in refs:
            ref[...] = jnp.zeros_like(ref)


def _tail_bwd(l, dx, out, y_a, y_b, proj, o, h_lru, gate, g_post, gw):
    s = dx.shape[0]
    tb = 256

    def body(dx_ref, out_ref, ya_ref, yb_ref, ma_ref, mb_ref, o_ref, ga_ref, h_ref, gl_ref, gt_ref, gp_ref,
             wpa_ref, wpb_ref, wo_ref,
             dout_ref, dya_ref, dyb_ref, rest_ref, do_ref, dh_ref, dgt_ref, dgp_ref):
        i = pl.program_id(0)
        ov = out_ref[...]
        dxv = dx_ref[...]
        rstd = lax.rsqrt(jnp.mean(ov * ov, axis=-1, keepdims=True) + NORM_EPS)
        nv = ov * rstd
        s_dn = jnp.sum(dxv * nv, axis=0, keepdims=True)
        _zero_first(i, dgt_ref, dgp_ref)
        dgt_ref[...] += s_dn * gp_ref[...]
        dgp_ref[...] += s_dn * gt_ref[...]
        dn = dxv * (gt_ref[...] * gp_ref[...])
        d_out = (rstd * (dn - nv * jnp.mean(dn * nv, axis=-1, keepdims=True))).astype(_MXU)
        dout_ref[...] = d_out
        dz = lax.dot_general(d_out, wo_ref[...].reshape(D, D), _NT, preferred_element_type=_F32)
        ga = _sigmoid(ma_ref[...].astype(_F32))
        gb = _sigmoid(mb_ref[...].astype(_F32))
        dya = (dz * ga).astype(_MXU)
        dyb = (dz * gb).astype(_MXU)
        dya_ref[...] = dya
        dyb_ref[...] = dyb
        rest_ref[:, R_MA:R_MB] = (dz * ya_ref[...].astype(_F32) * ga * (1.0 - ga)).astype(rest_ref.dtype)
        rest_ref[:, R_MB:R_END] = (dz * yb_ref[...].astype(_F32) * gb * (1.0 - gb)).astype(rest_ref.dtype)
        daa = lax.dot_general(dya[:, 0:256], wpa_ref[0], _NT, preferred_element_type=_F32)
        for j in range(1, N_CHIPS):
            daa = daa + lax.dot_general(dya[:, j * 256:(j + 1) * 256], wpa_ref[j], _NT, preferred_element_type=_F32)
        dba = lax.dot_general(dyb, wpb_ref[...].reshape(D, D), _NT, preferred_element_type=_F32)
        gav = ga_ref[...].astype(_F32)
        sa = _sigmoid(gav)
        do_ref[...] = daa * (gav * sa)
        rest_ref[:, 0:R_U] = (daa * o_ref[...] * (sa * (1.0 + gav * (1.0 - sa)))).astype(rest_ref.dtype)
        gl = gl_ref[...].astype(_F32)
        sl = _sigmoid(gl)
        dh_ref[...] = dba * (gl * sl)
        rest_ref[:, R_GLRU:R_MA] = (dba * h_ref[...] * (sl * (1.0 + gl * (1.0 - sl)))).astype(rest_ref.dtype)

    v = _vec((1, D))
    r5, r10 = _rows(tb, ATT_W), _rows(tb, D)
    return _rowwise("tail_bwd", body, grid=s // tb,
                    ins=[(dx, r10), (out, r10), (y_a, r10), (y_b, r10), (proj, _rows(tb, D, CB_MA)),
                         (proj, _rows(tb, D, CB_MB)), (o, r5), (proj, _rows(tb, ATT_W, CB_GATT)), (h_lru, r10),
                         (proj, _rows(tb, D, CB_GLRU)), (gate, v), (g_post, v)]
                    + list(zip((gw["w_pa"], gw["w_pb"], gw["w_o"]), _weight_specs(l))),
                    outs=[(_sds((s, D), _MXU), r10), (_sds((s, D), _MXU), r10), (_sds((s, D), _MXU), r10),
                          (_sds((s, R_END), _MXU), _rows(tb, R_END)),
                          (_sds((s, ATT_W)), r5), (_sds((s, D)), r10), (_sds((1, D)), v), (_sds((1, D)), v)])


def _scan_bwd(dh, uc, h_lru, wt, b_rg, b_ig, lam):
    s = uc.shape[0]
    tb = 256
    n = s // tb

    def body(dh_ref, uc_ref, h_ref, hp_ref, wt_ref, brg_ref, big_ref, lam_ref,
             duc_ref, dwt_ref, dbrg_ref, dbig_ref, dlam_ref, carry, c_s, g_s):
        i = pl.program_id(0)

        @pl.when(i == 0)
        def _():
            carry[...] = jnp.zeros_like(carry)
            for acc_ref in (dwt_ref, dbrg_ref, dbig_ref, dlam_ref):
                acc_ref[...] = jnp.zeros_like(acc_ref)

        ucv = uc_ref[...]
        pre_r, pre_i = _gate_preacts(ucv, wt_ref)
        r, ig, sp, a, sq, inv_sq =_lru_gates(pre_r, pre_i, ucv, brg_ref[...], big_ref[...], lam_ref[...])
        row = lax.broadcasted_iota(jnp.int32, (tb, 1), 0)
        cv = jnp.where(row == tb - 1, 1.0, pltpu.roll(a, tb - 1, 0))
        gv = dh_ref[...]
        cv = cv.reshape(tb // 8, 8, D)
        gv = gv.reshape(tb // 8, 8, D)
        row8 = lax.broadcasted_iota(jnp.int32, (1, 8, 1), 1)
        for sh in (1, 2, 4):
            m = row8 < 8 - sh
            g_sh = pltpu.roll(gv, 8 - sh, 1)
            c_sh = pltpu.roll(cv, 8 - sh, 1)
            gv = jnp.where(m, gv + cv * g_sh, gv)
            cv = jnp.where(m, cv * c_sh, cv)
        c_s[...] = cv.reshape(tb, D)
        g_s[...] = gv.reshape(tb, D)

        def tile(k, state):
            rows = pl.ds(pl.multiple_of((tb // 8 - 1 - k) * 8, 8), 8)
            gt = g_s[rows, :] + c_s[rows, :] * state
            g_s[rows, :] = gt
            return jnp.broadcast_to(gt[0:1, :], (8, D))

        lax.fori_loop(0, tb // 8, tile, jnp.broadcast_to(carry[0:1, :], (8, D)), unroll=4)
        gv = g_s[...]
        carry[...] = (a * gv)[0:8]

        halo = jnp.where(i < n - 1, hp_ref[...], 0.0)
        h_prev = _shift_down(h_ref[...], halo, 1, tb)
        d_a = gv * h_prev
        d_sq = gv * (ig * ucv)
        d_i = gv * sq * ucv
        d_la = d_a * a - d_sq * (a * a) * inv_sq
        d_r = d_la * (-LRU_C * sp)
        d_pre_r = d_r * r * (1.0 - r)
        d_pre_i = d_i * ig * (1.0 - ig)
        ucb = ucv.astype(_MXU)
        dpr = d_pre_r.astype(_MXU)
        dpi = d_pre_i.astype(_MXU)
        back = []
        for c in range(GATE_TILES):
            lanes = slice(128 * c, 128 * (c + 1))
            dp = jnp.concatenate([dpr[:, lanes], dpi[:, lanes]], axis=1)
            back.append(lax.dot_general(dp, wt_ref[c], _NT, preferred_element_type=_F32))
            dwt_ref[c] += lax.dot_general(ucb[:, lanes], dp, _TN, preferred_element_type=_F32)
        duc_ref[...] = gv * sq * ig + jnp.concatenate(back, axis=1)
        dbrg_ref[...] += jnp.sum(d_pre_r, axis=0, keepdims=True)
        dbig_ref[...] += jnp.sum(d_pre_i, axis=0, keepdims=True)
        lamv = lam_ref[...]
        dlam_ref[...] += jnp.sum(d_la * (-LRU_C * r), axis=0, keepdims=True) * (-_sigmoid(-lamv))

    v = _vec((1, D))
    rv = _rows(tb, D, 0, n)
    return _rowwise("scan_bwd", body, grid=n,
                    ins=[(dh, rv), (uc, rv), (h_lru, rv), (h_lru, _halo_prev(tb, D, 0, n)),
                         (wt, _vec((GATE_TILES, 128, 256))), (b_rg, v), (b_ig, v), (lam, v)],
                    outs=[(_sds((s, D)), rv), (_sds((GATE_TILES, 128, 256)), _vec((GATE_TILES, 128, 256))),
                          (_sds((1, D)), v), (_sds((1, D)), v), (_sds((1, D)), v)],
                    scratch=[pltpu.VMEM((8, D), _F32), pltpu.VMEM((tb, D), _F32), pltpu.VMEM((tb, D), _F32)])


def _conv_bwd(duc_a, proj, conv_w, rest):
    s = duc_a.shape[0]
    tb = 512
    n = s // tb
    hw = D // 2

    def body(da_ref, dan_ref, u_ref, up_ref, w_ref, rest_in, du_ref, dw_ref, dbias_ref):
        i = pl.program_id(1)
        duc = da_ref[...]
        nxt = jnp.where(i < n - 1, dan_ref[...], 0.0)
        u = u_ref[...].astype(_F32)
        halo = jnp.where(i > 0, up_ref[...].astype(_F32)[8:16], 0.0)
        du = duc * w_ref[0:1, :]
        dws = [jnp.sum(duc * u, axis=0, keepdims=True)]
        for j in range(1, 4):
            du = du + _shift_up(duc, nxt, j, tb) * w_ref[j:j + 1, :]
            dws.append(jnp.sum(duc * _shift_down(u, halo, j, tb), axis=0, keepdims=True))
        du_ref[...] = du.astype(du_ref.dtype)
        _zero_first(i, dw_ref, dbias_ref)
        for j in range(4):
            dw_ref[j:j + 1, :] += dws[j]
        dbias_ref[...] += jnp.sum(duc, axis=0, keepdims=True)

    r = pl.BlockSpec((tb, hw), lambda h, i: (i, h))
    nxt_spec = pl.BlockSpec((8, hw), lambda h, i: (jnp.minimum((i + 1) * (tb // 8), n * (tb // 8) - 1), h))
    return pl.pallas_call(
        body, name="conv_bwd", grid=(2, n),
        in_specs=[r, nxt_spec,
                  pl.BlockSpec((tb, hw), lambda h, i: (i, 2 * CB_U + h)),
                  pl.BlockSpec((16, hw), lambda h, i: (jnp.maximum(i * (tb // 16) - 1, 0), 2 * CB_U + h)),
                  pl.BlockSpec((4, hw), lambda h, i: (0, h)), pl.BlockSpec(memory_space=pl.ANY)],
        out_specs=[pl.BlockSpec((tb, hw), lambda h, i: (i, R_U // hw + h)),
                   pl.BlockSpec((4, hw), lambda h, i: (0, h)), pl.BlockSpec((1, hw), lambda h, i: (0, h))],
        out_shape=[_sds(rest.shape, rest.dtype), _sds((4, D)), _sds((1, D))],
        input_output_aliases={5: 0}, compiler_params=_params(2),
    )(duc_a, duc_a, proj, proj, conv_w, rest)


def _prenorm_bwd(dh, x, dx_out, g_pre, scale):
    s = x.shape[0]
    tb = 512

    def body(dh_ref, x_ref, dxo_ref, g_ref, sc_ref, dx_ref, dsh_ref, dsc_ref, dg_ref):
        i = pl.program_id(0)
        xv = x_ref[...]
        dhv = dh_ref[...]
        rstd = lax.rsqrt(jnp.mean(xv * xv, axis=-1, keepdims=True) + NORM_EPS)
        xn = xv * rstd
        one_sc = 1.0 + sc_ref[...]
        s1 = jnp.sum(dhv * xn, axis=0, keepdims=True)
        _zero_first(i, dsh_ref, dsc_ref, dg_ref)
        dsh_ref[...] += jnp.sum(dhv, axis=0, keepdims=True)
        dsc_ref[...] += s1 * g_ref[...]
        dg_ref[...] += s1 * one_sc
        dxn = dhv * (g_ref[...] * one_sc)
        dx_ref[...] = dxo_ref[...] + rstd * (dxn - xn * jnp.mean(dxn * xn, axis=-1, keepdims=True))

    v = _vec((1, D))
    r = _rows(tb, D)
    return _rowwise("prenorm_bwd", body, grid=s // tb,
                    ins=[(dh, r), (x, r), (dx_out, r), (g_pre, v), (scale, v)],
                    outs=[(_sds((s, D)), r), (_sds((1, D)), v), (_sds((1, D)), v), (_sds((1, D)), v)])


def _band_tiles(dil):
    tiles = []
    for rho in range(dil):
        for b in range(16 // dil):
            qs = rho + dil * BAND * b
            tiles.append((qs, QBLK + qs - dil * BAND, b))
    return tiles


def _strided(start, size, dil):
    return pl.ds(start, size, stride=dil) if dil > 1 else pl.ds(start, size)


def _band_mask(i, b):
    qi = lax.broadcasted_iota(jnp.int32, (BAND, 2 * BAND), 0)
    ki = lax.broadcasted_iota(jnp.int32, (BAND, 2 * BAND), 1)
    valid = (ki >= qi) & (ki <= qi + BAND)
    if b == 0:
        valid = valid & ((ki >= BAND) | (i > 0))
    return valid


def _attn_fwd(proj):
    s = proj.shape[0]
    n = s // QBLK
    scale = HEAD ** -0.5

    def body(*refs):
        q_refs, kp_refs, kc_refs, vp_refs, vc_refs = (refs[3 * t:3 * t + 3] for t in range(5))
        o_ref, lse_ref, qbuf, kbuf, vbuf = refs[15:20]
        accs, maxs, dens = refs[20:23], refs[23:26], refs[26:29]
        i = pl.program_id(1)
        for g, dil in enumerate(DILATIONS):
            qbuf[...] = q_refs[g][...].astype(_F32)
            kbuf[0:QBLK, :] = kp_refs[g][...].astype(_F32)
            kbuf[QBLK:2 * QBLK, :] = kc_refs[g][...].astype(_F32)
            vbuf[0:QBLK, :] = vp_refs[g][...].astype(_F32)
            vbuf[QBLK:2 * QBLK, :] = vc_refs[g][...].astype(_F32)
            for qs, ks, b in _band_tiles(dil):
                qsl = _strided(qs, BAND, dil)
                q = qbuf[qsl, :].astype(_MXU)
                kk = kbuf[_strided(ks, 2 * BAND, dil), :].astype(_MXU)
                vv = vbuf[_strided(ks, 2 * BAND, dil), :].astype(_MXU)
                sc = lax.dot_general(q, kk, _NT, preferred_element_type=_F32) * scale
                sc = jnp.where(_band_mask(i, b), sc, NEG_INF)
                m = jnp.max(sc, axis=-1, keepdims=True)
                p = jnp.exp(sc - m)
                accs[g][qsl, :] = jnp.dot(p.astype(_MXU), vv, preferred_element_type=_F32)
                maxs[g][qsl, :] = jnp.broadcast_to(m, (BAND, HEAD))
                dens[g][qsl, :] = jnp.broadcast_to(jnp.sum(p, axis=-1, keepdims=True), (BAND, HEAD))
        ms = [r[...] for r in maxs]
        mx = jnp.maximum(jnp.maximum(ms[0], ms[1]), ms[2])
        ws = [jnp.exp(m - mx) for m in ms]
        den = ws[0] * dens[0][...] + ws[1] * dens[1][...] + ws[2] * dens[2][...]
        o_ref[...] = (ws[0] * accs[0][...] + ws[1] * accs[1][...] + ws[2] * accs[2][...]) / den
        lse_ref[...] = mx + jnp.log(den)

    blk = (QBLK, HEAD)

    def spec(first_col, lag):
        specs = []
        for g in range(3):
            col = first_col + g * HEADS
            if lag:
                specs.append(pl.BlockSpec(blk, lambda j, i, col=col: (jnp.maximum(i - 1, 0), col + j)))
            else:
                specs.append(pl.BlockSpec(blk, lambda j, i, col=col: (i, col + j)))
        return specs

    out_spec = pl.BlockSpec(blk, lambda j, i: (i, j))
    return pl.pallas_call(
        body, name="attn_fwd", grid=(HEADS, n),
        in_specs=spec(0, False) + spec(12, True) + spec(12, False) + spec(24, True) + spec(24, False),
        out_specs=[out_spec] * 2, out_shape=[_sds((s, ATT_W))] * 2,
        scratch_shapes=[pltpu.VMEM(blk, _F32)] + [pltpu.VMEM((2 * QBLK, HEAD), _F32)] * 2
        + [pltpu.VMEM(blk, _F32)] * 9,
        compiler_params=_params(2))(*([proj] * 15))


def _attn_bwd(proj, d_o, o, lse, g, into):
    s = proj.shape[0]
    dil = DILATIONS[g]
    n = s // QBLK
    scale = HEAD ** -0.5
    tiles = _band_tiles(dil)

    def body(*refs):
        q_ref, kp_ref, kc_ref, vp_ref, vc_ref, do_ref, o_ref, lse_ref = refs[0:8]
        dq_ref, dk_ref, dv_ref, kbuf, vbuf, dkbuf, dvbuf, dqbuf, qbuf = refs[-9:]
        i = pl.program_id(1)

        @pl.when(i == 0)
        def _():
            dkbuf[0:QBLK, :] = jnp.zeros((QBLK, HEAD), _F32)
            dvbuf[0:QBLK, :] = jnp.zeros((QBLK, HEAD), _F32)

        @pl.when(i < n)
        def _():
            qbuf[...] = q_ref[...].astype(_F32)
            kbuf[0:QBLK, :] = kp_ref[...].astype(_F32)
            kbuf[QBLK:2 * QBLK, :] = kc_ref[...].astype(_F32)
            vbuf[0:QBLK, :] = vp_ref[...].astype(_F32)
            vbuf[QBLK:2 * QBLK, :] = vc_ref[...].astype(_F32)
            dkbuf[QBLK:2 * QBLK, :] = jnp.zeros((QBLK, HEAD), _F32)
            dvbuf[QBLK:2 * QBLK, :] = jnp.zeros((QBLK, HEAD), _F32)
            for qs, ks, b in tiles:
                qsl = _strided(qs, BAND, dil)
                ksl = _strided(ks, 2 * BAND, dil)
                q = qbuf[qsl, :].astype(_MXU)
                kk = kbuf[ksl, :].astype(_MXU)
                vv = vbuf[ksl, :].astype(_MXU)
                dov = do_ref[qsl, :]
                dd = jnp.sum(dov * o_ref[qsl, :], axis=-1, keepdims=True)
                lse_t = lse_ref[qsl, :][:, 0:1]
                sc = lax.dot_general(q, kk, _NT, preferred_element_type=_F32) * scale
                p = jnp.where(_band_mask(i, b), jnp.exp(sc - lse_t), 0.0)
                dob = dov.astype(_MXU)
                dp = lax.dot_general(dob, vv, _NT, preferred_element_type=_F32)
                ds = (p * (dp - dd) * scale).astype(_MXU)
                dqbuf[qsl, :] = jnp.dot(ds, kk, preferred_element_type=_F32)
                dkbuf[ksl, :] += lax.dot_general(ds, q, _TN, preferred_element_type=_F32)
                dvbuf[ksl, :] += lax.dot_general(p.astype(_MXU), dob, _TN, preferred_element_type=_F32)
            dq_ref[...] = dqbuf[...].astype(dq_ref.dtype)

        dk_ref[...] = dkbuf[0:QBLK, :].astype(dk_ref.dtype)
        dv_ref[...] = dvbuf[0:QBLK, :].astype(dv_ref.dtype)
        dkbuf[0:QBLK, :] = dkbuf[QBLK:2 * QBLK, :]
        dvbuf[0:QBLK, :] = dvbuf[QBLK:2 * QBLK, :]

    blk = (QBLK, HEAD)
    cq, ck, cv = g * HEADS, 12 + g * HEADS, 24 + g * HEADS

    def cur(i):
        return jnp.minimum(i, n - 1)

    def prev(i):
        return jnp.maximum(jnp.minimum(i, n - 1) - 1, 0)

    own = pl.BlockSpec(blk, lambda j, i: (cur(i), j))
    own_out = pl.BlockSpec(blk, lambda j, i: (cur(i), cq + j))
    late_out = pl.BlockSpec(blk, lambda j, i: (jnp.maximum(i - 1, 0), cq + j))
    extra = [] if into is None else list(into)
    return pl.pallas_call(
        body, name="attn_bwd_d%d" % dil, grid=(HEADS, n + 1),
        in_specs=[pl.BlockSpec(blk, lambda j, i: (cur(i), cq + j)),
                  pl.BlockSpec(blk, lambda j, i: (prev(i), ck + j)),
                  pl.BlockSpec(blk, lambda j, i: (cur(i), ck + j)),
                  pl.BlockSpec(blk, lambda j, i: (prev(i), cv + j)),
                  pl.BlockSpec(blk, lambda j, i: (cur(i), cv + j)),
                  own, own, own] + [pl.BlockSpec(memory_space=pl.ANY)] * len(extra),
        out_specs=[own_out, late_out, late_out], out_shape=[_sds((s, QKV_W), _MXU)] * 3,
        input_output_aliases={8 + t: t for t in range(len(extra))},
        scratch_shapes=[pltpu.VMEM((2 * QBLK, HEAD), _F32)] * 4 + [pltpu.VMEM((QBLK, HEAD), _F32)] * 2,
        compiler_params=_params(2))(proj, proj, proj, proj, proj, d_o, o, lse, *extra)


_PARTS = ((0, 2), (2, 2), (4, 2), (6, 6))
_CHUNK = 768


def _d_h(parts, w_in):
    s = parts[0].shape[0]
    nk = IN_W // _CHUNK

    def body(p0, p1, p2, p3, w_ref, o_ref, acc):
        k = pl.program_id(2)

        @pl.when(k == 0)
        def _():
            acc[...] = jnp.zeros_like(acc)

        for p_ref, (first, cnt) in zip((p0, p1, p2, p3), _PARTS):
            @pl.when((k >= first) & (k < first + cnt))
            def _(p_ref=p_ref):
                acc[...] += lax.dot_general(p_ref[...].astype(_MXU), w_ref[...], _NT, preferred_element_type=_F32)

        @pl.when(k == nk - 1)
        def _():
            o_ref[...] = acc[...]

    def part_spec(first, cnt):
        return pl.BlockSpec((1024, _CHUNK), lambda m, n, k: (m, jnp.clip(k - first, 0, cnt - 1)))

    return pl.pallas_call(
        body, name="d_h", grid=(s // 1024, 1, nk),
        in_specs=[part_spec(*p) for p in _PARTS]
        + [pl.BlockSpec((None, D, _CHUNK), lambda m, n, k: (k // 3, 0, k % 3))],
        out_specs=pl.BlockSpec((1024, D), lambda m, n, k: (m, 0)), out_shape=_sds((s, D)),
        scratch_shapes=[pltpu.VMEM((1024, D), _F32)], compiler_params=_params(3))(*parts, w_in)


def _g_w_in(l, h_t, parts, into):
    s = h_t.shape[1]
    nk = s // 1024

    def body(*refs):
        h_ref, p_refs = refs[0], refs[1:5]
        o_ref, acc = refs[-2], refs[-1]
        n = pl.program_id(1)
        k = pl.program_id(2)

        @pl.when(k == 0)
        def _():
            acc[...] = jnp.zeros_like(acc)

        for p_ref, (first, cnt) in zip(p_refs, _PARTS):
            @pl.when((n >= first) & (n < first + cnt))
            def _(p_ref=p_ref):
                acc[...] += jnp.dot(h_ref[...], p_ref[...].astype(_MXU), preferred_element_type=_F32)

        @pl.when(k == nk - 1)
        def _():
            o_ref[...] = acc[...]

    def part_spec(first, cnt):
        def index(m, n, k):
            row = jnp.where(n < first, 0, jnp.where(n >= first + cnt, nk - 1, k))
            return (row, jnp.clip(n - first, 0, cnt - 1))
        return pl.BlockSpec((1024, _CHUNK), index)

    extra = [] if into is None else [into]
    return pl.pallas_call(
        body, name="g_w_in", grid=(1, IN_W // _CHUNK, nk),
        in_specs=[pl.BlockSpec((D, 1024), lambda m, n, k: (0, k))] + [part_spec(*p) for p in _PARTS]
        + [pl.BlockSpec(memory_space=pl.ANY)] * len(extra),
        out_specs=pl.BlockSpec((None, None, D, _CHUNK), lambda m, n, k: (l, n // 3, 0, n % 3)),
        out_shape=_sds((2, N_CHIPS, D, 2304)), input_output_aliases={5: 0} if extra else {},
        scratch_shapes=[pltpu.VMEM((D, _CHUNK), _F32)], compiler_params=_params(3))(h_t, *parts, *extra)


def _layer_fwd(l, x, p, gw, late):
    s = x.shape[0]
    nm = s // 1024
    h, h_t = _prenorm_fwd(x, p["g_pre"], p["shift"], p["scale"])
    proj = _mm("proj", h, gw["w_in"][l], _sds((s, IN_W), _MXU), grid=(nm, N_CHIPS, 1),
               a_spec=pl.BlockSpec((1024, D), lambda m, n, k: (m, 0)),
               b_spec=pl.BlockSpec((None, D, 2304), lambda m, n, k: (n, 0, 0)),
               o_spec=pl.BlockSpec((1024, 2304), lambda m, n, k: (m, n)), dims=_NN, acc_shape=(1024, 2304))
    o, lse = _attn_fwd(proj)
    uc = _conv_fwd(proj, p["conv_w"], p["conv_b"])
    h_lru = _scan_fwd(uc, p["wt"], p["b_rg"], p["b_ig"], p["lam"])
    if late is not None:
        landed = dict(late(h_lru))
        gw["w_in"].append(landed.pop("w_in1"))
        gw.update(landed)
    a_att, b_act, y_a, y_b, z, out, x_new = _tail_fwd(l, o, h_lru, proj, x, p["gate"], p["g_post"], gw)
    saved = dict(x=x, h_t=h_t, proj=proj, o=o, lse=lse, uc=uc, h_lru=h_lru, a_att=a_att, b_act=b_act,
                 y_a=y_a, y_b=y_b, z=z, out=out)
    return x_new, saved


def _layer_bwd(l, dx, p, gw, sv, big):
    s = dx.shape[0]
    nm = s // 1024
    nt = s // 2048
    proj = sv["proj"]
    d_out, dy_a, dy_b, d_rest, d_o, dh_lru, d_gate, d_gpost = _tail_bwd(
        l, dx, sv["out"], sv["y_a"], sv["y_b"], proj, sv["o"], sv["h_lru"], p["gate"], p["g_post"], gw)

    def wgrad_rows(name, a, b, into):
        return _mm(name, a, b, _sds((2, N_CHIPS, 256, D)), grid=(4, 1, nt),
                   a_spec=pl.BlockSpec((2048, 256), lambda m, n, k: (k, m)),
                   b_spec=pl.BlockSpec((2048, D), lambda m, n, k: (k, 0)),
                   o_spec=pl.BlockSpec((None, None, 256, D), lambda m, n, k: (l, m, 0, 0)),
                   dims=_TN, acc_shape=(256, D), into=into)

    big = dict(big)
    big["w_o"] = wgrad_rows("g_w_o", sv["z"], d_out, big.get("w_o"))
    big["w_pa"] = _mm("g_w_pa", sv["a_att"], dy_a, _sds((2, N_CHIPS, ATT_W, 256)), grid=(1, 4, nt),
                      a_spec=pl.BlockSpec((2048, ATT_W), lambda m, n, k: (k, 0)),
                      b_spec=pl.BlockSpec((2048, 256), lambda m, n, k: (k, n)),
                      o_spec=pl.BlockSpec((None, None, ATT_W, 256), lambda m, n, k: (l, n, 0, 0)),
                      dims=_TN, acc_shape=(ATT_W, 256), into=big.get("w_pa"))
    big["w_pb"] = wgrad_rows("g_w_pb", sv["b_act"], dy_b, big.get("w_pb"))
    duc, g_wt, d_brg, d_big, d_lam = _scan_bwd(dh_lru, sv["uc"], sv["h_lru"], p["wt"],
                                               p["b_rg"], p["b_ig"], p["lam"])
    g_wrg, g_wig = _gate_tile_grads(g_wt)
    d_rest, g_convw, g_convb = _conv_bwd(duc, proj, p["conv_w"], d_rest)
    dqkv = None
    for g in range(3):
        dqkv = _attn_bwd(proj, d_o, sv["o"], sv["lse"], g, dqkv)
    parts = (dqkv[0], dqkv[1], dqkv[2], d_rest)
    dh = _d_h(parts, gw["w_in"][l])
    big["w_in"] = _g_w_in(l, sv["h_t"], parts, big.get("w_in"))
    dx_in, d_shift, d_scale, d_gpre = _prenorm_bwd(dh, sv["x"], dx, p["g_pre"], p["scale"])
    small = dict(dmod=jnp.concatenate([d_shift, d_scale, d_gate], axis=1), g_pre=d_gpre, conv_w=g_convw,
                 conv_b=g_convb, w_rg=g_wrg, b_rg=d_brg, w_ig=g_wig, b_ig=d_big, lam=d_lam, g_post=d_gpost)
    return dx_in, small, big


def _local_step(x, target, small_p, w_in0, late):
    saved = []
    h = x
    gw = dict(w_in=[w_in0])
    for l in range(2):
        h, sv = _layer_fwd(l, h, small_p[l], gw, late if l == 0 else None)
        saved.append(sv)
    dy, sq = _loss_head(h, target)
    loss = 0.5 * jnp.sum(sq) / D
    big = {}
    smalls = [None, None]
    dx = dy
    for l in (1, 0):
        dx, smalls[l], big = _layer_bwd(l, dx, small_p[l], gw, saved[l], big)
    return loss, dx, smalls, big


_SMALL_ROWS = 8 + 16 + 8 + 128 + 128


def _pack_small(smalls):
    dmod = jnp.concatenate([smalls[0]["dmod"].reshape(3, D), smalls[1]["dmod"].reshape(3, D),
                            jnp.zeros((2, D), _F32)], axis=0)
    vecs = jnp.concatenate([smalls[l][k] for k in ("g_pre", "conv_b", "b_rg", "b_ig", "lam", "g_post")
                            for l in range(2)] + [jnp.zeros((4, D), _F32)], axis=0)
    convw = jnp.concatenate([smalls[0]["conv_w"], smalls[1]["conv_w"]], axis=0)
    wrg = jnp.stack([smalls[0]["w_rg"], smalls[1]["w_rg"]]).reshape(128, D)
    wig = jnp.stack([smalls[0]["w_ig"], smalls[1]["w_ig"]]).reshape(128, D)
    return jnp.concatenate([dmod, vecs, convw, wrg, wig], axis=0)


def kernel(x, c, w_mod, b_mod, g_pre, w_in, conv_w, conv_b, w_rg, b_rg, w_ig, b_ig, lru_lambda, w_pa, w_pb, w_o, g_post, loss_target, m_w_mod, m_b_mod, m_g_pre, m_w_in, m_conv_w, m_conv_b, m_w_rg, m_b_rg, m_w_ig, m_b_ig, m_lru_lambda, m_w_pa, m_w_pb, m_w_o, m_g_post, v_w_mod, v_b_mod, v_g_pre, v_w_in, v_conv_w, v_conv_b, v_w_rg, v_b_rg, v_w_ig, v_b_ig, v_lru_lambda, v_w_pa, v_w_pb, v_w_o, v_g_post):
    xi, yi, ci = lax.axis_index("x"), lax.axis_index("y"), lax.axis_index("c")
    chip = 2 * xi + yi
    dev = 4 * xi + 2 * yi + ci
    mcols = w_mod.shape[2]

    pack1 = jnp.concatenate([jnp.broadcast_to(c, (8, D)),
                             jnp.pad(conv_w.reshape(8, 256), ((0, 0), (0, D - 256)))], axis=0)
    g1 = _exchange("gather_cond", [pack1], "xyc", False)[0]
    c_all = g1[:, 0, :]
    conv_w_full = jnp.transpose(g1[0::2, 8:16, 0:256], (1, 0, 2)).reshape(2, 4, D)

    b_cols = lax.dynamic_slice(b_mod, (0, chip * mcols), (2, mcols)).reshape(2, 1, mcols)
    mod_loc = _mod_fwd(c_all, w_mod, b_cols)
    g2 = _exchange("gather_mod", [mod_loc.reshape(16, mcols)], "xyc", False)[0]
    mod_full = jnp.transpose(g2[0::2], (1, 0, 2)).reshape(2, 8, 3 * D)
    mod_me = lax.dynamic_index_in_dim(mod_full, dev, axis=1, keepdims=False)

    wb_in = _cast("cast_w_in", w_in.reshape(2 * D, 2304), 256).reshape(2, D, 2304)
    late_src = [wb_in[1], _cast("cast_w_pa", w_pa.reshape(2 * ATT_W, 256), 256).reshape(2, ATT_W, 256),
                _cast("cast_w_pb", w_pb.reshape(512, D), 256).reshape(2, 256, D),
                _cast("cast_w_o", w_o.reshape(512, D), 256).reshape(2, 256, D)]
    late_chunks = [4, 2, 2, 2]
    w_in0 = _gather_weights([wb_in[0].reshape(2, D // 2, 2304)], [2])[0].reshape(N_CHIPS, D, 2304)
    chip1 = jnp.reshape(chip, (1,)).astype(jnp.int32)
    lands = [_own_slot("own_slot_" + k, a, chip1, 256) for k, a in zip(("w_in", "w_pa", "w_pb", "w_o"), late_src)]
    send_sems, recv_sems, src_thru, land_thru, token = _late_gather_start(
        late_src, lands, late_chunks, [w_in0, mod_me])

    def late(after):
        got = _late_gather_wait(send_sems, recv_sems, src_thru, land_thru, late_chunks, after)
        return dict(w_in1=got[0], w_pa=got[1], w_pb=got[2], w_o=got[3])

    small_p = []
    for l in range(2):
        gates = _gate_tiles(w_rg[l], w_ig[l]).astype(_MXU)
        small_p.append(dict(
            shift=mod_me[l:l + 1, 0:D], scale=mod_me[l:l + 1, D:2 * D], gate=mod_me[l:l + 1, 2 * D:3 * D],
            g_pre=g_pre[l:l + 1], conv_w=conv_w_full[l], conv_b=conv_b[l:l + 1], wt=gates,
            b_rg=b_rg[l:l + 1], b_ig=b_ig[l:l + 1], lam=lru_lambda[l:l + 1], g_post=g_post[l:l + 1]))

    small_p[0]["shift"] = small_p[0]["shift"] + token[0, 0]

    loss_loc, dx, smalls, big = _local_step(x[0], loss_target[0], small_p, w_in0, late)
    loss = lax.psum(loss_loc, ("x", "y", "c"))
    grad_x = dx[None]

    names = ("w_in", "w_pa", "w_pb", "w_o")
    core = jnp.reshape(ci, (1,)).astype(jnp.int32)
    where = jnp.stack([chip, ci]).astype(jnp.int32)
    pair = list(_exchange("reduce_pair", [big["w_in"].reshape(2, 16, 256, 2304)] + [big[k] for k in names[1:]],
                          "c", True, local=False, nchunks=[16, 4, 4, 4]))
    pair[0] = pair[0].reshape(N_CHIPS, D, 2304)
    t1 = [_sum_pair("sum_pair_" + k, big[k], r, core, 128) for k, r in zip(names, pair)]
    quad = _exchange("reduce_chips", [t[1] for t in t1], "xy", True, local=False, nchunks=[4, 1, 1, 1])
    t3 = [_sum_chips("sum_chips_" + k, t[0], r, where, 128) for k, t, r in zip(names, t1, quad)]
    both = _pair_fill("gather_layers", t3, [4, 1, 1, 1])
    g_big = dict(zip(names, both))

    g3 = _exchange("gather_small", [_pack_small(smalls)], "xyc", False)[0]
    tot = _sum_lead("sum_small", g3, 96)
    dmod_all = g3[:, 0:6, :].reshape(8, 2, 3 * D)
    dm_cols = jnp.transpose(lax.dynamic_slice(dmod_all, (0, 0, chip * mcols), (8, 2, mcols)), (1, 0, 2))
    g_w_mod = _mod_bwd(jnp.transpose(c_all), dm_cols)
    vec = tot[8:20].reshape(6, 2, D)
    g_conv_w_full = tot[24:32].reshape(2, 4, D)
    grads = dict(
        w_mod=g_w_mod, b_mod=tot[0:6].reshape(2, 3 * D), g_pre=vec[0], w_in=g_big["w_in"],
        conv_w=lax.dynamic_slice(g_conv_w_full, (0, 0, chip * 256), (2, 4, 256)), conv_b=vec[1],
        w_rg=tot[32:160].reshape(2, 16, 64, 64), b_rg=vec[2], w_ig=tot[160:288].reshape(2, 16, 64, 64),
        b_ig=vec[3], lru_lambda=vec[4], w_pa=g_big["w_pa"], w_pb=g_big["w_pb"], w_o=g_big["w_o"],
        g_post=vec[5])

    weights = dict(w_mod=w_mod, b_mod=b_mod, g_pre=g_pre, w_in=w_in, conv_w=conv_w, conv_b=conv_b, w_rg=w_rg,
                   b_rg=b_rg, w_ig=w_ig, b_ig=b_ig, lru_lambda=lru_lambda, w_pa=w_pa, w_pb=w_pb, w_o=w_o,
                   g_post=g_post)
    ms = dict(w_mod=m_w_mod, b_mod=m_b_mod, g_pre=m_g_pre, w_in=m_w_in, conv_w=m_conv_w, conv_b=m_conv_b,
              w_rg=m_w_rg, b_rg=m_b_rg, w_ig=m_w_ig, b_ig=m_b_ig, lru_lambda=m_lru_lambda, w_pa=m_w_pa,
              w_pb=m_w_pb, w_o=m_w_o, g_post=m_g_post)
    vs = dict(w_mod=v_w_mod, b_mod=v_b_mod, g_pre=v_g_pre, w_in=v_w_in, conv_w=v_conv_w, conv_b=v_conv_b,
              w_rg=v_w_rg, b_rg=v_b_rg, w_ig=v_w_ig, b_ig=v_b_ig, lru_lambda=v_lru_lambda, w_pa=v_w_pa,
              w_pb=v_w_pb, w_o=v_w_o, g_post=v_g_post)
    flat = dict(w_mod=(2 * D, mcols, 256), b_mod=(2, 3 * D, 2), g_pre=(2, D, 2), w_in=(2 * D, 2304, 256),
                conv_w=(8, 256, 8), conv_b=(2, D, 2), w_rg=(128, D, 128), b_rg=(2, D, 2), w_ig=(128, D, 128),
                b_ig=(2, D, 2), lru_lambda=(2, D, 2), w_pa=(2 * ATT_W, 256, 256), w_pb=(512, D, 256),
                w_o=(512, D, 256), g_post=(2, D, 2))
    order = ("w_mod", "b_mod", "g_pre", "w_in", "conv_w", "conv_b", "w_rg", "b_rg", "w_ig", "b_ig",
             "lru_lambda", "w_pa", "w_pb", "w_o", "g_post")
    deltas, new_m, new_v = [], [], []
    for k in order:
        rows, cols, tb = flat[k]
        shp = weights[k].shape
        d, nm_, nv_ = _adamw("adamw_" + k, weights[k].reshape(rows, cols), grads[k].reshape(rows, cols),
                             ms[k].reshape(rows, cols), vs[k].reshape(rows, cols), tb)
        deltas.append(d.reshape(shp))
        new_m.append(nm_.reshape(shp))
        new_v.append(nv_.reshape(shp))
    return (loss, grad_x, *[grads[k].reshape(weights[k].shape) for k in order], *deltas, *new_m, *new_v)
```

```python
import functools

import jax
import jax.numpy as jnp
from jax import lax
from jax.experimental import pallas as pl
from jax.experimental.pallas import tpu as pltpu

_F32 = jnp.float32
_MXU = jnp.bfloat16
_VMEM_LIMIT = 56 * 1024 * 1024
_MESH = pl.DeviceIdType.MESH

D = 1024
HEAD = 128
HEADS = 4
ATT_W = 512
QKV_W = 1536
IN_W = 9216
DILATIONS = (1, 4, 16)
BAND = 128
QBLK = BAND * 16
NORM_EPS = 1e-6
NEG_INF = -1e30
LRU_C = 8.0
N_CHIPS = 4
CB_GATT = 4608 // 512
CB_U, CB_GLRU, CB_MA, CB_MB = 5, 6, 7, 8
R_U, R_GLRU, R_MA, R_MB, R_END = 512, 1536, 2560, 3584, 4608

ADAM_LR, ADAM_B1, ADAM_B2, ADAM_EPS, ADAM_WD, ADAM_STEP = 0.001, 0.9, 0.999, 1e-08, 0.01, 10


def _params(ngrid):
    return pltpu.CompilerParams(dimension_semantics=("arbitrary",) * ngrid, vmem_limit_bytes=_VMEM_LIMIT)


def _sigmoid(v):
    return 0.5 * jnp.tanh(0.5 * v) + 0.5


_GROUPS = {
    "c": [(0, 0, 1)],
    "xy": [(1, 0, 0), (0, 1, 0), (1, 1, 0)],
    "xyc": [(0, 0, 1), (0, 1, 0), (0, 1, 1), (1, 0, 0), (1, 0, 1), (1, 1, 0), (1, 1, 1)],
}


def _rank(group, px, py, pc):
    if group == "c":
        return pc
    if group == "xy":
        return 2 * px + py
    return 4 * px + 2 * py + pc


def _flip(rel, x, y, c):
    dx, dy, dc = rel
    return (1 - x if dx else x, 1 - y if dy else y, 1 - c if dc else c)


def _pieces(ref, nchunk):
    step = ref.shape[0] // nchunk
    return [ref.at[pl.ds(q * step, step)] for q in range(nchunk)]


def _exchange(name, srcs, group, scatter, *, local=True, nchunks=None):
    rels = _GROUPS[group]
    gsize = len(rels) + 1
    n = len(srcs)
    nchunks = nchunks or [1] * n
    blks = [s.shape[1:] if scatter else s.shape for s in srcs]
    slotted = local or gsize > 2
    base = [sum(nchunks[:a]) for a in range(n)]
    tot = sum(nchunks)

    def body(*refs):
        src_refs, out_refs = refs[:n], refs[n:2 * n]
        send_sems, recv_sems, loc_sems = refs[2 * n:]
        x, y, c = lax.axis_index("x"), lax.axis_index("y"), lax.axis_index("c")
        me = _rank(group, x, y, c)
        copies = []
        for a in range(n):
            def part(r, a=a):
                return src_refs[a].at[r] if scatter else src_refs[a]
            dst = out_refs[a].at[me] if slotted else out_refs[a]
            if local:
                for q, (s_, d_) in enumerate(zip(_pieces(part(me), nchunks[a]), _pieces(dst, nchunks[a]))):
                    loc = pltpu.make_async_copy(s_, d_, loc_sems.at[base[a] + q])
                    loc.start()
                    copies.append(loc)
            for k, rel in enumerate(rels):
                peer = _flip(rel, x, y, c)
                for q, (s_, d_) in enumerate(zip(_pieces(part(_rank(group, *peer)), nchunks[a]),
                                                 _pieces(dst, nchunks[a]))):
                    cp = pltpu.make_async_remote_copy(
                        src_ref=s_, dst_ref=d_, send_sem=send_sems.at[(base[a] + q) * len(rels) + k],
                        recv_sem=recv_sems.at[(base[a] + q) * len(rels) + k],
                        device_id=peer, device_id_type=_MESH)
                    cp.start()
                    copies.append(cp)
        for cp in copies:
            cp.wait()

    any_spec = pl.BlockSpec(memory_space=pl.ANY)
    lead = (gsize,) if slotted else ()
    return pl.pallas_call(
        body, name=name,
        out_shape=[jax.ShapeDtypeStruct(lead + tuple(b), s.dtype) for b, s in zip(blks, srcs)],
        in_specs=[any_spec] * n, out_specs=[any_spec] * n,
        scratch_shapes=[pltpu.SemaphoreType.DMA((tot * len(rels),)), pltpu.SemaphoreType.DMA((tot * len(rels),)),
                        pltpu.SemaphoreType.DMA((tot,))],
    )(*srcs)


def _pair_fill(name, arrs, nchunks):
    n = len(arrs)
    base = [sum(nchunks[:a]) for a in range(n)]
    tot = sum(nchunks)

    def body(*refs):
        out_refs = refs[n:2 * n]
        send_sems, recv_sems = refs[2 * n:]
        x, y, c = lax.axis_index("x"), lax.axis_index("y"), lax.axis_index("c")
        copies = []
        for a in range(n):
            for q, blk in enumerate(_pieces(out_refs[a].at[c], nchunks[a])):
                cp = pltpu.make_async_remote_copy(
                    src_ref=blk, dst_ref=blk, send_sem=send_sems.at[base[a] + q], recv_sem=recv_sems.at[base[a] + q],
                    device_id=(x, y, 1 - c), device_id_type=_MESH)
                cp.start()
                copies.append(cp)
        for cp in copies:
            cp.wait()

    any_spec = pl.BlockSpec(memory_space=pl.ANY)
    return pl.pallas_call(
        body, name=name, out_shape=[jax.ShapeDtypeStruct(a.shape, a.dtype) for a in arrs],
        in_specs=[any_spec] * n, out_specs=[any_spec] * n, input_output_aliases={a: a for a in range(n)},
        scratch_shapes=[pltpu.SemaphoreType.DMA((tot,)), pltpu.SemaphoreType.DMA((tot,))],
    )(*arrs)


def _gather_weights(wb, nchunks):
    n = len(wb)
    rels = _GROUPS["xy"]
    base = [sum(nchunks[:a]) for a in range(n)]
    tot = sum(nchunks)

    def body(*refs):
        src_refs, out_refs = refs[:n], refs[n:2 * n]
        ici_send, ici_recv, d2d_send, d2d_recv, loc_sems = refs[2 * n:]
        x, y, c = lax.axis_index("x"), lax.axis_index("y"), lax.axis_index("c")
        me = 2 * x + y
        waits = []
        for a in range(n):
            for l in range(2):
                for q, (s_, d_) in enumerate(zip(_pieces(src_refs[a].at[l], nchunks[a]),
                                                 _pieces(out_refs[a].at[me, l], nchunks[a]))):
                    loc = pltpu.make_async_copy(s_, d_, loc_sems.at[(base[a] + q) * 2 + l])
                    loc.start()
                    waits.append(loc)
        first = []
        for a in range(n):
            for k, rel in enumerate(rels):
                px, py, _ = _flip(rel, x, y, c)
                for q, (s_, d_) in enumerate(zip(_pieces(src_refs[a].at[c], nchunks[a]),
                                                 _pieces(out_refs[a].at[me, c], nchunks[a]))):
                    sem = (base[a] + q) * 3 + k
                    cp = pltpu.make_async_remote_copy(src_ref=s_, dst_ref=d_, send_sem=ici_send.at[sem],
                                                      recv_sem=ici_recv.at[sem], device_id=(px, py, c),
                                                      device_id_type=_MESH)
                    cp.start()
                    first.append(cp)
        second = []
        for a in range(n):
            for k, rel in enumerate(rels):
                px, py, _ = _flip(rel, x, y, c)
                for q, blk in enumerate(_pieces(out_refs[a].at[2 * px + py, c], nchunks[a])):
                    sem = (base[a] + q) * 3 + k
                    landed = pltpu.make_async_remote_copy(src_ref=blk, dst_ref=blk, send_sem=ici_send.at[sem],
                                                          recv_sem=ici_recv.at[sem], device_id=(px, py, c),
                                                          device_id_type=_MESH)
                    landed.wait_recv()
                    cp = pltpu.make_async_remote_copy(src_ref=blk, dst_ref=blk, send_sem=d2d_send.at[sem],
                                                      recv_sem=d2d_recv.at[sem], device_id=(x, y, 1 - c),
                                                      device_id_type=_MESH)
                    cp.start()
                    second.append(cp)
        for cp in first:
            cp.wait_send()
        for cp in second:
            cp.wait_send()
        for a in range(n):
            for k, rel in enumerate(rels):
                px, py, _ = _flip(rel, x, y, c)
                for q, blk in enumerate(_pieces(out_refs[a].at[2 * px + py, 1 - c], nchunks[a])):
                    sem = (base[a] + q) * 3 + k
                    pltpu.make_async_remote_copy(src_ref=blk, dst_ref=blk, send_sem=d2d_send.at[sem],
                                                 recv_sem=d2d_recv.at[sem], device_id=(x, y, 1 - c),
                                                 device_id_type=_MESH).wait_recv()
        for cp in waits:
            cp.wait()

    any_spec = pl.BlockSpec(memory_space=pl.ANY)
    return pl.pallas_call(
        body, name="gather_weights",
        out_shape=[jax.ShapeDtypeStruct((N_CHIPS,) + a.shape, a.dtype) for a in wb],
        in_specs=[any_spec] * n, out_specs=[any_spec] * n,
        scratch_shapes=[pltpu.SemaphoreType.DMA((tot * 3,))] * 4 + [pltpu.SemaphoreType.DMA((tot * 2,))],
    )(*wb)


_HBM = pl.BlockSpec(memory_space=pltpu.HBM)
_SEM = pl.BlockSpec(memory_space=pltpu.SEMAPHORE)
_EFFECT = pltpu.SideEffectType.DATAFLOW_SIDE_EFFECTING


def _own_slot(name, src, chip, tb):
    rows, cols = src.shape[-2:]
    lead = src.shape[:-2]
    flat = src.reshape((-1, cols))

    def body(s_ref, a_ref, o_ref):
        o_ref[...] = a_ref[...]

    grid_spec = pltpu.PrefetchScalarGridSpec(
        num_scalar_prefetch=1, grid=(flat.shape[0] // tb,),
        in_specs=[pl.BlockSpec((tb, cols), lambda i, s: (i, 0))],
        out_specs=pl.BlockSpec((None, tb, cols), lambda i, s: (s[0], i, 0)))
    out = pl.pallas_call(body, name=name, grid_spec=grid_spec,
                         out_shape=jax.ShapeDtypeStruct((N_CHIPS,) + flat.shape, src.dtype),
                         compiler_params=_params(1))(chip, flat)
    return out.reshape((N_CHIPS,) + lead + (rows, cols))


def _numbered(pairs, peer, send_sems, recv_sems, first):
    return [pltpu.make_async_remote_copy(src_ref=s_, dst_ref=d_, send_sem=send_sems.at[first + q],
                                         recv_sem=recv_sems.at[first + q], device_id=peer, device_id_type=_MESH)
            for q, (s_, d_) in enumerate(pairs)]


def _gather_plan(n, nchunks):
    def plan(refs, send_sems, recv_sems):
        x, y, c = lax.axis_index("x"), lax.axis_index("y"), lax.axis_index("c")
        me = 2 * x + y
        copies = []
        for a in range(n):
            for rel in _GROUPS["xy"]:
                px, py, _ = _flip(rel, x, y, c)
                pairs = list(zip(_pieces(refs[a], nchunks[a]), _pieces(refs[n + a].at[me], nchunks[a])))
                copies += _numbered(pairs, (px, py, c), send_sems, recv_sems, len(copies))
        return copies
    return plan, 3 * sum(nchunks)


def _pair_plan(n, nchunks):
    def plan(refs, send_sems, recv_sems):
        x, y, c = lax.axis_index("x"), lax.axis_index("y"), lax.axis_index("c")
        copies = []
        for a in range(n):
            for j in range(N_CHIPS):
                pairs = list(zip(_pieces(refs[a].at[j, 1 - c], nchunks[a]), _pieces(refs[n + a].at[j], nchunks[a])))
                copies += _numbered(pairs, (x, y, 1 - c), send_sems, recv_sems, len(copies))
        return copies
    return plan, N_CHIPS * sum(nchunks)


def _chips_plan(n, nchunks):
    def plan(refs, send_sems, recv_sems):
        x, y, c = lax.axis_index("x"), lax.axis_index("y"), lax.axis_index("c")
        me = 2 * x + y
        copies = []
        for a in range(n):
            for rel in _GROUPS["xy"]:
                px, py, _ = _flip(rel, x, y, c)
                pairs = list(zip(_pieces(refs[a].at[2 * px + py], nchunks[a]), _pieces(refs[n + a].at[me], nchunks[a])))
                copies += _numbered(pairs, (px, py, c), send_sems, recv_sems, len(copies))
        return copies
    return plan, 3 * sum(nchunks)


def _fill_plan(n, nchunks, l):
    def plan(refs, send_sems, recv_sems):
        x, y, c = lax.axis_index("x"), lax.axis_index("y"), lax.axis_index("c")
        copies = []
        for a in range(n):
            blk = _pieces(refs[a].at[l, c], nchunks[a])
            copies += _numbered(list(zip(blk, blk)), (x, y, 1 - c), send_sems, recv_sems, len(copies))
        return copies
    return plan, sum(nchunks)


def _split_start(name, arrays, plan, nsem, after):
    n = len(arrays)
    na = len(after)

    def body(*refs):
        send_sems, recv_sems = refs[n + na], refs[n + na + 1]
        token = refs[-1]
        for cp in plan(refs[:n], send_sems, recv_sems):
            cp.start()
        token[...] = jnp.zeros_like(token)

    hbm = [pltpu.HBM(a.shape, a.dtype) for a in arrays]
    outs = pl.pallas_call(
        body, name=name,
        out_shape=(pltpu.SemaphoreType.DMA((nsem,)), pltpu.SemaphoreType.DMA((nsem,)), *hbm, _sds((8, 128))),
        in_specs=[_HBM] * n + [pl.BlockSpec(memory_space=pl.ANY)] * na,
        out_specs=(_SEM, _SEM, *([_HBM] * n), pl.BlockSpec(memory_space=pltpu.VMEM)),
        input_output_aliases={i: 2 + i for i in range(n)},
        compiler_params=pltpu.CompilerParams(has_side_effects=_EFFECT),
    )(*[pltpu.with_memory_space_constraint(a, pltpu.HBM) for a in arrays], *after)
    return outs[0], outs[1], list(outs[2:2 + n]), outs[-1]


def _split_wait(name, send_sems, recv_sems, arrays, plan, after):
    n = len(arrays)

    def body(*refs):
        for cp in plan(refs[:n], refs[n], refs[n + 1]):
            cp.wait_send()
            cp.wait_recv()

    hbm = [pltpu.HBM(a.shape, a.dtype) for a in arrays]
    return list(pl.pallas_call(
        body, name=name, out_shape=tuple(hbm),
        in_specs=[_HBM] * n + [_SEM, _SEM] + [pl.BlockSpec(memory_space=pl.ANY)] * len(after),
        out_specs=tuple([_HBM] * n), input_output_aliases={i: i for i in range(n)},
        compiler_params=pltpu.CompilerParams(has_side_effects=_EFFECT),
    )(*arrays, send_sems, recv_sems, *after))


def _mm(name, a, b, out_sds, *, grid, a_spec, b_spec, o_spec, dims, acc_shape, into=None):
    nk = grid[2]

    def body(*refs):
        a_ref, b_ref = refs[0], refs[1]
        o_ref, acc = refs[-2], refs[-1]
        k = pl.program_id(2)
        part = lax.dot_general(a_ref[...].astype(_MXU), b_ref[...].astype(_MXU), dims,
                               preferred_element_type=_F32)
        if nk == 1:
            o_ref[...] = part.astype(o_ref.dtype)
            return

        @pl.when(k == 0)
        def _():
            acc[...] = part

        @pl.when(k > 0)
        def _():
            acc[...] += part

        @pl.when(k == nk - 1)
        def _():
            o_ref[...] = acc[...].astype(o_ref.dtype)

    if nk == 1:
        acc_shape = (8, 128)
    in_specs = [a_spec, b_spec]
    args = [a, b]
    aliases = {}
    if into is not None:
        in_specs.append(pl.BlockSpec(memory_space=pl.ANY))
        args.append(into)
        aliases = {2: 0}
    return pl.pallas_call(
        body, name=name, grid=grid, in_specs=in_specs, out_specs=o_spec, out_shape=out_sds,
        scratch_shapes=[pltpu.VMEM(acc_shape, _F32)], input_output_aliases=aliases,
        compiler_params=_params(3))(*args)


_NN = (((1,), (0,)), ((), ()))
_NT = (((1,), (1,)), ((), ()))
_TN = (((0,), (0,)), ((), ()))


def _rowwise(name, body, *, grid, ins, outs, scratch=()):
    return pl.pallas_call(
        body, name=name, grid=(grid,), in_specs=[s for _, s in ins], out_specs=[s for _, s in outs],
        out_shape=[o for o, _ in outs], scratch_shapes=list(scratch),
        compiler_params=_params(1))(*[a for a, _ in ins])


def _rows(tb, w, cb=0, n=None):
    if n is None:
        return pl.BlockSpec((tb, w), lambda i: (i, cb))
    return pl.BlockSpec((tb, w), lambda i: (n - 1 - i, cb))


def _vec(shape):
    return pl.BlockSpec(shape, lambda i: (0,) * len(shape))


def _halo_prev(tb, w, cb=0, n=None, rows=8):
    if n is None:
        return pl.BlockSpec((rows, w), lambda i: (jnp.maximum(i * (tb // rows) - 1, 0), cb))
    return pl.BlockSpec((rows, w), lambda i: (jnp.maximum((n - 1 - i) * (tb // rows) - 1, 0), cb))


def _halo_next(tb, w, n, cb=0):
    return pl.BlockSpec((8, w), lambda i: (jnp.minimum((i + 1) * (tb // 8), n * (tb // 8) - 1), cb))


def _sds(shape, dtype=_F32):
    return jax.ShapeDtypeStruct(shape, dtype)


def _cast(name, a, tb):
    rows, cols = a.shape

    def body(a_ref, o_ref):
        o_ref[...] = a_ref[...].astype(o_ref.dtype)

    return _rowwise(name, body, grid=rows // tb, ins=[(a, _rows(tb, cols))],
                    outs=[(_sds((rows, cols), _MXU), _rows(tb, cols))])[0]


def _sum_lead(name, a, tb):
    g, rows, cols = a.shape

    def body(a_ref, o_ref):
        acc = a_ref[0]
        for k in range(1, g):
            acc = acc + a_ref[k]
        o_ref[...] = acc

    return _rowwise(name, body, grid=rows // tb,
                    ins=[(a, pl.BlockSpec((g, tb, cols), lambda i: (0, i, 0)))],
                    outs=[(_sds((rows, cols)), _rows(tb, cols))])[0]


def _sum_pair(name, mine, theirs, core, tb):
    nj, _, rows, cols = mine.shape

    def body(s_ref, a_ref, b_ref, o_ref, ob_ref):
        t = a_ref[...] + b_ref[...]
        o_ref[...] = t
        ob_ref[...] = t.astype(ob_ref.dtype)

    blk = pl.BlockSpec((None, tb, cols), lambda j, i, s: (j, i, 0))
    grid_spec = pltpu.PrefetchScalarGridSpec(
        num_scalar_prefetch=1, grid=(nj, rows // tb),
        in_specs=[pl.BlockSpec((None, None, tb, cols), lambda j, i, s: (j, s[0], i, 0)), blk],
        out_specs=[blk, blk])
    return pl.pallas_call(body, name=name, grid_spec=grid_spec,
                          out_shape=[_sds((nj, rows, cols)), _sds((nj, rows, cols), _MXU)],
                          compiler_params=_params(2))(core, mine, theirs)


def _sum_chips(name, mine, theirs, where, l, into, tb):
    _, rows, cols = mine.shape
    extra = [] if into is None else [into]

    def body(*refs):
        a_ref, b1_ref, b2_ref, b3_ref = refs[1:5]
        o_ref = refs[-1]
        o_ref[...] = ((a_ref[...] + b1_ref[...].astype(_F32)) + b2_ref[...].astype(_F32)) + b3_ref[...].astype(_F32)

    def slot(k):
        return pl.BlockSpec((None, tb, cols), lambda i, s: (jnp.bitwise_xor(s[0], k), i, 0))

    grid_spec = pltpu.PrefetchScalarGridSpec(
        num_scalar_prefetch=1, grid=(rows // tb,),
        in_specs=[slot(0), slot(1), slot(2), slot(3)] + [pl.BlockSpec(memory_space=pl.ANY)] * len(extra),
        out_specs=pl.BlockSpec((None, None, tb, cols), lambda i, s: (l, s[1], i, 0)))
    return pl.pallas_call(body, name=name, grid_spec=grid_spec, out_shape=_sds((2, 2, rows, cols)),
                          input_output_aliases={5: 0} if extra else {},
                          compiler_params=_params(1))(where, mine, theirs, theirs, theirs, *extra)


def _adamw(name, w, g, m, v, tb):
    rows, cols = w.shape
    c1 = 1.0 - ADAM_B1 ** ADAM_STEP
    c2 = 1.0 - ADAM_B2 ** ADAM_STEP

    def body(w_ref, g_ref, m_ref, v_ref, d_ref, nm_ref, nv_ref):
        gv = g_ref[...]
        nm = ADAM_B1 * m_ref[...] + (1.0 - ADAM_B1) * gv
        nv = ADAM_B2 * v_ref[...] + (1.0 - ADAM_B2) * (gv * gv)
        d_ref[...] = -ADAM_LR * ((nm / c1) / (jnp.sqrt(nv / c2) + ADAM_EPS) + ADAM_WD * w_ref[...])
        nm_ref[...] = nm
        nv_ref[...] = nv

    spec = _rows(tb, cols)
    return _rowwise(name, body, grid=rows // tb, ins=[(w, spec), (g, spec), (m, spec), (v, spec)],
                    outs=[(_sds((rows, cols)), spec)] * 3)


def _mod_fwd(c_all, w_mod, b_cols):
    cols = w_mod.shape[2]

    def body(c_ref, w_ref, b_ref, o_ref):
        cv = c_ref[...]
        sc = (cv * _sigmoid(cv)).astype(_MXU)
        o_ref[...] = jnp.dot(sc, w_ref[...].astype(_MXU), preferred_element_type=_F32) + b_ref[...]

    return pl.pallas_call(
        body, name="mod_fwd", grid=(2,),
        in_specs=[pl.BlockSpec((8, D), lambda l: (0, 0)), pl.BlockSpec((None, D, cols), lambda l: (l, 0, 0)),
                  pl.BlockSpec((None, 1, cols), lambda l: (l, 0, 0))],
        out_specs=pl.BlockSpec((None, 8, cols), lambda l: (l, 0, 0)),
        out_shape=_sds((2, 8, cols)), compiler_params=_params(1))(c_all, w_mod, b_cols)


def _mod_bwd(c_all_t, dm):
    cols = dm.shape[2]

    def body(c_ref, d_ref, o_ref):
        cv = c_ref[...]
        sc = (cv * _sigmoid(cv)).astype(_MXU)
        o_ref[...] = jnp.dot(sc, d_ref[...].astype(_MXU), preferred_element_type=_F32)

    return pl.pallas_call(
        body, name="mod_bwd", grid=(2,),
        in_specs=[pl.BlockSpec((D, 8), lambda l: (0, 0)), pl.BlockSpec((None, 8, cols), lambda l: (l, 0, 0))],
        out_specs=pl.BlockSpec((None, D, cols), lambda l: (l, 0, 0)),
        out_shape=_sds((2, D, cols)), compiler_params=_params(1))(c_all_t, dm)


def _prenorm_fwd(x, g_pre, shift, scale):
    s = x.shape[0]
    tb = 512

    def body(x_ref, g_ref, sh_ref, sc_ref, h_ref, ht_ref):
        xv = x_ref[...]
        rstd = lax.rsqrt(jnp.mean(xv * xv, axis=-1, keepdims=True) + NORM_EPS)
        hv = (xv * rstd) * g_ref[...] * (1.0 + sc_ref[...]) + sh_ref[...]
        h_ref[...] = hv.astype(h_ref.dtype)
        ht_ref[...] = hv.T.astype(ht_ref.dtype)

    v = _vec((1, D))
    return _rowwise("prenorm_fwd", body, grid=s // tb,
                    ins=[(x, _rows(tb, D)), (g_pre, v), (shift, v), (scale, v)],
                    outs=[(_sds((s, D), _MXU), _rows(tb, D)),
                          (_sds((D, s), _MXU), pl.BlockSpec((D, tb), lambda i: (0, i)))])


def _shift_down(cur, halo, j, tb):
    ext = jnp.concatenate([halo, cur], axis=0)
    return pltpu.roll(ext, j, 0)[8:8 + tb]


def _shift_up(cur, halo, j, tb):
    ext = jnp.concatenate([cur, halo], axis=0)
    return pltpu.roll(ext, tb + 8 - j, 0)[0:tb]


def _conv_fwd(proj, conv_w, conv_b):
    s = proj.shape[0]
    tb = 512

    def body(u_ref, hp_ref, w_ref, b_ref, o_ref):
        i = pl.program_id(0)
        u = u_ref[...].astype(_F32)
        halo = jnp.where(i > 0, hp_ref[...].astype(_F32)[8:16], 0.0)
        acc = b_ref[...] + u * w_ref[0:1, :]
        for j in range(1, 4):
            acc = acc + _shift_down(u, halo, j, tb) * w_ref[j:j + 1, :]
        o_ref[...] = acc

    return _rowwise("conv_fwd", body, grid=s // tb,
                    ins=[(proj, _rows(tb, D, CB_U)), (proj, _halo_prev(tb, D, CB_U, rows=16)),
                         (conv_w, _vec((4, D))), (conv_b, _vec((1, D)))],
                    outs=[(_sds((s, D)), _rows(tb, D))])[0]


def _lru_gates(pre_r, pre_i, uc, b_rg, b_ig, lam):
    r = _sigmoid(pre_r + b_rg)
    ig = _sigmoid(pre_i + b_ig)
    nl = -lam
    sp = jnp.maximum(nl, 0.0) + jnp.log(1.0 + jnp.exp(-jnp.abs(nl)))
    la = -LRU_C * r * sp
    a = jnp.exp(la)
    one_m_a2 = -jnp.tanh(la) * (a * a + 1.0)
    inv_sq = lax.rsqrt(jnp.maximum(one_m_a2, 1e-30))
    return r, ig, sp, a, one_m_a2 * inv_sq, inv_sq


GATE_TILES = 8


def _gate_tiles(w_rg, w_ig):
    eye = jnp.eye(2, dtype=w_rg.dtype)

    def tiles(w):
        return jnp.einsum("cpij,pq->cpiqj", w.reshape(GATE_TILES, 2, 64, 64), eye).reshape(GATE_TILES, 128, 128)

    return jnp.concatenate([tiles(w_rg), tiles(w_ig)], axis=2)


def _gate_tile_grads(gw):
    keep = jnp.eye(2, dtype=jnp.bool_)[None, :, None, :, None]

    def blocks(t):
        t5 = t.reshape(GATE_TILES, 2, 64, 2, 64)
        return jnp.sum(jnp.where(keep, t5, 0.0), axis=3).reshape(16, 64, 64)

    return blocks(gw[:, :, 0:128]), blocks(gw[:, :, 128:256])


def _gate_preacts(ucv, wt_ref):
    ucb = ucv.astype(_MXU)
    ps = [jnp.dot(ucb[:, 128 * c:128 * (c + 1)], wt_ref[c], preferred_element_type=_F32) for c in range(GATE_TILES)]
    pre_r = jnp.concatenate([p[:, 0:128] for p in ps], axis=1)
    pre_i = jnp.concatenate([p[:, 128:256] for p in ps], axis=1)
    return pre_r, pre_i


def _scan_fwd(uc, wt, b_rg, b_ig, lam):
    s = uc.shape[0]
    tb = 256

    def body(uc_ref, wt_ref, brg_ref, big_ref, lam_ref, h_ref, carry, a_s, b_s):
        i = pl.program_id(0)

        @pl.when(i == 0)
        def _():
            carry[...] = jnp.zeros_like(carry)

        ucv = uc_ref[...]
        pre_r, pre_i = _gate_preacts(ucv, wt_ref)
        _, ig, _, a, sq, _ = _lru_gates(pre_r, pre_i, ucv, brg_ref[...], big_ref[...], lam_ref[...])
        av = a
        bv = sq * (ig * ucv)
        av = av.reshape(tb // 8, 8, D)
        bv = bv.reshape(tb // 8, 8, D)
        row8 = lax.broadcasted_iota(jnp.int32, (1, 8, 1), 1)
        for sh in (1, 2, 4):
            m = row8 >= sh
            b_sh = pltpu.roll(bv, sh, 1)
            a_sh = pltpu.roll(av, sh, 1)
            bv = jnp.where(m, av * b_sh + bv, bv)
            av = jnp.where(m, av * a_sh, av)
        a_s[...] = av.reshape(tb, D)
        b_s[...] = bv.reshape(tb, D)

        def tile(t, state):
            rows = pl.ds(pl.multiple_of(t * 8, 8), 8)
            hv = b_s[rows, :] + a_s[rows, :] * state
            h_ref[rows, :] = hv
            return jnp.broadcast_to(hv[7:8, :], (8, D))

        carry[...] = lax.fori_loop(0, tb // 8, tile, jnp.broadcast_to(carry[7:8, :], (8, D)), unroll=4)

    v = _vec((1, D))
    return _rowwise("scan_fwd", body, grid=s // tb,
                    ins=[(uc, _rows(tb, D)), (wt, _vec((GATE_TILES, 128, 256))), (b_rg, v), (b_ig, v), (lam, v)],
                    outs=[(_sds((s, D)), _rows(tb, D))],
                    scratch=[pltpu.VMEM((8, D), _F32), pltpu.VMEM((tb, D), _F32), pltpu.VMEM((tb, D), _F32)])[0]


def _weight_specs(l):
    return [pl.BlockSpec((N_CHIPS, None, ATT_W, 256), lambda i: (0, l, 0, 0)),
            pl.BlockSpec((N_CHIPS, None, 256, D), lambda i: (0, l, 0, 0)),
            pl.BlockSpec((N_CHIPS, None, 256, D), lambda i: (0, l, 0, 0))]


def _tail_fwd(l, o, h_lru, proj, x, gate, g_post, gw):
    s = x.shape[0]
    tb = 512

    def body(o_ref, h_ref, ga_ref, gl_ref, ma_ref, mb_ref, x_ref, gt_ref, gp_ref, wpa_ref, wpb_ref, wo_ref,
             aa_ref, ba_ref, ya_ref, yb_ref, z_ref, out_ref, xn_ref):
        ga = ga_ref[...].astype(_F32)
        aa = (o_ref[...] * (ga * _sigmoid(ga))).astype(_MXU)
        aa_ref[...] = aa
        gl = gl_ref[...].astype(_F32)
        ba = (h_ref[...] * (gl * _sigmoid(gl))).astype(_MXU)
        ba_ref[...] = ba
        ya = jnp.concatenate([jnp.dot(aa, wpa_ref[j], preferred_element_type=_F32) for j in range(N_CHIPS)], axis=1)
        ya_ref[...] = ya.astype(ya_ref.dtype)
        yb = jnp.dot(ba, wpb_ref[...].reshape(D, D), preferred_element_type=_F32)
        yb_ref[...] = yb.astype(yb_ref.dtype)
        z = (_sigmoid(ma_ref[...].astype(_F32)) * ya
             + _sigmoid(mb_ref[...].astype(_F32)) * yb).astype(z_ref.dtype)
        z_ref[...] = z
        ov = jnp.dot(z, wo_ref[...].reshape(D, D), preferred_element_type=_F32)
        out_ref[...] = ov
        rstd = lax.rsqrt(jnp.mean(ov * ov, axis=-1, keepdims=True) + NORM_EPS)
        xn_ref[...] = x_ref[...] + gt_ref[...] * ((ov * rstd) * gp_ref[...])

    v = _vec((1, D))
    r = _rows(tb, D)
    r5 = _rows(tb, ATT_W)
    return _rowwise("tail_fwd", body, grid=s // tb,
                    ins=[(o, r5), (h_lru, r), (proj, _rows(tb, ATT_W, CB_GATT)), (proj, _rows(tb, D, CB_GLRU)),
                         (proj, _rows(tb, D, CB_MA)), (proj, _rows(tb, D, CB_MB)), (x, r), (gate, v), (g_post, v)]
                    + list(zip((gw["w_pa"], gw["w_pb"], gw["w_o"]), _weight_specs(l))),
                    outs=[(_sds((s, ATT_W), _MXU), r5), (_sds((s, D), _MXU), r), (_sds((s, D), _MXU), r),
                          (_sds((s, D), _MXU), r), (_sds((s, D), _MXU), r), (_sds((s, D)), r), (_sds((s, D)), r)])


def _loss_head(y, target):
    s = y.shape[0]
    tb = 512

    def body(y_ref, t_ref, dy_ref, acc_ref):
        i = pl.program_id(0)

        @pl.when(i == 0)
        def _():
            acc_ref[...] = jnp.zeros_like(acc_ref)

        err = y_ref[...] - t_ref[...]
        dy_ref[...] = err * (1.0 / D)
        acc_ref[...] += jnp.sum(err * err, axis=0, keepdims=True)

    return _rowwise("loss_head", body, grid=s // tb,
                    ins=[(y, _rows(tb, D)), (target, _rows(tb, D))],
                    outs=[(_sds((s, D)), _rows(tb, D)), (_sds((1, D)), _vec((1, D)))])


def _zero_first(i, *refs):
    @pl.when(i == 0)
    def _():
        for ref in refs:
            ref[...] = jnp.zeros_like(ref)


def _tail_bwd(l, dx, out, y_a, y_b, proj, o, h_lru, gate, g_post, gw):
    s = dx.shape[0]
    tb = 256

    def body(dx_ref, out_ref, ya_ref, yb_ref, ma_ref, mb_ref, o_ref, ga_ref, h_ref, gl_ref, gt_ref, gp_ref,
             wpa_ref, wpb_ref, wo_ref,
             dout_ref, dya_ref, dyb_ref, rest_ref, do_ref, dh_ref, dgt_ref, dgp_ref):
        i = pl.program_id(0)
        ov = out_ref[...]
        dxv = dx_ref[...]
        rstd = lax.rsqrt(jnp.mean(ov * ov, axis=-1, keepdims=True) + NORM_EPS)
        nv = ov * rstd
        s_dn = jnp.sum(dxv * nv, axis=0, keepdims=True)
        _zero_first(i, dgt_ref, dgp_ref)
        dgt_ref[...] += s_dn * gp_ref[...]
        dgp_ref[...] += s_dn * gt_ref[...]
        dn = dxv * (gt_ref[...] * gp_ref[...])
        d_out = (rstd * (dn - nv * jnp.mean(dn * nv, axis=-1, keepdims=True))).astype(_MXU)
        dout_ref[...] = d_out
        dz = lax.dot_general(d_out, wo_ref[...].reshape(D, D), _NT, preferred_element_type=_F32)
        ga = _sigmoid(ma_ref[...].astype(_F32))
        gb = _sigmoid(mb_ref[...].astype(_F32))
        dya = (dz * ga).astype(_MXU)
        dyb = (dz * gb).astype(_MXU)
        dya_ref[...] = dya
        dyb_ref[...] = dyb
        rest_ref[:, R_MA:R_MB] = (dz * ya_ref[...].astype(_F32) * ga * (1.0 - ga)).astype(rest_ref.dtype)
        rest_ref[:, R_MB:R_END] = (dz * yb_ref[...].astype(_F32) * gb * (1.0 - gb)).astype(rest_ref.dtype)
        daa = lax.dot_general(dya[:, 0:256], wpa_ref[0], _NT, preferred_element_type=_F32)
        for j in range(1, N_CHIPS):
            daa = daa + lax.dot_general(dya[:, j * 256:(j + 1) * 256], wpa_ref[j], _NT, preferred_element_type=_F32)
        dba = lax.dot_general(dyb, wpb_ref[...].reshape(D, D), _NT, preferred_element_type=_F32)
        gav = ga_ref[...].astype(_F32)
        sa = _sigmoid(gav)
        do_ref[...] = daa * (gav * sa)
        rest_ref[:, 0:R_U] = (daa * o_ref[...] * (sa * (1.0 + gav * (1.0 - sa)))).astype(rest_ref.dtype)
        gl = gl_ref[...].astype(_F32)
        sl = _sigmoid(gl)
        dh_ref[...] = dba * (gl * sl)
        rest_ref[:, R_GLRU:R_MA] = (dba * h_ref[...] * (sl * (1.0 + gl * (1.0 - sl)))).astype(rest_ref.dtype)

    v = _vec((1, D))
    r5, r10 = _rows(tb, ATT_W), _rows(tb, D)
    return _rowwise("tail_bwd", body, grid=s // tb,
                    ins=[(dx, r10), (out, r10), (y_a, r10), (y_b, r10), (proj, _rows(tb, D, CB_MA)),
                         (proj, _rows(tb, D, CB_MB)), (o, r5), (proj, _rows(tb, ATT_W, CB_GATT)), (h_lru, r10),
                         (proj, _rows(tb, D, CB_GLRU)), (gate, v), (g_post, v)]
                    + list(zip((gw["w_pa"], gw["w_pb"], gw["w_o"]), _weight_specs(l))),
                    outs=[(_sds((s, D), _MXU), r10), (_sds((s, D), _MXU), r10), (_sds((s, D), _MXU), r10),
                          (_sds((s, R_END), _MXU), _rows(tb, R_END)),
                          (_sds((s, ATT_W)), r5), (_sds((s, D)), r10), (_sds((1, D)), v), (_sds((1, D)), v)])


def _scan_bwd(dh, uc, h_lru, wt, b_rg, b_ig, lam):
    s = uc.shape[0]
    tb = 256
    n = s // tb

    def body(dh_ref, uc_ref, h_ref, hp_ref, wt_ref, brg_ref, big_ref, lam_ref,
             duc_ref, dwt_ref, dbrg_ref, dbig_ref, dlam_ref, carry, c_s, g_s):
        i = pl.program_id(0)

        @pl.when(i == 0)
        def _():
            carry[...] = jnp.zeros_like(carry)
            for acc_ref in (dwt_ref, dbrg_ref, dbig_ref, dlam_ref):
                acc_ref[...] = jnp.zeros_like(acc_ref)

        ucv = uc_ref[...]
        pre_r, pre_i = _gate_preacts(ucv, wt_ref)
        r, ig, sp, a, sq, inv_sq =_lru_gates(pre_r, pre_i, ucv, brg_ref[...], big_ref[...], lam_ref[...])
        row = lax.broadcasted_iota(jnp.int32, (tb, 1), 0)
        cv = jnp.where(row == tb - 1, 1.0, pltpu.roll(a, tb - 1, 0))
        gv = dh_ref[...]
        cv = cv.reshape(tb // 8, 8, D)
        gv = gv.reshape(tb // 8, 8, D)
        row8 = lax.broadcasted_iota(jnp.int32, (1, 8, 1), 1)
        for sh in (1, 2, 4):
            m = row8 < 8 - sh
            g_sh = pltpu.roll(gv, 8 - sh, 1)
            c_sh = pltpu.roll(cv, 8 - sh, 1)
            gv = jnp.where(m, gv + cv * g_sh, gv)
            cv = jnp.where(m, cv * c_sh, cv)
        c_s[...] = cv.reshape(tb, D)
        g_s[...] = gv.reshape(tb, D)

        def tile(k, state):
            rows = pl.ds(pl.multiple_of((tb // 8 - 1 - k) * 8, 8), 8)
            gt = g_s[rows, :] + c_s[rows, :] * state
            g_s[rows, :] = gt
            return jnp.broadcast_to(gt[0:1, :], (8, D))

        lax.fori_loop(0, tb // 8, tile, jnp.broadcast_to(carry[0:1, :], (8, D)), unroll=4)
        gv = g_s[...]
        carry[...] = (a * gv)[0:8]

        halo = jnp.where(i < n - 1, hp_ref[...], 0.0)
        h_prev = _shift_down(h_ref[...], halo, 1, tb)
        d_a = gv * h_prev
        d_sq = gv * (ig * ucv)
        d_i = gv * sq * ucv
        d_la = d_a * a - d_sq * (a * a) * inv_sq
        d_r = d_la * (-LRU_C * sp)
        d_pre_r = d_r * r * (1.0 - r)
        d_pre_i = d_i * ig * (1.0 - ig)
        ucb = ucv.astype(_MXU)
        dpr = d_pre_r.astype(_MXU)
        dpi = d_pre_i.astype(_MXU)
        back = []
        for c in range(GATE_TILES):
            lanes = slice(128 * c, 128 * (c + 1))
            dp = jnp.concatenate([dpr[:, lanes], dpi[:, lanes]], axis=1)
            back.append(lax.dot_general(dp, wt_ref[c], _NT, preferred_element_type=_F32))
            dwt_ref[c] += lax.dot_general(ucb[:, lanes], dp, _TN, preferred_element_type=_F32)
        duc_ref[...] = gv * sq * ig + jnp.concatenate(back, axis=1)
        dbrg_ref[...] += jnp.sum(d_pre_r, axis=0, keepdims=True)
        dbig_ref[...] += jnp.sum(d_pre_i, axis=0, keepdims=True)
        lamv = lam_ref[...]
        dlam_ref[...] += jnp.sum(d_la * (-LRU_C * r), axis=0, keepdims=True) * (-_sigmoid(-lamv))

    v = _vec((1, D))
    rv = _rows(tb, D, 0, n)
    return _rowwise("scan_bwd", body, grid=n,
                    ins=[(dh, rv), (uc, rv), (h_lru, rv), (h_lru, _halo_prev(tb, D, 0, n)),
                         (wt, _vec((GATE_TILES, 128, 256))), (b_rg, v), (b_ig, v), (lam, v)],
                    outs=[(_sds((s, D)), rv), (_sds((GATE_TILES, 128, 256)), _vec((GATE_TILES, 128, 256))),
                          (_sds((1, D)), v), (_sds((1, D)), v), (_sds((1, D)), v)],
                    scratch=[pltpu.VMEM((8, D), _F32), pltpu.VMEM((tb, D), _F32), pltpu.VMEM((tb, D), _F32)])


def _conv_bwd(duc_a, proj, conv_w, rest):
    s = duc_a.shape[0]
    tb = 512
    n = s // tb
    hw = D // 2

    def body(da_ref, dan_ref, u_ref, up_ref, w_ref, rest_in, du_ref, dw_ref, dbias_ref):
        i = pl.program_id(1)
        duc = da_ref[...]
        nxt = jnp.where(i < n - 1, dan_ref[...], 0.0)
        u = u_ref[...].astype(_F32)
        halo = jnp.where(i > 0, up_ref[...].astype(_F32)[8:16], 0.0)
        du = duc * w_ref[0:1, :]
        dws = [jnp.sum(duc * u, axis=0, keepdims=True)]
        for j in range(1, 4):
            du = du + _shift_up(duc, nxt, j, tb) * w_ref[j:j + 1, :]
            dws.append(jnp.sum(duc * _shift_down(u, halo, j, tb), axis=0, keepdims=True))
        du_ref[...] = du.astype(du_ref.dtype)
        _zero_first(i, dw_ref, dbias_ref)
        for j in range(4):
            dw_ref[j:j + 1, :] += dws[j]
        dbias_ref[...] += jnp.sum(duc, axis=0, keepdims=True)

    r = pl.BlockSpec((tb, hw), lambda h, i: (i, h))
    nxt_spec = pl.BlockSpec((8, hw), lambda h, i: (jnp.minimum((i + 1) * (tb // 8), n * (tb // 8) - 1), h))
    return pl.pallas_call(
        body, name="conv_bwd", grid=(2, n),
        in_specs=[r, nxt_spec,
                  pl.BlockSpec((tb, hw), lambda h, i: (i, 2 * CB_U + h)),
                  pl.BlockSpec((16, hw), lambda h, i: (jnp.maximum(i * (tb // 16) - 1, 0), 2 * CB_U + h)),
                  pl.BlockSpec((4, hw), lambda h, i: (0, h)), pl.BlockSpec(memory_space=pl.ANY)],
        out_specs=[pl.BlockSpec((tb, hw), lambda h, i: (i, R_U // hw + h)),
                   pl.BlockSpec((4, hw), lambda h, i: (0, h)), pl.BlockSpec((1, hw), lambda h, i: (0, h))],
        out_shape=[_sds(rest.shape, rest.dtype), _sds((4, D)), _sds((1, D))],
        input_output_aliases={5: 0}, compiler_params=_params(2),
    )(duc_a, duc_a, proj, proj, conv_w, rest)


def _prenorm_bwd(dh, x, dx_out, g_pre, scale):
    s = x.shape[0]
    tb = 512

    def body(dh_ref, x_ref, dxo_ref, g_ref, sc_ref, dx_ref, dsh_ref, dsc_ref, dg_ref):
        i = pl.program_id(0)
        xv = x_ref[...]
        dhv = dh_ref[...]
        rstd = lax.rsqrt(jnp.mean(xv * xv, axis=-1, keepdims=True) + NORM_EPS)
        xn = xv * rstd
        one_sc = 1.0 + sc_ref[...]
        s1 = jnp.sum(dhv * xn, axis=0, keepdims=True)
        _zero_first(i, dsh_ref, dsc_ref, dg_ref)
        dsh_ref[...] += jnp.sum(dhv, axis=0, keepdims=True)
        dsc_ref[...] += s1 * g_ref[...]
        dg_ref[...] += s1 * one_sc
        dxn = dhv * (g_ref[...] * one_sc)
        dx_ref[...] = dxo_ref[...] + rstd * (dxn - xn * jnp.mean(dxn * xn, axis=-1, keepdims=True))

    v = _vec((1, D))
    r = _rows(tb, D)
    return _rowwise("prenorm_bwd", body, grid=s // tb,
                    ins=[(dh, r), (x, r), (dx_out, r), (g_pre, v), (scale, v)],
                    outs=[(_sds((s, D)), r), (_sds((1, D)), v), (_sds((1, D)), v), (_sds((1, D)), v)])


def _band_tiles(dil):
    tiles = []
    for rho in range(dil):
        for b in range(16 // dil):
            qs = rho + dil * BAND * b
            tiles.append((qs, QBLK + qs - dil * BAND, b))
    return tiles


def _strided(start, size, dil):
    return pl.ds(start, size, stride=dil) if dil > 1 else pl.ds(start, size)


def _band_mask(i, b):
    qi = lax.broadcasted_iota(jnp.int32, (BAND, 2 * BAND), 0)
    ki = lax.broadcasted_iota(jnp.int32, (BAND, 2 * BAND), 1)
    valid = (ki >= qi) & (ki <= qi + BAND)
    if b == 0:
        valid = valid & ((ki >= BAND) | (i > 0))
    return valid


def _attn_fwd(proj):
    s = proj.shape[0]
    n = s // QBLK
    scale = HEAD ** -0.5

    def body(*refs):
        q_refs, kp_refs, kc_refs, vp_refs, vc_refs = (refs[3 * t:3 * t + 3] for t in range(5))
        o_ref, lse_ref, qbuf, kbuf, vbuf = refs[15:20]
        accs, maxs, dens = refs[20:23], refs[23:26], refs[26:29]
        i = pl.program_id(1)
        for g, dil in enumerate(DILATIONS):
            qbuf[...] = q_refs[g][...].astype(_F32)
            kbuf[0:QBLK, :] = kp_refs[g][...].astype(_F32)
            kbuf[QBLK:2 * QBLK, :] = kc_refs[g][...].astype(_F32)
            vbuf[0:QBLK, :] = vp_refs[g][...].astype(_F32)
            vbuf[QBLK:2 * QBLK, :] = vc_refs[g][...].astype(_F32)
            for qs, ks, b in _band_tiles(dil):
                qsl = _strided(qs, BAND, dil)
                q = qbuf[qsl, :].astype(_MXU)
                kk = kbuf[_strided(ks, 2 * BAND, dil), :].astype(_MXU)
                vv = vbuf[_strided(ks, 2 * BAND, dil), :].astype(_MXU)
                sc = lax.dot_general(q, kk, _NT, preferred_element_type=_F32) * scale
                sc = jnp.where(_band_mask(i, b), sc, NEG_INF)
                m = jnp.max(sc, axis=-1, keepdims=True)
                p = jnp.exp(sc - m)
                accs[g][qsl, :] = jnp.dot(p.astype(_MXU), vv, preferred_element_type=_F32)
                maxs[g][qsl, :] = jnp.broadcast_to(m, (BAND, HEAD))
                dens[g][qsl, :] = jnp.broadcast_to(jnp.sum(p, axis=-1, keepdims=True), (BAND, HEAD))
        ms = [r[...] for r in maxs]
        mx = jnp.maximum(jnp.maximum(ms[0], ms[1]), ms[2])
        ws = [jnp.exp(m - mx) for m in ms]
        den = ws[0] * dens[0][...] + ws[1] * dens[1][...] + ws[2] * dens[2][...]
        o_ref[...] = (ws[0] * accs[0][...] + ws[1] * accs[1][...] + ws[2] * accs[2][...]) / den
        lse_ref[...] = mx + jnp.log(den)

    blk = (QBLK, HEAD)

    def spec(first_col, lag):
        specs = []
        for g in range(3):
            col = first_col + g * HEADS
            if lag:
                specs.append(pl.BlockSpec(blk, lambda j, i, col=col: (jnp.maximum(i - 1, 0), col + j)))
            else:
                specs.append(pl.BlockSpec(blk, lambda j, i, col=col: (i, col + j)))
        return specs

    out_spec = pl.BlockSpec(blk, lambda j, i: (i, j))
    return pl.pallas_call(
        body, name="attn_fwd", grid=(HEADS, n),
        in_specs=spec(0, False) + spec(12, True) + spec(12, False) + spec(24, True) + spec(24, False),
        out_specs=[out_spec] * 2, out_shape=[_sds((s, ATT_W))] * 2,
        scratch_shapes=[pltpu.VMEM(blk, _F32)] + [pltpu.VMEM((2 * QBLK, HEAD), _F32)] * 2
        + [pltpu.VMEM(blk, _F32)] * 9,
        compiler_params=_params(2))(*([proj] * 15))


def _attn_bwd(proj, d_o, o, lse, g, into):
    s = proj.shape[0]
    dil = DILATIONS[g]
    n = s // QBLK
    scale = HEAD ** -0.5
    tiles = _band_tiles(dil)

    def body(*refs):
        q_ref, kp_ref, kc_ref, vp_ref, vc_ref, do_ref, o_ref, lse_ref = refs[0:8]
        dq_ref, dk_ref, dv_ref, kbuf, vbuf, dkbuf, dvbuf, dqbuf, qbuf = refs[-9:]
        i = pl.program_id(1)

        @pl.when(i == 0)
        def _():
            dkbuf[0:QBLK, :] = jnp.zeros((QBLK, HEAD), _F32)
            dvbuf[0:QBLK, :] = jnp.zeros((QBLK, HEAD), _F32)

        @pl.when(i < n)
        def _():
            qbuf[...] = q_ref[...].astype(_F32)
            kbuf[0:QBLK, :] = kp_ref[...].astype(_F32)
            kbuf[QBLK:2 * QBLK, :] = kc_ref[...].astype(_F32)
            vbuf[0:QBLK, :] = vp_ref[...].astype(_F32)
            vbuf[QBLK:2 * QBLK, :] = vc_ref[...].astype(_F32)
            dkbuf[QBLK:2 * QBLK, :] = jnp.zeros((QBLK, HEAD), _F32)
            dvbuf[QBLK:2 * QBLK, :] = jnp.zeros((QBLK, HEAD), _F32)
            for qs, ks, b in tiles:
                qsl = _strided(qs, BAND, dil)
                ksl = _strided(ks, 2 * BAND, dil)
                q = qbuf[qsl, :].astype(_MXU)
                kk = kbuf[ksl, :].astype(_MXU)
                vv = vbuf[ksl, :].astype(_MXU)
                dov = do_ref[qsl, :]
                dd = jnp.sum(dov * o_ref[qsl, :], axis=-1, keepdims=True)
                lse_t = lse_ref[qsl, :][:, 0:1]
                sc = lax.dot_general(q, kk, _NT, preferred_element_type=_F32) * scale
                p = jnp.where(_band_mask(i, b), jnp.exp(sc - lse_t), 0.0)
                dob = dov.astype(_MXU)
                dp = lax.dot_general(dob, vv, _NT, preferred_element_type=_F32)
                ds = (p * (dp - dd) * scale).astype(_MXU)
                dqbuf[qsl, :] = jnp.dot(ds, kk, preferred_element_type=_F32)
                dkbuf[ksl, :] += lax.dot_general(ds, q, _TN, preferred_element_type=_F32)
                dvbuf[ksl, :] += lax.dot_general(p.astype(_MXU), dob, _TN, preferred_element_type=_F32)
            dq_ref[...] = dqbuf[...].astype(dq_ref.dtype)

        dk_ref[...] = dkbuf[0:QBLK, :].astype(dk_ref.dtype)
        dv_ref[...] = dvbuf[0:QBLK, :].astype(dv_ref.dtype)
        dkbuf[0:QBLK, :] = dkbuf[QBLK:2 * QBLK, :]
        dvbuf[0:QBLK, :] = dvbuf[QBLK:2 * QBLK, :]

    blk = (QBLK, HEAD)
    cq, ck, cv = g * HEADS, 12 + g * HEADS, 24 + g * HEADS

    def cur(i):
        return jnp.minimum(i, n - 1)

    def prev(i):
        return jnp.maximum(jnp.minimum(i, n - 1) - 1, 0)

    own = pl.BlockSpec(blk, lambda j, i: (cur(i), j))
    own_out = pl.BlockSpec(blk, lambda j, i: (cur(i), cq + j))
    late_out = pl.BlockSpec(blk, lambda j, i: (jnp.maximum(i - 1, 0), cq + j))
    extra = [] if into is None else list(into)
    return pl.pallas_call(
        body, name="attn_bwd_d%d" % dil, grid=(HEADS, n + 1),
        in_specs=[pl.BlockSpec(blk, lambda j, i: (cur(i), cq + j)),
                  pl.BlockSpec(blk, lambda j, i: (prev(i), ck + j)),
                  pl.BlockSpec(blk, lambda j, i: (cur(i), ck + j)),
                  pl.BlockSpec(blk, lambda j, i: (prev(i), cv + j)),
                  pl.BlockSpec(blk, lambda j, i: (cur(i), cv + j)),
                  own, own, own] + [pl.BlockSpec(memory_space=pl.ANY)] * len(extra),
        out_specs=[own_out, late_out, late_out], out_shape=[_sds((s, QKV_W), _MXU)] * 3,
        input_output_aliases={8 + t: t for t in range(len(extra))},
        scratch_shapes=[pltpu.VMEM((2 * QBLK, HEAD), _F32)] * 4 + [pltpu.VMEM((QBLK, HEAD), _F32)] * 2,
        compiler_params=_params(2))(proj, proj, proj, proj, proj, d_o, o, lse, *extra)


_PARTS = ((0, 2), (2, 2), (4, 2), (6, 6))
_CHUNK = 768


def _d_h(parts, w_in):
    s = parts[0].shape[0]
    nk = IN_W // _CHUNK

    def body(p0, p1, p2, p3, w_ref, o_ref, acc):
        k = pl.program_id(2)

        @pl.when(k == 0)
        def _():
            acc[...] = jnp.zeros_like(acc)

        for p_ref, (first, cnt) in zip((p0, p1, p2, p3), _PARTS):
            @pl.when((k >= first) & (k < first + cnt))
            def _(p_ref=p_ref):
                acc[...] += lax.dot_general(p_ref[...].astype(_MXU), w_ref[...], _NT, preferred_element_type=_F32)

        @pl.when(k == nk - 1)
        def _():
            o_ref[...] = acc[...]

    def part_spec(first, cnt):
        return pl.BlockSpec((1024, _CHUNK), lambda m, n, k: (m, jnp.clip(k - first, 0, cnt - 1)))

    return pl.pallas_call(
        body, name="d_h", grid=(s // 1024, 1, nk),
        in_specs=[part_spec(*p) for p in _PARTS]
        + [pl.BlockSpec((None, D, _CHUNK), lambda m, n, k: (k // 3, 0, k % 3))],
        out_specs=pl.BlockSpec((1024, D), lambda m, n, k: (m, 0)), out_shape=_sds((s, D)),
        scratch_shapes=[pltpu.VMEM((1024, D), _F32)], compiler_params=_params(3))(*parts, w_in)


def _g_w_in(h_t, parts):
    s = h_t.shape[1]
    nk = s // 1024

    def body(*refs):
        h_ref, p_refs = refs[0], refs[1:5]
        o_ref, acc = refs[-2], refs[-1]
        n = pl.program_id(1)
        k = pl.program_id(2)

        @pl.when(k == 0)
        def _():
            acc[...] = jnp.zeros_like(acc)

        for p_ref, (first, cnt) in zip(p_refs, _PARTS):
            @pl.when((n >= first) & (n < first + cnt))
            def _(p_ref=p_ref):
                acc[...] += jnp.dot(h_ref[...], p_ref[...].astype(_MXU), preferred_element_type=_F32)

        @pl.when(k == nk - 1)
        def _():
            o_ref[...] = acc[...]

    def part_spec(first, cnt):
        def index(m, n, k):
            row = jnp.where(n < first, 0, jnp.where(n >= first + cnt, nk - 1, k))
            return (row, jnp.clip(n - first, 0, cnt - 1))
        return pl.BlockSpec((1024, _CHUNK), index)

    return pl.pallas_call(
        body, name="g_w_in", grid=(1, IN_W // _CHUNK, nk),
        in_specs=[pl.BlockSpec((D, 1024), lambda m, n, k: (0, k))] + [part_spec(*p) for p in _PARTS],
        out_specs=pl.BlockSpec((None, D, _CHUNK), lambda m, n, k: (n // 3, 0, n % 3)),
        out_shape=_sds((N_CHIPS, D, 2304)),
        scratch_shapes=[pltpu.VMEM((D, _CHUNK), _F32)], compiler_params=_params(3))(h_t, *parts)


def _layer_fwd(l, x, p, gw, late):
    s = x.shape[0]
    nm = s // 1024
    h, h_t = _prenorm_fwd(x, p["g_pre"], p["shift"], p["scale"])
    proj = _mm("proj", h, gw["w_in"][l], _sds((s, IN_W), _MXU), grid=(nm, N_CHIPS, 1),
               a_spec=pl.BlockSpec((1024, D), lambda m, n, k: (m, 0)),
               b_spec=pl.BlockSpec((None, D, 2304), lambda m, n, k: (n, 0, 0)),
               o_spec=pl.BlockSpec((1024, 2304), lambda m, n, k: (m, n)), dims=_NN, acc_shape=(1024, 2304))
    o, lse = _attn_fwd(proj)
    uc = _conv_fwd(proj, p["conv_w"], p["conv_b"])
    h_lru = _scan_fwd(uc, p["wt"], p["b_rg"], p["b_ig"], p["lam"])
    if late is not None:
        landed = dict(late(h_lru))
        gw["w_in"].append(landed.pop("w_in1"))
        gw.update(landed)
    a_att, b_act, y_a, y_b, z, out, x_new = _tail_fwd(l, o, h_lru, proj, x, p["gate"], p["g_post"], gw)
    saved = dict(x=x, h_t=h_t, proj=proj, o=o, lse=lse, uc=uc, h_lru=h_lru, a_att=a_att, b_act=b_act,
                 y_a=y_a, y_b=y_b, z=z, out=out)
    return x_new, saved


def _layer_bwd(l, dx, p, gw, sv, hooks):
    s = dx.shape[0]
    nt = s // 2048
    proj = sv["proj"]
    gate, b_rg, g_pre = p["gate"], p["b_rg"], p["g_pre"]
    if hooks is not None:
        gate = gate + hooks[0]([dx])
    d_out, dy_a, dy_b, d_rest, d_o, dh_lru, d_gate, d_gpost = _tail_bwd(
        l, dx, sv["out"], sv["y_a"], sv["y_b"], proj, sv["o"], sv["h_lru"], gate, p["g_post"], gw)
    if hooks is not None:
        b_rg = b_rg + hooks[1]([d_out])

    def wgrad_rows(name, a, b):
        return _mm(name, a, b, _sds((N_CHIPS, 256, D)), grid=(4, 1, nt),
                   a_spec=pl.BlockSpec((2048, 256), lambda m, n, k: (k, m)),
                   b_spec=pl.BlockSpec((2048, D), lambda m, n, k: (k, 0)),
                   o_spec=pl.BlockSpec((None, 256, D), lambda m, n, k: (m, 0, 0)),
                   dims=_TN, acc_shape=(256, D))

    big = {}
    big["w_o"] = wgrad_rows("g_w_o", sv["z"], d_out)
    big["w_pa"] = _mm("g_w_pa", sv["a_att"], dy_a, _sds((N_CHIPS, ATT_W, 256)), grid=(1, 4, nt),
                      a_spec=pl.BlockSpec((2048, ATT_W), lambda m, n, k: (k, 0)),
                      b_spec=pl.BlockSpec((2048, 256), lambda m, n, k: (k, n)),
                      o_spec=pl.BlockSpec((None, ATT_W, 256), lambda m, n, k: (n, 0, 0)),
                      dims=_TN, acc_shape=(ATT_W, 256))
    big["w_pb"] = wgrad_rows("g_w_pb", sv["b_act"], dy_b)
    duc, g_wt, d_brg, d_big, d_lam = _scan_bwd(dh_lru, sv["uc"], sv["h_lru"], p["wt"], b_rg, p["b_ig"], p["lam"])
    g_wrg, g_wig = _gate_tile_grads(g_wt)
    d_rest, g_convw, g_convb = _conv_bwd(duc, proj, p["conv_w"], d_rest)
    dqkv = None
    for g in range(3):
        dqkv = _attn_bwd(proj, d_o, sv["o"], sv["lse"], g, dqkv)
    if hooks is not None:
        g_pre = g_pre + hooks[2]([dqkv[0]])
    parts = (dqkv[0], dqkv[1], dqkv[2], d_rest)
    dh = _d_h(parts, gw["w_in"][l])
    big["w_in"] = _g_w_in(sv["h_t"], parts)
    dx_in, d_shift, d_scale, d_gpre = _prenorm_bwd(dh, sv["x"], dx, g_pre, p["scale"])
    small = dict(dmod=jnp.concatenate([d_shift, d_scale, d_gate], axis=1), g_pre=d_gpre, conv_w=g_convw,
                 conv_b=g_convb, w_rg=g_wrg, b_rg=d_brg, w_ig=g_wig, b_ig=d_big, lam=d_lam, g_post=d_gpost)
    return dx_in, small, big


_BIG = ("w_in", "w_pa", "w_pb", "w_o")


class _GradReduce:
    PAIR_CHUNKS = (2, 1, 1, 1)
    CHIP_CHUNKS = (2, 1, 1, 1)
    FILL_CHUNKS = (4, 1, 1, 1)

    def __init__(self, core, where):
        self.core, self.where = core, where
        self.finals = None

    def begin(self, l, big):
        n = len(_BIG)
        halves = [big[k].reshape(N_CHIPS, 2, big[k].shape[1] // 2, big[k].shape[2]) for k in _BIG]
        lands = [lax.empty((N_CHIPS,) + h.shape[2:], _F32) for h in halves]
        plan, nsem = _pair_plan(n, self.PAIR_CHUNKS)
        state = {}
        state["pair"] = _split_start("reduce_pair_start_%d" % l, halves + lands, plan, nsem, [])

        def started(after):
            return state["pair"][3][0, 0]

        def pair_done(after):
            send, recv, arrays, _ = state["pair"]
            arrays = _split_wait("reduce_pair_wait_%d" % l, send, recv, arrays, plan, after)
            sums = [_sum_pair("sum_pair_%s_%d" % (k, l), arrays[a], arrays[n + a], self.core, 128)
                    for a, k in enumerate(_BIG)]
            state["mine"] = [t[0] for t in sums]
            lands2 = [lax.empty(t[1].shape, _MXU) for t in sums]
            plan2, nsem2 = _chips_plan(n, self.CHIP_CHUNKS)
            state["plan2"] = plan2
            state["chips"] = _split_start("reduce_chips_start_%d" % l, [t[1] for t in sums] + lands2, plan2, nsem2, [])
            return state["chips"][3][0, 0]

        def chips_done(after):
            send, recv, arrays, _ = state["chips"]
            arrays = _split_wait("reduce_chips_wait_%d" % l, send, recv, arrays, state["plan2"], after)
            finals = [_sum_chips("sum_chips_%s_%d" % (k, l), state["mine"][a], arrays[n + a], self.where, l,
                                 None if self.finals is None else self.finals[a], 128)
                      for a, k in enumerate(_BIG)]
            plan3, nsem3 = _fill_plan(n, self.FILL_CHUNKS, l)
            state["plan3"] = plan3
            state["fill"] = _split_start("gather_halves_start_%d" % l, finals, plan3, nsem3, [])
            return state["fill"][3][0, 0]

        def finish(after):
            send, recv, arrays, _ = state["fill"]
            self.finals = _split_wait("gather_halves_wait_%d" % l, send, recv, arrays, state["plan3"], after)
            return self.finals

        self._finish = finish
        return [started, pair_done, chips_done]

    def finish(self, after):
        return self._finish(after)


def _local_step(x, target, small_p, w_in0, late, reducer):
    saved = []
    h = x
    gw = dict(w_in=[w_in0])
    for l in range(2):
        h, sv = _layer_fwd(l, h, small_p[l], gw, late if l == 0 else None)
        saved.append(sv)
    dy, sq = _loss_head(h, target)
    loss = 0.5 * jnp.sum(sq) / D
    smalls = [None, None]
    dx, smalls[1], big1 = _layer_bwd(1, dy, small_p[1], gw, saved[1], None)
    hooks = reducer.begin(1, big1)
    dx, smalls[0], big0 = _layer_bwd(0, dx, small_p[0], gw, saved[0], hooks)
    reducer.finish([dx])
    for hook in reducer.begin(0, big0):
        hook([dx])
    return loss, dx, smalls, reducer.finish([dx])


_SMALL_ROWS = 8 + 16 + 8 + 128 + 128


def _pack_small(smalls):
    dmod = jnp.concatenate([smalls[0]["dmod"].reshape(3, D), smalls[1]["dmod"].reshape(3, D),
                            jnp.zeros((2, D), _F32)], axis=0)
    vecs = jnp.concatenate([smalls[l][k] for k in ("g_pre", "conv_b", "b_rg", "b_ig", "lam", "g_post")
                            for l in range(2)] + [jnp.zeros((4, D), _F32)], axis=0)
    convw = jnp.concatenate([smalls[0]["conv_w"], smalls[1]["conv_w"]], axis=0)
    wrg = jnp.stack([smalls[0]["w_rg"], smalls[1]["w_rg"]]).reshape(128, D)
    wig = jnp.stack([smalls[0]["w_ig"], smalls[1]["w_ig"]]).reshape(128, D)
    return jnp.concatenate([dmod, vecs, convw, wrg, wig], axis=0)


def kernel(x, c, w_mod, b_mod, g_pre, w_in, conv_w, conv_b, w_rg, b_rg, w_ig, b_ig, lru_lambda, w_pa, w_pb, w_o, g_post, loss_target, m_w_mod, m_b_mod, m_g_pre, m_w_in, m_conv_w, m_conv_b, m_w_rg, m_b_rg, m_w_ig, m_b_ig, m_lru_lambda, m_w_pa, m_w_pb, m_w_o, m_g_post, v_w_mod, v_b_mod, v_g_pre, v_w_in, v_conv_w, v_conv_b, v_w_rg, v_b_rg, v_w_ig, v_b_ig, v_lru_lambda, v_w_pa, v_w_pb, v_w_o, v_g_post):
    xi, yi, ci = lax.axis_index("x"), lax.axis_index("y"), lax.axis_index("c")
    chip = 2 * xi + yi
    dev = 4 * xi + 2 * yi + ci
    mcols = w_mod.shape[2]

    pack1 = jnp.concatenate([jnp.broadcast_to(c, (8, D)),
                             jnp.pad(conv_w.reshape(8, 256), ((0, 0), (0, D - 256)))], axis=0)
    g1 = _exchange("gather_cond", [pack1], "xyc", False)[0]
    c_all = g1[:, 0, :]
    conv_w_full = jnp.transpose(g1[0::2, 8:16, 0:256], (1, 0, 2)).reshape(2, 4, D)

    b_cols = lax.dynamic_slice(b_mod, (0, chip * mcols), (2, mcols)).reshape(2, 1, mcols)
    mod_loc = _mod_fwd(c_all, w_mod, b_cols)
    g2 = _exchange("gather_mod", [mod_loc.reshape(16, mcols)], "xyc", False)[0]
    mod_full = jnp.transpose(g2[0::2], (1, 0, 2)).reshape(2, 8, 3 * D)
    mod_me = lax.dynamic_index_in_dim(mod_full, dev, axis=1, keepdims=False)

    wb_in = _cast("cast_w_in", w_in.reshape(2 * D, 2304), 256).reshape(2, D, 2304)
    late_src = [wb_in[1], _cast("cast_w_pa", w_pa.reshape(2 * ATT_W, 256), 256).reshape(2, ATT_W, 256),
                _cast("cast_w_pb", w_pb.reshape(512, D), 256).reshape(2, 256, D),
                _cast("cast_w_o", w_o.reshape(512, D), 256).reshape(2, 256, D)]
    late_chunks = [4, 2, 2, 2]
    w_in0 = _gather_weights([wb_in[0].reshape(2, D // 2, 2304)], [2])[0].reshape(N_CHIPS, D, 2304)
    chip1 = jnp.reshape(chip, (1,)).astype(jnp.int32)
    lands = [_own_slot("own_slot_" + k, a, chip1, 256) for k, a in zip(("w_in", "w_pa", "w_pb", "w_o"), late_src)]
    late_plan, late_nsem = _gather_plan(len(late_src), late_chunks)
    send_sems, recv_sems, late_arrays, token = _split_start(
        "late_gather_start", late_src + lands, late_plan, late_nsem, [w_in0, mod_me])

    def late(after):
        got = _split_wait("late_gather_wait", send_sems, recv_sems, late_arrays, late_plan, [after])[len(late_src):]
        return dict(w_in1=got[0], w_pa=got[1], w_pb=got[2], w_o=got[3])

    small_p = []
    for l in range(2):
        gates = _gate_tiles(w_rg[l], w_ig[l]).astype(_MXU)
        small_p.append(dict(
            shift=mod_me[l:l + 1, 0:D], scale=mod_me[l:l + 1, D:2 * D], gate=mod_me[l:l + 1, 2 * D:3 * D],
            g_pre=g_pre[l:l + 1], conv_w=conv_w_full[l], conv_b=conv_b[l:l + 1], wt=gates,
            b_rg=b_rg[l:l + 1], b_ig=b_ig[l:l + 1], lam=lru_lambda[l:l + 1], g_post=g_post[l:l + 1]))

    small_p[0]["shift"] = small_p[0]["shift"] + token[0, 0]

    core = jnp.reshape(ci, (1,)).astype(jnp.int32)
    where = jnp.stack([chip, ci]).astype(jnp.int32)
    loss_loc, dx, smalls, reduced = _local_step(x[0], loss_target[0], small_p, w_in0, late, _GradReduce(core, where))
    loss = lax.psum(loss_loc, ("x", "y", "c"))
    grad_x = dx[None]
    g_big = {k: a.reshape(2, 2 * a.shape[2], a.shape[3]) for k, a in zip(_BIG, reduced)}

    g3 = _exchange("gather_small", [_pack_small(smalls)], "xyc", False)[0]
    tot = _sum_lead("sum_small", g3, 96)
    dmod_all = g3[:, 0:6, :].reshape(8, 2, 3 * D)
    dm_cols = jnp.transpose(lax.dynamic_slice(dmod_all, (0, 0, chip * mcols), (8, 2, mcols)), (1, 0, 2))
    g_w_mod = _mod_bwd(jnp.transpose(c_all), dm_cols)
    vec = tot[8:20].reshape(6, 2, D)
    g_conv_w_full = tot[24:32].reshape(2, 4, D)
    grads = dict(
        w_mod=g_w_mod, b_mod=tot[0:6].reshape(2, 3 * D), g_pre=vec[0], w_in=g_big["w_in"],
        conv_w=lax.dynamic_slice(g_conv_w_full, (0, 0, chip * 256), (2, 4, 256)), conv_b=vec[1],
        w_rg=tot[32:160].reshape(2, 16, 64, 64), b_rg=vec[2], w_ig=tot[160:288].reshape(2, 16, 64, 64),
        b_ig=vec[3], lru_lambda=vec[4], w_pa=g_big["w_pa"], w_pb=g_big["w_pb"], w_o=g_big["w_o"],
        g_post=vec[5])

    weights = dict(w_mod=w_mod, b_mod=b_mod, g_pre=g_pre, w_in=w_in, conv_w=conv_w, conv_b=conv_b, w_rg=w_rg,
                   b_rg=b_rg, w_ig=w_ig, b_ig=b_ig, lru_lambda=lru_lambda, w_pa=w_pa, w_pb=w_pb, w_o=w_o,
                   g_post=g_post)
    ms = dict(w_mod=m_w_mod, b_mod=m_b_mod, g_pre=m_g_pre, w_in=m_w_in, conv_w=m_conv_w, conv_b=m_conv_b,
              w_rg=m_w_rg, b_rg=m_b_rg, w_ig=m_w_ig, b_ig=m_b_ig, lru_lambda=m_lru_lambda, w_pa=m_w_pa,
              w_pb=m_w_pb, w_o=m_w_o, g_post=m_g_post)
    vs = dict(w_mod=v_w_mod, b_mod=v_b_mod, g_pre=v_g_pre, w_in=v_w_in, conv_w=v_conv_w, conv_b=v_conv_b,
              w_rg=v_w_rg, b_rg=v_b_rg, w_ig=v_w_ig, b_ig=v_b_ig, lru_lambda=v_lru_lambda, w_pa=v_w_pa,
              w_pb=v_w_pb, w_o=v_w_o, g_post=v_g_post)
    flat = dict(w_mod=(2 * D, mcols, 256), b_mod=(2, 3 * D, 2), g_pre=(2, D, 2), w_in=(2 * D, 2304, 256),
                conv_w=(8, 256, 8), conv_b=(2, D, 2), w_rg=(128, D, 128), b_rg=(2, D, 2), w_ig=(128, D, 128),
                b_ig=(2, D, 2), lru_lambda=(2, D, 2), w_pa=(2 * ATT_W, 256, 256), w_pb=(512, D, 256),
                w_o=(512, D, 256), g_post=(2, D, 2))
    order = ("w_mod", "b_mod", "g_pre", "w_in", "conv_w", "conv_b", "w_rg", "b_rg", "w_ig", "b_ig",
             "lru_lambda", "w_pa", "w_pb", "w_o", "g_post")
    deltas, new_m, new_v = [], [], []
    for k in order:
        rows, cols, tb = flat[k]
        shp = weights[k].shape
        d, nm_, nv_ = _adamw("adamw_" + k, weights[k].reshape(rows, cols), grads[k].reshape(rows, cols),
                             ms[k].reshape(rows, cols), vs[k].reshape(rows, cols), tb)
        deltas.append(d.reshape(shp))
        new_m.append(nm_.reshape(shp))
        new_v.append(nv_.reshape(shp))
    return (loss, grad_x, *[grads[k].reshape(weights[k].shape) for k in order], *deltas, *new_m, *new_v)
```

```python
import functools

import jax
import jax.numpy as jnp
from jax import lax
from jax.experimental import pallas as pl
from jax.experimental.pallas import tpu as pltpu

_F32 = jnp.float32
_MXU = jnp.bfloat16
_VMEM_LIMIT = 56 * 1024 * 1024
_MESH = pl.DeviceIdType.MESH

D = 1024
HEAD = 128
HEADS = 4
ATT_W = 512
QKV_W = 1536
IN_W = 9216
DILATIONS = (1, 4, 16)
BAND = 128
QBLK = BAND * 16
NORM_EPS = 1e-6
NEG_INF = -1e30
LRU_C = 8.0
N_CHIPS = 4
CB_GATT = 4608 // 512
CB_U, CB_GLRU, CB_MA, CB_MB = 5, 6, 7, 8
R_U, R_GLRU, R_MA, R_MB, R_END = 512, 1536, 2560, 3584, 4608

ADAM_LR, ADAM_B1, ADAM_B2, ADAM_EPS, ADAM_WD, ADAM_STEP = 0.001, 0.9, 0.999, 1e-08, 0.01, 10


def _params(ngrid):
    return pltpu.CompilerParams(dimension_semantics=("arbitrary",) * ngrid, vmem_limit_bytes=_VMEM_LIMIT)


def _sigmoid(v):
    return 0.5 * jnp.tanh(0.5 * v) + 0.5


_GROUPS = {
    "c": [(0, 0, 1)],
    "xy": [(1, 0, 0), (0, 1, 0), (1, 1, 0)],
    "xyc": [(0, 0, 1), (0, 1, 0), (0, 1, 1), (1, 0, 0), (1, 0, 1), (1, 1, 0), (1, 1, 1)],
}


def _rank(group, px, py, pc):
    if group == "c":
        return pc
    if group == "xy":
        return 2 * px + py
    return 4 * px + 2 * py + pc


def _flip(rel, x, y, c):
    dx, dy, dc = rel
    return (1 - x if dx else x, 1 - y if dy else y, 1 - c if dc else c)


def _pieces(ref, nchunk):
    step = ref.shape[0] // nchunk
    return [ref.at[pl.ds(q * step, step)] for q in range(nchunk)]


def _exchange(name, srcs, group, scatter, *, local=True, nchunks=None):
    rels = _GROUPS[group]
    gsize = len(rels) + 1
    n = len(srcs)
    nchunks = nchunks or [1] * n
    blks = [s.shape[1:] if scatter else s.shape for s in srcs]
    slotted = local or gsize > 2
    base = [sum(nchunks[:a]) for a in range(n)]
    tot = sum(nchunks)

    def body(*refs):
        src_refs, out_refs = refs[:n], refs[n:2 * n]
        send_sems, recv_sems, loc_sems = refs[2 * n:]
        x, y, c = lax.axis_index("x"), lax.axis_index("y"), lax.axis_index("c")
        me = _rank(group, x, y, c)
        copies = []
        for a in range(n):
            def part(r, a=a):
                return src_refs[a].at[r] if scatter else src_refs[a]
            dst = out_refs[a].at[me] if slotted else out_refs[a]
            if local:
                for q, (s_, d_) in enumerate(zip(_pieces(part(me), nchunks[a]), _pieces(dst, nchunks[a]))):
                    loc = pltpu.make_async_copy(s_, d_, loc_sems.at[base[a] + q])
                    loc.start()
                    copies.append(loc)
            for k, rel in enumerate(rels):
                peer = _flip(rel, x, y, c)
                for q, (s_, d_) in enumerate(zip(_pieces(part(_rank(group, *peer)), nchunks[a]),
                                                 _pieces(dst, nchunks[a]))):
                    cp = pltpu.make_async_remote_copy(
                        src_ref=s_, dst_ref=d_, send_sem=send_sems.at[(base[a] + q) * len(rels) + k],
                        recv_sem=recv_sems.at[(base[a] + q) * len(rels) + k],
                        device_id=peer, device_id_type=_MESH)
                    cp.start()
                    copies.append(cp)
        for cp in copies:
            cp.wait()

    any_spec = pl.BlockSpec(memory_space=pl.ANY)
    lead = (gsize,) if slotted else ()
    return pl.pallas_call(
        body, name=name,
        out_shape=[jax.ShapeDtypeStruct(lead + tuple(b), s.dtype) for b, s in zip(blks, srcs)],
        in_specs=[any_spec] * n, out_specs=[any_spec] * n,
        scratch_shapes=[pltpu.SemaphoreType.DMA((tot * len(rels),)), pltpu.SemaphoreType.DMA((tot * len(rels),)),
                        pltpu.SemaphoreType.DMA((tot,))],
    )(*srcs)


def _pair_fill(name, arrs, nchunks):
    n = len(arrs)
    base = [sum(nchunks[:a]) for a in range(n)]
    tot = sum(nchunks)

    def body(*refs):
        out_refs = refs[n:2 * n]
        send_sems, recv_sems = refs[2 * n:]
        x, y, c = lax.axis_index("x"), lax.axis_index("y"), lax.axis_index("c")
        copies = []
        for a in range(n):
            for q, blk in enumerate(_pieces(out_refs[a].at[c], nchunks[a])):
                cp = pltpu.make_async_remote_copy(
                    src_ref=blk, dst_ref=blk, send_sem=send_sems.at[base[a] + q], recv_sem=recv_sems.at[base[a] + q],
                    device_id=(x, y, 1 - c), device_id_type=_MESH)
                cp.start()
                copies.append(cp)
        for cp in copies:
            cp.wait()

    any_spec = pl.BlockSpec(memory_space=pl.ANY)
    return pl.pallas_call(
        body, name=name, out_shape=[jax.ShapeDtypeStruct(a.shape, a.dtype) for a in arrs],
        in_specs=[any_spec] * n, out_specs=[any_spec] * n, input_output_aliases={a: a for a in range(n)},
        scratch_shapes=[pltpu.SemaphoreType.DMA((tot,)), pltpu.SemaphoreType.DMA((tot,))],
    )(*arrs)


def _gather_weights(wb, nchunks):
    n = len(wb)
    rels = _GROUPS["xy"]
    base = [sum(nchunks[:a]) for a in range(n)]
    tot = sum(nchunks)

    def body(*refs):
        src_refs, out_refs = refs[:n], refs[n:2 * n]
        ici_send, ici_recv, d2d_send, d2d_recv, loc_sems = refs[2 * n:]
        x, y, c = lax.axis_index("x"), lax.axis_index("y"), lax.axis_index("c")
        me = 2 * x + y
        waits = []
        for a in range(n):
            for l in range(2):
                for q, (s_, d_) in enumerate(zip(_pieces(src_refs[a].at[l], nchunks[a]),
                                                 _pieces(out_refs[a].at[me, l], nchunks[a]))):
                    loc = pltpu.make_async_copy(s_, d_, loc_sems.at[(base[a] + q) * 2 + l])
                    loc.start()
                    waits.append(loc)
        first = []
        for a in range(n):
            for k, rel in enumerate(rels):
                px, py, _ = _flip(rel, x, y, c)
                for q, (s_, d_) in enumerate(zip(_pieces(src_refs[a].at[c], nchunks[a]),
                                                 _pieces(out_refs[a].at[me, c], nchunks[a]))):
                    sem = (base[a] + q) * 3 + k
                    cp = pltpu.make_async_remote_copy(src_ref=s_, dst_ref=d_, send_sem=ici_send.at[sem],
                                                      recv_sem=ici_recv.at[sem], device_id=(px, py, c),
                                                      device_id_type=_MESH)
                    cp.start()
                    first.append(cp)
        second = []
        for a in range(n):
            for k, rel in enumerate(rels):
                px, py, _ = _flip(rel, x, y, c)
                for q, blk in enumerate(_pieces(out_refs[a].at[2 * px + py, c], nchunks[a])):
                    sem = (base[a] + q) * 3 + k
                    landed = pltpu.make_async_remote_copy(src_ref=blk, dst_ref=blk, send_sem=ici_send.at[sem],
                                                          recv_sem=ici_recv.at[sem], device_id=(px, py, c),
                                                          device_id_type=_MESH)
                    landed.wait_recv()
                    cp = pltpu.make_async_remote_copy(src_ref=blk, dst_ref=blk, send_sem=d2d_send.at[sem],
                                                      recv_sem=d2d_recv.at[sem], device_id=(x, y, 1 - c),
                                                      device_id_type=_MESH)
                    cp.start()
                    second.append(cp)
        for cp in first:
            cp.wait_send()
        for cp in second:
            cp.wait_send()
        for a in range(n):
            for k, rel in enumerate(rels):
                px, py, _ = _flip(rel, x, y, c)
                for q, blk in enumerate(_pieces(out_refs[a].at[2 * px + py, 1 - c], nchunks[a])):
                    sem = (base[a] + q) * 3 + k
                    pltpu.make_async_remote_copy(src_ref=blk, dst_ref=blk, send_sem=d2d_send.at[sem],
                                                 recv_sem=d2d_recv.at[sem], device_id=(x, y, 1 - c),
                                                 device_id_type=_MESH).wait_recv()
        for cp in waits:
            cp.wait()

    any_spec = pl.BlockSpec(memory_space=pl.ANY)
    return pl.pallas_call(
        body, name="gather_weights",
        out_shape=[jax.ShapeDtypeStruct((N_CHIPS,) + a.shape, a.dtype) for a in wb],
        in_specs=[any_spec] * n, out_specs=[any_spec] * n,
        scratch_shapes=[pltpu.SemaphoreType.DMA((tot * 3,))] * 4 + [pltpu.SemaphoreType.DMA((tot * 2,))],
    )(*wb)


_HBM = pl.BlockSpec(memory_space=pltpu.HBM)
_SEM = pl.BlockSpec(memory_space=pltpu.SEMAPHORE)
_EFFECT = pltpu.SideEffectType.DATAFLOW_SIDE_EFFECTING


def _own_slot(name, src, chip, tb):
    rows, cols = src.shape[-2:]
    lead = src.shape[:-2]
    flat = src.reshape((-1, cols))

    def body(s_ref, a_ref, o_ref):
        o_ref[...] = a_ref[...]

    grid_spec = pltpu.PrefetchScalarGridSpec(
        num_scalar_prefetch=1, grid=(flat.shape[0] // tb,),
        in_specs=[pl.BlockSpec((tb, cols), lambda i, s: (i, 0))],
        out_specs=pl.BlockSpec((None, tb, cols), lambda i, s: (s[0], i, 0)))
    out = pl.pallas_call(body, name=name, grid_spec=grid_spec,
                         out_shape=jax.ShapeDtypeStruct((N_CHIPS,) + flat.shape, src.dtype),
                         compiler_params=_params(1))(chip, flat)
    return out.reshape((N_CHIPS,) + lead + (rows, cols))


def _numbered(pairs, peer, send_sems, recv_sems, first):
    return [pltpu.make_async_remote_copy(src_ref=s_, dst_ref=d_, send_sem=send_sems.at[first + q],
                                         recv_sem=recv_sems.at[first + q], device_id=peer, device_id_type=_MESH)
            for q, (s_, d_) in enumerate(pairs)]


def _gather_plan(n, nchunks):
    def plan(refs, send_sems, recv_sems):
        x, y, c = lax.axis_index("x"), lax.axis_index("y"), lax.axis_index("c")
        me = 2 * x + y
        copies = []
        for a in range(n):
            for rel in _GROUPS["xy"]:
                px, py, _ = _flip(rel, x, y, c)
                pairs = list(zip(_pieces(refs[a], nchunks[a]), _pieces(refs[n + a].at[me], nchunks[a])))
                copies += _numbered(pairs, (px, py, c), send_sems, recv_sems, len(copies))
        return copies
    return plan, 3 * sum(nchunks)


def _pair_plan(n, nchunks):
    def plan(refs, send_sems, recv_sems):
        x, y, c = lax.axis_index("x"), lax.axis_index("y"), lax.axis_index("c")
        copies = []
        for a in range(n):
            for j in range(N_CHIPS):
                pairs = list(zip(_pieces(refs[a].at[j, 1 - c], nchunks[a]), _pieces(refs[n + a].at[j], nchunks[a])))
                copies += _numbered(pairs, (x, y, 1 - c), send_sems, recv_sems, len(copies))
        return copies
    return plan, N_CHIPS * sum(nchunks)


def _chips_plan(n, nchunks):
    def plan(refs, send_sems, recv_sems):
        x, y, c = lax.axis_index("x"), lax.axis_index("y"), lax.axis_index("c")
        me = 2 * x + y
        copies = []
        for a in range(n):
            for rel in _GROUPS["xy"]:
                px, py, _ = _flip(rel, x, y, c)
                pairs = list(zip(_pieces(refs[a].at[2 * px + py], nchunks[a]), _pieces(refs[n + a].at[me], nchunks[a])))
                copies += _numbered(pairs, (px, py, c), send_sems, recv_sems, len(copies))
        return copies
    return plan, 3 * sum(nchunks)


def _fill_plan(n, nchunks, l):
    def plan(refs, send_sems, recv_sems):
        x, y, c = lax.axis_index("x"), lax.axis_index("y"), lax.axis_index("c")
        copies = []
        for a in range(n):
            blk = _pieces(refs[a].at[l, c], nchunks[a])
            copies += _numbered(list(zip(blk, blk)), (x, y, 1 - c), send_sems, recv_sems, len(copies))
        return copies
    return plan, sum(nchunks)


def _split_start(name, arrays, plan, nsem, after):
    n = len(arrays)
    na = len(after)

    def body(*refs):
        send_sems, recv_sems = refs[n + na], refs[n + na + 1]
        token = refs[-1]
        for cp in plan(refs[:n], send_sems, recv_sems):
            cp.start()
        token[...] = jnp.zeros_like(token)

    hbm = [pltpu.HBM(a.shape, a.dtype) for a in arrays]
    outs = pl.pallas_call(
        body, name=name,
        out_shape=(pltpu.SemaphoreType.DMA((nsem,)), pltpu.SemaphoreType.DMA((nsem,)), *hbm, _sds((8, 128))),
        in_specs=[_HBM] * n + [pl.BlockSpec(memory_space=pl.ANY)] * na,
        out_specs=(_SEM, _SEM, *([_HBM] * n), pl.BlockSpec(memory_space=pltpu.VMEM)),
        input_output_aliases={i: 2 + i for i in range(n)},
        compiler_params=pltpu.CompilerParams(has_side_effects=_EFFECT),
    )(*[pltpu.with_memory_space_constraint(a, pltpu.HBM) for a in arrays], *after)
    return outs[0], outs[1], list(outs[2:2 + n]), outs[-1]


def _split_wait(name, send_sems, recv_sems, arrays, plan, after):
    n = len(arrays)

    def body(*refs):
        for cp in plan(refs[:n], refs[n], refs[n + 1]):
            cp.wait_send()
            cp.wait_recv()

    hbm = [pltpu.HBM(a.shape, a.dtype) for a in arrays]
    return list(pl.pallas_call(
        body, name=name, out_shape=tuple(hbm),
        in_specs=[_HBM] * n + [_SEM, _SEM] + [pl.BlockSpec(memory_space=pl.ANY)] * len(after),
        out_specs=tuple([_HBM] * n), input_output_aliases={i: i for i in range(n)},
        compiler_params=pltpu.CompilerParams(has_side_effects=_EFFECT),
    )(*arrays, send_sems, recv_sems, *after))


def _mm(name, a, b, out_sds, *, grid, a_spec, b_spec, o_spec, dims, acc_shape, into=None):
    nk = grid[2]

    def body(*refs):
        a_ref, b_ref = refs[0], refs[1]
        o_ref, acc = refs[-2], refs[-1]
        k = pl.program_id(2)
        part = lax.dot_general(a_ref[...].astype(_MXU), b_ref[...].astype(_MXU), dims,
                               preferred_element_type=_F32)
        if nk == 1:
            o_ref[...] = part.astype(o_ref.dtype)
            return

        @pl.when(k == 0)
        def _():
            acc[...] = part

        @pl.when(k > 0)
        def _():
            acc[...] += part

        @pl.when(k == nk - 1)
        def _():
            o_ref[...] = acc[...].astype(o_ref.dtype)

    if nk == 1:
        acc_shape = (8, 128)
    in_specs = [a_spec, b_spec]
    args = [a, b]
    aliases = {}
    if into is not None:
        in_specs.append(pl.BlockSpec(memory_space=pl.ANY))
        args.append(into)
        aliases = {2: 0}
    return pl.pallas_call(
        body, name=name, grid=grid, in_specs=in_specs, out_specs=o_spec, out_shape=out_sds,
        scratch_shapes=[pltpu.VMEM(acc_shape, _F32)], input_output_aliases=aliases,
        compiler_params=_params(3))(*args)


_NN = (((1,), (0,)), ((), ()))
_NT = (((1,), (1,)), ((), ()))
_TN = (((0,), (0,)), ((), ()))


def _rowwise(name, body, *, grid, ins, outs, scratch=()):
    return pl.pallas_call(
        body, name=name, grid=(grid,), in_specs=[s for _, s in ins], out_specs=[s for _, s in outs],
        out_shape=[o for o, _ in outs], scratch_shapes=list(scratch),
        compiler_params=_params(1))(*[a for a, _ in ins])


def _rows(tb, w, cb=0, n=None):
    if n is None:
        return pl.BlockSpec((tb, w), lambda i: (i, cb))
    return pl.BlockSpec((tb, w), lambda i: (n - 1 - i, cb))


def _vec(shape):
    return pl.BlockSpec(shape, lambda i: (0,) * len(shape))


def _halo_prev(tb, w, cb=0, n=None, rows=8):
    if n is None:
        return pl.BlockSpec((rows, w), lambda i: (jnp.maximum(i * (tb // rows) - 1, 0), cb))
    return pl.BlockSpec((rows, w), lambda i: (jnp.maximum((n - 1 - i) * (tb // rows) - 1, 0), cb))


def _halo_next(tb, w, n, cb=0):
    return pl.BlockSpec((8, w), lambda i: (jnp.minimum((i + 1) * (tb // 8), n * (tb // 8) - 1), cb))


def _sds(shape, dtype=_F32):
    return jax.ShapeDtypeStruct(shape, dtype)


def _cast(name, a, tb):
    rows, cols = a.shape

    def body(a_ref, o_ref):
        o_ref[...] = a_ref[...].astype(o_ref.dtype)

    return _rowwise(name, body, grid=rows // tb, ins=[(a, _rows(tb, cols))],
                    outs=[(_sds((rows, cols), _MXU), _rows(tb, cols))])[0]


def _sum_lead(name, a, tb):
    g, rows, cols = a.shape

    def body(a_ref, o_ref):
        acc = a_ref[0]
        for k in range(1, g):
            acc = acc + a_ref[k]
        o_ref[...] = acc

    return _rowwise(name, body, grid=rows // tb,
                    ins=[(a, pl.BlockSpec((g, tb, cols), lambda i: (0, i, 0)))],
                    outs=[(_sds((rows, cols)), _rows(tb, cols))])[0]


def _sum_pair(name, mine, theirs, core, tb):
    nj, _, rows, cols = mine.shape

    def body(s_ref, a_ref, b_ref, o_ref, ob_ref):
        t = a_ref[...] + b_ref[...]
        o_ref[...] = t
        ob_ref[...] = t.astype(ob_ref.dtype)

    blk = pl.BlockSpec((None, tb, cols), lambda j, i, s: (j, i, 0))
    grid_spec = pltpu.PrefetchScalarGridSpec(
        num_scalar_prefetch=1, grid=(nj, rows // tb),
        in_specs=[pl.BlockSpec((None, None, tb, cols), lambda j, i, s: (j, s[0], i, 0)), blk],
        out_specs=[blk, blk])
    return pl.pallas_call(body, name=name, grid_spec=grid_spec,
                          out_shape=[_sds((nj, rows, cols)), _sds((nj, rows, cols), _MXU)],
                          compiler_params=_params(2))(core, mine, theirs)


def _sum_chips(name, mine, theirs, where, l, into, tb):
    _, rows, cols = mine.shape
    extra = [] if into is None else [into]

    def body(*refs):
        a_ref, b1_ref, b2_ref, b3_ref = refs[1:5]
        o_ref = refs[-1]
        o_ref[...] = ((a_ref[...] + b1_ref[...].astype(_F32)) + b2_ref[...].astype(_F32)) + b3_ref[...].astype(_F32)

    def slot(k):
        return pl.BlockSpec((None, tb, cols), lambda i, s: (jnp.bitwise_xor(s[0], k), i, 0))

    grid_spec = pltpu.PrefetchScalarGridSpec(
        num_scalar_prefetch=1, grid=(rows // tb,),
        in_specs=[slot(0), slot(1), slot(2), slot(3)] + [pl.BlockSpec(memory_space=pl.ANY)] * len(extra),
        out_specs=pl.BlockSpec((None, None, tb, cols), lambda i, s: (l, s[1], i, 0)))
    return pl.pallas_call(body, name=name, grid_spec=grid_spec, out_shape=_sds((2, 2, rows, cols)),
                          input_output_aliases={5: 0} if extra else {},
                          compiler_params=_params(1))(where, mine, theirs, theirs, theirs, *extra)


def _adamw(name, w, g, m, v, tb):
    rows, cols = w.shape
    c1 = 1.0 - ADAM_B1 ** ADAM_STEP
    c2 = 1.0 - ADAM_B2 ** ADAM_STEP

    def body(w_ref, g_ref, m_ref, v_ref, d_ref, nm_ref, nv_ref):
        gv = g_ref[...]
        nm = ADAM_B1 * m_ref[...] + (1.0 - ADAM_B1) * gv
        nv = ADAM_B2 * v_ref[...] + (1.0 - ADAM_B2) * (gv * gv)
        d_ref[...] = -ADAM_LR * ((nm / c1) / (jnp.sqrt(nv / c2) + ADAM_EPS) + ADAM_WD * w_ref[...])
        nm_ref[...] = nm
        nv_ref[...] = nv

    spec = _rows(tb, cols)
    return _rowwise(name, body, grid=rows // tb, ins=[(w, spec), (g, spec), (m, spec), (v, spec)],
                    outs=[(_sds((rows, cols)), spec)] * 3)


def _mod_fwd(c_all, w_mod, b_cols):
    cols = w_mod.shape[2]

    def body(c_ref, w_ref, b_ref, o_ref):
        cv = c_ref[...]
        sc = (cv * _sigmoid(cv)).astype(_MXU)
        o_ref[...] = jnp.dot(sc, w_ref[...].astype(_MXU), preferred_element_type=_F32) + b_ref[...]

    return pl.pallas_call(
        body, name="mod_fwd", grid=(2,),
        in_specs=[pl.BlockSpec((8, D), lambda l: (0, 0)), pl.BlockSpec((None, D, cols), lambda l: (l, 0, 0)),
                  pl.BlockSpec((None, 1, cols), lambda l: (l, 0, 0))],
        out_specs=pl.BlockSpec((None, 8, cols), lambda l: (l, 0, 0)),
        out_shape=_sds((2, 8, cols)), compiler_params=_params(1))(c_all, w_mod, b_cols)


def _mod_bwd(c_all_t, dm):
    cols = dm.shape[2]

    def body(c_ref, d_ref, o_ref):
        cv = c_ref[...]
        sc = (cv * _sigmoid(cv)).astype(_MXU)
        o_ref[...] = jnp.dot(sc, d_ref[...].astype(_MXU), preferred_element_type=_F32)

    return pl.pallas_call(
        body, name="mod_bwd", grid=(2,),
        in_specs=[pl.BlockSpec((D, 8), lambda l: (0, 0)), pl.BlockSpec((None, 8, cols), lambda l: (l, 0, 0))],
        out_specs=pl.BlockSpec((None, D, cols), lambda l: (l, 0, 0)),
        out_shape=_sds((2, D, cols)), compiler_params=_params(1))(c_all_t, dm)


def _proj(x, g_pre, shift, scale, w_in):
    s = x.shape[0]
    tm = 1024

    def body(x_ref, g_ref, sh_ref, sc_ref, w_ref, o_ref, ht_ref, h_s):
        @pl.when(pl.program_id(1) == 0)
        def _():
            xv = x_ref[...]
            rstd = lax.rsqrt(jnp.mean(xv * xv, axis=-1, keepdims=True) + NORM_EPS)
            hv = (xv * rstd) * g_ref[...] * (1.0 + sc_ref[...]) + sh_ref[...]
            h_s[...] = hv.astype(h_s.dtype)
            ht_ref[...] = hv.T.astype(ht_ref.dtype)

        o_ref[...] = jnp.dot(h_s[...], w_ref[...], preferred_element_type=_F32).astype(o_ref.dtype)

    vec = pl.BlockSpec((1, D), lambda m, n: (0, 0))
    return pl.pallas_call(
        body, name="proj", grid=(s // tm, N_CHIPS),
        in_specs=[pl.BlockSpec((tm, D), lambda m, n: (m, 0)), vec, vec, vec,
                  pl.BlockSpec((None, D, 2304), lambda m, n: (n, 0, 0))],
        out_specs=[pl.BlockSpec((tm, 2304), lambda m, n: (m, n)), pl.BlockSpec((D, tm), lambda m, n: (0, m))],
        out_shape=[_sds((s, IN_W), _MXU), _sds((D, s), _MXU)],
        scratch_shapes=[pltpu.VMEM((tm, D), _MXU)], compiler_params=_params(2))(x, g_pre, shift, scale, w_in)


def _shift_down(cur, halo, j, tb):
    ext = jnp.concatenate([halo, cur], axis=0)
    return pltpu.roll(ext, j, 0)[8:8 + tb]


def _shift_up(cur, halo, j, tb):
    ext = jnp.concatenate([cur, halo], axis=0)
    return pltpu.roll(ext, tb + 8 - j, 0)[0:tb]


def _conv_fwd(proj, conv_w, conv_b):
    s = proj.shape[0]
    tb = 512

    def body(u_ref, hp_ref, w_ref, b_ref, o_ref):
        i = pl.program_id(0)
        u = u_ref[...].astype(_F32)
        halo = jnp.where(i > 0, hp_ref[...].astype(_F32)[8:16], 0.0)
        acc = b_ref[...] + u * w_ref[0:1, :]
        for j in range(1, 4):
            acc = acc + _shift_down(u, halo, j, tb) * w_ref[j:j + 1, :]
        o_ref[...] = acc

    return _rowwise("conv_fwd", body, grid=s // tb,
                    ins=[(proj, _rows(tb, D, CB_U)), (proj, _halo_prev(tb, D, CB_U, rows=16)),
                         (conv_w, _vec((4, D))), (conv_b, _vec((1, D)))],
                    outs=[(_sds((s, D)), _rows(tb, D))])[0]


def _lru_gates(pre_r, pre_i, uc, b_rg, b_ig, lam):
    r = _sigmoid(pre_r + b_rg)
    ig = _sigmoid(pre_i + b_ig)
    nl = -lam
    sp = jnp.maximum(nl, 0.0) + jnp.log(1.0 + jnp.exp(-jnp.abs(nl)))
    la = -LRU_C * r * sp
    a = jnp.exp(la)
    one_m_a2 = -jnp.tanh(la) * (a * a + 1.0)
    inv_sq = lax.rsqrt(jnp.maximum(one_m_a2, 1e-30))
    return r, ig, sp, a, one_m_a2 * inv_sq, inv_sq


GATE_TILES = 8


def _gate_tiles(w_rg, w_ig):
    eye = jnp.eye(2, dtype=w_rg.dtype)

    def tiles(w):
        return jnp.einsum("cpij,pq->cpiqj", w.reshape(GATE_TILES, 2, 64, 64), eye).reshape(GATE_TILES, 128, 128)

    return jnp.concatenate([tiles(w_rg), tiles(w_ig)], axis=2)


def _gate_tile_grads(gw):
    keep = jnp.eye(2, dtype=jnp.bool_)[None, :, None, :, None]

    def blocks(t):
        t5 = t.reshape(GATE_TILES, 2, 64, 2, 64)
        return jnp.sum(jnp.where(keep, t5, 0.0), axis=3).reshape(16, 64, 64)

    return blocks(gw[:, :, 0:128]), blocks(gw[:, :, 128:256])


def _gate_preacts(ucv, wt_ref):
    ucb = ucv.astype(_MXU)
    ps = [jnp.dot(ucb[:, 128 * c:128 * (c + 1)], wt_ref[c], preferred_element_type=_F32) for c in range(GATE_TILES)]
    pre_r = jnp.concatenate([p[:, 0:128] for p in ps], axis=1)
    pre_i = jnp.concatenate([p[:, 128:256] for p in ps], axis=1)
    return pre_r, pre_i


def _scan_fwd(uc, wt, b_rg, b_ig, lam):
    s = uc.shape[0]
    tb = 256

    def body(uc_ref, wt_ref, brg_ref, big_ref, lam_ref, h_ref, carry, a_s, b_s):
        i = pl.program_id(0)

        @pl.when(i == 0)
        def _():
            carry[...] = jnp.zeros_like(carry)

        ucv = uc_ref[...]
        pre_r, pre_i = _gate_preacts(ucv, wt_ref)
        _, ig, _, a, sq, _ = _lru_gates(pre_r, pre_i, ucv, brg_ref[...], big_ref[...], lam_ref[...])
        av = a
        bv = sq * (ig * ucv)
        av = av.reshape(tb // 8, 8, D)
        bv = bv.reshape(tb // 8, 8, D)
        row8 = lax.broadcasted_iota(jnp.int32, (1, 8, 1), 1)
        for sh in (1, 2, 4):
            m = row8 >= sh
            b_sh = pltpu.roll(bv, sh, 1)
            a_sh = pltpu.roll(av, sh, 1)
            bv = jnp.where(m, av * b_sh + bv, bv)
            av = jnp.where(m, av * a_sh, av)
        a_s[...] = av.reshape(tb, D)
        b_s[...] = bv.reshape(tb, D)

        def tile(t, state):
            rows = pl.ds(pl.multiple_of(t * 8, 8), 8)
            hv = b_s[rows, :] + a_s[rows, :] * state
            h_ref[rows, :] = hv
            return jnp.broadcast_to(hv[7:8, :], (8, D))

        carry[...] = lax.fori_loop(0, tb // 8, tile, jnp.broadcast_to(carry[7:8, :], (8, D)), unroll=4)

    v = _vec((1, D))
    return _rowwise("scan_fwd", body, grid=s // tb,
                    ins=[(uc, _rows(tb, D)), (wt, _vec((GATE_TILES, 128, 256))), (b_rg, v), (b_ig, v), (lam, v)],
                    outs=[(_sds((s, D)), _rows(tb, D))],
                    scratch=[pltpu.VMEM((8, D), _F32), pltpu.VMEM((tb, D), _F32), pltpu.VMEM((tb, D), _F32)])[0]


def _weight_specs(l):
    return [pl.BlockSpec((N_CHIPS, None, ATT_W, 256), lambda i: (0, l, 0, 0)),
            pl.BlockSpec((N_CHIPS, None, 256, D), lambda i: (0, l, 0, 0)),
            pl.BlockSpec((N_CHIPS, None, 256, D), lambda i: (0, l, 0, 0))]


def _tail_fwd(l, o, h_lru, proj, x, gate, g_post, gw, target):
    s = x.shape[0]
    tb = 512

    def body(*refs):
        o_ref, h_ref, ga_ref, gl_ref, ma_ref, mb_ref, x_ref, gt_ref, gp_ref, wpa_ref, wpb_ref, wo_ref = refs[0:12]
        aa_ref, ba_ref, ya_ref, yb_ref, z_ref, out_ref = refs[-8:-2] if target is not None else refs[-7:-1]
        ga = ga_ref[...].astype(_F32)
        aa = (o_ref[...] * (ga * _sigmoid(ga))).astype(_MXU)
        aa_ref[...] = aa
        gl = gl_ref[...].astype(_F32)
        ba = (h_ref[...] * (gl * _sigmoid(gl))).astype(_MXU)
        ba_ref[...] = ba
        ya = jnp.concatenate([jnp.dot(aa, wpa_ref[j], preferred_element_type=_F32) for j in range(N_CHIPS)], axis=1)
        ya_ref[...] = ya.astype(ya_ref.dtype)
        yb = jnp.dot(ba, wpb_ref[...].reshape(D, D), preferred_element_type=_F32)
        yb_ref[...] = yb.astype(yb_ref.dtype)
        z = (_sigmoid(ma_ref[...].astype(_F32)) * ya
             + _sigmoid(mb_ref[...].astype(_F32)) * yb).astype(z_ref.dtype)
        z_ref[...] = z
        ov = jnp.dot(z, wo_ref[...].reshape(D, D), preferred_element_type=_F32)
        out_ref[...] = ov
        rstd = lax.rsqrt(jnp.mean(ov * ov, axis=-1, keepdims=True) + NORM_EPS)
        xn = x_ref[...] + gt_ref[...] * ((ov * rstd) * gp_ref[...])
        if target is None:
            refs[-1][...] = xn
        else:
            dy_ref, acc_ref = refs[-2], refs[-1]
            err = xn - refs[12][...]
            dy_ref[...] = err * (1.0 / D)
            _zero_first(pl.program_id(0), acc_ref)
            acc_ref[...] += jnp.sum(err * err, axis=0, keepdims=True)

    v = _vec((1, D))
    r = _rows(tb, D)
    r5 = _rows(tb, ATT_W)
    weights = list(zip((gw["w_pa"], gw["w_pb"], gw["w_o"]), _weight_specs(l)))
    head_in = [] if target is None else [(target, r)]
    head_out = [] if target is None else [(_sds((1, D)), v)]
    return _rowwise("tail_fwd" if target is None else "tail_loss_fwd", body, grid=s // tb,
                    ins=[(o, r5), (h_lru, r), (proj, _rows(tb, ATT_W, CB_GATT)), (proj, _rows(tb, D, CB_GLRU)),
                         (proj, _rows(tb, D, CB_MA)), (proj, _rows(tb, D, CB_MB)), (x, r), (gate, v), (g_post, v)]
                    + weights + head_in,
                    outs=[(_sds((s, ATT_W), _MXU), r5), (_sds((s, D), _MXU), r), (_sds((s, D), _MXU), r),
                          (_sds((s, D), _MXU), r), (_sds((s, D), _MXU), r), (_sds((s, D)), r), (_sds((s, D)), r)]
                    + head_out)


def _zero_first(i, *refs):
    @pl.when(i == 0)
    def _():
        for ref in refs:
            ref[...] = jnp.zeros_like(ref)


def _tail_bwd(l, dx, out, y_a, y_b, proj, o, h_lru, gate, g_post, gw):
    s = dx.shape[0]
    tb = 256

    def body(dx_ref, out_ref, ya_ref, yb_ref, ma_ref, mb_ref, o_ref, ga_ref, h_ref, gl_ref, gt_ref, gp_ref,
             wpa_ref, wpb_ref, wo_ref,
             dout_ref, dya_ref, dyb_ref, rest_ref, do_ref, dh_ref, dgt_ref, dgp_ref):
        i = pl.program_id(0)
        ov = out_ref[...]
        dxv = dx_ref[...]
        rstd = lax.rsqrt(jnp.mean(ov * ov, axis=-1, keepdims=True) + NORM_EPS)
        nv = ov * rstd
        s_dn = jnp.sum(dxv * nv, axis=0, keepdims=True)
        _zero_first(i, dgt_ref, dgp_ref)
        dgt_ref[...] += s_dn * gp_ref[...]
        dgp_ref[...] += s_dn * gt_ref[...]
        dn = dxv * (gt_ref[...] * gp_ref[...])
        d_out = (rstd * (dn - nv * jnp.mean(dn * nv, axis=-1, keepdims=True))).astype(_MXU)
        dout_ref[...] = d_out
        dz = lax.dot_general(d_out, wo_ref[...].reshape(D, D), _NT, preferred_element_type=_F32)
        ga = _sigmoid(ma_ref[...].astype(_F32))
        gb = _sigmoid(mb_ref[...].astype(_F32))
        dya = (dz * ga).astype(_MXU)
        dyb = (dz * gb).astype(_MXU)
        dya_ref[...] = dya
        dyb_ref[...] = dyb
        rest_ref[:, R_MA:R_MB] = (dz * ya_ref[...].astype(_F32) * ga * (1.0 - ga)).astype(rest_ref.dtype)
        rest_ref[:, R_MB:R_END] = (dz * yb_ref[...].astype(_F32) * gb * (1.0 - gb)).astype(rest_ref.dtype)
        daa = lax.dot_general(dya[:, 0:256], wpa_ref[0], _NT, preferred_element_type=_F32)
        for j in range(1, N_CHIPS):
            daa = daa + lax.dot_general(dya[:, j * 256:(j + 1) * 256], wpa_ref[j], _NT, preferred_element_type=_F32)
        dba = lax.dot_general(dyb, wpb_ref[...].reshape(D, D), _NT, preferred_element_type=_F32)
        gav = ga_ref[...].astype(_F32)
        sa = _sigmoid(gav)
        do_ref[...] = daa * (gav * sa)
        rest_ref[:, 0:R_U] = (daa * o_ref[...] * (sa * (1.0 + gav * (1.0 - sa)))).astype(rest_ref.dtype)
        gl = gl_ref[...].astype(_F32)
        sl = _sigmoid(gl)
        dh_ref[...] = dba * (gl * sl)
        rest_ref[:, R_GLRU:R_MA] = (dba * h_ref[...] * (sl * (1.0 + gl * (1.0 - sl)))).astype(rest_ref.dtype)

    v = _vec((1, D))
    r5, r10 = _rows(tb, ATT_W), _rows(tb, D)
    return _rowwise("tail_bwd", body, grid=s // tb,
                    ins=[(dx, r10), (out, r10), (y_a, r10), (y_b, r10), (proj, _rows(tb, D, CB_MA)),
                         (proj, _rows(tb, D, CB_MB)), (o, r5), (proj, _rows(tb, ATT_W, CB_GATT)), (h_lru, r10),
                         (proj, _rows(tb, D, CB_GLRU)), (gate, v), (g_post, v)]
                    + list(zip((gw["w_pa"], gw["w_pb"], gw["w_o"]), _weight_specs(l))),
                    outs=[(_sds((s, D), _MXU), r10), (_sds((s, D), _MXU), r10), (_sds((s, D), _MXU), r10),
                          (_sds((s, R_END), _MXU), _rows(tb, R_END)),
                          (_sds((s, ATT_W)), r5), (_sds((s, D)), r10), (_sds((1, D)), v), (_sds((1, D)), v)])


def _scan_bwd(dh, uc, h_lru, wt, b_rg, b_ig, lam):
    s = uc.shape[0]
    tb = 256
    n = s // tb

    def body(dh_ref, uc_ref, h_ref, hp_ref, wt_ref, brg_ref, big_ref, lam_ref,
             duc_ref, dwt_ref, dbrg_ref, dbig_ref, dlam_ref, carry, c_s, g_s):
        i = pl.program_id(0)

        @pl.when(i == 0)
        def _():
            carry[...] = jnp.zeros_like(carry)
            for acc_ref in (dwt_ref, dbrg_ref, dbig_ref, dlam_ref):
                acc_ref[...] = jnp.zeros_like(acc_ref)

        ucv = uc_ref[...]
        pre_r, pre_i = _gate_preacts(ucv, wt_ref)
        r, ig, sp, a, sq, inv_sq =_lru_gates(pre_r, pre_i, ucv, brg_ref[...], big_ref[...], lam_ref[...])
        row = lax.broadcasted_iota(jnp.int32, (tb, 1), 0)
        cv = jnp.where(row == tb - 1, 1.0, pltpu.roll(a, tb - 1, 0))
        gv = dh_ref[...]
        cv = cv.reshape(tb // 8, 8, D)
        gv = gv.reshape(tb // 8, 8, D)
        row8 = lax.broadcasted_iota(jnp.int32, (1, 8, 1), 1)
        for sh in (1, 2, 4):
            m = row8 < 8 - sh
            g_sh = pltpu.roll(gv, 8 - sh, 1)
            c_sh = pltpu.roll(cv, 8 - sh, 1)
            gv = jnp.where(m, gv + cv * g_sh, gv)
            cv = jnp.where(m, cv * c_sh, cv)
        c_s[...] = cv.reshape(tb, D)
        g_s[...] = gv.reshape(tb, D)

        def tile(k, state):
            rows = pl.ds(pl.multiple_of((tb // 8 - 1 - k) * 8, 8), 8)
            gt = g_s[rows, :] + c_s[rows, :] * state
            g_s[rows, :] = gt
            return jnp.broadcast_to(gt[0:1, :], (8, D))

        lax.fori_loop(0, tb // 8, tile, jnp.broadcast_to(carry[0:1, :], (8, D)), unroll=4)
        gv = g_s[...]
        carry[...] = (a * gv)[0:8]

        halo = jnp.where(i < n - 1, hp_ref[...], 0.0)
        h_prev = _shift_down(h_ref[...], halo, 1, tb)
        d_a = gv * h_prev
        d_sq = gv * (ig * ucv)
        d_i = gv * sq * ucv
        d_la = d_a * a - d_sq * (a * a) * inv_sq
        d_r = d_la * (-LRU_C * sp)
        d_pre_r = d_r * r * (1.0 - r)
        d_pre_i = d_i * ig * (1.0 - ig)
        ucb = ucv.astype(_MXU)
        dpr = d_pre_r.astype(_MXU)
        dpi = d_pre_i.astype(_MXU)
        back = []
        for c in range(GATE_TILES):
            lanes = slice(128 * c, 128 * (c + 1))
            dp = jnp.concatenate([dpr[:, lanes], dpi[:, lanes]], axis=1)
            back.append(lax.dot_general(dp, wt_ref[c], _NT, preferred_element_type=_F32))
            dwt_ref[c] += lax.dot_general(ucb[:, lanes], dp, _TN, preferred_element_type=_F32)
        duc_ref[...] = gv * sq * ig + jnp.concatenate(back, axis=1)
        dbrg_ref[...] += jnp.sum(d_pre_r, axis=0, keepdims=True)
        dbig_ref[...] += jnp.sum(d_pre_i, axis=0, keepdims=True)
        lamv = lam_ref[...]
        dlam_ref[...] += jnp.sum(d_la * (-LRU_C * r), axis=0, keepdims=True) * (-_sigmoid(-lamv))

    v = _vec((1, D))
    rv = _rows(tb, D, 0, n)
    return _rowwise("scan_bwd", body, grid=n,
                    ins=[(dh, rv), (uc, rv), (h_lru, rv), (h_lru, _halo_prev(tb, D, 0, n)),
                         (wt, _vec((GATE_TILES, 128, 256))), (b_rg, v), (b_ig, v), (lam, v)],
                    outs=[(_sds((s, D)), rv), (_sds((GATE_TILES, 128, 256)), _vec((GATE_TILES, 128, 256))),
                          (_sds((1, D)), v), (_sds((1, D)), v), (_sds((1, D)), v)],
                    scratch=[pltpu.VMEM((8, D), _F32), pltpu.VMEM((tb, D), _F32), pltpu.VMEM((tb, D), _F32)])


def _conv_bwd(duc_a, proj, conv_w, rest):
    s = duc_a.shape[0]
    tb = 512
    n = s // tb
    hw = D // 2

    def body(da_ref, dan_ref, u_ref, up_ref, w_ref, rest_in, du_ref, dw_ref, dbias_ref):
        i = pl.program_id(1)
        duc = da_ref[...]
        nxt = jnp.where(i < n - 1, dan_ref[...], 0.0)
        u = u_ref[...].astype(_F32)
        halo = jnp.where(i > 0, up_ref[...].astype(_F32)[8:16], 0.0)
        du = duc * w_ref[0:1, :]
        dws = [jnp.sum(duc * u, axis=0, keepdims=True)]
        for j in range(1, 4):
            du = du + _shift_up(duc, nxt, j, tb) * w_ref[j:j + 1, :]
            dws.append(jnp.sum(duc * _shift_down(u, halo, j, tb), axis=0, keepdims=True))
        du_ref[...] = du.astype(du_ref.dtype)
        _zero_first(i, dw_ref, dbias_ref)
        for j in range(4):
            dw_ref[j:j + 1, :] += dws[j]
        dbias_ref[...] += jnp.sum(duc, axis=0, keepdims=True)

    r = pl.BlockSpec((tb, hw), lambda h, i: (i, h))
    nxt_spec = pl.BlockSpec((8, hw), lambda h, i: (jnp.minimum((i + 1) * (tb // 8), n * (tb // 8) - 1), h))
    return pl.pallas_call(
        body, name="conv_bwd", grid=(2, n),
        in_specs=[r, nxt_spec,
                  pl.BlockSpec((tb, hw), lambda h, i: (i, 2 * CB_U + h)),
                  pl.BlockSpec((16, hw), lambda h, i: (jnp.maximum(i * (tb // 16) - 1, 0), 2 * CB_U + h)),
                  pl.BlockSpec((4, hw), lambda h, i: (0, h)), pl.BlockSpec(memory_space=pl.ANY)],
        out_specs=[pl.BlockSpec((tb, hw), lambda h, i: (i, R_U // hw + h)),
                   pl.BlockSpec((4, hw), lambda h, i: (0, h)), pl.BlockSpec((1, hw), lambda h, i: (0, h))],
        out_shape=[_sds(rest.shape, rest.dtype), _sds((4, D)), _sds((1, D))],
        input_output_aliases={5: 0}, compiler_params=_params(2),
    )(duc_a, duc_a, proj, proj, conv_w, rest)


def _band_tiles(dil):
    tiles = []
    for rho in range(dil):
        for b in range(16 // dil):
            qs = rho + dil * BAND * b
            tiles.append((qs, QBLK + qs - dil * BAND, b))
    return tiles


def _strided(start, size, dil):
    return pl.ds(start, size, stride=dil) if dil > 1 else pl.ds(start, size)


def _band_mask(i, b):
    qi = lax.broadcasted_iota(jnp.int32, (BAND, 2 * BAND), 0)
    ki = lax.broadcasted_iota(jnp.int32, (BAND, 2 * BAND), 1)
    valid = (ki >= qi) & (ki <= qi + BAND)
    if b == 0:
        valid = valid & ((ki >= BAND) | (i > 0))
    return valid


def _attn_fwd(proj):
    s = proj.shape[0]
    n = s // QBLK
    scale = HEAD ** -0.5

    def body(*refs):
        q_refs, kp_refs, kc_refs, vp_refs, vc_refs = (refs[3 * t:3 * t + 3] for t in range(5))
        o_ref, lse_ref, qbuf, kbuf, vbuf = refs[15:20]
        accs, maxs, dens = refs[20:23], refs[23:26], refs[26:29]
        i = pl.program_id(1)
        for g, dil in enumerate(DILATIONS):
            qbuf[...] = q_refs[g][...].astype(_F32)
            kbuf[0:QBLK, :] = kp_refs[g][...].astype(_F32)
            kbuf[QBLK:2 * QBLK, :] = kc_refs[g][...].astype(_F32)
            vbuf[0:QBLK, :] = vp_refs[g][...].astype(_F32)
            vbuf[QBLK:2 * QBLK, :] = vc_refs[g][...].astype(_F32)
            for qs, ks, b in _band_tiles(dil):
                qsl = _strided(qs, BAND, dil)
                q = qbuf[qsl, :].astype(_MXU)
                kk = kbuf[_strided(ks, 2 * BAND, dil), :].astype(_MXU)
                vv = vbuf[_strided(ks, 2 * BAND, dil), :].astype(_MXU)
                sc = lax.dot_general(q, kk, _NT, preferred_element_type=_F32) * scale
                sc = jnp.where(_band_mask(i, b), sc, NEG_INF)
                m = jnp.max(sc, axis=-1, keepdims=True)
                p = jnp.exp(sc - m)
                accs[g][qsl, :] = jnp.dot(p.astype(_MXU), vv, preferred_element_type=_F32)
                maxs[g][qsl, :] = jnp.broadcast_to(m, (BAND, HEAD))
                dens[g][qsl, :] = jnp.broadcast_to(jnp.sum(p, axis=-1, keepdims=True), (BAND, HEAD))
        ms = [r[...] for r in maxs]
        mx = jnp.maximum(jnp.maximum(ms[0], ms[1]), ms[2])
        ws = [jnp.exp(m - mx) for m in ms]
        den = ws[0] * dens[0][...] + ws[1] * dens[1][...] + ws[2] * dens[2][...]
        o_ref[...] = (ws[0] * accs[0][...] + ws[1] * accs[1][...] + ws[2] * accs[2][...]) / den
        lse_ref[...] = mx + jnp.log(den)

    blk = (QBLK, HEAD)

    def spec(first_col, lag):
        specs = []
        for g in range(3):
            col = first_col + g * HEADS
            if lag:
                specs.append(pl.BlockSpec(blk, lambda j, i, col=col: (jnp.maximum(i - 1, 0), col + j)))
            else:
                specs.append(pl.BlockSpec(blk, lambda j, i, col=col: (i, col + j)))
        return specs

    out_spec = pl.BlockSpec(blk, lambda j, i: (i, j))
    return pl.pallas_call(
        body, name="attn_fwd", grid=(HEADS, n),
        in_specs=spec(0, False) + spec(12, True) + spec(12, False) + spec(24, True) + spec(24, False),
        out_specs=[out_spec] * 2, out_shape=[_sds((s, ATT_W))] * 2,
        scratch_shapes=[pltpu.VMEM(blk, _F32)] + [pltpu.VMEM((2 * QBLK, HEAD), _F32)] * 2
        + [pltpu.VMEM(blk, _F32)] * 9,
        compiler_params=_params(2))(*([proj] * 15))


def _attn_bwd(proj, d_o, o, lse, g, into):
    s = proj.shape[0]
    dil = DILATIONS[g]
    n = s // QBLK
    scale = HEAD ** -0.5
    tiles = _band_tiles(dil)

    def body(*refs):
        q_ref, kp_ref, kc_ref, vp_ref, vc_ref, do_ref, o_ref, lse_ref = refs[0:8]
        dq_ref, dk_ref, dv_ref, kbuf, vbuf, dkbuf, dvbuf, dqbuf, qbuf = refs[-9:]
        i = pl.program_id(1)

        @pl.when(i == 0)
        def _():
            dkbuf[0:QBLK, :] = jnp.zeros((QBLK, HEAD), _F32)
            dvbuf[0:QBLK, :] = jnp.zeros((QBLK, HEAD), _F32)

        @pl.when(i < n)
        def _():
            qbuf[...] = q_ref[...].astype(_F32)
            kbuf[0:QBLK, :] = kp_ref[...].astype(_F32)
            kbuf[QBLK:2 * QBLK, :] = kc_ref[...].astype(_F32)
            vbuf[0:QBLK, :] = vp_ref[...].astype(_F32)
            vbuf[QBLK:2 * QBLK, :] = vc_ref[...].astype(_F32)
            dkbuf[QBLK:2 * QBLK, :] = jnp.zeros((QBLK, HEAD), _F32)
            dvbuf[QBLK:2 * QBLK, :] = jnp.zeros((QBLK, HEAD), _F32)
            for qs, ks, b in tiles:
                qsl = _strided(qs, BAND, dil)
                ksl = _strided(ks, 2 * BAND, dil)
                q = qbuf[qsl, :].astype(_MXU)
                kk = kbuf[ksl, :].astype(_MXU)
                vv = vbuf[ksl, :].astype(_MXU)
                dov = do_ref[qsl, :]
                dd = jnp.sum(dov * o_ref[qsl, :], axis=-1, keepdims=True)
                lse_t = lse_ref[qsl, :][:, 0:1]
                sc = lax.dot_general(q, kk, _NT, preferred_element_type=_F32) * scale
                p = jnp.where(_band_mask(i, b), jnp.exp(sc - lse_t), 0.0)
                dob = dov.astype(_MXU)
                dp = lax.dot_general(dob, vv, _NT, preferred_element_type=_F32)
                ds = (p * (dp - dd) * scale).astype(_MXU)
                dqbuf[qsl, :] = jnp.dot(ds, kk, preferred_element_type=_F32)
                dkbuf[ksl, :] += lax.dot_general(ds, q, _TN, preferred_element_type=_F32)
                dvbuf[ksl, :] += lax.dot_general(p.astype(_MXU), dob, _TN, preferred_element_type=_F32)
            dq_ref[...] = dqbuf[...].astype(dq_ref.dtype)

        dk_ref[...] = dkbuf[0:QBLK, :].astype(dk_ref.dtype)
        dv_ref[...] = dvbuf[0:QBLK, :].astype(dv_ref.dtype)
        dkbuf[0:QBLK, :] = dkbuf[QBLK:2 * QBLK, :]
        dvbuf[0:QBLK, :] = dvbuf[QBLK:2 * QBLK, :]

    blk = (QBLK, HEAD)
    cq, ck, cv = g * HEADS, 12 + g * HEADS, 24 + g * HEADS

    def cur(i):
        return jnp.minimum(i, n - 1)

    def prev(i):
        return jnp.maximum(jnp.minimum(i, n - 1) - 1, 0)

    own = pl.BlockSpec(blk, lambda j, i: (cur(i), j))
    own_out = pl.BlockSpec(blk, lambda j, i: (cur(i), cq + j))
    late_out = pl.BlockSpec(blk, lambda j, i: (jnp.maximum(i - 1, 0), cq + j))
    extra = [] if into is None else list(into)
    return pl.pallas_call(
        body, name="attn_bwd_d%d" % dil, grid=(HEADS, n + 1),
        in_specs=[pl.BlockSpec(blk, lambda j, i: (cur(i), cq + j)),
                  pl.BlockSpec(blk, lambda j, i: (prev(i), ck + j)),
                  pl.BlockSpec(blk, lambda j, i: (cur(i), ck + j)),
                  pl.BlockSpec(blk, lambda j, i: (prev(i), cv + j)),
                  pl.BlockSpec(blk, lambda j, i: (cur(i), cv + j)),
                  own, own, own] + [pl.BlockSpec(memory_space=pl.ANY)] * len(extra),
        out_specs=[own_out, late_out, late_out], out_shape=[_sds((s, QKV_W), _MXU)] * 3,
        input_output_aliases={8 + t: t for t in range(len(extra))},
        scratch_shapes=[pltpu.VMEM((2 * QBLK, HEAD), _F32)] * 4 + [pltpu.VMEM((QBLK, HEAD), _F32)] * 2,
        compiler_params=_params(2))(proj, proj, proj, proj, proj, d_o, o, lse, *extra)


_PARTS = ((0, 2), (2, 2), (4, 2), (6, 6))
_CHUNK = 768


def _d_x(parts, w_in, x, dx_out, g_pre, scale):
    s = parts[0].shape[0]
    nk = IN_W // _CHUNK

    def body(p0, p1, p2, p3, w_ref, x_ref, dxo_ref, g_ref, sc_ref, dx_ref, dsh_ref, dsc_ref, dg_ref, acc):
        m = pl.program_id(0)
        k = pl.program_id(2)

        @pl.when(k == 0)
        def _():
            acc[...] = jnp.zeros_like(acc)

        @pl.when((k == 0) & (m == 0))
        def _():
            for ref in (dsh_ref, dsc_ref, dg_ref):
                ref[...] = jnp.zeros_like(ref)

        for p_ref, (first, cnt) in zip((p0, p1, p2, p3), _PARTS):
            @pl.when((k >= first) & (k < first + cnt))
            def _(p_ref=p_ref):
                acc[...] += lax.dot_general(p_ref[...].astype(_MXU), w_ref[...], _NT, preferred_element_type=_F32)

        @pl.when(k == nk - 1)
        def _():
            dhv = acc[...]
            xv = x_ref[...]
            rstd = lax.rsqrt(jnp.mean(xv * xv, axis=-1, keepdims=True) + NORM_EPS)
            xn = xv * rstd
            one_sc = 1.0 + sc_ref[...]
            s1 = jnp.sum(dhv * xn, axis=0, keepdims=True)
            dsh_ref[...] += jnp.sum(dhv, axis=0, keepdims=True)
            dsc_ref[...] += s1 * g_ref[...]
            dg_ref[...] += s1 * one_sc
            dxn = dhv * (g_ref[...] * one_sc)
            dx_ref[...] = dxo_ref[...] + rstd * (dxn - xn * jnp.mean(dxn * xn, axis=-1, keepdims=True))

    def part_spec(first, cnt):
        return pl.BlockSpec((1024, _CHUNK), lambda m, n, k: (m, jnp.clip(k - first, 0, cnt - 1)))

    rows = pl.BlockSpec((1024, D), lambda m, n, k: (m, 0))
    vec = pl.BlockSpec((1, D), lambda m, n, k: (0, 0))
    return pl.pallas_call(
        body, name="d_x", grid=(s // 1024, 1, nk),
        in_specs=[part_spec(*p) for p in _PARTS]
        + [pl.BlockSpec((None, D, _CHUNK), lambda m, n, k: (k // 3, 0, k % 3)), rows, rows, vec, vec],
        out_specs=[rows, vec, vec, vec], out_shape=[_sds((s, D)), _sds((1, D)), _sds((1, D)), _sds((1, D))],
        scratch_shapes=[pltpu.VMEM((1024, D), _F32)], compiler_params=_params(3))(
            *parts, w_in, x, dx_out, g_pre, scale)


def _g_w_in(h_t, parts):
    s = h_t.shape[1]
    nk = s // 1024

    def body(*refs):
        h_ref, p_refs = refs[0], refs[1:5]
        o_ref, acc = refs[-2], refs[-1]
        n = pl.program_id(1)
        k = pl.program_id(2)

        @pl.when(k == 0)
        def _():
            acc[...] = jnp.zeros_like(acc)

        for p_ref, (first, cnt) in zip(p_refs, _PARTS):
            @pl.when((n >= first) & (n < first + cnt))
            def _(p_ref=p_ref):
                acc[...] += jnp.dot(h_ref[...], p_ref[...].astype(_MXU), preferred_element_type=_F32)

        @pl.when(k == nk - 1)
        def _():
            o_ref[...] = acc[...]

    def part_spec(first, cnt):
        def index(m, n, k):
            row = jnp.where(n < first, 0, jnp.where(n >= first + cnt, nk - 1, k))
            return (row, jnp.clip(n - first, 0, cnt - 1))
        return pl.BlockSpec((1024, _CHUNK), index)

    return pl.pallas_call(
        body, name="g_w_in", grid=(1, IN_W // _CHUNK, nk),
        in_specs=[pl.BlockSpec((D, 1024), lambda m, n, k: (0, k))] + [part_spec(*p) for p in _PARTS],
        out_specs=pl.BlockSpec((None, D, _CHUNK), lambda m, n, k: (n // 3, 0, n % 3)),
        out_shape=_sds((N_CHIPS, D, 2304)),
        scratch_shapes=[pltpu.VMEM((D, _CHUNK), _F32)], compiler_params=_params(3))(h_t, *parts)


def _layer_fwd(l, x, p, gw, late, target):
    proj, h_t = _proj(x, p["g_pre"], p["shift"], p["scale"], gw["w_in"][l])
    o, lse = _attn_fwd(proj)
    uc = _conv_fwd(proj, p["conv_w"], p["conv_b"])
    h_lru = _scan_fwd(uc, p["wt"], p["b_rg"], p["b_ig"], p["lam"])
    if late is not None:
        landed = dict(late(h_lru))
        gw["w_in"].append(landed.pop("w_in1"))
        gw.update(landed)
    a_att, b_act, y_a, y_b, z, out, *last = _tail_fwd(l, o, h_lru, proj, x, p["gate"], p["g_post"], gw, target)
    saved = dict(x=x, h_t=h_t, proj=proj, o=o, lse=lse, uc=uc, h_lru=h_lru, a_att=a_att, b_act=b_act,
                 y_a=y_a, y_b=y_b, z=z, out=out)
    return (last[0] if target is None else last), saved


def _layer_bwd(l, dx, p, gw, sv, hooks):
    s = dx.shape[0]
    nt = s // 2048
    proj = sv["proj"]
    gate, b_rg, g_pre = p["gate"], p["b_rg"], p["g_pre"]
    if hooks is not None:
        gate = gate + hooks[0]([dx])
    d_out, dy_a, dy_b, d_rest, d_o, dh_lru, d_gate, d_gpost = _tail_bwd(
        l, dx, sv["out"], sv["y_a"], sv["y_b"], proj, sv["o"], sv["h_lru"], gate, p["g_post"], gw)
    if hooks is not None:
        b_rg = b_rg + hooks[1]([d_out])

    def wgrad_rows(name, a, b):
        return _mm(name, a, b, _sds((N_CHIPS, 256, D)), grid=(4, 1, nt),
                   a_spec=pl.BlockSpec((2048, 256), lambda m, n, k: (k, m)),
                   b_spec=pl.BlockSpec((2048, D), lambda m, n, k: (k, 0)),
                   o_spec=pl.BlockSpec((None, 256, D), lambda m, n, k: (m, 0, 0)),
                   dims=_TN, acc_shape=(256, D))

    big = {}
    big["w_o"] = wgrad_rows("g_w_o", sv["z"], d_out)
    big["w_pa"] = _mm("g_w_pa", sv["a_att"], dy_a, _sds((N_CHIPS, ATT_W, 256)), grid=(1, 4, nt),
                      a_spec=pl.BlockSpec((2048, ATT_W), lambda m, n, k: (k, 0)),
                      b_spec=pl.BlockSpec((2048, 256), lambda m, n, k: (k, n)),
                      o_spec=pl.BlockSpec((None, ATT_W, 256), lambda m, n, k: (n, 0, 0)),
                      dims=_TN, acc_shape=(ATT_W, 256))
    big["w_pb"] = wgrad_rows("g_w_pb", sv["b_act"], dy_b)
    duc, g_wt, d_brg, d_big, d_lam = _scan_bwd(dh_lru, sv["uc"], sv["h_lru"], p["wt"], b_rg, p["b_ig"], p["lam"])
    g_wrg, g_wig = _gate_tile_grads(g_wt)
    d_rest, g_convw, g_convb = _conv_bwd(duc, proj, p["conv_w"], d_rest)
    dqkv = None
    for g in range(3):
        dqkv = _attn_bwd(proj, d_o, sv["o"], sv["lse"], g, dqkv)
    if hooks is not None:
        g_pre = g_pre + hooks[2]([dqkv[0]])
    parts = (dqkv[0], dqkv[1], dqkv[2], d_rest)
    big["w_in"] = _g_w_in(sv["h_t"], parts)
    if hooks is not None:
        g_pre = g_pre + hooks[3](big)
    dx_in, d_shift, d_scale, d_gpre = _d_x(parts, gw["w_in"][l], sv["x"], dx, g_pre, p["scale"])
    small = dict(dmod=jnp.concatenate([d_shift, d_scale, d_gate], axis=1), g_pre=d_gpre, conv_w=g_convw,
                 conv_b=g_convb, w_rg=g_wrg, b_rg=d_brg, w_ig=g_wig, b_ig=d_big, lam=d_lam, g_post=d_gpost)
    return dx_in, small, big


_BIG = ("w_in", "w_pa", "w_pb", "w_o")


class _GradReduce:
    PAIR_CHUNKS = (2, 1, 1, 1)
    CHIP_CHUNKS = (2, 1, 1, 1)
    FILL_CHUNKS = (4, 1, 1, 1)

    def __init__(self, core, where):
        self.core, self.where = core, where
        self.finals = None

    def begin(self, l, big):
        n = len(_BIG)
        halves = [big[k].reshape(N_CHIPS, 2, big[k].shape[1] // 2, big[k].shape[2]) for k in _BIG]
        lands = [lax.empty((N_CHIPS,) + h.shape[2:], _F32) for h in halves]
        plan, nsem = _pair_plan(n, self.PAIR_CHUNKS)
        state = {}
        state["pair"] = _split_start("reduce_pair_start_%d" % l, halves + lands, plan, nsem, [])

        def started(after):
            return state["pair"][3][0, 0]

        def pair_done(after):
            send, recv, arrays, _ = state["pair"]
            arrays = _split_wait("reduce_pair_wait_%d" % l, send, recv, arrays, plan, after)
            sums = [_sum_pair("sum_pair_%s_%d" % (k, l), arrays[a], arrays[n + a], self.core, 128)
                    for a, k in enumerate(_BIG)]
            state["mine"] = [t[0] for t in sums]
            lands2 = [lax.empty(t[1].shape, _MXU) for t in sums]
            plan2, nsem2 = _chips_plan(n, self.CHIP_CHUNKS)
            state["plan2"] = plan2
            state["chips"] = _split_start("reduce_chips_start_%d" % l, [t[1] for t in sums] + lands2, plan2, nsem2, [])
            return state["chips"][3][0, 0]

        def chips_done(after):
            send, recv, arrays, _ = state["chips"]
            arrays = _split_wait("reduce_chips_wait_%d" % l, send, recv, arrays, state["plan2"], after)
            finals = [_sum_chips("sum_chips_%s_%d" % (k, l), state["mine"][a], arrays[n + a], self.where, l,
                                 None if self.finals is None else self.finals[a], 128)
                      for a, k in enumerate(_BIG)]
            plan3, nsem3 = _fill_plan(n, self.FILL_CHUNKS, l)
            state["plan3"] = plan3
            state["fill"] = _split_start("gather_halves_start_%d" % l, finals, plan3, nsem3, [])
            return state["fill"][3][0, 0]

        def finish(after):
            send, recv, arrays, _ = state["fill"]
            self.finals = _split_wait("gather_halves_wait_%d" % l, send, recv, arrays, state["plan3"], after)
            return self.finals

        self._finish = finish
        return [started, pair_done, chips_done]

    def finish(self, after):
        return self._finish(after)


def _local_step(x, target, small_p, w_in0, late, reducer):
    saved = []
    h = x
    gw = dict(w_in=[w_in0])
    h, sv = _layer_fwd(0, h, small_p[0], gw, late, None)
    saved.append(sv)
    (dy, sq), sv = _layer_fwd(1, h, small_p[1], gw, None, target)
    saved.append(sv)
    loss = 0.5 * jnp.sum(sq) / D
    smalls = [None, None]
    dx, smalls[1], big1 = _layer_bwd(1, dy, small_p[1], gw, saved[1], None)
    hooks1 = reducer.begin(1, big1)
    own = {}

    def layer0_ready(big0):
        reducer.finish([big0["w_in"]])
        own["hooks"] = reducer.begin(0, big0)
        return own["hooks"][0]([])

    dx, smalls[0], _ = _layer_bwd(0, dx, small_p[0], gw, saved[0], hooks1 + [layer0_ready])
    own["hooks"][1]([dx])

    def finish_reduce(after):
        own["hooks"][2](after)
        return reducer.finish(after)

    return loss, dx, smalls, finish_reduce


_SMALL_ROWS = 8 + 16 + 8 + 128 + 128


def _pack_small(smalls):
    dmod = jnp.concatenate([smalls[0]["dmod"].reshape(3, D), smalls[1]["dmod"].reshape(3, D),
                            jnp.zeros((2, D), _F32)], axis=0)
    vecs = jnp.concatenate([smalls[l][k] for k in ("g_pre", "conv_b", "b_rg", "b_ig", "lam", "g_post")
                            for l in range(2)] + [jnp.zeros((4, D), _F32)], axis=0)
    convw = jnp.concatenate([smalls[0]["conv_w"], smalls[1]["conv_w"]], axis=0)
    wrg = jnp.stack([smalls[0]["w_rg"], smalls[1]["w_rg"]]).reshape(128, D)
    wig = jnp.stack([smalls[0]["w_ig"], smalls[1]["w_ig"]]).reshape(128, D)
    return jnp.concatenate([dmod, vecs, convw, wrg, wig], axis=0)


def kernel(x, c, w_mod, b_mod, g_pre, w_in, conv_w, conv_b, w_rg, b_rg, w_ig, b_ig, lru_lambda, w_pa, w_pb, w_o, g_post, loss_target, m_w_mod, m_b_mod, m_g_pre, m_w_in, m_conv_w, m_conv_b, m_w_rg, m_b_rg, m_w_ig, m_b_ig, m_lru_lambda, m_w_pa, m_w_pb, m_w_o, m_g_post, v_w_mod, v_b_mod, v_g_pre, v_w_in, v_conv_w, v_conv_b, v_w_rg, v_b_rg, v_w_ig, v_b_ig, v_lru_lambda, v_w_pa, v_w_pb, v_w_o, v_g_post):
    xi, yi, ci = lax.axis_index("x"), lax.axis_index("y"), lax.axis_index("c")
    chip = 2 * xi + yi
    dev = 4 * xi + 2 * yi + ci
    mcols = w_mod.shape[2]

    pack1 = jnp.concatenate([jnp.broadcast_to(c, (8, D)),
                             jnp.pad(conv_w.reshape(8, 256), ((0, 0), (0, D - 256)))], axis=0)
    g1 = _exchange("gather_cond", [pack1], "xyc", False)[0]
    c_all = g1[:, 0, :]
    conv_w_full = jnp.transpose(g1[0::2, 8:16, 0:256], (1, 0, 2)).reshape(2, 4, D)

    b_cols = lax.dynamic_slice(b_mod, (0, chip * mcols), (2, mcols)).reshape(2, 1, mcols)
    mod_loc = _mod_fwd(c_all, w_mod, b_cols)
    g2 = _exchange("gather_mod", [mod_loc.reshape(16, mcols)], "xyc", False)[0]
    mod_full = jnp.transpose(g2[0::2], (1, 0, 2)).reshape(2, 8, 3 * D)
    mod_me = lax.dynamic_index_in_dim(mod_full, dev, axis=1, keepdims=False)

    wb_in = _cast("cast_w_in", w_in.reshape(2 * D, 2304), 256).reshape(2, D, 2304)
    late_src = [wb_in[1], _cast("cast_w_pa", w_pa.reshape(2 * ATT_W, 256), 256).reshape(2, ATT_W, 256),
                _cast("cast_w_pb", w_pb.reshape(512, D), 256).reshape(2, 256, D),
                _cast("cast_w_o", w_o.reshape(512, D), 256).reshape(2, 256, D)]
    late_chunks = [4, 2, 2, 2]
    w_in0 = _gather_weights([wb_in[0].reshape(2, D // 2, 2304)], [2])[0].reshape(N_CHIPS, D, 2304)
    chip1 = jnp.reshape(chip, (1,)).astype(jnp.int32)
    lands = [_own_slot("own_slot_" + k, a, chip1, 256) for k, a in zip(("w_in", "w_pa", "w_pb", "w_o"), late_src)]
    late_plan, late_nsem = _gather_plan(len(late_src), late_chunks)
    send_sems, recv_sems, late_arrays, token = _split_start(
        "late_gather_start", late_src + lands, late_plan, late_nsem, [w_in0, mod_me])

    def late(after):
        got = _split_wait("late_gather_wait", send_sems, recv_sems, late_arrays, late_plan, [after])[len(late_src):]
        return dict(w_in1=got[0], w_pa=got[1], w_pb=got[2], w_o=got[3])

    small_p = []
    for l in range(2):
        gates = _gate_tiles(w_rg[l], w_ig[l]).astype(_MXU)
        small_p.append(dict(
            shift=mod_me[l:l + 1, 0:D], scale=mod_me[l:l + 1, D:2 * D], gate=mod_me[l:l + 1, 2 * D:3 * D],
            g_pre=g_pre[l:l + 1], conv_w=conv_w_full[l], conv_b=conv_b[l:l + 1], wt=gates,
            b_rg=b_rg[l:l + 1], b_ig=b_ig[l:l + 1], lam=lru_lambda[l:l + 1], g_post=g_post[l:l + 1]))

    small_p[0]["shift"] = small_p[0]["shift"] + token[0, 0]

    core = jnp.reshape(ci, (1,)).astype(jnp.int32)
    where = jnp.stack([chip, ci]).astype(jnp.int32)
    loss_loc, dx, smalls, finish_reduce = _local_step(x[0], loss_target[0], small_p, w_in0, late,
                                                      _GradReduce(core, where))
    loss = lax.psum(loss_loc, ("x", "y", "c"))
    grad_x = dx[None]
    reduced = finish_reduce([dx])
    g_big ={k: a.reshape(2, 2 * a.shape[2], a.shape[3]) for k, a in zip(_BIG, reduced)}

    g3 = _exchange("gather_small", [_pack_small(smalls)], "xyc", False)[0]
    tot = _sum_lead("sum_small", g3, 96)
    dmod_all = g3[:, 0:6, :].reshape(8, 2, 3 * D)
    dm_cols = jnp.transpose(lax.dynamic_slice(dmod_all, (0, 0, chip * mcols), (8, 2, mcols)), (1, 0, 2))
    g_w_mod = _mod_bwd(jnp.transpose(c_all), dm_cols)
    vec = tot[8:20].reshape(6, 2, D)
    g_conv_w_full = tot[24:32].reshape(2, 4, D)
    grads = dict(
        w_mod=g_w_mod, b_mod=tot[0:6].reshape(2, 3 * D), g_pre=vec[0], w_in=g_big["w_in"],
        conv_w=lax.dynamic_slice(g_conv_w_full, (0, 0, chip * 256), (2, 4, 256)), conv_b=vec[1],
        w_rg=tot[32:160].reshape(2, 16, 64, 64), b_rg=vec[2], w_ig=tot[160:288].reshape(2, 16, 64, 64),
        b_ig=vec[3], lru_lambda=vec[4], w_pa=g_big["w_pa"], w_pb=g_big["w_pb"], w_o=g_big["w_o"],
        g_post=vec[5])

    weights = dict(w_mod=w_mod, b_mod=b_mod, g_pre=g_pre, w_in=w_in, conv_w=conv_w, conv_b=conv_b, w_rg=w_rg,
                   b_rg=b_rg, w_ig=w_ig, b_ig=b_ig, lru_lambda=lru_lambda, w_pa=w_pa, w_pb=w_pb, w_o=w_o,
                   g_post=g_post)
    ms = dict(w_mod=m_w_mod, b_mod=m_b_mod, g_pre=m_g_pre, w_in=m_w_in, conv_w=m_conv_w, conv_b=m_conv_b,
              w_rg=m_w_rg, b_rg=m_b_rg, w_ig=m_w_ig, b_ig=m_b_ig, lru_lambda=m_lru_lambda, w_pa=m_w_pa,
              w_pb=m_w_pb, w_o=m_w_o, g_post=m_g_post)
    vs = dict(w_mod=v_w_mod, b_mod=v_b_mod, g_pre=v_g_pre, w_in=v_w_in, conv_w=v_conv_w, conv_b=v_conv_b,
              w_rg=v_w_rg, b_rg=v_b_rg, w_ig=v_w_ig, b_ig=v_b_ig, lru_lambda=v_lru_lambda, w_pa=v_w_pa,
              w_pb=v_w_pb, w_o=v_w_o, g_post=v_g_post)
    flat = dict(w_mod=(2 * D, mcols, 256), b_mod=(2, 3 * D, 2), g_pre=(2, D, 2), w_in=(2 * D, 2304, 256),
                conv_w=(8, 256, 8), conv_b=(2, D, 2), w_rg=(128, D, 128), b_rg=(2, D, 2), w_ig=(128, D, 128),
                b_ig=(2, D, 2), lru_lambda=(2, D, 2), w_pa=(2 * ATT_W, 256, 256), w_pb=(512, D, 256),
                w_o=(512, D, 256), g_post=(2, D, 2))
    order = ("w_mod", "b_mod", "g_pre", "w_in", "conv_w", "conv_b", "w_rg", "b_rg", "w_ig", "b_ig",
             "lru_lambda", "w_pa", "w_pb", "w_o", "g_post")
    deltas, new_m, new_v = [], [], []
    for k in order:
        rows, cols, tb = flat[k]
        shp = weights[k].shape
        d, nm_, nv_ = _adamw("adamw_" + k, weights[k].reshape(rows, cols), grads[k].reshape(rows, cols),
                             ms[k].reshape(rows, cols), vs[k].reshape(rows, cols), tb)
        deltas.append(d.reshape(shp))
        new_m.append(nm_.reshape(shp))
        new_v.append(nv_.reshape(shp))
    return (loss, grad_x, *[grads[k].reshape(weights[k].shape) for k in order], *deltas, *new_m, *new_v)
```

```python
import functools

import jax
import jax.numpy as jnp
from jax import lax
from jax.experimental import pallas as pl
from jax.experimental.pallas import tpu as pltpu

_F32 = jnp.float32
_MXU = jnp.bfloat16
_VMEM_LIMIT = 56 * 1024 * 1024
_MESH = pl.DeviceIdType.MESH

D = 1024
HEAD = 128
HEADS = 4
ATT_W = 512
QKV_W = 1536
IN_W = 9216
DILATIONS = (1, 4, 16)
BAND = 128
QBLK = BAND * 16
NORM_EPS = 1e-6
NEG_INF = -1e30
LRU_C = 8.0
N_CHIPS = 4
CB_GATT = 4608 // 512
CB_U, CB_GLRU, CB_MA, CB_MB = 5, 6, 7, 8
R_U, R_GLRU, R_MA, R_MB, R_END = 512, 1536, 2560, 3584, 4608

ADAM_LR, ADAM_B1, ADAM_B2, ADAM_EPS, ADAM_WD, ADAM_STEP = 0.001, 0.9, 0.999, 1e-08, 0.01, 10


def _params(ngrid):
    return pltpu.CompilerParams(dimension_semantics=("arbitrary",) * ngrid, vmem_limit_bytes=_VMEM_LIMIT)


def _sigmoid(v):
    return 0.5 * jnp.tanh(0.5 * v) + 0.5


_GROUPS = {
    "c": [(0, 0, 1)],
    "xy": [(1, 0, 0), (0, 1, 0), (1, 1, 0)],
    "xyc": [(0, 0, 1), (0, 1, 0), (0, 1, 1), (1, 0, 0), (1, 0, 1), (1, 1, 0), (1, 1, 1)],
}


def _rank(group, px, py, pc):
    if group == "c":
        return pc
    if group == "xy":
        return 2 * px + py
    return 4 * px + 2 * py + pc


def _flip(rel, x, y, c):
    dx, dy, dc = rel
    return (1 - x if dx else x, 1 - y if dy else y, 1 - c if dc else c)


def _pieces(ref, nchunk):
    step = ref.shape[0] // nchunk
    return [ref.at[pl.ds(q * step, step)] for q in range(nchunk)]


def _exchange(name, srcs, group, scatter, *, local=True, nchunks=None):
    rels = _GROUPS[group]
    gsize = len(rels) + 1
    n = len(srcs)
    nchunks = nchunks or [1] * n
    blks = [s.shape[1:] if scatter else s.shape for s in srcs]
    slotted = local or gsize > 2
    base = [sum(nchunks[:a]) for a in range(n)]
    tot = sum(nchunks)

    def body(*refs):
        src_refs, out_refs = refs[:n], refs[n:2 * n]
        send_sems, recv_sems, loc_sems = refs[2 * n:]
        x, y, c = lax.axis_index("x"), lax.axis_index("y"), lax.axis_index("c")
        me = _rank(group, x, y, c)
        copies = []
        for a in range(n):
            def part(r, a=a):
                return src_refs[a].at[r] if scatter else src_refs[a]
            dst = out_refs[a].at[me] if slotted else out_refs[a]
            if local:
                for q, (s_, d_) in enumerate(zip(_pieces(part(me), nchunks[a]), _pieces(dst, nchunks[a]))):
                    loc = pltpu.make_async_copy(s_, d_, loc_sems.at[base[a] + q])
                    loc.start()
                    copies.append(loc)
            for k, rel in enumerate(rels):
                peer = _flip(rel, x, y, c)
                for q, (s_, d_) in enumerate(zip(_pieces(part(_rank(group, *peer)), nchunks[a]),
                                                 _pieces(dst, nchunks[a]))):
                    cp = pltpu.make_async_remote_copy(
                        src_ref=s_, dst_ref=d_, send_sem=send_sems.at[(base[a] + q) * len(rels) + k],
                        recv_sem=recv_sems.at[(base[a] + q) * len(rels) + k],
                        device_id=peer, device_id_type=_MESH)
                    cp.start()
                    copies.append(cp)
        for cp in copies:
            cp.wait()

    any_spec = pl.BlockSpec(memory_space=pl.ANY)
    lead = (gsize,) if slotted else ()
    return pl.pallas_call(
        body, name=name,
        out_shape=[jax.ShapeDtypeStruct(lead + tuple(b), s.dtype) for b, s in zip(blks, srcs)],
        in_specs=[any_spec] * n, out_specs=[any_spec] * n,
        scratch_shapes=[pltpu.SemaphoreType.DMA((tot * len(rels),)), pltpu.SemaphoreType.DMA((tot * len(rels),)),
                        pltpu.SemaphoreType.DMA((tot,))],
    )(*srcs)


def _pair_fill(name, arrs, nchunks):
    n = len(arrs)
    base = [sum(nchunks[:a]) for a in range(n)]
    tot = sum(nchunks)

    def body(*refs):
        out_refs = refs[n:2 * n]
        send_sems, recv_sems = refs[2 * n:]
        x, y, c = lax.axis_index("x"), lax.axis_index("y"), lax.axis_index("c")
        copies = []
        for a in range(n):
            for q, blk in enumerate(_pieces(out_refs[a].at[c], nchunks[a])):
                cp = pltpu.make_async_remote_copy(
                    src_ref=blk, dst_ref=blk, send_sem=send_sems.at[base[a] + q], recv_sem=recv_sems.at[base[a] + q],
                    device_id=(x, y, 1 - c), device_id_type=_MESH)
                cp.start()
                copies.append(cp)
        for cp in copies:
            cp.wait()

    any_spec = pl.BlockSpec(memory_space=pl.ANY)
    return pl.pallas_call(
        body, name=name, out_shape=[jax.ShapeDtypeStruct(a.shape, a.dtype) for a in arrs],
        in_specs=[any_spec] * n, out_specs=[any_spec] * n, input_output_aliases={a: a for a in range(n)},
        scratch_shapes=[pltpu.SemaphoreType.DMA((tot,)), pltpu.SemaphoreType.DMA((tot,))],
    )(*arrs)


def _gather_weights(wb, nchunks):
    n = len(wb)
    rels = _GROUPS["xy"]
    base = [sum(nchunks[:a]) for a in range(n)]
    tot = sum(nchunks)

    def body(*refs):
        src_refs, out_refs = refs[:n], refs[n:2 * n]
        ici_send, ici_recv, d2d_send, d2d_recv, loc_sems = refs[2 * n:]
        x, y, c = lax.axis_index("x"), lax.axis_index("y"), lax.axis_index("c")
        me = 2 * x + y
        waits = []
        for a in range(n):
            for l in range(2):
                for q, (s_, d_) in enumerate(zip(_pieces(src_refs[a].at[l], nchunks[a]),
                                                 _pieces(out_refs[a].at[me, l], nchunks[a]))):
                    loc = pltpu.make_async_copy(s_, d_, loc_sems.at[(base[a] + q) * 2 + l])
                    loc.start()
                    waits.append(loc)
        first = []
        for a in range(n):
            for k, rel in enumerate(rels):
                px, py, _ = _flip(rel, x, y, c)
                for q, (s_, d_) in enumerate(zip(_pieces(src_refs[a].at[c], nchunks[a]),
                                                 _pieces(out_refs[a].at[me, c], nchunks[a]))):
                    sem = (base[a] + q) * 3 + k
                    cp = pltpu.make_async_remote_copy(src_ref=s_, dst_ref=d_, send_sem=ici_send.at[sem],
                                                      recv_sem=ici_recv.at[sem], device_id=(px, py, c),
                                                      device_id_type=_MESH)
                    cp.start()
                    first.append(cp)
        second = []
        for a in range(n):
            for k, rel in enumerate(rels):
                px, py, _ = _flip(rel, x, y, c)
                for q, blk in enumerate(_pieces(out_refs[a].at[2 * px + py, c], nchunks[a])):
                    sem = (base[a] + q) * 3 + k
                    landed = pltpu.make_async_remote_copy(src_ref=blk, dst_ref=blk, send_sem=ici_send.at[sem],
                                                          recv_sem=ici_recv.at[sem], device_id=(px, py, c),
                                                          device_id_type=_MESH)
                    landed.wait_recv()
                    cp = pltpu.make_async_remote_copy(src_ref=blk, dst_ref=blk, send_sem=d2d_send.at[sem],
                                                      recv_sem=d2d_recv.at[sem], device_id=(x, y, 1 - c),
                                                      device_id_type=_MESH)
                    cp.start()
                    second.append(cp)
        for cp in first:
            cp.wait_send()
        for cp in second:
            cp.wait_send()
        for a in range(n):
            for k, rel in enumerate(rels):
                px, py, _ = _flip(rel, x, y, c)
                for q, blk in enumerate(_pieces(out_refs[a].at[2 * px + py, 1 - c], nchunks[a])):
                    sem = (base[a] + q) * 3 + k
                    pltpu.make_async_remote_copy(src_ref=blk, dst_ref=blk, send_sem=d2d_send.at[sem],
                                                 recv_sem=d2d_recv.at[sem], device_id=(x, y, 1 - c),
                                                 device_id_type=_MESH).wait_recv()
        for cp in waits:
            cp.wait()

    any_spec = pl.BlockSpec(memory_space=pl.ANY)
    return pl.pallas_call(
        body, name="gather_weights",
        out_shape=[jax.ShapeDtypeStruct((N_CHIPS,) + a.shape, a.dtype) for a in wb],
        in_specs=[any_spec] * n, out_specs=[any_spec] * n,
        scratch_shapes=[pltpu.SemaphoreType.DMA((tot * 3,))] * 4 + [pltpu.SemaphoreType.DMA((tot * 2,))],
    )(*wb)


_HBM = pl.BlockSpec(memory_space=pltpu.HBM)
_SEM = pl.BlockSpec(memory_space=pltpu.SEMAPHORE)
_EFFECT = pltpu.SideEffectType.DATAFLOW_SIDE_EFFECTING


def _own_slot(name, src, chip, tb):
    rows, cols = src.shape[-2:]
    lead = src.shape[:-2]
    flat = src.reshape((-1, cols))

    def body(s_ref, a_ref, o_ref):
        o_ref[...] = a_ref[...]

    grid_spec = pltpu.PrefetchScalarGridSpec(
        num_scalar_prefetch=1, grid=(flat.shape[0] // tb,),
        in_specs=[pl.BlockSpec((tb, cols), lambda i, s: (i, 0))],
        out_specs=pl.BlockSpec((None, tb, cols), lambda i, s: (s[0], i, 0)))
    out = pl.pallas_call(body, name=name, grid_spec=grid_spec,
                         out_shape=jax.ShapeDtypeStruct((N_CHIPS,) + flat.shape, src.dtype),
                         compiler_params=_params(1))(chip, flat)
    return out.reshape((N_CHIPS,) + lead + (rows, cols))


def _numbered(pairs, peer, send_sems, recv_sems, first):
    return [pltpu.make_async_remote_copy(src_ref=s_, dst_ref=d_, send_sem=send_sems.at[first + q],
                                         recv_sem=recv_sems.at[first + q], device_id=peer, device_id_type=_MESH)
            for q, (s_, d_) in enumerate(pairs)]


def _gather_plan(n, nchunks):
    def plan(refs, send_sems, recv_sems):
        x, y, c = lax.axis_index("x"), lax.axis_index("y"), lax.axis_index("c")
        me = 2 * x + y
        copies = []
        for a in range(n):
            for rel in _GROUPS["xy"]:
                px, py, _ = _flip(rel, x, y, c)
                pairs = list(zip(_pieces(refs[a], nchunks[a]), _pieces(refs[n + a].at[me], nchunks[a])))
                copies += _numbered(pairs, (px, py, c), send_sems, recv_sems, len(copies))
        return copies
    return plan, 3 * sum(nchunks)


def _pair_plan(n, nchunks):
    def plan(refs, send_sems, recv_sems):
        x, y, c = lax.axis_index("x"), lax.axis_index("y"), lax.axis_index("c")
        copies = []
        for a in range(n):
            for j in range(N_CHIPS):
                pairs = list(zip(_pieces(refs[a].at[j, 1 - c], nchunks[a]), _pieces(refs[n + a].at[j], nchunks[a])))
                copies += _numbered(pairs, (x, y, 1 - c), send_sems, recv_sems, len(copies))
        return copies
    return plan, N_CHIPS * sum(nchunks)


def _chips_plan(n, nchunks):
    def plan(refs, send_sems, recv_sems):
        x, y, c = lax.axis_index("x"), lax.axis_index("y"), lax.axis_index("c")
        me = 2 * x + y
        copies = []
        for a in range(n):
            for rel in _GROUPS["xy"]:
                px, py, _ = _flip(rel, x, y, c)
                pairs = list(zip(_pieces(refs[a].at[2 * px + py], nchunks[a]), _pieces(refs[n + a].at[me], nchunks[a])))
                copies += _numbered(pairs, (px, py, c), send_sems, recv_sems, len(copies))
        return copies
    return plan, 3 * sum(nchunks)


def _fill_plan(n, nchunks, l):
    def plan(refs, send_sems, recv_sems):
        x, y, c = lax.axis_index("x"), lax.axis_index("y"), lax.axis_index("c")
        copies = []
        for a in range(n):
            blk = _pieces(refs[a].at[l, c], nchunks[a])
            copies += _numbered(list(zip(blk, blk)), (x, y, 1 - c), send_sems, recv_sems, len(copies))
        return copies
    return plan, sum(nchunks)


def _split_start(name, arrays, plan, nsem, after):
    n = len(arrays)
    na = len(after)

    def body(*refs):
        send_sems, recv_sems = refs[n + na], refs[n + na + 1]
        token = refs[-1]
        for cp in plan(refs[:n], send_sems, recv_sems):
            cp.start()
        token[...] = jnp.zeros_like(token)

    hbm = [pltpu.HBM(a.shape, a.dtype) for a in arrays]
    outs = pl.pallas_call(
        body, name=name,
        out_shape=(pltpu.SemaphoreType.DMA((nsem,)), pltpu.SemaphoreType.DMA((nsem,)), *hbm, _sds((8, 128))),
        in_specs=[_HBM] * n + [pl.BlockSpec(memory_space=pl.ANY)] * na,
        out_specs=(_SEM, _SEM, *([_HBM] * n), pl.BlockSpec(memory_space=pltpu.VMEM)),
        input_output_aliases={i: 2 + i for i in range(n)},
        compiler_params=pltpu.CompilerParams(has_side_effects=_EFFECT),
    )(*[pltpu.with_memory_space_constraint(a, pltpu.HBM) for a in arrays], *after)
    return outs[0], outs[1], list(outs[2:2 + n]), outs[-1]


def _split_wait(name, send_sems, recv_sems, arrays, plan, after):
    n = len(arrays)

    def body(*refs):
        for cp in plan(refs[:n], refs[n], refs[n + 1]):
            cp.wait_send()
            cp.wait_recv()

    hbm = [pltpu.HBM(a.shape, a.dtype) for a in arrays]
    return list(pl.pallas_call(
        body, name=name, out_shape=tuple(hbm),
        in_specs=[_HBM] * n + [_SEM, _SEM] + [pl.BlockSpec(memory_space=pl.ANY)] * len(after),
        out_specs=tuple([_HBM] * n), input_output_aliases={i: i for i in range(n)},
        compiler_params=pltpu.CompilerParams(has_side_effects=_EFFECT),
    )(*arrays, send_sems, recv_sems, *after))


def _mm(name, a, b, out_sds, *, grid, a_spec, b_spec, o_spec, dims, acc_shape, into=None):
    nk = grid[2]

    def body(*refs):
        a_ref, b_ref = refs[0], refs[1]
        o_ref, acc = refs[-2], refs[-1]
        k = pl.program_id(2)
        part = lax.dot_general(a_ref[...].astype(_MXU), b_ref[...].astype(_MXU), dims,
                               preferred_element_type=_F32)
        if nk == 1:
            o_ref[...] = part.astype(o_ref.dtype)
            return

        @pl.when(k == 0)
        def _():
            acc[...] = part

        @pl.when(k > 0)
        def _():
            acc[...] += part

        @pl.when(k == nk - 1)
        def _():
            o_ref[...] = acc[...].astype(o_ref.dtype)

    if nk == 1:
        acc_shape = (8, 128)
    in_specs = [a_spec, b_spec]
    args = [a, b]
    aliases = {}
    if into is not None:
        in_specs.append(pl.BlockSpec(memory_space=pl.ANY))
        args.append(into)
        aliases = {2: 0}
    return pl.pallas_call(
        body, name=name, grid=grid, in_specs=in_specs, out_specs=o_spec, out_shape=out_sds,
        scratch_shapes=[pltpu.VMEM(acc_shape, _F32)], input_output_aliases=aliases,
        compiler_params=_params(3))(*args)


_NN = (((1,), (0,)), ((), ()))
_NT = (((1,), (1,)), ((), ()))
_TN = (((0,), (0,)), ((), ()))


def _rowwise(name, body, *, grid, ins, outs, scratch=()):
    return pl.pallas_call(
        body, name=name, grid=(grid,), in_specs=[s for _, s in ins], out_specs=[s for _, s in outs],
        out_shape=[o for o, _ in outs], scratch_shapes=list(scratch),
        compiler_params=_params(1))(*[a for a, _ in ins])


def _rows(tb, w, cb=0, n=None):
    if n is None:
        return pl.BlockSpec((tb, w), lambda i: (i, cb))
    return pl.BlockSpec((tb, w), lambda i: (n - 1 - i, cb))


def _vec(shape):
    return pl.BlockSpec(shape, lambda i: (0,) * len(shape))


def _halo_prev(tb, w, cb=0, n=None, rows=8):
    if n is None:
        return pl.BlockSpec((rows, w), lambda i: (jnp.maximum(i * (tb // rows) - 1, 0), cb))
    return pl.BlockSpec((rows, w), lambda i: (jnp.maximum((n - 1 - i) * (tb // rows) - 1, 0), cb))


def _halo_next(tb, w, n, cb=0):
    return pl.BlockSpec((8, w), lambda i: (jnp.minimum((i + 1) * (tb // 8), n * (tb // 8) - 1), cb))


def _sds(shape, dtype=_F32):
    return jax.ShapeDtypeStruct(shape, dtype)


def _cast(name, a, tb):
    rows, cols = a.shape

    def body(a_ref, o_ref):
        o_ref[...] = a_ref[...].astype(o_ref.dtype)

    return _rowwise(name, body, grid=rows // tb, ins=[(a, _rows(tb, cols))],
                    outs=[(_sds((rows, cols), _MXU), _rows(tb, cols))])[0]


def _sum_lead(name, a, tb):
    g, rows, cols = a.shape

    def body(a_ref, o_ref):
        acc = a_ref[0]
        for k in range(1, g):
            acc = acc + a_ref[k]
        o_ref[...] = acc

    return _rowwise(name, body, grid=rows // tb,
                    ins=[(a, pl.BlockSpec((g, tb, cols), lambda i: (0, i, 0)))],
                    outs=[(_sds((rows, cols)), _rows(tb, cols))])[0]


def _sum_pair(name, mine, theirs, core, tb):
    nj, _, rows, cols = mine.shape

    def body(s_ref, a_ref, b_ref, o_ref, ob_ref):
        t = a_ref[...] + b_ref[...]
        o_ref[...] = t
        ob_ref[...] = t.astype(ob_ref.dtype)

    blk = pl.BlockSpec((None, tb, cols), lambda j, i, s: (j, i, 0))
    grid_spec = pltpu.PrefetchScalarGridSpec(
        num_scalar_prefetch=1, grid=(nj, rows // tb),
        in_specs=[pl.BlockSpec((None, None, tb, cols), lambda j, i, s: (j, s[0], i, 0)), blk],
        out_specs=[blk, blk])
    return pl.pallas_call(body, name=name, grid_spec=grid_spec,
                          out_shape=[_sds((nj, rows, cols)), _sds((nj, rows, cols), _MXU)],
                          compiler_params=_params(2))(core, mine, theirs)


def _sum_chips(name, mine, theirs, where, l, into, tb):
    _, rows, cols = mine.shape
    extra = [] if into is None else [into]

    def body(*refs):
        a_ref, b1_ref, b2_ref, b3_ref = refs[1:5]
        o_ref = refs[-1]
        o_ref[...] = ((a_ref[...] + b1_ref[...].astype(_F32)) + b2_ref[...].astype(_F32)) + b3_ref[...].astype(_F32)

    def slot(k):
        return pl.BlockSpec((None, tb, cols), lambda i, s: (jnp.bitwise_xor(s[0], k), i, 0))

    grid_spec = pltpu.PrefetchScalarGridSpec(
        num_scalar_prefetch=1, grid=(rows // tb,),
        in_specs=[slot(0), slot(1), slot(2), slot(3)] + [pl.BlockSpec(memory_space=pl.ANY)] * len(extra),
        out_specs=pl.BlockSpec((None, None, tb, cols), lambda i, s: (l, s[1], i, 0)))
    return pl.pallas_call(body, name=name, grid_spec=grid_spec, out_shape=_sds((2, 2, rows, cols)),
                          input_output_aliases={5: 0} if extra else {},
                          compiler_params=_params(1))(where, mine, theirs, theirs, theirs, *extra)


def _adamw(name, w, g, m, v, tb):
    rows, cols = w.shape
    c1 = 1.0 - ADAM_B1 ** ADAM_STEP
    c2 = 1.0 - ADAM_B2 ** ADAM_STEP

    def body(w_ref, g_ref, m_ref, v_ref, d_ref, nm_ref, nv_ref):
        gv = g_ref[...]
        nm = ADAM_B1 * m_ref[...] + (1.0 - ADAM_B1) * gv
        nv = ADAM_B2 * v_ref[...] + (1.0 - ADAM_B2) * (gv * gv)
        d_ref[...] = -ADAM_LR * ((nm / c1) / (jnp.sqrt(nv / c2) + ADAM_EPS) + ADAM_WD * w_ref[...])
        nm_ref[...] = nm
        nv_ref[...] = nv

    spec = _rows(tb, cols)
    return _rowwise(name, body, grid=rows // tb, ins=[(w, spec), (g, spec), (m, spec), (v, spec)],
                    outs=[(_sds((rows, cols)), spec)] * 3)


def _mod_fwd(c_all, w_mod, b_cols):
    cols = w_mod.shape[2]

    def body(c_ref, w_ref, b_ref, o_ref):
        cv = c_ref[...]
        sc = (cv * _sigmoid(cv)).astype(_MXU)
        o_ref[...] = jnp.dot(sc, w_ref[...].astype(_MXU), preferred_element_type=_F32) + b_ref[...]

    return pl.pallas_call(
        body, name="mod_fwd", grid=(2,),
        in_specs=[pl.BlockSpec((8, D), lambda l: (0, 0)), pl.BlockSpec((None, D, cols), lambda l: (l, 0, 0)),
                  pl.BlockSpec((None, 1, cols), lambda l: (l, 0, 0))],
        out_specs=pl.BlockSpec((None, 8, cols), lambda l: (l, 0, 0)),
        out_shape=_sds((2, 8, cols)), compiler_params=_params(1))(c_all, w_mod, b_cols)


def _mod_bwd(c_all_t, dm):
    cols = dm.shape[2]

    def body(c_ref, d_ref, o_ref):
        cv = c_ref[...]
        sc = (cv * _sigmoid(cv)).astype(_MXU)
        o_ref[...] = jnp.dot(sc, d_ref[...].astype(_MXU), preferred_element_type=_F32)

    return pl.pallas_call(
        body, name="mod_bwd", grid=(2,),
        in_specs=[pl.BlockSpec((D, 8), lambda l: (0, 0)), pl.BlockSpec((None, 8, cols), lambda l: (l, 0, 0))],
        out_specs=pl.BlockSpec((None, D, cols), lambda l: (l, 0, 0)),
        out_shape=_sds((2, D, cols)), compiler_params=_params(1))(c_all_t, dm)


def _proj(x, g_pre, shift, scale, w_in):
    s = x.shape[0]
    tm = 1024

    def body(x_ref, g_ref, sh_ref, sc_ref, w_ref, o_ref, ht_ref, h_s):
        @pl.when(pl.program_id(1) == 0)
        def _():
            xv = x_ref[...]
            rstd = lax.rsqrt(jnp.mean(xv * xv, axis=-1, keepdims=True) + NORM_EPS)
            hv = (xv * rstd) * g_ref[...] * (1.0 + sc_ref[...]) + sh_ref[...]
            h_s[...] = hv.astype(h_s.dtype)
            ht_ref[...] = hv.T.astype(ht_ref.dtype)

        o_ref[...] = jnp.dot(h_s[...], w_ref[...], preferred_element_type=_F32).astype(o_ref.dtype)

    vec = pl.BlockSpec((1, D), lambda m, n: (0, 0))
    return pl.pallas_call(
        body, name="proj", grid=(s // tm, N_CHIPS),
        in_specs=[pl.BlockSpec((tm, D), lambda m, n: (m, 0)), vec, vec, vec,
                  pl.BlockSpec((None, D, 2304), lambda m, n: (n, 0, 0))],
        out_specs=[pl.BlockSpec((tm, 2304), lambda m, n: (m, n)), pl.BlockSpec((D, tm), lambda m, n: (0, m))],
        out_shape=[_sds((s, IN_W), _MXU), _sds((D, s), _MXU)],
        scratch_shapes=[pltpu.VMEM((tm, D), _MXU)], compiler_params=_params(2))(x, g_pre, shift, scale, w_in)


def _shift_down(cur, halo, j, tb):
    ext = jnp.concatenate([halo, cur], axis=0)
    return pltpu.roll(ext, j, 0)[8:8 + tb]


def _shift_up(cur, halo, j, tb):
    ext = jnp.concatenate([cur, halo], axis=0)
    return pltpu.roll(ext, tb + 8 - j, 0)[0:tb]


def _conv_fwd(proj, conv_w, conv_b):
    s = proj.shape[0]
    tb = 512

    def body(u_ref, hp_ref, w_ref, b_ref, o_ref):
        i = pl.program_id(0)
        u = u_ref[...].astype(_F32)
        halo = jnp.where(i > 0, hp_ref[...].astype(_F32)[8:16], 0.0)
        acc = b_ref[...] + u * w_ref[0:1, :]
        for j in range(1, 4):
            acc = acc + _shift_down(u, halo, j, tb) * w_ref[j:j + 1, :]
        o_ref[...] = acc

    return _rowwise("conv_fwd", body, grid=s // tb,
                    ins=[(proj, _rows(tb, D, CB_U)), (proj, _halo_prev(tb, D, CB_U, rows=16)),
                         (conv_w, _vec((4, D))), (conv_b, _vec((1, D)))],
                    outs=[(_sds((s, D)), _rows(tb, D))])[0]


def _lru_gates(pre_r, pre_i, uc, b_rg, b_ig, lam):
    r = _sigmoid(pre_r + b_rg)
    ig = _sigmoid(pre_i + b_ig)
    nl = -lam
    sp = jnp.maximum(nl, 0.0) + jnp.log(1.0 + jnp.exp(-jnp.abs(nl)))
    la = -LRU_C * r * sp
    a = jnp.exp(la)
    one_m_a2 = -jnp.tanh(la) * (a * a + 1.0)
    inv_sq = lax.rsqrt(jnp.maximum(one_m_a2, 1e-30))
    return r, ig, sp, a, one_m_a2 * inv_sq, inv_sq


GATE_TILES = 8


def _gate_tiles(w_rg, w_ig):
    eye = jnp.eye(2, dtype=w_rg.dtype)

    def tiles(w):
        return jnp.einsum("cpij,pq->cpiqj", w.reshape(GATE_TILES, 2, 64, 64), eye).reshape(GATE_TILES, 128, 128)

    return jnp.concatenate([tiles(w_rg), tiles(w_ig)], axis=2)


def _gate_tile_grads(gw):
    keep = jnp.eye(2, dtype=jnp.bool_)[None, :, None, :, None]

    def blocks(t):
        t5 = t.reshape(GATE_TILES, 2, 64, 2, 64)
        return jnp.sum(jnp.where(keep, t5, 0.0), axis=3).reshape(16, 64, 64)

    return blocks(gw[:, :, 0:128]), blocks(gw[:, :, 128:256])


def _gate_preacts(ucv, wt_ref):
    ucb = ucv.astype(_MXU)
    ps = [jnp.dot(ucb[:, 128 * c:128 * (c + 1)], wt_ref[c], preferred_element_type=_F32) for c in range(GATE_TILES)]
    pre_r = jnp.concatenate([p[:, 0:128] for p in ps], axis=1)
    pre_i = jnp.concatenate([p[:, 128:256] for p in ps], axis=1)
    return pre_r, pre_i


def _scan_fwd(uc, wt, b_rg, b_ig, lam):
    s = uc.shape[0]
    tb = 256

    def body(uc_ref, wt_ref, brg_ref, big_ref, lam_ref, h_ref, carry, a_s, b_s):
        i = pl.program_id(0)

        @pl.when(i == 0)
        def _():
            carry[...] = jnp.zeros_like(carry)

        ucv = uc_ref[...]
        pre_r, pre_i = _gate_preacts(ucv, wt_ref)
        _, ig, _, a, sq, _ = _lru_gates(pre_r, pre_i, ucv, brg_ref[...], big_ref[...], lam_ref[...])
        av = a
        bv = sq * (ig * ucv)
        av = av.reshape(tb // 8, 8, D)
        bv = bv.reshape(tb // 8, 8, D)
        row8 = lax.broadcasted_iota(jnp.int32, (1, 8, 1), 1)
        for sh in (1, 2, 4):
            m = row8 >= sh
            b_sh = pltpu.roll(bv, sh, 1)
            a_sh = pltpu.roll(av, sh, 1)
            bv = jnp.where(m, av * b_sh + bv, bv)
            av = jnp.where(m, av * a_sh, av)
        a_s[...] = av.reshape(tb, D)
        b_s[...] = bv.reshape(tb, D)

        def tile(t, state):
            rows = pl.ds(pl.multiple_of(t * 8, 8), 8)
            hv = b_s[rows, :] + a_s[rows, :] * state
            b_s[rows, :] = hv
            return jnp.broadcast_to(hv[7:8, :], (8, D))

        carry[...] = lax.fori_loop(0, tb // 8, tile, jnp.broadcast_to(carry[7:8, :], (8, D)), unroll=4)
        h_ref[...] = b_s[...].astype(h_ref.dtype)

    v = _vec((1, D))
    return _rowwise("scan_fwd", body, grid=s // tb,
                    ins=[(uc, _rows(tb, D)), (wt, _vec((GATE_TILES, 128, 256))), (b_rg, v), (b_ig, v), (lam, v)],
                    outs=[(_sds((s, D), _MXU), _rows(tb, D))],
                    scratch=[pltpu.VMEM((8, D), _F32), pltpu.VMEM((tb, D), _F32), pltpu.VMEM((tb, D), _F32)])[0]


def _weight_specs(l):
    return [pl.BlockSpec((N_CHIPS, None, ATT_W, 256), lambda i: (0, l, 0, 0)),
            pl.BlockSpec((N_CHIPS, None, 256, D), lambda i: (0, l, 0, 0)),
            pl.BlockSpec((N_CHIPS, None, 256, D), lambda i: (0, l, 0, 0))]


def _tail_fwd(l, o, h_lru, proj, x, gate, g_post, gw, target):
    s = x.shape[0]
    tb = 512

    def body(*refs):
        o_ref, h_ref, ga_ref, gl_ref, ma_ref, mb_ref, x_ref, gt_ref, gp_ref, wpa_ref, wpb_ref, wo_ref = refs[0:12]
        aa_ref, ba_ref, ya_ref, yb_ref, z_ref, out_ref = refs[-8:-2] if target is not None else refs[-7:-1]
        ga = ga_ref[...].astype(_F32)
        aa32 = o_ref[...] * (ga * _sigmoid(ga))
        aa = aa32.astype(_MXU)
        aa_ref[...] = aa32.T.astype(aa_ref.dtype)
        gl = gl_ref[...].astype(_F32)
        ba32 = h_ref[...].astype(_F32) * (gl * _sigmoid(gl))
        ba = ba32.astype(_MXU)
        ba_ref[...] = ba32.T.astype(ba_ref.dtype)
        ya = jnp.concatenate([jnp.dot(aa, wpa_ref[j], preferred_element_type=_F32) for j in range(N_CHIPS)], axis=1)
        ya_ref[...] = ya.astype(ya_ref.dtype)
        yb = jnp.dot(ba, wpb_ref[...].reshape(D, D), preferred_element_type=_F32)
        yb_ref[...] = yb.astype(yb_ref.dtype)
        z32 = _sigmoid(ma_ref[...].astype(_F32)) * ya + _sigmoid(mb_ref[...].astype(_F32)) * yb
        z = z32.astype(_MXU)
        z_ref[...] = z32.T.astype(z_ref.dtype)
        ov = jnp.dot(z, wo_ref[...].reshape(D, D), preferred_element_type=_F32)
        out_ref[...] = ov.astype(out_ref.dtype)
        rstd = lax.rsqrt(jnp.mean(ov * ov, axis=-1, keepdims=True) + NORM_EPS)
        xn =x_ref[...] + gt_ref[...] * ((ov * rstd) * gp_ref[...])
        if target is None:
            refs[-1][...] = xn
        else:
            dy_ref, acc_ref = refs[-2], refs[-1]
            err = xn - refs[12][...]
            dy_ref[...] = err * (1.0 / D)
            _zero_first(pl.program_id(0), acc_ref)
            acc_ref[...] += jnp.sum(err * err, axis=0, keepdims=True)

    v = _vec((1, D))
    r = _rows(tb, D)
    r5 = _rows(tb, ATT_W)
    weights = list(zip((gw["w_pa"], gw["w_pb"], gw["w_o"]), _weight_specs(l)))
    cols = pl.BlockSpec((D, tb), lambda i: (0, i))
    head_in = [] if target is None else [(target, r)]
    head_out = [] if target is None else [(_sds((1, D)), v)]
    return _rowwise("tail_fwd" if target is None else "tail_loss_fwd", body, grid=s // tb,
                    ins=[(o, r5), (h_lru, r), (proj, _rows(tb, ATT_W, CB_GATT)), (proj, _rows(tb, D, CB_GLRU)),
                         (proj, _rows(tb, D, CB_MA)), (proj, _rows(tb, D, CB_MB)), (x, r), (gate, v), (g_post, v)]
                    + weights + head_in,
                    outs=[(_sds((ATT_W, s), _MXU), pl.BlockSpec((ATT_W, tb), lambda i: (0, i))),
                          (_sds((D, s), _MXU), cols), (_sds((s, D), _MXU), r), (_sds((s, D), _MXU), r),
                          (_sds((D, s), _MXU), cols), (_sds((s, D), _MXU), r), (_sds((s, D)), r)]
                    + head_out)


def _zero_first(i, *refs):
    @pl.when(i == 0)
    def _():
        for ref in refs:
            ref[...] = jnp.zeros_like(ref)


def _tail_bwd(l, dx, out, y_a, y_b, proj, o, h_lru, gate, g_post, gw):
    s = dx.shape[0]
    tb = 256

    def body(dx_ref, out_ref, ya_ref, yb_ref, ma_ref, mb_ref, o_ref, ga_ref, h_ref, gl_ref, gt_ref, gp_ref,
             wpa_ref, wpb_ref, wo_ref,
             dout_ref, dya_ref, dyb_ref, rest_ref, do_ref, dh_ref, dgt_ref, dgp_ref):
        i = pl.program_id(0)
        ov = out_ref[...].astype(_F32)
        dxv = dx_ref[...]
        rstd = lax.rsqrt(jnp.mean(ov * ov, axis=-1, keepdims=True) + NORM_EPS)
        nv = ov * rstd
        s_dn = jnp.sum(dxv * nv, axis=0, keepdims=True)
        _zero_first(i, dgt_ref, dgp_ref)
        dgt_ref[...] += s_dn * gp_ref[...]
        dgp_ref[...] += s_dn * gt_ref[...]
        dn = dxv * (gt_ref[...] * gp_ref[...])
        d_out = (rstd * (dn - nv * jnp.mean(dn * nv, axis=-1, keepdims=True))).astype(_MXU)
        dout_ref[...] = d_out
        dz = lax.dot_general(d_out, wo_ref[...].reshape(D, D), _NT, preferred_element_type=_F32)
        ga = _sigmoid(ma_ref[...].astype(_F32))
        gb = _sigmoid(mb_ref[...].astype(_F32))
        dya = (dz * ga).astype(_MXU)
        dyb = (dz * gb).astype(_MXU)
        dya_ref[...] = dya
        dyb_ref[...] = dyb
        rest_ref[:, R_MA:R_MB] = (dz * ya_ref[...].astype(_F32) * ga * (1.0 - ga)).astype(rest_ref.dtype)
        rest_ref[:, R_MB:R_END] = (dz * yb_ref[...].astype(_F32) * gb * (1.0 - gb)).astype(rest_ref.dtype)
        daa = lax.dot_general(dya[:, 0:256], wpa_ref[0], _NT, preferred_element_type=_F32)
        for j in range(1, N_CHIPS):
            daa = daa + lax.dot_general(dya[:, j * 256:(j + 1) * 256], wpa_ref[j], _NT, preferred_element_type=_F32)
        dba = lax.dot_general(dyb, wpb_ref[...].reshape(D, D), _NT, preferred_element_type=_F32)
        gav = ga_ref[...].astype(_F32)
        sa = _sigmoid(gav)
        do_ref[...] = daa * (gav * sa)
        rest_ref[:, 0:R_U] = (daa * o_ref[...] * (sa * (1.0 + gav * (1.0 - sa)))).astype(rest_ref.dtype)
        gl = gl_ref[...].astype(_F32)
        sl = _sigmoid(gl)
        dh_ref[...] = dba * (gl * sl)
        rest_ref[:, R_GLRU:R_MA] = (dba * h_ref[...].astype(_F32)
                                    * (sl * (1.0 + gl * (1.0 - sl)))).astype(rest_ref.dtype)

    v = _vec((1, D))
    r5, r10 = _rows(tb, ATT_W), _rows(tb, D)
    return _rowwise("tail_bwd", body, grid=s // tb,
                    ins=[(dx, r10), (out, r10), (y_a, r10), (y_b, r10), (proj, _rows(tb, D, CB_MA)),
                         (proj, _rows(tb, D, CB_MB)), (o, r5), (proj, _rows(tb, ATT_W, CB_GATT)), (h_lru, r10),
                         (proj, _rows(tb, D, CB_GLRU)), (gate, v), (g_post, v)]
                    + list(zip((gw["w_pa"], gw["w_pb"], gw["w_o"]), _weight_specs(l))),
                    outs=[(_sds((s, D), _MXU), r10), (_sds((s, D), _MXU), r10), (_sds((s, D), _MXU), r10),
                          (_sds((s, R_END), _MXU), _rows(tb, R_END)),
                          (_sds((s, ATT_W)), r5), (_sds((s, D)), r10), (_sds((1, D)), v), (_sds((1, D)), v)])


def _scan_bwd(dh, uc, h_lru, wt, b_rg, b_ig, lam):
    s = uc.shape[0]
    tb = 256
    n = s // tb

    def body(dh_ref, uc_ref, h_ref, hp_ref, wt_ref, brg_ref, big_ref, lam_ref,
             duc_ref, dwt_ref, dbrg_ref, dbig_ref, dlam_ref, carry, c_s, g_s):
        i = pl.program_id(0)

        @pl.when(i == 0)
        def _():
            carry[...] = jnp.zeros_like(carry)
            for acc_ref in (dwt_ref, dbrg_ref, dbig_ref, dlam_ref):
                acc_ref[...] = jnp.zeros_like(acc_ref)

        ucv = uc_ref[...]
        pre_r, pre_i = _gate_preacts(ucv, wt_ref)
        r, ig, sp, a, sq, inv_sq =_lru_gates(pre_r, pre_i, ucv, brg_ref[...], big_ref[...], lam_ref[...])
        row = lax.broadcasted_iota(jnp.int32, (tb, 1), 0)
        cv = jnp.where(row == tb - 1, 1.0, pltpu.roll(a, tb - 1, 0))
        gv = dh_ref[...]
        cv = cv.reshape(tb // 8, 8, D)
        gv = gv.reshape(tb // 8, 8, D)
        row8 = lax.broadcasted_iota(jnp.int32, (1, 8, 1), 1)
        for sh in (1, 2, 4):
            m = row8 < 8 - sh
            g_sh = pltpu.roll(gv, 8 - sh, 1)
            c_sh = pltpu.roll(cv, 8 - sh, 1)
            gv = jnp.where(m, gv + cv * g_sh, gv)
            cv = jnp.where(m, cv * c_sh, cv)
        c_s[...] = cv.reshape(tb, D)
        g_s[...] = gv.reshape(tb, D)

        def tile(k, state):
            rows = pl.ds(pl.multiple_of((tb // 8 - 1 - k) * 8, 8), 8)
            gt = g_s[rows, :] + c_s[rows, :] * state
            g_s[rows, :] = gt
            return jnp.broadcast_to(gt[0:1, :], (8, D))

        lax.fori_loop(0, tb // 8, tile, jnp.broadcast_to(carry[0:1, :], (8, D)), unroll=4)
        gv = g_s[...]
        carry[...] = (a * gv)[0:8]

        halo = jnp.where(i < n - 1, hp_ref[...].astype(_F32)[8:16], 0.0)
        h_prev = _shift_down(h_ref[...].astype(_F32), halo, 1, tb)
        d_a = gv * h_prev
        d_sq = gv * (ig * ucv)
        d_i = gv * sq * ucv
        d_la = d_a * a - d_sq * (a * a) * inv_sq
        d_r = d_la * (-LRU_C * sp)
        d_pre_r = d_r * r * (1.0 - r)
        d_pre_i = d_i * ig * (1.0 - ig)
        ucb = ucv.astype(_MXU)
        dpr = d_pre_r.astype(_MXU)
        dpi = d_pre_i.astype(_MXU)
        back = []
        for c in range(GATE_TILES):
            lanes = slice(128 * c, 128 * (c + 1))
            dp = jnp.concatenate([dpr[:, lanes], dpi[:, lanes]], axis=1)
            back.append(lax.dot_general(dp, wt_ref[c], _NT, preferred_element_type=_F32))
            dwt_ref[c] += lax.dot_general(ucb[:, lanes], dp, _TN, preferred_element_type=_F32)
        duc_ref[...] = gv * sq * ig + jnp.concatenate(back, axis=1)
        dbrg_ref[...] += jnp.sum(d_pre_r, axis=0, keepdims=True)
        dbig_ref[...] += jnp.sum(d_pre_i, axis=0, keepdims=True)
        lamv = lam_ref[...]
        dlam_ref[...] += jnp.sum(d_la * (-LRU_C * r), axis=0, keepdims=True) * (-_sigmoid(-lamv))

    v = _vec((1, D))
    rv = _rows(tb, D, 0, n)
    return _rowwise("scan_bwd", body, grid=n,
                    ins=[(dh, rv), (uc, rv), (h_lru, rv), (h_lru, _halo_prev(tb, D, 0, n, rows=16)),
                         (wt, _vec((GATE_TILES, 128, 256))), (b_rg, v), (b_ig, v), (lam, v)],
                    outs=[(_sds((s, D)), rv), (_sds((GATE_TILES, 128, 256)), _vec((GATE_TILES, 128, 256))),
                          (_sds((1, D)), v), (_sds((1, D)), v), (_sds((1, D)), v)],
                    scratch=[pltpu.VMEM((8, D), _F32), pltpu.VMEM((tb, D), _F32), pltpu.VMEM((tb, D), _F32)])


def _conv_bwd(duc_a, proj, conv_w, rest):
    s = duc_a.shape[0]
    tb = 512
    n = s // tb
    hw = D // 2

    def body(da_ref, dan_ref, u_ref, up_ref, w_ref, rest_in, du_ref, dw_ref, dbias_ref):
        i = pl.program_id(1)
        duc = da_ref[...]
        nxt = jnp.where(i < n - 1, dan_ref[...], 0.0)
        u = u_ref[...].astype(_F32)
        halo = jnp.where(i > 0, up_ref[...].astype(_F32)[8:16], 0.0)
        du = duc * w_ref[0:1, :]
        dws = [jnp.sum(duc * u, axis=0, keepdims=True)]
        for j in range(1, 4):
            du = du + _shift_up(duc, nxt, j, tb) * w_ref[j:j + 1, :]
            dws.append(jnp.sum(duc * _shift_down(u, halo, j, tb), axis=0, keepdims=True))
        du_ref[...] = du.astype(du_ref.dtype)
        _zero_first(i, dw_ref, dbias_ref)
        for j in range(4):
            dw_ref[j:j + 1, :] += dws[j]
        dbias_ref[...] += jnp.sum(duc, axis=0, keepdims=True)

    r = pl.BlockSpec((tb, hw), lambda h, i: (i, h))
    nxt_spec = pl.BlockSpec((8, hw), lambda h, i: (jnp.minimum((i + 1) * (tb // 8), n * (tb // 8) - 1), h))
    return pl.pallas_call(
        body, name="conv_bwd", grid=(2, n),
        in_specs=[r, nxt_spec,
                  pl.BlockSpec((tb, hw), lambda h, i: (i, 2 * CB_U + h)),
                  pl.BlockSpec((16, hw), lambda h, i: (jnp.maximum(i * (tb // 16) - 1, 0), 2 * CB_U + h)),
                  pl.BlockSpec((4, hw), lambda h, i: (0, h)), pl.BlockSpec(memory_space=pl.ANY)],
        out_specs=[pl.BlockSpec((tb, hw), lambda h, i: (i, R_U // hw + h)),
                   pl.BlockSpec((4, hw), lambda h, i: (0, h)), pl.BlockSpec((1, hw), lambda h, i: (0, h))],
        out_shape=[_sds(rest.shape, rest.dtype), _sds((4, D)), _sds((1, D))],
        input_output_aliases={5: 0}, compiler_params=_params(2),
    )(duc_a, duc_a, proj, proj, conv_w, rest)


def _band_tiles(dil):
    tiles = []
    for rho in range(dil):
        for b in range(16 // dil):
            qs = rho + dil * BAND * b
            tiles.append((qs, QBLK + qs - dil * BAND, b))
    return tiles


def _strided(start, size, dil):
    return pl.ds(start, size, stride=dil) if dil > 1 else pl.ds(start, size)


def _band_mask(i, b):
    qi = lax.broadcasted_iota(jnp.int32, (BAND, 2 * BAND), 0)
    ki = lax.broadcasted_iota(jnp.int32, (BAND, 2 * BAND), 1)
    valid = (ki >= qi) & (ki <= qi + BAND)
    if b == 0:
        valid = valid & ((ki >= BAND) | (i > 0))
    return valid


def _attn_fwd(proj):
    s = proj.shape[0]
    n = s // QBLK
    scale = HEAD ** -0.5

    def body(*refs):
        q_refs, kp_refs, kc_refs, vp_refs, vc_refs = (refs[3 * t:3 * t + 3] for t in range(5))
        o_ref, lse_ref, qbuf, kbuf, vbuf = refs[15:20]
        accs, maxs, dens = refs[20:23], refs[23:26], refs[26:29]
        i = pl.program_id(1)
        for g, dil in enumerate(DILATIONS):
            qbuf[...] = q_refs[g][...].astype(_F32)
            kbuf[0:QBLK, :] = kp_refs[g][...].astype(_F32)
            kbuf[QBLK:2 * QBLK, :] = kc_refs[g][...].astype(_F32)
            vbuf[0:QBLK, :] = vp_refs[g][...].astype(_F32)
            vbuf[QBLK:2 * QBLK, :] = vc_refs[g][...].astype(_F32)
            for qs, ks, b in _band_tiles(dil):
                qsl = _strided(qs, BAND, dil)
                q = qbuf[qsl, :].astype(_MXU)
                kk = kbuf[_strided(ks, 2 * BAND, dil), :].astype(_MXU)
                vv = vbuf[_strided(ks, 2 * BAND, dil), :].astype(_MXU)
                sc = lax.dot_general(q, kk, _NT, preferred_element_type=_F32) * scale
                sc = jnp.where(_band_mask(i, b), sc, NEG_INF)
                m = jnp.max(sc, axis=-1, keepdims=True)
                p = jnp.exp(sc - m)
                accs[g][qsl, :] = jnp.dot(p.astype(_MXU), vv, preferred_element_type=_F32)
                maxs[g][qsl, :] = jnp.broadcast_to(m, (BAND, HEAD))
                dens[g][qsl, :] = jnp.broadcast_to(jnp.sum(p, axis=-1, keepdims=True), (BAND, HEAD))
        ms = [r[...] for r in maxs]
        mx = jnp.maximum(jnp.maximum(ms[0], ms[1]), ms[2])
        ws = [jnp.exp(m - mx) for m in ms]
        den = ws[0] * dens[0][...] + ws[1] * dens[1][...] + ws[2] * dens[2][...]
        o_ref[...] = (ws[0] * accs[0][...] + ws[1] * accs[1][...] + ws[2] * accs[2][...]) / den
        lse_ref[...] = mx + jnp.log(den)

    blk = (QBLK, HEAD)

    def spec(first_col, lag):
        specs = []
        for g in range(3):
            col = first_col + g * HEADS
            if lag:
                specs.append(pl.BlockSpec(blk, lambda j, i, col=col: (jnp.maximum(i - 1, 0), col + j)))
            else:
                specs.append(pl.BlockSpec(blk, lambda j, i, col=col: (i, col + j)))
        return specs

    out_spec = pl.BlockSpec(blk, lambda j, i: (i, j))
    return pl.pallas_call(
        body, name="attn_fwd", grid=(HEADS, n),
        in_specs=spec(0, False) + spec(12, True) + spec(12, False) + spec(24, True) + spec(24, False),
        out_specs=[out_spec] * 2, out_shape=[_sds((s, ATT_W))] * 2,
        scratch_shapes=[pltpu.VMEM(blk, _F32)] + [pltpu.VMEM((2 * QBLK, HEAD), _F32)] * 2
        + [pltpu.VMEM(blk, _F32)] * 9,
        compiler_params=_params(2))(*([proj] * 15))


def _attn_bwd(proj, d_o, o, lse, g, into):
    s = proj.shape[0]
    dil = DILATIONS[g]
    n = s // QBLK
    scale = HEAD ** -0.5
    tiles = _band_tiles(dil)

    def body(*refs):
        q_ref, kp_ref, kc_ref, vp_ref, vc_ref, do_ref, o_ref, lse_ref = refs[0:8]
        dq_ref, dk_ref, dv_ref, kbuf, vbuf, dkbuf, dvbuf, dqbuf, qbuf = refs[-9:]
        i = pl.program_id(1)

        @pl.when(i == 0)
        def _():
            dkbuf[0:QBLK, :] = jnp.zeros((QBLK, HEAD), _F32)
            dvbuf[0:QBLK, :] = jnp.zeros((QBLK, HEAD), _F32)

        @pl.when(i < n)
        def _():
            qbuf[...] = q_ref[...].astype(_F32)
            kbuf[0:QBLK, :] = kp_ref[...].astype(_F32)
            kbuf[QBLK:2 * QBLK, :] = kc_ref[...].astype(_F32)
            vbuf[0:QBLK, :] = vp_ref[...].astype(_F32)
            vbuf[QBLK:2 * QBLK, :] = vc_ref[...].astype(_F32)
            dkbuf[QBLK:2 * QBLK, :] = jnp.zeros((QBLK, HEAD), _F32)
            dvbuf[QBLK:2 * QBLK, :] = jnp.zeros((QBLK, HEAD), _F32)
            for qs, ks, b in tiles:
                qsl = _strided(qs, BAND, dil)
                ksl = _strided(ks, 2 * BAND, dil)
                q = qbuf[qsl, :].astype(_MXU)
                kk = kbuf[ksl, :].astype(_MXU)
                vv = vbuf[ksl, :].astype(_MXU)
                dov = do_ref[qsl, :]
                dd = jnp.sum(dov * o_ref[qsl, :], axis=-1, keepdims=True)
                lse_t = lse_ref[qsl, :][:, 0:1]
                sc = lax.dot_general(q, kk, _NT, preferred_element_type=_F32) * scale
                p = jnp.where(_band_mask(i, b), jnp.exp(sc - lse_t), 0.0)
                dob = dov.astype(_MXU)
                dp = lax.dot_general(dob, vv, _NT, preferred_element_type=_F32)
                ds = (p * (dp - dd) * scale).astype(_MXU)
                dqbuf[qsl, :] = jnp.dot(ds, kk, preferred_element_type=_F32)
                dkbuf[ksl, :] += lax.dot_general(ds, q, _TN, preferred_element_type=_F32)
                dvbuf[ksl, :] += lax.dot_general(p.astype(_MXU), dob, _TN, preferred_element_type=_F32)
            dq_ref[...] = dqbuf[...].astype(dq_ref.dtype)

        dk_ref[...] = dkbuf[0:QBLK, :].astype(dk_ref.dtype)
        dv_ref[...] = dvbuf[0:QBLK, :].astype(dv_ref.dtype)
        dkbuf[0:QBLK, :] = dkbuf[QBLK:2 * QBLK, :]
        dvbuf[0:QBLK, :] = dvbuf[QBLK:2 * QBLK, :]

    blk = (QBLK, HEAD)
    cq, ck, cv = g * HEADS, 12 + g * HEADS, 24 + g * HEADS

    def cur(i):
        return jnp.minimum(i, n - 1)

    def prev(i):
        return jnp.maximum(jnp.minimum(i, n - 1) - 1, 0)

    own = pl.BlockSpec(blk, lambda j, i: (cur(i), j))
    own_out = pl.BlockSpec(blk, lambda j, i: (cur(i), cq + j))
    late_out = pl.BlockSpec(blk, lambda j, i: (jnp.maximum(i - 1, 0), cq + j))
    extra = [] if into is None else list(into)
    return pl.pallas_call(
        body, name="attn_bwd_d%d" % dil, grid=(HEADS, n + 1),
        in_specs=[pl.BlockSpec(blk, lambda j, i: (cur(i), cq + j)),
                  pl.BlockSpec(blk, lambda j, i: (prev(i), ck + j)),
                  pl.BlockSpec(blk, lambda j, i: (cur(i), ck + j)),
                  pl.BlockSpec(blk, lambda j, i: (prev(i), cv + j)),
                  pl.BlockSpec(blk, lambda j, i: (cur(i), cv + j)),
                  own, own, own] + [pl.BlockSpec(memory_space=pl.ANY)] * len(extra),
        out_specs=[own_out, late_out, late_out], out_shape=[_sds((s, QKV_W), _MXU)] * 3,
        input_output_aliases={8 + t: t for t in range(len(extra))},
        scratch_shapes=[pltpu.VMEM((2 * QBLK, HEAD), _F32)] * 4 + [pltpu.VMEM((QBLK, HEAD), _F32)] * 2,
        compiler_params=_params(2))(proj, proj, proj, proj, proj, d_o, o, lse, *extra)


_PARTS = ((0, 2), (2, 2), (4, 2), (6, 6))
_CHUNK = 768


def _d_x(parts, w_in, x, dx_out, g_pre, scale):
    s = parts[0].shape[0]
    nk = IN_W // _CHUNK

    def body(p0, p1, p2, p3, w_ref, x_ref, dxo_ref, g_ref, sc_ref, dx_ref, dsh_ref, dsc_ref, dg_ref, acc):
        m = pl.program_id(0)
        k = pl.program_id(2)

        @pl.when(k == 0)
        def _():
            acc[...] = jnp.zeros_like(acc)

        @pl.when((k == 0) & (m == 0))
        def _():
            for ref in (dsh_ref, dsc_ref, dg_ref):
                ref[...] = jnp.zeros_like(ref)

        for p_ref, (first, cnt) in zip((p0, p1, p2, p3), _PARTS):
            @pl.when((k >= first) & (k < first + cnt))
            def _(p_ref=p_ref):
                acc[...] += lax.dot_general(p_ref[...].astype(_MXU), w_ref[...], _NT, preferred_element_type=_F32)

        @pl.when(k == nk - 1)
        def _():
            dhv = acc[...]
            xv = x_ref[...]
            rstd = lax.rsqrt(jnp.mean(xv * xv, axis=-1, keepdims=True) + NORM_EPS)
            xn = xv * rstd
            one_sc = 1.0 + sc_ref[...]
            s1 = jnp.sum(dhv * xn, axis=0, keepdims=True)
            dsh_ref[...] += jnp.sum(dhv, axis=0, keepdims=True)
            dsc_ref[...] += s1 * g_ref[...]
            dg_ref[...] += s1 * one_sc
            dxn = dhv * (g_ref[...] * one_sc)
            dx_ref[...] = dxo_ref[...] + rstd * (dxn - xn * jnp.mean(dxn * xn, axis=-1, keepdims=True))

    def part_spec(first, cnt):
        return pl.BlockSpec((1024, _CHUNK), lambda m, n, k: (m, jnp.clip(k - first, 0, cnt - 1)))

    rows = pl.BlockSpec((1024, D), lambda m, n, k: (m, 0))
    vec = pl.BlockSpec((1, D), lambda m, n, k: (0, 0))
    return pl.pallas_call(
        body, name="d_x", grid=(s // 1024, 1, nk),
        in_specs=[part_spec(*p) for p in _PARTS]
        + [pl.BlockSpec((None, D, _CHUNK), lambda m, n, k: (k // 3, 0, k % 3)), rows, rows, vec, vec],
        out_specs=[rows, vec, vec, vec], out_shape=[_sds((s, D)), _sds((1, D)), _sds((1, D)), _sds((1, D))],
        scratch_shapes=[pltpu.VMEM((1024, D), _F32)], compiler_params=_params(3))(
            *parts, w_in, x, dx_out, g_pre, scale)


def _g_w_in(h_t, parts):
    s = h_t.shape[1]
    nk = s // 1024

    def body(*refs):
        h_ref, p_refs = refs[0], refs[1:5]
        o_ref, acc = refs[-2], refs[-1]
        n = pl.program_id(1)
        k = pl.program_id(2)

        @pl.when(k == 0)
        def _():
            acc[...] = jnp.zeros_like(acc)

        for p_ref, (first, cnt) in zip(p_refs, _PARTS):
            @pl.when((n >= first) & (n < first + cnt))
            def _(p_ref=p_ref):
                acc[...] += jnp.dot(h_ref[...], p_ref[...].astype(_MXU), preferred_element_type=_F32)

        @pl.when(k == nk - 1)
        def _():
            o_ref[...] = acc[...]

    def part_spec(first, cnt):
        def index(m, n, k):
            row = jnp.where(n < first, 0, jnp.where(n >= first + cnt, nk - 1, k))
            return (row, jnp.clip(n - first, 0, cnt - 1))
        return pl.BlockSpec((1024, _CHUNK), index)

    return pl.pallas_call(
        body, name="g_w_in", grid=(1, IN_W // _CHUNK, nk),
        in_specs=[pl.BlockSpec((D, 1024), lambda m, n, k: (0, k))] + [part_spec(*p) for p in _PARTS],
        out_specs=pl.BlockSpec((None, D, _CHUNK), lambda m, n, k: (n // 3, 0, n % 3)),
        out_shape=_sds((N_CHIPS, D, 2304)),
        scratch_shapes=[pltpu.VMEM((D, _CHUNK), _F32)], compiler_params=_params(3))(h_t, *parts)


def _layer_fwd(l, x, p, gw, late, target):
    proj, h_t = _proj(x, p["g_pre"], p["shift"], p["scale"], gw["w_in"][l])
    o, lse = _attn_fwd(proj)
    uc = _conv_fwd(proj, p["conv_w"], p["conv_b"])
    h_lru = _scan_fwd(uc, p["wt"], p["b_rg"], p["b_ig"], p["lam"])
    if late is not None:
        landed = dict(late(h_lru))
        gw["w_in"].append(landed.pop("w_in1"))
        gw.update(landed)
    a_att, b_act, y_a, y_b, z, out, *last = _tail_fwd(l, o, h_lru, proj, x, p["gate"], p["g_post"], gw, target)
    saved = dict(x=x, h_t=h_t, proj=proj, o=o, lse=lse, uc=uc, h_lru=h_lru, a_att=a_att, b_act=b_act,
                 y_a=y_a, y_b=y_b, z=z, out=out)
    return (last[0] if target is None else last), saved


def _layer_bwd(l, dx, p, gw, sv, hooks):
    s = dx.shape[0]
    nt = s // 2048
    proj = sv["proj"]
    gate, b_rg, g_pre = p["gate"], p["b_rg"], p["g_pre"]
    if hooks is not None:
        gate = gate + hooks[0]([dx])
    d_out, dy_a, dy_b, d_rest, d_o, dh_lru, d_gate, d_gpost = _tail_bwd(
        l, dx, sv["out"], sv["y_a"], sv["y_b"], proj, sv["o"], sv["h_lru"], gate, p["g_post"], gw)
    if hooks is not None:
        b_rg = b_rg + hooks[1]([d_out])

    def wgrad_rows(name, a, b):
        return _mm(name, a, b, _sds((N_CHIPS, 256, D)), grid=(4, 1, nt),
                   a_spec=pl.BlockSpec((256, 2048), lambda m, n, k: (m, k)),
                   b_spec=pl.BlockSpec((2048, D), lambda m, n, k: (k, 0)),
                   o_spec=pl.BlockSpec((None, 256, D), lambda m, n, k: (m, 0, 0)),
                   dims=_NN, acc_shape=(256, D))

    big = {}
    big["w_o"] = wgrad_rows("g_w_o", sv["z"], d_out)
    big["w_pa"] = _mm("g_w_pa", sv["a_att"], dy_a, _sds((N_CHIPS, ATT_W, 256)), grid=(1, 4, nt),
                      a_spec=pl.BlockSpec((ATT_W, 2048), lambda m, n, k: (0, k)),
                      b_spec=pl.BlockSpec((2048, 256), lambda m, n, k: (k, n)),
                      o_spec=pl.BlockSpec((None, ATT_W, 256), lambda m, n, k: (n, 0, 0)),
                      dims=_NN, acc_shape=(ATT_W, 256))
    big["w_pb"] = wgrad_rows("g_w_pb", sv["b_act"], dy_b)
    duc, g_wt, d_brg, d_big, d_lam = _scan_bwd(dh_lru, sv["uc"], sv["h_lru"], p["wt"], b_rg, p["b_ig"], p["lam"])
    g_wrg, g_wig = _gate_tile_grads(g_wt)
    d_rest, g_convw, g_convb = _conv_bwd(duc, proj, p["conv_w"], d_rest)
    dqkv = None
    for g in range(3):
        dqkv = _attn_bwd(proj, d_o, sv["o"], sv["lse"], g, dqkv)
    if hooks is not None:
        g_pre = g_pre + hooks[2]([dqkv[0]])
    parts = (dqkv[0], dqkv[1], dqkv[2], d_rest)
    big["w_in"] = _g_w_in(sv["h_t"], parts)
    if hooks is not None:
        g_pre = g_pre + hooks[3](big)
    dx_in, d_shift, d_scale, d_gpre = _d_x(parts, gw["w_in"][l], sv["x"], dx, g_pre, p["scale"])
    small = dict(dmod=jnp.concatenate([d_shift, d_scale, d_gate], axis=1), g_pre=d_gpre, conv_w=g_convw,
                 conv_b=g_convb, w_rg=g_wrg, b_rg=d_brg, w_ig=g_wig, b_ig=d_big, lam=d_lam, g_post=d_gpost)
    return dx_in, small, big


_BIG = ("w_in", "w_pa", "w_pb", "w_o")


class _GradReduce:
    PAIR_CHUNKS = (2, 1, 1, 1)
    CHIP_CHUNKS = (2, 1, 1, 1)
    FILL_CHUNKS = (4, 1, 1, 1)

    def __init__(self, core, where):
        self.core, self.where = core, where
        self.finals = None

    def begin(self, l, big):
        n = len(_BIG)
        halves = [big[k].reshape(N_CHIPS, 2, big[k].shape[1] // 2, big[k].shape[2]) for k in _BIG]
        lands = [lax.empty((N_CHIPS,) + h.shape[2:], _F32) for h in halves]
        plan, nsem = _pair_plan(n, self.PAIR_CHUNKS)
        state = {}
        state["pair"] = _split_start("reduce_pair_start_%d" % l, halves + lands, plan, nsem, [])

        def started(after):
            return state["pair"][3][0, 0]

        def pair_done(after):
            send, recv, arrays, _ = state["pair"]
            arrays = _split_wait("reduce_pair_wait_%d" % l, send, recv, arrays, plan, after)
            sums = [_sum_pair("sum_pair_%s_%d" % (k, l), arrays[a], arrays[n + a], self.core, 128)
                    for a, k in enumerate(_BIG)]
            state["mine"] = [t[0] for t in sums]
            lands2 = [lax.empty(t[1].shape, _MXU) for t in sums]
            plan2, nsem2 = _chips_plan(n, self.CHIP_CHUNKS)
            state["plan2"] = plan2
            state["chips"] = _split_start("reduce_chips_start_%d" % l, [t[1] for t in sums] + lands2, plan2, nsem2, [])
            return state["chips"][3][0, 0]

        def chips_done(after):
            send, recv, arrays, _ = state["chips"]
            arrays = _split_wait("reduce_chips_wait_%d" % l, send, recv, arrays, state["plan2"], after)
            finals = [_sum_chips("sum_chips_%s_%d" % (k, l), state["mine"][a], arrays[n + a], self.where, l,
                                 None if self.finals is None else self.finals[a], 128)
                      for a, k in enumerate(_BIG)]
            plan3, nsem3 = _fill_plan(n, self.FILL_CHUNKS, l)
            state["plan3"] = plan3
            state["fill"] = _split_start("gather_halves_start_%d" % l, finals, plan3, nsem3, [])
            return state["fill"][3][0, 0]

        def finish(after):
            send, recv, arrays, _ = state["fill"]
            self.finals = _split_wait("gather_halves_wait_%d" % l, send, recv, arrays, state["plan3"], after)
            return self.finals

        self._finish = finish
        return [started, pair_done, chips_done]

    def finish(self, after):
        return self._finish(after)


def _local_step(x, target, small_p, w_in0, late, reducer):
    saved = []
    h = x
    gw = dict(w_in=[w_in0])
    h, sv = _layer_fwd(0, h, small_p[0], gw, late, None)
    saved.append(sv)
    (dy, sq), sv = _layer_fwd(1, h, small_p[1], gw, None, target)
    saved.append(sv)
    loss = 0.5 * jnp.sum(sq) / D
    smalls = [None, None]
    dx, smalls[1], big1 = _layer_bwd(1, dy, small_p[1], gw, saved[1], None)
    hooks1 = reducer.begin(1, big1)
    own = {}

    def layer0_ready(big0):
        reducer.finish([big0["w_in"]])
        own["hooks"] = reducer.begin(0, big0)
        return own["hooks"][0]([])

    dx, smalls[0], _ = _layer_bwd(0, dx, small_p[0], gw, saved[0], hooks1 + [layer0_ready])
    own["hooks"][1]([dx])

    def finish_reduce(after):
        own["hooks"][2](after)
        return reducer.finish(after)

    return loss, dx, smalls, finish_reduce


_SMALL_ROWS = 8 + 16 + 8 + 128 + 128


def _pack_small(smalls):
    dmod = jnp.concatenate([smalls[0]["dmod"].reshape(3, D), smalls[1]["dmod"].reshape(3, D),
                            jnp.zeros((2, D), _F32)], axis=0)
    vecs = jnp.concatenate([smalls[l][k] for k in ("g_pre", "conv_b", "b_rg", "b_ig", "lam", "g_post")
                            for l in range(2)] + [jnp.zeros((4, D), _F32)], axis=0)
    convw = jnp.concatenate([smalls[0]["conv_w"], smalls[1]["conv_w"]], axis=0)
    wrg = jnp.stack([smalls[0]["w_rg"], smalls[1]["w_rg"]]).reshape(128, D)
    wig = jnp.stack([smalls[0]["w_ig"], smalls[1]["w_ig"]]).reshape(128, D)
    return jnp.concatenate([dmod, vecs, convw, wrg, wig], axis=0)


def kernel(x, c, w_mod, b_mod, g_pre, w_in, conv_w, conv_b, w_rg, b_rg, w_ig, b_ig, lru_lambda, w_pa, w_pb, w_o, g_post, loss_target, m_w_mod, m_b_mod, m_g_pre, m_w_in, m_conv_w, m_conv_b, m_w_rg, m_b_rg, m_w_ig, m_b_ig, m_lru_lambda, m_w_pa, m_w_pb, m_w_o, m_g_post, v_w_mod, v_b_mod, v_g_pre, v_w_in, v_conv_w, v_conv_b, v_w_rg, v_b_rg, v_w_ig, v_b_ig, v_lru_lambda, v_w_pa, v_w_pb, v_w_o, v_g_post):
    xi, yi, ci = lax.axis_index("x"), lax.axis_index("y"), lax.axis_index("c")
    chip = 2 * xi + yi
    dev = 4 * xi + 2 * yi + ci
    mcols = w_mod.shape[2]

    pack1 = jnp.concatenate([jnp.broadcast_to(c, (8, D)),
                             jnp.pad(conv_w.reshape(8, 256), ((0, 0), (0, D - 256)))], axis=0)
    g1 = _exchange("gather_cond", [pack1], "xyc", False)[0]
    c_all = g1[:, 0, :]
    conv_w_full = jnp.transpose(g1[0::2, 8:16, 0:256], (1, 0, 2)).reshape(2, 4, D)

    b_cols = lax.dynamic_slice(b_mod, (0, chip * mcols), (2, mcols)).reshape(2, 1, mcols)
    mod_loc = _mod_fwd(c_all, w_mod, b_cols)
    g2 = _exchange("gather_mod", [mod_loc.reshape(16, mcols)], "xyc", False)[0]
    mod_full = jnp.transpose(g2[0::2], (1, 0, 2)).reshape(2, 8, 3 * D)
    mod_me = lax.dynamic_index_in_dim(mod_full, dev, axis=1, keepdims=False)

    wb_in = _cast("cast_w_in", w_in.reshape(2 * D, 2304), 256).reshape(2, D, 2304)
    late_src = [wb_in[1], _cast("cast_w_pa", w_pa.reshape(2 * ATT_W, 256), 256).reshape(2, ATT_W, 256),
                _cast("cast_w_pb", w_pb.reshape(512, D), 256).reshape(2, 256, D),
                _cast("cast_w_o", w_o.reshape(512, D), 256).reshape(2, 256, D)]
    late_chunks = [4, 2, 2, 2]
    w_in0 = _gather_weights([wb_in[0].reshape(2, D // 2, 2304)], [2])[0].reshape(N_CHIPS, D, 2304)
    chip1 = jnp.reshape(chip, (1,)).astype(jnp.int32)
    lands = [_own_slot("own_slot_" + k, a, chip1, 256) for k, a in zip(("w_in", "w_pa", "w_pb", "w_o"), late_src)]
    late_plan, late_nsem = _gather_plan(len(late_src), late_chunks)
    send_sems, recv_sems, late_arrays, token = _split_start(
        "late_gather_start", late_src + lands, late_plan, late_nsem, [w_in0, mod_me])

    def late(after):
        got = _split_wait("late_gather_wait", send_sems, recv_sems, late_arrays, late_plan, [after])[len(late_src):]
        return dict(w_in1=got[0], w_pa=got[1], w_pb=got[2], w_o=got[3])

    small_p = []
    for l in range(2):
        gates = _gate_tiles(w_rg[l], w_ig[l]).astype(_MXU)
        small_p.append(dict(
            shift=mod_me[l:l + 1, 0:D], scale=mod_me[l:l + 1, D:2 * D], gate=mod_me[l:l + 1, 2 * D:3 * D],
            g_pre=g_pre[l:l + 1], conv_w=conv_w_full[l], conv_b=conv_b[l:l + 1], wt=gates,
            b_rg=b_rg[l:l + 1], b_ig=b_ig[l:l + 1], lam=lru_lambda[l:l + 1], g_post=g_post[l:l + 1]))

    small_p[0]["shift"] = small_p[0]["shift"] + token[0, 0]

    core = jnp.reshape(ci, (1,)).astype(jnp.int32)
    where = jnp.stack([chip, ci]).astype(jnp.int32)
    loss_loc, dx, smalls, finish_reduce = _local_step(x[0], loss_target[0], small_p, w_in0, late,
                                                      _GradReduce(core, where))
    loss = lax.psum(loss_loc, ("x", "y", "c"))
    grad_x = dx[None]
    reduced = finish_reduce([dx])
    g_big ={k: a.reshape(2, 2 * a.shape[2], a.shape[3]) for k, a in zip(_BIG, reduced)}

    g3 = _exchange("gather_small", [_pack_small(smalls)], "xyc", False)[0]
    tot = _sum_lead("sum_small", g3, 96)
    dmod_all = g3[:, 0:6, :].reshape(8, 2, 3 * D)
    dm_cols = jnp.transpose(lax.dynamic_slice(dmod_all, (0, 0, chip * mcols), (8, 2, mcols)), (1, 0, 2))
    g_w_mod = _mod_bwd(jnp.transpose(c_all), dm_cols)
    vec = tot[8:20].reshape(6, 2, D)
    g_conv_w_full = tot[24:32].reshape(2, 4, D)
    grads = dict(
        w_mod=g_w_mod, b_mod=tot[0:6].reshape(2, 3 * D), g_pre=vec[0], w_in=g_big["w_in"],
        conv_w=lax.dynamic_slice(g_conv_w_full, (0, 0, chip * 256), (2, 4, 256)), conv_b=vec[1],
        w_rg=tot[32:160].reshape(2, 16, 64, 64), b_rg=vec[2], w_ig=tot[160:288].reshape(2, 16, 64, 64),
        b_ig=vec[3], lru_lambda=vec[4], w_pa=g_big["w_pa"], w_pb=g_big["w_pb"], w_o=g_big["w_o"],
        g_post=vec[5])

    weights = dict(w_mod=w_mod, b_mod=b_mod, g_pre=g_pre, w_in=w_in, conv_w=conv_w, conv_b=conv_b, w_rg=w_rg,
                   b_rg=b_rg, w_ig=w_ig, b_ig=b_ig, lru_lambda=lru_lambda, w_pa=w_pa, w_pb=w_pb, w_o=w_o,
                   g_post=g_post)
    ms = dict(w_mod=m_w_mod, b_mod=m_b_mod, g_pre=m_g_pre, w_in=m_w_in, conv_w=m_conv_w, conv_b=m_conv_b,
              w_rg=m_w_rg, b_rg=m_b_rg, w_ig=m_w_ig, b_ig=m_b_ig, lru_lambda=m_lru_lambda, w_pa=m_w_pa,
              w_pb=m_w_pb, w_o=m_w_o, g_post=m_g_post)
    vs = dict(w_mod=v_w_mod, b_mod=v_b_mod, g_pre=v_g_pre, w_in=v_w_in, conv_w=v_conv_w, conv_b=v_conv_b,
              w_rg=v_w_rg, b_rg=v_b_rg, w_ig=v_w_ig, b_ig=v_b_ig, lru_lambda=v_lru_lambda, w_pa=v_w_pa,
              w_pb=v_w_pb, w_o=v_w_o, g_post=v_g_post)
    flat = dict(w_mod=(2 * D, mcols, 256), b_mod=(2, 3 * D, 2), g_pre=(2, D, 2), w_in=(2 * D, 2304, 256),
                conv_w=(8, 256, 8), conv_b=(2, D, 2), w_rg=(128, D, 128), b_rg=(2, D, 2), w_ig=(128, D, 128),
                b_ig=(2, D, 2), lru_lambda=(2, D, 2), w_pa=(2 * ATT_W, 256, 256), w_pb=(512, D, 256),
                w_o=(512, D, 256), g_post=(2, D, 2))
    order = ("w_mod", "b_mod", "g_pre", "w_in", "conv_w", "conv_b", "w_rg", "b_rg", "w_ig", "b_ig",
             "lru_lambda", "w_pa", "w_pb", "w_o", "g_post")
    deltas, new_m, new_v = [], [], []
    for k in order:
        rows, cols, tb = flat[k]
        shp = weights[k].shape
        d, nm_, nv_ = _adamw("adamw_" + k, weights[k].reshape(rows, cols), grads[k].reshape(rows, cols),
                             ms[k].reshape(rows, cols), vs[k].reshape(rows, cols), tb)
        deltas.append(d.reshape(shp))
        new_m.append(nm_.reshape(shp))
        new_v.append(nv_.reshape(shp))
    return (loss, grad_x, *[grads[k].reshape(weights[k].shape) for k in order], *deltas, *new_m, *new_v)
```

```python
import functools

import jax
import jax.numpy as jnp
from jax import lax
from jax.experimental import pallas as pl
from jax.experimental.pallas import tpu as pltpu

_F32 = jnp.float32
_MXU = jnp.bfloat16
_VMEM_LIMIT = 56 * 1024 * 1024
_MESH = pl.DeviceIdType.MESH

D = 1024
HEAD = 128
HEADS = 4
ATT_W = 512
QKV_W = 1536
IN_W = 9216
DILATIONS = (1, 4, 16)
BAND = 128
QBLK = BAND * 16
NORM_EPS = 1e-6
NEG_INF = -1e30
LRU_C = 8.0
N_CHIPS = 4
CB_GATT = 4608 // 512
CB_U, CB_GLRU, CB_MA, CB_MB = 5, 6, 7, 8
R_U, R_GLRU, R_MA, R_MB, R_END = 512, 1536, 2560, 3584, 4608

ADAM_LR, ADAM_B1, ADAM_B2, ADAM_EPS, ADAM_WD, ADAM_STEP = 0.001, 0.9, 0.999, 1e-08, 0.01, 10


def _params(ngrid):
    return pltpu.CompilerParams(dimension_semantics=("arbitrary",) * ngrid, vmem_limit_bytes=_VMEM_LIMIT)


def _sigmoid(v):
    return 0.5 * jnp.tanh(0.5 * v) + 0.5


_GROUPS = {
    "c": [(0, 0, 1)],
    "xy": [(1, 0, 0), (0, 1, 0), (1, 1, 0)],
    "xyc": [(0, 0, 1), (0, 1, 0), (0, 1, 1), (1, 0, 0), (1, 0, 1), (1, 1, 0), (1, 1, 1)],
}


def _rank(group, px, py, pc):
    if group == "c":
        return pc
    if group == "xy":
        return 2 * px + py
    return 4 * px + 2 * py + pc


def _flip(rel, x, y, c):
    dx, dy, dc = rel
    return (1 - x if dx else x, 1 - y if dy else y, 1 - c if dc else c)


def _pieces(ref, nchunk):
    step = ref.shape[0] // nchunk
    return [ref.at[pl.ds(q * step, step)] for q in range(nchunk)]


def _exchange(name, srcs, group, scatter, *, local=True, nchunks=None):
    rels = _GROUPS[group]
    gsize = len(rels) + 1
    n = len(srcs)
    nchunks = nchunks or [1] * n
    blks = [s.shape[1:] if scatter else s.shape for s in srcs]
    slotted = local or gsize > 2
    base = [sum(nchunks[:a]) for a in range(n)]
    tot = sum(nchunks)

    def body(*refs):
        src_refs, out_refs = refs[:n], refs[n:2 * n]
        send_sems, recv_sems, loc_sems = refs[2 * n:]
        x, y, c = lax.axis_index("x"), lax.axis_index("y"), lax.axis_index("c")
        me = _rank(group, x, y, c)
        copies = []
        for a in range(n):
            def part(r, a=a):
                return src_refs[a].at[r] if scatter else src_refs[a]
            dst = out_refs[a].at[me] if slotted else out_refs[a]
            if local:
                for q, (s_, d_) in enumerate(zip(_pieces(part(me), nchunks[a]), _pieces(dst, nchunks[a]))):
                    loc = pltpu.make_async_copy(s_, d_, loc_sems.at[base[a] + q])
                    loc.start()
                    copies.append(loc)
            for k, rel in enumerate(rels):
                peer = _flip(rel, x, y, c)
                for q, (s_, d_) in enumerate(zip(_pieces(part(_rank(group, *peer)), nchunks[a]),
                                                 _pieces(dst, nchunks[a]))):
                    cp = pltpu.make_async_remote_copy(
                        src_ref=s_, dst_ref=d_, send_sem=send_sems.at[(base[a] + q) * len(rels) + k],
                        recv_sem=recv_sems.at[(base[a] + q) * len(rels) + k],
                        device_id=peer, device_id_type=_MESH)
                    cp.start()
                    copies.append(cp)
        for cp in copies:
            cp.wait()

    any_spec = pl.BlockSpec(memory_space=pl.ANY)
    lead = (gsize,) if slotted else ()
    return pl.pallas_call(
        body, name=name,
        out_shape=[jax.ShapeDtypeStruct(lead + tuple(b), s.dtype) for b, s in zip(blks, srcs)],
        in_specs=[any_spec] * n, out_specs=[any_spec] * n,
        scratch_shapes=[pltpu.SemaphoreType.DMA((tot * len(rels),)), pltpu.SemaphoreType.DMA((tot * len(rels),)),
                        pltpu.SemaphoreType.DMA((tot,))],
    )(*srcs)


def _pair_fill(name, arrs, nchunks):
    n = len(arrs)
    base = [sum(nchunks[:a]) for a in range(n)]
    tot = sum(nchunks)

    def body(*refs):
        out_refs = refs[n:2 * n]
        send_sems, recv_sems = refs[2 * n:]
        x, y, c = lax.axis_index("x"), lax.axis_index("y"), lax.axis_index("c")
        copies = []
        for a in range(n):
            for q, blk in enumerate(_pieces(out_refs[a].at[c], nchunks[a])):
                cp = pltpu.make_async_remote_copy(
                    src_ref=blk, dst_ref=blk, send_sem=send_sems.at[base[a] + q], recv_sem=recv_sems.at[base[a] + q],
                    device_id=(x, y, 1 - c), device_id_type=_MESH)
                cp.start()
                copies.append(cp)
        for cp in copies:
            cp.wait()

    any_spec = pl.BlockSpec(memory_space=pl.ANY)
    return pl.pallas_call(
        body, name=name, out_shape=[jax.ShapeDtypeStruct(a.shape, a.dtype) for a in arrs],
        in_specs=[any_spec] * n, out_specs=[any_spec] * n, input_output_aliases={a: a for a in range(n)},
        scratch_shapes=[pltpu.SemaphoreType.DMA((tot,)), pltpu.SemaphoreType.DMA((tot,))],
    )(*arrs)


def _gather_weights(wb, nchunks):
    n = len(wb)
    rels = _GROUPS["xy"]
    base = [sum(nchunks[:a]) for a in range(n)]
    tot = sum(nchunks)

    def body(*refs):
        src_refs, out_refs = refs[:n], refs[n:2 * n]
        ici_send, ici_recv, d2d_send, d2d_recv, loc_sems = refs[2 * n:]
        x, y, c = lax.axis_index("x"), lax.axis_index("y"), lax.axis_index("c")
        me = 2 * x + y
        waits = []
        for a in range(n):
            for l in range(2):
                for q, (s_, d_) in enumerate(zip(_pieces(src_refs[a].at[l], nchunks[a]),
                                                 _pieces(out_refs[a].at[me, l], nchunks[a]))):
                    loc = pltpu.make_async_copy(s_, d_, loc_sems.at[(base[a] + q) * 2 + l])
                    loc.start()
                    waits.append(loc)
        first = []
        for a in range(n):
            for k, rel in enumerate(rels):
                px, py, _ = _flip(rel, x, y, c)
                for q, (s_, d_) in enumerate(zip(_pieces(src_refs[a].at[c], nchunks[a]),
                                                 _pieces(out_refs[a].at[me, c], nchunks[a]))):
                    sem = (base[a] + q) * 3 + k
                    cp = pltpu.make_async_remote_copy(src_ref=s_, dst_ref=d_, send_sem=ici_send.at[sem],
                                                      recv_sem=ici_recv.at[sem], device_id=(px, py, c),
                                                      device_id_type=_MESH)
                    cp.start()
                    first.append(cp)
        second = []
        for a in range(n):
            for k, rel in enumerate(rels):
                px, py, _ = _flip(rel, x, y, c)
                for q, blk in enumerate(_pieces(out_refs[a].at[2 * px + py, c], nchunks[a])):
                    sem = (base[a] + q) * 3 + k
                    landed = pltpu.make_async_remote_copy(src_ref=blk, dst_ref=blk, send_sem=ici_send.at[sem],
                                                          recv_sem=ici_recv.at[sem], device_id=(px, py, c),
                                                          device_id_type=_MESH)
                    landed.wait_recv()
                    cp = pltpu.make_async_remote_copy(src_ref=blk, dst_ref=blk, send_sem=d2d_send.at[sem],
                                                      recv_sem=d2d_recv.at[sem], device_id=(x, y, 1 - c),
                                                      device_id_type=_MESH)
                    cp.start()
                    second.append(cp)
        for cp in first:
            cp.wait_send()
        for cp in second:
            cp.wait_send()
        for a in range(n):
            for k, rel in enumerate(rels):
                px, py, _ = _flip(rel, x, y, c)
                for q, blk in enumerate(_pieces(out_refs[a].at[2 * px + py, 1 - c], nchunks[a])):
                    sem = (base[a] + q) * 3 + k
                    pltpu.make_async_remote_copy(src_ref=blk, dst_ref=blk, send_sem=d2d_send.at[sem],
                                                 recv_sem=d2d_recv.at[sem], device_id=(x, y, 1 - c),
                                                 device_id_type=_MESH).wait_recv()
        for cp in waits:
            cp.wait()

    any_spec = pl.BlockSpec(memory_space=pl.ANY)
    return pl.pallas_call(
        body, name="gather_weights",
        out_shape=[jax.ShapeDtypeStruct((N_CHIPS,) + a.shape, a.dtype) for a in wb],
        in_specs=[any_spec] * n, out_specs=[any_spec] * n,
        scratch_shapes=[pltpu.SemaphoreType.DMA((tot * 3,))] * 4 + [pltpu.SemaphoreType.DMA((tot * 2,))],
    )(*wb)


_HBM = pl.BlockSpec(memory_space=pltpu.HBM)
_SEM = pl.BlockSpec(memory_space=pltpu.SEMAPHORE)
_EFFECT = pltpu.SideEffectType.DATAFLOW_SIDE_EFFECTING


def _own_slot(name, src, chip, tb, slots=N_CHIPS):
    rows, cols = src.shape[-2:]
    lead = src.shape[:-2]
    flat = src.reshape((-1, cols))

    def body(s_ref, a_ref, o_ref):
        o_ref[...] = a_ref[...]

    grid_spec = pltpu.PrefetchScalarGridSpec(
        num_scalar_prefetch=1, grid=(flat.shape[0] // tb,),
        in_specs=[pl.BlockSpec((tb, cols), lambda i, s: (i, 0))],
        out_specs=pl.BlockSpec((None, tb, cols), lambda i, s: (s[0], i, 0)))
    out = pl.pallas_call(body, name=name, grid_spec=grid_spec,
                         out_shape=jax.ShapeDtypeStruct((slots,) + flat.shape, src.dtype),
                         compiler_params=_params(1))(chip, flat)
    return out.reshape((slots,) + lead + (rows, cols))


def _numbered(pairs, peer, send_sems, recv_sems, first):
    return [pltpu.make_async_remote_copy(src_ref=s_, dst_ref=d_, send_sem=send_sems.at[first + q],
                                         recv_sem=recv_sems.at[first + q], device_id=peer, device_id_type=_MESH)
            for q, (s_, d_) in enumerate(pairs)]


def _gather_plan(n, nchunks):
    def plan(refs, send_sems, recv_sems):
        x, y, c = lax.axis_index("x"), lax.axis_index("y"), lax.axis_index("c")
        me = 2 * x + y
        copies = []
        for a in range(n):
            for rel in _GROUPS["xy"]:
                px, py, _ = _flip(rel, x, y, c)
                pairs = list(zip(_pieces(refs[a], nchunks[a]), _pieces(refs[n + a].at[me], nchunks[a])))
                copies += _numbered(pairs, (px, py, c), send_sems, recv_sems, len(copies))
        return copies
    return plan, 3 * sum(nchunks)


def _all_plan():
    def plan(refs, send_sems, recv_sems):
        x, y, c = lax.axis_index("x"), lax.axis_index("y"), lax.axis_index("c")
        me = 4 * x + 2 * y + c
        copies = []
        for rel in _GROUPS["xyc"]:
            copies += _numbered([(refs[0], refs[1].at[me])], _flip(rel, x, y, c), send_sems, recv_sems, len(copies))
        return copies
    return plan, len(_GROUPS["xyc"])


def _pair_plan(n, nchunks):
    def plan(refs, send_sems, recv_sems):
        x, y, c = lax.axis_index("x"), lax.axis_index("y"), lax.axis_index("c")
        copies = []
        for a in range(n):
            for j in range(N_CHIPS):
                pairs = list(zip(_pieces(refs[a].at[j, 1 - c], nchunks[a]), _pieces(refs[n + a].at[j], nchunks[a])))
                copies += _numbered(pairs, (x, y, 1 - c), send_sems, recv_sems, len(copies))
        return copies
    return plan, N_CHIPS * sum(nchunks)


def _chips_plan(n, nchunks):
    def plan(refs, send_sems, recv_sems):
        x, y, c = lax.axis_index("x"), lax.axis_index("y"), lax.axis_index("c")
        me = 2 * x + y
        copies = []
        for a in range(n):
            for rel in _GROUPS["xy"]:
                px, py, _ = _flip(rel, x, y, c)
                pairs = list(zip(_pieces(refs[a].at[2 * px + py], nchunks[a]), _pieces(refs[n + a].at[me], nchunks[a])))
                copies += _numbered(pairs, (px, py, c), send_sems, recv_sems, len(copies))
        return copies
    return plan, 3 * sum(nchunks)


def _fill_plan(n, nchunks, l):
    def plan(refs, send_sems, recv_sems):
        x, y, c = lax.axis_index("x"), lax.axis_index("y"), lax.axis_index("c")
        copies = []
        for a in range(n):
            blk = _pieces(refs[a].at[l, c], nchunks[a])
            copies += _numbered(list(zip(blk, blk)), (x, y, 1 - c), send_sems, recv_sems, len(copies))
        return copies
    return plan, sum(nchunks)


def _split_start(name, arrays, plan, nsem, after):
    n = len(arrays)
    na = len(after)

    def body(*refs):
        send_sems, recv_sems = refs[n + na], refs[n + na + 1]
        token = refs[-1]
        for cp in plan(refs[:n], send_sems, recv_sems):
            cp.start()
        token[...] = jnp.zeros_like(token)

    hbm = [pltpu.HBM(a.shape, a.dtype) for a in arrays]
    outs = pl.pallas_call(
        body, name=name,
        out_shape=(pltpu.SemaphoreType.DMA((nsem,)), pltpu.SemaphoreType.DMA((nsem,)), *hbm, _sds((8, 128))),
        in_specs=[_HBM] * n + [pl.BlockSpec(memory_space=pl.ANY)] * na,
        out_specs=(_SEM, _SEM, *([_HBM] * n), pl.BlockSpec(memory_space=pltpu.VMEM)),
        input_output_aliases={i: 2 + i for i in range(n)},
        compiler_params=pltpu.CompilerParams(has_side_effects=_EFFECT),
    )(*[pltpu.with_memory_space_constraint(a, pltpu.HBM) for a in arrays], *after)
    return outs[0], outs[1], list(outs[2:2 + n]), outs[-1]


def _split_wait(name, send_sems, recv_sems, arrays, plan, after):
    n = len(arrays)

    def body(*refs):
        for cp in plan(refs[:n], refs[n], refs[n + 1]):
            cp.wait_send()
            cp.wait_recv()

    hbm = [pltpu.HBM(a.shape, a.dtype) for a in arrays]
    return list(pl.pallas_call(
        body, name=name, out_shape=tuple(hbm),
        in_specs=[_HBM] * n + [_SEM, _SEM] + [pl.BlockSpec(memory_space=pl.ANY)] * len(after),
        out_specs=tuple([_HBM] * n), input_output_aliases={i: i for i in range(n)},
        compiler_params=pltpu.CompilerParams(has_side_effects=_EFFECT),
    )(*arrays, send_sems, recv_sems, *after))


def _mm(name, a, b, out_sds, *, grid, a_spec, b_spec, o_spec, dims, acc_shape, into=None):
    nk = grid[2]

    def body(*refs):
        a_ref, b_ref = refs[0], refs[1]
        o_ref, acc = refs[-2], refs[-1]
        k = pl.program_id(2)
        part = lax.dot_general(a_ref[...].astype(_MXU), b_ref[...].astype(_MXU), dims,
                               preferred_element_type=_F32)
        if nk == 1:
            o_ref[...] = part.astype(o_ref.dtype)
            return

        @pl.when(k == 0)
        def _():
            acc[...] = part

        @pl.when(k > 0)
        def _():
            acc[...] += part

        @pl.when(k == nk - 1)
        def _():
            o_ref[...] = acc[...].astype(o_ref.dtype).reshape(o_ref.shape)

    if nk == 1:
        acc_shape = (8, 128)
    in_specs = [a_spec, b_spec]
    args = [a, b]
    aliases = {}
    if into is not None:
        in_specs.append(pl.BlockSpec(memory_space=pl.ANY))
        args.append(into)
        aliases = {2: 0}
    return pl.pallas_call(
        body, name=name, grid=grid, in_specs=in_specs, out_specs=o_spec, out_shape=out_sds,
        scratch_shapes=[pltpu.VMEM(acc_shape, _F32)], input_output_aliases=aliases,
        compiler_params=_params(3))(*args)


_NN = (((1,), (0,)), ((), ()))
_NT = (((1,), (1,)), ((), ()))
_TN = (((0,), (0,)), ((), ()))


def _rowwise(name, body, *, grid, ins, outs, scratch=()):
    return pl.pallas_call(
        body, name=name, grid=(grid,), in_specs=[s for _, s in ins], out_specs=[s for _, s in outs],
        out_shape=[o for o, _ in outs], scratch_shapes=list(scratch),
        compiler_params=_params(1))(*[a for a, _ in ins])


def _rows(tb, w, cb=0, n=None):
    if n is None:
        return pl.BlockSpec((tb, w), lambda i: (i, cb))
    return pl.BlockSpec((tb, w), lambda i: (n - 1 - i, cb))


def _vec(shape):
    return pl.BlockSpec(shape, lambda i: (0,) * len(shape))


def _halo_prev(tb, w, cb=0, n=None, rows=8):
    if n is None:
        return pl.BlockSpec((rows, w), lambda i: (jnp.maximum(i * (tb // rows) - 1, 0), cb))
    return pl.BlockSpec((rows, w), lambda i: (jnp.maximum((n - 1 - i) * (tb // rows) - 1, 0), cb))


def _halo_next(tb, w, n, cb=0):
    return pl.BlockSpec((8, w), lambda i: (jnp.minimum((i + 1) * (tb // 8), n * (tb // 8) - 1), cb))


def _sds(shape, dtype=_F32):
    return jax.ShapeDtypeStruct(shape, dtype)


def _cast(name, a, tb):
    rows, cols = a.shape

    def body(a_ref, o_ref):
        o_ref[...] = a_ref[...].astype(o_ref.dtype)

    return _rowwise(name, body, grid=rows // tb, ins=[(a, _rows(tb, cols))],
                    outs=[(_sds((rows, cols), _MXU), _rows(tb, cols))])[0]


def _sum_lead(name, a, tb):
    g, rows, cols = a.shape

    def body(a_ref, o_ref):
        acc = a_ref[0]
        for k in range(1, g):
            acc = acc + a_ref[k]
        o_ref[...] = acc

    return _rowwise(name, body, grid=rows // tb,
                    ins=[(a, pl.BlockSpec((g, tb, cols), lambda i: (0, i, 0)))],
                    outs=[(_sds((rows, cols)), _rows(tb, cols))])[0]


def _sum_pair(name, mine, theirs, core, tb):
    nj, _, rows, cols = mine.shape

    def body(s_ref, a_ref, b_ref, o_ref, ob_ref):
        t = a_ref[...] + b_ref[...]
        o_ref[...] = t
        ob_ref[...] = t.astype(ob_ref.dtype)

    blk = pl.BlockSpec((None, tb, cols), lambda j, i, s: (j, i, 0))
    grid_spec = pltpu.PrefetchScalarGridSpec(
        num_scalar_prefetch=1, grid=(nj, rows // tb),
        in_specs=[pl.BlockSpec((None, None, tb, cols), lambda j, i, s: (j, s[0], i, 0)), blk],
        out_specs=[blk, blk])
    return pl.pallas_call(body, name=name, grid_spec=grid_spec,
                          out_shape=[_sds((nj, rows, cols)), _sds((nj, rows, cols), _MXU)],
                          compiler_params=_params(2))(core, mine, theirs)


def _sum_chips(name, mine, theirs, where, l, into, tb):
    _, rows, cols = mine.shape
    extra = [] if into is None else [into]

    def body(*refs):
        a_ref, b1_ref, b2_ref, b3_ref = refs[1:5]
        o_ref = refs[-1]
        o_ref[...] = ((a_ref[...] + b1_ref[...].astype(_F32)) + b2_ref[...].astype(_F32)) + b3_ref[...].astype(_F32)

    def slot(k):
        return pl.BlockSpec((None, tb, cols), lambda i, s: (jnp.bitwise_xor(s[0], k), i, 0))

    grid_spec = pltpu.PrefetchScalarGridSpec(
        num_scalar_prefetch=1, grid=(rows // tb,),
        in_specs=[slot(0), slot(1), slot(2), slot(3)] + [pl.BlockSpec(memory_space=pl.ANY)] * len(extra),
        out_specs=pl.BlockSpec((None, None, tb, cols), lambda i, s: (l, s[1], i, 0)))
    return pl.pallas_call(body, name=name, grid_spec=grid_spec, out_shape=_sds((2, 2, rows, cols)),
                          input_output_aliases={5: 0} if extra else {},
                          compiler_params=_params(1))(where, mine, theirs, theirs, theirs, *extra)


def _adamw(name, w, g, m, v, tb):
    rows, cols = w.shape
    c1 = 1.0 - ADAM_B1 ** ADAM_STEP
    c2 = 1.0 - ADAM_B2 ** ADAM_STEP

    def body(w_ref, g_ref, m_ref, v_ref, d_ref, nm_ref, nv_ref):
        gv = g_ref[...]
        nm = ADAM_B1 * m_ref[...] + (1.0 - ADAM_B1) * gv
        nv = ADAM_B2 * v_ref[...] + (1.0 - ADAM_B2) * (gv * gv)
        d_ref[...] = -ADAM_LR * ((nm / c1) / (jnp.sqrt(nv / c2) + ADAM_EPS) + ADAM_WD * w_ref[...])
        nm_ref[...] = nm
        nv_ref[...] = nv

    spec = _rows(tb, cols)
    return _rowwise(name, body, grid=rows // tb, ins=[(w, spec), (g, spec), (m, spec), (v, spec)],
                    outs=[(_sds((rows, cols)), spec)] * 3)


def _mod_fwd(c_all, w_mod, b_cols):
    cols = w_mod.shape[2]

    def body(c_ref, w_ref, b_ref, o_ref):
        cv = c_ref[...]
        sc = (cv * _sigmoid(cv)).astype(_MXU)
        o_ref[...] = jnp.dot(sc, w_ref[...].astype(_MXU), preferred_element_type=_F32) + b_ref[...]

    return pl.pallas_call(
        body, name="mod_fwd", grid=(2,),
        in_specs=[pl.BlockSpec((8, D), lambda l: (0, 0)), pl.BlockSpec((None, D, cols), lambda l: (l, 0, 0)),
                  pl.BlockSpec((None, 1, cols), lambda l: (l, 0, 0))],
        out_specs=pl.BlockSpec((None, 8, cols), lambda l: (l, 0, 0)),
        out_shape=_sds((2, 8, cols)), compiler_params=_params(1))(c_all, w_mod, b_cols)


def _mod_bwd(c_all_t, dm):
    cols = dm.shape[2]

    def body(c_ref, d_ref, o_ref):
        cv = c_ref[...]
        sc = (cv * _sigmoid(cv)).astype(_MXU)
        o_ref[...] = jnp.dot(sc, d_ref[...].astype(_MXU), preferred_element_type=_F32)

    return pl.pallas_call(
        body, name="mod_bwd", grid=(2,),
        in_specs=[pl.BlockSpec((D, 8), lambda l: (0, 0)), pl.BlockSpec((None, 8, cols), lambda l: (l, 0, 0))],
        out_specs=pl.BlockSpec((None, D, cols), lambda l: (l, 0, 0)),
        out_shape=_sds((2, D, cols)), compiler_params=_params(1))(c_all_t, dm)


def _proj(x, g_pre, shift, scale, w_in):
    s = x.shape[0]
    tm = 1024

    def body(x_ref, g_ref, sh_ref, sc_ref, w_ref, o_ref, ht_ref, h_s):
        @pl.when(pl.program_id(1) == 0)
        def _():
            xv = x_ref[...]
            rstd = lax.rsqrt(jnp.mean(xv * xv, axis=-1, keepdims=True) + NORM_EPS)
            hv = (xv * rstd) * g_ref[...] * (1.0 + sc_ref[...]) + sh_ref[...]
            h_s[...] = hv.astype(h_s.dtype)
            ht_ref[...] = hv.T.astype(ht_ref.dtype)

        o_ref[...] = jnp.dot(h_s[...], w_ref[...], preferred_element_type=_F32).astype(o_ref.dtype)

    vec = pl.BlockSpec((1, D), lambda m, n: (0, 0))
    return pl.pallas_call(
        body, name="proj", grid=(s // tm, N_CHIPS),
        in_specs=[pl.BlockSpec((tm, D), lambda m, n: (m, 0)), vec, vec, vec,
                  pl.BlockSpec((None, D, 2304), lambda m, n: (n, 0, 0))],
        out_specs=[pl.BlockSpec((tm, 2304), lambda m, n: (m, n)), pl.BlockSpec((D, tm), lambda m, n: (0, m))],
        out_shape=[_sds((s, IN_W), _MXU), _sds((D, s), _MXU)],
        scratch_shapes=[pltpu.VMEM((tm, D), _MXU)], compiler_params=_params(2))(x, g_pre, shift, scale, w_in)


def _shift_down(cur, halo, j, tb):
    ext = jnp.concatenate([halo, cur], axis=0)
    return pltpu.roll(ext, j, 0)[8:8 + tb]


def _shift_up(cur, halo, j, tb):
    ext = jnp.concatenate([cur, halo], axis=0)
    return pltpu.roll(ext, tb + 8 - j, 0)[0:tb]


def _conv_fwd(proj, conv_w, conv_b):
    s = proj.shape[0]
    tb = 512

    def body(u_ref, hp_ref, w_ref, b_ref, o_ref):
        i = pl.program_id(0)
        u = u_ref[...].astype(_F32)
        halo = jnp.where(i > 0, hp_ref[...].astype(_F32)[8:16], 0.0)
        acc = b_ref[...] + u * w_ref[0:1, :]
        for j in range(1, 4):
            acc = acc + _shift_down(u, halo, j, tb) * w_ref[j:j + 1, :]
        o_ref[...] = acc

    return _rowwise("conv_fwd", body, grid=s // tb,
                    ins=[(proj, _rows(tb, D, CB_U)), (proj, _halo_prev(tb, D, CB_U, rows=16)),
                         (conv_w, _vec((4, D))), (conv_b, _vec((1, D)))],
                    outs=[(_sds((s, D)), _rows(tb, D))])[0]


def _lru_gates(pre_r, pre_i, uc, b_rg, b_ig, lam):
    r = _sigmoid(pre_r + b_rg)
    ig = _sigmoid(pre_i + b_ig)
    nl = -lam
    sp = jnp.maximum(nl, 0.0) + jnp.log(1.0 + jnp.exp(-jnp.abs(nl)))
    la = -LRU_C * r * sp
    a = jnp.exp(la)
    one_m_a2 = -jnp.tanh(la) * (a * a + 1.0)
    inv_sq = lax.rsqrt(jnp.maximum(one_m_a2, 1e-30))
    return r, ig, sp, a, one_m_a2 * inv_sq, inv_sq


GATE_TILES = 8


def _gate_tiles(w_rg, w_ig):
    eye = jnp.eye(2, dtype=w_rg.dtype)

    def tiles(w):
        return jnp.einsum("cpij,pq->cpiqj", w.reshape(GATE_TILES, 2, 64, 64), eye).reshape(GATE_TILES, 128, 128)

    return jnp.concatenate([tiles(w_rg), tiles(w_ig)], axis=2)


def _gate_tile_grads(gw):
    keep = jnp.eye(2, dtype=jnp.bool_)[None, :, None, :, None]

    def blocks(t):
        t5 = t.reshape(GATE_TILES, 2, 64, 2, 64)
        return jnp.sum(jnp.where(keep, t5, 0.0), axis=3).reshape(16, 64, 64)

    return blocks(gw[:, :, 0:128]), blocks(gw[:, :, 128:256])


def _gate_preacts(ucv, wt_ref):
    ucb = ucv.astype(_MXU)
    ps = [jnp.dot(ucb[:, 128 * c:128 * (c + 1)], wt_ref[c], preferred_element_type=_F32) for c in range(GATE_TILES)]
    pre_r = jnp.concatenate([p[:, 0:128] for p in ps], axis=1)
    pre_i = jnp.concatenate([p[:, 128:256] for p in ps], axis=1)
    return pre_r, pre_i


def _scan_fwd(uc, wt, b_rg, b_ig, lam):
    s = uc.shape[0]
    tb = 256

    def body(uc_ref, wt_ref, brg_ref, big_ref, lam_ref, h_ref, carry, a_s, b_s):
        i = pl.program_id(0)

        @pl.when(i == 0)
        def _():
            carry[...] = jnp.zeros_like(carry)

        ucv = uc_ref[...]
        pre_r, pre_i = _gate_preacts(ucv, wt_ref)
        _, ig, _, a, sq, _ = _lru_gates(pre_r, pre_i, ucv, brg_ref[...], big_ref[...], lam_ref[...])
        av = a
        bv = sq * (ig * ucv)
        av = av.reshape(tb // 8, 8, D)
        bv = bv.reshape(tb // 8, 8, D)
        row8 = lax.broadcasted_iota(jnp.int32, (1, 8, 1), 1)
        for sh in (1, 2, 4):
            m = row8 >= sh
            b_sh = pltpu.roll(bv, sh, 1)
            a_sh = pltpu.roll(av, sh, 1)
            bv = jnp.where(m, av * b_sh + bv, bv)
            av = jnp.where(m, av * a_sh, av)
        a_s[...] = av.reshape(tb, D)
        b_s[...] = bv.reshape(tb, D)

        def tile(t, state):
            rows = pl.ds(pl.multiple_of(t * 8, 8), 8)
            hv = b_s[rows, :] + a_s[rows, :] * state
            b_s[rows, :] = hv
            return jnp.broadcast_to(hv[7:8, :], (8, D))

        carry[...] = lax.fori_loop(0, tb // 8, tile, jnp.broadcast_to(carry[7:8, :], (8, D)), unroll=4)
        h_ref[...] = b_s[...].astype(h_ref.dtype)

    v = _vec((1, D))
    return _rowwise("scan_fwd", body, grid=s // tb,
                    ins=[(uc, _rows(tb, D)), (wt, _vec((GATE_TILES, 128, 256))), (b_rg, v), (b_ig, v), (lam, v)],
                    outs=[(_sds((s, D), _MXU), _rows(tb, D))],
                    scratch=[pltpu.VMEM((8, D), _F32), pltpu.VMEM((tb, D), _F32), pltpu.VMEM((tb, D), _F32)])[0]


def _weight_specs(l):
    return [pl.BlockSpec((N_CHIPS, None, ATT_W, 256), lambda i: (0, l, 0, 0)),
            pl.BlockSpec((N_CHIPS, None, 256, D), lambda i: (0, l, 0, 0)),
            pl.BlockSpec((N_CHIPS, None, 256, D), lambda i: (0, l, 0, 0))]


def _tail_fwd(l, o, h_lru, proj, x, gate, g_post, gw, target):
    s = x.shape[0]
    tb = 512

    def body(*refs):
        o_ref, h_ref, ga_ref, gl_ref, ma_ref, mb_ref, x_ref, gt_ref, gp_ref, wpa_ref, wpb_ref, wo_ref = refs[0:12]
        aa_ref, ba_ref, ya_ref, yb_ref, z_ref, out_ref = refs[-8:-2] if target is not None else refs[-7:-1]
        ga = ga_ref[...].astype(_F32)
        aa32 = o_ref[...] * (ga * _sigmoid(ga))
        aa = aa32.astype(_MXU)
        aa_ref[...] = aa32.T.astype(aa_ref.dtype)
        gl = gl_ref[...].astype(_F32)
        ba32 = h_ref[...].astype(_F32) * (gl * _sigmoid(gl))
        ba = ba32.astype(_MXU)
        ba_ref[...] = ba32.T.astype(ba_ref.dtype)
        ya = jnp.concatenate([jnp.dot(aa, wpa_ref[j], preferred_element_type=_F32) for j in range(N_CHIPS)], axis=1)
        ya_ref[...] = ya.astype(ya_ref.dtype)
        yb = jnp.dot(ba, wpb_ref[...].reshape(D, D), preferred_element_type=_F32)
        yb_ref[...] = yb.astype(yb_ref.dtype)
        z32 = _sigmoid(ma_ref[...].astype(_F32)) * ya + _sigmoid(mb_ref[...].astype(_F32)) * yb
        z = z32.astype(_MXU)
        z_ref[...] = z32.T.astype(z_ref.dtype)
        ov = jnp.dot(z, wo_ref[...].reshape(D, D), preferred_element_type=_F32)
        out_ref[...] = ov.astype(out_ref.dtype)
        rstd = lax.rsqrt(jnp.mean(ov * ov, axis=-1, keepdims=True) + NORM_EPS)
        xn =x_ref[...] + gt_ref[...] * ((ov * rstd) * gp_ref[...])
        if target is None:
            refs[-1][...] = xn
        else:
            dy_ref, acc_ref = refs[-2], refs[-1]
            err = xn - refs[12][...]
            dy_ref[...] = err * (1.0 / D)
            _zero_first(pl.program_id(0), acc_ref)
            acc_ref[...] += jnp.sum(err * err, axis=0, keepdims=True)

    v = _vec((1, D))
    r = _rows(tb, D)
    r5 = _rows(tb, ATT_W)
    weights = list(zip((gw["w_pa"], gw["w_pb"], gw["w_o"]), _weight_specs(l)))
    cols = pl.BlockSpec((D, tb), lambda i: (0, i))
    head_in = [] if target is None else [(target, r)]
    head_out = [] if target is None else [(_sds((1, D)), v)]
    return _rowwise("tail_fwd" if target is None else "tail_loss_fwd", body, grid=s // tb,
                    ins=[(o, r5), (h_lru, r), (proj, _rows(tb, ATT_W, CB_GATT)), (proj, _rows(tb, D, CB_GLRU)),
                         (proj, _rows(tb, D, CB_MA)), (proj, _rows(tb, D, CB_MB)), (x, r), (gate, v), (g_post, v)]
                    + weights + head_in,
                    outs=[(_sds((ATT_W, s), _MXU), pl.BlockSpec((ATT_W, tb), lambda i: (0, i))),
                          (_sds((D, s), _MXU), cols), (_sds((s, D), _MXU), r), (_sds((s, D), _MXU), r),
                          (_sds((D, s), _MXU), cols), (_sds((s, D), _MXU), r), (_sds((s, D)), r)]
                    + head_out)


def _zero_first(i, *refs):
    @pl.when(i == 0)
    def _():
        for ref in refs:
            ref[...] = jnp.zeros_like(ref)


def _tail_bwd(l, dx, out, y_a, y_b, proj, o, h_lru, gate, g_post, gw):
    s = dx.shape[0]
    tb = 256

    def body(dx_ref, out_ref, ya_ref, yb_ref, ma_ref, mb_ref, o_ref, ga_ref, h_ref, gl_ref, gt_ref, gp_ref,
             wpa_ref, wpb_ref, wo_ref,
             dout_ref, dya_ref, dyb_ref, rest_ref, do_ref, dh_ref, dgt_ref, dgp_ref):
        i = pl.program_id(0)
        ov = out_ref[...].astype(_F32)
        dxv = dx_ref[...]
        rstd = lax.rsqrt(jnp.mean(ov * ov, axis=-1, keepdims=True) + NORM_EPS)
        nv = ov * rstd
        s_dn = jnp.sum(dxv * nv, axis=0, keepdims=True)
        _zero_first(i, dgt_ref, dgp_ref)
        dgt_ref[...] += s_dn * gp_ref[...]
        dgp_ref[...] += s_dn * gt_ref[...]
        dn = dxv * (gt_ref[...] * gp_ref[...])
        d_out = (rstd * (dn - nv * jnp.mean(dn * nv, axis=-1, keepdims=True))).astype(_MXU)
        dout_ref[...] = d_out
        dz = lax.dot_general(d_out, wo_ref[...].reshape(D, D), _NT, preferred_element_type=_F32)
        ga = _sigmoid(ma_ref[...].astype(_F32))
        gb = _sigmoid(mb_ref[...].astype(_F32))
        dya = (dz * ga).astype(_MXU)
        dyb = (dz * gb).astype(_MXU)
        dya_ref[...] = dya
        dyb_ref[...] = dyb
        rest_ref[:, R_MA:R_MB] = (dz * ya_ref[...].astype(_F32) * ga * (1.0 - ga)).astype(rest_ref.dtype)
        rest_ref[:, R_MB:R_END] = (dz * yb_ref[...].astype(_F32) * gb * (1.0 - gb)).astype(rest_ref.dtype)
        daa = lax.dot_general(dya[:, 0:256], wpa_ref[0], _NT, preferred_element_type=_F32)
        for j in range(1, N_CHIPS):
            daa = daa + lax.dot_general(dya[:, j * 256:(j + 1) * 256], wpa_ref[j], _NT, preferred_element_type=_F32)
        dba = lax.dot_general(dyb, wpb_ref[...].reshape(D, D), _NT, preferred_element_type=_F32)
        gav = ga_ref[...].astype(_F32)
        sa = _sigmoid(gav)
        do_ref[...] = daa * (gav * sa)
        rest_ref[:, 0:R_U] = (daa * o_ref[...] * (sa * (1.0 + gav * (1.0 - sa)))).astype(rest_ref.dtype)
        gl = gl_ref[...].astype(_F32)
        sl = _sigmoid(gl)
        dh_ref[...] = dba * (gl * sl)
        rest_ref[:, R_GLRU:R_MA] = (dba * h_ref[...].astype(_F32)
                                    * (sl * (1.0 + gl * (1.0 - sl)))).astype(rest_ref.dtype)

    v = _vec((1, D))
    r5, r10 = _rows(tb, ATT_W), _rows(tb, D)
    return _rowwise("tail_bwd", body, grid=s // tb,
                    ins=[(dx, r10), (out, r10), (y_a, r10), (y_b, r10), (proj, _rows(tb, D, CB_MA)),
                         (proj, _rows(tb, D, CB_MB)), (o, r5), (proj, _rows(tb, ATT_W, CB_GATT)), (h_lru, r10),
                         (proj, _rows(tb, D, CB_GLRU)), (gate, v), (g_post, v)]
                    + list(zip((gw["w_pa"], gw["w_pb"], gw["w_o"]), _weight_specs(l))),
                    outs=[(_sds((s, D), _MXU), r10), (_sds((s, D), _MXU), r10), (_sds((s, D), _MXU), r10),
                          (_sds((s, R_END), _MXU), _rows(tb, R_END)),
                          (_sds((s, ATT_W)), r5), (_sds((s, D)), r10), (_sds((1, D)), v), (_sds((1, D)), v)])


def _scan_bwd(dh, uc, h_lru, wt, b_rg, b_ig, lam):
    s = uc.shape[0]
    tb = 256
    n = s // tb

    def body(dh_ref, uc_ref, h_ref, hp_ref, wt_ref, brg_ref, big_ref, lam_ref,
             duc_ref, dwt_ref, dbrg_ref, dbig_ref, dlam_ref, carry, c_s, g_s):
        i = pl.program_id(0)

        @pl.when(i == 0)
        def _():
            carry[...] = jnp.zeros_like(carry)
            for acc_ref in (dwt_ref, dbrg_ref, dbig_ref, dlam_ref):
                acc_ref[...] = jnp.zeros_like(acc_ref)

        ucv = uc_ref[...]
        pre_r, pre_i = _gate_preacts(ucv, wt_ref)
        r, ig, sp, a, sq, inv_sq =_lru_gates(pre_r, pre_i, ucv, brg_ref[...], big_ref[...], lam_ref[...])
        row = lax.broadcasted_iota(jnp.int32, (tb, 1), 0)
        cv = jnp.where(row == tb - 1, 1.0, pltpu.roll(a, tb - 1, 0))
        gv = dh_ref[...]
        cv = cv.reshape(tb // 8, 8, D)
        gv = gv.reshape(tb // 8, 8, D)
        row8 = lax.broadcasted_iota(jnp.int32, (1, 8, 1), 1)
        for sh in (1, 2, 4):
            m = row8 < 8 - sh
            g_sh = pltpu.roll(gv, 8 - sh, 1)
            c_sh = pltpu.roll(cv, 8 - sh, 1)
            gv = jnp.where(m, gv + cv * g_sh, gv)
            cv = jnp.where(m, cv * c_sh, cv)
        c_s[...] = cv.reshape(tb, D)
        g_s[...] = gv.reshape(tb, D)

        def tile(k, state):
            rows = pl.ds(pl.multiple_of((tb // 8 - 1 - k) * 8, 8), 8)
            gt = g_s[rows, :] + c_s[rows, :] * state
            g_s[rows, :] = gt
            return jnp.broadcast_to(gt[0:1, :], (8, D))

        lax.fori_loop(0, tb // 8, tile, jnp.broadcast_to(carry[0:1, :], (8, D)), unroll=4)
        gv = g_s[...]
        carry[...] = (a * gv)[0:8]

        halo = jnp.where(i < n - 1, hp_ref[...].astype(_F32)[8:16], 0.0)
        h_prev = _shift_down(h_ref[...].astype(_F32), halo, 1, tb)
        d_a = gv * h_prev
        d_sq = gv * (ig * ucv)
        d_i = gv * sq * ucv
        d_la = d_a * a - d_sq * (a * a) * inv_sq
        d_r = d_la * (-LRU_C * sp)
        d_pre_r = d_r * r * (1.0 - r)
        d_pre_i = d_i * ig * (1.0 - ig)
        ucb = ucv.astype(_MXU)
        dpr = d_pre_r.astype(_MXU)
        dpi = d_pre_i.astype(_MXU)
        back = []
        for c in range(GATE_TILES):
            lanes = slice(128 * c, 128 * (c + 1))
            dp = jnp.concatenate([dpr[:, lanes], dpi[:, lanes]], axis=1)
            back.append(lax.dot_general(dp, wt_ref[c], _NT, preferred_element_type=_F32))
            dwt_ref[c] += lax.dot_general(ucb[:, lanes], dp, _TN, preferred_element_type=_F32)
        duc_ref[...] = gv * sq * ig + jnp.concatenate(back, axis=1)
        dbrg_ref[...] += jnp.sum(d_pre_r, axis=0, keepdims=True)
        dbig_ref[...] += jnp.sum(d_pre_i, axis=0, keepdims=True)
        lamv = lam_ref[...]
        dlam_ref[...] += jnp.sum(d_la * (-LRU_C * r), axis=0, keepdims=True) * (-_sigmoid(-lamv))

    v = _vec((1, D))
    rv = _rows(tb, D, 0, n)
    return _rowwise("scan_bwd", body, grid=n,
                    ins=[(dh, rv), (uc, rv), (h_lru, rv), (h_lru, _halo_prev(tb, D, 0, n, rows=16)),
                         (wt, _vec((GATE_TILES, 128, 256))), (b_rg, v), (b_ig, v), (lam, v)],
                    outs=[(_sds((s, D)), rv), (_sds((GATE_TILES, 128, 256)), _vec((GATE_TILES, 128, 256))),
                          (_sds((1, D)), v), (_sds((1, D)), v), (_sds((1, D)), v)],
                    scratch=[pltpu.VMEM((8, D), _F32), pltpu.VMEM((tb, D), _F32), pltpu.VMEM((tb, D), _F32)])


def _conv_bwd(duc_a, proj, conv_w, rest):
    s = duc_a.shape[0]
    tb = 512
    n = s // tb
    hw = D // 2

    def body(da_ref, dan_ref, u_ref, up_ref, w_ref, rest_in, du_ref, dw_ref, dbias_ref):
        i = pl.program_id(1)
        duc = da_ref[...]
        nxt = jnp.where(i < n - 1, dan_ref[...], 0.0)
        u = u_ref[...].astype(_F32)
        halo = jnp.where(i > 0, up_ref[...].astype(_F32)[8:16], 0.0)
        du = duc * w_ref[0:1, :]
        dws = [jnp.sum(duc * u, axis=0, keepdims=True)]
        for j in range(1, 4):
            du = du + _shift_up(duc, nxt, j, tb) * w_ref[j:j + 1, :]
            dws.append(jnp.sum(duc * _shift_down(u, halo, j, tb), axis=0, keepdims=True))
        du_ref[...] = du.astype(du_ref.dtype)
        _zero_first(i, dw_ref, dbias_ref)
        for j in range(4):
            dw_ref[j:j + 1, :] += dws[j]
        dbias_ref[...] += jnp.sum(duc, axis=0, keepdims=True)

    r = pl.BlockSpec((tb, hw), lambda h, i: (i, h))
    nxt_spec = pl.BlockSpec((8, hw), lambda h, i: (jnp.minimum((i + 1) * (tb // 8), n * (tb // 8) - 1), h))
    return pl.pallas_call(
        body, name="conv_bwd", grid=(2, n),
        in_specs=[r, nxt_spec,
                  pl.BlockSpec((tb, hw), lambda h, i: (i, 2 * CB_U + h)),
                  pl.BlockSpec((16, hw), lambda h, i: (jnp.maximum(i * (tb // 16) - 1, 0), 2 * CB_U + h)),
                  pl.BlockSpec((4, hw), lambda h, i: (0, h)), pl.BlockSpec(memory_space=pl.ANY)],
        out_specs=[pl.BlockSpec((tb, hw), lambda h, i: (i, R_U // hw + h)),
                   pl.BlockSpec((4, hw), lambda h, i: (0, h)), pl.BlockSpec((1, hw), lambda h, i: (0, h))],
        out_shape=[_sds(rest.shape, rest.dtype), _sds((4, D)), _sds((1, D))],
        input_output_aliases={5: 0}, compiler_params=_params(2),
    )(duc_a, duc_a, proj, proj, conv_w, rest)


def _band_tiles(dil):
    tiles = []
    for rho in range(dil):
        for b in range(16 // dil):
            qs = rho + dil * BAND * b
            tiles.append((qs, QBLK + qs - dil * BAND, b))
    return tiles


def _strided(start, size, dil):
    return pl.ds(start, size, stride=dil) if dil > 1 else pl.ds(start, size)


def _band_mask(i, b):
    qi = lax.broadcasted_iota(jnp.int32, (BAND, 2 * BAND), 0)
    ki = lax.broadcasted_iota(jnp.int32, (BAND, 2 * BAND), 1)
    valid = (ki >= qi) & (ki <= qi + BAND)
    if b == 0:
        valid = valid & ((ki >= BAND) | (i > 0))
    return valid


def _attn_fwd(proj):
    s = proj.shape[0]
    n = s // QBLK
    scale = HEAD ** -0.5

    def body(*refs):
        q_refs, kp_refs, kc_refs, vp_refs, vc_refs = (refs[3 * t:3 * t + 3] for t in range(5))
        o_ref, lse_ref, qbuf, kbuf, vbuf = refs[15:20]
        accs, maxs, dens = refs[20:23], refs[23:26], refs[26:29]
        i = pl.program_id(1)
        for g, dil in enumerate(DILATIONS):
            qbuf[...] = q_refs[g][...].astype(_F32)
            kbuf[0:QBLK, :] = kp_refs[g][...].astype(_F32)
            kbuf[QBLK:2 * QBLK, :] = kc_refs[g][...].astype(_F32)
            vbuf[0:QBLK, :] = vp_refs[g][...].astype(_F32)
            vbuf[QBLK:2 * QBLK, :] = vc_refs[g][...].astype(_F32)
            for qs, ks, b in _band_tiles(dil):
                qsl = _strided(qs, BAND, dil)
                q = qbuf[qsl, :].astype(_MXU)
                kk = kbuf[_strided(ks, 2 * BAND, dil), :].astype(_MXU)
                vv = vbuf[_strided(ks, 2 * BAND, dil), :].astype(_MXU)
                sc = lax.dot_general(q, kk, _NT, preferred_element_type=_F32) * scale
                sc = jnp.where(_band_mask(i, b), sc, NEG_INF)
                m = jnp.max(sc, axis=-1, keepdims=True)
                p = jnp.exp(sc - m)
                accs[g][qsl, :] = jnp.dot(p.astype(_MXU), vv, preferred_element_type=_F32)
                maxs[g][qsl, :] = jnp.broadcast_to(m, (BAND, HEAD))
                dens[g][qsl, :] = jnp.broadcast_to(jnp.sum(p, axis=-1, keepdims=True), (BAND, HEAD))
        ms = [r[...] for r in maxs]
        mx = jnp.maximum(jnp.maximum(ms[0], ms[1]), ms[2])
        ws = [jnp.exp(m - mx) for m in ms]
        den = ws[0] * dens[0][...] + ws[1] * dens[1][...] + ws[2] * dens[2][...]
        o_ref[...] = (ws[0] * accs[0][...] + ws[1] * accs[1][...] + ws[2] * accs[2][...]) / den
        lse_ref[...] = mx + jnp.log(den)

    blk = (QBLK, HEAD)

    def spec(first_col, lag):
        specs = []
        for g in range(3):
            col = first_col + g * HEADS
            if lag:
                specs.append(pl.BlockSpec(blk, lambda j, i, col=col: (jnp.maximum(i - 1, 0), col + j)))
            else:
                specs.append(pl.BlockSpec(blk, lambda j, i, col=col: (i, col + j)))
        return specs

    out_spec = pl.BlockSpec(blk, lambda j, i: (i, j))
    return pl.pallas_call(
        body, name="attn_fwd", grid=(HEADS, n),
        in_specs=spec(0, False) + spec(12, True) + spec(12, False) + spec(24, True) + spec(24, False),
        out_specs=[out_spec] * 2, out_shape=[_sds((s, ATT_W))] * 2,
        scratch_shapes=[pltpu.VMEM(blk, _F32)] + [pltpu.VMEM((2 * QBLK, HEAD), _F32)] * 2
        + [pltpu.VMEM(blk, _F32)] * 9,
        compiler_params=_params(2))(*([proj] * 15))


def _attn_bwd(proj, d_o, o, lse, g, into):
    s = proj.shape[0]
    dil = DILATIONS[g]
    n = s // QBLK
    scale = HEAD ** -0.5
    tiles = _band_tiles(dil)

    def body(*refs):
        q_ref, kp_ref, kc_ref, vp_ref, vc_ref, do_ref, o_ref, lse_ref = refs[0:8]
        dq_ref, dk_ref, dv_ref, kbuf, vbuf, dkbuf, dvbuf, dqbuf, qbuf = refs[-9:]
        i = pl.program_id(1)

        @pl.when(i == 0)
        def _():
            dkbuf[0:QBLK, :] = jnp.zeros((QBLK, HEAD), _F32)
            dvbuf[0:QBLK, :] = jnp.zeros((QBLK, HEAD), _F32)

        @pl.when(i < n)
        def _():
            qbuf[...] = q_ref[...].astype(_F32)
            kbuf[0:QBLK, :] = kp_ref[...].astype(_F32)
            kbuf[QBLK:2 * QBLK, :] = kc_ref[...].astype(_F32)
            vbuf[0:QBLK, :] = vp_ref[...].astype(_F32)
            vbuf[QBLK:2 * QBLK, :] = vc_ref[...].astype(_F32)
            dkbuf[QBLK:2 * QBLK, :] = jnp.zeros((QBLK, HEAD), _F32)
            dvbuf[QBLK:2 * QBLK, :] = jnp.zeros((QBLK, HEAD), _F32)
            for qs, ks, b in tiles:
                qsl = _strided(qs, BAND, dil)
                ksl = _strided(ks, 2 * BAND, dil)
                q = qbuf[qsl, :].astype(_MXU)
                kk = kbuf[ksl, :].astype(_MXU)
                vv = vbuf[ksl, :].astype(_MXU)
                dov = do_ref[qsl, :]
                dd = jnp.sum(dov * o_ref[qsl, :], axis=-1, keepdims=True)
                lse_t = lse_ref[qsl, :][:, 0:1]
                sc = lax.dot_general(q, kk, _NT, preferred_element_type=_F32) * scale
                p = jnp.where(_band_mask(i, b), jnp.exp(sc - lse_t), 0.0)
                dob = dov.astype(_MXU)
                dp = lax.dot_general(dob, vv, _NT, preferred_element_type=_F32)
                ds = (p * (dp - dd) * scale).astype(_MXU)
                dqbuf[qsl, :] = jnp.dot(ds, kk, preferred_element_type=_F32)
                dkbuf[ksl, :] += lax.dot_general(ds, q, _TN, preferred_element_type=_F32)
                dvbuf[ksl, :] += lax.dot_general(p.astype(_MXU), dob, _TN, preferred_element_type=_F32)
            dq_ref[...] = dqbuf[...].astype(dq_ref.dtype)

        dk_ref[...] = dkbuf[0:QBLK, :].astype(dk_ref.dtype)
        dv_ref[...] = dvbuf[0:QBLK, :].astype(dv_ref.dtype)
        dkbuf[0:QBLK, :] = dkbuf[QBLK:2 * QBLK, :]
        dvbuf[0:QBLK, :] = dvbuf[QBLK:2 * QBLK, :]

    blk = (QBLK, HEAD)
    cq, ck, cv = g * HEADS, 12 + g * HEADS, 24 + g * HEADS

    def cur(i):
        return jnp.minimum(i, n - 1)

    def prev(i):
        return jnp.maximum(jnp.minimum(i, n - 1) - 1, 0)

    own = pl.BlockSpec(blk, lambda j, i: (cur(i), j))
    own_out = pl.BlockSpec(blk, lambda j, i: (cur(i), cq + j))
    late_out = pl.BlockSpec(blk, lambda j, i: (jnp.maximum(i - 1, 0), cq + j))
    extra = [] if into is None else list(into)
    return pl.pallas_call(
        body, name="attn_bwd_d%d" % dil, grid=(HEADS, n + 1),
        in_specs=[pl.BlockSpec(blk, lambda j, i: (cur(i), cq + j)),
                  pl.BlockSpec(blk, lambda j, i: (prev(i), ck + j)),
                  pl.BlockSpec(blk, lambda j, i: (cur(i), ck + j)),
                  pl.BlockSpec(blk, lambda j, i: (prev(i), cv + j)),
                  pl.BlockSpec(blk, lambda j, i: (cur(i), cv + j)),
                  own, own, own] + [pl.BlockSpec(memory_space=pl.ANY)] * len(extra),
        out_specs=[own_out, late_out, late_out], out_shape=[_sds((s, QKV_W), _MXU)] * 3,
        input_output_aliases={8 + t: t for t in range(len(extra))},
        scratch_shapes=[pltpu.VMEM((2 * QBLK, HEAD), _F32)] * 4 + [pltpu.VMEM((QBLK, HEAD), _F32)] * 2,
        compiler_params=_params(2))(proj, proj, proj, proj, proj, d_o, o, lse, *extra)


_PARTS = ((0, 2), (2, 2), (4, 2), (6, 6))
_CHUNK = 768


def _d_x(parts, w_in, x, dx_out, g_pre, scale):
    s = parts[0].shape[0]
    nk = IN_W // _CHUNK

    def body(p0, p1, p2, p3, w_ref, x_ref, dxo_ref, g_ref, sc_ref, dx_ref, dsh_ref, dsc_ref, dg_ref, acc):
        m = pl.program_id(0)
        k = pl.program_id(2)

        @pl.when(k == 0)
        def _():
            acc[...] = jnp.zeros_like(acc)

        @pl.when((k == 0) & (m == 0))
        def _():
            for ref in (dsh_ref, dsc_ref, dg_ref):
                ref[...] = jnp.zeros_like(ref)

        for p_ref, (first, cnt) in zip((p0, p1, p2, p3), _PARTS):
            @pl.when((k >= first) & (k < first + cnt))
            def _(p_ref=p_ref):
                acc[...] += lax.dot_general(p_ref[...].astype(_MXU), w_ref[...], _NT, preferred_element_type=_F32)

        @pl.when(k == nk - 1)
        def _():
            dhv = acc[...]
            xv = x_ref[...]
            rstd = lax.rsqrt(jnp.mean(xv * xv, axis=-1, keepdims=True) + NORM_EPS)
            xn = xv * rstd
            one_sc = 1.0 + sc_ref[...]
            s1 = jnp.sum(dhv * xn, axis=0, keepdims=True)
            dsh_ref[...] += jnp.sum(dhv, axis=0, keepdims=True)
            dsc_ref[...] += s1 * g_ref[...]
            dg_ref[...] += s1 * one_sc
            dxn = dhv * (g_ref[...] * one_sc)
            dx_ref[...] = dxo_ref[...] + rstd * (dxn - xn * jnp.mean(dxn * xn, axis=-1, keepdims=True))

    def part_spec(first, cnt):
        return pl.BlockSpec((1024, _CHUNK), lambda m, n, k: (m, jnp.clip(k - first, 0, cnt - 1)))

    rows = pl.BlockSpec((1024, D), lambda m, n, k: (m, 0))
    vec = pl.BlockSpec((1, D), lambda m, n, k: (0, 0))
    return pl.pallas_call(
        body, name="d_x", grid=(s // 1024, 1, nk),
        in_specs=[part_spec(*p) for p in _PARTS]
        + [pl.BlockSpec((None, D, _CHUNK), lambda m, n, k: (k // 3, 0, k % 3)), rows, rows, vec, vec],
        out_specs=[rows, vec, vec, vec], out_shape=[_sds((s, D)), _sds((1, D)), _sds((1, D)), _sds((1, D))],
        scratch_shapes=[pltpu.VMEM((1024, D), _F32)], compiler_params=_params(3))(
            *parts, w_in, x, dx_out, g_pre, scale)


def _g_w_in(h_t, parts):
    s = h_t.shape[1]
    nk = s // 1024

    def body(*refs):
        h_ref, p_refs = refs[0], refs[1:5]
        o_ref, acc = refs[-2], refs[-1]
        n = pl.program_id(1)
        k = pl.program_id(2)

        @pl.when(k == 0)
        def _():
            acc[...] = jnp.zeros_like(acc)

        for p_ref, (first, cnt) in zip(p_refs, _PARTS):
            @pl.when((n >= first) & (n < first + cnt))
            def _(p_ref=p_ref):
                acc[...] += jnp.dot(h_ref[...], p_ref[...].astype(_MXU), preferred_element_type=_F32)

        @pl.when(k == nk - 1)
        def _():
            o_ref[...] = acc[...]

    def part_spec(first, cnt):
        def index(m, n, k):
            row = jnp.where(n < first, 0, jnp.where(n >= first + cnt, nk - 1, k))
            return (row, jnp.clip(n - first, 0, cnt - 1))
        return pl.BlockSpec((1024, _CHUNK), index)

    return pl.pallas_call(
        body, name="g_w_in", grid=(1, IN_W // _CHUNK, nk),
        in_specs=[pl.BlockSpec((D, 1024), lambda m, n, k: (0, k))] + [part_spec(*p) for p in _PARTS],
        out_specs=pl.BlockSpec((None, D, _CHUNK), lambda m, n, k: (n // 3, 0, n % 3)),
        out_shape=_sds((N_CHIPS, D, 2304)),
        scratch_shapes=[pltpu.VMEM((D, _CHUNK), _F32)], compiler_params=_params(3))(h_t, *parts)


def _layer_fwd(l, x, p, gw, late, target):
    proj, h_t = _proj(x, p["g_pre"], p["shift"], p["scale"], gw["w_in"][l])
    o, lse = _attn_fwd(proj)
    uc = _conv_fwd(proj, p["conv_w"], p["conv_b"])
    h_lru = _scan_fwd(uc, p["wt"], p["b_rg"], p["b_ig"], p["lam"])
    if late is not None:
        landed = dict(late(h_lru))
        gw["w_in"].append(landed.pop("w_in1"))
        gw.update(landed)
    a_att, b_act, y_a, y_b, z, out, *last = _tail_fwd(l, o, h_lru, proj, x, p["gate"], p["g_post"], gw, target)
    saved = dict(x=x, h_t=h_t, proj=proj, o=o, lse=lse, uc=uc, h_lru=h_lru, a_att=a_att, b_act=b_act,
                 y_a=y_a, y_b=y_b, z=z, out=out)
    return (last[0] if target is None else last), saved


def _layer_bwd(l, dx, p, gw, sv, hooks):
    s = dx.shape[0]
    nt = s // 2048
    proj = sv["proj"]
    gate, b_rg, g_pre = p["gate"], p["b_rg"], p["g_pre"]
    if hooks is not None:
        gate = gate + hooks[0]([dx])
    d_out, dy_a, dy_b, d_rest, d_o, dh_lru, d_gate, d_gpost = _tail_bwd(
        l, dx, sv["out"], sv["y_a"], sv["y_b"], proj, sv["o"], sv["h_lru"], gate, p["g_post"], gw)
    if hooks is not None:
        b_rg = b_rg + hooks[1]([d_out])

    def wgrad_rows(name, a, b):
        return _mm(name, a, b, _sds((N_CHIPS, 256, D)), grid=(1, 1, nt),
                   a_spec=pl.BlockSpec((D, 2048), lambda m, n, k: (0, k)),
                   b_spec=pl.BlockSpec((2048, D), lambda m, n, k: (k, 0)),
                   o_spec=pl.BlockSpec((N_CHIPS, 256, D), lambda m, n, k: (0, 0, 0)),
                   dims=_NN, acc_shape=(D, D))

    big = {}
    big["w_o"] = wgrad_rows("g_w_o", sv["z"], d_out)
    big["w_pa"] = _mm("g_w_pa", sv["a_att"], dy_a, _sds((N_CHIPS, ATT_W, 256)), grid=(1, 4, nt),
                      a_spec=pl.BlockSpec((ATT_W, 2048), lambda m, n, k: (0, k)),
                      b_spec=pl.BlockSpec((2048, 256), lambda m, n, k: (k, n)),
                      o_spec=pl.BlockSpec((None, ATT_W, 256), lambda m, n, k: (n, 0, 0)),
                      dims=_NN, acc_shape=(ATT_W, 256))
    big["w_pb"] = wgrad_rows("g_w_pb", sv["b_act"], dy_b)
    duc, g_wt, d_brg, d_big, d_lam = _scan_bwd(dh_lru, sv["uc"], sv["h_lru"], p["wt"], b_rg, p["b_ig"], p["lam"])
    g_wrg, g_wig = _gate_tile_grads(g_wt)
    d_rest, g_convw, g_convb = _conv_bwd(duc, proj, p["conv_w"], d_rest)
    dqkv = None
    for g in range(3):
        dqkv = _attn_bwd(proj, d_o, sv["o"], sv["lse"], g, dqkv)
    if hooks is not None:
        g_pre = g_pre + hooks[2]([dqkv[0]])
    parts = (dqkv[0], dqkv[1], dqkv[2], d_rest)
    big["w_in"] = _g_w_in(sv["h_t"], parts)
    if hooks is not None:
        g_pre = g_pre + hooks[3](big)
    dx_in, d_shift, d_scale, d_gpre = _d_x(parts, gw["w_in"][l], sv["x"], dx, g_pre, p["scale"])
    small = dict(dmod=jnp.concatenate([d_shift, d_scale, d_gate], axis=1), g_pre=d_gpre, conv_w=g_convw,
                 conv_b=g_convb, w_rg=g_wrg, b_rg=d_brg, w_ig=g_wig, b_ig=d_big, lam=d_lam, g_post=d_gpost)
    return dx_in, small, big


_BIG = ("w_in", "w_pa", "w_pb", "w_o")


class _GradReduce:
    PAIR_CHUNKS = (2, 1, 1, 1)
    CHIP_CHUNKS = (2, 1, 1, 1)
    FILL_CHUNKS = (4, 1, 1, 1)

    def __init__(self, core, where):
        self.core, self.where = core, where
        self.finals = None

    def begin(self, l, big):
        n = len(_BIG)
        halves = [big[k].reshape(N_CHIPS, 2, big[k].shape[1] // 2, big[k].shape[2]) for k in _BIG]
        lands = [lax.empty((N_CHIPS,) + h.shape[2:], _F32) for h in halves]
        plan, nsem = _pair_plan(n, self.PAIR_CHUNKS)
        state = {}
        state["pair"] = _split_start("reduce_pair_start_%d" % l, halves + lands, plan, nsem, [])

        def started(after):
            return state["pair"][3][0, 0]

        def pair_done(after):
            send, recv, arrays, _ = state["pair"]
            arrays = _split_wait("reduce_pair_wait_%d" % l, send, recv, arrays, plan, after)
            sums = [_sum_pair("sum_pair_%s_%d" % (k, l), arrays[a], arrays[n + a], self.core, 128)
                    for a, k in enumerate(_BIG)]
            state["mine"] = [t[0] for t in sums]
            lands2 = [lax.empty(t[1].shape, _MXU) for t in sums]
            plan2, nsem2 = _chips_plan(n, self.CHIP_CHUNKS)
            state["plan2"] = plan2
            state["chips"] = _split_start("reduce_chips_start_%d" % l, [t[1] for t in sums] + lands2, plan2, nsem2, [])
            return state["chips"][3][0, 0]

        def chips_done(after):
            send, recv, arrays, _ = state["chips"]
            arrays = _split_wait("reduce_chips_wait_%d" % l, send, recv, arrays, state["plan2"], after)
            finals = [_sum_chips("sum_chips_%s_%d" % (k, l), state["mine"][a], arrays[n + a], self.where, l,
                                 None if self.finals is None else self.finals[a], 128)
                      for a, k in enumerate(_BIG)]
            plan3, nsem3 = _fill_plan(n, self.FILL_CHUNKS, l)
            state["plan3"] = plan3
            state["fill"] = _split_start("gather_halves_start_%d" % l, finals, plan3, nsem3, [])
            return state["fill"][3][0, 0]

        def finish(after):
            send, recv, arrays, _ = state["fill"]
            self.finals = _split_wait("gather_halves_wait_%d" % l, send, recv, arrays, state["plan3"], after)
            return self.finals

        self._finish = finish
        return [started, pair_done, chips_done]

    def finish(self, after):
        return self._finish(after)


def _local_step(x, target, small_p, w_in0, late, reducer, on_smalls):
    saved = []
    h = x
    gw = dict(w_in=[w_in0])
    h, sv = _layer_fwd(0, h, small_p[0], gw, late, None)
    saved.append(sv)
    (dy, sq), sv = _layer_fwd(1, h, small_p[1], gw, None, target)
    saved.append(sv)
    loss = 0.5 * jnp.sum(sq) / D
    smalls = [None, None]
    dx, smalls[1], big1 = _layer_bwd(1, dy, small_p[1], gw, saved[1], None)
    hooks1 = reducer.begin(1, big1)
    small_started = on_smalls(1, smalls[1])
    pair_started = hooks1[0]
    hooks1[0] = lambda after: pair_started(after) + small_started
    own = {}

    def layer0_ready(big0):
        reducer.finish([big0["w_in"]])
        own["hooks"] = reducer.begin(0, big0)
        return own["hooks"][0]([])

    dx, smalls[0], _ = _layer_bwd(0, dx, small_p[0], gw, saved[0], hooks1 + [layer0_ready])
    own["hooks"][1]([dx])
    on_smalls(0, smalls[0])

    def finish_reduce(after):
        own["hooks"][2](after)
        return reducer.finish(after)

    return loss, dx, smalls, finish_reduce


_SMALL_ROWS = 8 + 8 + 8 + 64 + 64
_SMALL_VECS = ("g_pre", "conv_b", "b_rg", "b_ig", "lam", "g_post")


def _pack_small(small):
    pad = lambda rows: jnp.zeros((rows, D), _F32)
    return jnp.concatenate(
        [small["dmod"].reshape(3, D), pad(5)] + [small[k] for k in _SMALL_VECS] + [pad(2)]
        + [small["conv_w"], pad(4), small["w_rg"].reshape(64, D), small["w_ig"].reshape(64, D)], axis=0)


def kernel(x, c, w_mod, b_mod, g_pre, w_in, conv_w, conv_b, w_rg, b_rg, w_ig, b_ig, lru_lambda, w_pa, w_pb, w_o, g_post, loss_target, m_w_mod, m_b_mod, m_g_pre, m_w_in, m_conv_w, m_conv_b, m_w_rg, m_b_rg, m_w_ig, m_b_ig, m_lru_lambda, m_w_pa, m_w_pb, m_w_o, m_g_post, v_w_mod, v_b_mod, v_g_pre, v_w_in, v_conv_w, v_conv_b, v_w_rg, v_b_rg, v_w_ig, v_b_ig, v_lru_lambda, v_w_pa, v_w_pb, v_w_o, v_g_post):
    xi, yi, ci = lax.axis_index("x"), lax.axis_index("y"), lax.axis_index("c")
    chip = 2 * xi + yi
    dev = 4 * xi + 2 * yi + ci
    mcols = w_mod.shape[2]

    pack1 = jnp.concatenate([jnp.broadcast_to(c, (8, D)),
                             jnp.pad(conv_w.reshape(8, 256), ((0, 0), (0, D - 256)))], axis=0)
    g1 = _exchange("gather_cond", [pack1], "xyc", False)[0]
    c_all = g1[:, 0, :]
    conv_w_full = jnp.transpose(g1[0::2, 8:16, 0:256], (1, 0, 2)).reshape(2, 4, D)

    b_cols = lax.dynamic_slice(b_mod, (0, chip * mcols), (2, mcols)).reshape(2, 1, mcols)
    mod_loc = _mod_fwd(c_all, w_mod, b_cols)
    g2 = _exchange("gather_mod", [mod_loc.reshape(16, mcols)], "xyc", False)[0]
    mod_full = jnp.transpose(g2[0::2], (1, 0, 2)).reshape(2, 8, 3 * D)
    mod_me = lax.dynamic_index_in_dim(mod_full, dev, axis=1, keepdims=False)

    wb_in = _cast("cast_w_in", w_in.reshape(2 * D, 2304), 256).reshape(2, D, 2304)
    late_src = [wb_in[1], _cast("cast_w_pa", w_pa.reshape(2 * ATT_W, 256), 256).reshape(2, ATT_W, 256),
                _cast("cast_w_pb", w_pb.reshape(512, D), 256).reshape(2, 256, D),
                _cast("cast_w_o", w_o.reshape(512, D), 256).reshape(2, 256, D)]
    late_chunks = [4, 2, 2, 2]
    w_in0 = _gather_weights([wb_in[0].reshape(2, D // 2, 2304)], [2])[0].reshape(N_CHIPS, D, 2304)
    chip1 = jnp.reshape(chip, (1,)).astype(jnp.int32)
    lands = [_own_slot("own_slot_" + k, a, chip1, 256) for k, a in zip(("w_in", "w_pa", "w_pb", "w_o"), late_src)]
    late_plan, late_nsem = _gather_plan(len(late_src), late_chunks)
    send_sems, recv_sems, late_arrays, token = _split_start(
        "late_gather_start", late_src + lands, late_plan, late_nsem, [w_in0, mod_me])

    def late(after):
        got = _split_wait("late_gather_wait", send_sems, recv_sems, late_arrays, late_plan, [after])[len(late_src):]
        return dict(w_in1=got[0], w_pa=got[1], w_pb=got[2], w_o=got[3])

    small_p = []
    for l in range(2):
        gates = _gate_tiles(w_rg[l], w_ig[l]).astype(_MXU)
        small_p.append(dict(
            shift=mod_me[l:l + 1, 0:D], scale=mod_me[l:l + 1, D:2 * D], gate=mod_me[l:l + 1, 2 * D:3 * D],
            g_pre=g_pre[l:l + 1], conv_w=conv_w_full[l], conv_b=conv_b[l:l + 1], wt=gates,
            b_rg=b_rg[l:l + 1], b_ig=b_ig[l:l + 1], lam=lru_lambda[l:l + 1], g_post=g_post[l:l + 1]))

    small_p[0]["shift"] = small_p[0]["shift"] + token[0, 0]

    core = jnp.reshape(ci, (1,)).astype(jnp.int32)
    where = jnp.stack([chip, ci]).astype(jnp.int32)
    dev1 = jnp.reshape(dev, (1,)).astype(jnp.int32)
    small_plan, small_nsem = _all_plan()
    small_state = {}

    def on_smalls(l, small):
        pack = _pack_small(small)
        land = _own_slot("own_small_%d" % l, pack, dev1, _SMALL_ROWS, slots=8)
        small_state[l] = _split_start("gather_small_start_%d" % l, [pack, land], small_plan, small_nsem, [])
        return small_state[l][3][0, 0]

    def small_done(l, after):
        send, recv, arrays, _ = small_state[l]
        return _split_wait("gather_small_wait_%d" % l, send, recv, arrays, small_plan, after)[1]

    loss_loc, dx, _, finish_reduce = _local_step(x[0], loss_target[0], small_p, w_in0, late,
                                                 _GradReduce(core, where), on_smalls)
    loss = lax.psum(loss_loc, ("x", "y", "c"))
    grad_x = dx[None]
    reduced = finish_reduce([dx])
    g_big ={k: a.reshape(2, 2 * a.shape[2], a.shape[3]) for k, a in zip(_BIG, reduced)}

    g3 = [small_done(l, [dx]) for l in range(2)]
    tot = [_sum_lead("sum_small_%d" % l, g3[l], _SMALL_ROWS) for l in range(2)]
    dmod_all = jnp.stack([g3[l][:, 0:3, :].reshape(8, 3 * D) for l in range(2)], axis=0)
    dm_cols = lax.dynamic_slice(dmod_all, (0, 0, chip * mcols), (2, 8, mcols))
    g_w_mod = _mod_bwd(jnp.transpose(c_all), dm_cols)
    both = lambda first, rows: jnp.stack([tot[l][first:first + rows] for l in range(2)], axis=0)
    vec = both(8, 6)
    grads = dict(
        w_mod=g_w_mod, b_mod=both(0, 3).reshape(2, 3 * D), g_pre=vec[:, 0], w_in=g_big["w_in"],
        conv_w=lax.dynamic_slice(both(16, 4), (0, 0, chip * 256), (2, 4, 256)), conv_b=vec[:, 1],
        w_rg=both(24, 64).reshape(2, 16, 64, 64), b_rg=vec[:, 2], w_ig=both(88, 64).reshape(2, 16, 64, 64),
        b_ig=vec[:, 3], lru_lambda=vec[:, 4], w_pa=g_big["w_pa"], w_pb=g_big["w_pb"], w_o=g_big["w_o"],
        g_post=vec[:, 5])

    weights = dict(w_mod=w_mod, b_mod=b_mod, g_pre=g_pre, w_in=w_in, conv_w=conv_w, conv_b=conv_b, w_rg=w_rg,
                   b_rg=b_rg, w_ig=w_ig, b_ig=b_ig, lru_lambda=lru_lambda, w_pa=w_pa, w_pb=w_pb, w_o=w_o,
                   g_post=g_post)
    ms = dict(w_mod=m_w_mod, b_mod=m_b_mod, g_pre=m_g_pre, w_in=m_w_in, conv_w=m_conv_w, conv_b=m_conv_b,
              w_rg=m_w_rg, b_rg=m_b_rg, w_ig=m_w_ig, b_ig=m_b_ig, lru_lambda=m_lru_lambda, w_pa=m_w_pa,
              w_pb=m_w_pb, w_o=m_w_o, g_post=m_g_post)
    vs = dict(w_mod=v_w_mod, b_mod=v_b_mod, g_pre=v_g_pre, w_in=v_w_in, conv_w=v_conv_w, conv_b=v_conv_b,
              w_rg=v_w_rg, b_rg=v_b_rg, w_ig=v_w_ig, b_ig=v_b_ig, lru_lambda=v_lru_lambda, w_pa=v_w_pa,
              w_pb=v_w_pb, w_o=v_w_o, g_post=v_g_post)
    flat = dict(w_mod=(2 * D, mcols, 256), b_mod=(2, 3 * D, 2), g_pre=(2, D, 2), w_in=(2 * D, 2304, 256),
                conv_w=(8, 256, 8), conv_b=(2, D, 2), w_rg=(128, D, 128), b_rg=(2, D, 2), w_ig=(128, D, 128),
                b_ig=(2, D, 2), lru_lambda=(2, D, 2), w_pa=(2 * ATT_W, 256, 256), w_pb=(512, D, 256),
                w_o=(512, D, 256), g_post=(2, D, 2))
    order = ("w_mod", "b_mod", "g_pre", "w_in", "conv_w", "conv_b", "w_rg", "b_rg", "w_ig", "b_ig",
             "lru_lambda", "w_pa", "w_pb", "w_o", "g_post")
    deltas, new_m, new_v = [], [], []
    for k in order:
        rows, cols, tb = flat[k]
        shp = weights[k].shape
        d, nm_, nv_ = _adamw("adamw_" + k, weights[k].reshape(rows, cols), grads[k].reshape(rows, cols),
                             ms[k].reshape(rows, cols), vs[k].reshape(rows, cols), tb)
        deltas.append(d.reshape(shp))
        new_m.append(nm_.reshape(shp))
        new_v.append(nv_.reshape(shp))
    return (loss, grad_x, *[grads[k].reshape(weights[k].shape) for k in order], *deltas, *new_m, *new_v)
```

```python
import functools

import jax
import jax.numpy as jnp
from jax import lax
from jax.experimental import pallas as pl
from jax.experimental.pallas import tpu as pltpu

_F32 = jnp.float32
_MXU = jnp.bfloat16
_VMEM_LIMIT = 56 * 1024 * 1024
_MESH = pl.DeviceIdType.MESH

D = 1024
HEAD = 128
HEADS = 4
ATT_W = 512
QKV_W = 1536
IN_W = 9216
DILATIONS = (1, 4, 16)
BAND = 128
QBLK = BAND * 16
NORM_EPS = 1e-6
NEG_INF = -1e30
LRU_C = 8.0
N_CHIPS = 4
CB_GATT = 4608 // 512
CB_U, CB_GLRU, CB_MA, CB_MB = 5, 6, 7, 8
R_U, R_GLRU, R_MA, R_MB, R_END = 512, 1536, 2560, 3584, 4608

ADAM_LR, ADAM_B1, ADAM_B2, ADAM_EPS, ADAM_WD, ADAM_STEP = 0.001, 0.9, 0.999, 1e-08, 0.01, 10


def _params(ngrid):
    return pltpu.CompilerParams(dimension_semantics=("arbitrary",) * ngrid, vmem_limit_bytes=_VMEM_LIMIT)


def _sigmoid(v):
    return 0.5 * jnp.tanh(0.5 * v) + 0.5


_GROUPS = {
    "c": [(0, 0, 1)],
    "xy": [(1, 0, 0), (0, 1, 0), (1, 1, 0)],
    "xyc": [(0, 0, 1), (0, 1, 0), (0, 1, 1), (1, 0, 0), (1, 0, 1), (1, 1, 0), (1, 1, 1)],
}


def _rank(group, px, py, pc):
    if group == "c":
        return pc
    if group == "xy":
        return 2 * px + py
    return 4 * px + 2 * py + pc


def _flip(rel, x, y, c):
    dx, dy, dc = rel
    return (1 - x if dx else x, 1 - y if dy else y, 1 - c if dc else c)


def _pieces(ref, nchunk):
    step = ref.shape[0] // nchunk
    return [ref.at[pl.ds(q * step, step)] for q in range(nchunk)]


def _exchange(name, srcs, group, scatter, *, local=True, nchunks=None):
    rels = _GROUPS[group]
    gsize = len(rels) + 1
    n = len(srcs)
    nchunks = nchunks or [1] * n
    blks = [s.shape[1:] if scatter else s.shape for s in srcs]
    slotted = local or gsize > 2
    base = [sum(nchunks[:a]) for a in range(n)]
    tot = sum(nchunks)

    def body(*refs):
        src_refs, out_refs = refs[:n], refs[n:2 * n]
        send_sems, recv_sems, loc_sems = refs[2 * n:]
        x, y, c = lax.axis_index("x"), lax.axis_index("y"), lax.axis_index("c")
        me = _rank(group, x, y, c)
        copies = []
        for a in range(n):
            def part(r, a=a):
                return src_refs[a].at[r] if scatter else src_refs[a]
            dst = out_refs[a].at[me] if slotted else out_refs[a]
            if local:
                for q, (s_, d_) in enumerate(zip(_pieces(part(me), nchunks[a]), _pieces(dst, nchunks[a]))):
                    loc = pltpu.make_async_copy(s_, d_, loc_sems.at[base[a] + q])
                    loc.start()
                    copies.append(loc)
            for k, rel in enumerate(rels):
                peer = _flip(rel, x, y, c)
                for q, (s_, d_) in enumerate(zip(_pieces(part(_rank(group, *peer)), nchunks[a]),
                                                 _pieces(dst, nchunks[a]))):
                    cp = pltpu.make_async_remote_copy(
                        src_ref=s_, dst_ref=d_, send_sem=send_sems.at[(base[a] + q) * len(rels) + k],
                        recv_sem=recv_sems.at[(base[a] + q) * len(rels) + k],
                        device_id=peer, device_id_type=_MESH)
                    cp.start()
                    copies.append(cp)
        for cp in copies:
            cp.wait()

    any_spec = pl.BlockSpec(memory_space=pl.ANY)
    lead = (gsize,) if slotted else ()
    return pl.pallas_call(
        body, name=name,
        out_shape=[jax.ShapeDtypeStruct(lead + tuple(b), s.dtype) for b, s in zip(blks, srcs)],
        in_specs=[any_spec] * n, out_specs=[any_spec] * n,
        scratch_shapes=[pltpu.SemaphoreType.DMA((tot * len(rels),)), pltpu.SemaphoreType.DMA((tot * len(rels),)),
                        pltpu.SemaphoreType.DMA((tot,))],
    )(*srcs)


def _pair_fill(name, arrs, nchunks):
    n = len(arrs)
    base = [sum(nchunks[:a]) for a in range(n)]
    tot = sum(nchunks)

    def body(*refs):
        out_refs = refs[n:2 * n]
        send_sems, recv_sems = refs[2 * n:]
        x, y, c = lax.axis_index("x"), lax.axis_index("y"), lax.axis_index("c")
        copies = []
        for a in range(n):
            for q, blk in enumerate(_pieces(out_refs[a].at[c], nchunks[a])):
                cp = pltpu.make_async_remote_copy(
                    src_ref=blk, dst_ref=blk, send_sem=send_sems.at[base[a] + q], recv_sem=recv_sems.at[base[a] + q],
                    device_id=(x, y, 1 - c), device_id_type=_MESH)
                cp.start()
                copies.append(cp)
        for cp in copies:
            cp.wait()

    any_spec = pl.BlockSpec(memory_space=pl.ANY)
    return pl.pallas_call(
        body, name=name, out_shape=[jax.ShapeDtypeStruct(a.shape, a.dtype) for a in arrs],
        in_specs=[any_spec] * n, out_specs=[any_spec] * n, input_output_aliases={a: a for a in range(n)},
        scratch_shapes=[pltpu.SemaphoreType.DMA((tot,)), pltpu.SemaphoreType.DMA((tot,))],
    )(*arrs)


def _gather_weights(wb, nchunks):
    n = len(wb)
    rels = _GROUPS["xy"]
    base = [sum(nchunks[:a]) for a in range(n)]
    tot = sum(nchunks)

    def body(*refs):
        src_refs, out_refs = refs[:n], refs[n:2 * n]
        ici_send, ici_recv, d2d_send, d2d_recv, loc_sems = refs[2 * n:]
        x, y, c = lax.axis_index("x"), lax.axis_index("y"), lax.axis_index("c")
        me = 2 * x + y
        waits = []
        for a in range(n):
            for l in range(2):
                for q, (s_, d_) in enumerate(zip(_pieces(src_refs[a].at[l], nchunks[a]),
                                                 _pieces(out_refs[a].at[me, l], nchunks[a]))):
                    loc = pltpu.make_async_copy(s_, d_, loc_sems.at[(base[a] + q) * 2 + l])
                    loc.start()
                    waits.append(loc)
        first = []
        for a in range(n):
            for k, rel in enumerate(rels):
                px, py, _ = _flip(rel, x, y, c)
                for q, (s_, d_) in enumerate(zip(_pieces(src_refs[a].at[c], nchunks[a]),
                                                 _pieces(out_refs[a].at[me, c], nchunks[a]))):
                    sem = (base[a] + q) * 3 + k
                    cp = pltpu.make_async_remote_copy(src_ref=s_, dst_ref=d_, send_sem=ici_send.at[sem],
                                                      recv_sem=ici_recv.at[sem], device_id=(px, py, c),
                                                      device_id_type=_MESH)
                    cp.start()
                    first.append(cp)
        second = []
        for a in range(n):
            for k, rel in enumerate(rels):
                px, py, _ = _flip(rel, x, y, c)
                for q, blk in enumerate(_pieces(out_refs[a].at[2 * px + py, c], nchunks[a])):
                    sem = (base[a] + q) * 3 + k
                    landed = pltpu.make_async_remote_copy(src_ref=blk, dst_ref=blk, send_sem=ici_send.at[sem],
                                                          recv_sem=ici_recv.at[sem], device_id=(px, py, c),
                                                          device_id_type=_MESH)
                    landed.wait_recv()
                    cp = pltpu.make_async_remote_copy(src_ref=blk, dst_ref=blk, send_sem=d2d_send.at[sem],
                                                      recv_sem=d2d_recv.at[sem], device_id=(x, y, 1 - c),
                                                      device_id_type=_MESH)
                    cp.start()
                    second.append(cp)
        for cp in first:
            cp.wait_send()
        for cp in second:
            cp.wait_send()
        for a in range(n):
            for k, rel in enumerate(rels):
                px, py, _ = _flip(rel, x, y, c)
                for q, blk in enumerate(_pieces(out_refs[a].at[2 * px + py, 1 - c], nchunks[a])):
                    sem = (base[a] + q) * 3 + k
                    pltpu.make_async_remote_copy(src_ref=blk, dst_ref=blk, send_sem=d2d_send.at[sem],
                                                 recv_sem=d2d_recv.at[sem], device_id=(x, y, 1 - c),
                                                 device_id_type=_MESH).wait_recv()
        for cp in waits:
            cp.wait()

    any_spec = pl.BlockSpec(memory_space=pl.ANY)
    return pl.pallas_call(
        body, name="gather_weights",
        out_shape=[jax.ShapeDtypeStruct((N_CHIPS,) + a.shape, a.dtype) for a in wb],
        in_specs=[any_spec] * n, out_specs=[any_spec] * n,
        scratch_shapes=[pltpu.SemaphoreType.DMA((tot * 3,))] * 4 + [pltpu.SemaphoreType.DMA((tot * 2,))],
    )(*wb)


_HBM = pl.BlockSpec(memory_space=pltpu.HBM)
_SEM = pl.BlockSpec(memory_space=pltpu.SEMAPHORE)
_EFFECT = pltpu.SideEffectType.DATAFLOW_SIDE_EFFECTING


def _own_slot(name, src, chip, tb, slots=N_CHIPS):
    rows, cols = src.shape[-2:]
    lead = src.shape[:-2]
    flat = src.reshape((-1, cols))

    def body(s_ref, a_ref, o_ref):
        o_ref[...] = a_ref[...]

    grid_spec = pltpu.PrefetchScalarGridSpec(
        num_scalar_prefetch=1, grid=(flat.shape[0] // tb,),
        in_specs=[pl.BlockSpec((tb, cols), lambda i, s: (i, 0))],
        out_specs=pl.BlockSpec((None, tb, cols), lambda i, s: (s[0], i, 0)))
    out = pl.pallas_call(body, name=name, grid_spec=grid_spec,
                         out_shape=jax.ShapeDtypeStruct((slots,) + flat.shape, src.dtype),
                         compiler_params=_params(1))(chip, flat)
    return out.reshape((slots,) + lead + (rows, cols))


def _numbered(pairs, peer, send_sems, recv_sems, first):
    return [pltpu.make_async_remote_copy(src_ref=s_, dst_ref=d_, send_sem=send_sems.at[first + q],
                                         recv_sem=recv_sems.at[first + q], device_id=peer, device_id_type=_MESH)
            for q, (s_, d_) in enumerate(pairs)]


def _gather_plan(n, nchunks):
    def plan(refs, send_sems, recv_sems):
        x, y, c = lax.axis_index("x"), lax.axis_index("y"), lax.axis_index("c")
        me = 2 * x + y
        copies = []
        for a in range(n):
            for rel in _GROUPS["xy"]:
                px, py, _ = _flip(rel, x, y, c)
                pairs = list(zip(_pieces(refs[a], nchunks[a]), _pieces(refs[n + a].at[me], nchunks[a])))
                copies += _numbered(pairs, (px, py, c), send_sems, recv_sems, len(copies))
        return copies
    return plan, 3 * sum(nchunks)


def _all_plan():
    def plan(refs, send_sems, recv_sems):
        x, y, c = lax.axis_index("x"), lax.axis_index("y"), lax.axis_index("c")
        me = 4 * x + 2 * y + c
        copies = []
        for rel in _GROUPS["xyc"]:
            copies += _numbered([(refs[0], refs[1].at[me])], _flip(rel, x, y, c), send_sems, recv_sems, len(copies))
        return copies
    return plan, len(_GROUPS["xyc"])


def _pair_plan(n, nchunks):
    def plan(refs, send_sems, recv_sems):
        x, y, c = lax.axis_index("x"), lax.axis_index("y"), lax.axis_index("c")
        copies = []
        for a in range(n):
            for j in range(N_CHIPS):
                pairs = list(zip(_pieces(refs[a].at[j, 1 - c], nchunks[a]), _pieces(refs[n + a].at[j], nchunks[a])))
                copies += _numbered(pairs, (x, y, 1 - c), send_sems, recv_sems, len(copies))
        return copies
    return plan, N_CHIPS * sum(nchunks)


def _chips_plan(n, nchunks):
    def plan(refs, send_sems, recv_sems):
        x, y, c = lax.axis_index("x"), lax.axis_index("y"), lax.axis_index("c")
        me = 2 * x + y
        copies = []
        for a in range(n):
            for rel in _GROUPS["xy"]:
                px, py, _ = _flip(rel, x, y, c)
                pairs = list(zip(_pieces(refs[a].at[2 * px + py], nchunks[a]), _pieces(refs[n + a].at[me], nchunks[a])))
                copies += _numbered(pairs, (px, py, c), send_sems, recv_sems, len(copies))
        return copies
    return plan, 3 * sum(nchunks)


def _fill_plan(n, nchunks, l):
    def plan(refs, send_sems, recv_sems):
        x, y, c = lax.axis_index("x"), lax.axis_index("y"), lax.axis_index("c")
        copies = []
        for a in range(n):
            blk = _pieces(refs[a].at[l, c], nchunks[a])
            copies += _numbered(list(zip(blk, blk)), (x, y, 1 - c), send_sems, recv_sems, len(copies))
        return copies
    return plan, sum(nchunks)


def _split_start(name, arrays, plan, nsem, after):
    n = len(arrays)
    na = len(after)

    def body(*refs):
        send_sems, recv_sems = refs[n + na], refs[n + na + 1]
        token = refs[-1]
        for cp in plan(refs[:n], send_sems, recv_sems):
            cp.start()
        token[...] = jnp.zeros_like(token)

    hbm = [pltpu.HBM(a.shape, a.dtype) for a in arrays]
    outs = pl.pallas_call(
        body, name=name,
        out_shape=(pltpu.SemaphoreType.DMA((nsem,)), pltpu.SemaphoreType.DMA((nsem,)), *hbm, _sds((8, 128))),
        in_specs=[_HBM] * n + [pl.BlockSpec(memory_space=pl.ANY)] * na,
        out_specs=(_SEM, _SEM, *([_HBM] * n), pl.BlockSpec(memory_space=pltpu.VMEM)),
        input_output_aliases={i: 2 + i for i in range(n)},
        compiler_params=pltpu.CompilerParams(has_side_effects=_EFFECT),
    )(*[pltpu.with_memory_space_constraint(a, pltpu.HBM) for a in arrays], *after)
    return outs[0], outs[1], list(outs[2:2 + n]), outs[-1]


def _split_wait(name, send_sems, recv_sems, arrays, plan, after):
    n = len(arrays)

    def body(*refs):
        for cp in plan(refs[:n], refs[n], refs[n + 1]):
            cp.wait_send()
            cp.wait_recv()

    hbm = [pltpu.HBM(a.shape, a.dtype) for a in arrays]
    return list(pl.pallas_call(
        body, name=name, out_shape=tuple(hbm),
        in_specs=[_HBM] * n + [_SEM, _SEM] + [pl.BlockSpec(memory_space=pl.ANY)] * len(after),
        out_specs=tuple([_HBM] * n), input_output_aliases={i: i for i in range(n)},
        compiler_params=pltpu.CompilerParams(has_side_effects=_EFFECT),
    )(*arrays, send_sems, recv_sems, *after))


def _mm(name, a, b, out_sds, *, grid, a_spec, b_spec, o_spec, dims, acc_shape, into=None):
    nk = grid[2]

    def body(*refs):
        a_ref, b_ref = refs[0], refs[1]
        o_ref, acc = refs[-2], refs[-1]
        k = pl.program_id(2)
        part = lax.dot_general(a_ref[...].astype(_MXU), b_ref[...].astype(_MXU), dims,
                               preferred_element_type=_F32)
        if nk == 1:
            o_ref[...] = part.astype(o_ref.dtype)
            return

        @pl.when(k == 0)
        def _():
            acc[...] = part

        @pl.when(k > 0)
        def _():
            acc[...] += part

        @pl.when(k == nk - 1)
        def _():
            o_ref[...] = acc[...].astype(o_ref.dtype).reshape(o_ref.shape)

    if nk == 1:
        acc_shape = (8, 128)
    in_specs = [a_spec, b_spec]
    args = [a, b]
    aliases = {}
    if into is not None:
        in_specs.append(pl.BlockSpec(memory_space=pl.ANY))
        args.append(into)
        aliases = {2: 0}
    return pl.pallas_call(
        body, name=name, grid=grid, in_specs=in_specs, out_specs=o_spec, out_shape=out_sds,
        scratch_shapes=[pltpu.VMEM(acc_shape, _F32)], input_output_aliases=aliases,
        compiler_params=_params(3))(*args)


_NN = (((1,), (0,)), ((), ()))
_NT = (((1,), (1,)), ((), ()))
_TN = (((0,), (0,)), ((), ()))


def _rowwise(name, body, *, grid, ins, outs, scratch=()):
    return pl.pallas_call(
        body, name=name, grid=(grid,), in_specs=[s for _, s in ins], out_specs=[s for _, s in outs],
        out_shape=[o for o, _ in outs], scratch_shapes=list(scratch),
        compiler_params=_params(1))(*[a for a, _ in ins])


def _rows(tb, w, cb=0, n=None):
    if n is None:
        return pl.BlockSpec((tb, w), lambda i: (i, cb))
    return pl.BlockSpec((tb, w), lambda i: (n - 1 - i, cb))


def _vec(shape):
    return pl.BlockSpec(shape, lambda i: (0,) * len(shape))


def _halo_prev(tb, w, cb=0, n=None, rows=8):
    if n is None:
        return pl.BlockSpec((rows, w), lambda i: (jnp.maximum(i * (tb // rows) - 1, 0), cb))
    return pl.BlockSpec((rows, w), lambda i: (jnp.maximum((n - 1 - i) * (tb // rows) - 1, 0), cb))


def _halo_next(tb, w, n, cb=0):
    return pl.BlockSpec((8, w), lambda i: (jnp.minimum((i + 1) * (tb // 8), n * (tb // 8) - 1), cb))


def _sds(shape, dtype=_F32):
    return jax.ShapeDtypeStruct(shape, dtype)


def _cast(name, a, tb):
    rows, cols = a.shape

    def body(a_ref, o_ref):
        o_ref[...] = a_ref[...].astype(o_ref.dtype)

    return _rowwise(name, body, grid=rows // tb, ins=[(a, _rows(tb, cols))],
                    outs=[(_sds((rows, cols), _MXU), _rows(tb, cols))])[0]


def _sum_lead(name, a, tb):
    g, rows, cols = a.shape

    def body(a_ref, o_ref):
        acc = a_ref[0]
        for k in range(1, g):
            acc = acc + a_ref[k]
        o_ref[...] = acc

    return _rowwise(name, body, grid=rows // tb,
                    ins=[(a, pl.BlockSpec((g, tb, cols), lambda i: (0, i, 0)))],
                    outs=[(_sds((rows, cols)), _rows(tb, cols))])[0]


def _sum_pair(name, mine, theirs, core, tb):
    nj, _, rows, cols = mine.shape

    def body(s_ref, a_ref, b_ref, o_ref, ob_ref):
        t = a_ref[...] + b_ref[...]
        o_ref[...] = t
        ob_ref[...] = t.astype(ob_ref.dtype)

    blk = pl.BlockSpec((None, tb, cols), lambda j, i, s: (j, i, 0))
    grid_spec = pltpu.PrefetchScalarGridSpec(
        num_scalar_prefetch=1, grid=(nj, rows // tb),
        in_specs=[pl.BlockSpec((None, None, tb, cols), lambda j, i, s: (j, s[0], i, 0)), blk],
        out_specs=[blk, blk])
    return pl.pallas_call(body, name=name, grid_spec=grid_spec,
                          out_shape=[_sds((nj, rows, cols)), _sds((nj, rows, cols), _MXU)],
                          compiler_params=_params(2))(core, mine, theirs)


def _sum_chips(name, mine, theirs, where, l, into, tb):
    _, rows, cols = mine.shape
    extra = [] if into is None else [into]

    def body(*refs):
        a_ref, b1_ref, b2_ref, b3_ref = refs[1:5]
        o_ref = refs[-1]
        o_ref[...] = ((a_ref[...] + b1_ref[...].astype(_F32)) + b2_ref[...].astype(_F32)) + b3_ref[...].astype(_F32)

    def slot(k):
        return pl.BlockSpec((None, tb, cols), lambda i, s: (jnp.bitwise_xor(s[0], k), i, 0))

    grid_spec = pltpu.PrefetchScalarGridSpec(
        num_scalar_prefetch=1, grid=(rows // tb,),
        in_specs=[slot(0), slot(1), slot(2), slot(3)] + [pl.BlockSpec(memory_space=pl.ANY)] * len(extra),
        out_specs=pl.BlockSpec((None, None, tb, cols), lambda i, s: (l, s[1], i, 0)))
    return pl.pallas_call(body, name=name, grid_spec=grid_spec, out_shape=_sds((2, 2, rows, cols)),
                          input_output_aliases={5: 0} if extra else {},
                          compiler_params=_params(1))(where, mine, theirs, theirs, theirs, *extra)


def _adamw(name, w, g, m, v, tb):
    rows, cols = w.shape
    c1 = 1.0 - ADAM_B1 ** ADAM_STEP
    c2 = 1.0 - ADAM_B2 ** ADAM_STEP

    def body(w_ref, g_ref, m_ref, v_ref, d_ref, nm_ref, nv_ref):
        gv = g_ref[...]
        nm = ADAM_B1 * m_ref[...] + (1.0 - ADAM_B1) * gv
        nv = ADAM_B2 * v_ref[...] + (1.0 - ADAM_B2) * (gv * gv)
        d_ref[...] = -ADAM_LR * ((nm / c1) / (jnp.sqrt(nv / c2) + ADAM_EPS) + ADAM_WD * w_ref[...])
        nm_ref[...] = nm
        nv_ref[...] = nv

    spec = _rows(tb, cols)
    return _rowwise(name, body, grid=rows // tb, ins=[(w, spec), (g, spec), (m, spec), (v, spec)],
                    outs=[(_sds((rows, cols)), spec)] * 3)


def _mod_fwd(c_all, w_mod, b_cols):
    cols = w_mod.shape[2]

    def body(c_ref, w_ref, b_ref, o_ref):
        cv = c_ref[...]
        sc = (cv * _sigmoid(cv)).astype(_MXU)
        o_ref[...] = jnp.dot(sc, w_ref[...].astype(_MXU), preferred_element_type=_F32) + b_ref[...]

    return pl.pallas_call(
        body, name="mod_fwd", grid=(2,),
        in_specs=[pl.BlockSpec((8, D), lambda l: (0, 0)), pl.BlockSpec((None, D, cols), lambda l: (l, 0, 0)),
                  pl.BlockSpec((None, 1, cols), lambda l: (l, 0, 0))],
        out_specs=pl.BlockSpec((None, 8, cols), lambda l: (l, 0, 0)),
        out_shape=_sds((2, 8, cols)), compiler_params=_params(1))(c_all, w_mod, b_cols)


def _mod_bwd(c_all_t, dm):
    cols = dm.shape[2]

    def body(c_ref, d_ref, o_ref):
        cv = c_ref[...]
        sc = (cv * _sigmoid(cv)).astype(_MXU)
        o_ref[...] = jnp.dot(sc, d_ref[...].astype(_MXU), preferred_element_type=_F32)

    return pl.pallas_call(
        body, name="mod_bwd", grid=(2,),
        in_specs=[pl.BlockSpec((D, 8), lambda l: (0, 0)), pl.BlockSpec((None, 8, cols), lambda l: (l, 0, 0))],
        out_specs=pl.BlockSpec((None, D, cols), lambda l: (l, 0, 0)),
        out_shape=_sds((2, D, cols)), compiler_params=_params(1))(c_all_t, dm)


def _proj(x, g_pre, shift, scale, w_in):
    s = x.shape[0]
    tm = 1024

    def body(x_ref, g_ref, sh_ref, sc_ref, w_ref, o_ref, ht_ref, h_s):
        @pl.when(pl.program_id(1) == 0)
        def _():
            xv = x_ref[...]
            rstd = lax.rsqrt(jnp.mean(xv * xv, axis=-1, keepdims=True) + NORM_EPS)
            hv = (xv * rstd) * g_ref[...] * (1.0 + sc_ref[...]) + sh_ref[...]
            h_s[...] = hv.astype(h_s.dtype)
            ht_ref[...] = hv.T.astype(ht_ref.dtype)

        o_ref[...] = jnp.dot(h_s[...], w_ref[...], preferred_element_type=_F32).astype(o_ref.dtype)

    vec = pl.BlockSpec((1, D), lambda m, n: (0, 0))
    return pl.pallas_call(
        body, name="proj", grid=(s // tm, N_CHIPS),
        in_specs=[pl.BlockSpec((tm, D), lambda m, n: (m, 0)), vec, vec, vec,
                  pl.BlockSpec((None, D, 2304), lambda m, n: (n, 0, 0))],
        out_specs=[pl.BlockSpec((tm, 2304), lambda m, n: (m, n)), pl.BlockSpec((D, tm), lambda m, n: (0, m))],
        out_shape=[_sds((s, IN_W), _MXU), _sds((D, s), _MXU)],
        scratch_shapes=[pltpu.VMEM((tm, D), _MXU)], compiler_params=_params(2))(x, g_pre, shift, scale, w_in)


def _shift_down(cur, halo, j, tb):
    ext = jnp.concatenate([halo, cur], axis=0)
    return pltpu.roll(ext, j, 0)[8:8 + tb]


def _shift_up(cur, halo, j, tb):
    ext = jnp.concatenate([cur, halo], axis=0)
    return pltpu.roll(ext, tb + 8 - j, 0)[0:tb]


def _conv(u_ref, halo_ref, w_ref, b_ref, first, tb):
    u = u_ref[...].astype(_F32)
    halo = jnp.where(first, 0.0, halo_ref[...].astype(_F32)[8:16])
    acc = b_ref[...] + u * w_ref[0:1, :]
    for j in range(1, 4):
        acc = acc + _shift_down(u, halo, j, tb) * w_ref[j:j + 1, :]
    return acc


def _lru_gates(pre_r, pre_i, uc, b_rg, b_ig, lam):
    r = _sigmoid(pre_r + b_rg)
    ig = _sigmoid(pre_i + b_ig)
    nl = -lam
    sp = jnp.maximum(nl, 0.0) + jnp.log(1.0 + jnp.exp(-jnp.abs(nl)))
    la = -LRU_C * r * sp
    a = jnp.exp(la)
    one_m_a2 = -jnp.tanh(la) * (a * a + 1.0)
    inv_sq = lax.rsqrt(jnp.maximum(one_m_a2, 1e-30))
    return r, ig, sp, a, one_m_a2 * inv_sq, inv_sq


GATE_TILES = 8


def _gate_tiles(w_rg, w_ig):
    eye = jnp.eye(2, dtype=w_rg.dtype)

    def tiles(w):
        return jnp.einsum("cpij,pq->cpiqj", w.reshape(GATE_TILES, 2, 64, 64), eye).reshape(GATE_TILES, 128, 128)

    return jnp.concatenate([tiles(w_rg), tiles(w_ig)], axis=2)


def _gate_tile_grads(gw):
    keep = jnp.eye(2, dtype=jnp.bool_)[None, :, None, :, None]

    def blocks(t):
        t5 = t.reshape(GATE_TILES, 2, 64, 2, 64)
        return jnp.sum(jnp.where(keep, t5, 0.0), axis=3).reshape(16, 64, 64)

    return blocks(gw[:, :, 0:128]), blocks(gw[:, :, 128:256])


def _gate_preacts(ucv, wt_ref):
    ucb = ucv.astype(_MXU)
    ps = [jnp.dot(ucb[:, 128 * c:128 * (c + 1)], wt_ref[c], preferred_element_type=_F32) for c in range(GATE_TILES)]
    pre_r = jnp.concatenate([p[:, 0:128] for p in ps], axis=1)
    pre_i = jnp.concatenate([p[:, 128:256] for p in ps], axis=1)
    return pre_r, pre_i


def _scan_fwd(proj, conv_w, conv_b, wt, b_rg, b_ig, lam):
    s = proj.shape[0]
    tb = 256

    def body(u_ref, up_ref, cw_ref, cb_ref, wt_ref, brg_ref, big_ref, lam_ref, h_ref, carry, a_s, b_s):
        i = pl.program_id(0)

        @pl.when(i == 0)
        def _():
            carry[...] = jnp.zeros_like(carry)

        ucv = _conv(u_ref, up_ref, cw_ref, cb_ref, i == 0, tb)
        pre_r, pre_i = _gate_preacts(ucv, wt_ref)
        _, ig, _, a, sq, _ = _lru_gates(pre_r, pre_i, ucv, brg_ref[...], big_ref[...], lam_ref[...])
        av = a
        bv = sq * (ig * ucv)
        av = av.reshape(tb // 8, 8, D)
        bv = bv.reshape(tb // 8, 8, D)
        row8 = lax.broadcasted_iota(jnp.int32, (1, 8, 1), 1)
        for sh in (1, 2, 4):
            m = row8 >= sh
            b_sh = pltpu.roll(bv, sh, 1)
            a_sh = pltpu.roll(av, sh, 1)
            bv = jnp.where(m, av * b_sh + bv, bv)
            av = jnp.where(m, av * a_sh, av)
        a_s[...] = av.reshape(tb, D)
        b_s[...] = bv.reshape(tb, D)

        def tile(t, state):
            rows = pl.ds(pl.multiple_of(t * 8, 8), 8)
            hv = b_s[rows, :] + a_s[rows, :] * state
            b_s[rows, :] = hv
            return jnp.broadcast_to(hv[7:8, :], (8, D))

        carry[...] = lax.fori_loop(0, tb // 8, tile, jnp.broadcast_to(carry[7:8, :], (8, D)), unroll=4)
        h_ref[...] = b_s[...].astype(h_ref.dtype)

    v = _vec((1, D))
    return _rowwise("scan_fwd", body, grid=s // tb,
                    ins=[(proj, _rows(tb, D, CB_U)), (proj, _halo_prev(tb, D, CB_U, rows=16)),
                         (conv_w, _vec((4, D))), (conv_b, v), (wt, _vec((GATE_TILES, 128, 256))),
                         (b_rg, v), (b_ig, v), (lam, v)],
                    outs=[(_sds((s, D), _MXU), _rows(tb, D))],
                    scratch=[pltpu.VMEM((8, D), _F32), pltpu.VMEM((tb, D), _F32), pltpu.VMEM((tb, D), _F32)])[0]


def _weight_specs(l):
    return [pl.BlockSpec((N_CHIPS, None, ATT_W, 256), lambda i: (0, l, 0, 0)),
            pl.BlockSpec((N_CHIPS, None, 256, D), lambda i: (0, l, 0, 0)),
            pl.BlockSpec((N_CHIPS, None, 256, D), lambda i: (0, l, 0, 0))]


def _tail_fwd(l, o, h_lru, proj, x, gate, g_post, gw, target):
    s = x.shape[0]
    tb = 512

    def body(*refs):
        o_ref, h_ref, ga_ref, gl_ref, ma_ref, mb_ref, x_ref, gt_ref, gp_ref, wpa_ref, wpb_ref, wo_ref = refs[0:12]
        aa_ref, ba_ref, ya_ref, yb_ref, z_ref, out_ref = refs[-8:-2] if target is not None else refs[-7:-1]
        ga = ga_ref[...].astype(_F32)
        aa32 = o_ref[...] * (ga * _sigmoid(ga))
        aa = aa32.astype(_MXU)
        aa_ref[...] = aa32.T.astype(aa_ref.dtype)
        gl = gl_ref[...].astype(_F32)
        ba32 = h_ref[...].astype(_F32) * (gl * _sigmoid(gl))
        ba = ba32.astype(_MXU)
        ba_ref[...] = ba32.T.astype(ba_ref.dtype)
        ya = jnp.concatenate([jnp.dot(aa, wpa_ref[j], preferred_element_type=_F32) for j in range(N_CHIPS)], axis=1)
        ya_ref[...] = ya.astype(ya_ref.dtype)
        yb = jnp.dot(ba, wpb_ref[...].reshape(D, D), preferred_element_type=_F32)
        yb_ref[...] = yb.astype(yb_ref.dtype)
        z32 = _sigmoid(ma_ref[...].astype(_F32)) * ya + _sigmoid(mb_ref[...].astype(_F32)) * yb
        z = z32.astype(_MXU)
        z_ref[...] = z32.T.astype(z_ref.dtype)
        ov = jnp.dot(z, wo_ref[...].reshape(D, D), preferred_element_type=_F32)
        out_ref[...] = ov.astype(out_ref.dtype)
        rstd = lax.rsqrt(jnp.mean(ov * ov, axis=-1, keepdims=True) + NORM_EPS)
        xn =x_ref[...] + gt_ref[...] * ((ov * rstd) * gp_ref[...])
        if target is None:
            refs[-1][...] = xn
        else:
            dy_ref, acc_ref = refs[-2], refs[-1]
            err = xn - refs[12][...]
            dy_ref[...] = err * (1.0 / D)
            _zero_first(pl.program_id(0), acc_ref)
            acc_ref[...] += jnp.sum(err * err, axis=0, keepdims=True)

    v = _vec((1, D))
    r = _rows(tb, D)
    r5 = _rows(tb, ATT_W)
    weights = list(zip((gw["w_pa"], gw["w_pb"], gw["w_o"]), _weight_specs(l)))
    cols = pl.BlockSpec((D, tb), lambda i: (0, i))
    head_in = [] if target is None else [(target, r)]
    head_out = [] if target is None else [(_sds((1, D)), v)]
    return _rowwise("tail_fwd" if target is None else "tail_loss_fwd", body, grid=s // tb,
                    ins=[(o, r5), (h_lru, r), (proj, _rows(tb, ATT_W, CB_GATT)), (proj, _rows(tb, D, CB_GLRU)),
                         (proj, _rows(tb, D, CB_MA)), (proj, _rows(tb, D, CB_MB)), (x, r), (gate, v), (g_post, v)]
                    + weights + head_in,
                    outs=[(_sds((ATT_W, s), _MXU), pl.BlockSpec((ATT_W, tb), lambda i: (0, i))),
                          (_sds((D, s), _MXU), cols), (_sds((s, D), _MXU), r), (_sds((s, D), _MXU), r),
                          (_sds((D, s), _MXU), cols), (_sds((s, D), _MXU), r), (_sds((s, D)), r)]
                    + head_out)


def _zero_first(i, *refs):
    @pl.when(i == 0)
    def _():
        for ref in refs:
            ref[...] = jnp.zeros_like(ref)


def _tail_bwd(l, dx, out, y_a, y_b, proj, o, h_lru, gate, g_post, gw):
    s = dx.shape[0]
    tb = 256

    def body(dx_ref, out_ref, ya_ref, yb_ref, ma_ref, mb_ref, o_ref, ga_ref, h_ref, gl_ref, gt_ref, gp_ref,
             wpa_ref, wpb_ref, wo_ref,
             dout_ref, dya_ref, dyb_ref, rest_ref, do_ref, dh_ref, dgt_ref, dgp_ref):
        i = pl.program_id(0)
        ov = out_ref[...].astype(_F32)
        dxv = dx_ref[...]
        rstd = lax.rsqrt(jnp.mean(ov * ov, axis=-1, keepdims=True) + NORM_EPS)
        nv = ov * rstd
        s_dn = jnp.sum(dxv * nv, axis=0, keepdims=True)
        _zero_first(i, dgt_ref, dgp_ref)
        dgt_ref[...] += s_dn * gp_ref[...]
        dgp_ref[...] += s_dn * gt_ref[...]
        dn = dxv * (gt_ref[...] * gp_ref[...])
        d_out = (rstd * (dn - nv * jnp.mean(dn * nv, axis=-1, keepdims=True))).astype(_MXU)
        dout_ref[...] = d_out
        dz = lax.dot_general(d_out, wo_ref[...].reshape(D, D), _NT, preferred_element_type=_F32)
        ga = _sigmoid(ma_ref[...].astype(_F32))
        gb = _sigmoid(mb_ref[...].astype(_F32))
        dya = (dz * ga).astype(_MXU)
        dyb = (dz * gb).astype(_MXU)
        dya_ref[...] = dya
        dyb_ref[...] = dyb
        rest_ref[:, R_MA:R_MB] = (dz * ya_ref[...].astype(_F32) * ga * (1.0 - ga)).astype(rest_ref.dtype)
        rest_ref[:, R_MB:R_END] = (dz * yb_ref[...].astype(_F32) * gb * (1.0 - gb)).astype(rest_ref.dtype)
        daa = lax.dot_general(dya[:, 0:256], wpa_ref[0], _NT, preferred_element_type=_F32)
        for j in range(1, N_CHIPS):
            daa = daa + lax.dot_general(dya[:, j * 256:(j + 1) * 256], wpa_ref[j], _NT, preferred_element_type=_F32)
        dba = lax.dot_general(dyb, wpb_ref[...].reshape(D, D), _NT, preferred_element_type=_F32)
        gav = ga_ref[...].astype(_F32)
        sa = _sigmoid(gav)
        do_ref[...] = daa * (gav * sa)
        rest_ref[:, 0:R_U] = (daa * o_ref[...] * (sa * (1.0 + gav * (1.0 - sa)))).astype(rest_ref.dtype)
        gl = gl_ref[...].astype(_F32)
        sl = _sigmoid(gl)
        dh_ref[...] = dba * (gl * sl)
        rest_ref[:, R_GLRU:R_MA] = (dba * h_ref[...].astype(_F32)
                                    * (sl * (1.0 + gl * (1.0 - sl)))).astype(rest_ref.dtype)

    v = _vec((1, D))
    r5, r10 = _rows(tb, ATT_W), _rows(tb, D)
    return _rowwise("tail_bwd", body, grid=s // tb,
                    ins=[(dx, r10), (out, r10), (y_a, r10), (y_b, r10), (proj, _rows(tb, D, CB_MA)),
                         (proj, _rows(tb, D, CB_MB)), (o, r5), (proj, _rows(tb, ATT_W, CB_GATT)), (h_lru, r10),
                         (proj, _rows(tb, D, CB_GLRU)), (gate, v), (g_post, v)]
                    + list(zip((gw["w_pa"], gw["w_pb"], gw["w_o"]), _weight_specs(l))),
                    outs=[(_sds((s, D), _MXU), r10), (_sds((s, D), _MXU), r10), (_sds((s, D), _MXU), r10),
                          (_sds((s, R_END), _MXU), _rows(tb, R_END)),
                          (_sds((s, ATT_W)), r5), (_sds((s, D)), r10), (_sds((1, D)), v), (_sds((1, D)), v)])


def _scan_bwd(dh, proj, conv_w, conv_b, h_lru, wt, b_rg, b_ig, lam):
    s = dh.shape[0]
    tb = 256
    n = s // tb

    def body(dh_ref, u_ref, up_ref, cw_ref, cb_ref, h_ref, hp_ref, wt_ref, brg_ref, big_ref, lam_ref,
             duc_ref, dwt_ref, dbrg_ref, dbig_ref, dlam_ref, carry, c_s, g_s):
        i = pl.program_id(0)

        @pl.when(i == 0)
        def _():
            carry[...] = jnp.zeros_like(carry)
            for acc_ref in (dwt_ref, dbrg_ref, dbig_ref, dlam_ref):
                acc_ref[...] = jnp.zeros_like(acc_ref)

        ucv = _conv(u_ref, up_ref, cw_ref, cb_ref, i == n - 1, tb)
        pre_r, pre_i = _gate_preacts(ucv, wt_ref)
        r, ig, sp, a, sq, inv_sq =_lru_gates(pre_r, pre_i, ucv, brg_ref[...], big_ref[...], lam_ref[...])
        row = lax.broadcasted_iota(jnp.int32, (tb, 1), 0)
        cv = jnp.where(row == tb - 1, 1.0, pltpu.roll(a, tb - 1, 0))
        gv = dh_ref[...]
        cv = cv.reshape(tb // 8, 8, D)
        gv = gv.reshape(tb // 8, 8, D)
        row8 = lax.broadcasted_iota(jnp.int32, (1, 8, 1), 1)
        for sh in (1, 2, 4):
            m = row8 < 8 - sh
            g_sh = pltpu.roll(gv, 8 - sh, 1)
            c_sh = pltpu.roll(cv, 8 - sh, 1)
            gv = jnp.where(m, gv + cv * g_sh, gv)
            cv = jnp.where(m, cv * c_sh, cv)
        c_s[...] = cv.reshape(tb, D)
        g_s[...] = gv.reshape(tb, D)

        def tile(k, state):
            rows = pl.ds(pl.multiple_of((tb // 8 - 1 - k) * 8, 8), 8)
            gt = g_s[rows, :] + c_s[rows, :] * state
            g_s[rows, :] = gt
            return jnp.broadcast_to(gt[0:1, :], (8, D))

        lax.fori_loop(0, tb // 8, tile, jnp.broadcast_to(carry[0:1, :], (8, D)), unroll=4)
        gv = g_s[...]
        carry[...] = (a * gv)[0:8]

        halo = jnp.where(i < n - 1, hp_ref[...].astype(_F32)[8:16], 0.0)
        h_prev = _shift_down(h_ref[...].astype(_F32), halo, 1, tb)
        d_a = gv * h_prev
        d_sq = gv * (ig * ucv)
        d_i = gv * sq * ucv
        d_la = d_a * a - d_sq * (a * a) * inv_sq
        d_r = d_la * (-LRU_C * sp)
        d_pre_r = d_r * r * (1.0 - r)
        d_pre_i = d_i * ig * (1.0 - ig)
        ucb = ucv.astype(_MXU)
        dpr = d_pre_r.astype(_MXU)
        dpi = d_pre_i.astype(_MXU)
        back = []
        for c in range(GATE_TILES):
            lanes = slice(128 * c, 128 * (c + 1))
            dp = jnp.concatenate([dpr[:, lanes], dpi[:, lanes]], axis=1)
            back.append(lax.dot_general(dp, wt_ref[c], _NT, preferred_element_type=_F32))
            dwt_ref[c] += lax.dot_general(ucb[:, lanes], dp, _TN, preferred_element_type=_F32)
        duc_ref[...] = gv * sq * ig + jnp.concatenate(back, axis=1)
        dbrg_ref[...] += jnp.sum(d_pre_r, axis=0, keepdims=True)
        dbig_ref[...] += jnp.sum(d_pre_i, axis=0, keepdims=True)
        lamv = lam_ref[...]
        dlam_ref[...] += jnp.sum(d_la * (-LRU_C * r), axis=0, keepdims=True) * (-_sigmoid(-lamv))

    v = _vec((1, D))
    rv = _rows(tb, D, 0, n)
    return _rowwise("scan_bwd", body, grid=n,
                    ins=[(dh, rv), (proj, _rows(tb, D, CB_U, n)), (proj, _halo_prev(tb, D, CB_U, n, rows=16)),
                         (conv_w, _vec((4, D))), (conv_b, v), (h_lru, rv), (h_lru, _halo_prev(tb, D, 0, n, rows=16)),
                         (wt, _vec((GATE_TILES, 128, 256))), (b_rg, v), (b_ig, v), (lam, v)],
                    outs=[(_sds((s, D)), rv), (_sds((GATE_TILES, 128, 256)), _vec((GATE_TILES, 128, 256))),
                          (_sds((1, D)), v), (_sds((1, D)), v), (_sds((1, D)), v)],
                    scratch=[pltpu.VMEM((8, D), _F32), pltpu.VMEM((tb, D), _F32), pltpu.VMEM((tb, D), _F32)])


def _conv_bwd(duc_a, proj, conv_w, rest):
    s = duc_a.shape[0]
    tb = 512
    n = s // tb
    hw = D // 2

    def body(da_ref, dan_ref, u_ref, up_ref, w_ref, rest_in, du_ref, dw_ref, dbias_ref):
        i = pl.program_id(1)
        duc = da_ref[...]
        nxt = jnp.where(i < n - 1, dan_ref[...], 0.0)
        u = u_ref[...].astype(_F32)
        halo = jnp.where(i > 0, up_ref[...].astype(_F32)[8:16], 0.0)
        du = duc * w_ref[0:1, :]
        dws = [jnp.sum(duc * u, axis=0, keepdims=True)]
        for j in range(1, 4):
            du = du + _shift_up(duc, nxt, j, tb) * w_ref[j:j + 1, :]
            dws.append(jnp.sum(duc * _shift_down(u, halo, j, tb), axis=0, keepdims=True))
        du_ref[...] = du.astype(du_ref.dtype)
        _zero_first(i, dw_ref, dbias_ref)
        for j in range(4):
            dw_ref[j:j + 1, :] += dws[j]
        dbias_ref[...] += jnp.sum(duc, axis=0, keepdims=True)

    r = pl.BlockSpec((tb, hw), lambda h, i: (i, h))
    nxt_spec = pl.BlockSpec((8, hw), lambda h, i: (jnp.minimum((i + 1) * (tb // 8), n * (tb // 8) - 1), h))
    return pl.pallas_call(
        body, name="conv_bwd", grid=(2, n),
        in_specs=[r, nxt_spec,
                  pl.BlockSpec((tb, hw), lambda h, i: (i, 2 * CB_U + h)),
                  pl.BlockSpec((16, hw), lambda h, i: (jnp.maximum(i * (tb // 16) - 1, 0), 2 * CB_U + h)),
                  pl.BlockSpec((4, hw), lambda h, i: (0, h)), pl.BlockSpec(memory_space=pl.ANY)],
        out_specs=[pl.BlockSpec((tb, hw), lambda h, i: (i, R_U // hw + h)),
                   pl.BlockSpec((4, hw), lambda h, i: (0, h)), pl.BlockSpec((1, hw), lambda h, i: (0, h))],
        out_shape=[_sds(rest.shape, rest.dtype), _sds((4, D)), _sds((1, D))],
        input_output_aliases={5: 0}, compiler_params=_params(2),
    )(duc_a, duc_a, proj, proj, conv_w, rest)


def _band_tiles(dil):
    tiles = []
    for rho in range(dil):
        for b in range(16 // dil):
            qs = rho + dil * BAND * b
            tiles.append((qs, QBLK + qs - dil * BAND, b))
    return tiles


def _strided(start, size, dil):
    return pl.ds(start, size, stride=dil) if dil > 1 else pl.ds(start, size)


def _band_mask(i, b):
    qi = lax.broadcasted_iota(jnp.int32, (BAND, 2 * BAND), 0)
    ki = lax.broadcasted_iota(jnp.int32, (BAND, 2 * BAND), 1)
    valid = (ki >= qi) & (ki <= qi + BAND)
    if b == 0:
        valid = valid & ((ki >= BAND) | (i > 0))
    return valid


def _attn_fwd(proj):
    s = proj.shape[0]
    n = s // QBLK
    scale = HEAD ** -0.5

    def body(*refs):
        q_refs, kp_refs, kc_refs, vp_refs, vc_refs = (refs[3 * t:3 * t + 3] for t in range(5))
        o_ref, lse_ref, qbuf, kbuf, vbuf = refs[15:20]
        accs, maxs, dens = refs[20:23], refs[23:26], refs[26:29]
        i = pl.program_id(1)
        for g, dil in enumerate(DILATIONS):
            qbuf[...] = q_refs[g][...].astype(_F32)
            kbuf[0:QBLK, :] = kp_refs[g][...].astype(_F32)
            kbuf[QBLK:2 * QBLK, :] = kc_refs[g][...].astype(_F32)
            vbuf[0:QBLK, :] = vp_refs[g][...].astype(_F32)
            vbuf[QBLK:2 * QBLK, :] = vc_refs[g][...].astype(_F32)
            for qs, ks, b in _band_tiles(dil):
                qsl = _strided(qs, BAND, dil)
                q = qbuf[qsl, :].astype(_MXU)
                kk = kbuf[_strided(ks, 2 * BAND, dil), :].astype(_MXU)
                vv = vbuf[_strided(ks, 2 * BAND, dil), :].astype(_MXU)
                sc = lax.dot_general(q, kk, _NT, preferred_element_type=_F32) * scale
                sc = jnp.where(_band_mask(i, b), sc, NEG_INF)
                m = jnp.max(sc, axis=-1, keepdims=True)
                p = jnp.exp(sc - m)
                accs[g][qsl, :] = jnp.dot(p.astype(_MXU), vv, preferred_element_type=_F32)
                maxs[g][qsl, :] = jnp.broadcast_to(m, (BAND, HEAD))
                dens[g][qsl, :] = jnp.broadcast_to(jnp.sum(p, axis=-1, keepdims=True), (BAND, HEAD))
        ms = [r[...] for r in maxs]
        mx = jnp.maximum(jnp.maximum(ms[0], ms[1]), ms[2])
        ws = [jnp.exp(m - mx) for m in ms]
        den = ws[0] * dens[0][...] + ws[1] * dens[1][...] + ws[2] * dens[2][...]
        o_ref[...] = (ws[0] * accs[0][...] + ws[1] * accs[1][...] + ws[2] * accs[2][...]) / den
        lse_ref[...] = mx + jnp.log(den)

    blk = (QBLK, HEAD)

    def spec(first_col, lag):
        specs = []
        for g in range(3):
            col = first_col + g * HEADS
            if lag:
                specs.append(pl.BlockSpec(blk, lambda j, i, col=col: (jnp.maximum(i - 1, 0), col + j)))
            else:
                specs.append(pl.BlockSpec(blk, lambda j, i, col=col: (i, col + j)))
        return specs

    out_spec = pl.BlockSpec(blk, lambda j, i: (i, j))
    return pl.pallas_call(
        body, name="attn_fwd", grid=(HEADS, n),
        in_specs=spec(0, False) + spec(12, True) + spec(12, False) + spec(24, True) + spec(24, False),
        out_specs=[out_spec] * 2, out_shape=[_sds((s, ATT_W))] * 2,
        scratch_shapes=[pltpu.VMEM(blk, _F32)] + [pltpu.VMEM((2 * QBLK, HEAD), _F32)] * 2
        + [pltpu.VMEM(blk, _F32)] * 9,
        compiler_params=_params(2))(*([proj] * 15))


def _attn_bwd(proj, d_o, o, lse, g, into):
    s = proj.shape[0]
    dil = DILATIONS[g]
    n = s // QBLK
    scale = HEAD ** -0.5
    tiles = _band_tiles(dil)

    def body(*refs):
        q_ref, kp_ref, kc_ref, vp_ref, vc_ref, do_ref, o_ref, lse_ref = refs[0:8]
        dq_ref, dk_ref, dv_ref, kbuf, vbuf, dkbuf, dvbuf, dqbuf, qbuf = refs[-9:]
        i = pl.program_id(1)

        @pl.when(i == 0)
        def _():
            dkbuf[0:QBLK, :] = jnp.zeros((QBLK, HEAD), _F32)
            dvbuf[0:QBLK, :] = jnp.zeros((QBLK, HEAD), _F32)

        @pl.when(i < n)
        def _():
            qbuf[...] = q_ref[...].astype(_F32)
            kbuf[0:QBLK, :] = kp_ref[...].astype(_F32)
            kbuf[QBLK:2 * QBLK, :] = kc_ref[...].astype(_F32)
            vbuf[0:QBLK, :] = vp_ref[...].astype(_F32)
            vbuf[QBLK:2 * QBLK, :] = vc_ref[...].astype(_F32)
            dkbuf[QBLK:2 * QBLK, :] = jnp.zeros((QBLK, HEAD), _F32)
            dvbuf[QBLK:2 * QBLK, :] = jnp.zeros((QBLK, HEAD), _F32)
            for qs, ks, b in tiles:
                qsl = _strided(qs, BAND, dil)
                ksl = _strided(ks, 2 * BAND, dil)
                q = qbuf[qsl, :].astype(_MXU)
                kk = kbuf[ksl, :].astype(_MXU)
                vv = vbuf[ksl, :].astype(_MXU)
                dov = do_ref[qsl, :]
                dd = jnp.sum(dov * o_ref[qsl, :], axis=-1, keepdims=True)
                lse_t = lse_ref[qsl, :][:, 0:1]
                sc = lax.dot_general(q, kk, _NT, preferred_element_type=_F32) * scale
                p = jnp.where(_band_mask(i, b), jnp.exp(sc - lse_t), 0.0)
                dob = dov.astype(_MXU)
                dp = lax.dot_general(dob, vv, _NT, preferred_element_type=_F32)
                ds = (p * (dp - dd) * scale).astype(_MXU)
                dqbuf[qsl, :] = jnp.dot(ds, kk, preferred_element_type=_F32)
                dkbuf[ksl, :] += lax.dot_general(ds, q, _TN, preferred_element_type=_F32)
                dvbuf[ksl, :] += lax.dot_general(p.astype(_MXU), dob, _TN, preferred_element_type=_F32)
            dq_ref[...] = dqbuf[...].astype(dq_ref.dtype)

        dk_ref[...] = dkbuf[0:QBLK, :].astype(dk_ref.dtype)
        dv_ref[...] = dvbuf[0:QBLK, :].astype(dv_ref.dtype)
        dkbuf[0:QBLK, :] = dkbuf[QBLK:2 * QBLK, :]
        dvbuf[0:QBLK, :] = dvbuf[QBLK:2 * QBLK, :]

    blk = (QBLK, HEAD)
    cq, ck, cv = g * HEADS, 12 + g * HEADS, 24 + g * HEADS

    def cur(i):
        return jnp.minimum(i, n - 1)

    def prev(i):
        return jnp.maximum(jnp.minimum(i, n - 1) - 1, 0)

    own = pl.BlockSpec(blk, lambda j, i: (cur(i), j))
    own_out = pl.BlockSpec(blk, lambda j, i: (cur(i), cq + j))
    late_out = pl.BlockSpec(blk, lambda j, i: (jnp.maximum(i - 1, 0), cq + j))
    extra = [] if into is None else list(into)
    return pl.pallas_call(
        body, name="attn_bwd_d%d" % dil, grid=(HEADS, n + 1),
        in_specs=[pl.BlockSpec(blk, lambda j, i: (cur(i), cq + j)),
                  pl.BlockSpec(blk, lambda j, i: (prev(i), ck + j)),
                  pl.BlockSpec(blk, lambda j, i: (cur(i), ck + j)),
                  pl.BlockSpec(blk, lambda j, i: (prev(i), cv + j)),
                  pl.BlockSpec(blk, lambda j, i: (cur(i), cv + j)),
                  own, own, own] + [pl.BlockSpec(memory_space=pl.ANY)] * len(extra),
        out_specs=[own_out, late_out, late_out], out_shape=[_sds((s, QKV_W), _MXU)] * 3,
        input_output_aliases={8 + t: t for t in range(len(extra))},
        scratch_shapes=[pltpu.VMEM((2 * QBLK, HEAD), _F32)] * 4 + [pltpu.VMEM((QBLK, HEAD), _F32)] * 2,
        compiler_params=_params(2))(proj, proj, proj, proj, proj, d_o, o, lse, *extra)


_PARTS = ((0, 2), (2, 2), (4, 2), (6, 6))
_CHUNK = 768


def _d_x(parts, w_in, x, dx_out, g_pre, scale):
    s = parts[0].shape[0]
    nk = IN_W // _CHUNK

    def body(p0, p1, p2, p3, w_ref, x_ref, dxo_ref, g_ref, sc_ref, dx_ref, dsh_ref, dsc_ref, dg_ref, acc):
        m = pl.program_id(0)
        k = pl.program_id(2)

        @pl.when(k == 0)
        def _():
            acc[...] = jnp.zeros_like(acc)

        @pl.when((k == 0) & (m == 0))
        def _():
            for ref in (dsh_ref, dsc_ref, dg_ref):
                ref[...] = jnp.zeros_like(ref)

        for p_ref, (first, cnt) in zip((p0, p1, p2, p3), _PARTS):
            @pl.when((k >= first) & (k < first + cnt))
            def _(p_ref=p_ref):
                acc[...] += lax.dot_general(p_ref[...].astype(_MXU), w_ref[...], _NT, preferred_element_type=_F32)

        @pl.when(k == nk - 1)
        def _():
            dhv = acc[...]
            xv = x_ref[...]
            rstd = lax.rsqrt(jnp.mean(xv * xv, axis=-1, keepdims=True) + NORM_EPS)
            xn = xv * rstd
            one_sc = 1.0 + sc_ref[...]
            s1 = jnp.sum(dhv * xn, axis=0, keepdims=True)
            dsh_ref[...] += jnp.sum(dhv, axis=0, keepdims=True)
            dsc_ref[...] += s1 * g_ref[...]
            dg_ref[...] += s1 * one_sc
            dxn = dhv * (g_ref[...] * one_sc)
            dx_ref[...] = dxo_ref[...] + rstd * (dxn - xn * jnp.mean(dxn * xn, axis=-1, keepdims=True))

    def part_spec(first, cnt):
        return pl.BlockSpec((1024, _CHUNK), lambda m, n, k: (m, jnp.clip(k - first, 0, cnt - 1)))

    rows = pl.BlockSpec((1024, D), lambda m, n, k: (m, 0))
    vec = pl.BlockSpec((1, D), lambda m, n, k: (0, 0))
    return pl.pallas_call(
        body, name="d_x", grid=(s // 1024, 1, nk),
        in_specs=[part_spec(*p) for p in _PARTS]
        + [pl.BlockSpec((None, D, _CHUNK), lambda m, n, k: (k // 3, 0, k % 3)), rows, rows, vec, vec],
        out_specs=[rows, vec, vec, vec], out_shape=[_sds((s, D)), _sds((1, D)), _sds((1, D)), _sds((1, D))],
        scratch_shapes=[pltpu.VMEM((1024, D), _F32)], compiler_params=_params(3))(
            *parts, w_in, x, dx_out, g_pre, scale)


def _g_w_in(h_t, parts):
    s = h_t.shape[1]
    nk = s // 1024

    def body(*refs):
        h_ref, p_refs = refs[0], refs[1:5]
        o_ref, acc = refs[-2], refs[-1]
        n = pl.program_id(1)
        k = pl.program_id(2)

        @pl.when(k == 0)
        def _():
            acc[...] = jnp.zeros_like(acc)

        for p_ref, (first, cnt) in zip(p_refs, _PARTS):
            @pl.when((n >= first) & (n < first + cnt))
            def _(p_ref=p_ref):
                acc[...] += jnp.dot(h_ref[...], p_ref[...].astype(_MXU), preferred_element_type=_F32)

        @pl.when(k == nk - 1)
        def _():
            o_ref[...] = acc[...]

    def part_spec(first, cnt):
        def index(m, n, k):
            row = jnp.where(n < first, 0, jnp.where(n >= first + cnt, nk - 1, k))
            return (row, jnp.clip(n - first, 0, cnt - 1))
        return pl.BlockSpec((1024, _CHUNK), index)

    return pl.pallas_call(
        body, name="g_w_in", grid=(1, IN_W // _CHUNK, nk),
        in_specs=[pl.BlockSpec((D, 1024), lambda m, n, k: (0, k))] + [part_spec(*p) for p in _PARTS],
        out_specs=pl.BlockSpec((None, D, _CHUNK), lambda m, n, k: (n // 3, 0, n % 3)),
        out_shape=_sds((N_CHIPS, D, 2304)),
        scratch_shapes=[pltpu.VMEM((D, _CHUNK), _F32)], compiler_params=_params(3))(h_t, *parts)


def _layer_fwd(l, x, p, gw, late, target):
    proj, h_t = _proj(x, p["g_pre"], p["shift"], p["scale"], gw["w_in"][l])
    o, lse = _attn_fwd(proj)
    h_lru = _scan_fwd(proj, p["conv_w"], p["conv_b"], p["wt"], p["b_rg"], p["b_ig"], p["lam"])
    if late is not None:
        landed = dict(late(h_lru))
        gw["w_in"].append(landed.pop("w_in1"))
        gw.update(landed)
    a_att, b_act, y_a, y_b, z, out, *last = _tail_fwd(l, o, h_lru, proj, x, p["gate"], p["g_post"], gw, target)
    saved = dict(x=x, h_t=h_t, proj=proj, o=o, lse=lse, h_lru=h_lru, a_att=a_att, b_act=b_act,
                 y_a=y_a, y_b=y_b, z=z, out=out)
    return (last[0] if target is None else last), saved


def _layer_bwd(l, dx, p, gw, sv, hooks):
    s = dx.shape[0]
    nt = s // 2048
    proj = sv["proj"]
    gate, b_rg, g_pre = p["gate"], p["b_rg"], p["g_pre"]
    if hooks is not None:
        gate = gate + hooks[0]([dx])
    d_out, dy_a, dy_b, d_rest, d_o, dh_lru, d_gate, d_gpost = _tail_bwd(
        l, dx, sv["out"], sv["y_a"], sv["y_b"], proj, sv["o"], sv["h_lru"], gate, p["g_post"], gw)
    if hooks is not None:
        b_rg = b_rg + hooks[1]([d_out])

    def wgrad_rows(name, a, b):
        return _mm(name, a, b, _sds((N_CHIPS, 256, D)), grid=(1, 1, nt),
                   a_spec=pl.BlockSpec((D, 2048), lambda m, n, k: (0, k)),
                   b_spec=pl.BlockSpec((2048, D), lambda m, n, k: (k, 0)),
                   o_spec=pl.BlockSpec((N_CHIPS, 256, D), lambda m, n, k: (0, 0, 0)),
                   dims=_NN, acc_shape=(D, D))

    big = {}
    big["w_o"] = wgrad_rows("g_w_o", sv["z"], d_out)
    big["w_pa"] = _mm("g_w_pa", sv["a_att"], dy_a, _sds((N_CHIPS, ATT_W, 256)), grid=(1, 4, nt),
                      a_spec=pl.BlockSpec((ATT_W, 2048), lambda m, n, k: (0, k)),
                      b_spec=pl.BlockSpec((2048, 256), lambda m, n, k: (k, n)),
                      o_spec=pl.BlockSpec((None, ATT_W, 256), lambda m, n, k: (n, 0, 0)),
                      dims=_NN, acc_shape=(ATT_W, 256))
    big["w_pb"] = wgrad_rows("g_w_pb", sv["b_act"], dy_b)
    duc, g_wt, d_brg, d_big, d_lam = _scan_bwd(dh_lru, proj, p["conv_w"], p["conv_b"], sv["h_lru"], p["wt"], b_rg,
                                               p["b_ig"], p["lam"])
    g_wrg, g_wig = _gate_tile_grads(g_wt)
    d_rest, g_convw, g_convb = _conv_bwd(duc, proj, p["conv_w"], d_rest)
    dqkv = None
    for g in range(3):
        dqkv = _attn_bwd(proj, d_o, sv["o"], sv["lse"], g, dqkv)
    if hooks is not None:
        g_pre = g_pre + hooks[2]([dqkv[0]])
    parts = (dqkv[0], dqkv[1], dqkv[2], d_rest)
    big["w_in"] = _g_w_in(sv["h_t"], parts)
    if hooks is not None:
        g_pre = g_pre + hooks[3](big)
    dx_in, d_shift, d_scale, d_gpre = _d_x(parts, gw["w_in"][l], sv["x"], dx, g_pre, p["scale"])
    small = dict(dmod=jnp.concatenate([d_shift, d_scale, d_gate], axis=1), g_pre=d_gpre, conv_w=g_convw,
                 conv_b=g_convb, w_rg=g_wrg, b_rg=d_brg, w_ig=g_wig, b_ig=d_big, lam=d_lam, g_post=d_gpost)
    return dx_in, small, big


_BIG = ("w_in", "w_pa", "w_pb", "w_o")


class _GradReduce:
    PAIR_CHUNKS = (2, 1, 1, 1)
    CHIP_CHUNKS = (2, 1, 1, 1)
    FILL_CHUNKS = (4, 1, 1, 1)

    def __init__(self, core, where):
        self.core, self.where = core, where
        self.finals = None

    def begin(self, l, big):
        n = len(_BIG)
        halves = [big[k].reshape(N_CHIPS, 2, big[k].shape[1] // 2, big[k].shape[2]) for k in _BIG]
        lands = [lax.empty((N_CHIPS,) + h.shape[2:], _F32) for h in halves]
        plan, nsem = _pair_plan(n, self.PAIR_CHUNKS)
        state = {}
        state["pair"] = _split_start("reduce_pair_start_%d" % l, halves + lands, plan, nsem, [])

        def started(after):
            return state["pair"][3][0, 0]

        def pair_done(after):
            send, recv, arrays, _ = state["pair"]
            arrays = _split_wait("reduce_pair_wait_%d" % l, send, recv, arrays, plan, after)
            sums = [_sum_pair("sum_pair_%s_%d" % (k, l), arrays[a], arrays[n + a], self.core, 128)
                    for a, k in enumerate(_BIG)]
            state["mine"] = [t[0] for t in sums]
            lands2 = [lax.empty(t[1].shape, _MXU) for t in sums]
            plan2, nsem2 = _chips_plan(n, self.CHIP_CHUNKS)
            state["plan2"] = plan2
            state["chips"] = _split_start("reduce_chips_start_%d" % l, [t[1] for t in sums] + lands2, plan2, nsem2, [])
            return state["chips"][3][0, 0]

        def chips_done(after):
            send, recv, arrays, _ = state["chips"]
            arrays = _split_wait("reduce_chips_wait_%d" % l, send, recv, arrays, state["plan2"], after)
            finals = [_sum_chips("sum_chips_%s_%d" % (k, l), state["mine"][a], arrays[n + a], self.where, l,
                                 None if self.finals is None else self.finals[a], 128)
                      for a, k in enumerate(_BIG)]
            plan3, nsem3 = _fill_plan(n, self.FILL_CHUNKS, l)
            state["plan3"] = plan3
            state["fill"] = _split_start("gather_halves_start_%d" % l, finals, plan3, nsem3, [])
            return state["fill"][3][0, 0]

        def finish(after):
            send, recv, arrays, _ = state["fill"]
            self.finals = _split_wait("gather_halves_wait_%d" % l, send, recv, arrays, state["plan3"], after)
            return self.finals

        self._finish = finish
        return [started, pair_done, chips_done]

    def finish(self, after):
        return self._finish(after)


def _local_step(x, target, small_p, w_in0, late, reducer, on_smalls):
    saved = []
    h = x
    gw = dict(w_in=[w_in0])
    h, sv = _layer_fwd(0, h, small_p[0], gw, late, None)
    saved.append(sv)
    (dy, sq), sv = _layer_fwd(1, h, small_p[1], gw, None, target)
    saved.append(sv)
    loss = 0.5 * jnp.sum(sq) / D
    smalls = [None, None]
    dx, smalls[1], big1 = _layer_bwd(1, dy, small_p[1], gw, saved[1], None)
    hooks1 = reducer.begin(1, big1)
    small_started = on_smalls(1, smalls[1])
    pair_started = hooks1[0]
    hooks1[0] = lambda after: pair_started(after) + small_started
    own = {}

    def layer0_ready(big0):
        reducer.finish([big0["w_in"]])
        own["hooks"] = reducer.begin(0, big0)
        return own["hooks"][0]([])

    dx, smalls[0], _ = _layer_bwd(0, dx, small_p[0], gw, saved[0], hooks1 + [layer0_ready])
    own["hooks"][1]([dx])
    on_smalls(0, smalls[0])

    def finish_reduce(after):
        own["hooks"][2](after)
        return reducer.finish(after)

    return loss, dx, smalls, finish_reduce


_SMALL_ROWS = 8 + 8 + 8 + 64 + 64
_SMALL_VECS = ("g_pre", "conv_b", "b_rg", "b_ig", "lam", "g_post")


def _pack_small(small):
    pad = lambda rows: jnp.zeros((rows, D), _F32)
    return jnp.concatenate(
        [small["dmod"].reshape(3, D), pad(5)] + [small[k] for k in _SMALL_VECS] + [pad(2)]
        + [small["conv_w"], pad(4), small["w_rg"].reshape(64, D), small["w_ig"].reshape(64, D)], axis=0)


def kernel(x, c, w_mod, b_mod, g_pre, w_in, conv_w, conv_b, w_rg, b_rg, w_ig, b_ig, lru_lambda, w_pa, w_pb, w_o, g_post, loss_target, m_w_mod, m_b_mod, m_g_pre, m_w_in, m_conv_w, m_conv_b, m_w_rg, m_b_rg, m_w_ig, m_b_ig, m_lru_lambda, m_w_pa, m_w_pb, m_w_o, m_g_post, v_w_mod, v_b_mod, v_g_pre, v_w_in, v_conv_w, v_conv_b, v_w_rg, v_b_rg, v_w_ig, v_b_ig, v_lru_lambda, v_w_pa, v_w_pb, v_w_o, v_g_post):
    xi, yi, ci = lax.axis_index("x"), lax.axis_index("y"), lax.axis_index("c")
    chip = 2 * xi + yi
    dev = 4 * xi + 2 * yi + ci
    mcols = w_mod.shape[2]

    pack1 = jnp.concatenate([jnp.broadcast_to(c, (8, D)),
                             jnp.pad(conv_w.reshape(8, 256), ((0, 0), (0, D - 256)))], axis=0)
    g1 = _exchange("gather_cond", [pack1], "xyc", False)[0]
    c_all = g1[:, 0, :]
    conv_w_full = jnp.transpose(g1[0::2, 8:16, 0:256], (1, 0, 2)).reshape(2, 4, D)

    b_cols = lax.dynamic_slice(b_mod, (0, chip * mcols), (2, mcols)).reshape(2, 1, mcols)
    mod_loc = _mod_fwd(c_all, w_mod, b_cols)
    g2 = _exchange("gather_mod", [mod_loc.reshape(16, mcols)], "xyc", False)[0]
    mod_full = jnp.transpose(g2[0::2], (1, 0, 2)).reshape(2, 8, 3 * D)
    mod_me = lax.dynamic_index_in_dim(mod_full, dev, axis=1, keepdims=False)

    wb_in = _cast("cast_w_in", w_in.reshape(2 * D, 2304), 256).reshape(2, D, 2304)
    late_src = [wb_in[1], _cast("cast_w_pa", w_pa.reshape(2 * ATT_W, 256), 256).reshape(2, ATT_W, 256),
                _cast("cast_w_pb", w_pb.reshape(512, D), 256).reshape(2, 256, D),
                _cast("cast_w_o", w_o.reshape(512, D), 256).reshape(2, 256, D)]
    late_chunks = [4, 2, 2, 2]
    w_in0 = _gather_weights([wb_in[0].reshape(2, D // 2, 2304)], [2])[0].reshape(N_CHIPS, D, 2304)
    chip1 = jnp.reshape(chip, (1,)).astype(jnp.int32)
    lands = [_own_slot("own_slot_" + k, a, chip1, 256) for k, a in zip(("w_in", "w_pa", "w_pb", "w_o"), late_src)]
    late_plan, late_nsem = _gather_plan(len(late_src), late_chunks)
    send_sems, recv_sems, late_arrays, token = _split_start(
        "late_gather_start", late_src + lands, late_plan, late_nsem, [w_in0, mod_me])

    def late(after):
        got = _split_wait("late_gather_wait", send_sems, recv_sems, late_arrays, late_plan, [after])[len(late_src):]
        return dict(w_in1=got[0], w_pa=got[1], w_pb=got[2], w_o=got[3])

    small_p = []
    for l in range(2):
        gates = _gate_tiles(w_rg[l], w_ig[l]).astype(_MXU)
        small_p.append(dict(
            shift=mod_me[l:l + 1, 0:D], scale=mod_me[l:l + 1, D:2 * D], gate=mod_me[l:l + 1, 2 * D:3 * D],
            g_pre=g_pre[l:l + 1], conv_w=conv_w_full[l], conv_b=conv_b[l:l + 1], wt=gates,
            b_rg=b_rg[l:l + 1], b_ig=b_ig[l:l + 1], lam=lru_lambda[l:l + 1], g_post=g_post[l:l + 1]))

    small_p[0]["shift"] = small_p[0]["shift"] + token[0, 0]

    core = jnp.reshape(ci, (1,)).astype(jnp.int32)
    where = jnp.stack([chip, ci]).astype(jnp.int32)
    dev1 = jnp.reshape(dev, (1,)).astype(jnp.int32)
    small_plan, small_nsem = _all_plan()
    small_state = {}

    def on_smalls(l, small):
        pack = _pack_small(small)
        land = _own_slot("own_small_%d" % l, pack, dev1, _SMALL_ROWS, slots=8)
        small_state[l] = _split_start("gather_small_start_%d" % l, [pack, land], small_plan, small_nsem, [])
        return small_state[l][3][0, 0]

    def small_done(l, after):
        send, recv, arrays, _ = small_state[l]
        return _split_wait("gather_small_wait_%d" % l, send, recv, arrays, small_plan, after)[1]

    loss_loc, dx, _, finish_reduce = _local_step(x[0], loss_target[0], small_p, w_in0, late,
                                                 _GradReduce(core, where), on_smalls)
    loss = lax.psum(loss_loc, ("x", "y", "c"))
    grad_x = dx[None]
    reduced = finish_reduce([dx])
    g_big ={k: a.reshape(2, 2 * a.shape[2], a.shape[3]) for k, a in zip(_BIG, reduced)}

    g3 = [small_done(l, [dx]) for l in range(2)]
    tot = [_sum_lead("sum_small_%d" % l, g3[l], _SMALL_ROWS) for l in range(2)]
    dmod_all = jnp.stack([g3[l][:, 0:3, :].reshape(8, 3 * D) for l in range(2)], axis=0)
    dm_cols = lax.dynamic_slice(dmod_all, (0, 0, chip * mcols), (2, 8, mcols))
    g_w_mod = _mod_bwd(jnp.transpose(c_all), dm_cols)
    both = lambda first, rows: jnp.stack([tot[l][first:first + rows] for l in range(2)], axis=0)
    vec = both(8, 6)
    grads = dict(
        w_mod=g_w_mod, b_mod=both(0, 3).reshape(2, 3 * D), g_pre=vec[:, 0], w_in=g_big["w_in"],
        conv_w=lax.dynamic_slice(both(16, 4), (0, 0, chip * 256), (2, 4, 256)), conv_b=vec[:, 1],
        w_rg=both(24, 64).reshape(2, 16, 64, 64), b_rg=vec[:, 2], w_ig=both(88, 64).reshape(2, 16, 64, 64),
        b_ig=vec[:, 3], lru_lambda=vec[:, 4], w_pa=g_big["w_pa"], w_pb=g_big["w_pb"], w_o=g_big["w_o"],
        g_post=vec[:, 5])

    weights = dict(w_mod=w_mod, b_mod=b_mod, g_pre=g_pre, w_in=w_in, conv_w=conv_w, conv_b=conv_b, w_rg=w_rg,
                   b_rg=b_rg, w_ig=w_ig, b_ig=b_ig, lru_lambda=lru_lambda, w_pa=w_pa, w_pb=w_pb, w_o=w_o,
                   g_post=g_post)
    ms = dict(w_mod=m_w_mod, b_mod=m_b_mod, g_pre=m_g_pre, w_in=m_w_in, conv_w=m_conv_w, conv_b=m_conv_b,
              w_rg=m_w_rg, b_rg=m_b_rg, w_ig=m_w_ig, b_ig=m_b_ig, lru_lambda=m_lru_lambda, w_pa=m_w_pa,
              w_pb=m_w_pb, w_o=m_w_o, g_post=m_g_post)
    vs = dict(w_mod=v_w_mod, b_mod=v_b_mod, g_pre=v_g_pre, w_in=v_w_in, conv_w=v_conv_w, conv_b=v_conv_b,
              w_rg=v_w_rg, b_rg=v_b_rg, w_ig=v_w_ig, b_ig=v_b_ig, lru_lambda=v_lru_lambda, w_pa=v_w_pa,
              w_pb=v_w_pb, w_o=v_w_o, g_post=v_g_post)
    flat = dict(w_mod=(2 * D, mcols, 256), b_mod=(2, 3 * D, 2), g_pre=(2, D, 2), w_in=(2 * D, 2304, 256),
                conv_w=(8, 256, 8), conv_b=(2, D, 2), w_rg=(128, D, 128), b_rg=(2, D, 2), w_ig=(128, D, 128),
                b_ig=(2, D, 2), lru_lambda=(2, D, 2), w_pa=(2 * ATT_W, 256, 256), w_pb=(512, D, 256),
                w_o=(512, D, 256), g_post=(2, D, 2))
    order = ("w_mod", "b_mod", "g_pre", "w_in", "conv_w", "conv_b", "w_rg", "b_rg", "w_ig", "b_ig",
             "lru_lambda", "w_pa", "w_pb", "w_o", "g_post")
    deltas, new_m, new_v = [], [], []
    for k in order:
        rows, cols, tb = flat[k]
        shp = weights[k].shape
        d, nm_, nv_ = _adamw("adamw_" + k, weights[k].reshape(rows, cols), grads[k].reshape(rows, cols),
                             ms[k].reshape(rows, cols), vs[k].reshape(rows, cols), tb)
        deltas.append(d.reshape(shp))
        new_m.append(nm_.reshape(shp))
        new_v.append(nv_.reshape(shp))
    return (loss, grad_x, *[grads[k].reshape(weights[k].shape) for k in order], *deltas, *new_m, *new_v)
```

```python
import functools

import jax
import jax.numpy as jnp
from jax import lax
from jax.experimental import pallas as pl
from jax.experimental.pallas import tpu as pltpu

_F32 = jnp.float32
_MXU = jnp.bfloat16
_VMEM_LIMIT = 56 * 1024 * 1024
_MESH = pl.DeviceIdType.MESH

D = 1024
HEAD = 128
HEADS = 4
ATT_W = 512
QKV_W = 1536
IN_W = 9216
DILATIONS = (1, 4, 16)
BAND = 128
QBLK = BAND * 16
NORM_EPS = 1e-6
NEG_INF = -1e30
LRU_C = 8.0
N_CHIPS = 4
CB_GATT = 4608 // 512
CB_U, CB_GLRU, CB_MA, CB_MB = 5, 6, 7, 8
R_U, R_GLRU, R_MA, R_MB, R_END = 512, 1536, 2560, 3584, 4608

ADAM_LR, ADAM_B1, ADAM_B2, ADAM_EPS, ADAM_WD, ADAM_STEP = 0.001, 0.9, 0.999, 1e-08, 0.01, 10


def _params(ngrid):
    return pltpu.CompilerParams(dimension_semantics=("arbitrary",) * ngrid, vmem_limit_bytes=_VMEM_LIMIT)


def _sigmoid(v):
    return 0.5 * jnp.tanh(0.5 * v) + 0.5


_GROUPS = {
    "c": [(0, 0, 1)],
    "xy": [(1, 0, 0), (0, 1, 0), (1, 1, 0)],
    "xyc": [(0, 0, 1), (0, 1, 0), (0, 1, 1), (1, 0, 0), (1, 0, 1), (1, 1, 0), (1, 1, 1)],
}


def _rank(group, px, py, pc):
    if group == "c":
        return pc
    if group == "xy":
        return 2 * px + py
    return 4 * px + 2 * py + pc


def _flip(rel, x, y, c):
    dx, dy, dc = rel
    return (1 - x if dx else x, 1 - y if dy else y, 1 - c if dc else c)


def _pieces(ref, nchunk):
    step = ref.shape[0] // nchunk
    return [ref.at[pl.ds(q * step, step)] for q in range(nchunk)]


def _exchange(name, srcs, group, scatter, *, local=True, nchunks=None):
    rels = _GROUPS[group]
    gsize = len(rels) + 1
    n = len(srcs)
    nchunks = nchunks or [1] * n
    blks = [s.shape[1:] if scatter else s.shape for s in srcs]
    slotted = local or gsize > 2
    base = [sum(nchunks[:a]) for a in range(n)]
    tot = sum(nchunks)

    def body(*refs):
        src_refs, out_refs = refs[:n], refs[n:2 * n]
        send_sems, recv_sems, loc_sems = refs[2 * n:]
        x, y, c = lax.axis_index("x"), lax.axis_index("y"), lax.axis_index("c")
        me = _rank(group, x, y, c)
        copies = []
        for a in range(n):
            def part(r, a=a):
                return src_refs[a].at[r] if scatter else src_refs[a]
            dst = out_refs[a].at[me] if slotted else out_refs[a]
            if local:
                for q, (s_, d_) in enumerate(zip(_pieces(part(me), nchunks[a]), _pieces(dst, nchunks[a]))):
                    loc = pltpu.make_async_copy(s_, d_, loc_sems.at[base[a] + q])
                    loc.start()
                    copies.append(loc)
            for k, rel in enumerate(rels):
                peer = _flip(rel, x, y, c)
                for q, (s_, d_) in enumerate(zip(_pieces(part(_rank(group, *peer)), nchunks[a]),
                                                 _pieces(dst, nchunks[a]))):
                    cp = pltpu.make_async_remote_copy(
                        src_ref=s_, dst_ref=d_, send_sem=send_sems.at[(base[a] + q) * len(rels) + k],
                        recv_sem=recv_sems.at[(base[a] + q) * len(rels) + k],
                        device_id=peer, device_id_type=_MESH)
                    cp.start()
                    copies.append(cp)
        for cp in copies:
            cp.wait()

    any_spec = pl.BlockSpec(memory_space=pl.ANY)
    lead = (gsize,) if slotted else ()
    return pl.pallas_call(
        body, name=name,
        out_shape=[jax.ShapeDtypeStruct(lead + tuple(b), s.dtype) for b, s in zip(blks, srcs)],
        in_specs=[any_spec] * n, out_specs=[any_spec] * n,
        scratch_shapes=[pltpu.SemaphoreType.DMA((tot * len(rels),)), pltpu.SemaphoreType.DMA((tot * len(rels),)),
                        pltpu.SemaphoreType.DMA((tot,))],
    )(*srcs)


def _pair_fill(name, arrs, nchunks):
    n = len(arrs)
    base = [sum(nchunks[:a]) for a in range(n)]
    tot = sum(nchunks)

    def body(*refs):
        out_refs = refs[n:2 * n]
        send_sems, recv_sems = refs[2 * n:]
        x, y, c = lax.axis_index("x"), lax.axis_index("y"), lax.axis_index("c")
        copies = []
        for a in range(n):
            for q, blk in enumerate(_pieces(out_refs[a].at[c], nchunks[a])):
                cp = pltpu.make_async_remote_copy(
                    src_ref=blk, dst_ref=blk, send_sem=send_sems.at[base[a] + q], recv_sem=recv_sems.at[base[a] + q],
                    device_id=(x, y, 1 - c), device_id_type=_MESH)
                cp.start()
                copies.append(cp)
        for cp in copies:
            cp.wait()

    any_spec = pl.BlockSpec(memory_space=pl.ANY)
    return pl.pallas_call(
        body, name=name, out_shape=[jax.ShapeDtypeStruct(a.shape, a.dtype) for a in arrs],
        in_specs=[any_spec] * n, out_specs=[any_spec] * n, input_output_aliases={a: a for a in range(n)},
        scratch_shapes=[pltpu.SemaphoreType.DMA((tot,)), pltpu.SemaphoreType.DMA((tot,))],
    )(*arrs)


def _gather_weights(wb, nchunks):
    n = len(wb)
    rels = _GROUPS["xy"]
    base = [sum(nchunks[:a]) for a in range(n)]
    tot = sum(nchunks)

    def body(*refs):
        src_refs, out_refs = refs[:n], refs[n:2 * n]
        ici_send, ici_recv, d2d_send, d2d_recv, loc_sems = refs[2 * n:]
        x, y, c = lax.axis_index("x"), lax.axis_index("y"), lax.axis_index("c")
        me = 2 * x + y
        waits = []
        for a in range(n):
            for l in range(2):
                for q, (s_, d_) in enumerate(zip(_pieces(src_refs[a].at[l], nchunks[a]),
                                                 _pieces(out_refs[a].at[me, l], nchunks[a]))):
                    loc = pltpu.make_async_copy(s_, d_, loc_sems.at[(base[a] + q) * 2 + l])
                    loc.start()
                    waits.append(loc)
        first = []
        for a in range(n):
            for k, rel in enumerate(rels):
                px, py, _ = _flip(rel, x, y, c)
                for q, (s_, d_) in enumerate(zip(_pieces(src_refs[a].at[c], nchunks[a]),
                                                 _pieces(out_refs[a].at[me, c], nchunks[a]))):
                    sem = (base[a] + q) * 3 + k
                    cp = pltpu.make_async_remote_copy(src_ref=s_, dst_ref=d_, send_sem=ici_send.at[sem],
                                                      recv_sem=ici_recv.at[sem], device_id=(px, py, c),
                                                      device_id_type=_MESH)
                    cp.start()
                    first.append(cp)
        second = []
        for a in range(n):
            for k, rel in enumerate(rels):
                px, py, _ = _flip(rel, x, y, c)
                for q, blk in enumerate(_pieces(out_refs[a].at[2 * px + py, c], nchunks[a])):
                    sem = (base[a] + q) * 3 + k
                    landed = pltpu.make_async_remote_copy(src_ref=blk, dst_ref=blk, send_sem=ici_send.at[sem],
                                                          recv_sem=ici_recv.at[sem], device_id=(px, py, c),
                                                          device_id_type=_MESH)
                    landed.wait_recv()
                    cp = pltpu.make_async_remote_copy(src_ref=blk, dst_ref=blk, send_sem=d2d_send.at[sem],
                                                      recv_sem=d2d_recv.at[sem], device_id=(x, y, 1 - c),
                                                      device_id_type=_MESH)
                    cp.start()
                    second.append(cp)
        for cp in first:
            cp.wait_send()
        for cp in second:
            cp.wait_send()
        for a in range(n):
            for k, rel in enumerate(rels):
                px, py, _ = _flip(rel, x, y, c)
                for q, blk in enumerate(_pieces(out_refs[a].at[2 * px + py, 1 - c], nchunks[a])):
                    sem = (base[a] + q) * 3 + k
                    pltpu.make_async_remote_copy(src_ref=blk, dst_ref=blk, send_sem=d2d_send.at[sem],
                                                 recv_sem=d2d_recv.at[sem], device_id=(x, y, 1 - c),
                                                 device_id_type=_MESH).wait_recv()
        for cp in waits:
            cp.wait()

    any_spec = pl.BlockSpec(memory_space=pl.ANY)
    return pl.pallas_call(
        body, name="gather_weights",
        out_shape=[jax.ShapeDtypeStruct((N_CHIPS,) + a.shape, a.dtype) for a in wb],
        in_specs=[any_spec] * n, out_specs=[any_spec] * n,
        scratch_shapes=[pltpu.SemaphoreType.DMA((tot * 3,))] * 4 + [pltpu.SemaphoreType.DMA((tot * 2,))],
    )(*wb)


_HBM = pl.BlockSpec(memory_space=pltpu.HBM)
_SEM = pl.BlockSpec(memory_space=pltpu.SEMAPHORE)
_EFFECT = pltpu.SideEffectType.DATAFLOW_SIDE_EFFECTING


def _own_slot(name, src, chip, tb, slots=N_CHIPS):
    rows, cols = src.shape[-2:]
    lead = src.shape[:-2]
    flat = src.reshape((-1, cols))

    def body(s_ref, a_ref, o_ref):
        o_ref[...] = a_ref[...]

    grid_spec = pltpu.PrefetchScalarGridSpec(
        num_scalar_prefetch=1, grid=(flat.shape[0] // tb,),
        in_specs=[pl.BlockSpec((tb, cols), lambda i, s: (i, 0))],
        out_specs=pl.BlockSpec((None, tb, cols), lambda i, s: (s[0], i, 0)))
    out = pl.pallas_call(body, name=name, grid_spec=grid_spec,
                         out_shape=jax.ShapeDtypeStruct((slots,) + flat.shape, src.dtype),
                         compiler_params=_params(1))(chip, flat)
    return out.reshape((slots,) + lead + (rows, cols))


def _numbered(pairs, peer, send_sems, recv_sems, first):
    return [pltpu.make_async_remote_copy(src_ref=s_, dst_ref=d_, send_sem=send_sems.at[first + q],
                                         recv_sem=recv_sems.at[first + q], device_id=peer, device_id_type=_MESH)
            for q, (s_, d_) in enumerate(pairs)]


def _gather_plan(n, nchunks):
    def plan(refs, send_sems, recv_sems):
        x, y, c = lax.axis_index("x"), lax.axis_index("y"), lax.axis_index("c")
        me = 2 * x + y
        copies = []
        for a in range(n):
            for rel in _GROUPS["xy"]:
                px, py, _ = _flip(rel, x, y, c)
                pairs = list(zip(_pieces(refs[a], nchunks[a]), _pieces(refs[n + a].at[me], nchunks[a])))
                copies += _numbered(pairs, (px, py, c), send_sems, recv_sems, len(copies))
        return copies
    return plan, 3 * sum(nchunks)


def _all_plan():
    def plan(refs, send_sems, recv_sems):
        x, y, c = lax.axis_index("x"), lax.axis_index("y"), lax.axis_index("c")
        me = 4 * x + 2 * y + c
        copies = []
        for rel in _GROUPS["xyc"]:
            copies += _numbered([(refs[0], refs[1].at[me])], _flip(rel, x, y, c), send_sems, recv_sems, len(copies))
        return copies
    return plan, len(_GROUPS["xyc"])


def _pair_plan(n, nchunks):
    def plan(refs, send_sems, recv_sems):
        x, y, c = lax.axis_index("x"), lax.axis_index("y"), lax.axis_index("c")
        copies = []
        for a in range(n):
            for j in range(N_CHIPS):
                pairs = list(zip(_pieces(refs[a].at[j, 1 - c], nchunks[a]), _pieces(refs[n + a].at[j], nchunks[a])))
                copies += _numbered(pairs, (x, y, 1 - c), send_sems, recv_sems, len(copies))
        return copies
    return plan, N_CHIPS * sum(nchunks)


def _chips_plan(n, nchunks):
    def plan(refs, send_sems, recv_sems):
        x, y, c = lax.axis_index("x"), lax.axis_index("y"), lax.axis_index("c")
        me = 2 * x + y
        copies = []
        for a in range(n):
            for rel in _GROUPS["xy"]:
                px, py, _ = _flip(rel, x, y, c)
                pairs = list(zip(_pieces(refs[a].at[2 * px + py], nchunks[a]), _pieces(refs[n + a].at[me], nchunks[a])))
                copies += _numbered(pairs, (px, py, c), send_sems, recv_sems, len(copies))
        return copies
    return plan, 3 * sum(nchunks)


def _fill_plan(n, nchunks, l):
    def plan(refs, send_sems, recv_sems):
        x, y, c = lax.axis_index("x"), lax.axis_index("y"), lax.axis_index("c")
        copies = []
        for a in range(n):
            blk = _pieces(refs[a].at[l, c], nchunks[a])
            copies += _numbered(list(zip(blk, blk)), (x, y, 1 - c), send_sems, recv_sems, len(copies))
        return copies
    return plan, sum(nchunks)


def _split_start(name, arrays, plan, nsem, after):
    n = len(arrays)
    na = len(after)

    def body(*refs):
        send_sems, recv_sems = refs[n + na], refs[n + na + 1]
        token = refs[-1]
        for cp in plan(refs[:n], send_sems, recv_sems):
            cp.start()
        token[...] = jnp.zeros_like(token)

    hbm = [pltpu.HBM(a.shape, a.dtype) for a in arrays]
    outs = pl.pallas_call(
        body, name=name,
        out_shape=(pltpu.SemaphoreType.DMA((nsem,)), pltpu.SemaphoreType.DMA((nsem,)), *hbm, _sds((8, 128))),
        in_specs=[_HBM] * n + [pl.BlockSpec(memory_space=pl.ANY)] * na,
        out_specs=(_SEM, _SEM, *([_HBM] * n), pl.BlockSpec(memory_space=pltpu.VMEM)),
        input_output_aliases={i: 2 + i for i in range(n)},
        compiler_params=pltpu.CompilerParams(has_side_effects=_EFFECT),
    )(*[pltpu.with_memory_space_constraint(a, pltpu.HBM) for a in arrays], *after)
    return outs[0], outs[1], list(outs[2:2 + n]), outs[-1]


def _split_wait(name, send_sems, recv_sems, arrays, plan, after):
    n = len(arrays)

    def body(*refs):
        for cp in plan(refs[:n], refs[n], refs[n + 1]):
            cp.wait_send()
            cp.wait_recv()

    hbm = [pltpu.HBM(a.shape, a.dtype) for a in arrays]
    return list(pl.pallas_call(
        body, name=name, out_shape=tuple(hbm),
        in_specs=[_HBM] * n + [_SEM, _SEM] + [pl.BlockSpec(memory_space=pl.ANY)] * len(after),
        out_specs=tuple([_HBM] * n), input_output_aliases={i: i for i in range(n)},
        compiler_params=pltpu.CompilerParams(has_side_effects=_EFFECT),
    )(*arrays, send_sems, recv_sems, *after))


def _mm(name, a, b, out_sds, *, grid, a_spec, b_spec, o_spec, dims, acc_shape, into=None):
    nk = grid[2]

    def body(*refs):
        a_ref, b_ref = refs[0], refs[1]
        o_ref, acc = refs[-2], refs[-1]
        k = pl.program_id(2)
        part = lax.dot_general(a_ref[...].astype(_MXU), b_ref[...].astype(_MXU), dims,
                               preferred_element_type=_F32)
        if nk == 1:
            o_ref[...] = part.astype(o_ref.dtype)
            return

        @pl.when(k == 0)
        def _():
            acc[...] = part

        @pl.when(k > 0)
        def _():
            acc[...] += part

        @pl.when(k == nk - 1)
        def _():
            o_ref[...] = acc[...].astype(o_ref.dtype).reshape(o_ref.shape)

    if nk == 1:
        acc_shape = (8, 128)
    in_specs = [a_spec, b_spec]
    args = [a, b]
    aliases = {}
    if into is not None:
        in_specs.append(pl.BlockSpec(memory_space=pl.ANY))
        args.append(into)
        aliases = {2: 0}
    return pl.pallas_call(
        body, name=name, grid=grid, in_specs=in_specs, out_specs=o_spec, out_shape=out_sds,
        scratch_shapes=[pltpu.VMEM(acc_shape, _F32)], input_output_aliases=aliases,
        compiler_params=_params(3))(*args)


_NN = (((1,), (0,)), ((), ()))
_NT = (((1,), (1,)), ((), ()))
_TN = (((0,), (0,)), ((), ()))


def _rowwise(name, body, *, grid, ins, outs, scratch=()):
    return pl.pallas_call(
        body, name=name, grid=(grid,), in_specs=[s for _, s in ins], out_specs=[s for _, s in outs],
        out_shape=[o for o, _ in outs], scratch_shapes=list(scratch),
        compiler_params=_params(1))(*[a for a, _ in ins])


def _rows(tb, w, cb=0, n=None):
    if n is None:
        return pl.BlockSpec((tb, w), lambda i: (i, cb))
    return pl.BlockSpec((tb, w), lambda i: (n - 1 - i, cb))


def _vec(shape):
    return pl.BlockSpec(shape, lambda i: (0,) * len(shape))


def _halo_prev(tb, w, cb=0, n=None, rows=8):
    if n is None:
        return pl.BlockSpec((rows, w), lambda i: (jnp.maximum(i * (tb // rows) - 1, 0), cb))
    return pl.BlockSpec((rows, w), lambda i: (jnp.maximum((n - 1 - i) * (tb // rows) - 1, 0), cb))


def _halo_next(tb, w, n, cb=0):
    return pl.BlockSpec((8, w), lambda i: (jnp.minimum((i + 1) * (tb // 8), n * (tb // 8) - 1), cb))


def _sds(shape, dtype=_F32):
    return jax.ShapeDtypeStruct(shape, dtype)


def _cast(name, a, tb):
    rows, cols = a.shape

    def body(a_ref, o_ref):
        o_ref[...] = a_ref[...].astype(o_ref.dtype)

    return _rowwise(name, body, grid=rows // tb, ins=[(a, _rows(tb, cols))],
                    outs=[(_sds((rows, cols), _MXU), _rows(tb, cols))])[0]


def _sum_lead(name, a, tb):
    g, rows, cols = a.shape

    def body(a_ref, o_ref):
        acc = a_ref[0]
        for k in range(1, g):
            acc = acc + a_ref[k]
        o_ref[...] = acc

    return _rowwise(name, body, grid=rows // tb,
                    ins=[(a, pl.BlockSpec((g, tb, cols), lambda i: (0, i, 0)))],
                    outs=[(_sds((rows, cols)), _rows(tb, cols))])[0]


def _sum_pair(name, mine, theirs, core, tb):
    nj, _, rows, cols = mine.shape

    def body(s_ref, a_ref, b_ref, o_ref, ob_ref):
        t = a_ref[...] + b_ref[...]
        o_ref[...] = t
        ob_ref[...] = t.astype(ob_ref.dtype)

    blk = pl.BlockSpec((None, tb, cols), lambda j, i, s: (j, i, 0))
    grid_spec = pltpu.PrefetchScalarGridSpec(
        num_scalar_prefetch=1, grid=(nj, rows // tb),
        in_specs=[pl.BlockSpec((None, None, tb, cols), lambda j, i, s: (j, s[0], i, 0)), blk],
        out_specs=[blk, blk])
    return pl.pallas_call(body, name=name, grid_spec=grid_spec,
                          out_shape=[_sds((nj, rows, cols)), _sds((nj, rows, cols), _MXU)],
                          compiler_params=_params(2))(core, mine, theirs)


def _sum_chips(name, mine, theirs, where, l, into, tb):
    _, rows, cols = mine.shape
    extra = [] if into is None else [into]

    def body(*refs):
        a_ref, b1_ref, b2_ref, b3_ref = refs[1:5]
        o_ref = refs[-1]
        o_ref[...] = ((a_ref[...] + b1_ref[...].astype(_F32)) + b2_ref[...].astype(_F32)) + b3_ref[...].astype(_F32)

    def slot(k):
        return pl.BlockSpec((None, tb, cols), lambda i, s: (jnp.bitwise_xor(s[0], k), i, 0))

    grid_spec = pltpu.PrefetchScalarGridSpec(
        num_scalar_prefetch=1, grid=(rows // tb,),
        in_specs=[slot(0), slot(1), slot(2), slot(3)] + [pl.BlockSpec(memory_space=pl.ANY)] * len(extra),
        out_specs=pl.BlockSpec((None, None, tb, cols), lambda i, s: (l, s[1], i, 0)))
    return pl.pallas_call(body, name=name, grid_spec=grid_spec, out_shape=_sds((2, 2, rows, cols)),
                          input_output_aliases={5: 0} if extra else {},
                          compiler_params=_params(1))(where, mine, theirs, theirs, theirs, *extra)


def _adamw(name, w, g, m, v, tb):
    rows, cols = w.shape
    c1 = 1.0 - ADAM_B1 ** ADAM_STEP
    c2 = 1.0 - ADAM_B2 ** ADAM_STEP

    def body(w_ref, g_ref, m_ref, v_ref, d_ref, nm_ref, nv_ref):
        gv = g_ref[...]
        nm = ADAM_B1 * m_ref[...] + (1.0 - ADAM_B1) * gv
        nv = ADAM_B2 * v_ref[...] + (1.0 - ADAM_B2) * (gv * gv)
        d_ref[...] = -ADAM_LR * ((nm / c1) / (jnp.sqrt(nv / c2) + ADAM_EPS) + ADAM_WD * w_ref[...])
        nm_ref[...] = nm
        nv_ref[...] = nv

    spec = _rows(tb, cols)
    return _rowwise(name, body, grid=rows // tb, ins=[(w, spec), (g, spec), (m, spec), (v, spec)],
                    outs=[(_sds((rows, cols)), spec)] * 3)


def _mod_fwd(c_all, w_mod, b_cols):
    cols = w_mod.shape[2]

    def body(c_ref, w_ref, b_ref, o_ref):
        cv = c_ref[...]
        sc = (cv * _sigmoid(cv)).astype(_MXU)
        o_ref[...] = jnp.dot(sc, w_ref[...].astype(_MXU), preferred_element_type=_F32) + b_ref[...]

    return pl.pallas_call(
        body, name="mod_fwd", grid=(2,),
        in_specs=[pl.BlockSpec((8, D), lambda l: (0, 0)), pl.BlockSpec((None, D, cols), lambda l: (l, 0, 0)),
                  pl.BlockSpec((None, 1, cols), lambda l: (l, 0, 0))],
        out_specs=pl.BlockSpec((None, 8, cols), lambda l: (l, 0, 0)),
        out_shape=_sds((2, 8, cols)), compiler_params=_params(1))(c_all, w_mod, b_cols)


def _mod_bwd(c_all_t, dm):
    cols = dm.shape[2]

    def body(c_ref, d_ref, o_ref):
        cv = c_ref[...]
        sc = (cv * _sigmoid(cv)).astype(_MXU)
        o_ref[...] = jnp.dot(sc, d_ref[...].astype(_MXU), preferred_element_type=_F32)

    return pl.pallas_call(
        body, name="mod_bwd", grid=(2,),
        in_specs=[pl.BlockSpec((D, 8), lambda l: (0, 0)), pl.BlockSpec((None, 8, cols), lambda l: (l, 0, 0))],
        out_specs=pl.BlockSpec((None, D, cols), lambda l: (l, 0, 0)),
        out_shape=_sds((2, D, cols)), compiler_params=_params(1))(c_all_t, dm)


def _proj(x, g_pre, shift, scale, w_in):
    s = x.shape[0]
    tm = 1024

    def body(x_ref, g_ref, sh_ref, sc_ref, w_ref, o_ref, ht_ref, h_s):
        @pl.when(pl.program_id(1) == 0)
        def _():
            xv = x_ref[...]
            rstd = lax.rsqrt(jnp.mean(xv * xv, axis=-1, keepdims=True) + NORM_EPS)
            hv = (xv * rstd) * g_ref[...] * (1.0 + sc_ref[...]) + sh_ref[...]
            h_s[...] = hv.astype(h_s.dtype)
            ht_ref[...] = hv.T.astype(ht_ref.dtype)

        o_ref[...] = jnp.dot(h_s[...], w_ref[...], preferred_element_type=_F32).astype(o_ref.dtype)

    vec = pl.BlockSpec((1, D), lambda m, n: (0, 0))
    return pl.pallas_call(
        body, name="proj", grid=(s // tm, N_CHIPS),
        in_specs=[pl.BlockSpec((tm, D), lambda m, n: (m, 0)), vec, vec, vec,
                  pl.BlockSpec((None, D, 2304), lambda m, n: (n, 0, 0))],
        out_specs=[pl.BlockSpec((tm, 2304), lambda m, n: (m, n)), pl.BlockSpec((D, tm), lambda m, n: (0, m))],
        out_shape=[_sds((s, IN_W), _MXU), _sds((D, s), _MXU)],
        scratch_shapes=[pltpu.VMEM((tm, D), _MXU)], compiler_params=_params(2))(x, g_pre, shift, scale, w_in)


def _shift_down(cur, halo, j, tb):
    ext = jnp.concatenate([halo, cur], axis=0)
    return pltpu.roll(ext, j, 0)[8:8 + tb]


def _shift_up(cur, halo, j, tb):
    ext = jnp.concatenate([cur, halo], axis=0)
    return pltpu.roll(ext, tb + 8 - j, 0)[0:tb]


def _conv(u_ref, halo_ref, w_ref, b_ref, first, tb):
    u = u_ref[...].astype(_F32)
    halo = jnp.where(first, 0.0, halo_ref[...].astype(_F32)[8:16])
    acc = b_ref[...] + u * w_ref[0:1, :]
    for j in range(1, 4):
        acc = acc + _shift_down(u, halo, j, tb) * w_ref[j:j + 1, :]
    return acc


def _lru_gates(pre_r, pre_i, uc, b_rg, b_ig, lam):
    r = _sigmoid(pre_r + b_rg)
    ig = _sigmoid(pre_i + b_ig)
    nl = -lam
    sp = jnp.maximum(nl, 0.0) + jnp.log(1.0 + jnp.exp(-jnp.abs(nl)))
    la = -LRU_C * r * sp
    a = jnp.exp(la)
    one_m_a2 = -jnp.tanh(la) * (a * a + 1.0)
    inv_sq = lax.rsqrt(jnp.maximum(one_m_a2, 1e-30))
    return r, ig, sp, a, one_m_a2 * inv_sq, inv_sq


GATE_TILES = 8


def _gate_tiles(w_rg, w_ig):
    eye = jnp.eye(2, dtype=w_rg.dtype)

    def tiles(w):
        return jnp.einsum("cpij,pq->cpiqj", w.reshape(GATE_TILES, 2, 64, 64), eye).reshape(GATE_TILES, 128, 128)

    return jnp.concatenate([tiles(w_rg), tiles(w_ig)], axis=2)


def _gate_tile_grads(gw):
    keep = jnp.eye(2, dtype=jnp.bool_)[None, :, None, :, None]

    def blocks(t):
        t5 = t.reshape(GATE_TILES, 2, 64, 2, 64)
        return jnp.sum(jnp.where(keep, t5, 0.0), axis=3).reshape(16, 64, 64)

    return blocks(gw[:, :, 0:128]), blocks(gw[:, :, 128:256])


def _gate_preacts(ucv, wt_ref):
    ucb = ucv.astype(_MXU)
    ps = [jnp.dot(ucb[:, 128 * c:128 * (c + 1)], wt_ref[c], preferred_element_type=_F32) for c in range(GATE_TILES)]
    pre_r = jnp.concatenate([p[:, 0:128] for p in ps], axis=1)
    pre_i = jnp.concatenate([p[:, 128:256] for p in ps], axis=1)
    return pre_r, pre_i


def _scan_fwd(proj, conv_w, conv_b, wt, b_rg, b_ig, lam):
    s = proj.shape[0]
    tb = 256

    def body(u_ref, up_ref, cw_ref, cb_ref, wt_ref, brg_ref, big_ref, lam_ref, h_ref, carry, a_s, b_s):
        i = pl.program_id(0)

        @pl.when(i == 0)
        def _():
            carry[...] = jnp.zeros_like(carry)

        ucv = _conv(u_ref, up_ref, cw_ref, cb_ref, i == 0, tb)
        pre_r, pre_i = _gate_preacts(ucv, wt_ref)
        _, ig, _, a, sq, _ = _lru_gates(pre_r, pre_i, ucv, brg_ref[...], big_ref[...], lam_ref[...])
        av = a
        bv = sq * (ig * ucv)
        av = av.reshape(tb // 8, 8, D)
        bv = bv.reshape(tb // 8, 8, D)
        row8 = lax.broadcasted_iota(jnp.int32, (1, 8, 1), 1)
        for sh in (1, 2, 4):
            m = row8 >= sh
            b_sh = pltpu.roll(bv, sh, 1)
            a_sh = pltpu.roll(av, sh, 1)
            bv = jnp.where(m, av * b_sh + bv, bv)
            av = jnp.where(m, av * a_sh, av)
        a_s[...] = av.reshape(tb, D)
        b_s[...] = bv.reshape(tb, D)

        def tile(t, state):
            rows = pl.ds(pl.multiple_of(t * 8, 8), 8)
            hv = b_s[rows, :] + a_s[rows, :] * state
            b_s[rows, :] = hv
            return jnp.broadcast_to(hv[7:8, :], (8, D))

        carry[...] = lax.fori_loop(0, tb // 8, tile, jnp.broadcast_to(carry[7:8, :], (8, D)), unroll=4)
        h_ref[...] = b_s[...].astype(h_ref.dtype)

    v = _vec((1, D))
    return _rowwise("scan_fwd", body, grid=s // tb,
                    ins=[(proj, _rows(tb, D, CB_U)), (proj, _halo_prev(tb, D, CB_U, rows=16)),
                         (conv_w, _vec((4, D))), (conv_b, v), (wt, _vec((GATE_TILES, 128, 256))),
                         (b_rg, v), (b_ig, v), (lam, v)],
                    outs=[(_sds((s, D), _MXU), _rows(tb, D))],
                    scratch=[pltpu.VMEM((8, D), _F32), pltpu.VMEM((tb, D), _F32), pltpu.VMEM((tb, D), _F32)])[0]


def _weight_specs(l):
    return [pl.BlockSpec((N_CHIPS, None, ATT_W, 256), lambda i: (0, l, 0, 0)),
            pl.BlockSpec((N_CHIPS, None, 256, D), lambda i: (0, l, 0, 0)),
            pl.BlockSpec((N_CHIPS, None, 256, D), lambda i: (0, l, 0, 0))]


def _tail_fwd(l, o, h_lru, proj, x, gate, g_post, gw, target):
    s = x.shape[0]
    tb = 512

    def body(*refs):
        o_ref, h_ref, ga_ref, gl_ref, ma_ref, mb_ref, x_ref, gt_ref, gp_ref, wpa_ref, wpb_ref, wo_ref = refs[0:12]
        aa_ref, ba_ref, ya_ref, yb_ref, z_ref, out_ref = refs[-8:-2] if target is not None else refs[-7:-1]
        ga = ga_ref[...].astype(_F32)
        aa32 = o_ref[...] * (ga * _sigmoid(ga))
        aa = aa32.astype(_MXU)
        aa_ref[...] = aa32.T.astype(aa_ref.dtype)
        gl = gl_ref[...].astype(_F32)
        ba32 = h_ref[...].astype(_F32) * (gl * _sigmoid(gl))
        ba = ba32.astype(_MXU)
        ba_ref[...] = ba32.T.astype(ba_ref.dtype)
        ya = jnp.concatenate([jnp.dot(aa, wpa_ref[j], preferred_element_type=_F32) for j in range(N_CHIPS)], axis=1)
        ya_ref[...] = ya.astype(ya_ref.dtype)
        yb = jnp.dot(ba, wpb_ref[...].reshape(D, D), preferred_element_type=_F32)
        yb_ref[...] = yb.astype(yb_ref.dtype)
        z32 = _sigmoid(ma_ref[...].astype(_F32)) * ya + _sigmoid(mb_ref[...].astype(_F32)) * yb
        z = z32.astype(_MXU)
        z_ref[...] = z32.T.astype(z_ref.dtype)
        ov = jnp.dot(z, wo_ref[...].reshape(D, D), preferred_element_type=_F32)
        out_ref[...] = ov.astype(out_ref.dtype)
        rstd = lax.rsqrt(jnp.mean(ov * ov, axis=-1, keepdims=True) + NORM_EPS)
        xn =x_ref[...] + gt_ref[...] * ((ov * rstd) * gp_ref[...])
        if target is None:
            refs[-1][...] = xn
        else:
            dy_ref, acc_ref = refs[-2], refs[-1]
            err = xn - refs[12][...]
            dy_ref[...] = err * (1.0 / D)
            _zero_first(pl.program_id(0), acc_ref)
            acc_ref[...] += jnp.sum(err * err, axis=0, keepdims=True)

    v = _vec((1, D))
    r = _rows(tb, D)
    r5 = _rows(tb, ATT_W)
    weights = list(zip((gw["w_pa"], gw["w_pb"], gw["w_o"]), _weight_specs(l)))
    cols = pl.BlockSpec((D, tb), lambda i: (0, i))
    head_in = [] if target is None else [(target, r)]
    head_out = [] if target is None else [(_sds((1, D)), v)]
    return _rowwise("tail_fwd" if target is None else "tail_loss_fwd", body, grid=s // tb,
                    ins=[(o, r5), (h_lru, r), (proj, _rows(tb, ATT_W, CB_GATT)), (proj, _rows(tb, D, CB_GLRU)),
                         (proj, _rows(tb, D, CB_MA)), (proj, _rows(tb, D, CB_MB)), (x, r), (gate, v), (g_post, v)]
                    + weights + head_in,
                    outs=[(_sds((ATT_W, s), _MXU), pl.BlockSpec((ATT_W, tb), lambda i: (0, i))),
                          (_sds((D, s), _MXU), cols), (_sds((s, D), _MXU), r), (_sds((s, D), _MXU), r),
                          (_sds((D, s), _MXU), cols), (_sds((s, D), _MXU), r), (_sds((s, D)), r)]
                    + head_out)


def _zero_first(i, *refs):
    @pl.when(i == 0)
    def _():
        for ref in refs:
            ref[...] = jnp.zeros_like(ref)


def _tail_bwd(l, dx, out, y_a, y_b, proj, o, h_lru, gate, g_post, gw):
    s = dx.shape[0]
    tb = 256

    def body(dx_ref, out_ref, ya_ref, yb_ref, ma_ref, mb_ref, o_ref, ga_ref, h_ref, gl_ref, gt_ref, gp_ref,
             wpa_ref, wpb_ref, wo_ref,
             dout_ref, dya_ref, dyb_ref, rest_ref, do_ref, dh_ref, dgt_ref, dgp_ref):
        i = pl.program_id(0)
        ov = out_ref[...].astype(_F32)
        dxv = dx_ref[...]
        rstd = lax.rsqrt(jnp.mean(ov * ov, axis=-1, keepdims=True) + NORM_EPS)
        nv = ov * rstd
        s_dn = jnp.sum(dxv * nv, axis=0, keepdims=True)
        _zero_first(i, dgt_ref, dgp_ref)
        dgt_ref[...] += s_dn * gp_ref[...]
        dgp_ref[...] += s_dn * gt_ref[...]
        dn = dxv * (gt_ref[...] * gp_ref[...])
        d_out = (rstd * (dn - nv * jnp.mean(dn * nv, axis=-1, keepdims=True))).astype(_MXU)
        dout_ref[...] = d_out
        dz = lax.dot_general(d_out, wo_ref[...].reshape(D, D), _NT, preferred_element_type=_F32)
        ga = _sigmoid(ma_ref[...].astype(_F32))
        gb = _sigmoid(mb_ref[...].astype(_F32))
        dya = (dz * ga).astype(_MXU)
        dyb = (dz * gb).astype(_MXU)
        dya_ref[...] = dya
        dyb_ref[...] = dyb
        rest_ref[:, R_MA:R_MB] = (dz * ya_ref[...].astype(_F32) * ga * (1.0 - ga)).astype(rest_ref.dtype)
        rest_ref[:, R_MB:R_END] = (dz * yb_ref[...].astype(_F32) * gb * (1.0 - gb)).astype(rest_ref.dtype)
        daa = lax.dot_general(dya[:, 0:256], wpa_ref[0], _NT, preferred_element_type=_F32)
        for j in range(1, N_CHIPS):
            daa = daa + lax.dot_general(dya[:, j * 256:(j + 1) * 256], wpa_ref[j], _NT, preferred_element_type=_F32)
        dba = lax.dot_general(dyb, wpb_ref[...].reshape(D, D), _NT, preferred_element_type=_F32)
        gav = ga_ref[...].astype(_F32)
        sa = _sigmoid(gav)
        do_ref[...] = daa * (gav * sa)
        rest_ref[:, 0:R_U] = (daa * o_ref[...] * (sa * (1.0 + gav * (1.0 - sa)))).astype(rest_ref.dtype)
        gl = gl_ref[...].astype(_F32)
        sl = _sigmoid(gl)
        dh_ref[...] = dba * (gl * sl)
        rest_ref[:, R_GLRU:R_MA] = (dba * h_ref[...].astype(_F32)
                                    * (sl * (1.0 + gl * (1.0 - sl)))).astype(rest_ref.dtype)

    v = _vec((1, D))
    r5, r10 = _rows(tb, ATT_W), _rows(tb, D)
    return _rowwise("tail_bwd", body, grid=s // tb,
                    ins=[(dx, r10), (out, r10), (y_a, r10), (y_b, r10), (proj, _rows(tb, D, CB_MA)),
                         (proj, _rows(tb, D, CB_MB)), (o, r5), (proj, _rows(tb, ATT_W, CB_GATT)), (h_lru, r10),
                         (proj, _rows(tb, D, CB_GLRU)), (gate, v), (g_post, v)]
                    + list(zip((gw["w_pa"], gw["w_pb"], gw["w_o"]), _weight_specs(l))),
                    outs=[(_sds((s, D), _MXU), r10), (_sds((s, D), _MXU), r10), (_sds((s, D), _MXU), r10),
                          (_sds((s, R_END), _MXU), _rows(tb, R_END)),
                          (_sds((s, ATT_W)), r5), (_sds((s, D)), r10), (_sds((1, D)), v), (_sds((1, D)), v)])


def _scan_bwd(dh, proj, conv_w, conv_b, h_lru, wt, b_rg, b_ig, lam):
    s = dh.shape[0]
    tb = 256
    n = s // tb

    def body(dh_ref, u_ref, up_ref, cw_ref, cb_ref, h_ref, hp_ref, wt_ref, brg_ref, big_ref, lam_ref,
             duc_ref, dwt_ref, dbrg_ref, dbig_ref, dlam_ref, carry, c_s, g_s):
        i = pl.program_id(0)

        @pl.when(i == 0)
        def _():
            carry[...] = jnp.zeros_like(carry)
            for acc_ref in (dwt_ref, dbrg_ref, dbig_ref, dlam_ref):
                acc_ref[...] = jnp.zeros_like(acc_ref)

        ucv = _conv(u_ref, up_ref, cw_ref, cb_ref, i == n - 1, tb)
        pre_r, pre_i = _gate_preacts(ucv, wt_ref)
        r, ig, sp, a, sq, inv_sq =_lru_gates(pre_r, pre_i, ucv, brg_ref[...], big_ref[...], lam_ref[...])
        row = lax.broadcasted_iota(jnp.int32, (tb, 1), 0)
        cv = jnp.where(row == tb - 1, 1.0, pltpu.roll(a, tb - 1, 0))
        gv = dh_ref[...]
        cv = cv.reshape(tb // 8, 8, D)
        gv = gv.reshape(tb // 8, 8, D)
        row8 = lax.broadcasted_iota(jnp.int32, (1, 8, 1), 1)
        for sh in (1, 2, 4):
            m = row8 < 8 - sh
            g_sh = pltpu.roll(gv, 8 - sh, 1)
            c_sh = pltpu.roll(cv, 8 - sh, 1)
            gv = jnp.where(m, gv + cv * g_sh, gv)
            cv = jnp.where(m, cv * c_sh, cv)
        c_s[...] = cv.reshape(tb, D)
        g_s[...] = gv.reshape(tb, D)

        def tile(k, state):
            rows = pl.ds(pl.multiple_of((tb // 8 - 1 - k) * 8, 8), 8)
            gt = g_s[rows, :] + c_s[rows, :] * state
            g_s[rows, :] = gt
            return jnp.broadcast_to(gt[0:1, :], (8, D))

        lax.fori_loop(0, tb // 8, tile, jnp.broadcast_to(carry[0:1, :], (8, D)), unroll=4)
        gv = g_s[...]
        carry[...] = (a * gv)[0:8]

        halo = jnp.where(i < n - 1, hp_ref[...].astype(_F32)[8:16], 0.0)
        h_prev = _shift_down(h_ref[...].astype(_F32), halo, 1, tb)
        d_a = gv * h_prev
        d_sq = gv * (ig * ucv)
        d_i = gv * sq * ucv
        d_la = d_a * a - d_sq * (a * a) * inv_sq
        d_r = d_la * (-LRU_C * sp)
        d_pre_r = d_r * r * (1.0 - r)
        d_pre_i = d_i * ig * (1.0 - ig)
        ucb = ucv.astype(_MXU)
        dpr = d_pre_r.astype(_MXU)
        dpi = d_pre_i.astype(_MXU)
        back = []
        for c in range(GATE_TILES):
            lanes = slice(128 * c, 128 * (c + 1))
            dp = jnp.concatenate([dpr[:, lanes], dpi[:, lanes]], axis=1)
            back.append(lax.dot_general(dp, wt_ref[c], _NT, preferred_element_type=_F32))
            dwt_ref[c] += lax.dot_general(ucb[:, lanes], dp, _TN, preferred_element_type=_F32)
        duc_ref[...] = gv * sq * ig + jnp.concatenate(back, axis=1)
        dbrg_ref[...] += jnp.sum(d_pre_r, axis=0, keepdims=True)
        dbig_ref[...] += jnp.sum(d_pre_i, axis=0, keepdims=True)
        lamv = lam_ref[...]
        dlam_ref[...] += jnp.sum(d_la * (-LRU_C * r), axis=0, keepdims=True) * (-_sigmoid(-lamv))

    v = _vec((1, D))
    rv = _rows(tb, D, 0, n)
    return _rowwise("scan_bwd", body, grid=n,
                    ins=[(dh, rv), (proj, _rows(tb, D, CB_U, n)), (proj, _halo_prev(tb, D, CB_U, n, rows=16)),
                         (conv_w, _vec((4, D))), (conv_b, v), (h_lru, rv), (h_lru, _halo_prev(tb, D, 0, n, rows=16)),
                         (wt, _vec((GATE_TILES, 128, 256))), (b_rg, v), (b_ig, v), (lam, v)],
                    outs=[(_sds((s, D)), rv), (_sds((GATE_TILES, 128, 256)), _vec((GATE_TILES, 128, 256))),
                          (_sds((1, D)), v), (_sds((1, D)), v), (_sds((1, D)), v)],
                    scratch=[pltpu.VMEM((8, D), _F32), pltpu.VMEM((tb, D), _F32), pltpu.VMEM((tb, D), _F32)])


def _conv_bwd(duc_a, proj, conv_w, rest):
    s = duc_a.shape[0]
    tb = 512
    n = s // tb
    hw = D // 2

    def body(da_ref, dan_ref, u_ref, up_ref, w_ref, rest_in, du_ref, dw_ref, dbias_ref):
        i = pl.program_id(1)
        duc = da_ref[...]
        nxt = jnp.where(i < n - 1, dan_ref[...], 0.0)
        u = u_ref[...].astype(_F32)
        halo = jnp.where(i > 0, up_ref[...].astype(_F32)[8:16], 0.0)
        du = duc * w_ref[0:1, :]
        dws = [jnp.sum(duc * u, axis=0, keepdims=True)]
        for j in range(1, 4):
            du = du + _shift_up(duc, nxt, j, tb) * w_ref[j:j + 1, :]
            dws.append(jnp.sum(duc * _shift_down(u, halo, j, tb), axis=0, keepdims=True))
        du_ref[...] = du.astype(du_ref.dtype)
        _zero_first(i, dw_ref, dbias_ref)
        for j in range(4):
            dw_ref[j:j + 1, :] += dws[j]
        dbias_ref[...] += jnp.sum(duc, axis=0, keepdims=True)

    r = pl.BlockSpec((tb, hw), lambda h, i: (i, h))
    nxt_spec = pl.BlockSpec((8, hw), lambda h, i: (jnp.minimum((i + 1) * (tb // 8), n * (tb // 8) - 1), h))
    return pl.pallas_call(
        body, name="conv_bwd", grid=(2, n),
        in_specs=[r, nxt_spec,
                  pl.BlockSpec((tb, hw), lambda h, i: (i, 2 * CB_U + h)),
                  pl.BlockSpec((16, hw), lambda h, i: (jnp.maximum(i * (tb // 16) - 1, 0), 2 * CB_U + h)),
                  pl.BlockSpec((4, hw), lambda h, i: (0, h)), pl.BlockSpec(memory_space=pl.ANY)],
        out_specs=[pl.BlockSpec((tb, hw), lambda h, i: (i, R_U // hw + h)),
                   pl.BlockSpec((4, hw), lambda h, i: (0, h)), pl.BlockSpec((1, hw), lambda h, i: (0, h))],
        out_shape=[_sds(rest.shape, rest.dtype), _sds((4, D)), _sds((1, D))],
        input_output_aliases={5: 0}, compiler_params=_params(2),
    )(duc_a, duc_a, proj, proj, conv_w, rest)


def _band_tiles(dil):
    tiles = []
    for rho in range(dil):
        for b in range(16 // dil):
            qs = rho + dil * BAND * b
            tiles.append((qs, QBLK + qs - dil * BAND, b))
    return tiles


def _strided(start, size, dil):
    return pl.ds(start, size, stride=dil) if dil > 1 else pl.ds(start, size)


def _band_mask(i, b):
    qi = lax.broadcasted_iota(jnp.int32, (BAND, 2 * BAND), 0)
    ki = lax.broadcasted_iota(jnp.int32, (BAND, 2 * BAND), 1)
    valid = (ki >= qi) & (ki <= qi + BAND)
    if b == 0:
        valid = valid & ((ki >= BAND) | (i > 0))
    return valid


def _attn_fwd(proj):
    s = proj.shape[0]
    n = s // QBLK
    scale = HEAD ** -0.5

    def body(*refs):
        q_refs, kp_refs, kc_refs, vp_refs, vc_refs = (refs[3 * t:3 * t + 3] for t in range(5))
        o_ref, lse_ref, qbuf, kbuf, vbuf = refs[15:20]
        accs, maxs, dens = refs[20:23], refs[23:26], refs[26:29]
        i = pl.program_id(1)
        for g, dil in enumerate(DILATIONS):
            qbuf[...] = q_refs[g][...].astype(_F32)
            kbuf[0:QBLK, :] = kp_refs[g][...].astype(_F32)
            kbuf[QBLK:2 * QBLK, :] = kc_refs[g][...].astype(_F32)
            vbuf[0:QBLK, :] = vp_refs[g][...].astype(_F32)
            vbuf[QBLK:2 * QBLK, :] = vc_refs[g][...].astype(_F32)
            for qs, ks, b in _band_tiles(dil):
                qsl = _strided(qs, BAND, dil)
                q = qbuf[qsl, :].astype(_MXU)
                kk = kbuf[_strided(ks, 2 * BAND, dil), :].astype(_MXU)
                vv = vbuf[_strided(ks, 2 * BAND, dil), :].astype(_MXU)
                sc = lax.dot_general(q, kk, _NT, preferred_element_type=_F32) * scale
                sc = jnp.where(_band_mask(i, b), sc, NEG_INF)
                m = jnp.max(sc, axis=-1, keepdims=True)
                p = jnp.exp(sc - m)
                accs[g][qsl, :] = jnp.dot(p.astype(_MXU), vv, preferred_element_type=_F32)
                maxs[g][qsl, :] = jnp.broadcast_to(m, (BAND, HEAD))
                dens[g][qsl, :] = jnp.broadcast_to(jnp.sum(p, axis=-1, keepdims=True), (BAND, HEAD))
        ms = [r[...] for r in maxs]
        mx = jnp.maximum(jnp.maximum(ms[0], ms[1]), ms[2])
        ws = [jnp.exp(m - mx) for m in ms]
        den = ws[0] * dens[0][...] + ws[1] * dens[1][...] + ws[2] * dens[2][...]
        o_ref[...] = (ws[0] * accs[0][...] + ws[1] * accs[1][...] + ws[2] * accs[2][...]) / den
        lse_ref[...] = mx + jnp.log(den)

    blk = (QBLK, HEAD)

    def spec(first_col, lag):
        specs = []
        for g in range(3):
            col = first_col + g * HEADS
            if lag:
                specs.append(pl.BlockSpec(blk, lambda j, i, col=col: (jnp.maximum(i - 1, 0), col + j)))
            else:
                specs.append(pl.BlockSpec(blk, lambda j, i, col=col: (i, col + j)))
        return specs

    out_spec = pl.BlockSpec(blk, lambda j, i: (i, j))
    return pl.pallas_call(
        body, name="attn_fwd", grid=(HEADS, n),
        in_specs=spec(0, False) + spec(12, True) + spec(12, False) + spec(24, True) + spec(24, False),
        out_specs=[out_spec] * 2, out_shape=[_sds((s, ATT_W))] * 2,
        scratch_shapes=[pltpu.VMEM(blk, _F32)] + [pltpu.VMEM((2 * QBLK, HEAD), _F32)] * 2
        + [pltpu.VMEM(blk, _F32)] * 9,
        compiler_params=_params(2))(*([proj] * 15))


def _attn_bwd(proj, d_o, o, lse, g, into):
    s = proj.shape[0]
    dil = DILATIONS[g]
    n = s // QBLK
    scale = HEAD ** -0.5
    tiles = _band_tiles(dil)

    def body(*refs):
        q_ref, kp_ref, kc_ref, vp_ref, vc_ref, do_ref, o_ref, lse_ref = refs[0:8]
        dq_ref, dk_ref, dv_ref, kbuf, vbuf, dkbuf, dvbuf, dqbuf, qbuf = refs[-9:]
        i = pl.program_id(1)

        @pl.when(i == 0)
        def _():
            dkbuf[0:QBLK, :] = jnp.zeros((QBLK, HEAD), _F32)
            dvbuf[0:QBLK, :] = jnp.zeros((QBLK, HEAD), _F32)

        @pl.when(i < n)
        def _():
            qbuf[...] = q_ref[...].astype(_F32)
            kbuf[0:QBLK, :] = kp_ref[...].astype(_F32)
            kbuf[QBLK:2 * QBLK, :] = kc_ref[...].astype(_F32)
            vbuf[0:QBLK, :] = vp_ref[...].astype(_F32)
            vbuf[QBLK:2 * QBLK, :] = vc_ref[...].astype(_F32)
            dkbuf[QBLK:2 * QBLK, :] = jnp.zeros((QBLK, HEAD), _F32)
            dvbuf[QBLK:2 * QBLK, :] = jnp.zeros((QBLK, HEAD), _F32)
            for qs, ks, b in tiles:
                qsl = _strided(qs, BAND, dil)
                ksl = _strided(ks, 2 * BAND, dil)
                q = qbuf[qsl, :].astype(_MXU)
                kk = kbuf[ksl, :].astype(_MXU)
                vv = vbuf[ksl, :].astype(_MXU)
                dov = do_ref[qsl, :]
                dd = jnp.sum(dov * o_ref[qsl, :], axis=-1, keepdims=True)
                lse_t = lse_ref[qsl, :][:, 0:1]
                sc = lax.dot_general(q, kk, _NT, preferred_element_type=_F32) * scale
                p = jnp.where(_band_mask(i, b), jnp.exp(sc - lse_t), 0.0)
                dob = dov.astype(_MXU)
                dp = lax.dot_general(dob, vv, _NT, preferred_element_type=_F32)
                ds = (p * (dp - dd) * scale).astype(_MXU)
                dqbuf[qsl, :] = jnp.dot(ds, kk, preferred_element_type=_F32)
                dkbuf[ksl, :] += lax.dot_general(ds, q, _TN, preferred_element_type=_F32)
                dvbuf[ksl, :] += lax.dot_general(p.astype(_MXU), dob, _TN, preferred_element_type=_F32)
            dq_ref[...] = dqbuf[...].astype(dq_ref.dtype)

        dk_ref[...] = dkbuf[0:QBLK, :].astype(dk_ref.dtype)
        dv_ref[...] = dvbuf[0:QBLK, :].astype(dv_ref.dtype)
        dkbuf[0:QBLK, :] = dkbuf[QBLK:2 * QBLK, :]
        dvbuf[0:QBLK, :] = dvbuf[QBLK:2 * QBLK, :]

    blk = (QBLK, HEAD)
    cq, ck, cv = g * HEADS, 12 + g * HEADS, 24 + g * HEADS

    def cur(i):
        return jnp.minimum(i, n - 1)

    def prev(i):
        return jnp.maximum(jnp.minimum(i, n - 1) - 1, 0)

    own = pl.BlockSpec(blk, lambda j, i: (cur(i), j))
    own_out = pl.BlockSpec(blk, lambda j, i: (cur(i), cq + j))
    late_out = pl.BlockSpec(blk, lambda j, i: (jnp.maximum(i - 1, 0), cq + j))
    extra = [] if into is None else list(into)
    return pl.pallas_call(
        body, name="attn_bwd_d%d" % dil, grid=(HEADS, n + 1),
        in_specs=[pl.BlockSpec(blk, lambda j, i: (cur(i), cq + j)),
                  pl.BlockSpec(blk, lambda j, i: (prev(i), ck + j)),
                  pl.BlockSpec(blk, lambda j, i: (cur(i), ck + j)),
                  pl.BlockSpec(blk, lambda j, i: (prev(i), cv + j)),
                  pl.BlockSpec(blk, lambda j, i: (cur(i), cv + j)),
                  own, own, own] + [pl.BlockSpec(memory_space=pl.ANY)] * len(extra),
        out_specs=[own_out, late_out, late_out], out_shape=[_sds((s, QKV_W), _MXU)] * 3,
        input_output_aliases={8 + t: t for t in range(len(extra))},
        scratch_shapes=[pltpu.VMEM((2 * QBLK, HEAD), _F32)] * 4 + [pltpu.VMEM((QBLK, HEAD), _F32)] * 2,
        compiler_params=_params(2))(proj, proj, proj, proj, proj, d_o, o, lse, *extra)


_PARTS = ((0, 2), (2, 2), (4, 2), (6, 6))
_CHUNK = 768


def _d_x(name, parts, w_in, x, dx_out, g_pre, scale, blocks, into):
    s = parts[0].shape[0]
    nk = IN_W // _CHUNK
    first_block, n_blocks = blocks

    def body(*refs):
        p0, p1, p2, p3, w_ref, x_ref, dxo_ref, g_ref, sc_ref = refs[0:9]
        dx_ref, dsh_ref, dsc_ref, dg_ref, acc = refs[-5:]
        m = pl.program_id(0)
        k = pl.program_id(2)

        @pl.when(k == 0)
        def _():
            acc[...] = jnp.zeros_like(acc)

        @pl.when((k == 0) & (m == 0))
        def _():
            for ref in (dsh_ref, dsc_ref, dg_ref):
                ref[...] = jnp.zeros_like(ref)

        for p_ref, (first, cnt) in zip((p0, p1, p2, p3), _PARTS):
            @pl.when((k >= first) & (k < first + cnt))
            def _(p_ref=p_ref):
                acc[...] += lax.dot_general(p_ref[...].astype(_MXU), w_ref[...], _NT, preferred_element_type=_F32)

        @pl.when(k == nk - 1)
        def _():
            dhv = acc[...]
            xv = x_ref[...]
            rstd = lax.rsqrt(jnp.mean(xv * xv, axis=-1, keepdims=True) + NORM_EPS)
            xn = xv * rstd
            one_sc = 1.0 + sc_ref[...]
            s1 = jnp.sum(dhv * xn, axis=0, keepdims=True)
            dsh_ref[...] += jnp.sum(dhv, axis=0, keepdims=True)
            dsc_ref[...] += s1 * g_ref[...]
            dg_ref[...] += s1 * one_sc
            dxn = dhv * (g_ref[...] * one_sc)
            dx_ref[...] = dxo_ref[...] + rstd * (dxn - xn * jnp.mean(dxn * xn, axis=-1, keepdims=True))

    def part_spec(first, cnt):
        return pl.BlockSpec((1024, _CHUNK), lambda m, n, k: (first_block + m, jnp.clip(k - first, 0, cnt - 1)))

    rows = pl.BlockSpec((1024, D), lambda m, n, k: (first_block + m, 0))
    vec = pl.BlockSpec((1, D), lambda m, n, k: (0, 0))
    extra = [] if into is None else [into]
    return pl.pallas_call(
        body, name=name, grid=(n_blocks, 1, nk),
        in_specs=[part_spec(*p) for p in _PARTS]
        + [pl.BlockSpec((None, D, _CHUNK), lambda m, n, k: (k // 3, 0, k % 3)), rows, rows, vec, vec]
        + [pl.BlockSpec(memory_space=pl.ANY)] * len(extra),
        out_specs=[rows, vec, vec, vec], out_shape=[_sds((s, D)), _sds((1, D)), _sds((1, D)), _sds((1, D))],
        input_output_aliases={9: 0} if extra else {},
        scratch_shapes=[pltpu.VMEM((1024, D), _F32)], compiler_params=_params(3))(
            *parts, w_in, x, dx_out, g_pre, scale, *extra)


def _g_w_in(h_t, parts):
    s = h_t.shape[1]
    nk = s // 1024

    def body(*refs):
        h_ref, p_refs = refs[0], refs[1:5]
        o_ref, acc = refs[-2], refs[-1]
        n = pl.program_id(1)
        k = pl.program_id(2)

        @pl.when(k == 0)
        def _():
            acc[...] = jnp.zeros_like(acc)

        for p_ref, (first, cnt) in zip(p_refs, _PARTS):
            @pl.when((n >= first) & (n < first + cnt))
            def _(p_ref=p_ref):
                acc[...] += jnp.dot(h_ref[...], p_ref[...].astype(_MXU), preferred_element_type=_F32)

        @pl.when(k == nk - 1)
        def _():
            o_ref[...] = acc[...]

    def part_spec(first, cnt):
        def index(m, n, k):
            row = jnp.where(n < first, 0, jnp.where(n >= first + cnt, nk - 1, k))
            return (row, jnp.clip(n - first, 0, cnt - 1))
        return pl.BlockSpec((1024, _CHUNK), index)

    return pl.pallas_call(
        body, name="g_w_in", grid=(1, IN_W // _CHUNK, nk),
        in_specs=[pl.BlockSpec((D, 1024), lambda m, n, k: (0, k))] + [part_spec(*p) for p in _PARTS],
        out_specs=pl.BlockSpec((None, D, _CHUNK), lambda m, n, k: (n // 3, 0, n % 3)),
        out_shape=_sds((N_CHIPS, D, 2304)),
        scratch_shapes=[pltpu.VMEM((D, _CHUNK), _F32)], compiler_params=_params(3))(h_t, *parts)


def _layer_fwd(l, x, p, gw, late, target):
    if callable(gw["w_in"][l]):
        gw["w_in"][l] = gw["w_in"][l](x)
    proj, h_t = _proj(x, p["g_pre"], p["shift"], p["scale"], gw["w_in"][l])
    o, lse = _attn_fwd(proj)
    h_lru = _scan_fwd(proj, p["conv_w"], p["conv_b"], p["wt"], p["b_rg"], p["b_ig"], p["lam"])
    if late is not None:
        landed = dict(late(h_lru))
        gw["w_in"].append(landed.pop("w_in1"))
        gw.update(landed)
    a_att, b_act, y_a, y_b, z, out, *last = _tail_fwd(l, o, h_lru, proj, x, p["gate"], p["g_post"], gw, target)
    saved = dict(x=x, h_t=h_t, proj=proj, o=o, lse=lse, h_lru=h_lru, a_att=a_att, b_act=b_act,
                 y_a=y_a, y_b=y_b, z=z, out=out)
    return (last[0] if target is None else last), saved


def _layer_bwd(l, dx, p, gw, sv, hooks):
    s = dx.shape[0]
    nt = s // 2048
    proj = sv["proj"]
    gate, b_rg, g_pre = p["gate"], p["b_rg"], p["g_pre"]
    if hooks is not None:
        gate = gate + hooks[0]([dx])
    d_out, dy_a, dy_b, d_rest, d_o, dh_lru, d_gate, d_gpost = _tail_bwd(
        l, dx, sv["out"], sv["y_a"], sv["y_b"], proj, sv["o"], sv["h_lru"], gate, p["g_post"], gw)
    if hooks is not None:
        b_rg = b_rg + hooks[1]([d_out])

    def wgrad_rows(name, a, b):
        return _mm(name, a, b, _sds((N_CHIPS, 256, D)), grid=(1, 1, nt),
                   a_spec=pl.BlockSpec((D, 2048), lambda m, n, k: (0, k)),
                   b_spec=pl.BlockSpec((2048, D), lambda m, n, k: (k, 0)),
                   o_spec=pl.BlockSpec((N_CHIPS, 256, D), lambda m, n, k: (0, 0, 0)),
                   dims=_NN, acc_shape=(D, D))

    big = {}
    big["w_o"] = wgrad_rows("g_w_o", sv["z"], d_out)
    big["w_pa"] = _mm("g_w_pa", sv["a_att"], dy_a, _sds((N_CHIPS, ATT_W, 256)), grid=(1, 4, nt),
                      a_spec=pl.BlockSpec((ATT_W, 2048), lambda m, n, k: (0, k)),
                      b_spec=pl.BlockSpec((2048, 256), lambda m, n, k: (k, n)),
                      o_spec=pl.BlockSpec((None, ATT_W, 256), lambda m, n, k: (n, 0, 0)),
                      dims=_NN, acc_shape=(ATT_W, 256))
    big["w_pb"] = wgrad_rows("g_w_pb", sv["b_act"], dy_b)
    duc, g_wt, d_brg, d_big, d_lam = _scan_bwd(dh_lru, proj, p["conv_w"], p["conv_b"], sv["h_lru"], p["wt"], b_rg,
                                               p["b_ig"], p["lam"])
    g_wrg, g_wig = _gate_tile_grads(g_wt)
    d_rest, g_convw, g_convb = _conv_bwd(duc, proj, p["conv_w"], d_rest)
    dqkv = None
    for g in range(3):
        dqkv = _attn_bwd(proj, d_o, sv["o"], sv["lse"], g, dqkv)
    if hooks is not None:
        g_pre = g_pre + hooks[2]([dqkv[0]])
    parts = (dqkv[0], dqkv[1], dqkv[2], d_rest)
    big["w_in"] = _g_w_in(sv["h_t"], parts)
    nb = s // 1024
    if hooks is None:
        dx_in, d_shift, d_scale, d_gpre = _d_x("d_x", parts, gw["w_in"][l], sv["x"], dx, g_pre, p["scale"],
                                               (0, nb), None)
    else:
        first = _d_x("d_x_a", parts, gw["w_in"][l], sv["x"], dx, g_pre + hooks[3](big), p["scale"],
                     (0, nb // 2), None)
        second = _d_x("d_x_b", parts, gw["w_in"][l], sv["x"], dx, g_pre + hooks[4]([first[0]]), p["scale"],
                      (nb // 2, nb - nb // 2), first[0])
        dx_in = second[0]
        d_shift, d_scale, d_gpre = (a + b for a, b in zip(first[1:], second[1:]))
    small = dict(dmod=jnp.concatenate([d_shift, d_scale, d_gate], axis=1), g_pre=d_gpre, conv_w=g_convw,
                 conv_b=g_convb, w_rg=g_wrg, b_rg=d_brg, w_ig=g_wig, b_ig=d_big, lam=d_lam, g_post=d_gpost)
    return dx_in, small, big


_BIG = ("w_in", "w_pa", "w_pb", "w_o")


class _GradReduce:
    PAIR_CHUNKS = (2, 1, 1, 1)
    CHIP_CHUNKS = (2, 1, 1, 1)
    FILL_CHUNKS = (4, 1, 1, 1)

    def __init__(self, core, where):
        self.core, self.where = core, where
        self.finals = None

    def begin(self, l, big):
        n = len(_BIG)
        halves = [big[k].reshape(N_CHIPS, 2, big[k].shape[1] // 2, big[k].shape[2]) for k in _BIG]
        lands = [lax.empty((N_CHIPS,) + h.shape[2:], _F32) for h in halves]
        plan, nsem = _pair_plan(n, self.PAIR_CHUNKS)
        state = {}
        state["pair"] = _split_start("reduce_pair_start_%d" % l, halves + lands, plan, nsem, [])

        def started(after):
            return state["pair"][3][0, 0]

        def pair_done(after):
            send, recv, arrays, _ = state["pair"]
            arrays = _split_wait("reduce_pair_wait_%d" % l, send, recv, arrays, plan, after)
            sums = [_sum_pair("sum_pair_%s_%d" % (k, l), arrays[a], arrays[n + a], self.core, 128)
                    for a, k in enumerate(_BIG)]
            state["mine"] = [t[0] for t in sums]
            lands2 = [lax.empty(t[1].shape, _MXU) for t in sums]
            plan2, nsem2 = _chips_plan(n, self.CHIP_CHUNKS)
            state["plan2"] = plan2
            state["chips"] = _split_start("reduce_chips_start_%d" % l, [t[1] for t in sums] + lands2, plan2, nsem2, [])
            return state["chips"][3][0, 0]

        def chips_done(after):
            send, recv, arrays, _ = state["chips"]
            arrays = _split_wait("reduce_chips_wait_%d" % l, send, recv, arrays, state["plan2"], after)
            finals = [_sum_chips("sum_chips_%s_%d" % (k, l), state["mine"][a], arrays[n + a], self.where, l,
                                 None if self.finals is None else self.finals[a], 128)
                      for a, k in enumerate(_BIG)]
            plan3, nsem3 = _fill_plan(n, self.FILL_CHUNKS, l)
            state["plan3"] = plan3
            state["fill"] = _split_start("gather_halves_start_%d" % l, finals, plan3, nsem3, [])
            return state["fill"][3][0, 0]

        def finish(after):
            send, recv, arrays, _ = state["fill"]
            self.finals = _split_wait("gather_halves_wait_%d" % l, send, recv, arrays, state["plan3"], after)
            return self.finals

        self._finish = finish
        return [started, pair_done, chips_done]

    def finish(self, after):
        return self._finish(after)


def _local_step(x, target, small_p, w_in0, late, reducer, on_smalls):
    saved = []
    h = x
    gw = dict(w_in=[w_in0])
    h, sv = _layer_fwd(0, h, small_p[0], gw, late, None)
    saved.append(sv)
    (dy, sq), sv = _layer_fwd(1, h, small_p[1], gw, None, target)
    saved.append(sv)
    loss = 0.5 * jnp.sum(sq) / D
    smalls = [None, None]
    dx, smalls[1], big1 = _layer_bwd(1, dy, small_p[1], gw, saved[1], None)
    hooks1 = reducer.begin(1, big1)
    small_started = on_smalls(1, smalls[1])
    pair_started = hooks1[0]
    hooks1[0] = lambda after: pair_started(after) + small_started
    own = {}

    def layer0_ready(big0):
        reducer.finish([big0["w_in"]])
        own["hooks"] = reducer.begin(0, big0)
        return own["hooks"][0]([])

    dx, smalls[0], _ = _layer_bwd(0, dx, small_p[0], gw, saved[0],
                                  hooks1 + [layer0_ready, lambda after: own["hooks"][1](after)])
    on_smalls(0, smalls[0])

    def finish_reduce(after):
        own["hooks"][2](after)
        return reducer.finish(after)

    return loss, dx, smalls, finish_reduce


_SMALL_ROWS = 8 + 8 + 8 + 64 + 64
_SMALL_VECS = ("g_pre", "conv_b", "b_rg", "b_ig", "lam", "g_post")


def _pack_small(small):
    pad = lambda rows: jnp.zeros((rows, D), _F32)
    return jnp.concatenate(
        [small["dmod"].reshape(3, D), pad(5)] + [small[k] for k in _SMALL_VECS] + [pad(2)]
        + [small["conv_w"], pad(4), small["w_rg"].reshape(64, D), small["w_ig"].reshape(64, D)], axis=0)


def kernel(x, c, w_mod, b_mod, g_pre, w_in, conv_w, conv_b, w_rg, b_rg, w_ig, b_ig, lru_lambda, w_pa, w_pb, w_o, g_post, loss_target, m_w_mod, m_b_mod, m_g_pre, m_w_in, m_conv_w, m_conv_b, m_w_rg, m_b_rg, m_w_ig, m_b_ig, m_lru_lambda, m_w_pa, m_w_pb, m_w_o, m_g_post, v_w_mod, v_b_mod, v_g_pre, v_w_in, v_conv_w, v_conv_b, v_w_rg, v_b_rg, v_w_ig, v_b_ig, v_lru_lambda, v_w_pa, v_w_pb, v_w_o, v_g_post):
    xi, yi, ci = lax.axis_index("x"), lax.axis_index("y"), lax.axis_index("c")
    chip = 2 * xi + yi
    dev = 4 * xi + 2 * yi + ci
    mcols = w_mod.shape[2]

    pack1 = jnp.concatenate([jnp.broadcast_to(c, (8, D)),
                             jnp.pad(conv_w.reshape(8, 256), ((0, 0), (0, D - 256)))], axis=0)
    g1 = _exchange("gather_cond", [pack1], "xyc", False)[0]
    c_all = g1[:, 0, :]
    conv_w_full = jnp.transpose(g1[0::2, 8:16, 0:256], (1, 0, 2)).reshape(2, 4, D)

    b_cols = lax.dynamic_slice(b_mod, (0, chip * mcols), (2, mcols)).reshape(2, 1, mcols)
    mod_loc = _mod_fwd(c_all, w_mod, b_cols)
    g2 = _exchange("gather_mod", [mod_loc.reshape(16, mcols)], "xyc", False)[0]
    mod_full = jnp.transpose(g2[0::2], (1, 0, 2)).reshape(2, 8, 3 * D)
    mod_me = lax.dynamic_index_in_dim(mod_full, dev, axis=1, keepdims=False)

    wb_in = _cast("cast_w_in", w_in.reshape(2 * D, 2304), 256).reshape(2, D, 2304)
    late_src = [wb_in[1], _cast("cast_w_pa", w_pa.reshape(2 * ATT_W, 256), 256).reshape(2, ATT_W, 256),
                _cast("cast_w_pb", w_pb.reshape(512, D), 256).reshape(2, 256, D),
                _cast("cast_w_o", w_o.reshape(512, D), 256).reshape(2, 256, D)]
    late_chunks = [4, 2, 2, 2]
    w_in0 = _gather_weights([wb_in[0].reshape(2, D // 2, 2304)], [2])[0].reshape(N_CHIPS, D, 2304)
    chip1 = jnp.reshape(chip, (1,)).astype(jnp.int32)
    lands = [_own_slot("own_slot_" + k, a, chip1, 256) for k, a in zip(("w_in", "w_pa", "w_pb", "w_o"), late_src)]
    plan_a, nsem_a = _gather_plan(3, late_chunks[1:])
    send_a, recv_a, arrays_a, token_a = _split_start(
        "late_gather_start_a", late_src[1:] + lands[1:], plan_a, nsem_a, [w_in0, mod_me])
    plan_b, nsem_b = _gather_plan(1, late_chunks[:1])
    send_b, recv_b, arrays_b, token_b = _split_start(
        "late_gather_start_b", late_src[:1] + lands[:1], plan_b, nsem_b, [w_in0, mod_me, arrays_a[0]])
    token = token_a + token_b

    def late(after):
        got = _split_wait("late_gather_wait_a", send_a, recv_a, arrays_a, plan_a, [after])[3:]
        w_in1 = lambda later: _split_wait("late_gather_wait_b", send_b, recv_b, arrays_b, plan_b, [later])[1]
        return dict(w_in1=w_in1, w_pa=got[0], w_pb=got[1], w_o=got[2])

    small_p = []
    for l in range(2):
        gates = _gate_tiles(w_rg[l], w_ig[l]).astype(_MXU)
        small_p.append(dict(
            shift=mod_me[l:l + 1, 0:D], scale=mod_me[l:l + 1, D:2 * D], gate=mod_me[l:l + 1, 2 * D:3 * D],
            g_pre=g_pre[l:l + 1], conv_w=conv_w_full[l], conv_b=conv_b[l:l + 1], wt=gates,
            b_rg=b_rg[l:l + 1], b_ig=b_ig[l:l + 1], lam=lru_lambda[l:l + 1], g_post=g_post[l:l + 1]))

    small_p[0]["shift"] = small_p[0]["shift"] + token[0, 0]

    core = jnp.reshape(ci, (1,)).astype(jnp.int32)
    where = jnp.stack([chip, ci]).astype(jnp.int32)
    dev1 = jnp.reshape(dev, (1,)).astype(jnp.int32)
    small_plan, small_nsem = _all_plan()
    small_state = {}

    def on_smalls(l, small):
        pack = _pack_small(small)
        land = _own_slot("own_small_%d" % l, pack, dev1, _SMALL_ROWS, slots=8)
        small_state[l] = _split_start("gather_small_start_%d" % l, [pack, land], small_plan, small_nsem, [])
        return small_state[l][3][0, 0]

    def small_done(l, after):
        send, recv, arrays, _ = small_state[l]
        return _split_wait("gather_small_wait_%d" % l, send, recv, arrays, small_plan, after)[1]

    loss_loc, dx, _, finish_reduce = _local_step(x[0], loss_target[0], small_p, w_in0, late,
                                                 _GradReduce(core, where), on_smalls)
    loss = lax.psum(loss_loc, ("x", "y", "c"))
    grad_x = dx[None]
    reduced = finish_reduce([dx])
    g_big ={k: a.reshape(2, 2 * a.shape[2], a.shape[3]) for k, a in zip(_BIG, reduced)}

    g3 = [small_done(l, [dx]) for l in range(2)]
    tot = [_sum_lead("sum_small_%d" % l, g3[l], _SMALL_ROWS) for l in range(2)]
    dmod_all = jnp.stack([g3[l][:, 0:3, :].reshape(8, 3 * D) for l in range(2)], axis=0)
    dm_cols = lax.dynamic_slice(dmod_all, (0, 0, chip * mcols), (2, 8, mcols))
    g_w_mod = _mod_bwd(jnp.transpose(c_all), dm_cols)
    both = lambda first, rows: jnp.stack([tot[l][first:first + rows] for l in range(2)], axis=0)
    vec = both(8, 6)
    grads = dict(
        w_mod=g_w_mod, b_mod=both(0, 3).reshape(2, 3 * D), g_pre=vec[:, 0], w_in=g_big["w_in"],
        conv_w=lax.dynamic_slice(both(16, 4), (0, 0, chip * 256), (2, 4, 256)), conv_b=vec[:, 1],
        w_rg=both(24, 64).reshape(2, 16, 64, 64), b_rg=vec[:, 2], w_ig=both(88, 64).reshape(2, 16, 64, 64),
        b_ig=vec[:, 3], lru_lambda=vec[:, 4], w_pa=g_big["w_pa"], w_pb=g_big["w_pb"], w_o=g_big["w_o"],
        g_post=vec[:, 5])

    weights = dict(w_mod=w_mod, b_mod=b_mod, g_pre=g_pre, w_in=w_in, conv_w=conv_w, conv_b=conv_b, w_rg=w_rg,
                   b_rg=b_rg, w_ig=w_ig, b_ig=b_ig, lru_lambda=lru_lambda, w_pa=w_pa, w_pb=w_pb, w_o=w_o,
                   g_post=g_post)
    ms = dict(w_mod=m_w_mod, b_mod=m_b_mod, g_pre=m_g_pre, w_in=m_w_in, conv_w=m_conv_w, conv_b=m_conv_b,
              w_rg=m_w_rg, b_rg=m_b_rg, w_ig=m_w_ig, b_ig=m_b_ig, lru_lambda=m_lru_lambda, w_pa=m_w_pa,
              w_pb=m_w_pb, w_o=m_w_o, g_post=m_g_post)
    vs = dict(w_mod=v_w_mod, b_mod=v_b_mod, g_pre=v_g_pre, w_in=v_w_in, conv_w=v_conv_w, conv_b=v_conv_b,
              w_rg=v_w_rg, b_rg=v_b_rg, w_ig=v_w_ig, b_ig=v_b_ig, lru_lambda=v_lru_lambda, w_pa=v_w_pa,
              w_pb=v_w_pb, w_o=v_w_o, g_post=v_g_post)
    flat = dict(w_mod=(2 * D, mcols, 256), b_mod=(2, 3 * D, 2), g_pre=(2, D, 2), w_in=(2 * D, 2304, 256),
                conv_w=(8, 256, 8), conv_b=(2, D, 2), w_rg=(128, D, 128), b_rg=(2, D, 2), w_ig=(128, D, 128),
                b_ig=(2, D, 2), lru_lambda=(2, D, 2), w_pa=(2 * ATT_W, 256, 256), w_pb=(512, D, 256),
                w_o=(512, D, 256), g_post=(2, D, 2))
    order = ("w_mod", "b_mod", "g_pre", "w_in", "conv_w", "conv_b", "w_rg", "b_rg", "w_ig", "b_ig",
             "lru_lambda", "w_pa", "w_pb", "w_o", "g_post")
    deltas, new_m, new_v = [], [], []
    for k in order:
        rows, cols, tb = flat[k]
        shp = weights[k].shape
        d, nm_, nv_ = _adamw("adamw_" + k, weights[k].reshape(rows, cols), grads[k].reshape(rows, cols),
                             ms[k].reshape(rows, cols), vs[k].reshape(rows, cols), tb)
        deltas.append(d.reshape(shp))
        new_m.append(nm_.reshape(shp))
        new_v.append(nv_.reshape(shp))
    return (loss, grad_x, *[grads[k].reshape(weights[k].shape) for k in order], *deltas, *new_m, *new_v)
```

```python
import functools

import jax
import jax.numpy as jnp
from jax import lax
from jax.experimental import pallas as pl
from jax.experimental.pallas import tpu as pltpu

_F32 = jnp.float32
_MXU = jnp.bfloat16
_VMEM_LIMIT = 56 * 1024 * 1024
_MESH = pl.DeviceIdType.MESH

D = 1024
HEAD = 128
HEADS = 4
ATT_W = 512
QKV_W = 1536
IN_W = 9216
DILATIONS = (1, 4, 16)
BAND = 128
QBLK = BAND * 16
NORM_EPS = 1e-6
NEG_INF = -1e30
LRU_C = 8.0
N_CHIPS = 4
CB_GATT = 4608 // 512
CB_U, CB_GLRU, CB_MA, CB_MB = 5, 6, 7, 8
R_U, R_GLRU, R_MA, R_MB, R_END = 512, 1536, 2560, 3584, 4608

ADAM_LR, ADAM_B1, ADAM_B2, ADAM_EPS, ADAM_WD, ADAM_STEP = 0.001, 0.9, 0.999, 1e-08, 0.01, 10


def _params(ngrid):
    return pltpu.CompilerParams(dimension_semantics=("arbitrary",) * ngrid, vmem_limit_bytes=_VMEM_LIMIT)


def _sigmoid(v):
    return 0.5 * jnp.tanh(0.5 * v) + 0.5


_GROUPS = {
    "c": [(0, 0, 1)],
    "xy": [(1, 0, 0), (0, 1, 0), (1, 1, 0)],
    "xyc": [(0, 0, 1), (0, 1, 0), (0, 1, 1), (1, 0, 0), (1, 0, 1), (1, 1, 0), (1, 1, 1)],
}


def _rank(group, px, py, pc):
    if group == "c":
        return pc
    if group == "xy":
        return 2 * px + py
    return 4 * px + 2 * py + pc


def _flip(rel, x, y, c):
    dx, dy, dc = rel
    return (1 - x if dx else x, 1 - y if dy else y, 1 - c if dc else c)


def _pieces(ref, nchunk):
    step = ref.shape[0] // nchunk
    return [ref.at[pl.ds(q * step, step)] for q in range(nchunk)]


def _exchange(name, srcs, group, scatter, *, local=True, nchunks=None):
    rels = _GROUPS[group]
    gsize = len(rels) + 1
    n = len(srcs)
    nchunks = nchunks or [1] * n
    blks = [s.shape[1:] if scatter else s.shape for s in srcs]
    slotted = local or gsize > 2
    base = [sum(nchunks[:a]) for a in range(n)]
    tot = sum(nchunks)

    def body(*refs):
        src_refs, out_refs = refs[:n], refs[n:2 * n]
        send_sems, recv_sems, loc_sems = refs[2 * n:]
        x, y, c = lax.axis_index("x"), lax.axis_index("y"), lax.axis_index("c")
        me = _rank(group, x, y, c)
        copies = []
        for a in range(n):
            def part(r, a=a):
                return src_refs[a].at[r] if scatter else src_refs[a]
            dst = out_refs[a].at[me] if slotted else out_refs[a]
            if local:
                for q, (s_, d_) in enumerate(zip(_pieces(part(me), nchunks[a]), _pieces(dst, nchunks[a]))):
                    loc = pltpu.make_async_copy(s_, d_, loc_sems.at[base[a] + q])
                    loc.start()
                    copies.append(loc)
            for k, rel in enumerate(rels):
                peer = _flip(rel, x, y, c)
                for q, (s_, d_) in enumerate(zip(_pieces(part(_rank(group, *peer)), nchunks[a]),
                                                 _pieces(dst, nchunks[a]))):
                    cp = pltpu.make_async_remote_copy(
                        src_ref=s_, dst_ref=d_, send_sem=send_sems.at[(base[a] + q) * len(rels) + k],
                        recv_sem=recv_sems.at[(base[a] + q) * len(rels) + k],
                        device_id=peer, device_id_type=_MESH)
                    cp.start()
                    copies.append(cp)
        for cp in copies:
            cp.wait()

    any_spec = pl.BlockSpec(memory_space=pl.ANY)
    lead = (gsize,) if slotted else ()
    return pl.pallas_call(
        body, name=name,
        out_shape=[jax.ShapeDtypeStruct(lead + tuple(b), s.dtype) for b, s in zip(blks, srcs)],
        in_specs=[any_spec] * n, out_specs=[any_spec] * n,
        scratch_shapes=[pltpu.SemaphoreType.DMA((tot * len(rels),)), pltpu.SemaphoreType.DMA((tot * len(rels),)),
                        pltpu.SemaphoreType.DMA((tot,))],
    )(*srcs)


def _gather_weights(wb, nchunks):
    n = len(wb)
    rels = _GROUPS["xy"]
    base = [sum(nchunks[:a]) for a in range(n)]
    tot = sum(nchunks)

    def body(*refs):
        src_refs, out_refs = refs[:n], refs[n:2 * n]
        ici_send, ici_recv, d2d_send, d2d_recv, loc_sems = refs[2 * n:]
        x, y, c = lax.axis_index("x"), lax.axis_index("y"), lax.axis_index("c")
        me = 2 * x + y
        waits = []
        for a in range(n):
            for l in range(2):
                for q, (s_, d_) in enumerate(zip(_pieces(src_refs[a].at[l], nchunks[a]),
                                                 _pieces(out_refs[a].at[me, l], nchunks[a]))):
                    loc = pltpu.make_async_copy(s_, d_, loc_sems.at[(base[a] + q) * 2 + l])
                    loc.start()
                    waits.append(loc)
        first = []
        for a in range(n):
            for k, rel in enumerate(rels):
                px, py, _ = _flip(rel, x, y, c)
                for q, (s_, d_) in enumerate(zip(_pieces(src_refs[a].at[c], nchunks[a]),
                                                 _pieces(out_refs[a].at[me, c], nchunks[a]))):
                    sem = (base[a] + q) * 3 + k
                    cp = pltpu.make_async_remote_copy(src_ref=s_, dst_ref=d_, send_sem=ici_send.at[sem],
                                                      recv_sem=ici_recv.at[sem], device_id=(px, py, c),
                                                      device_id_type=_MESH)
                    cp.start()
                    first.append(cp)
        second = []
        for a in range(n):
            for k, rel in enumerate(rels):
                px, py, _ = _flip(rel, x, y, c)
                for q, blk in enumerate(_pieces(out_refs[a].at[2 * px + py, c], nchunks[a])):
                    sem = (base[a] + q) * 3 + k
                    landed = pltpu.make_async_remote_copy(src_ref=blk, dst_ref=blk, send_sem=ici_send.at[sem],
                                                          recv_sem=ici_recv.at[sem], device_id=(px, py, c),
                                                          device_id_type=_MESH)
                    landed.wait_recv()
                    cp = pltpu.make_async_remote_copy(src_ref=blk, dst_ref=blk, send_sem=d2d_send.at[sem],
                                                      recv_sem=d2d_recv.at[sem], device_id=(x, y, 1 - c),
                                                      device_id_type=_MESH)
                    cp.start()
                    second.append(cp)
        for cp in first:
            cp.wait_send()
        for cp in second:
            cp.wait_send()
        for a in range(n):
            for k, rel in enumerate(rels):
                px, py, _ = _flip(rel, x, y, c)
                for q, blk in enumerate(_pieces(out_refs[a].at[2 * px + py, 1 - c], nchunks[a])):
                    sem = (base[a] + q) * 3 + k
                    pltpu.make_async_remote_copy(src_ref=blk, dst_ref=blk, send_sem=d2d_send.at[sem],
                                                 recv_sem=d2d_recv.at[sem], device_id=(x, y, 1 - c),
                                                 device_id_type=_MESH).wait_recv()
        for cp in waits:
            cp.wait()

    any_spec = pl.BlockSpec(memory_space=pl.ANY)
    return pl.pallas_call(
        body, name="gather_weights",
        out_shape=[jax.ShapeDtypeStruct((N_CHIPS,) + a.shape, a.dtype) for a in wb],
        in_specs=[any_spec] * n, out_specs=[any_spec] * n,
        scratch_shapes=[pltpu.SemaphoreType.DMA((tot * 3,))] * 4 + [pltpu.SemaphoreType.DMA((tot * 2,))],
    )(*wb)


_HBM = pl.BlockSpec(memory_space=pltpu.HBM)
_SEM = pl.BlockSpec(memory_space=pltpu.SEMAPHORE)
_EFFECT = pltpu.SideEffectType.DATAFLOW_SIDE_EFFECTING


def _own_slot(name, src, chip, tb, slots=N_CHIPS):
    rows, cols = src.shape[-2:]
    lead = src.shape[:-2]
    flat = src.reshape((-1, cols))

    def body(s_ref, a_ref, o_ref):
        o_ref[...] = a_ref[...]

    grid_spec = pltpu.PrefetchScalarGridSpec(
        num_scalar_prefetch=1, grid=(flat.shape[0] // tb,),
        in_specs=[pl.BlockSpec((tb, cols), lambda i, s: (i, 0))],
        out_specs=pl.BlockSpec((None, tb, cols), lambda i, s: (s[0], i, 0)))
    out = pl.pallas_call(body, name=name, grid_spec=grid_spec,
                         out_shape=jax.ShapeDtypeStruct((slots,) + flat.shape, src.dtype),
                         compiler_params=_params(1))(chip, flat)
    return out.reshape((slots,) + lead + (rows, cols))


def _numbered(pairs, peer, send_sems, recv_sems, first):
    return [pltpu.make_async_remote_copy(src_ref=s_, dst_ref=d_, send_sem=send_sems.at[first + q],
                                         recv_sem=recv_sems.at[first + q], device_id=peer, device_id_type=_MESH)
            for q, (s_, d_) in enumerate(pairs)]


def _gather_plan(n, nchunks):
    def plan(refs, send_sems, recv_sems):
        x, y, c = lax.axis_index("x"), lax.axis_index("y"), lax.axis_index("c")
        me = 2 * x + y
        copies = []
        for a in range(n):
            for rel in _GROUPS["xy"]:
                px, py, _ = _flip(rel, x, y, c)
                pairs = list(zip(_pieces(refs[a], nchunks[a]), _pieces(refs[n + a].at[me], nchunks[a])))
                copies += _numbered(pairs, (px, py, c), send_sems, recv_sems, len(copies))
        return copies
    return plan, 3 * sum(nchunks)


def _all_plan():
    def plan(refs, send_sems, recv_sems):
        x, y, c = lax.axis_index("x"), lax.axis_index("y"), lax.axis_index("c")
        me = 4 * x + 2 * y + c
        copies = []
        for rel in _GROUPS["xyc"]:
            copies += _numbered([(refs[0], refs[1].at[me])], _flip(rel, x, y, c), send_sems, recv_sems, len(copies))
        return copies
    return plan, len(_GROUPS["xyc"])


def _pair_plan(n, nchunks):
    def plan(refs, send_sems, recv_sems):
        x, y, c = lax.axis_index("x"), lax.axis_index("y"), lax.axis_index("c")
        copies = []
        for a in range(n):
            for j in range(N_CHIPS):
                pairs = list(zip(_pieces(refs[a].at[j, 1 - c], nchunks[a]), _pieces(refs[n + a].at[j], nchunks[a])))
                copies += _numbered(pairs, (x, y, 1 - c), send_sems, recv_sems, len(copies))
        return copies
    return plan, N_CHIPS * sum(nchunks)


def _chips_plan(n, nchunks):
    def plan(refs, send_sems, recv_sems):
        x, y, c = lax.axis_index("x"), lax.axis_index("y"), lax.axis_index("c")
        me = 2 * x + y
        copies = []
        for a in range(n):
            for rel in _GROUPS["xy"]:
                px, py, _ = _flip(rel, x, y, c)
                pairs = list(zip(_pieces(refs[a].at[2 * px + py], nchunks[a]), _pieces(refs[n + a].at[me], nchunks[a])))
                copies += _numbered(pairs, (px, py, c), send_sems, recv_sems, len(copies))
        return copies
    return plan, 3 * sum(nchunks)


def _fill_plan(n, nchunks, l):
    def plan(refs, send_sems, recv_sems):
        x, y, c = lax.axis_index("x"), lax.axis_index("y"), lax.axis_index("c")
        copies = []
        for a in range(n):
            blk = _pieces(refs[a].at[l, c], nchunks[a])
            copies += _numbered(list(zip(blk, blk)), (x, y, 1 - c), send_sems, recv_sems, len(copies))
        return copies
    return plan, sum(nchunks)


def _split_start(name, arrays, plan, nsem, after):
    n = len(arrays)
    na = len(after)

    def body(*refs):
        send_sems, recv_sems = refs[n + na], refs[n + na + 1]
        token = refs[-1]
        for cp in plan(refs[:n], send_sems, recv_sems):
            cp.start()
        token[...] = jnp.zeros_like(token)

    hbm = [pltpu.HBM(a.shape, a.dtype) for a in arrays]
    outs = pl.pallas_call(
        body, name=name,
        out_shape=(pltpu.SemaphoreType.DMA((nsem,)), pltpu.SemaphoreType.DMA((nsem,)), *hbm, _sds((8, 128))),
        in_specs=[_HBM] * n + [pl.BlockSpec(memory_space=pl.ANY)] * na,
        out_specs=(_SEM, _SEM, *([_HBM] * n), pl.BlockSpec(memory_space=pltpu.VMEM)),
        input_output_aliases={i: 2 + i for i in range(n)},
        compiler_params=pltpu.CompilerParams(has_side_effects=_EFFECT),
    )(*[pltpu.with_memory_space_constraint(a, pltpu.HBM) for a in arrays], *after)
    return outs[0], outs[1], list(outs[2:2 + n]), outs[-1]


def _split_wait(name, send_sems, recv_sems, arrays, plan, after):
    n = len(arrays)

    def body(*refs):
        for cp in plan(refs[:n], refs[n], refs[n + 1]):
            cp.wait_send()
            cp.wait_recv()

    hbm = [pltpu.HBM(a.shape, a.dtype) for a in arrays]
    return list(pl.pallas_call(
        body, name=name, out_shape=tuple(hbm),
        in_specs=[_HBM] * n + [_SEM, _SEM] + [pl.BlockSpec(memory_space=pl.ANY)] * len(after),
        out_specs=tuple([_HBM] * n), input_output_aliases={i: i for i in range(n)},
        compiler_params=pltpu.CompilerParams(has_side_effects=_EFFECT),
    )(*arrays, send_sems, recv_sems, *after))


def _mm(name, a, b, out_sds, *, grid, a_spec, b_spec, o_spec, dims, acc_shape, into=None):
    nk = grid[2]

    def body(*refs):
        a_ref, b_ref = refs[0], refs[1]
        o_ref, acc = refs[-2], refs[-1]
        k = pl.program_id(2)
        part = lax.dot_general(a_ref[...].astype(_MXU), b_ref[...].astype(_MXU), dims,
                               preferred_element_type=_F32)
        if nk == 1:
            o_ref[...] = part.astype(o_ref.dtype)
            return

        @pl.when(k == 0)
        def _():
            acc[...] = part

        @pl.when(k > 0)
        def _():
            acc[...] += part

        @pl.when(k == nk - 1)
        def _():
            o_ref[...] = acc[...].astype(o_ref.dtype).reshape(o_ref.shape)

    if nk == 1:
        acc_shape = (8, 128)
    in_specs = [a_spec, b_spec]
    args = [a, b]
    aliases = {}
    if into is not None:
        in_specs.append(pl.BlockSpec(memory_space=pl.ANY))
        args.append(into)
        aliases = {2: 0}
    return pl.pallas_call(
        body, name=name, grid=grid, in_specs=in_specs, out_specs=o_spec, out_shape=out_sds,
        scratch_shapes=[pltpu.VMEM(acc_shape, _F32)], input_output_aliases=aliases,
        compiler_params=_params(3))(*args)


_NN = (((1,), (0,)), ((), ()))
_NT = (((1,), (1,)), ((), ()))
_TN = (((0,), (0,)), ((), ()))


def _rowwise(name, body, *, grid, ins, outs, scratch=()):
    return pl.pallas_call(
        body, name=name, grid=(grid,), in_specs=[s for _, s in ins], out_specs=[s for _, s in outs],
        out_shape=[o for o, _ in outs], scratch_shapes=list(scratch),
        compiler_params=_params(1))(*[a for a, _ in ins])


def _rows(tb, w, cb=0, n=None):
    if n is None:
        return pl.BlockSpec((tb, w), lambda i: (i, cb))
    return pl.BlockSpec((tb, w), lambda i: (n - 1 - i, cb))


def _vec(shape):
    return pl.BlockSpec(shape, lambda i: (0,) * len(shape))


def _halo_prev(tb, w, cb=0, n=None, rows=8):
    if n is None:
        return pl.BlockSpec((rows, w), lambda i: (jnp.maximum(i * (tb // rows) - 1, 0), cb))
    return pl.BlockSpec((rows, w), lambda i: (jnp.maximum((n - 1 - i) * (tb // rows) - 1, 0), cb))


def _halo_next(tb, w, n, cb=0):
    return pl.BlockSpec((8, w), lambda i: (jnp.minimum((i + 1) * (tb // 8), n * (tb // 8) - 1), cb))


def _sds(shape, dtype=_F32):
    return jax.ShapeDtypeStruct(shape, dtype)


def _cast(name, a, tb):
    rows, cols = a.shape

    def body(a_ref, o_ref):
        o_ref[...] = a_ref[...].astype(o_ref.dtype)

    return _rowwise(name, body, grid=rows // tb, ins=[(a, _rows(tb, cols))],
                    outs=[(_sds((rows, cols), _MXU), _rows(tb, cols))])[0]


def _sum_lead(name, a, tb):
    g, rows, cols = a.shape

    def body(a_ref, o_ref):
        acc = a_ref[0]
        for k in range(1, g):
            acc = acc + a_ref[k]
        o_ref[...] = acc

    return _rowwise(name, body, grid=rows // tb,
                    ins=[(a, pl.BlockSpec((g, tb, cols), lambda i: (0, i, 0)))],
                    outs=[(_sds((rows, cols)), _rows(tb, cols))])[0]


def _sum_pair(name, mine, theirs, core, tb):
    nj, _, rows, cols = mine.shape

    def body(s_ref, a_ref, b_ref, o_ref, ob_ref):
        t = a_ref[...] + b_ref[...]
        o_ref[...] = t
        ob_ref[...] = t.astype(ob_ref.dtype)

    blk = pl.BlockSpec((None, tb, cols), lambda j, i, s: (j, i, 0))
    grid_spec = pltpu.PrefetchScalarGridSpec(
        num_scalar_prefetch=1, grid=(nj, rows // tb),
        in_specs=[pl.BlockSpec((None, None, tb, cols), lambda j, i, s: (j, s[0], i, 0)), blk],
        out_specs=[blk, blk])
    return pl.pallas_call(body, name=name, grid_spec=grid_spec,
                          out_shape=[_sds((nj, rows, cols)), _sds((nj, rows, cols), _MXU)],
                          compiler_params=_params(2))(core, mine, theirs)


def _sum_chips(name, mine, theirs, where, l, into, tb):
    _, rows, cols = mine.shape
    extra = [] if into is None else [into]

    def body(*refs):
        a_ref, b1_ref, b2_ref, b3_ref = refs[1:5]
        o_ref = refs[-1]
        o_ref[...] = ((a_ref[...] + b1_ref[...].astype(_F32)) + b2_ref[...].astype(_F32)) + b3_ref[...].astype(_F32)

    def slot(k):
        return pl.BlockSpec((None, tb, cols), lambda i, s: (jnp.bitwise_xor(s[0], k), i, 0))

    grid_spec = pltpu.PrefetchScalarGridSpec(
        num_scalar_prefetch=1, grid=(rows // tb,),
        in_specs=[slot(0), slot(1), slot(2), slot(3)] + [pl.BlockSpec(memory_space=pl.ANY)] * len(extra),
        out_specs=pl.BlockSpec((None, None, tb, cols), lambda i, s: (l, s[1], i, 0)))
    return pl.pallas_call(body, name=name, grid_spec=grid_spec, out_shape=_sds((2, 2, rows, cols)),
                          input_output_aliases={5: 0} if extra else {},
                          compiler_params=_params(1))(where, mine, theirs, theirs, theirs, *extra)


def _adamw(name, w, g, m, v, tb):
    rows, cols = w.shape
    c1 = 1.0 - ADAM_B1 ** ADAM_STEP
    c2 = 1.0 - ADAM_B2 ** ADAM_STEP

    def body(w_ref, g_ref, m_ref, v_ref, d_ref, nm_ref, nv_ref):
        gv = g_ref[...]
        nm = ADAM_B1 * m_ref[...] + (1.0 - ADAM_B1) * gv
        nv = ADAM_B2 * v_ref[...] + (1.0 - ADAM_B2) * (gv * gv)
        d_ref[...] = -ADAM_LR * ((nm / c1) / (jnp.sqrt(nv / c2) + ADAM_EPS) + ADAM_WD * w_ref[...])
        nm_ref[...] = nm
        nv_ref[...] = nv

    spec = _rows(tb, cols)
    return _rowwise(name, body, grid=rows // tb, ins=[(w, spec), (g, spec), (m, spec), (v, spec)],
                    outs=[(_sds((rows, cols)), spec)] * 3)


def _mod_fwd(c_all, w_mod, b_cols):
    cols = w_mod.shape[2]

    def body(c_ref, w_ref, b_ref, o_ref):
        cv = c_ref[...]
        sc = (cv * _sigmoid(cv)).astype(_MXU)
        o_ref[...] = jnp.dot(sc, w_ref[...].astype(_MXU), preferred_element_type=_F32) + b_ref[...]

    return pl.pallas_call(
        body, name="mod_fwd", grid=(2,),
        in_specs=[pl.BlockSpec((8, D), lambda l: (0, 0)), pl.BlockSpec((None, D, cols), lambda l: (l, 0, 0)),
                  pl.BlockSpec((None, 1, cols), lambda l: (l, 0, 0))],
        out_specs=pl.BlockSpec((None, 8, cols), lambda l: (l, 0, 0)),
        out_shape=_sds((2, 8, cols)), compiler_params=_params(1))(c_all, w_mod, b_cols)


def _mod_bwd(c_all_t, dm):
    cols = dm.shape[2]

    def body(c_ref, d_ref, o_ref):
        cv = c_ref[...]
        sc = (cv * _sigmoid(cv)).astype(_MXU)
        o_ref[...] = jnp.dot(sc, d_ref[...].astype(_MXU), preferred_element_type=_F32)

    return pl.pallas_call(
        body, name="mod_bwd", grid=(2,),
        in_specs=[pl.BlockSpec((D, 8), lambda l: (0, 0)), pl.BlockSpec((None, 8, cols), lambda l: (l, 0, 0))],
        out_specs=pl.BlockSpec((None, D, cols), lambda l: (l, 0, 0)),
        out_shape=_sds((2, D, cols)), compiler_params=_params(1))(c_all_t, dm)


def _proj(x, g_pre, shift, scale, w_in):
    s = x.shape[0]
    tm = 1024

    def body(x_ref, g_ref, sh_ref, sc_ref, w_ref, o_ref, ht_ref, h_s):
        @pl.when(pl.program_id(1) == 0)
        def _():
            xv = x_ref[...]
            rstd = lax.rsqrt(jnp.mean(xv * xv, axis=-1, keepdims=True) + NORM_EPS)
            hv = (xv * rstd) * g_ref[...] * (1.0 + sc_ref[...]) + sh_ref[...]
            h_s[...] = hv.astype(h_s.dtype)
            ht_ref[...] = hv.T.astype(ht_ref.dtype)

        o_ref[...] = jnp.dot(h_s[...], w_ref[...], preferred_element_type=_F32).astype(o_ref.dtype)

    vec = pl.BlockSpec((1, D), lambda m, n: (0, 0))
    return pl.pallas_call(
        body, name="proj", grid=(s // tm, N_CHIPS),
        in_specs=[pl.BlockSpec((tm, D), lambda m, n: (m, 0)), vec, vec, vec,
                  pl.BlockSpec((None, D, 2304), lambda m, n: (n, 0, 0))],
        out_specs=[pl.BlockSpec((tm, 2304), lambda m, n: (m, n)), pl.BlockSpec((D, tm), lambda m, n: (0, m))],
        out_shape=[_sds((s, IN_W), _MXU), _sds((D, s), _MXU)],
        scratch_shapes=[pltpu.VMEM((tm, D), _MXU)], compiler_params=_params(2))(x, g_pre, shift, scale, w_in)


def _shift_down(cur, halo, j, tb):
    ext = jnp.concatenate([halo, cur], axis=0)
    return pltpu.roll(ext, j, 0)[8:8 + tb]


def _shift_up(cur, halo, j, tb):
    ext = jnp.concatenate([cur, halo], axis=0)
    return pltpu.roll(ext, tb + 8 - j, 0)[0:tb]


def _conv(u_ref, halo_ref, w_ref, b_ref, first, tb):
    u = u_ref[...].astype(_F32)
    halo = jnp.where(first, 0.0, halo_ref[...].astype(_F32)[8:16])
    acc = b_ref[...] + u * w_ref[0:1, :]
    for j in range(1, 4):
        acc = acc + _shift_down(u, halo, j, tb) * w_ref[j:j + 1, :]
    return acc


def _lru_gates(pre_r, pre_i, uc, b_rg, b_ig, lam):
    r = _sigmoid(pre_r + b_rg)
    ig = _sigmoid(pre_i + b_ig)
    nl = -lam
    sp = jnp.maximum(nl, 0.0) + jnp.log(1.0 + jnp.exp(-jnp.abs(nl)))
    la = -LRU_C * r * sp
    a = jnp.exp(la)
    one_m_a2 = -jnp.tanh(la) * (a * a + 1.0)
    inv_sq = lax.rsqrt(jnp.maximum(one_m_a2, 1e-30))
    return r, ig, sp, a, one_m_a2 * inv_sq, inv_sq


GATE_TILES = 8


def _gate_tiles(w_rg, w_ig):
    eye = jnp.eye(2, dtype=w_rg.dtype)

    def tiles(w):
        return jnp.einsum("cpij,pq->cpiqj", w.reshape(GATE_TILES, 2, 64, 64), eye).reshape(GATE_TILES, 128, 128)

    return jnp.concatenate([tiles(w_rg), tiles(w_ig)], axis=2)


def _gate_tile_grads(gw):
    keep = jnp.eye(2, dtype=jnp.bool_)[None, :, None, :, None]

    def blocks(t):
        t5 = t.reshape(GATE_TILES, 2, 64, 2, 64)
        return jnp.sum(jnp.where(keep, t5, 0.0), axis=3).reshape(16, 64, 64)

    return blocks(gw[:, :, 0:128]), blocks(gw[:, :, 128:256])


def _gate_preacts(ucv, wt_ref):
    ucb = ucv.astype(_MXU)
    ps = [jnp.dot(ucb[:, 128 * c:128 * (c + 1)], wt_ref[c], preferred_element_type=_F32) for c in range(GATE_TILES)]
    pre_r = jnp.concatenate([p[:, 0:128] for p in ps], axis=1)
    pre_i = jnp.concatenate([p[:, 128:256] for p in ps], axis=1)
    return pre_r, pre_i


def _scan_fwd(proj, conv_w, conv_b, wt, b_rg, b_ig, lam):
    s = proj.shape[0]
    tb = 256

    def body(u_ref, up_ref, cw_ref, cb_ref, wt_ref, brg_ref, big_ref, lam_ref, h_ref, carry, a_s, b_s):
        i = pl.program_id(0)

        @pl.when(i == 0)
        def _():
            carry[...] = jnp.zeros_like(carry)

        ucv = _conv(u_ref, up_ref, cw_ref, cb_ref, i == 0, tb)
        pre_r, pre_i = _gate_preacts(ucv, wt_ref)
        _, ig, _, a, sq, _ = _lru_gates(pre_r, pre_i, ucv, brg_ref[...], big_ref[...], lam_ref[...])
        av = a
        bv = sq * (ig * ucv)
        av = av.reshape(tb // 8, 8, D)
        bv = bv.reshape(tb // 8, 8, D)
        row8 = lax.broadcasted_iota(jnp.int32, (1, 8, 1), 1)
        for sh in (1, 2, 4):
            m = row8 >= sh
            b_sh = pltpu.roll(bv, sh, 1)
            a_sh = pltpu.roll(av, sh, 1)
            bv = jnp.where(m, av * b_sh + bv, bv)
            av = jnp.where(m, av * a_sh, av)
        a_s[...] = av.reshape(tb, D)
        b_s[...] = bv.reshape(tb, D)

        def tile(t, state):
            rows = pl.ds(pl.multiple_of(t * 8, 8), 8)
            hv = b_s[rows, :] + a_s[rows, :] * state
            b_s[rows, :] = hv
            return jnp.broadcast_to(hv[7:8, :], (8, D))

        carry[...] = lax.fori_loop(0, tb // 8, tile, jnp.broadcast_to(carry[7:8, :], (8, D)), unroll=4)
        h_ref[...] = b_s[...].astype(h_ref.dtype)

    v = _vec((1, D))
    return _rowwise("scan_fwd", body, grid=s // tb,
                    ins=[(proj, _rows(tb, D, CB_U)), (proj, _halo_prev(tb, D, CB_U, rows=16)),
                         (conv_w, _vec((4, D))), (conv_b, v), (wt, _vec((GATE_TILES, 128, 256))),
                         (b_rg, v), (b_ig, v), (lam, v)],
                    outs=[(_sds((s, D), _MXU), _rows(tb, D))],
                    scratch=[pltpu.VMEM((8, D), _F32), pltpu.VMEM((tb, D), _F32), pltpu.VMEM((tb, D), _F32)])[0]


def _weight_specs(l):
    return [pl.BlockSpec((N_CHIPS, None, ATT_W, 256), lambda i: (0, l, 0, 0)),
            pl.BlockSpec((N_CHIPS, None, 256, D), lambda i: (0, l, 0, 0)),
            pl.BlockSpec((N_CHIPS, None, 256, D), lambda i: (0, l, 0, 0))]


def _tail_fwd(l, o, h_lru, proj, x, gate, g_post, gw, target):
    s = x.shape[0]
    tb = 512

    def body(*refs):
        o_ref, h_ref, ga_ref, gl_ref, ma_ref, mb_ref, x_ref, gt_ref, gp_ref, wpa_ref, wpb_ref, wo_ref = refs[0:12]
        aa_ref, ba_ref, ya_ref, yb_ref, z_ref, out_ref = refs[-8:-2] if target is not None else refs[-7:-1]
        ga = ga_ref[...].astype(_F32)
        aa32 = o_ref[...] * (ga * _sigmoid(ga))
        aa = aa32.astype(_MXU)
        aa_ref[...] = aa32.T.astype(aa_ref.dtype)
        gl = gl_ref[...].astype(_F32)
        ba32 = h_ref[...].astype(_F32) * (gl * _sigmoid(gl))
        ba = ba32.astype(_MXU)
        ba_ref[...] = ba32.T.astype(ba_ref.dtype)
        ya = jnp.concatenate([jnp.dot(aa, wpa_ref[j], preferred_element_type=_F32) for j in range(N_CHIPS)], axis=1)
        ya_ref[...] = ya.astype(ya_ref.dtype)
        yb = jnp.dot(ba, wpb_ref[...].reshape(D, D), preferred_element_type=_F32)
        yb_ref[...] = yb.astype(yb_ref.dtype)
        z32 = _sigmoid(ma_ref[...].astype(_F32)) * ya + _sigmoid(mb_ref[...].astype(_F32)) * yb
        z = z32.astype(_MXU)
        z_ref[...] = z32.T.astype(z_ref.dtype)
        ov = jnp.dot(z, wo_ref[...].reshape(D, D), preferred_element_type=_F32)
        out_ref[...] = ov.astype(out_ref.dtype)
        rstd = lax.rsqrt(jnp.mean(ov * ov, axis=-1, keepdims=True) + NORM_EPS)
        xn =x_ref[...] + gt_ref[...] * ((ov * rstd) * gp_ref[...])
        if target is None:
            refs[-1][...] = xn
        else:
            dy_ref, acc_ref = refs[-2], refs[-1]
            err = xn - refs[12][...]
            dy_ref[...] = err * (1.0 / D)
            _zero_first(pl.program_id(0), acc_ref)
            acc_ref[...] += jnp.sum(err * err, axis=0, keepdims=True)

    v = _vec((1, D))
    r = _rows(tb, D)
    r5 = _rows(tb, ATT_W)
    weights = list(zip((gw["w_pa"], gw["w_pb"], gw["w_o"]), _weight_specs(l)))
    cols = pl.BlockSpec((D, tb), lambda i: (0, i))
    head_in = [] if target is None else [(target, r)]
    head_out = [] if target is None else [(_sds((1, D)), v)]
    return _rowwise("tail_fwd" if target is None else "tail_loss_fwd", body, grid=s // tb,
                    ins=[(o, r5), (h_lru, r), (proj, _rows(tb, ATT_W, CB_GATT)), (proj, _rows(tb, D, CB_GLRU)),
                         (proj, _rows(tb, D, CB_MA)), (proj, _rows(tb, D, CB_MB)), (x, r), (gate, v), (g_post, v)]
                    + weights + head_in,
                    outs=[(_sds((ATT_W, s), _MXU), pl.BlockSpec((ATT_W, tb), lambda i: (0, i))),
                          (_sds((D, s), _MXU), cols), (_sds((s, D), _MXU), r), (_sds((s, D), _MXU), r),
                          (_sds((D, s), _MXU), cols), (_sds((s, D), _MXU), r), (_sds((s, D)), r)]
                    + head_out)


def _zero_first(i, *refs):
    @pl.when(i == 0)
    def _():
        for ref in refs:
            ref[...] = jnp.zeros_like(ref)


def _tail_bwd(l, dx, out, y_a, y_b, proj, o, h_lru, gate, g_post, gw):
    s = dx.shape[0]
    tb = 256

    def body(dx_ref, out_ref, ya_ref, yb_ref, ma_ref, mb_ref, o_ref, ga_ref, h_ref, gl_ref, gt_ref, gp_ref,
             wpa_ref, wpb_ref, wo_ref,
             dout_ref, dya_ref, dyb_ref, rest_ref, do_ref, dh_ref, dgt_ref, dgp_ref):
        i = pl.program_id(0)
        ov = out_ref[...].astype(_F32)
        dxv = dx_ref[...]
        rstd = lax.rsqrt(jnp.mean(ov * ov, axis=-1, keepdims=True) + NORM_EPS)
        nv = ov * rstd
        s_dn = jnp.sum(dxv * nv, axis=0, keepdims=True)
        _zero_first(i, dgt_ref, dgp_ref)
        dgt_ref[...] += s_dn * gp_ref[...]
        dgp_ref[...] += s_dn * gt_ref[...]
        dn = dxv * (gt_ref[...] * gp_ref[...])
        d_out = (rstd * (dn - nv * jnp.mean(dn * nv, axis=-1, keepdims=True))).astype(_MXU)
        dout_ref[...] = d_out
        dz = lax.dot_general(d_out, wo_ref[...].reshape(D, D), _NT, preferred_element_type=_F32)
        ga = _sigmoid(ma_ref[...].astype(_F32))
        gb = _sigmoid(mb_ref[...].astype(_F32))
        dya = (dz * ga).astype(_MXU)
        dyb = (dz * gb).astype(_MXU)
        dya_ref[...] = dya
        dyb_ref[...] = dyb
        rest_ref[:, R_MA:R_MB] = (dz * ya_ref[...].astype(_F32) * ga * (1.0 - ga)).astype(rest_ref.dtype)
        rest_ref[:, R_MB:R_END] = (dz * yb_ref[...].astype(_F32) * gb * (1.0 - gb)).astype(rest_ref.dtype)
        daa = lax.dot_general(dya[:, 0:256], wpa_ref[0], _NT, preferred_element_type=_F32)
        for j in range(1, N_CHIPS):
            daa = daa + lax.dot_general(dya[:, j * 256:(j + 1) * 256], wpa_ref[j], _NT, preferred_element_type=_F32)
        dba = lax.dot_general(dyb, wpb_ref[...].reshape(D, D), _NT, preferred_element_type=_F32)
        gav = ga_ref[...].astype(_F32)
        sa = _sigmoid(gav)
        do_ref[...] = daa * (gav * sa)
        rest_ref[:, 0:R_U] = (daa * o_ref[...] * (sa * (1.0 + gav * (1.0 - sa)))).astype(rest_ref.dtype)
        gl = gl_ref[...].astype(_F32)
        sl = _sigmoid(gl)
        dh_ref[...] = dba * (gl * sl)
        rest_ref[:, R_GLRU:R_MA] = (dba * h_ref[...].astype(_F32)
                                    * (sl * (1.0 + gl * (1.0 - sl)))).astype(rest_ref.dtype)

    v = _vec((1, D))
    r5, r10 = _rows(tb, ATT_W), _rows(tb, D)
    return _rowwise("tail_bwd", body, grid=s // tb,
                    ins=[(dx, r10), (out, r10), (y_a, r10), (y_b, r10), (proj, _rows(tb, D, CB_MA)),
                         (proj, _rows(tb, D, CB_MB)), (o, r5), (proj, _rows(tb, ATT_W, CB_GATT)), (h_lru, r10),
                         (proj, _rows(tb, D, CB_GLRU)), (gate, v), (g_post, v)]
                    + list(zip((gw["w_pa"], gw["w_pb"], gw["w_o"]), _weight_specs(l))),
                    outs=[(_sds((s, D), _MXU), r10), (_sds((s, D), _MXU), r10), (_sds((s, D), _MXU), r10),
                          (_sds((s, R_END), _MXU), _rows(tb, R_END)),
                          (_sds((s, ATT_W)), r5), (_sds((s, D)), r10), (_sds((1, D)), v), (_sds((1, D)), v)])


def _scan_bwd(dh, proj, conv_w, conv_b, h_lru, wt, b_rg, b_ig, lam):
    s = dh.shape[0]
    tb = 256
    n = s // tb

    def body(dh_ref, u_ref, up_ref, cw_ref, cb_ref, h_ref, hp_ref, wt_ref, brg_ref, big_ref, lam_ref,
             duc_ref, dwt_ref, dbrg_ref, dbig_ref, dlam_ref, carry, c_s, g_s):
        i = pl.program_id(0)

        @pl.when(i == 0)
        def _():
            carry[...] = jnp.zeros_like(carry)
            for acc_ref in (dwt_ref, dbrg_ref, dbig_ref, dlam_ref):
                acc_ref[...] = jnp.zeros_like(acc_ref)

        ucv = _conv(u_ref, up_ref, cw_ref, cb_ref, i == n - 1, tb)
        pre_r, pre_i = _gate_preacts(ucv, wt_ref)
        r, ig, sp, a, sq, inv_sq =_lru_gates(pre_r, pre_i, ucv, brg_ref[...], big_ref[...], lam_ref[...])
        row = lax.broadcasted_iota(jnp.int32, (tb, 1), 0)
        cv = jnp.where(row == tb - 1, 1.0, pltpu.roll(a, tb - 1, 0))
        gv = dh_ref[...]
        cv = cv.reshape(tb // 8, 8, D)
        gv = gv.reshape(tb // 8, 8, D)
        row8 = lax.broadcasted_iota(jnp.int32, (1, 8, 1), 1)
        for sh in (1, 2, 4):
            m = row8 < 8 - sh
            g_sh = pltpu.roll(gv, 8 - sh, 1)
            c_sh = pltpu.roll(cv, 8 - sh, 1)
            gv = jnp.where(m, gv + cv * g_sh, gv)
            cv = jnp.where(m, cv * c_sh, cv)
        c_s[...] = cv.reshape(tb, D)
        g_s[...] = gv.reshape(tb, D)

        def tile(k, state):
            rows = pl.ds(pl.multiple_of((tb // 8 - 1 - k) * 8, 8), 8)
            gt = g_s[rows, :] + c_s[rows, :] * state
            g_s[rows, :] = gt
            return jnp.broadcast_to(gt[0:1, :], (8, D))

        lax.fori_loop(0, tb // 8, tile, jnp.broadcast_to(carry[0:1, :], (8, D)), unroll=4)
        gv = g_s[...]
        carry[...] = (a * gv)[0:8]

        halo = jnp.where(i < n - 1, hp_ref[...].astype(_F32)[8:16], 0.0)
        h_prev = _shift_down(h_ref[...].astype(_F32), halo, 1, tb)
        d_a = gv * h_prev
        d_sq = gv * (ig * ucv)
        d_i = gv * sq * ucv
        d_la = d_a * a - d_sq * (a * a) * inv_sq
        d_r = d_la * (-LRU_C * sp)
        d_pre_r = d_r * r * (1.0 - r)
        d_pre_i = d_i * ig * (1.0 - ig)
        ucb = ucv.astype(_MXU)
        dpr = d_pre_r.astype(_MXU)
        dpi = d_pre_i.astype(_MXU)
        back = []
        for c in range(GATE_TILES):
            lanes = slice(128 * c, 128 * (c + 1))
            dp = jnp.concatenate([dpr[:, lanes], dpi[:, lanes]], axis=1)
            back.append(lax.dot_general(dp, wt_ref[c], _NT, preferred_element_type=_F32))
            dwt_ref[c] += lax.dot_general(ucb[:, lanes], dp, _TN, preferred_element_type=_F32)
        duc_ref[...] = gv * sq * ig + jnp.concatenate(back, axis=1)
        dbrg_ref[...] += jnp.sum(d_pre_r, axis=0, keepdims=True)
        dbig_ref[...] += jnp.sum(d_pre_i, axis=0, keepdims=True)
        lamv = lam_ref[...]
        dlam_ref[...] += jnp.sum(d_la * (-LRU_C * r), axis=0, keepdims=True) * (-_sigmoid(-lamv))

    v = _vec((1, D))
    rv = _rows(tb, D, 0, n)
    return _rowwise("scan_bwd", body, grid=n,
                    ins=[(dh, rv), (proj, _rows(tb, D, CB_U, n)), (proj, _halo_prev(tb, D, CB_U, n, rows=16)),
                         (conv_w, _vec((4, D))), (conv_b, v), (h_lru, rv), (h_lru, _halo_prev(tb, D, 0, n, rows=16)),
                         (wt, _vec((GATE_TILES, 128, 256))), (b_rg, v), (b_ig, v), (lam, v)],
                    outs=[(_sds((s, D)), rv), (_sds((GATE_TILES, 128, 256)), _vec((GATE_TILES, 128, 256))),
                          (_sds((1, D)), v), (_sds((1, D)), v), (_sds((1, D)), v)],
                    scratch=[pltpu.VMEM((8, D), _F32), pltpu.VMEM((tb, D), _F32), pltpu.VMEM((tb, D), _F32)])


def _conv_bwd(duc_a, proj, conv_w, rest):
    s = duc_a.shape[0]
    tb = 512
    n = s // tb
    hw = D // 2

    def body(da_ref, dan_ref, u_ref, up_ref, w_ref, rest_in, du_ref, dw_ref, dbias_ref):
        i = pl.program_id(1)
        duc = da_ref[...]
        nxt = jnp.where(i < n - 1, dan_ref[...], 0.0)
        u = u_ref[...].astype(_F32)
        halo = jnp.where(i > 0, up_ref[...].astype(_F32)[8:16], 0.0)
        du = duc * w_ref[0:1, :]
        dws = [jnp.sum(duc * u, axis=0, keepdims=True)]
        for j in range(1, 4):
            du = du + _shift_up(duc, nxt, j, tb) * w_ref[j:j + 1, :]
            dws.append(jnp.sum(duc * _shift_down(u, halo, j, tb), axis=0, keepdims=True))
        du_ref[...] = du.astype(du_ref.dtype)
        _zero_first(i, dw_ref, dbias_ref)
        for j in range(4):
            dw_ref[j:j + 1, :] += dws[j]
        dbias_ref[...] += jnp.sum(duc, axis=0, keepdims=True)

    r = pl.BlockSpec((tb, hw), lambda h, i: (i, h))
    nxt_spec = pl.BlockSpec((8, hw), lambda h, i: (jnp.minimum((i + 1) * (tb // 8), n * (tb // 8) - 1), h))
    return pl.pallas_call(
        body, name="conv_bwd", grid=(2, n),
        in_specs=[r, nxt_spec,
                  pl.BlockSpec((tb, hw), lambda h, i: (i, 2 * CB_U + h)),
                  pl.BlockSpec((16, hw), lambda h, i: (jnp.maximum(i * (tb // 16) - 1, 0), 2 * CB_U + h)),
                  pl.BlockSpec((4, hw), lambda h, i: (0, h)), pl.BlockSpec(memory_space=pl.ANY)],
        out_specs=[pl.BlockSpec((tb, hw), lambda h, i: (i, R_U // hw + h)),
                   pl.BlockSpec((4, hw), lambda h, i: (0, h)), pl.BlockSpec((1, hw), lambda h, i: (0, h))],
        out_shape=[_sds(rest.shape, rest.dtype), _sds((4, D)), _sds((1, D))],
        input_output_aliases={5: 0}, compiler_params=_params(2),
    )(duc_a, duc_a, proj, proj, conv_w, rest)


def _band_tiles(dil):
    tiles = []
    for rho in range(dil):
        for b in range(16 // dil):
            qs = rho + dil * BAND * b
            tiles.append((qs, QBLK + qs - dil * BAND, b))
    return tiles


def _strided(start, size, dil):
    return pl.ds(start, size, stride=dil) if dil > 1 else pl.ds(start, size)


def _band_mask(i, b):
    qi = lax.broadcasted_iota(jnp.int32, (BAND, 2 * BAND), 0)
    ki = lax.broadcasted_iota(jnp.int32, (BAND, 2 * BAND), 1)
    valid = (ki >= qi) & (ki <= qi + BAND)
    if b == 0:
        valid = valid & ((ki >= BAND) | (i > 0))
    return valid


def _attn_fwd(proj):
    s = proj.shape[0]
    n = s // QBLK
    scale = HEAD ** -0.5

    def body(*refs):
        q_refs, kp_refs, kc_refs, vp_refs, vc_refs = (refs[3 * t:3 * t + 3] for t in range(5))
        o_ref, lse_ref, qbuf, kbuf, vbuf = refs[15:20]
        accs, maxs, dens = refs[20:23], refs[23:26], refs[26:29]
        i = pl.program_id(1)
        for g, dil in enumerate(DILATIONS):
            qbuf[...] = q_refs[g][...].astype(_F32)
            kbuf[0:QBLK, :] = kp_refs[g][...].astype(_F32)
            kbuf[QBLK:2 * QBLK, :] = kc_refs[g][...].astype(_F32)
            vbuf[0:QBLK, :] = vp_refs[g][...].astype(_F32)
            vbuf[QBLK:2 * QBLK, :] = vc_refs[g][...].astype(_F32)
            for qs, ks, b in _band_tiles(dil):
                qsl = _strided(qs, BAND, dil)
                q = qbuf[qsl, :].astype(_MXU)
                kk = kbuf[_strided(ks, 2 * BAND, dil), :].astype(_MXU)
                vv = vbuf[_strided(ks, 2 * BAND, dil), :].astype(_MXU)
                sc = lax.dot_general(q, kk, _NT, preferred_element_type=_F32) * scale
                sc = jnp.where(_band_mask(i, b), sc, NEG_INF)
                m = jnp.max(sc, axis=-1, keepdims=True)
                p = jnp.exp(sc - m)
                accs[g][qsl, :] = jnp.dot(p.astype(_MXU), vv, preferred_element_type=_F32)
                maxs[g][qsl, :] = jnp.broadcast_to(m, (BAND, HEAD))
                dens[g][qsl, :] = jnp.broadcast_to(jnp.sum(p, axis=-1, keepdims=True), (BAND, HEAD))
        ms = [r[...] for r in maxs]
        mx = jnp.maximum(jnp.maximum(ms[0], ms[1]), ms[2])
        ws = [jnp.exp(m - mx) for m in ms]
        den = ws[0] * dens[0][...] + ws[1] * dens[1][...] + ws[2] * dens[2][...]
        o_ref[...] = (ws[0] * accs[0][...] + ws[1] * accs[1][...] + ws[2] * accs[2][...]) / den
        lse_ref[...] = mx + jnp.log(den)

    blk = (QBLK, HEAD)

    def spec(first_col, lag):
        specs = []
        for g in range(3):
            col = first_col + g * HEADS
            if lag:
                specs.append(pl.BlockSpec(blk, lambda j, i, col=col: (jnp.maximum(i - 1, 0), col + j)))
            else:
                specs.append(pl.BlockSpec(blk, lambda j, i, col=col: (i, col + j)))
        return specs

    out_spec = pl.BlockSpec(blk, lambda j, i: (i, j))
    return pl.pallas_call(
        body, name="attn_fwd", grid=(HEADS, n),
        in_specs=spec(0, False) + spec(12, True) + spec(12, False) + spec(24, True) + spec(24, False),
        out_specs=[out_spec] * 2, out_shape=[_sds((s, ATT_W))] * 2,
        scratch_shapes=[pltpu.VMEM(blk, _F32)] + [pltpu.VMEM((2 * QBLK, HEAD), _F32)] * 2
        + [pltpu.VMEM(blk, _F32)] * 9,
        compiler_params=_params(2))(*([proj] * 15))


def _attn_bwd(proj, d_o, o, lse, g, into):
    s = proj.shape[0]
    dil = DILATIONS[g]
    n = s // QBLK
    scale = HEAD ** -0.5
    tiles = _band_tiles(dil)

    def body(*refs):
        q_ref, kp_ref, kc_ref, vp_ref, vc_ref, do_ref, o_ref, lse_ref = refs[0:8]
        dq_ref, dk_ref, dv_ref, kbuf, vbuf, dkbuf, dvbuf, dqbuf, qbuf = refs[-9:]
        i = pl.program_id(1)

        @pl.when(i == 0)
        def _():
            dkbuf[0:QBLK, :] = jnp.zeros((QBLK, HEAD), _F32)
            dvbuf[0:QBLK, :] = jnp.zeros((QBLK, HEAD), _F32)

        @pl.when(i < n)
        def _():
            qbuf[...] = q_ref[...].astype(_F32)
            kbuf[0:QBLK, :] = kp_ref[...].astype(_F32)
            kbuf[QBLK:2 * QBLK, :] = kc_ref[...].astype(_F32)
            vbuf[0:QBLK, :] = vp_ref[...].astype(_F32)
            vbuf[QBLK:2 * QBLK, :] = vc_ref[...].astype(_F32)
            dkbuf[QBLK:2 * QBLK, :] = jnp.zeros((QBLK, HEAD), _F32)
            dvbuf[QBLK:2 * QBLK, :] = jnp.zeros((QBLK, HEAD), _F32)
            for qs, ks, b in tiles:
                qsl = _strided(qs, BAND, dil)
                ksl = _strided(ks, 2 * BAND, dil)
                q = qbuf[qsl, :].astype(_MXU)
                kk = kbuf[ksl, :].astype(_MXU)
                vv = vbuf[ksl, :].astype(_MXU)
                dov = do_ref[qsl, :]
                dd = jnp.sum(dov * o_ref[qsl, :], axis=-1, keepdims=True)
                lse_t = lse_ref[qsl, :][:, 0:1]
                sc = lax.dot_general(q, kk, _NT, preferred_element_type=_F32) * scale
                p = jnp.where(_band_mask(i, b), jnp.exp(sc - lse_t), 0.0)
                dob = dov.astype(_MXU)
                dp = lax.dot_general(dob, vv, _NT, preferred_element_type=_F32)
                ds = (p * (dp - dd) * scale).astype(_MXU)
                dqbuf[qsl, :] = jnp.dot(ds, kk, preferred_element_type=_F32)
                dkbuf[ksl, :] += lax.dot_general(ds, q, _TN, preferred_element_type=_F32)
                dvbuf[ksl, :] += lax.dot_general(p.astype(_MXU), dob, _TN, preferred_element_type=_F32)
            dq_ref[...] = dqbuf[...].astype(dq_ref.dtype)

        dk_ref[...] = dkbuf[0:QBLK, :].astype(dk_ref.dtype)
        dv_ref[...] = dvbuf[0:QBLK, :].astype(dv_ref.dtype)
        dkbuf[0:QBLK, :] = dkbuf[QBLK:2 * QBLK, :]
        dvbuf[0:QBLK, :] = dvbuf[QBLK:2 * QBLK, :]

    blk = (QBLK, HEAD)
    cq, ck, cv = g * HEADS, 12 + g * HEADS, 24 + g * HEADS

    def cur(i):
        return jnp.minimum(i, n - 1)

    def prev(i):
        return jnp.maximum(jnp.minimum(i, n - 1) - 1, 0)

    own = pl.BlockSpec(blk, lambda j, i: (cur(i), j))
    own_out = pl.BlockSpec(blk, lambda j, i: (cur(i), cq + j))
    late_out = pl.BlockSpec(blk, lambda j, i: (jnp.maximum(i - 1, 0), cq + j))
    extra = [] if into is None else list(into)
    return pl.pallas_call(
        body, name="attn_bwd_d%d" % dil, grid=(HEADS, n + 1),
        in_specs=[pl.BlockSpec(blk, lambda j, i: (cur(i), cq + j)),
                  pl.BlockSpec(blk, lambda j, i: (prev(i), ck + j)),
                  pl.BlockSpec(blk, lambda j, i: (cur(i), ck + j)),
                  pl.BlockSpec(blk, lambda j, i: (prev(i), cv + j)),
                  pl.BlockSpec(blk, lambda j, i: (cur(i), cv + j)),
                  own, own, own] + [pl.BlockSpec(memory_space=pl.ANY)] * len(extra),
        out_specs=[own_out, late_out, late_out], out_shape=[_sds((s, QKV_W), _MXU)] * 3,
        input_output_aliases={8 + t: t for t in range(len(extra))},
        scratch_shapes=[pltpu.VMEM((2 * QBLK, HEAD), _F32)] * 4 + [pltpu.VMEM((QBLK, HEAD), _F32)] * 2,
        compiler_params=_params(2))(proj, proj, proj, proj, proj, d_o, o, lse, *extra)


_PARTS = ((0, 2), (2, 2), (4, 2), (6, 6))
_CHUNK = 768


def _d_x(name, parts, w_in, x, dx_out, g_pre, scale, blocks, into):
    s = parts[0].shape[0]
    nk = IN_W // _CHUNK
    first_block, n_blocks = blocks

    def body(*refs):
        p0, p1, p2, p3, w_ref, x_ref, dxo_ref, g_ref, sc_ref = refs[0:9]
        dx_ref, dsh_ref, dsc_ref, dg_ref, acc = refs[-5:]
        m = pl.program_id(0)
        k = pl.program_id(2)

        @pl.when(k == 0)
        def _():
            acc[...] = jnp.zeros_like(acc)

        @pl.when((k == 0) & (m == 0))
        def _():
            for ref in (dsh_ref, dsc_ref, dg_ref):
                ref[...] = jnp.zeros_like(ref)

        for p_ref, (first, cnt) in zip((p0, p1, p2, p3), _PARTS):
            @pl.when((k >= first) & (k < first + cnt))
            def _(p_ref=p_ref):
                acc[...] += lax.dot_general(p_ref[...].astype(_MXU), w_ref[...], _NT, preferred_element_type=_F32)

        @pl.when(k == nk - 1)
        def _():
            dhv = acc[...]
            xv = x_ref[...]
            rstd = lax.rsqrt(jnp.mean(xv * xv, axis=-1, keepdims=True) + NORM_EPS)
            xn = xv * rstd
            one_sc = 1.0 + sc_ref[...]
            s1 = jnp.sum(dhv * xn, axis=0, keepdims=True)
            dsh_ref[...] += jnp.sum(dhv, axis=0, keepdims=True)
            dsc_ref[...] += s1 * g_ref[...]
            dg_ref[...] += s1 * one_sc
            dxn = dhv * (g_ref[...] * one_sc)
            dx_ref[...] = dxo_ref[...] + rstd * (dxn - xn * jnp.mean(dxn * xn, axis=-1, keepdims=True))

    def part_spec(first, cnt):
        return pl.BlockSpec((1024, _CHUNK), lambda m, n, k: (first_block + m, jnp.clip(k - first, 0, cnt - 1)))

    rows = pl.BlockSpec((1024, D), lambda m, n, k: (first_block + m, 0))
    vec = pl.BlockSpec((1, D), lambda m, n, k: (0, 0))
    extra = [] if into is None else [into]
    return pl.pallas_call(
        body, name=name, grid=(n_blocks, 1, nk),
        in_specs=[part_spec(*p) for p in _PARTS]
        + [pl.BlockSpec((None, D, _CHUNK), lambda m, n, k: (k // 3, 0, k % 3)), rows, rows, vec, vec]
        + [pl.BlockSpec(memory_space=pl.ANY)] * len(extra),
        out_specs=[rows, vec, vec, vec], out_shape=[_sds((s, D)), _sds((1, D)), _sds((1, D)), _sds((1, D))],
        input_output_aliases={9: 0} if extra else {},
        scratch_shapes=[pltpu.VMEM((1024, D), _F32)], compiler_params=_params(3))(
            *parts, w_in, x, dx_out, g_pre, scale, *extra)


def _g_w_in(h_t, parts):
    s = h_t.shape[1]
    nk = s // 1024

    def body(*refs):
        h_ref, p_refs = refs[0], refs[1:5]
        o_ref, acc = refs[-2], refs[-1]
        n = pl.program_id(1)
        k = pl.program_id(2)

        @pl.when(k == 0)
        def _():
            acc[...] = jnp.zeros_like(acc)

        for p_ref, (first, cnt) in zip(p_refs, _PARTS):
            @pl.when((n >= first) & (n < first + cnt))
            def _(p_ref=p_ref):
                acc[...] += jnp.dot(h_ref[...], p_ref[...].astype(_MXU), preferred_element_type=_F32)

        @pl.when(k == nk - 1)
        def _():
            o_ref[...] = acc[...]

    def part_spec(first, cnt):
        def index(m, n, k):
            row = jnp.where(n < first, 0, jnp.where(n >= first + cnt, nk - 1, k))
            return (row, jnp.clip(n - first, 0, cnt - 1))
        return pl.BlockSpec((1024, _CHUNK), index)

    return pl.pallas_call(
        body, name="g_w_in", grid=(1, IN_W // _CHUNK, nk),
        in_specs=[pl.BlockSpec((D, 1024), lambda m, n, k: (0, k))] + [part_spec(*p) for p in _PARTS],
        out_specs=pl.BlockSpec((None, D, _CHUNK), lambda m, n, k: (n // 3, 0, n % 3)),
        out_shape=_sds((N_CHIPS, D, 2304)),
        scratch_shapes=[pltpu.VMEM((D, _CHUNK), _F32)], compiler_params=_params(3))(h_t, *parts)


def _layer_fwd(l, x, p, gw, late, target):
    if callable(gw["w_in"][l]):
        gw["w_in"][l] = gw["w_in"][l](x)
    proj, h_t = _proj(x, p["g_pre"], p["shift"], p["scale"], gw["w_in"][l])
    o, lse = _attn_fwd(proj)
    h_lru = _scan_fwd(proj, p["conv_w"], p["conv_b"], p["wt"], p["b_rg"], p["b_ig"], p["lam"])
    if late is not None:
        landed = dict(late(h_lru))
        gw["w_in"].append(landed.pop("w_in1"))
        gw.update(landed)
    a_att, b_act, y_a, y_b, z, out, *last = _tail_fwd(l, o, h_lru, proj, x, p["gate"], p["g_post"], gw, target)
    saved = dict(x=x, h_t=h_t, proj=proj, o=o, lse=lse, h_lru=h_lru, a_att=a_att, b_act=b_act,
                 y_a=y_a, y_b=y_b, z=z, out=out)
    return (last[0] if target is None else last), saved


def _layer_bwd(l, dx, p, gw, sv, hooks):
    s = dx.shape[0]
    nt = s // 2048
    proj = sv["proj"]
    gate, b_rg, g_pre = p["gate"], p["b_rg"], p["g_pre"]
    if hooks is not None:
        gate = gate + hooks[0]([dx])
    d_out, dy_a, dy_b, d_rest, d_o, dh_lru, d_gate, d_gpost = _tail_bwd(
        l, dx, sv["out"], sv["y_a"], sv["y_b"], proj, sv["o"], sv["h_lru"], gate, p["g_post"], gw)
    if hooks is not None:
        b_rg = b_rg + hooks[1]([d_out])

    def wgrad_rows(name, a, b):
        return _mm(name, a, b, _sds((N_CHIPS, 256, D)), grid=(1, 1, nt),
                   a_spec=pl.BlockSpec((D, 2048), lambda m, n, k: (0, k)),
                   b_spec=pl.BlockSpec((2048, D), lambda m, n, k: (k, 0)),
                   o_spec=pl.BlockSpec((N_CHIPS, 256, D), lambda m, n, k: (0, 0, 0)),
                   dims=_NN, acc_shape=(D, D))

    big = {}
    big["w_o"] = wgrad_rows("g_w_o", sv["z"], d_out)
    big["w_pa"] = _mm("g_w_pa", sv["a_att"], dy_a, _sds((N_CHIPS, ATT_W, 256)), grid=(1, 4, nt),
                      a_spec=pl.BlockSpec((ATT_W, 2048), lambda m, n, k: (0, k)),
                      b_spec=pl.BlockSpec((2048, 256), lambda m, n, k: (k, n)),
                      o_spec=pl.BlockSpec((None, ATT_W, 256), lambda m, n, k: (n, 0, 0)),
                      dims=_NN, acc_shape=(ATT_W, 256))
    big["w_pb"] = wgrad_rows("g_w_pb", sv["b_act"], dy_b)
    duc, g_wt, d_brg, d_big, d_lam = _scan_bwd(dh_lru, proj, p["conv_w"], p["conv_b"], sv["h_lru"], p["wt"], b_rg,
                                               p["b_ig"], p["lam"])
    g_wrg, g_wig = _gate_tile_grads(g_wt)
    d_rest, g_convw, g_convb = _conv_bwd(duc, proj, p["conv_w"], d_rest)
    dqkv = None
    for g in range(3):
        dqkv = _attn_bwd(proj, d_o, sv["o"], sv["lse"], g, dqkv)
    if hooks is not None:
        g_pre = g_pre + hooks[2]([dqkv[0]])
    parts = (dqkv[0], dqkv[1], dqkv[2], d_rest)
    big["w_in"] = _g_w_in(sv["h_t"], parts)
    nb = s // 1024
    if hooks is None:
        dx_in, d_shift, d_scale, d_gpre = _d_x("d_x", parts, gw["w_in"][l], sv["x"], dx, g_pre, p["scale"],
                                               (0, nb), None)
    else:
        first = _d_x("d_x_a", parts, gw["w_in"][l], sv["x"], dx, g_pre + hooks[3](big), p["scale"],
                     (0, nb // 2), None)
        second = _d_x("d_x_b", parts, gw["w_in"][l], sv["x"], dx, g_pre + hooks[4]([first[0]]), p["scale"],
                      (nb // 2, nb - nb // 2), first[0])
        dx_in = second[0]
        d_shift, d_scale, d_gpre = (a + b for a, b in zip(first[1:], second[1:]))
    small = dict(dmod=jnp.concatenate([d_shift, d_scale, d_gate], axis=1), g_pre=d_gpre, conv_w=g_convw,
                 conv_b=g_convb, w_rg=g_wrg, b_rg=d_brg, w_ig=g_wig, b_ig=d_big, lam=d_lam, g_post=d_gpost)
    return dx_in, small, big


_BIG = ("w_in", "w_pa", "w_pb", "w_o")


class _GradReduce:
    PAIR_CHUNKS = (2, 1, 1, 1)
    CHIP_CHUNKS = (2, 1, 1, 1)
    FILL_CHUNKS = (4, 1, 1, 1)

    def __init__(self, core, where):
        self.core, self.where = core, where
        self.finals = None

    def begin(self, l, big):
        n = len(_BIG)
        halves = [big[k].reshape(N_CHIPS, 2, big[k].shape[1] // 2, big[k].shape[2]) for k in _BIG]
        lands = [lax.empty((N_CHIPS,) + h.shape[2:], _F32) for h in halves]
        plan, nsem = _pair_plan(n, self.PAIR_CHUNKS)
        state = {}
        state["pair"] = _split_start("reduce_pair_start_%d" % l, halves + lands, plan, nsem, [])

        def started(after):
            return state["pair"][3][0, 0]

        def pair_done(after):
            send, recv, arrays, _ = state["pair"]
            arrays = _split_wait("reduce_pair_wait_%d" % l, send, recv, arrays, plan, after)
            sums = [_sum_pair("sum_pair_%s_%d" % (k, l), arrays[a], arrays[n + a], self.core, 128)
                    for a, k in enumerate(_BIG)]
            state["mine"] = [t[0] for t in sums]
            lands2 = [lax.empty(t[1].shape, _MXU) for t in sums]
            plan2, nsem2 = _chips_plan(n, self.CHIP_CHUNKS)
            state["plan2"] = plan2
            state["chips"] = _split_start("reduce_chips_start_%d" % l, [t[1] for t in sums] + lands2, plan2, nsem2, [])
            return state["chips"][3][0, 0]

        def chips_done(after):
            send, recv, arrays, _ = state["chips"]
            arrays = _split_wait("reduce_chips_wait_%d" % l, send, recv, arrays, state["plan2"], after)
            finals = [_sum_chips("sum_chips_%s_%d" % (k, l), state["mine"][a], arrays[n + a], self.where, l,
                                 None if self.finals is None else self.finals[a], 128)
                      for a, k in enumerate(_BIG)]
            plan3, nsem3 = _fill_plan(n, self.FILL_CHUNKS, l)
            state["plan3"] = plan3
            state["fill"] = _split_start("gather_halves_start_%d" % l, finals, plan3, nsem3, [])
            return state["fill"][3][0, 0]

        def finish(after):
            send, recv, arrays, _ = state["fill"]
            self.finals = _split_wait("gather_halves_wait_%d" % l, send, recv, arrays, state["plan3"], after)
            return self.finals

        self._finish = finish
        return [started, pair_done, chips_done]

    def finish(self, after):
        return self._finish(after)


def _local_step(x, target, small_p, w_in0, late, reducer, on_smalls):
    saved = []
    h = x
    gw = dict(w_in=[w_in0])
    h, sv = _layer_fwd(0, h, small_p[0], gw, late, None)
    saved.append(sv)
    (dy, sq), sv = _layer_fwd(1, h, small_p[1], gw, None, target)
    saved.append(sv)
    loss = 0.5 * jnp.sum(sq) / D
    smalls = [None, None]
    dx, smalls[1], big1 = _layer_bwd(1, dy, small_p[1], gw, saved[1], None)
    hooks1 = reducer.begin(1, big1)
    small_started = on_smalls(1, smalls[1])
    pair_started = hooks1[0]
    hooks1[0] = lambda after: pair_started(after) + small_started
    own = {}

    def layer0_ready(big0):
        reducer.finish([big0["w_in"]])
        own["hooks"] = reducer.begin(0, big0)
        return own["hooks"][0]([])

    dx, smalls[0], _ = _layer_bwd(0, dx, small_p[0], gw, saved[0],
                                  hooks1 + [layer0_ready, lambda after: own["hooks"][1](after)])
    on_smalls(0, smalls[0])

    def finish_reduce(after):
        own["hooks"][2](after)
        return reducer.finish(after)

    return loss, dx, smalls, finish_reduce


_SMALL_ROWS = 8 + 8 + 8 + 64 + 64
_SMALL_VECS = ("g_pre", "conv_b", "b_rg", "b_ig", "lam", "g_post")


def _pack_small(small):
    pad = lambda rows: jnp.zeros((rows, D), _F32)
    return jnp.concatenate(
        [small["dmod"].reshape(3, D), pad(5)] + [small[k] for k in _SMALL_VECS] + [pad(2)]
        + [small["conv_w"], pad(4), small["w_rg"].reshape(64, D), small["w_ig"].reshape(64, D)], axis=0)


def kernel(x, c, w_mod, b_mod, g_pre, w_in, conv_w, conv_b, w_rg, b_rg, w_ig, b_ig, lru_lambda, w_pa, w_pb, w_o, g_post, loss_target, m_w_mod, m_b_mod, m_g_pre, m_w_in, m_conv_w, m_conv_b, m_w_rg, m_b_rg, m_w_ig, m_b_ig, m_lru_lambda, m_w_pa, m_w_pb, m_w_o, m_g_post, v_w_mod, v_b_mod, v_g_pre, v_w_in, v_conv_w, v_conv_b, v_w_rg, v_b_rg, v_w_ig, v_b_ig, v_lru_lambda, v_w_pa, v_w_pb, v_w_o, v_g_post):
    xi, yi, ci = lax.axis_index("x"), lax.axis_index("y"), lax.axis_index("c")
    chip = 2 * xi + yi
    dev = 4 * xi + 2 * yi + ci
    mcols = w_mod.shape[2]

    pack1 = jnp.concatenate([jnp.broadcast_to(c, (8, D)),
                             jnp.pad(conv_w.reshape(8, 256), ((0, 0), (0, D - 256)))], axis=0)
    g1 = _exchange("gather_cond", [pack1], "xyc", False)[0]
    c_all = g1[:, 0, :]
    conv_w_full = jnp.transpose(g1[0::2, 8:16, 0:256], (1, 0, 2)).reshape(2, 4, D)

    b_cols = lax.dynamic_slice(b_mod, (0, chip * mcols), (2, mcols)).reshape(2, 1, mcols)
    mod_loc = _mod_fwd(c_all, w_mod, b_cols)
    g2 = _exchange("gather_mod", [mod_loc.reshape(16, mcols)], "xyc", False)[0]
    mod_full = jnp.transpose(g2[0::2], (1, 0, 2)).reshape(2, 8, 3 * D)
    mod_me = lax.dynamic_index_in_dim(mod_full, dev, axis=1, keepdims=False)

    wb_in = _cast("cast_w_in", w_in.reshape(2 * D, 2304), 256).reshape(2, D, 2304)
    late_src = [wb_in[1], _cast("cast_w_pa", w_pa.reshape(2 * ATT_W, 256), 256).reshape(2, ATT_W, 256),
                _cast("cast_w_pb", w_pb.reshape(512, D), 256).reshape(2, 256, D),
                _cast("cast_w_o", w_o.reshape(512, D), 256).reshape(2, 256, D)]
    late_chunks = [4, 2, 2, 2]
    w_in0 = _gather_weights([wb_in[0].reshape(2, D // 2, 2304)], [2])[0].reshape(N_CHIPS, D, 2304)
    chip1 = jnp.reshape(chip, (1,)).astype(jnp.int32)
    lands = [_own_slot("own_slot_" + k, a, chip1, 256) for k, a in zip(("w_in", "w_pa", "w_pb", "w_o"), late_src)]
    plan_a, nsem_a = _gather_plan(3, late_chunks[1:])
    send_a, recv_a, arrays_a, token_a = _split_start(
        "late_gather_start_a", late_src[1:] + lands[1:], plan_a, nsem_a, [w_in0, mod_me])
    plan_b, nsem_b = _gather_plan(1, late_chunks[:1])
    send_b, recv_b, arrays_b, token_b = _split_start(
        "late_gather_start_b", late_src[:1] + lands[:1], plan_b, nsem_b, [w_in0, mod_me, arrays_a[0]])
    token = token_a + token_b

    def late(after):
        got = _split_wait("late_gather_wait_a", send_a, recv_a, arrays_a, plan_a, [after])[3:]
        w_in1 = lambda later: _split_wait("late_gather_wait_b", send_b, recv_b, arrays_b, plan_b, [later])[1]
        return dict(w_in1=w_in1, w_pa=got[0], w_pb=got[1], w_o=got[2])

    small_p = []
    for l in range(2):
        gates = _gate_tiles(w_rg[l], w_ig[l]).astype(_MXU)
        small_p.append(dict(
            shift=mod_me[l:l + 1, 0:D], scale=mod_me[l:l + 1, D:2 * D], gate=mod_me[l:l + 1, 2 * D:3 * D],
            g_pre=g_pre[l:l + 1], conv_w=conv_w_full[l], conv_b=conv_b[l:l + 1], wt=gates,
            b_rg=b_rg[l:l + 1], b_ig=b_ig[l:l + 1], lam=lru_lambda[l:l + 1], g_post=g_post[l:l + 1]))

    small_p[0]["shift"] = small_p[0]["shift"] + token[0, 0]

    core = jnp.reshape(ci, (1,)).astype(jnp.int32)
    where = jnp.stack([chip, ci]).astype(jnp.int32)
    dev1 = jnp.reshape(dev, (1,)).astype(jnp.int32)
    small_plan, small_nsem = _all_plan()
    small_state = {}

    def on_smalls(l, small):
        pack = _pack_small(small)
        land = _own_slot("own_small_%d" % l, pack, dev1, _SMALL_ROWS, slots=8)
        small_state[l] = _split_start("gather_small_start_%d" % l, [pack, land], small_plan, small_nsem, [])
        return small_state[l][3][0, 0]

    def small_done(l, after):
        send, recv, arrays, _ = small_state[l]
        return _split_wait("gather_small_wait_%d" % l, send, recv, arrays, small_plan, after)[1]

    loss_loc, dx, _, finish_reduce = _local_step(x[0], loss_target[0], small_p, w_in0, late,
                                                 _GradReduce(core, where), on_smalls)
    loss = lax.psum(loss_loc, ("x", "y", "c"))
    grad_x = dx[None]
    reduced = finish_reduce([dx])
    g_big ={k: a.reshape(2, 2 * a.shape[2], a.shape[3]) for k, a in zip(_BIG, reduced)}

    weights = dict(w_mod=w_mod, b_mod=b_mod, g_pre=g_pre, w_in=w_in, conv_w=conv_w, conv_b=conv_b, w_rg=w_rg,
                   b_rg=b_rg, w_ig=w_ig, b_ig=b_ig, lru_lambda=lru_lambda, w_pa=w_pa, w_pb=w_pb, w_o=w_o,
                   g_post=g_post)
    ms = dict(w_mod=m_w_mod, b_mod=m_b_mod, g_pre=m_g_pre, w_in=m_w_in, conv_w=m_conv_w, conv_b=m_conv_b,
              w_rg=m_w_rg, b_rg=m_b_rg, w_ig=m_w_ig, b_ig=m_b_ig, lru_lambda=m_lru_lambda, w_pa=m_w_pa,
              w_pb=m_w_pb, w_o=m_w_o, g_post=m_g_post)
    vs = dict(w_mod=v_w_mod, b_mod=v_b_mod, g_pre=v_g_pre, w_in=v_w_in, conv_w=v_conv_w, conv_b=v_conv_b,
              w_rg=v_w_rg, b_rg=v_b_rg, w_ig=v_w_ig, b_ig=v_b_ig, lru_lambda=v_lru_lambda, w_pa=v_w_pa,
              w_pb=v_w_pb, w_o=v_w_o, g_post=v_g_post)
    flat = dict(w_mod=(2 * D, mcols, 256), b_mod=(2, 3 * D, 2), g_pre=(2, D, 2), w_in=(2 * D, 2304, 256),
                conv_w=(8, 256, 8), conv_b=(2, D, 2), w_rg=(128, D, 128), b_rg=(2, D, 2), w_ig=(128, D, 128),
                b_ig=(2, D, 2), lru_lambda=(2, D, 2), w_pa=(2 * ATT_W, 256, 256), w_pb=(512, D, 256),
                w_o=(512, D, 256), g_post=(2, D, 2))
    order = ("w_mod", "b_mod", "g_pre", "w_in", "conv_w", "conv_b", "w_rg", "b_rg", "w_ig", "b_ig",
             "lru_lambda", "w_pa", "w_pb", "w_o", "g_post")

    def adam(k, g):
        rows, cols, tb = flat[k]
        return _adamw("adamw_" + k, weights[k].reshape(rows, cols), g.reshape(rows, cols),
                      ms[k].reshape(rows, cols), vs[k].reshape(rows, cols), tb)

    stepped = {k: adam(k, g_big[k]) for k in _BIG}

    g3 = [small_done(l, [stepped["w_in"][0]]) for l in range(2)]
    tot =[_sum_lead("sum_small_%d" % l, g3[l], _SMALL_ROWS) for l in range(2)]
    dmod_all = jnp.stack([g3[l][:, 0:3, :].reshape(8, 3 * D) for l in range(2)], axis=0)
    dm_cols = lax.dynamic_slice(dmod_all, (0, 0, chip * mcols), (2, 8, mcols))
    g_w_mod = _mod_bwd(jnp.transpose(c_all), dm_cols)
    both = lambda first, rows: jnp.stack([tot[l][first:first + rows] for l in range(2)], axis=0)
    vec = both(8, 6)
    grads = dict(
        w_mod=g_w_mod, b_mod=both(0, 3).reshape(2, 3 * D), g_pre=vec[:, 0], w_in=g_big["w_in"],
        conv_w=lax.dynamic_slice(both(16, 4), (0, 0, chip * 256), (2, 4, 256)), conv_b=vec[:, 1],
        w_rg=both(24, 64).reshape(2, 16, 64, 64), b_rg=vec[:, 2], w_ig=both(88, 64).reshape(2, 16, 64, 64),
        b_ig=vec[:, 3], lru_lambda=vec[:, 4], w_pa=g_big["w_pa"], w_pb=g_big["w_pb"], w_o=g_big["w_o"],
        g_post=vec[:, 5])
    for k in order:
        if k not in stepped:
            stepped[k] = adam(k, grads[k])
    deltas, new_m, new_v = ([stepped[k][t].reshape(weights[k].shape) for k in order] for t in range(3))
    return (loss, grad_x, *[grads[k].reshape(weights[k].shape) for k in order], *deltas, *new_m, *new_v)
```

```python
import functools

import jax
import jax.numpy as jnp
from jax import lax
from jax.experimental import pallas as pl
from jax.experimental.pallas import tpu as pltpu

_F32 = jnp.float32
_MXU = jnp.bfloat16
_VMEM_LIMIT = 56 * 1024 * 1024
_MESH = pl.DeviceIdType.MESH

D = 1024
HEAD = 128
HEADS = 4
ATT_W = 512
QKV_W = 1536
IN_W = 9216
DILATIONS = (1, 4, 16)
BAND = 128
QBLK = BAND * 16
NORM_EPS = 1e-6
NEG_INF = -1e30
LRU_C = 8.0
N_CHIPS = 4
CB_GATT = 4608 // 512
CB_U, CB_GLRU, CB_MA, CB_MB = 5, 6, 7, 8
R_U, R_GLRU, R_MA, R_MB, R_END = 512, 1536, 2560, 3584, 4608

ADAM_LR, ADAM_B1, ADAM_B2, ADAM_EPS, ADAM_WD, ADAM_STEP = 0.001, 0.9, 0.999, 1e-08, 0.01, 10


def _params(ngrid):
    return pltpu.CompilerParams(dimension_semantics=("arbitrary",) * ngrid, vmem_limit_bytes=_VMEM_LIMIT)


def _sigmoid(v):
    return 0.5 * jnp.tanh(0.5 * v) + 0.5


_GROUPS = {
    "c": [(0, 0, 1)],
    "xy": [(1, 0, 0), (0, 1, 0), (1, 1, 0)],
    "xyc": [(0, 0, 1), (0, 1, 0), (0, 1, 1), (1, 0, 0), (1, 0, 1), (1, 1, 0), (1, 1, 1)],
}


def _rank(group, px, py, pc):
    if group == "c":
        return pc
    if group == "xy":
        return 2 * px + py
    return 4 * px + 2 * py + pc


def _flip(rel, x, y, c):
    dx, dy, dc = rel
    return (1 - x if dx else x, 1 - y if dy else y, 1 - c if dc else c)


def _pieces(ref, nchunk):
    step = ref.shape[0] // nchunk
    return [ref.at[pl.ds(q * step, step)] for q in range(nchunk)]


def _exchange(name, srcs, group, scatter, *, local=True, nchunks=None):
    rels = _GROUPS[group]
    gsize = len(rels) + 1
    n = len(srcs)
    nchunks = nchunks or [1] * n
    blks = [s.shape[1:] if scatter else s.shape for s in srcs]
    slotted = local or gsize > 2
    base = [sum(nchunks[:a]) for a in range(n)]
    tot = sum(nchunks)

    def body(*refs):
        src_refs, out_refs = refs[:n], refs[n:2 * n]
        send_sems, recv_sems, loc_sems = refs[2 * n:]
        x, y, c = lax.axis_index("x"), lax.axis_index("y"), lax.axis_index("c")
        me = _rank(group, x, y, c)
        copies = []
        for a in range(n):
            def part(r, a=a):
                return src_refs[a].at[r] if scatter else src_refs[a]
            dst = out_refs[a].at[me] if slotted else out_refs[a]
            if local:
                for q, (s_, d_) in enumerate(zip(_pieces(part(me), nchunks[a]), _pieces(dst, nchunks[a]))):
                    loc = pltpu.make_async_copy(s_, d_, loc_sems.at[base[a] + q])
                    loc.start()
                    copies.append(loc)
            for k, rel in enumerate(rels):
                peer = _flip(rel, x, y, c)
                for q, (s_, d_) in enumerate(zip(_pieces(part(_rank(group, *peer)), nchunks[a]),
                                                 _pieces(dst, nchunks[a]))):
                    cp = pltpu.make_async_remote_copy(
                        src_ref=s_, dst_ref=d_, send_sem=send_sems.at[(base[a] + q) * len(rels) + k],
                        recv_sem=recv_sems.at[(base[a] + q) * len(rels) + k],
                        device_id=peer, device_id_type=_MESH)
                    cp.start()
                    copies.append(cp)
        for cp in copies:
            cp.wait()

    any_spec = pl.BlockSpec(memory_space=pl.ANY)
    lead = (gsize,) if slotted else ()
    return pl.pallas_call(
        body, name=name,
        out_shape=[jax.ShapeDtypeStruct(lead + tuple(b), s.dtype) for b, s in zip(blks, srcs)],
        in_specs=[any_spec] * n, out_specs=[any_spec] * n,
        scratch_shapes=[pltpu.SemaphoreType.DMA((tot * len(rels),)), pltpu.SemaphoreType.DMA((tot * len(rels),)),
                        pltpu.SemaphoreType.DMA((tot,))],
    )(*srcs)


def _gather_weights(wb, nchunks):
    n = len(wb)
    rels = _GROUPS["xy"]
    base = [sum(nchunks[:a]) for a in range(n)]
    tot = sum(nchunks)

    def body(*refs):
        src_refs, out_refs = refs[:n], refs[n:2 * n]
        ici_send, ici_recv, d2d_send, d2d_recv, loc_sems = refs[2 * n:]
        x, y, c = lax.axis_index("x"), lax.axis_index("y"), lax.axis_index("c")
        me = 2 * x + y
        waits = []
        for a in range(n):
            for l in range(2):
                for q, (s_, d_) in enumerate(zip(_pieces(src_refs[a].at[l], nchunks[a]),
                                                 _pieces(out_refs[a].at[me, l], nchunks[a]))):
                    loc = pltpu.make_async_copy(s_, d_, loc_sems.at[(base[a] + q) * 2 + l])
                    loc.start()
                    waits.append(loc)
        first = []
        for a in range(n):
            for k, rel in enumerate(rels):
                px, py, _ = _flip(rel, x, y, c)
                for q, (s_, d_) in enumerate(zip(_pieces(src_refs[a].at[c], nchunks[a]),
                                                 _pieces(out_refs[a].at[me, c], nchunks[a]))):
                    sem = (base[a] + q) * 3 + k
                    cp = pltpu.make_async_remote_copy(src_ref=s_, dst_ref=d_, send_sem=ici_send.at[sem],
                                                      recv_sem=ici_recv.at[sem], device_id=(px, py, c),
                                                      device_id_type=_MESH)
                    cp.start()
                    first.append(cp)
        second = []
        for a in range(n):
            for k, rel in enumerate(rels):
                px, py, _ = _flip(rel, x, y, c)
                for q, blk in enumerate(_pieces(out_refs[a].at[2 * px + py, c], nchunks[a])):
                    sem = (base[a] + q) * 3 + k
                    landed = pltpu.make_async_remote_copy(src_ref=blk, dst_ref=blk, send_sem=ici_send.at[sem],
                                                          recv_sem=ici_recv.at[sem], device_id=(px, py, c),
                                                          device_id_type=_MESH)
                    landed.wait_recv()
                    cp = pltpu.make_async_remote_copy(src_ref=blk, dst_ref=blk, send_sem=d2d_send.at[sem],
                                                      recv_sem=d2d_recv.at[sem], device_id=(x, y, 1 - c),
                                                      device_id_type=_MESH)
                    cp.start()
                    second.append(cp)
        for cp in first:
            cp.wait_send()
        for cp in second:
            cp.wait_send()
        for a in range(n):
            for k, rel in enumerate(rels):
                px, py, _ = _flip(rel, x, y, c)
                for q, blk in enumerate(_pieces(out_refs[a].at[2 * px + py, 1 - c], nchunks[a])):
                    sem = (base[a] + q) * 3 + k
                    pltpu.make_async_remote_copy(src_ref=blk, dst_ref=blk, send_sem=d2d_send.at[sem],
                                                 recv_sem=d2d_recv.at[sem], device_id=(x, y, 1 - c),
                                                 device_id_type=_MESH).wait_recv()
        for cp in waits:
            cp.wait()

    any_spec = pl.BlockSpec(memory_space=pl.ANY)
    return pl.pallas_call(
        body, name="gather_weights",
        out_shape=[jax.ShapeDtypeStruct((N_CHIPS,) + a.shape, a.dtype) for a in wb],
        in_specs=[any_spec] * n, out_specs=[any_spec] * n,
        scratch_shapes=[pltpu.SemaphoreType.DMA((tot * 3,))] * 4 + [pltpu.SemaphoreType.DMA((tot * 2,))],
    )(*wb)


_HBM = pl.BlockSpec(memory_space=pltpu.HBM)
_SEM = pl.BlockSpec(memory_space=pltpu.SEMAPHORE)
_EFFECT = pltpu.SideEffectType.DATAFLOW_SIDE_EFFECTING


def _own_slot(name, src, chip, tb, slots=N_CHIPS):
    rows, cols = src.shape[-2:]
    lead = src.shape[:-2]
    flat = src.reshape((-1, cols))

    def body(s_ref, a_ref, o_ref):
        o_ref[...] = a_ref[...]

    grid_spec = pltpu.PrefetchScalarGridSpec(
        num_scalar_prefetch=1, grid=(flat.shape[0] // tb,),
        in_specs=[pl.BlockSpec((tb, cols), lambda i, s: (i, 0))],
        out_specs=pl.BlockSpec((None, tb, cols), lambda i, s: (s[0], i, 0)))
    out = pl.pallas_call(body, name=name, grid_spec=grid_spec,
                         out_shape=jax.ShapeDtypeStruct((slots,) + flat.shape, src.dtype),
                         compiler_params=_params(1))(chip, flat)
    return out.reshape((slots,) + lead + (rows, cols))


def _numbered(pairs, peer, send_sems, recv_sems, first):
    return [pltpu.make_async_remote_copy(src_ref=s_, dst_ref=d_, send_sem=send_sems.at[first + q],
                                         recv_sem=recv_sems.at[first + q], device_id=peer, device_id_type=_MESH)
            for q, (s_, d_) in enumerate(pairs)]


def _gather_plan(n, nchunks):
    def plan(refs, send_sems, recv_sems):
        x, y, c = lax.axis_index("x"), lax.axis_index("y"), lax.axis_index("c")
        me = 2 * x + y
        copies = []
        for a in range(n):
            for rel in _GROUPS["xy"]:
                px, py, _ = _flip(rel, x, y, c)
                pairs = list(zip(_pieces(refs[a], nchunks[a]), _pieces(refs[n + a].at[me], nchunks[a])))
                copies += _numbered(pairs, (px, py, c), send_sems, recv_sems, len(copies))
        return copies
    return plan, 3 * sum(nchunks)


def _all_plan():
    def plan(refs, send_sems, recv_sems):
        x, y, c = lax.axis_index("x"), lax.axis_index("y"), lax.axis_index("c")
        me = 4 * x + 2 * y + c
        copies = []
        for rel in _GROUPS["xyc"]:
            copies += _numbered([(refs[0], refs[1].at[me])], _flip(rel, x, y, c), send_sems, recv_sems, len(copies))
        return copies
    return plan, len(_GROUPS["xyc"])


def _pair_plan(n, nchunks):
    def plan(refs, send_sems, recv_sems):
        x, y, c = lax.axis_index("x"), lax.axis_index("y"), lax.axis_index("c")
        copies = []
        for a in range(n):
            for j in range(N_CHIPS):
                pairs = list(zip(_pieces(refs[a].at[j, 1 - c], nchunks[a]), _pieces(refs[n + a].at[j], nchunks[a])))
                copies += _numbered(pairs, (x, y, 1 - c), send_sems, recv_sems, len(copies))
        return copies
    return plan, N_CHIPS * sum(nchunks)


def _chips_plan(n, nchunks):
    def plan(refs, send_sems, recv_sems):
        x, y, c = lax.axis_index("x"), lax.axis_index("y"), lax.axis_index("c")
        me = 2 * x + y
        copies = []
        for a in range(n):
            for rel in _GROUPS["xy"]:
                px, py, _ = _flip(rel, x, y, c)
                pairs = list(zip(_pieces(refs[a].at[2 * px + py], nchunks[a]), _pieces(refs[n + a].at[me], nchunks[a])))
                copies += _numbered(pairs, (px, py, c), send_sems, recv_sems, len(copies))
        return copies
    return plan, 3 * sum(nchunks)


def _fill_plan(n, nchunks, l):
    def plan(refs, send_sems, recv_sems):
        x, y, c = lax.axis_index("x"), lax.axis_index("y"), lax.axis_index("c")
        copies = []
        for a in range(n):
            blk = _pieces(refs[a].at[l, c], nchunks[a])
            copies += _numbered(list(zip(blk, blk)), (x, y, 1 - c), send_sems, recv_sems, len(copies))
        return copies
    return plan, sum(nchunks)


def _split_start(name, arrays, plan, nsem, after):
    n = len(arrays)
    na = len(after)

    def body(*refs):
        send_sems, recv_sems = refs[n + na], refs[n + na + 1]
        token = refs[-1]
        for cp in plan(refs[:n], send_sems, recv_sems):
            cp.start()
        token[...] = jnp.zeros_like(token)

    hbm = [pltpu.HBM(a.shape, a.dtype) for a in arrays]
    outs = pl.pallas_call(
        body, name=name,
        out_shape=(pltpu.SemaphoreType.DMA((nsem,)), pltpu.SemaphoreType.DMA((nsem,)), *hbm, _sds((8, 128))),
        in_specs=[_HBM] * n + [pl.BlockSpec(memory_space=pl.ANY)] * na,
        out_specs=(_SEM, _SEM, *([_HBM] * n), pl.BlockSpec(memory_space=pltpu.VMEM)),
        input_output_aliases={i: 2 + i for i in range(n)},
        compiler_params=pltpu.CompilerParams(has_side_effects=_EFFECT),
    )(*[pltpu.with_memory_space_constraint(a, pltpu.HBM) for a in arrays], *after)
    return outs[0], outs[1], list(outs[2:2 + n]), outs[-1]


def _split_wait(name, send_sems, recv_sems, arrays, plan, after):
    n = len(arrays)

    def body(*refs):
        for cp in plan(refs[:n], refs[n], refs[n + 1]):
            cp.wait_send()
            cp.wait_recv()

    hbm = [pltpu.HBM(a.shape, a.dtype) for a in arrays]
    return list(pl.pallas_call(
        body, name=name, out_shape=tuple(hbm),
        in_specs=[_HBM] * n + [_SEM, _SEM] + [pl.BlockSpec(memory_space=pl.ANY)] * len(after),
        out_specs=tuple([_HBM] * n), input_output_aliases={i: i for i in range(n)},
        compiler_params=pltpu.CompilerParams(has_side_effects=_EFFECT),
    )(*arrays, send_sems, recv_sems, *after))


def _mm(name, a, b, out_sds, *, grid, a_spec, b_spec, o_spec, dims, acc_shape, into=None):
    nk = grid[2]

    def body(*refs):
        a_ref, b_ref = refs[0], refs[1]
        o_ref, acc = refs[-2], refs[-1]
        k = pl.program_id(2)
        part = lax.dot_general(a_ref[...].astype(_MXU), b_ref[...].astype(_MXU), dims,
                               preferred_element_type=_F32)
        if nk == 1:
            o_ref[...] = part.astype(o_ref.dtype)
            return

        @pl.when(k == 0)
        def _():
            acc[...] = part

        @pl.when(k > 0)
        def _():
            acc[...] += part

        @pl.when(k == nk - 1)
        def _():
            o_ref[...] = acc[...].astype(o_ref.dtype).reshape(o_ref.shape)

    if nk == 1:
        acc_shape = (8, 128)
    in_specs = [a_spec, b_spec]
    args = [a, b]
    aliases = {}
    if into is not None:
        in_specs.append(pl.BlockSpec(memory_space=pl.ANY))
        args.append(into)
        aliases = {2: 0}
    return pl.pallas_call(
        body, name=name, grid=grid, in_specs=in_specs, out_specs=o_spec, out_shape=out_sds,
        scratch_shapes=[pltpu.VMEM(acc_shape, _F32)], input_output_aliases=aliases,
        compiler_params=_params(3))(*args)


_NN = (((1,), (0,)), ((), ()))
_NT = (((1,), (1,)), ((), ()))
_TN = (((0,), (0,)), ((), ()))


def _rowwise(name, body, *, grid, ins, outs, scratch=()):
    return pl.pallas_call(
        body, name=name, grid=(grid,), in_specs=[s for _, s in ins], out_specs=[s for _, s in outs],
        out_shape=[o for o, _ in outs], scratch_shapes=list(scratch),
        compiler_params=_params(1))(*[a for a, _ in ins])


def _rows(tb, w, cb=0, n=None):
    if n is None:
        return pl.BlockSpec((tb, w), lambda i: (i, cb))
    return pl.BlockSpec((tb, w), lambda i: (n - 1 - i, cb))


def _vec(shape):
    return pl.BlockSpec(shape, lambda i: (0,) * len(shape))


def _halo_prev(tb, w, cb=0, n=None, rows=8):
    if n is None:
        return pl.BlockSpec((rows, w), lambda i: (jnp.maximum(i * (tb // rows) - 1, 0), cb))
    return pl.BlockSpec((rows, w), lambda i: (jnp.maximum((n - 1 - i) * (tb // rows) - 1, 0), cb))


def _halo_next(tb, w, n, cb=0):
    return pl.BlockSpec((8, w), lambda i: (jnp.minimum((i + 1) * (tb // 8), n * (tb // 8) - 1), cb))


def _sds(shape, dtype=_F32):
    return jax.ShapeDtypeStruct(shape, dtype)


def _cast(name, a, tb):
    rows, cols = a.shape

    def body(a_ref, o_ref):
        o_ref[...] = a_ref[...].astype(o_ref.dtype)

    return _rowwise(name, body, grid=rows // tb, ins=[(a, _rows(tb, cols))],
                    outs=[(_sds((rows, cols), _MXU), _rows(tb, cols))])[0]


def _sum_lead(name, a, tb):
    g, rows, cols = a.shape

    def body(a_ref, o_ref):
        acc = a_ref[0]
        for k in range(1, g):
            acc = acc + a_ref[k]
        o_ref[...] = acc

    return _rowwise(name, body, grid=rows // tb,
                    ins=[(a, pl.BlockSpec((g, tb, cols), lambda i: (0, i, 0)))],
                    outs=[(_sds((rows, cols)), _rows(tb, cols))])[0]


def _sum_pair(name, mine, theirs, core, tb):
    nj, _, rows, cols = mine.shape

    def body(s_ref, a_ref, b_ref, o_ref, ob_ref):
        t = a_ref[...] + b_ref[...]
        o_ref[...] = t
        ob_ref[...] = t.astype(ob_ref.dtype)

    blk = pl.BlockSpec((None, tb, cols), lambda j, i, s: (j, i, 0))
    grid_spec = pltpu.PrefetchScalarGridSpec(
        num_scalar_prefetch=1, grid=(nj, rows // tb),
        in_specs=[pl.BlockSpec((None, None, tb, cols), lambda j, i, s: (j, s[0], i, 0)), blk],
        out_specs=[blk, blk])
    return pl.pallas_call(body, name=name, grid_spec=grid_spec,
                          out_shape=[_sds((nj, rows, cols)), _sds((nj, rows, cols), _MXU)],
                          compiler_params=_params(2))(core, mine, theirs)


def _sum_chips(name, mine, theirs, where, l, into, tb):
    _, rows, cols = mine.shape
    extra = [] if into is None else [into]

    def body(*refs):
        a_ref, b1_ref, b2_ref, b3_ref = refs[1:5]
        o_ref = refs[-1]
        o_ref[...] = ((a_ref[...] + b1_ref[...].astype(_F32)) + b2_ref[...].astype(_F32)) + b3_ref[...].astype(_F32)

    def slot(k):
        return pl.BlockSpec((None, tb, cols), lambda i, s: (jnp.bitwise_xor(s[0], k), i, 0))

    grid_spec = pltpu.PrefetchScalarGridSpec(
        num_scalar_prefetch=1, grid=(rows // tb,),
        in_specs=[slot(0), slot(1), slot(2), slot(3)] + [pl.BlockSpec(memory_space=pl.ANY)] * len(extra),
        out_specs=pl.BlockSpec((None, None, tb, cols), lambda i, s: (l, s[1], i, 0)))
    return pl.pallas_call(body, name=name, grid_spec=grid_spec, out_shape=_sds((2, 2, rows, cols)),
                          input_output_aliases={5: 0} if extra else {},
                          compiler_params=_params(1))(where, mine, theirs, theirs, theirs, *extra)


def _adamw(name, w, g, m, v, tb):
    rows, cols = w.shape
    c1 = 1.0 - ADAM_B1 ** ADAM_STEP
    c2 = 1.0 - ADAM_B2 ** ADAM_STEP

    def body(w_ref, g_ref, m_ref, v_ref, d_ref, nm_ref, nv_ref):
        gv = g_ref[...]
        nm = ADAM_B1 * m_ref[...] + (1.0 - ADAM_B1) * gv
        nv = ADAM_B2 * v_ref[...] + (1.0 - ADAM_B2) * (gv * gv)
        d_ref[...] = -ADAM_LR * ((nm / c1) / (jnp.sqrt(nv / c2) + ADAM_EPS) + ADAM_WD * w_ref[...])
        nm_ref[...] = nm
        nv_ref[...] = nv

    spec = _rows(tb, cols)
    return _rowwise(name, body, grid=rows // tb, ins=[(w, spec), (g, spec), (m, spec), (v, spec)],
                    outs=[(_sds((rows, cols)), spec)] * 3)


def _mod_fwd(c_all, w_mod, b_cols):
    cols = w_mod.shape[2]

    def body(c_ref, w_ref, b_ref, o_ref):
        cv = c_ref[...]
        sc = (cv * _sigmoid(cv)).astype(_MXU)
        o_ref[...] = jnp.dot(sc, w_ref[...].astype(_MXU), preferred_element_type=_F32) + b_ref[...]

    return pl.pallas_call(
        body, name="mod_fwd", grid=(2,),
        in_specs=[pl.BlockSpec((8, D), lambda l: (0, 0)), pl.BlockSpec((None, D, cols), lambda l: (l, 0, 0)),
                  pl.BlockSpec((None, 1, cols), lambda l: (l, 0, 0))],
        out_specs=pl.BlockSpec((None, 8, cols), lambda l: (l, 0, 0)),
        out_shape=_sds((2, 8, cols)), compiler_params=_params(1))(c_all, w_mod, b_cols)


def _mod_bwd(c_all_t, dm):
    cols = dm.shape[2]

    def body(c_ref, d_ref, o_ref):
        cv = c_ref[...]
        sc = (cv * _sigmoid(cv)).astype(_MXU)
        o_ref[...] = jnp.dot(sc, d_ref[...].astype(_MXU), preferred_element_type=_F32)

    return pl.pallas_call(
        body, name="mod_bwd", grid=(2,),
        in_specs=[pl.BlockSpec((D, 8), lambda l: (0, 0)), pl.BlockSpec((None, 8, cols), lambda l: (l, 0, 0))],
        out_specs=pl.BlockSpec((None, D, cols), lambda l: (l, 0, 0)),
        out_shape=_sds((2, D, cols)), compiler_params=_params(1))(c_all_t, dm)


def _proj(x, g_pre, shift, scale, w_in):
    s = x.shape[0]
    tm = 1024

    def body(x_ref, g_ref, sh_ref, sc_ref, w_ref, o_ref, ht_ref, h_s):
        @pl.when(pl.program_id(1) == 0)
        def _():
            xv = x_ref[...]
            rstd = lax.rsqrt(jnp.mean(xv * xv, axis=-1, keepdims=True) + NORM_EPS)
            hv = (xv * rstd) * g_ref[...] * (1.0 + sc_ref[...]) + sh_ref[...]
            h_s[...] = hv.astype(h_s.dtype)
            ht_ref[...] = hv.T.astype(ht_ref.dtype)

        o_ref[...] = jnp.dot(h_s[...], w_ref[...], preferred_element_type=_F32).astype(o_ref.dtype)

    vec = pl.BlockSpec((1, D), lambda m, n: (0, 0))
    return pl.pallas_call(
        body, name="proj", grid=(s // tm, N_CHIPS),
        in_specs=[pl.BlockSpec((tm, D), lambda m, n: (m, 0)), vec, vec, vec,
                  pl.BlockSpec((None, D, 2304), lambda m, n: (n, 0, 0))],
        out_specs=[pl.BlockSpec((tm, 2304), lambda m, n: (m, n)), pl.BlockSpec((D, tm), lambda m, n: (0, m))],
        out_shape=[_sds((s, IN_W), _MXU), _sds((D, s), _MXU)],
        scratch_shapes=[pltpu.VMEM((tm, D), _MXU)], compiler_params=_params(2))(x, g_pre, shift, scale, w_in)


def _shift_down(cur, halo, j, tb):
    ext = jnp.concatenate([halo, cur], axis=0)
    return pltpu.roll(ext, j, 0)[8:8 + tb]


def _shift_up(cur, halo, j, tb):
    ext = jnp.concatenate([cur, halo], axis=0)
    return pltpu.roll(ext, tb + 8 - j, 0)[0:tb]


def _conv(u_ref, halo_ref, w_ref, b_ref, first, tb):
    u = u_ref[...].astype(_F32)
    halo = jnp.where(first, 0.0, halo_ref[...].astype(_F32)[8:16])
    acc = b_ref[...] + u * w_ref[0:1, :]
    for j in range(1, 4):
        acc = acc + _shift_down(u, halo, j, tb) * w_ref[j:j + 1, :]
    return acc


def _lru_gates(pre_r, pre_i, uc, b_rg, b_ig, lam):
    r = _sigmoid(pre_r + b_rg)
    ig = _sigmoid(pre_i + b_ig)
    nl = -lam
    sp = jnp.maximum(nl, 0.0) + jnp.log(1.0 + jnp.exp(-jnp.abs(nl)))
    la = -LRU_C * r * sp
    a = jnp.exp(la)
    one_m_a2 = -jnp.tanh(la) * (a * a + 1.0)
    inv_sq = lax.rsqrt(jnp.maximum(one_m_a2, 1e-30))
    return r, ig, sp, a, one_m_a2 * inv_sq, inv_sq


GATE_TILES = 8


def _gate_tiles(w_rg, w_ig):
    eye = jnp.eye(2, dtype=w_rg.dtype)

    def tiles(w):
        return jnp.einsum("cpij,pq->cpiqj", w.reshape(GATE_TILES, 2, 64, 64), eye).reshape(GATE_TILES, 128, 128)

    return jnp.concatenate([tiles(w_rg), tiles(w_ig)], axis=2)


def _gate_tile_grads(gw):
    keep = jnp.eye(2, dtype=jnp.bool_)[None, :, None, :, None]

    def blocks(t):
        t5 = t.reshape(GATE_TILES, 2, 64, 2, 64)
        return jnp.sum(jnp.where(keep, t5, 0.0), axis=3).reshape(16, 64, 64)

    return blocks(gw[:, :, 0:128]), blocks(gw[:, :, 128:256])


def _gate_preacts(ucv, wt_ref):
    ucb = ucv.astype(_MXU)
    ps = [jnp.dot(ucb[:, 128 * c:128 * (c + 1)], wt_ref[c], preferred_element_type=_F32) for c in range(GATE_TILES)]
    pre_r = jnp.concatenate([p[:, 0:128] for p in ps], axis=1)
    pre_i = jnp.concatenate([p[:, 128:256] for p in ps], axis=1)
    return pre_r, pre_i


def _scan_fwd(proj, conv_w, conv_b, wt, b_rg, b_ig, lam):
    s = proj.shape[0]
    tb = 256

    def body(u_ref, up_ref, cw_ref, cb_ref, wt_ref, brg_ref, big_ref, lam_ref, h_ref, carry, a_s, b_s):
        i = pl.program_id(0)

        @pl.when(i == 0)
        def _():
            carry[...] = jnp.zeros_like(carry)

        ucv = _conv(u_ref, up_ref, cw_ref, cb_ref, i == 0, tb)
        pre_r, pre_i = _gate_preacts(ucv, wt_ref)
        _, ig, _, a, sq, _ = _lru_gates(pre_r, pre_i, ucv, brg_ref[...], big_ref[...], lam_ref[...])
        av = a
        bv = sq * (ig * ucv)
        av = av.reshape(tb // 8, 8, D)
        bv = bv.reshape(tb // 8, 8, D)
        row8 = lax.broadcasted_iota(jnp.int32, (1, 8, 1), 1)
        for sh in (1, 2, 4):
            m = row8 >= sh
            b_sh = pltpu.roll(bv, sh, 1)
            a_sh = pltpu.roll(av, sh, 1)
            bv = jnp.where(m, av * b_sh + bv, bv)
            av = jnp.where(m, av * a_sh, av)
        a_s[...] = av.reshape(tb, D)
        b_s[...] = bv.reshape(tb, D)

        def tile(t, state):
            rows = pl.ds(pl.multiple_of(t * 8, 8), 8)
            hv = b_s[rows, :] + a_s[rows, :] * state
            b_s[rows, :] = hv
            return jnp.broadcast_to(hv[7:8, :], (8, D))

        carry[...] = lax.fori_loop(0, tb // 8, tile, jnp.broadcast_to(carry[7:8, :], (8, D)), unroll=4)
        h_ref[...] = b_s[...].astype(h_ref.dtype)

    v = _vec((1, D))
    return _rowwise("scan_fwd", body, grid=s // tb,
                    ins=[(proj, _rows(tb, D, CB_U)), (proj, _halo_prev(tb, D, CB_U, rows=16)),
                         (conv_w, _vec((4, D))), (conv_b, v), (wt, _vec((GATE_TILES, 128, 256))),
                         (b_rg, v), (b_ig, v), (lam, v)],
                    outs=[(_sds((s, D), _MXU), _rows(tb, D))],
                    scratch=[pltpu.VMEM((8, D), _F32), pltpu.VMEM((tb, D), _F32), pltpu.VMEM((tb, D), _F32)])[0]


def _weight_specs(l):
    return [pl.BlockSpec((N_CHIPS, None, ATT_W, 256), lambda i: (0, l, 0, 0)),
            pl.BlockSpec((N_CHIPS, None, 256, D), lambda i: (0, l, 0, 0)),
            pl.BlockSpec((N_CHIPS, None, 256, D), lambda i: (0, l, 0, 0))]


def _tail_fwd(l, o, h_lru, proj, x, gate, g_post, gw, target):
    s = x.shape[0]
    tb = 512

    def body(*refs):
        o_ref, h_ref, ga_ref, gl_ref, ma_ref, mb_ref, x_ref, gt_ref, gp_ref, wpa_ref, wpb_ref, wo_ref = refs[0:12]
        aa_ref, ba_ref, ya_ref, yb_ref, z_ref, out_ref = refs[-8:-2] if target is not None else refs[-7:-1]
        ga = ga_ref[...].astype(_F32)
        aa32 = o_ref[...].astype(_F32) * (ga * _sigmoid(ga))
        aa = aa32.astype(_MXU)
        aa_ref[...] = aa32.T.astype(aa_ref.dtype)
        gl = gl_ref[...].astype(_F32)
        ba32 = h_ref[...].astype(_F32) * (gl * _sigmoid(gl))
        ba = ba32.astype(_MXU)
        ba_ref[...] = ba32.T.astype(ba_ref.dtype)
        ya = jnp.concatenate([jnp.dot(aa, wpa_ref[j], preferred_element_type=_F32) for j in range(N_CHIPS)], axis=1)
        ya_ref[...] = ya.astype(ya_ref.dtype)
        yb = jnp.dot(ba, wpb_ref[...].reshape(D, D), preferred_element_type=_F32)
        yb_ref[...] = yb.astype(yb_ref.dtype)
        z32 = _sigmoid(ma_ref[...].astype(_F32)) * ya + _sigmoid(mb_ref[...].astype(_F32)) * yb
        z = z32.astype(_MXU)
        z_ref[...] = z32.T.astype(z_ref.dtype)
        ov = jnp.dot(z, wo_ref[...].reshape(D, D), preferred_element_type=_F32)
        out_ref[...] = ov.astype(out_ref.dtype)
        rstd = lax.rsqrt(jnp.mean(ov * ov, axis=-1, keepdims=True) + NORM_EPS)
        xn =x_ref[...] + gt_ref[...] * ((ov * rstd) * gp_ref[...])
        if target is None:
            refs[-1][...] = xn
        else:
            dy_ref, acc_ref = refs[-2], refs[-1]
            err = xn - refs[12][...]
            dy_ref[...] = err * (1.0 / D)
            _zero_first(pl.program_id(0), acc_ref)
            acc_ref[...] += jnp.sum(err * err, axis=0, keepdims=True)

    v = _vec((1, D))
    r = _rows(tb, D)
    r5 = _rows(tb, ATT_W)
    weights = list(zip((gw["w_pa"], gw["w_pb"], gw["w_o"]), _weight_specs(l)))
    cols = pl.BlockSpec((D, tb), lambda i: (0, i))
    head_in = [] if target is None else [(target, r)]
    head_out = [] if target is None else [(_sds((1, D)), v)]
    return _rowwise("tail_fwd" if target is None else "tail_loss_fwd", body, grid=s // tb,
                    ins=[(o, r5), (h_lru, r), (proj, _rows(tb, ATT_W, CB_GATT)), (proj, _rows(tb, D, CB_GLRU)),
                         (proj, _rows(tb, D, CB_MA)), (proj, _rows(tb, D, CB_MB)), (x, r), (gate, v), (g_post, v)]
                    + weights + head_in,
                    outs=[(_sds((ATT_W, s), _MXU), pl.BlockSpec((ATT_W, tb), lambda i: (0, i))),
                          (_sds((D, s), _MXU), cols), (_sds((s, D), _MXU), r), (_sds((s, D), _MXU), r),
                          (_sds((D, s), _MXU), cols), (_sds((s, D), _MXU), r), (_sds((s, D)), r)]
                    + head_out)


def _zero_first(i, *refs):
    @pl.when(i == 0)
    def _():
        for ref in refs:
            ref[...] = jnp.zeros_like(ref)


def _tail_bwd(l, dx, out, y_a, y_b, proj, o, h_lru, gate, g_post, gw):
    s = dx.shape[0]
    tb = 256

    def body(dx_ref, out_ref, ya_ref, yb_ref, ma_ref, mb_ref, o_ref, ga_ref, h_ref, gl_ref, gt_ref, gp_ref,
             wpa_ref, wpb_ref, wo_ref,
             dout_ref, dya_ref, dyb_ref, rest_ref, do_ref, dh_ref, dgt_ref, dgp_ref):
        i = pl.program_id(0)
        ov = out_ref[...].astype(_F32)
        dxv = dx_ref[...]
        rstd = lax.rsqrt(jnp.mean(ov * ov, axis=-1, keepdims=True) + NORM_EPS)
        nv = ov * rstd
        s_dn = jnp.sum(dxv * nv, axis=0, keepdims=True)
        _zero_first(i, dgt_ref, dgp_ref)
        dgt_ref[...] += s_dn * gp_ref[...]
        dgp_ref[...] += s_dn * gt_ref[...]
        dn = dxv * (gt_ref[...] * gp_ref[...])
        d_out = (rstd * (dn - nv * jnp.mean(dn * nv, axis=-1, keepdims=True))).astype(_MXU)
        dout_ref[...] = d_out
        dz = lax.dot_general(d_out, wo_ref[...].reshape(D, D), _NT, preferred_element_type=_F32)
        ga = _sigmoid(ma_ref[...].astype(_F32))
        gb = _sigmoid(mb_ref[...].astype(_F32))
        dya = (dz * ga).astype(_MXU)
        dyb = (dz * gb).astype(_MXU)
        dya_ref[...] = dya
        dyb_ref[...] = dyb
        rest_ref[:, R_MA:R_MB] = (dz * ya_ref[...].astype(_F32) * ga * (1.0 - ga)).astype(rest_ref.dtype)
        rest_ref[:, R_MB:R_END] = (dz * yb_ref[...].astype(_F32) * gb * (1.0 - gb)).astype(rest_ref.dtype)
        daa = lax.dot_general(dya[:, 0:256], wpa_ref[0], _NT, preferred_element_type=_F32)
        for j in range(1, N_CHIPS):
            daa = daa + lax.dot_general(dya[:, j * 256:(j + 1) * 256], wpa_ref[j], _NT, preferred_element_type=_F32)
        dba = lax.dot_general(dyb, wpb_ref[...].reshape(D, D), _NT, preferred_element_type=_F32)
        gav = ga_ref[...].astype(_F32)
        sa = _sigmoid(gav)
        do_ref[...] = daa * (gav * sa)
        rest_ref[:, 0:R_U] = (daa * o_ref[...].astype(_F32) * (sa * (1.0 + gav * (1.0 - sa)))).astype(rest_ref.dtype)
        gl = gl_ref[...].astype(_F32)
        sl = _sigmoid(gl)
        dh_ref[...] = dba * (gl * sl)
        rest_ref[:, R_GLRU:R_MA] = (dba * h_ref[...].astype(_F32)
                                    * (sl * (1.0 + gl * (1.0 - sl)))).astype(rest_ref.dtype)

    v = _vec((1, D))
    r5, r10 = _rows(tb, ATT_W), _rows(tb, D)
    return _rowwise("tail_bwd", body, grid=s // tb,
                    ins=[(dx, r10), (out, r10), (y_a, r10), (y_b, r10), (proj, _rows(tb, D, CB_MA)),
                         (proj, _rows(tb, D, CB_MB)), (o, r5), (proj, _rows(tb, ATT_W, CB_GATT)), (h_lru, r10),
                         (proj, _rows(tb, D, CB_GLRU)), (gate, v), (g_post, v)]
                    + list(zip((gw["w_pa"], gw["w_pb"], gw["w_o"]), _weight_specs(l))),
                    outs=[(_sds((s, D), _MXU), r10), (_sds((s, D), _MXU), r10), (_sds((s, D), _MXU), r10),
                          (_sds((s, R_END), _MXU), _rows(tb, R_END)),
                          (_sds((s, ATT_W)), r5), (_sds((s, D)), r10), (_sds((1, D)), v), (_sds((1, D)), v)])


def _scan_bwd(dh, proj, conv_w, conv_b, h_lru, wt, b_rg, b_ig, lam):
    s = dh.shape[0]
    tb = 256
    n = s // tb

    def body(dh_ref, u_ref, up_ref, cw_ref, cb_ref, h_ref, hp_ref, wt_ref, brg_ref, big_ref, lam_ref,
             duc_ref, dwt_ref, dbrg_ref, dbig_ref, dlam_ref, carry, c_s, g_s):
        i = pl.program_id(0)

        @pl.when(i == 0)
        def _():
            carry[...] = jnp.zeros_like(carry)
            for acc_ref in (dwt_ref, dbrg_ref, dbig_ref, dlam_ref):
                acc_ref[...] = jnp.zeros_like(acc_ref)

        ucv = _conv(u_ref, up_ref, cw_ref, cb_ref, i == n - 1, tb)
        pre_r, pre_i = _gate_preacts(ucv, wt_ref)
        r, ig, sp, a, sq, inv_sq =_lru_gates(pre_r, pre_i, ucv, brg_ref[...], big_ref[...], lam_ref[...])
        row = lax.broadcasted_iota(jnp.int32, (tb, 1), 0)
        cv = jnp.where(row == tb - 1, 1.0, pltpu.roll(a, tb - 1, 0))
        gv = dh_ref[...]
        cv = cv.reshape(tb // 8, 8, D)
        gv = gv.reshape(tb // 8, 8, D)
        row8 = lax.broadcasted_iota(jnp.int32, (1, 8, 1), 1)
        for sh in (1, 2, 4):
            m = row8 < 8 - sh
            g_sh = pltpu.roll(gv, 8 - sh, 1)
            c_sh = pltpu.roll(cv, 8 - sh, 1)
            gv = jnp.where(m, gv + cv * g_sh, gv)
            cv = jnp.where(m, cv * c_sh, cv)
        c_s[...] = cv.reshape(tb, D)
        g_s[...] = gv.reshape(tb, D)

        def tile(k, state):
            rows = pl.ds(pl.multiple_of((tb // 8 - 1 - k) * 8, 8), 8)
            gt = g_s[rows, :] + c_s[rows, :] * state
            g_s[rows, :] = gt
            return jnp.broadcast_to(gt[0:1, :], (8, D))

        lax.fori_loop(0, tb // 8, tile, jnp.broadcast_to(carry[0:1, :], (8, D)), unroll=4)
        gv = g_s[...]
        carry[...] = (a * gv)[0:8]

        halo = jnp.where(i < n - 1, hp_ref[...].astype(_F32)[8:16], 0.0)
        h_prev = _shift_down(h_ref[...].astype(_F32), halo, 1, tb)
        d_a = gv * h_prev
        d_sq = gv * (ig * ucv)
        d_i = gv * sq * ucv
        d_la = d_a * a - d_sq * (a * a) * inv_sq
        d_r = d_la * (-LRU_C * sp)
        d_pre_r = d_r * r * (1.0 - r)
        d_pre_i = d_i * ig * (1.0 - ig)
        ucb = ucv.astype(_MXU)
        dpr = d_pre_r.astype(_MXU)
        dpi = d_pre_i.astype(_MXU)
        back = []
        for c in range(GATE_TILES):
            lanes = slice(128 * c, 128 * (c + 1))
            dp = jnp.concatenate([dpr[:, lanes], dpi[:, lanes]], axis=1)
            back.append(lax.dot_general(dp, wt_ref[c], _NT, preferred_element_type=_F32))
            dwt_ref[c] += lax.dot_general(ucb[:, lanes], dp, _TN, preferred_element_type=_F32)
        duc_ref[...] = gv * sq * ig + jnp.concatenate(back, axis=1)
        dbrg_ref[...] += jnp.sum(d_pre_r, axis=0, keepdims=True)
        dbig_ref[...] += jnp.sum(d_pre_i, axis=0, keepdims=True)
        lamv = lam_ref[...]
        dlam_ref[...] += jnp.sum(d_la * (-LRU_C * r), axis=0, keepdims=True) * (-_sigmoid(-lamv))

    v = _vec((1, D))
    rv = _rows(tb, D, 0, n)
    return _rowwise("scan_bwd", body, grid=n,
                    ins=[(dh, rv), (proj, _rows(tb, D, CB_U, n)), (proj, _halo_prev(tb, D, CB_U, n, rows=16)),
                         (conv_w, _vec((4, D))), (conv_b, v), (h_lru, rv), (h_lru, _halo_prev(tb, D, 0, n, rows=16)),
                         (wt, _vec((GATE_TILES, 128, 256))), (b_rg, v), (b_ig, v), (lam, v)],
                    outs=[(_sds((s, D)), rv), (_sds((GATE_TILES, 128, 256)), _vec((GATE_TILES, 128, 256))),
                          (_sds((1, D)), v), (_sds((1, D)), v), (_sds((1, D)), v)],
                    scratch=[pltpu.VMEM((8, D), _F32), pltpu.VMEM((tb, D), _F32), pltpu.VMEM((tb, D), _F32)])


def _conv_bwd(duc_a, proj, conv_w, rest):
    s = duc_a.shape[0]
    tb = 512
    n = s // tb
    hw = D // 2

    def body(da_ref, dan_ref, u_ref, up_ref, w_ref, rest_in, du_ref, dw_ref, dbias_ref):
        i = pl.program_id(1)
        duc = da_ref[...]
        nxt = jnp.where(i < n - 1, dan_ref[...], 0.0)
        u = u_ref[...].astype(_F32)
        halo = jnp.where(i > 0, up_ref[...].astype(_F32)[8:16], 0.0)
        du = duc * w_ref[0:1, :]
        dws = [jnp.sum(duc * u, axis=0, keepdims=True)]
        for j in range(1, 4):
            du = du + _shift_up(duc, nxt, j, tb) * w_ref[j:j + 1, :]
            dws.append(jnp.sum(duc * _shift_down(u, halo, j, tb), axis=0, keepdims=True))
        du_ref[...] = du.astype(du_ref.dtype)
        _zero_first(i, dw_ref, dbias_ref)
        for j in range(4):
            dw_ref[j:j + 1, :] += dws[j]
        dbias_ref[...] += jnp.sum(duc, axis=0, keepdims=True)

    r = pl.BlockSpec((tb, hw), lambda h, i: (i, h))
    nxt_spec = pl.BlockSpec((8, hw), lambda h, i: (jnp.minimum((i + 1) * (tb // 8), n * (tb // 8) - 1), h))
    return pl.pallas_call(
        body, name="conv_bwd", grid=(2, n),
        in_specs=[r, nxt_spec,
                  pl.BlockSpec((tb, hw), lambda h, i: (i, 2 * CB_U + h)),
                  pl.BlockSpec((16, hw), lambda h, i: (jnp.maximum(i * (tb // 16) - 1, 0), 2 * CB_U + h)),
                  pl.BlockSpec((4, hw), lambda h, i: (0, h)), pl.BlockSpec(memory_space=pl.ANY)],
        out_specs=[pl.BlockSpec((tb, hw), lambda h, i: (i, R_U // hw + h)),
                   pl.BlockSpec((4, hw), lambda h, i: (0, h)), pl.BlockSpec((1, hw), lambda h, i: (0, h))],
        out_shape=[_sds(rest.shape, rest.dtype), _sds((4, D)), _sds((1, D))],
        input_output_aliases={5: 0}, compiler_params=_params(2),
    )(duc_a, duc_a, proj, proj, conv_w, rest)


def _band_tiles(dil):
    tiles = []
    for rho in range(dil):
        for b in range(16 // dil):
            qs = rho + dil * BAND * b
            tiles.append((qs, QBLK + qs - dil * BAND, b))
    return tiles


def _strided(start, size, dil):
    return pl.ds(start, size, stride=dil) if dil > 1 else pl.ds(start, size)


def _band_mask(i, b):
    qi = lax.broadcasted_iota(jnp.int32, (BAND, 2 * BAND), 0)
    ki = lax.broadcasted_iota(jnp.int32, (BAND, 2 * BAND), 1)
    valid = (ki >= qi) & (ki <= qi + BAND)
    if b == 0:
        valid = valid & ((ki >= BAND) | (i > 0))
    return valid


def _attn_fwd(proj):
    s = proj.shape[0]
    n = s // QBLK
    scale = HEAD ** -0.5

    def body(*refs):
        q_refs, kp_refs, kc_refs, vp_refs, vc_refs = (refs[3 * t:3 * t + 3] for t in range(5))
        o_ref, lse_ref, qbuf, kbuf, vbuf = refs[15:20]
        accs, maxs, dens = refs[20:23], refs[23:26], refs[26:29]
        i = pl.program_id(1)
        for g, dil in enumerate(DILATIONS):
            qbuf[...] = q_refs[g][...].astype(_F32)
            kbuf[0:QBLK, :] = kp_refs[g][...].astype(_F32)
            kbuf[QBLK:2 * QBLK, :] = kc_refs[g][...].astype(_F32)
            vbuf[0:QBLK, :] = vp_refs[g][...].astype(_F32)
            vbuf[QBLK:2 * QBLK, :] = vc_refs[g][...].astype(_F32)
            for qs, ks, b in _band_tiles(dil):
                qsl = _strided(qs, BAND, dil)
                q = qbuf[qsl, :].astype(_MXU)
                kk = kbuf[_strided(ks, 2 * BAND, dil), :].astype(_MXU)
                vv = vbuf[_strided(ks, 2 * BAND, dil), :].astype(_MXU)
                sc = lax.dot_general(q, kk, _NT, preferred_element_type=_F32) * scale
                sc = jnp.where(_band_mask(i, b), sc, NEG_INF)
                m = jnp.max(sc, axis=-1, keepdims=True)
                p = jnp.exp(sc - m)
                accs[g][qsl, :] = jnp.dot(p.astype(_MXU), vv, preferred_element_type=_F32)
                maxs[g][qsl, :] = jnp.broadcast_to(m, (BAND, HEAD))
                dens[g][qsl, :] = jnp.broadcast_to(jnp.sum(p, axis=-1, keepdims=True), (BAND, HEAD))
        ms = [r[...] for r in maxs]
        mx = jnp.maximum(jnp.maximum(ms[0], ms[1]), ms[2])
        ws = [jnp.exp(m - mx) for m in ms]
        den = ws[0] * dens[0][...] + ws[1] * dens[1][...] + ws[2] * dens[2][...]
        o_ref[...] = ((ws[0] * accs[0][...] + ws[1] * accs[1][...] + ws[2] * accs[2][...]) / den).astype(o_ref.dtype)
        lse_ref[...] = mx + jnp.log(den)

    blk = (QBLK, HEAD)

    def spec(first_col, lag):
        specs = []
        for g in range(3):
            col = first_col + g * HEADS
            if lag:
                specs.append(pl.BlockSpec(blk, lambda j, i, col=col: (jnp.maximum(i - 1, 0), col + j)))
            else:
                specs.append(pl.BlockSpec(blk, lambda j, i, col=col: (i, col + j)))
        return specs

    out_spec = pl.BlockSpec(blk, lambda j, i: (i, j))
    return pl.pallas_call(
        body, name="attn_fwd", grid=(HEADS, n),
        in_specs=spec(0, False) + spec(12, True) + spec(12, False) + spec(24, True) + spec(24, False),
        out_specs=[out_spec] * 2, out_shape=[_sds((s, ATT_W), _MXU), _sds((s, ATT_W))],
        scratch_shapes=[pltpu.VMEM(blk, _F32)] + [pltpu.VMEM((2 * QBLK, HEAD), _F32)] * 2
        + [pltpu.VMEM(blk, _F32)] * 9,
        compiler_params=_params(2))(*([proj] * 15))


def _attn_bwd(proj, d_o, o, lse, g, into):
    s = proj.shape[0]
    dil = DILATIONS[g]
    n = s // QBLK
    scale = HEAD ** -0.5
    tiles = _band_tiles(dil)

    def body(*refs):
        q_ref, kp_ref, kc_ref, vp_ref, vc_ref, do_ref, o_ref, lse_ref = refs[0:8]
        dq_ref, dk_ref, dv_ref, kbuf, vbuf, dkbuf, dvbuf, dqbuf, qbuf, obuf = refs[-10:]
        i = pl.program_id(1)

        @pl.when(i == 0)
        def _():
            dkbuf[0:QBLK, :] = jnp.zeros((QBLK, HEAD), _F32)
            dvbuf[0:QBLK, :] = jnp.zeros((QBLK, HEAD), _F32)

        @pl.when(i < n)
        def _():
            qbuf[...] = q_ref[...].astype(_F32)
            obuf[...] = o_ref[...].astype(_F32)
            kbuf[0:QBLK, :] = kp_ref[...].astype(_F32)
            kbuf[QBLK:2 * QBLK, :] = kc_ref[...].astype(_F32)
            vbuf[0:QBLK, :] = vp_ref[...].astype(_F32)
            vbuf[QBLK:2 * QBLK, :] = vc_ref[...].astype(_F32)
            dkbuf[QBLK:2 * QBLK, :] = jnp.zeros((QBLK, HEAD), _F32)
            dvbuf[QBLK:2 * QBLK, :] = jnp.zeros((QBLK, HEAD), _F32)
            for qs, ks, b in tiles:
                qsl = _strided(qs, BAND, dil)
                ksl = _strided(ks, 2 * BAND, dil)
                q = qbuf[qsl, :].astype(_MXU)
                kk = kbuf[ksl, :].astype(_MXU)
                vv = vbuf[ksl, :].astype(_MXU)
                dov = do_ref[qsl, :]
                dd = jnp.sum(dov * obuf[qsl, :], axis=-1, keepdims=True)
                lse_t = lse_ref[qsl, :][:, 0:1]
                sc = lax.dot_general(q, kk, _NT, preferred_element_type=_F32) * scale
                p = jnp.where(_band_mask(i, b), jnp.exp(sc - lse_t), 0.0)
                dob = dov.astype(_MXU)
                dp = lax.dot_general(dob, vv, _NT, preferred_element_type=_F32)
                ds = (p * (dp - dd) * scale).astype(_MXU)
                dqbuf[qsl, :] = jnp.dot(ds, kk, preferred_element_type=_F32)
                dkbuf[ksl, :] += lax.dot_general(ds, q, _TN, preferred_element_type=_F32)
                dvbuf[ksl, :] += lax.dot_general(p.astype(_MXU), dob, _TN, preferred_element_type=_F32)
            dq_ref[...] = dqbuf[...].astype(dq_ref.dtype)

        dk_ref[...] = dkbuf[0:QBLK, :].astype(dk_ref.dtype)
        dv_ref[...] = dvbuf[0:QBLK, :].astype(dv_ref.dtype)
        dkbuf[0:QBLK, :] = dkbuf[QBLK:2 * QBLK, :]
        dvbuf[0:QBLK, :] = dvbuf[QBLK:2 * QBLK, :]

    blk = (QBLK, HEAD)
    cq, ck, cv = g * HEADS, 12 + g * HEADS, 24 + g * HEADS

    def cur(i):
        return jnp.minimum(i, n - 1)

    def prev(i):
        return jnp.maximum(jnp.minimum(i, n - 1) - 1, 0)

    own = pl.BlockSpec(blk, lambda j, i: (cur(i), j))
    own_out = pl.BlockSpec(blk, lambda j, i: (cur(i), cq + j))
    late_out = pl.BlockSpec(blk, lambda j, i: (jnp.maximum(i - 1, 0), cq + j))
    extra = [] if into is None else list(into)
    return pl.pallas_call(
        body, name="attn_bwd_d%d" % dil, grid=(HEADS, n + 1),
        in_specs=[pl.BlockSpec(blk, lambda j, i: (cur(i), cq + j)),
                  pl.BlockSpec(blk, lambda j, i: (prev(i), ck + j)),
                  pl.BlockSpec(blk, lambda j, i: (cur(i), ck + j)),
                  pl.BlockSpec(blk, lambda j, i: (prev(i), cv + j)),
                  pl.BlockSpec(blk, lambda j, i: (cur(i), cv + j)),
                  own, own, own] + [pl.BlockSpec(memory_space=pl.ANY)] * len(extra),
        out_specs=[own_out, late_out, late_out], out_shape=[_sds((s, QKV_W), _MXU)] * 3,
        input_output_aliases={8 + t: t for t in range(len(extra))},
        scratch_shapes=[pltpu.VMEM((2 * QBLK, HEAD), _F32)] * 4 + [pltpu.VMEM((QBLK, HEAD), _F32)] * 3,
        compiler_params=_params(2))(proj, proj, proj, proj, proj, d_o, o, lse, *extra)


_PARTS = ((0, 2), (2, 2), (4, 2), (6, 6))
_CHUNK = 768


def _d_x(name, parts, w_in, x, dx_out, g_pre, scale, blocks, into):
    s = parts[0].shape[0]
    nk = IN_W // _CHUNK
    first_block, n_blocks = blocks

    def body(*refs):
        p0, p1, p2, p3, w_ref, x_ref, dxo_ref, g_ref, sc_ref = refs[0:9]
        dx_ref, dsh_ref, dsc_ref, dg_ref, acc = refs[-5:]
        m = pl.program_id(0)
        k = pl.program_id(2)

        @pl.when(k == 0)
        def _():
            acc[...] = jnp.zeros_like(acc)

        @pl.when((k == 0) & (m == 0))
        def _():
            for ref in (dsh_ref, dsc_ref, dg_ref):
                ref[...] = jnp.zeros_like(ref)

        for p_ref, (first, cnt) in zip((p0, p1, p2, p3), _PARTS):
            @pl.when((k >= first) & (k < first + cnt))
            def _(p_ref=p_ref):
                acc[...] += lax.dot_general(p_ref[...].astype(_MXU), w_ref[...], _NT, preferred_element_type=_F32)

        @pl.when(k == nk - 1)
        def _():
            dhv = acc[...]
            xv = x_ref[...]
            rstd = lax.rsqrt(jnp.mean(xv * xv, axis=-1, keepdims=True) + NORM_EPS)
            xn = xv * rstd
            one_sc = 1.0 + sc_ref[...]
            s1 = jnp.sum(dhv * xn, axis=0, keepdims=True)
            dsh_ref[...] += jnp.sum(dhv, axis=0, keepdims=True)
            dsc_ref[...] += s1 * g_ref[...]
            dg_ref[...] += s1 * one_sc
            dxn = dhv * (g_ref[...] * one_sc)
            dx_ref[...] = dxo_ref[...] + rstd * (dxn - xn * jnp.mean(dxn * xn, axis=-1, keepdims=True))

    def part_spec(first, cnt):
        return pl.BlockSpec((1024, _CHUNK), lambda m, n, k: (first_block + m, jnp.clip(k - first, 0, cnt - 1)))

    rows = pl.BlockSpec((1024, D), lambda m, n, k: (first_block + m, 0))
    vec = pl.BlockSpec((1, D), lambda m, n, k: (0, 0))
    extra = [] if into is None else [into]
    return pl.pallas_call(
        body, name=name, grid=(n_blocks, 1, nk),
        in_specs=[part_spec(*p) for p in _PARTS]
        + [pl.BlockSpec((None, D, _CHUNK), lambda m, n, k: (k // 3, 0, k % 3)), rows, rows, vec, vec]
        + [pl.BlockSpec(memory_space=pl.ANY)] * len(extra),
        out_specs=[rows, vec, vec, vec], out_shape=[_sds((s, D)), _sds((1, D)), _sds((1, D)), _sds((1, D))],
        input_output_aliases={9: 0} if extra else {},
        scratch_shapes=[pltpu.VMEM((1024, D), _F32)], compiler_params=_params(3))(
            *parts, w_in, x, dx_out, g_pre, scale, *extra)


def _g_w_in(h_t, parts):
    s = h_t.shape[1]
    nk = s // 1024

    def body(*refs):
        h_ref, p_refs = refs[0], refs[1:5]
        o_ref, acc = refs[-2], refs[-1]
        n = pl.program_id(1)
        k = pl.program_id(2)

        @pl.when(k == 0)
        def _():
            acc[...] = jnp.zeros_like(acc)

        for p_ref, (first, cnt) in zip(p_refs, _PARTS):
            @pl.when((n >= first) & (n < first + cnt))
            def _(p_ref=p_ref):
                acc[...] += jnp.dot(h_ref[...], p_ref[...].astype(_MXU), preferred_element_type=_F32)

        @pl.when(k == nk - 1)
        def _():
            o_ref[...] = acc[...]

    def part_spec(first, cnt):
        def index(m, n, k):
            row = jnp.where(n < first, 0, jnp.where(n >= first + cnt, nk - 1, k))
            return (row, jnp.clip(n - first, 0, cnt - 1))
        return pl.BlockSpec((1024, _CHUNK), index)

    return pl.pallas_call(
        body, name="g_w_in", grid=(1, IN_W // _CHUNK, nk),
        in_specs=[pl.BlockSpec((D, 1024), lambda m, n, k: (0, k))] + [part_spec(*p) for p in _PARTS],
        out_specs=pl.BlockSpec((None, D, _CHUNK), lambda m, n, k: (n // 3, 0, n % 3)),
        out_shape=_sds((N_CHIPS, D, 2304)),
        scratch_shapes=[pltpu.VMEM((D, _CHUNK), _F32)], compiler_params=_params(3))(h_t, *parts)


def _layer_fwd(l, x, p, gw, late, target):
    if callable(gw["w_in"][l]):
        gw["w_in"][l] = gw["w_in"][l](x)
    proj, h_t = _proj(x, p["g_pre"], p["shift"], p["scale"], gw["w_in"][l])
    o, lse = _attn_fwd(proj)
    h_lru = _scan_fwd(proj, p["conv_w"], p["conv_b"], p["wt"], p["b_rg"], p["b_ig"], p["lam"])
    if late is not None:
        landed = dict(late(h_lru))
        gw["w_in"].append(landed.pop("w_in1"))
        gw.update(landed)
    a_att, b_act, y_a, y_b, z, out, *last = _tail_fwd(l, o, h_lru, proj, x, p["gate"], p["g_post"], gw, target)
    saved = dict(x=x, h_t=h_t, proj=proj, o=o, lse=lse, h_lru=h_lru, a_att=a_att, b_act=b_act,
                 y_a=y_a, y_b=y_b, z=z, out=out)
    return (last[0] if target is None else last), saved


def _layer_bwd(l, dx, p, gw, sv, hooks):
    s = dx.shape[0]
    nt = s // 2048
    proj = sv["proj"]
    gate, b_rg, g_pre = p["gate"], p["b_rg"], p["g_pre"]
    if hooks is not None:
        gate = gate + hooks[0]([dx])
    d_out, dy_a, dy_b, d_rest, d_o, dh_lru, d_gate, d_gpost = _tail_bwd(
        l, dx, sv["out"], sv["y_a"], sv["y_b"], proj, sv["o"], sv["h_lru"], gate, p["g_post"], gw)
    if hooks is not None:
        b_rg = b_rg + hooks[1]([d_out])

    def wgrad_rows(name, a, b):
        return _mm(name, a, b, _sds((N_CHIPS, 256, D)), grid=(1, 1, nt),
                   a_spec=pl.BlockSpec((D, 2048), lambda m, n, k: (0, k)),
                   b_spec=pl.BlockSpec((2048, D), lambda m, n, k: (k, 0)),
                   o_spec=pl.BlockSpec((N_CHIPS, 256, D), lambda m, n, k: (0, 0, 0)),
                   dims=_NN, acc_shape=(D, D))

    big = {}
    big["w_o"] = wgrad_rows("g_w_o", sv["z"], d_out)
    big["w_pa"] = _mm("g_w_pa", sv["a_att"], dy_a, _sds((N_CHIPS, ATT_W, 256)), grid=(1, 4, nt),
                      a_spec=pl.BlockSpec((ATT_W, 2048), lambda m, n, k: (0, k)),
                      b_spec=pl.BlockSpec((2048, 256), lambda m, n, k: (k, n)),
                      o_spec=pl.BlockSpec((None, ATT_W, 256), lambda m, n, k: (n, 0, 0)),
                      dims=_NN, acc_shape=(ATT_W, 256))
    big["w_pb"] = wgrad_rows("g_w_pb", sv["b_act"], dy_b)
    duc, g_wt, d_brg, d_big, d_lam = _scan_bwd(dh_lru, proj, p["conv_w"], p["conv_b"], sv["h_lru"], p["wt"], b_rg,
                                               p["b_ig"], p["lam"])
    g_wrg, g_wig = _gate_tile_grads(g_wt)
    d_rest, g_convw, g_convb = _conv_bwd(duc, proj, p["conv_w"], d_rest)
    dqkv = None
    for g in range(3):
        dqkv = _attn_bwd(proj, d_o, sv["o"], sv["lse"], g, dqkv)
    if hooks is not None:
        g_pre = g_pre + hooks[2]([dqkv[0]])
    parts = (dqkv[0], dqkv[1], dqkv[2], d_rest)
    big["w_in"] = _g_w_in(sv["h_t"], parts)
    nb = s // 1024
    if hooks is None:
        dx_in, d_shift, d_scale, d_gpre = _d_x("d_x", parts, gw["w_in"][l], sv["x"], dx, g_pre, p["scale"],
                                               (0, nb), None)
    else:
        first = _d_x("d_x_a", parts, gw["w_in"][l], sv["x"], dx, g_pre + hooks[3](big), p["scale"],
                     (0, nb // 2), None)
        second = _d_x("d_x_b", parts, gw["w_in"][l], sv["x"], dx, g_pre + hooks[4]([first[0]]), p["scale"],
                      (nb // 2, nb - nb // 2), first[0])
        dx_in = second[0]
        d_shift, d_scale, d_gpre = (a + b for a, b in zip(first[1:], second[1:]))
    small = dict(dmod=jnp.concatenate([d_shift, d_scale, d_gate], axis=1), g_pre=d_gpre, conv_w=g_convw,
                 conv_b=g_convb, w_rg=g_wrg, b_rg=d_brg, w_ig=g_wig, b_ig=d_big, lam=d_lam, g_post=d_gpost)
    return dx_in, small, big


_BIG = ("w_in", "w_pa", "w_pb", "w_o")


class _GradReduce:
    PAIR_CHUNKS = (2, 1, 1, 1)
    CHIP_CHUNKS = (2, 1, 1, 1)
    FILL_CHUNKS = (4, 1, 1, 1)

    def __init__(self, core, where):
        self.core, self.where = core, where
        self.finals = None

    def begin(self, l, big):
        n = len(_BIG)
        halves = [big[k].reshape(N_CHIPS, 2, big[k].shape[1] // 2, big[k].shape[2]) for k in _BIG]
        lands = [lax.empty((N_CHIPS,) + h.shape[2:], _F32) for h in halves]
        plan, nsem = _pair_plan(n, self.PAIR_CHUNKS)
        state = {}
        state["pair"] = _split_start("reduce_pair_start_%d" % l, halves + lands, plan, nsem, [])

        def started(after):
            return state["pair"][3][0, 0]

        def pair_done(after):
            send, recv, arrays, _ = state["pair"]
            arrays = _split_wait("reduce_pair_wait_%d" % l, send, recv, arrays, plan, after)
            sums = [_sum_pair("sum_pair_%s_%d" % (k, l), arrays[a], arrays[n + a], self.core, 128)
                    for a, k in enumerate(_BIG)]
            state["mine"] = [t[0] for t in sums]
            lands2 = [lax.empty(t[1].shape, _MXU) for t in sums]
            plan2, nsem2 = _chips_plan(n, self.CHIP_CHUNKS)
            state["plan2"] = plan2
            state["chips"] = _split_start("reduce_chips_start_%d" % l, [t[1] for t in sums] + lands2, plan2, nsem2, [])
            return state["chips"][3][0, 0]

        def chips_done(after):
            send, recv, arrays, _ = state["chips"]
            arrays = _split_wait("reduce_chips_wait_%d" % l, send, recv, arrays, state["plan2"], after)
            finals = [_sum_chips("sum_chips_%s_%d" % (k, l), state["mine"][a], arrays[n + a], self.where, l,
                                 None if self.finals is None else self.finals[a], 128)
                      for a, k in enumerate(_BIG)]
            plan3, nsem3 = _fill_plan(n, self.FILL_CHUNKS, l)
            state["plan3"] = plan3
            state["fill"] = _split_start("gather_halves_start_%d" % l, finals, plan3, nsem3, [])
            return state["fill"][3][0, 0]

        def finish(after):
            send, recv, arrays, _ = state["fill"]
            self.finals = _split_wait("gather_halves_wait_%d" % l, send, recv, arrays, state["plan3"], after)
            return self.finals

        self._finish = finish
        return [started, pair_done, chips_done]

    def finish(self, after):
        return self._finish(after)


def _local_step(x, target, small_p, w_in0, late, reducer, on_smalls):
    saved = []
    h = x
    gw = dict(w_in=[w_in0])
    h, sv = _layer_fwd(0, h, small_p[0], gw, late, None)
    saved.append(sv)
    (dy, sq), sv = _layer_fwd(1, h, small_p[1], gw, None, target)
    saved.append(sv)
    loss = 0.5 * jnp.sum(sq) / D
    smalls = [None, None]
    dx, smalls[1], big1 = _layer_bwd(1, dy, small_p[1], gw, saved[1], None)
    hooks1 = reducer.begin(1, big1)
    small_started = on_smalls(1, smalls[1])
    pair_started = hooks1[0]
    hooks1[0] = lambda after: pair_started(after) + small_started
    own = {}

    def layer0_ready(big0):
        reducer.finish([big0["w_in"]])
        own["hooks"] = reducer.begin(0, big0)
        return own["hooks"][0]([])

    dx, smalls[0], _ = _layer_bwd(0, dx, small_p[0], gw, saved[0],
                                  hooks1 + [layer0_ready, lambda after: own["hooks"][1](after)])
    on_smalls(0, smalls[0])

    def finish_reduce(after):
        own["hooks"][2](after)
        return reducer.finish(after)

    return loss, dx, smalls, finish_reduce


_SMALL_ROWS = 8 + 8 + 8 + 64 + 64
_SMALL_VECS = ("g_pre", "conv_b", "b_rg", "b_ig", "lam", "g_post")


def _pack_small(small):
    pad = lambda rows: jnp.zeros((rows, D), _F32)
    return jnp.concatenate(
        [small["dmod"].reshape(3, D), pad(5)] + [small[k] for k in _SMALL_VECS] + [pad(2)]
        + [small["conv_w"], pad(4), small["w_rg"].reshape(64, D), small["w_ig"].reshape(64, D)], axis=0)


def kernel(x, c, w_mod, b_mod, g_pre, w_in, conv_w, conv_b, w_rg, b_rg, w_ig, b_ig, lru_lambda, w_pa, w_pb, w_o, g_post, loss_target, m_w_mod, m_b_mod, m_g_pre, m_w_in, m_conv_w, m_conv_b, m_w_rg, m_b_rg, m_w_ig, m_b_ig, m_lru_lambda, m_w_pa, m_w_pb, m_w_o, m_g_post, v_w_mod, v_b_mod, v_g_pre, v_w_in, v_conv_w, v_conv_b, v_w_rg, v_b_rg, v_w_ig, v_b_ig, v_lru_lambda, v_w_pa, v_w_pb, v_w_o, v_g_post):
    xi, yi, ci = lax.axis_index("x"), lax.axis_index("y"), lax.axis_index("c")
    chip = 2 * xi + yi
    dev = 4 * xi + 2 * yi + ci
    mcols = w_mod.shape[2]

    pack1 = jnp.concatenate([jnp.broadcast_to(c, (8, D)),
                             jnp.pad(conv_w.reshape(8, 256), ((0, 0), (0, D - 256)))], axis=0)
    g1 = _exchange("gather_cond", [pack1], "xyc", False)[0]
    c_all = g1[:, 0, :]
    conv_w_full = jnp.transpose(g1[0::2, 8:16, 0:256], (1, 0, 2)).reshape(2, 4, D)

    b_cols = lax.dynamic_slice(b_mod, (0, chip * mcols), (2, mcols)).reshape(2, 1, mcols)
    mod_loc = _mod_fwd(c_all, w_mod, b_cols)
    g2 = _exchange("gather_mod", [mod_loc.reshape(16, mcols)], "xyc", False)[0]
    mod_full = jnp.transpose(g2[0::2], (1, 0, 2)).reshape(2, 8, 3 * D)
    mod_me = lax.dynamic_index_in_dim(mod_full, dev, axis=1, keepdims=False)

    wb_in = _cast("cast_w_in", w_in.reshape(2 * D, 2304), 256).reshape(2, D, 2304)
    late_src = [wb_in[1], _cast("cast_w_pa", w_pa.reshape(2 * ATT_W, 256), 256).reshape(2, ATT_W, 256),
                _cast("cast_w_pb", w_pb.reshape(512, D), 256).reshape(2, 256, D),
                _cast("cast_w_o", w_o.reshape(512, D), 256).reshape(2, 256, D)]
    late_chunks = [4, 2, 2, 2]
    w_in0 = _gather_weights([wb_in[0].reshape(2, D // 2, 2304)], [2])[0].reshape(N_CHIPS, D, 2304)
    chip1 = jnp.reshape(chip, (1,)).astype(jnp.int32)
    lands = [_own_slot("own_slot_" + k, a, chip1, 256) for k, a in zip(("w_in", "w_pa", "w_pb", "w_o"), late_src)]
    plan_a, nsem_a = _gather_plan(3, late_chunks[1:])
    send_a, recv_a, arrays_a, token_a = _split_start(
        "late_gather_start_a", late_src[1:] + lands[1:], plan_a, nsem_a, [w_in0, mod_me])
    plan_b, nsem_b = _gather_plan(1, late_chunks[:1])
    send_b, recv_b, arrays_b, token_b = _split_start(
        "late_gather_start_b", late_src[:1] + lands[:1], plan_b, nsem_b, [w_in0, mod_me, arrays_a[0]])
    token = token_a + token_b

    def late(after):
        got = _split_wait("late_gather_wait_a", send_a, recv_a, arrays_a, plan_a, [after])[3:]
        w_in1 = lambda later: _split_wait("late_gather_wait_b", send_b, recv_b, arrays_b, plan_b, [later])[1]
        return dict(w_in1=w_in1, w_pa=got[0], w_pb=got[1], w_o=got[2])

    small_p = []
    for l in range(2):
        gates = _gate_tiles(w_rg[l], w_ig[l]).astype(_MXU)
        small_p.append(dict(
            shift=mod_me[l:l + 1, 0:D], scale=mod_me[l:l + 1, D:2 * D], gate=mod_me[l:l + 1, 2 * D:3 * D],
            g_pre=g_pre[l:l + 1], conv_w=conv_w_full[l], conv_b=conv_b[l:l + 1], wt=gates,
            b_rg=b_rg[l:l + 1], b_ig=b_ig[l:l + 1], lam=lru_lambda[l:l + 1], g_post=g_post[l:l + 1]))

    small_p[0]["shift"] = small_p[0]["shift"] + token[0, 0]

    core = jnp.reshape(ci, (1,)).astype(jnp.int32)
    where = jnp.stack([chip, ci]).astype(jnp.int32)
    dev1 = jnp.reshape(dev, (1,)).astype(jnp.int32)
    small_plan, small_nsem = _all_plan()
    small_state = {}

    def on_smalls(l, small):
        pack = _pack_small(small)
        land = _own_slot("own_small_%d" % l, pack, dev1, _SMALL_ROWS, slots=8)
        small_state[l] = _split_start("gather_small_start_%d" % l, [pack, land], small_plan, small_nsem, [])
        return small_state[l][3][0, 0]

    def small_done(l, after):
        send, recv, arrays, _ = small_state[l]
        return _split_wait("gather_small_wait_%d" % l, send, recv, arrays, small_plan, after)[1]

    loss_loc, dx, _, finish_reduce = _local_step(x[0], loss_target[0], small_p, w_in0, late,
                                                 _GradReduce(core, where), on_smalls)
    loss = lax.psum(loss_loc, ("x", "y", "c"))
    grad_x = dx[None]
    reduced = finish_reduce([dx, small_state[0][2][0]])
    g_big ={k: a.reshape(2, 2 * a.shape[2], a.shape[3]) for k, a in zip(_BIG, reduced)}

    weights = dict(w_mod=w_mod, b_mod=b_mod, g_pre=g_pre, w_in=w_in, conv_w=conv_w, conv_b=conv_b, w_rg=w_rg,
                   b_rg=b_rg, w_ig=w_ig, b_ig=b_ig, lru_lambda=lru_lambda, w_pa=w_pa, w_pb=w_pb, w_o=w_o,
                   g_post=g_post)
    ms = dict(w_mod=m_w_mod, b_mod=m_b_mod, g_pre=m_g_pre, w_in=m_w_in, conv_w=m_conv_w, conv_b=m_conv_b,
              w_rg=m_w_rg, b_rg=m_b_rg, w_ig=m_w_ig, b_ig=m_b_ig, lru_lambda=m_lru_lambda, w_pa=m_w_pa,
              w_pb=m_w_pb, w_o=m_w_o, g_post=m_g_post)
    vs = dict(w_mod=v_w_mod, b_mod=v_b_mod, g_pre=v_g_pre, w_in=v_w_in, conv_w=v_conv_w, conv_b=v_conv_b,
              w_rg=v_w_rg, b_rg=v_b_rg, w_ig=v_w_ig, b_ig=v_b_ig, lru_lambda=v_lru_lambda, w_pa=v_w_pa,
              w_pb=v_w_pb, w_o=v_w_o, g_post=v_g_post)
    flat = dict(w_mod=(2 * D, mcols, 256), b_mod=(2, 3 * D, 2), g_pre=(2, D, 2), w_in=(2 * D, 2304, 256),
                conv_w=(8, 256, 8), conv_b=(2, D, 2), w_rg=(128, D, 128), b_rg=(2, D, 2), w_ig=(128, D, 128),
                b_ig=(2, D, 2), lru_lambda=(2, D, 2), w_pa=(2 * ATT_W, 256, 256), w_pb=(512, D, 256),
                w_o=(512, D, 256), g_post=(2, D, 2))
    order = ("w_mod", "b_mod", "g_pre", "w_in", "conv_w", "conv_b", "w_rg", "b_rg", "w_ig", "b_ig",
             "lru_lambda", "w_pa", "w_pb", "w_o", "g_post")

    def adam(k, g):
        rows, cols, tb = flat[k]
        return _adamw("adamw_" + k, weights[k].reshape(rows, cols), g.reshape(rows, cols),
                      ms[k].reshape(rows, cols), vs[k].reshape(rows, cols), tb)

    stepped = {k: adam(k, g_big[k]) for k in _BIG}

    g3 = [small_done(l, [stepped["w_in"][0]]) for l in range(2)]
    tot =[_sum_lead("sum_small_%d" % l, g3[l], _SMALL_ROWS) for l in range(2)]
    dmod_all = jnp.stack([g3[l][:, 0:3, :].reshape(8, 3 * D) for l in range(2)], axis=0)
    dm_cols = lax.dynamic_slice(dmod_all, (0, 0, chip * mcols), (2, 8, mcols))
    g_w_mod = _mod_bwd(jnp.transpose(c_all), dm_cols)
    both = lambda first, rows: jnp.stack([tot[l][first:first + rows] for l in range(2)], axis=0)
    vec = both(8, 6)
    grads = dict(
        w_mod=g_w_mod, b_mod=both(0, 3).reshape(2, 3 * D), g_pre=vec[:, 0], w_in=g_big["w_in"],
        conv_w=lax.dynamic_slice(both(16, 4), (0, 0, chip * 256), (2, 4, 256)), conv_b=vec[:, 1],
        w_rg=both(24, 64).reshape(2, 16, 64, 64), b_rg=vec[:, 2], w_ig=both(88, 64).reshape(2, 16, 64, 64),
        b_ig=vec[:, 3], lru_lambda=vec[:, 4], w_pa=g_big["w_pa"], w_pb=g_big["w_pb"], w_o=g_big["w_o"],
        g_post=vec[:, 5])
    for k in order:
        if k not in stepped:
            stepped[k] = adam(k, grads[k])
    deltas, new_m, new_v = ([stepped[k][t].reshape(weights[k].shape) for k in order] for t in range(3))
    return (loss, grad_x, *[grads[k].reshape(weights[k].shape) for k in order], *deltas, *new_m, *new_v)
```

```python
import functools

import jax
import jax.numpy as jnp
from jax import lax
from jax.experimental import pallas as pl
from jax.experimental.pallas import tpu as pltpu

_F32 = jnp.float32
_MXU = jnp.bfloat16
_VMEM_LIMIT = 56 * 1024 * 1024
_MESH = pl.DeviceIdType.MESH

D = 1024
HEAD = 128
HEADS = 4
ATT_W = 512
QKV_W = 1536
IN_W = 9216
DILATIONS = (1, 4, 16)
BAND = 128
QBLK = BAND * 16
NORM_EPS = 1e-6
NEG_INF = -1e30
LRU_C = 8.0
N_CHIPS = 4
CB_GATT = 4608 // 512
CB_U, CB_GLRU, CB_MA, CB_MB = 5, 6, 7, 8
R_U, R_GLRU, R_MA, R_MB, R_END = 512, 1536, 2560, 3584, 4608

ADAM_LR, ADAM_B1, ADAM_B2, ADAM_EPS, ADAM_WD, ADAM_STEP = 0.001, 0.9, 0.999, 1e-08, 0.01, 10


def _params(ngrid):
    return pltpu.CompilerParams(dimension_semantics=("arbitrary",) * ngrid, vmem_limit_bytes=_VMEM_LIMIT)


def _sigmoid(v):
    return 0.5 * jnp.tanh(0.5 * v) + 0.5


_GROUPS = {
    "c": [(0, 0, 1)],
    "xy": [(1, 0, 0), (0, 1, 0), (1, 1, 0)],
    "xyc": [(0, 0, 1), (0, 1, 0), (0, 1, 1), (1, 0, 0), (1, 0, 1), (1, 1, 0), (1, 1, 1)],
}


def _rank(group, px, py, pc):
    if group == "c":
        return pc
    if group == "xy":
        return 2 * px + py
    return 4 * px + 2 * py + pc


def _flip(rel, x, y, c):
    dx, dy, dc = rel
    return (1 - x if dx else x, 1 - y if dy else y, 1 - c if dc else c)


def _pieces(ref, nchunk):
    step = ref.shape[0] // nchunk
    return [ref.at[pl.ds(q * step, step)] for q in range(nchunk)]


def _exchange(name, srcs, group, scatter, *, local=True, nchunks=None):
    rels = _GROUPS[group]
    gsize = len(rels) + 1
    n = len(srcs)
    nchunks = nchunks or [1] * n
    blks = [s.shape[1:] if scatter else s.shape for s in srcs]
    slotted = local or gsize > 2
    base = [sum(nchunks[:a]) for a in range(n)]
    tot = sum(nchunks)

    def body(*refs):
        src_refs, out_refs = refs[:n], refs[n:2 * n]
        send_sems, recv_sems, loc_sems = refs[2 * n:]
        x, y, c = lax.axis_index("x"), lax.axis_index("y"), lax.axis_index("c")
        me = _rank(group, x, y, c)
        copies = []
        for a in range(n):
            def part(r, a=a):
                return src_refs[a].at[r] if scatter else src_refs[a]
            dst = out_refs[a].at[me] if slotted else out_refs[a]
            if local:
                for q, (s_, d_) in enumerate(zip(_pieces(part(me), nchunks[a]), _pieces(dst, nchunks[a]))):
                    loc = pltpu.make_async_copy(s_, d_, loc_sems.at[base[a] + q])
                    loc.start()
                    copies.append(loc)
            for k, rel in enumerate(rels):
                peer = _flip(rel, x, y, c)
                for q, (s_, d_) in enumerate(zip(_pieces(part(_rank(group, *peer)), nchunks[a]),
                                                 _pieces(dst, nchunks[a]))):
                    cp = pltpu.make_async_remote_copy(
                        src_ref=s_, dst_ref=d_, send_sem=send_sems.at[(base[a] + q) * len(rels) + k],
                        recv_sem=recv_sems.at[(base[a] + q) * len(rels) + k],
                        device_id=peer, device_id_type=_MESH)
                    cp.start()
                    copies.append(cp)
        for cp in copies:
            cp.wait()

    any_spec = pl.BlockSpec(memory_space=pl.ANY)
    lead = (gsize,) if slotted else ()
    return pl.pallas_call(
        body, name=name,
        out_shape=[jax.ShapeDtypeStruct(lead + tuple(b), s.dtype) for b, s in zip(blks, srcs)],
        in_specs=[any_spec] * n, out_specs=[any_spec] * n,
        scratch_shapes=[pltpu.SemaphoreType.DMA((tot * len(rels),)), pltpu.SemaphoreType.DMA((tot * len(rels),)),
                        pltpu.SemaphoreType.DMA((tot,))],
    )(*srcs)


def _gather_weights(wb, nchunks):
    n = len(wb)
    rels = _GROUPS["xy"]
    base = [sum(nchunks[:a]) for a in range(n)]
    tot = sum(nchunks)

    def body(*refs):
        src_refs, out_refs = refs[:n], refs[n:2 * n]
        ici_send, ici_recv, d2d_send, d2d_recv, loc_sems = refs[2 * n:]
        x, y, c = lax.axis_index("x"), lax.axis_index("y"), lax.axis_index("c")
        me = 2 * x + y
        waits = []
        for a in range(n):
            for l in range(2):
                for q, (s_, d_) in enumerate(zip(_pieces(src_refs[a].at[l], nchunks[a]),
                                                 _pieces(out_refs[a].at[me, l], nchunks[a]))):
                    loc = pltpu.make_async_copy(s_, d_, loc_sems.at[(base[a] + q) * 2 + l])
                    loc.start()
                    waits.append(loc)
        first = []
        for a in range(n):
            for k, rel in enumerate(rels):
                px, py, _ = _flip(rel, x, y, c)
                for q, (s_, d_) in enumerate(zip(_pieces(src_refs[a].at[c], nchunks[a]),
                                                 _pieces(out_refs[a].at[me, c], nchunks[a]))):
                    sem = (base[a] + q) * 3 + k
                    cp = pltpu.make_async_remote_copy(src_ref=s_, dst_ref=d_, send_sem=ici_send.at[sem],
                                                      recv_sem=ici_recv.at[sem], device_id=(px, py, c),
                                                      device_id_type=_MESH)
                    cp.start()
                    first.append(cp)
        second = []
        for a in range(n):
            for k, rel in enumerate(rels):
                px, py, _ = _flip(rel, x, y, c)
                for q, blk in enumerate(_pieces(out_refs[a].at[2 * px + py, c], nchunks[a])):
                    sem = (base[a] + q) * 3 + k
                    landed = pltpu.make_async_remote_copy(src_ref=blk, dst_ref=blk, send_sem=ici_send.at[sem],
                                                          recv_sem=ici_recv.at[sem], device_id=(px, py, c),
                                                          device_id_type=_MESH)
                    landed.wait_recv()
                    cp = pltpu.make_async_remote_copy(src_ref=blk, dst_ref=blk, send_sem=d2d_send.at[sem],
                                                      recv_sem=d2d_recv.at[sem], device_id=(x, y, 1 - c),
                                                      device_id_type=_MESH)
                    cp.start()
                    second.append(cp)
        for cp in first:
            cp.wait_send()
        for cp in second:
            cp.wait_send()
        for a in range(n):
            for k, rel in enumerate(rels):
                px, py, _ = _flip(rel, x, y, c)
                for q, blk in enumerate(_pieces(out_refs[a].at[2 * px + py, 1 - c], nchunks[a])):
                    sem = (base[a] + q) * 3 + k
                    pltpu.make_async_remote_copy(src_ref=blk, dst_ref=blk, send_sem=d2d_send.at[sem],
                                                 recv_sem=d2d_recv.at[sem], device_id=(x, y, 1 - c),
                                                 device_id_type=_MESH).wait_recv()
        for cp in waits:
            cp.wait()

    any_spec = pl.BlockSpec(memory_space=pl.ANY)
    return pl.pallas_call(
        body, name="gather_weights",
        out_shape=[jax.ShapeDtypeStruct((N_CHIPS,) + a.shape, a.dtype) for a in wb],
        in_specs=[any_spec] * n, out_specs=[any_spec] * n,
        scratch_shapes=[pltpu.SemaphoreType.DMA((tot * 3,))] * 4 + [pltpu.SemaphoreType.DMA((tot * 2,))],
    )(*wb)


_HBM = pl.BlockSpec(memory_space=pltpu.HBM)
_SEM = pl.BlockSpec(memory_space=pltpu.SEMAPHORE)
_EFFECT = pltpu.SideEffectType.DATAFLOW_SIDE_EFFECTING


def _own_slot(name, src, chip, tb, slots=N_CHIPS):
    rows, cols = src.shape[-2:]
    lead = src.shape[:-2]
    flat = src.reshape((-1, cols))

    def body(s_ref, a_ref, o_ref):
        o_ref[...] = a_ref[...]

    grid_spec = pltpu.PrefetchScalarGridSpec(
        num_scalar_prefetch=1, grid=(flat.shape[0] // tb,),
        in_specs=[pl.BlockSpec((tb, cols), lambda i, s: (i, 0))],
        out_specs=pl.BlockSpec((None, tb, cols), lambda i, s: (s[0], i, 0)))
    out = pl.pallas_call(body, name=name, grid_spec=grid_spec,
                         out_shape=jax.ShapeDtypeStruct((slots,) + flat.shape, src.dtype),
                         compiler_params=_params(1))(chip, flat)
    return out.reshape((slots,) + lead + (rows, cols))


def _numbered(pairs, peer, send_sems, recv_sems, first):
    return [pltpu.make_async_remote_copy(src_ref=s_, dst_ref=d_, send_sem=send_sems.at[first + q],
                                         recv_sem=recv_sems.at[first + q], device_id=peer, device_id_type=_MESH)
            for q, (s_, d_) in enumerate(pairs)]


def _gather_plan(n, nchunks):
    def plan(refs, send_sems, recv_sems):
        x, y, c = lax.axis_index("x"), lax.axis_index("y"), lax.axis_index("c")
        me = 2 * x + y
        copies = []
        for a in range(n):
            for rel in _GROUPS["xy"]:
                px, py, _ = _flip(rel, x, y, c)
                pairs = list(zip(_pieces(refs[a], nchunks[a]), _pieces(refs[n + a].at[me], nchunks[a])))
                copies += _numbered(pairs, (px, py, c), send_sems, recv_sems, len(copies))
        return copies
    return plan, 3 * sum(nchunks)


def _all_plan():
    def plan(refs, send_sems, recv_sems):
        x, y, c = lax.axis_index("x"), lax.axis_index("y"), lax.axis_index("c")
        me = 4 * x + 2 * y + c
        copies = []
        for rel in _GROUPS["xyc"]:
            copies += _numbered([(refs[0], refs[1].at[me])], _flip(rel, x, y, c), send_sems, recv_sems, len(copies))
        return copies
    return plan, len(_GROUPS["xyc"])


def _pair_plan(n, nchunks):
    def plan(refs, send_sems, recv_sems):
        x, y, c = lax.axis_index("x"), lax.axis_index("y"), lax.axis_index("c")
        copies = []
        for a in range(n):
            for j in range(N_CHIPS):
                pairs = list(zip(_pieces(refs[a].at[j, 1 - c], nchunks[a]), _pieces(refs[n + a].at[j], nchunks[a])))
                copies += _numbered(pairs, (x, y, 1 - c), send_sems, recv_sems, len(copies))
        return copies
    return plan, N_CHIPS * sum(nchunks)


def _chips_plan(n, nchunks):
    def plan(refs, send_sems, recv_sems):
        x, y, c = lax.axis_index("x"), lax.axis_index("y"), lax.axis_index("c")
        me = 2 * x + y
        copies = []
        for a in range(n):
            for rel in _GROUPS["xy"]:
                px, py, _ = _flip(rel, x, y, c)
                pairs = list(zip(_pieces(refs[a].at[2 * px + py], nchunks[a]), _pieces(refs[n + a].at[me], nchunks[a])))
                copies += _numbered(pairs, (px, py, c), send_sems, recv_sems, len(copies))
        return copies
    return plan, 3 * sum(nchunks)


def _fill_plan(n, nchunks, l):
    def plan(refs, send_sems, recv_sems):
        x, y, c = lax.axis_index("x"), lax.axis_index("y"), lax.axis_index("c")
        copies = []
        for a in range(n):
            blk = _pieces(refs[a].at[l, c], nchunks[a])
            copies += _numbered(list(zip(blk, blk)), (x, y, 1 - c), send_sems, recv_sems, len(copies))
        return copies
    return plan, sum(nchunks)


def _split_start(name, arrays, plan, nsem, after):
    n = len(arrays)
    na = len(after)

    def body(*refs):
        send_sems, recv_sems = refs[n + na], refs[n + na + 1]
        token = refs[-1]
        for cp in plan(refs[:n], send_sems, recv_sems):
            cp.start()
        token[...] = jnp.zeros_like(token)

    hbm = [pltpu.HBM(a.shape, a.dtype) for a in arrays]
    outs = pl.pallas_call(
        body, name=name,
        out_shape=(pltpu.SemaphoreType.DMA((nsem,)), pltpu.SemaphoreType.DMA((nsem,)), *hbm, _sds((8, 128))),
        in_specs=[_HBM] * n + [pl.BlockSpec(memory_space=pl.ANY)] * na,
        out_specs=(_SEM, _SEM, *([_HBM] * n), pl.BlockSpec(memory_space=pltpu.VMEM)),
        input_output_aliases={i: 2 + i for i in range(n)},
        compiler_params=pltpu.CompilerParams(has_side_effects=_EFFECT),
    )(*[pltpu.with_memory_space_constraint(a, pltpu.HBM) for a in arrays], *after)
    return outs[0], outs[1], list(outs[2:2 + n]), outs[-1]


def _split_wait(name, send_sems, recv_sems, arrays, plan, after):
    n = len(arrays)

    def body(*refs):
        for cp in plan(refs[:n], refs[n], refs[n + 1]):
            cp.wait_send()
            cp.wait_recv()

    hbm = [pltpu.HBM(a.shape, a.dtype) for a in arrays]
    return list(pl.pallas_call(
        body, name=name, out_shape=tuple(hbm),
        in_specs=[_HBM] * n + [_SEM, _SEM] + [pl.BlockSpec(memory_space=pl.ANY)] * len(after),
        out_specs=tuple([_HBM] * n), input_output_aliases={i: i for i in range(n)},
        compiler_params=pltpu.CompilerParams(has_side_effects=_EFFECT),
    )(*arrays, send_sems, recv_sems, *after))


def _mm(name, a, b, out_sds, *, grid, a_spec, b_spec, o_spec, dims, acc_shape, into=None):
    nk = grid[2]

    def body(*refs):
        a_ref, b_ref = refs[0], refs[1]
        o_ref, acc = refs[-2], refs[-1]
        k = pl.program_id(2)
        part = lax.dot_general(a_ref[...].astype(_MXU), b_ref[...].astype(_MXU), dims,
                               preferred_element_type=_F32)
        if nk == 1:
            o_ref[...] = part.astype(o_ref.dtype)
            return

        @pl.when(k == 0)
        def _():
            acc[...] = part

        @pl.when(k > 0)
        def _():
            acc[...] += part

        @pl.when(k == nk - 1)
        def _():
            o_ref[...] = acc[...].astype(o_ref.dtype).reshape(o_ref.shape)

    if nk == 1:
        acc_shape = (8, 128)
    in_specs = [a_spec, b_spec]
    args = [a, b]
    aliases = {}
    if into is not None:
        in_specs.append(pl.BlockSpec(memory_space=pl.ANY))
        args.append(into)
        aliases = {2: 0}
    return pl.pallas_call(
        body, name=name, grid=grid, in_specs=in_specs, out_specs=o_spec, out_shape=out_sds,
        scratch_shapes=[pltpu.VMEM(acc_shape, _F32)], input_output_aliases=aliases,
        compiler_params=_params(3))(*args)


_NN = (((1,), (0,)), ((), ()))
_NT = (((1,), (1,)), ((), ()))
_TN = (((0,), (0,)), ((), ()))


def _rowwise(name, body, *, grid, ins, outs, scratch=()):
    return pl.pallas_call(
        body, name=name, grid=(grid,), in_specs=[s for _, s in ins], out_specs=[s for _, s in outs],
        out_shape=[o for o, _ in outs], scratch_shapes=list(scratch),
        compiler_params=_params(1))(*[a for a, _ in ins])


def _rows(tb, w, cb=0, n=None):
    if n is None:
        return pl.BlockSpec((tb, w), lambda i: (i, cb))
    return pl.BlockSpec((tb, w), lambda i: (n - 1 - i, cb))


def _vec(shape):
    return pl.BlockSpec(shape, lambda i: (0,) * len(shape))


def _halo_prev(tb, w, cb=0, n=None, rows=8):
    if n is None:
        return pl.BlockSpec((rows, w), lambda i: (jnp.maximum(i * (tb // rows) - 1, 0), cb))
    return pl.BlockSpec((rows, w), lambda i: (jnp.maximum((n - 1 - i) * (tb // rows) - 1, 0), cb))


def _halo_next(tb, w, n, cb=0):
    return pl.BlockSpec((8, w), lambda i: (jnp.minimum((i + 1) * (tb // 8), n * (tb // 8) - 1), cb))


def _sds(shape, dtype=_F32):
    return jax.ShapeDtypeStruct(shape, dtype)


def _cast(name, a, tb):
    rows, cols = a.shape

    def body(a_ref, o_ref):
        o_ref[...] = a_ref[...].astype(o_ref.dtype)

    return _rowwise(name, body, grid=rows // tb, ins=[(a, _rows(tb, cols))],
                    outs=[(_sds((rows, cols), _MXU), _rows(tb, cols))])[0]


def _sum_lead(name, a, tb):
    g, rows, cols = a.shape

    def body(a_ref, o_ref):
        acc = a_ref[0]
        for k in range(1, g):
            acc = acc + a_ref[k]
        o_ref[...] = acc

    return _rowwise(name, body, grid=rows // tb,
                    ins=[(a, pl.BlockSpec((g, tb, cols), lambda i: (0, i, 0)))],
                    outs=[(_sds((rows, cols)), _rows(tb, cols))])[0]


def _sum_pair(name, mine, theirs, core, tb):
    nj, _, rows, cols = mine.shape

    def body(s_ref, a_ref, b_ref, o_ref, ob_ref):
        t = a_ref[...] + b_ref[...]
        o_ref[...] = t
        ob_ref[...] = t.astype(ob_ref.dtype)

    blk = pl.BlockSpec((None, tb, cols), lambda j, i, s: (j, i, 0))
    grid_spec = pltpu.PrefetchScalarGridSpec(
        num_scalar_prefetch=1, grid=(nj, rows // tb),
        in_specs=[pl.BlockSpec((None, None, tb, cols), lambda j, i, s: (j, s[0], i, 0)), blk],
        out_specs=[blk, blk])
    return pl.pallas_call(body, name=name, grid_spec=grid_spec,
                          out_shape=[_sds((nj, rows, cols)), _sds((nj, rows, cols), _MXU)],
                          compiler_params=_params(2))(core, mine, theirs)


def _sum_chips(name, mine, theirs, where, l, into, tb):
    _, rows, cols = mine.shape
    extra = [] if into is None else [into]

    def body(*refs):
        a_ref, b1_ref, b2_ref, b3_ref = refs[1:5]
        o_ref = refs[-1]
        o_ref[...] = ((a_ref[...] + b1_ref[...].astype(_F32)) + b2_ref[...].astype(_F32)) + b3_ref[...].astype(_F32)

    def slot(k):
        return pl.BlockSpec((None, tb, cols), lambda i, s: (jnp.bitwise_xor(s[0], k), i, 0))

    grid_spec = pltpu.PrefetchScalarGridSpec(
        num_scalar_prefetch=1, grid=(rows // tb,),
        in_specs=[slot(0), slot(1), slot(2), slot(3)] + [pl.BlockSpec(memory_space=pl.ANY)] * len(extra),
        out_specs=pl.BlockSpec((None, None, tb, cols), lambda i, s: (l, s[1], i, 0)))
    return pl.pallas_call(body, name=name, grid_spec=grid_spec, out_shape=_sds((2, 2, rows, cols)),
                          input_output_aliases={5: 0} if extra else {},
                          compiler_params=_params(1))(where, mine, theirs, theirs, theirs, *extra)


def _adamw(name, w, g, m, v, tb):
    rows, cols = w.shape
    c1 = 1.0 - ADAM_B1 ** ADAM_STEP
    c2 = 1.0 - ADAM_B2 ** ADAM_STEP

    def body(w_ref, g_ref, m_ref, v_ref, d_ref, nm_ref, nv_ref):
        gv = g_ref[...]
        nm = ADAM_B1 * m_ref[...] + (1.0 - ADAM_B1) * gv
        nv = ADAM_B2 * v_ref[...] + (1.0 - ADAM_B2) * (gv * gv)
        d_ref[...] = -ADAM_LR * ((nm / c1) / (jnp.sqrt(nv / c2) + ADAM_EPS) + ADAM_WD * w_ref[...])
        nm_ref[...] = nm
        nv_ref[...] = nv

    spec = _rows(tb, cols)
    return _rowwise(name, body, grid=rows // tb, ins=[(w, spec), (g, spec), (m, spec), (v, spec)],
                    outs=[(_sds((rows, cols)), spec)] * 3)


def _mod_fwd(c_all, w_mod, b_cols):
    cols = w_mod.shape[2]

    def body(c_ref, w_ref, b_ref, o_ref):
        cv = c_ref[...]
        sc = (cv * _sigmoid(cv)).astype(_MXU)
        o_ref[...] = jnp.dot(sc, w_ref[...].astype(_MXU), preferred_element_type=_F32) + b_ref[...]

    return pl.pallas_call(
        body, name="mod_fwd", grid=(2,),
        in_specs=[pl.BlockSpec((8, D), lambda l: (0, 0)), pl.BlockSpec((None, D, cols), lambda l: (l, 0, 0)),
                  pl.BlockSpec((None, 1, cols), lambda l: (l, 0, 0))],
        out_specs=pl.BlockSpec((None, 8, cols), lambda l: (l, 0, 0)),
        out_shape=_sds((2, 8, cols)), compiler_params=_params(1))(c_all, w_mod, b_cols)


def _mod_bwd(c_all_t, dm):
    cols = dm.shape[2]

    def body(c_ref, d_ref, o_ref):
        cv = c_ref[...]
        sc = (cv * _sigmoid(cv)).astype(_MXU)
        o_ref[...] = jnp.dot(sc, d_ref[...].astype(_MXU), preferred_element_type=_F32)

    return pl.pallas_call(
        body, name="mod_bwd", grid=(2,),
        in_specs=[pl.BlockSpec((D, 8), lambda l: (0, 0)), pl.BlockSpec((None, 8, cols), lambda l: (l, 0, 0))],
        out_specs=pl.BlockSpec((None, D, cols), lambda l: (l, 0, 0)),
        out_shape=_sds((2, D, cols)), compiler_params=_params(1))(c_all_t, dm)


def _proj(x, g_pre, shift, scale, w_in):
    s = x.shape[0]
    tm = 1024

    def body(x_ref, g_ref, sh_ref, sc_ref, w_ref, o_ref, ht_ref, h_s):
        @pl.when(pl.program_id(1) == 0)
        def _():
            xv = x_ref[...]
            rstd = lax.rsqrt(jnp.mean(xv * xv, axis=-1, keepdims=True) + NORM_EPS)
            hv = (xv * rstd) * g_ref[...] * (1.0 + sc_ref[...]) + sh_ref[...]
            h_s[...] = hv.astype(h_s.dtype)
            ht_ref[...] = hv.T.astype(ht_ref.dtype)

        o_ref[...] = jnp.dot(h_s[...], w_ref[...], preferred_element_type=_F32).astype(o_ref.dtype)

    vec = pl.BlockSpec((1, D), lambda m, n: (0, 0))
    return pl.pallas_call(
        body, name="proj", grid=(s // tm, N_CHIPS),
        in_specs=[pl.BlockSpec((tm, D), lambda m, n: (m, 0)), vec, vec, vec,
                  pl.BlockSpec((None, D, 2304), lambda m, n: (n, 0, 0))],
        out_specs=[pl.BlockSpec((tm, 2304), lambda m, n: (m, n)), pl.BlockSpec((D, tm), lambda m, n: (0, m))],
        out_shape=[_sds((s, IN_W), _MXU), _sds((D, s), _MXU)],
        scratch_shapes=[pltpu.VMEM((tm, D), _MXU)], compiler_params=_params(2))(x, g_pre, shift, scale, w_in)


def _shift_down(cur, halo, j, tb):
    ext = jnp.concatenate([halo, cur], axis=0)
    return pltpu.roll(ext, j, 0)[8:8 + tb]


def _shift_up(cur, halo, j, tb):
    ext = jnp.concatenate([cur, halo], axis=0)
    return pltpu.roll(ext, tb + 8 - j, 0)[0:tb]


def _conv(u_ref, halo_ref, w_ref, b_ref, first, tb):
    u = u_ref[...].astype(_F32)
    halo = jnp.where(first, 0.0, halo_ref[...].astype(_F32)[8:16])
    acc = b_ref[...] + u * w_ref[0:1, :]
    for j in range(1, 4):
        acc = acc + _shift_down(u, halo, j, tb) * w_ref[j:j + 1, :]
    return acc


def _lru_gates(pre_r, pre_i, uc, b_rg, b_ig, lam):
    r = _sigmoid(pre_r + b_rg)
    ig = _sigmoid(pre_i + b_ig)
    nl = -lam
    sp = jnp.maximum(nl, 0.0) + jnp.log(1.0 + jnp.exp(-jnp.abs(nl)))
    la = -LRU_C * r * sp
    a = jnp.exp(la)
    one_m_a2 = -jnp.tanh(la) * (a * a + 1.0)
    inv_sq = lax.rsqrt(jnp.maximum(one_m_a2, 1e-30))
    return r, ig, sp, a, one_m_a2 * inv_sq, inv_sq


GATE_TILES = 8


def _gate_tiles(w_rg, w_ig):
    eye = jnp.eye(2, dtype=w_rg.dtype)

    def tiles(w):
        return jnp.einsum("cpij,pq->cpiqj", w.reshape(GATE_TILES, 2, 64, 64), eye).reshape(GATE_TILES, 128, 128)

    return jnp.concatenate([tiles(w_rg), tiles(w_ig)], axis=2)


def _gate_tile_grads(gw):
    keep = jnp.eye(2, dtype=jnp.bool_)[None, :, None, :, None]

    def blocks(t):
        t5 = t.reshape(GATE_TILES, 2, 64, 2, 64)
        return jnp.sum(jnp.where(keep, t5, 0.0), axis=3).reshape(16, 64, 64)

    return blocks(gw[:, :, 0:128]), blocks(gw[:, :, 128:256])


def _gate_preacts(ucv, wt_ref):
    ucb = ucv.astype(_MXU)
    ps = [jnp.dot(ucb[:, 128 * c:128 * (c + 1)], wt_ref[c], preferred_element_type=_F32) for c in range(GATE_TILES)]
    pre_r = jnp.concatenate([p[:, 0:128] for p in ps], axis=1)
    pre_i = jnp.concatenate([p[:, 128:256] for p in ps], axis=1)
    return pre_r, pre_i


def _scan_fwd(proj, conv_w, conv_b, wt, b_rg, b_ig, lam):
    s = proj.shape[0]
    tb = 256

    def body(u_ref, up_ref, cw_ref, cb_ref, wt_ref, brg_ref, big_ref, lam_ref, h_ref, carry, a_s, b_s):
        i = pl.program_id(0)

        @pl.when(i == 0)
        def _():
            carry[...] = jnp.zeros_like(carry)

        ucv = _conv(u_ref, up_ref, cw_ref, cb_ref, i == 0, tb)
        pre_r, pre_i = _gate_preacts(ucv, wt_ref)
        _, ig, _, a, sq, _ = _lru_gates(pre_r, pre_i, ucv, brg_ref[...], big_ref[...], lam_ref[...])
        av = a
        bv = sq * (ig * ucv)
        av = av.reshape(tb // 8, 8, D)
        bv = bv.reshape(tb // 8, 8, D)
        row8 = lax.broadcasted_iota(jnp.int32, (1, 8, 1), 1)
        for sh in (1, 2, 4):
            m = row8 >= sh
            b_sh = pltpu.roll(bv, sh, 1)
            a_sh = pltpu.roll(av, sh, 1)
            bv = jnp.where(m, av * b_sh + bv, bv)
            av = jnp.where(m, av * a_sh, av)
        a_s[...] = av.reshape(tb, D)
        b_s[...] = bv.reshape(tb, D)

        def tile(t, state):
            rows = pl.ds(pl.multiple_of(t * 8, 8), 8)
            hv = b_s[rows, :] + a_s[rows, :] * state
            b_s[rows, :] = hv
            return jnp.broadcast_to(hv[7:8, :], (8, D))

        carry[...] = lax.fori_loop(0, tb // 8, tile, jnp.broadcast_to(carry[7:8, :], (8, D)), unroll=4)
        h_ref[...] = b_s[...].astype(h_ref.dtype)

    v = _vec((1, D))
    return _rowwise("scan_fwd", body, grid=s // tb,
                    ins=[(proj, _rows(tb, D, CB_U)), (proj, _halo_prev(tb, D, CB_U, rows=16)),
                         (conv_w, _vec((4, D))), (conv_b, v), (wt, _vec((GATE_TILES, 128, 256))),
                         (b_rg, v), (b_ig, v), (lam, v)],
                    outs=[(_sds((s, D), _MXU), _rows(tb, D))],
                    scratch=[pltpu.VMEM((8, D), _F32), pltpu.VMEM((tb, D), _F32), pltpu.VMEM((tb, D), _F32)])[0]


def _weight_specs(l):
    return [pl.BlockSpec((N_CHIPS, None, ATT_W, 256), lambda i: (0, l, 0, 0)),
            pl.BlockSpec((N_CHIPS, None, 256, D), lambda i: (0, l, 0, 0)),
            pl.BlockSpec((N_CHIPS, None, 256, D), lambda i: (0, l, 0, 0))]


def _tail_fwd(l, o, h_lru, proj, x, gate, g_post, gw, target):
    s = x.shape[0]
    tb = 512

    def body(*refs):
        o_ref, h_ref, ga_ref, gl_ref, ma_ref, mb_ref, x_ref, gt_ref, gp_ref, wpa_ref, wpb_ref, wo_ref = refs[0:12]
        aa_ref, ba_ref, ya_ref, yb_ref, z_ref, out_ref = refs[-8:-2] if target is not None else refs[-7:-1]
        ga = ga_ref[...].astype(_F32)
        aa32 = o_ref[...].astype(_F32) * (ga * _sigmoid(ga))
        aa = aa32.astype(_MXU)
        aa_ref[...] = aa32.T.astype(aa_ref.dtype)
        gl = gl_ref[...].astype(_F32)
        ba32 = h_ref[...].astype(_F32) * (gl * _sigmoid(gl))
        ba = ba32.astype(_MXU)
        ba_ref[...] = ba32.T.astype(ba_ref.dtype)
        ya = jnp.concatenate([jnp.dot(aa, wpa_ref[j], preferred_element_type=_F32) for j in range(N_CHIPS)], axis=1)
        ya_ref[...] = ya.astype(ya_ref.dtype)
        yb = jnp.dot(ba, wpb_ref[...].reshape(D, D), preferred_element_type=_F32)
        yb_ref[...] = yb.astype(yb_ref.dtype)
        z32 = _sigmoid(ma_ref[...].astype(_F32)) * ya + _sigmoid(mb_ref[...].astype(_F32)) * yb
        z = z32.astype(_MXU)
        z_ref[...] = z32.T.astype(z_ref.dtype)
        ov = jnp.dot(z, wo_ref[...].reshape(D, D), preferred_element_type=_F32)
        out_ref[...] = ov.astype(out_ref.dtype)
        rstd = lax.rsqrt(jnp.mean(ov * ov, axis=-1, keepdims=True) + NORM_EPS)
        xn =x_ref[...] + gt_ref[...] * ((ov * rstd) * gp_ref[...])
        if target is None:
            refs[-1][...] = xn
        else:
            dy_ref, acc_ref = refs[-2], refs[-1]
            err = xn - refs[12][...]
            dy_ref[...] = err * (1.0 / D)
            _zero_first(pl.program_id(0), acc_ref)
            acc_ref[...] += jnp.sum(err * err, axis=0, keepdims=True)

    v = _vec((1, D))
    r = _rows(tb, D)
    r5 = _rows(tb, ATT_W)
    weights = list(zip((gw["w_pa"], gw["w_pb"], gw["w_o"]), _weight_specs(l)))
    cols = pl.BlockSpec((D, tb), lambda i: (0, i))
    head_in = [] if target is None else [(target, r)]
    head_out = [] if target is None else [(_sds((1, D)), v)]
    return _rowwise("tail_fwd" if target is None else "tail_loss_fwd", body, grid=s // tb,
                    ins=[(o, r5), (h_lru, r), (proj, _rows(tb, ATT_W, CB_GATT)), (proj, _rows(tb, D, CB_GLRU)),
                         (proj, _rows(tb, D, CB_MA)), (proj, _rows(tb, D, CB_MB)), (x, r), (gate, v), (g_post, v)]
                    + weights + head_in,
                    outs=[(_sds((ATT_W, s), _MXU), pl.BlockSpec((ATT_W, tb), lambda i: (0, i))),
                          (_sds((D, s), _MXU), cols), (_sds((s, D), _MXU), r), (_sds((s, D), _MXU), r),
                          (_sds((D, s), _MXU), cols), (_sds((s, D), _MXU), r), (_sds((s, D)), r)]
                    + head_out)


def _zero_first(i, *refs):
    @pl.when(i == 0)
    def _():
        for ref in refs:
            ref[...] = jnp.zeros_like(ref)


def _tail_bwd(l, dx, out, y_a, y_b, proj, o, h_lru, gate, g_post, gw):
    s = dx.shape[0]
    tb = 256

    def body(dx_ref, out_ref, ya_ref, yb_ref, ma_ref, mb_ref, o_ref, ga_ref, h_ref, gl_ref, gt_ref, gp_ref,
             wpa_ref, wpb_ref, wo_ref,
             dout_ref, dya_ref, dyb_ref, rest_ref, do_ref, dh_ref, dgt_ref, dgp_ref):
        i = pl.program_id(0)
        ov = out_ref[...].astype(_F32)
        dxv = dx_ref[...]
        rstd = lax.rsqrt(jnp.mean(ov * ov, axis=-1, keepdims=True) + NORM_EPS)
        nv = ov * rstd
        s_dn = jnp.sum(dxv * nv, axis=0, keepdims=True)
        _zero_first(i, dgt_ref, dgp_ref)
        dgt_ref[...] += s_dn * gp_ref[...]
        dgp_ref[...] += s_dn * gt_ref[...]
        dn = dxv * (gt_ref[...] * gp_ref[...])
        d_out = (rstd * (dn - nv * jnp.mean(dn * nv, axis=-1, keepdims=True))).astype(_MXU)
        dout_ref[...] = d_out
        dz = lax.dot_general(d_out, wo_ref[...].reshape(D, D), _NT, preferred_element_type=_F32)
        ga = _sigmoid(ma_ref[...].astype(_F32))
        gb = _sigmoid(mb_ref[...].astype(_F32))
        dya = (dz * ga).astype(_MXU)
        dyb = (dz * gb).astype(_MXU)
        dya_ref[...] = dya
        dyb_ref[...] = dyb
        rest_ref[:, R_MA:R_MB] = (dz * ya_ref[...].astype(_F32) * ga * (1.0 - ga)).astype(rest_ref.dtype)
        rest_ref[:, R_MB:R_END] = (dz * yb_ref[...].astype(_F32) * gb * (1.0 - gb)).astype(rest_ref.dtype)
        daa = lax.dot_general(dya[:, 0:256], wpa_ref[0], _NT, preferred_element_type=_F32)
        for j in range(1, N_CHIPS):
            daa = daa + lax.dot_general(dya[:, j * 256:(j + 1) * 256], wpa_ref[j], _NT, preferred_element_type=_F32)
        dba = lax.dot_general(dyb, wpb_ref[...].reshape(D, D), _NT, preferred_element_type=_F32)
        gav = ga_ref[...].astype(_F32)
        sa = _sigmoid(gav)
        do_ref[...] = daa * (gav * sa)
        rest_ref[:, 0:R_U] = (daa * o_ref[...].astype(_F32) * (sa * (1.0 + gav * (1.0 - sa)))).astype(rest_ref.dtype)
        gl = gl_ref[...].astype(_F32)
        sl = _sigmoid(gl)
        dh_ref[...] = dba * (gl * sl)
        rest_ref[:, R_GLRU:R_MA] = (dba * h_ref[...].astype(_F32)
                                    * (sl * (1.0 + gl * (1.0 - sl)))).astype(rest_ref.dtype)

    v = _vec((1, D))
    r5, r10 = _rows(tb, ATT_W), _rows(tb, D)
    return _rowwise("tail_bwd", body, grid=s // tb,
                    ins=[(dx, r10), (out, r10), (y_a, r10), (y_b, r10), (proj, _rows(tb, D, CB_MA)),
                         (proj, _rows(tb, D, CB_MB)), (o, r5), (proj, _rows(tb, ATT_W, CB_GATT)), (h_lru, r10),
                         (proj, _rows(tb, D, CB_GLRU)), (gate, v), (g_post, v)]
                    + list(zip((gw["w_pa"], gw["w_pb"], gw["w_o"]), _weight_specs(l))),
                    outs=[(_sds((s, D), _MXU), r10), (_sds((s, D), _MXU), r10), (_sds((s, D), _MXU), r10),
                          (_sds((s, R_END), _MXU), _rows(tb, R_END)),
                          (_sds((s, ATT_W)), r5), (_sds((s, D)), r10), (_sds((1, D)), v), (_sds((1, D)), v)])


def _scan_bwd(dh, proj, conv_w, conv_b, h_lru, wt, b_rg, b_ig, lam):
    s = dh.shape[0]
    tb = 256
    n = s // tb

    def body(dh_ref, u_ref, up_ref, cw_ref, cb_ref, h_ref, hp_ref, wt_ref, brg_ref, big_ref, lam_ref,
             duc_ref, dwt_ref, dbrg_ref, dbig_ref, dlam_ref, carry, c_s, g_s):
        i = pl.program_id(0)

        @pl.when(i == 0)
        def _():
            carry[...] = jnp.zeros_like(carry)
            for acc_ref in (dwt_ref, dbrg_ref, dbig_ref, dlam_ref):
                acc_ref[...] = jnp.zeros_like(acc_ref)

        ucv = _conv(u_ref, up_ref, cw_ref, cb_ref, i == n - 1, tb)
        pre_r, pre_i = _gate_preacts(ucv, wt_ref)
        r, ig, sp, a, sq, inv_sq =_lru_gates(pre_r, pre_i, ucv, brg_ref[...], big_ref[...], lam_ref[...])
        row = lax.broadcasted_iota(jnp.int32, (tb, 1), 0)
        cv = jnp.where(row == tb - 1, 1.0, pltpu.roll(a, tb - 1, 0))
        gv = dh_ref[...]
        cv = cv.reshape(tb // 8, 8, D)
        gv = gv.reshape(tb // 8, 8, D)
        row8 = lax.broadcasted_iota(jnp.int32, (1, 8, 1), 1)
        for sh in (1, 2, 4):
            m = row8 < 8 - sh
            g_sh = pltpu.roll(gv, 8 - sh, 1)
            c_sh = pltpu.roll(cv, 8 - sh, 1)
            gv = jnp.where(m, gv + cv * g_sh, gv)
            cv = jnp.where(m, cv * c_sh, cv)
        c_s[...] = cv.reshape(tb, D)
        g_s[...] = gv.reshape(tb, D)

        def tile(k, state):
            rows = pl.ds(pl.multiple_of((tb // 8 - 1 - k) * 8, 8), 8)
            gt = g_s[rows, :] + c_s[rows, :] * state
            g_s[rows, :] = gt
            return jnp.broadcast_to(gt[0:1, :], (8, D))

        lax.fori_loop(0, tb // 8, tile, jnp.broadcast_to(carry[0:1, :], (8, D)), unroll=4)
        gv = g_s[...]
        carry[...] = (a * gv)[0:8]

        halo = jnp.where(i < n - 1, hp_ref[...].astype(_F32)[8:16], 0.0)
        h_prev = _shift_down(h_ref[...].astype(_F32), halo, 1, tb)
        d_a = gv * h_prev
        d_sq = gv * (ig * ucv)
        d_i = gv * sq * ucv
        d_la = d_a * a - d_sq * (a * a) * inv_sq
        d_r = d_la * (-LRU_C * sp)
        d_pre_r = d_r * r * (1.0 - r)
        d_pre_i = d_i * ig * (1.0 - ig)
        ucb = ucv.astype(_MXU)
        dpr = d_pre_r.astype(_MXU)
        dpi = d_pre_i.astype(_MXU)
        back = []
        for c in range(GATE_TILES):
            lanes = slice(128 * c, 128 * (c + 1))
            dp = jnp.concatenate([dpr[:, lanes], dpi[:, lanes]], axis=1)
            back.append(lax.dot_general(dp, wt_ref[c], _NT, preferred_element_type=_F32))
            dwt_ref[c] += lax.dot_general(ucb[:, lanes], dp, _TN, preferred_element_type=_F32)
        duc_ref[...] = gv * sq * ig + jnp.concatenate(back, axis=1)
        dbrg_ref[...] += jnp.sum(d_pre_r, axis=0, keepdims=True)
        dbig_ref[...] += jnp.sum(d_pre_i, axis=0, keepdims=True)
        lamv = lam_ref[...]
        dlam_ref[...] += jnp.sum(d_la * (-LRU_C * r), axis=0, keepdims=True) * (-_sigmoid(-lamv))

    v = _vec((1, D))
    rv = _rows(tb, D, 0, n)
    return _rowwise("scan_bwd", body, grid=n,
                    ins=[(dh, rv), (proj, _rows(tb, D, CB_U, n)), (proj, _halo_prev(tb, D, CB_U, n, rows=16)),
                         (conv_w, _vec((4, D))), (conv_b, v), (h_lru, rv), (h_lru, _halo_prev(tb, D, 0, n, rows=16)),
                         (wt, _vec((GATE_TILES, 128, 256))), (b_rg, v), (b_ig, v), (lam, v)],
                    outs=[(_sds((s, D)), rv), (_sds((GATE_TILES, 128, 256)), _vec((GATE_TILES, 128, 256))),
                          (_sds((1, D)), v), (_sds((1, D)), v), (_sds((1, D)), v)],
                    scratch=[pltpu.VMEM((8, D), _F32), pltpu.VMEM((tb, D), _F32), pltpu.VMEM((tb, D), _F32)])


def _conv_bwd(duc_a, proj, conv_w, rest):
    s = duc_a.shape[0]
    tb = 512
    n = s // tb
    hw = D // 2

    def body(da_ref, dan_ref, u_ref, up_ref, w_ref, rest_in, du_ref, dw_ref, dbias_ref):
        i = pl.program_id(1)
        duc = da_ref[...]
        nxt = jnp.where(i < n - 1, dan_ref[...], 0.0)
        u = u_ref[...].astype(_F32)
        halo = jnp.where(i > 0, up_ref[...].astype(_F32)[8:16], 0.0)
        du = duc * w_ref[0:1, :]
        dws = [jnp.sum(duc * u, axis=0, keepdims=True)]
        for j in range(1, 4):
            du = du + _shift_up(duc, nxt, j, tb) * w_ref[j:j + 1, :]
            dws.append(jnp.sum(duc * _shift_down(u, halo, j, tb), axis=0, keepdims=True))
        du_ref[...] = du.astype(du_ref.dtype)
        _zero_first(i, dw_ref, dbias_ref)
        for j in range(4):
            dw_ref[j:j + 1, :] += dws[j]
        dbias_ref[...] += jnp.sum(duc, axis=0, keepdims=True)

    r = pl.BlockSpec((tb, hw), lambda h, i: (i, h))
    nxt_spec = pl.BlockSpec((8, hw), lambda h, i: (jnp.minimum((i + 1) * (tb // 8), n * (tb // 8) - 1), h))
    return pl.pallas_call(
        body, name="conv_bwd", grid=(2, n),
        in_specs=[r, nxt_spec,
                  pl.BlockSpec((tb, hw), lambda h, i: (i, 2 * CB_U + h)),
                  pl.BlockSpec((16, hw), lambda h, i: (jnp.maximum(i * (tb // 16) - 1, 0), 2 * CB_U + h)),
                  pl.BlockSpec((4, hw), lambda h, i: (0, h)), pl.BlockSpec(memory_space=pl.ANY)],
        out_specs=[pl.BlockSpec((tb, hw), lambda h, i: (i, R_U // hw + h)),
                   pl.BlockSpec((4, hw), lambda h, i: (0, h)), pl.BlockSpec((1, hw), lambda h, i: (0, h))],
        out_shape=[_sds(rest.shape, rest.dtype), _sds((4, D)), _sds((1, D))],
        input_output_aliases={5: 0}, compiler_params=_params(2),
    )(duc_a, duc_a, proj, proj, conv_w, rest)


def _band_tiles(dil):
    tiles = []
    for rho in range(dil):
        for b in range(16 // dil):
            qs = rho + dil * BAND * b
            tiles.append((qs, QBLK + qs - dil * BAND, b))
    return tiles


def _strided(start, size, dil):
    return pl.ds(start, size, stride=dil) if dil > 1 else pl.ds(start, size)


def _band_mask(i, b):
    qi = lax.broadcasted_iota(jnp.int32, (BAND, 2 * BAND), 0)
    ki = lax.broadcasted_iota(jnp.int32, (BAND, 2 * BAND), 1)
    valid = (ki >= qi) & (ki <= qi + BAND)
    if b == 0:
        valid = valid & ((ki >= BAND) | (i > 0))
    return valid


def _attn_fwd(proj):
    s = proj.shape[0]
    n = s // QBLK
    scale = HEAD ** -0.5

    def body(*refs):
        q_refs, kp_refs, kc_refs, vp_refs, vc_refs = (refs[3 * t:3 * t + 3] for t in range(5))
        o_ref, lse_ref, qbuf, kbuf, vbuf = refs[15:20]
        accs, maxs, dens = refs[20:23], refs[23:26], refs[26:29]
        i = pl.program_id(1)
        for g, dil in enumerate(DILATIONS):
            qbuf[...] = q_refs[g][...].astype(_F32)
            kbuf[0:QBLK, :] = kp_refs[g][...].astype(_F32)
            kbuf[QBLK:2 * QBLK, :] = kc_refs[g][...].astype(_F32)
            vbuf[0:QBLK, :] = vp_refs[g][...].astype(_F32)
            vbuf[QBLK:2 * QBLK, :] = vc_refs[g][...].astype(_F32)
            for qs, ks, b in _band_tiles(dil):
                qsl = _strided(qs, BAND, dil)
                q = qbuf[qsl, :].astype(_MXU)
                kk = kbuf[_strided(ks, 2 * BAND, dil), :].astype(_MXU)
                vv = vbuf[_strided(ks, 2 * BAND, dil), :].astype(_MXU)
                sc = lax.dot_general(q, kk, _NT, preferred_element_type=_F32) * scale
                sc = jnp.where(_band_mask(i, b), sc, NEG_INF)
                m = jnp.max(sc, axis=-1, keepdims=True)
                p = jnp.exp(sc - m)
                accs[g][qsl, :] = jnp.dot(p.astype(_MXU), vv, preferred_element_type=_F32)
                maxs[g][qsl, :] = jnp.broadcast_to(m, (BAND, HEAD))
                dens[g][qsl, :] = jnp.broadcast_to(jnp.sum(p, axis=-1, keepdims=True), (BAND, HEAD))
        ms = [r[...] for r in maxs]
        mx = jnp.maximum(jnp.maximum(ms[0], ms[1]), ms[2])
        ws = [jnp.exp(m - mx) for m in ms]
        den = ws[0] * dens[0][...] + ws[1] * dens[1][...] + ws[2] * dens[2][...]
        o_ref[...] = ((ws[0] * accs[0][...] + ws[1] * accs[1][...] + ws[2] * accs[2][...]) / den).astype(o_ref.dtype)
        lse_ref[...] = mx + jnp.log(den)

    blk = (QBLK, HEAD)

    def spec(first_col, lag):
        specs = []
        for g in range(3):
            col = first_col + g * HEADS
            if lag:
                specs.append(pl.BlockSpec(blk, lambda j, i, col=col: (jnp.maximum(i - 1, 0), col + j)))
            else:
                specs.append(pl.BlockSpec(blk, lambda j, i, col=col: (i, col + j)))
        return specs

    out_spec = pl.BlockSpec(blk, lambda j, i: (i, j))
    return pl.pallas_call(
        body, name="attn_fwd", grid=(HEADS, n),
        in_specs=spec(0, False) + spec(12, True) + spec(12, False) + spec(24, True) + spec(24, False),
        out_specs=[out_spec] * 2, out_shape=[_sds((s, ATT_W), _MXU), _sds((s, ATT_W))],
        scratch_shapes=[pltpu.VMEM(blk, _F32)] + [pltpu.VMEM((2 * QBLK, HEAD), _F32)] * 2
        + [pltpu.VMEM(blk, _F32)] * 9,
        compiler_params=_params(2))(*([proj] * 15))


def _attn_bwd(proj, d_o, o, lse, g, into):
    s = proj.shape[0]
    dil = DILATIONS[g]
    n = s // QBLK
    scale = HEAD ** -0.5
    tiles = _band_tiles(dil)

    def body(*refs):
        q_ref, kp_ref, kc_ref, vp_ref, vc_ref, do_ref, o_ref, lse_ref = refs[0:8]
        dq_ref, dk_ref, dv_ref, kbuf, vbuf, dkbuf, dvbuf, dqbuf, qbuf, obuf = refs[-10:]
        i = pl.program_id(1)

        @pl.when(i == 0)
        def _():
            dkbuf[0:QBLK, :] = jnp.zeros((QBLK, HEAD), _F32)
            dvbuf[0:QBLK, :] = jnp.zeros((QBLK, HEAD), _F32)

        @pl.when(i < n)
        def _():
            qbuf[...] = q_ref[...].astype(_F32)
            obuf[...] = o_ref[...].astype(_F32)
            kbuf[0:QBLK, :] = kp_ref[...].astype(_F32)
            kbuf[QBLK:2 * QBLK, :] = kc_ref[...].astype(_F32)
            vbuf[0:QBLK, :] = vp_ref[...].astype(_F32)
            vbuf[QBLK:2 * QBLK, :] = vc_ref[...].astype(_F32)
            dkbuf[QBLK:2 * QBLK, :] = jnp.zeros((QBLK, HEAD), _F32)
            dvbuf[QBLK:2 * QBLK, :] = jnp.zeros((QBLK, HEAD), _F32)
            for qs, ks, b in tiles:
                qsl = _strided(qs, BAND, dil)
                ksl = _strided(ks, 2 * BAND, dil)
                q = qbuf[qsl, :].astype(_MXU)
                kk = kbuf[ksl, :].astype(_MXU)
                vv = vbuf[ksl, :].astype(_MXU)
                dov = do_ref[qsl, :]
                dd = jnp.sum(dov * obuf[qsl, :], axis=-1, keepdims=True)
                lse_t = lse_ref[qsl, :][:, 0:1]
                sc = lax.dot_general(q, kk, _NT, preferred_element_type=_F32) * scale
                p = jnp.where(_band_mask(i, b), jnp.exp(sc - lse_t), 0.0)
                dob = dov.astype(_MXU)
                dp = lax.dot_general(dob, vv, _NT, preferred_element_type=_F32)
                ds = (p * (dp - dd) * scale).astype(_MXU)
                dqbuf[qsl, :] = jnp.dot(ds, kk, preferred_element_type=_F32)
                dkbuf[ksl, :] += lax.dot_general(ds, q, _TN, preferred_element_type=_F32)
                dvbuf[ksl, :] += lax.dot_general(p.astype(_MXU), dob, _TN, preferred_element_type=_F32)
            dq_ref[...] = dqbuf[...].astype(dq_ref.dtype)

        dk_ref[...] = dkbuf[0:QBLK, :].astype(dk_ref.dtype)
        dv_ref[...] = dvbuf[0:QBLK, :].astype(dv_ref.dtype)
        dkbuf[0:QBLK, :] = dkbuf[QBLK:2 * QBLK, :]
        dvbuf[0:QBLK, :] = dvbuf[QBLK:2 * QBLK, :]

    blk = (QBLK, HEAD)
    cq, ck, cv = g * HEADS, 12 + g * HEADS, 24 + g * HEADS

    def cur(i):
        return jnp.minimum(i, n - 1)

    def prev(i):
        return jnp.maximum(jnp.minimum(i, n - 1) - 1, 0)

    own = pl.BlockSpec(blk, lambda j, i: (cur(i), j))
    own_out = pl.BlockSpec(blk, lambda j, i: (cur(i), cq + j))
    late_out = pl.BlockSpec(blk, lambda j, i: (jnp.maximum(i - 1, 0), cq + j))
    extra = [] if into is None else list(into)
    return pl.pallas_call(
        body, name="attn_bwd_d%d" % dil, grid=(HEADS, n + 1),
        in_specs=[pl.BlockSpec(blk, lambda j, i: (cur(i), cq + j)),
                  pl.BlockSpec(blk, lambda j, i: (prev(i), ck + j)),
                  pl.BlockSpec(blk, lambda j, i: (cur(i), ck + j)),
                  pl.BlockSpec(blk, lambda j, i: (prev(i), cv + j)),
                  pl.BlockSpec(blk, lambda j, i: (cur(i), cv + j)),
                  own, own, own] + [pl.BlockSpec(memory_space=pl.ANY)] * len(extra),
        out_specs=[own_out, late_out, late_out], out_shape=[_sds((s, QKV_W), _MXU)] * 3,
        input_output_aliases={8 + t: t for t in range(len(extra))},
        scratch_shapes=[pltpu.VMEM((2 * QBLK, HEAD), _F32)] * 4 + [pltpu.VMEM((QBLK, HEAD), _F32)] * 3,
        compiler_params=_params(2))(proj, proj, proj, proj, proj, d_o, o, lse, *extra)


_PARTS = ((0, 2), (2, 2), (4, 2), (6, 6))
_CHUNK = 768


def _d_x(name, parts, w_in, x, dx_out, g_pre, scale, blocks, into):
    s = parts[0].shape[0]
    nk = IN_W // _CHUNK
    first_block, n_blocks = blocks

    def body(*refs):
        p0, p1, p2, p3, w_ref, x_ref, dxo_ref, g_ref, sc_ref = refs[0:9]
        dx_ref, dsh_ref, dsc_ref, dg_ref, acc = refs[-5:]
        m = pl.program_id(0)
        k = pl.program_id(2)

        @pl.when(k == 0)
        def _():
            acc[...] = jnp.zeros_like(acc)

        @pl.when((k == 0) & (m == 0))
        def _():
            for ref in (dsh_ref, dsc_ref, dg_ref):
                ref[...] = jnp.zeros_like(ref)

        for p_ref, (first, cnt) in zip((p0, p1, p2, p3), _PARTS):
            @pl.when((k >= first) & (k < first + cnt))
            def _(p_ref=p_ref):
                acc[...] += lax.dot_general(p_ref[...].astype(_MXU), w_ref[...], _NT, preferred_element_type=_F32)

        @pl.when(k == nk - 1)
        def _():
            dhv = acc[...]
            xv = x_ref[...]
            rstd = lax.rsqrt(jnp.mean(xv * xv, axis=-1, keepdims=True) + NORM_EPS)
            xn = xv * rstd
            one_sc = 1.0 + sc_ref[...]
            s1 = jnp.sum(dhv * xn, axis=0, keepdims=True)
            dsh_ref[...] += jnp.sum(dhv, axis=0, keepdims=True)
            dsc_ref[...] += s1 * g_ref[...]
            dg_ref[...] += s1 * one_sc
            dxn = dhv * (g_ref[...] * one_sc)
            dx_ref[...] = dxo_ref[...] + rstd * (dxn - xn * jnp.mean(dxn * xn, axis=-1, keepdims=True))

    def part_spec(first, cnt):
        return pl.BlockSpec((1024, _CHUNK), lambda m, n, k: (first_block + m, jnp.clip(k - first, 0, cnt - 1)))

    rows = pl.BlockSpec((1024, D), lambda m, n, k: (first_block + m, 0))
    vec = pl.BlockSpec((1, D), lambda m, n, k: (0, 0))
    extra = [] if into is None else [into]
    return pl.pallas_call(
        body, name=name, grid=(n_blocks, 1, nk),
        in_specs=[part_spec(*p) for p in _PARTS]
        + [pl.BlockSpec((None, D, _CHUNK), lambda m, n, k: (k // 3, 0, k % 3)), rows, rows, vec, vec]
        + [pl.BlockSpec(memory_space=pl.ANY)] * len(extra),
        out_specs=[rows, vec, vec, vec], out_shape=[_sds((s, D)), _sds((1, D)), _sds((1, D)), _sds((1, D))],
        input_output_aliases={9: 0} if extra else {},
        scratch_shapes=[pltpu.VMEM((1024, D), _F32)], compiler_params=_params(3))(
            *parts, w_in, x, dx_out, g_pre, scale, *extra)


def _g_w_in(h_t, parts):
    s = h_t.shape[1]
    tk = 2048
    nk = s // tk

    def body(*refs):
        h_ref, p_refs = refs[0], refs[1:5]
        o_ref, acc = refs[-2], refs[-1]
        n = pl.program_id(1)
        k = pl.program_id(2)

        @pl.when(k == 0)
        def _():
            acc[...] = jnp.zeros_like(acc)

        for p_ref, (first, cnt) in zip(p_refs, _PARTS):
            @pl.when((n >= first) & (n < first + cnt))
            def _(p_ref=p_ref):
                acc[...] += jnp.dot(h_ref[...], p_ref[...].astype(_MXU), preferred_element_type=_F32)

        @pl.when(k == nk - 1)
        def _():
            o_ref[...] = acc[...]

    def part_spec(first, cnt):
        def index(m, n, k):
            row = jnp.where(n < first, 0, jnp.where(n >= first + cnt, nk - 1, k))
            return (row, jnp.clip(n - first, 0, cnt - 1))
        return pl.BlockSpec((tk, _CHUNK), index)

    return pl.pallas_call(
        body, name="g_w_in", grid=(1, IN_W // _CHUNK, nk),
        in_specs=[pl.BlockSpec((D, tk), lambda m, n, k: (0, k))] + [part_spec(*p) for p in _PARTS],
        out_specs=pl.BlockSpec((None, D, _CHUNK), lambda m, n, k: (n // 3, 0, n % 3)),
        out_shape=_sds((N_CHIPS, D, 2304)),
        scratch_shapes=[pltpu.VMEM((D, _CHUNK), _F32)], compiler_params=_params(3))(h_t, *parts)


def _layer_fwd(l, x, p, gw, late, target):
    if callable(gw["w_in"][l]):
        gw["w_in"][l] = gw["w_in"][l](x)
    proj, h_t = _proj(x, p["g_pre"], p["shift"], p["scale"], gw["w_in"][l])
    o, lse = _attn_fwd(proj)
    h_lru = _scan_fwd(proj, p["conv_w"], p["conv_b"], p["wt"], p["b_rg"], p["b_ig"], p["lam"])
    if late is not None:
        landed = dict(late(h_lru))
        gw["w_in"].append(landed.pop("w_in1"))
        gw.update(landed)
    a_att, b_act, y_a, y_b, z, out, *last = _tail_fwd(l, o, h_lru, proj, x, p["gate"], p["g_post"], gw, target)
    saved = dict(x=x, h_t=h_t, proj=proj, o=o, lse=lse, h_lru=h_lru, a_att=a_att, b_act=b_act,
                 y_a=y_a, y_b=y_b, z=z, out=out)
    return (last[0] if target is None else last), saved


def _layer_bwd(l, dx, p, gw, sv, hooks):
    s = dx.shape[0]
    nt = s // 2048
    proj = sv["proj"]
    gate, b_rg, g_pre = p["gate"], p["b_rg"], p["g_pre"]
    if hooks is not None:
        gate = gate + hooks[0]([dx])
    d_out, dy_a, dy_b, d_rest, d_o, dh_lru, d_gate, d_gpost = _tail_bwd(
        l, dx, sv["out"], sv["y_a"], sv["y_b"], proj, sv["o"], sv["h_lru"], gate, p["g_post"], gw)
    if hooks is not None:
        b_rg = b_rg + hooks[1]([d_out])

    def wgrad_rows(name, a, b):
        return _mm(name, a, b, _sds((N_CHIPS, 256, D)), grid=(1, 1, nt),
                   a_spec=pl.BlockSpec((D, 2048), lambda m, n, k: (0, k)),
                   b_spec=pl.BlockSpec((2048, D), lambda m, n, k: (k, 0)),
                   o_spec=pl.BlockSpec((N_CHIPS, 256, D), lambda m, n, k: (0, 0, 0)),
                   dims=_NN, acc_shape=(D, D))

    big = {}
    big["w_o"] = wgrad_rows("g_w_o", sv["z"], d_out)
    big["w_pa"] = _mm("g_w_pa", sv["a_att"], dy_a, _sds((N_CHIPS, ATT_W, 256)), grid=(1, 4, nt),
                      a_spec=pl.BlockSpec((ATT_W, 2048), lambda m, n, k: (0, k)),
                      b_spec=pl.BlockSpec((2048, 256), lambda m, n, k: (k, n)),
                      o_spec=pl.BlockSpec((None, ATT_W, 256), lambda m, n, k: (n, 0, 0)),
                      dims=_NN, acc_shape=(ATT_W, 256))
    big["w_pb"] = wgrad_rows("g_w_pb", sv["b_act"], dy_b)
    duc, g_wt, d_brg, d_big, d_lam = _scan_bwd(dh_lru, proj, p["conv_w"], p["conv_b"], sv["h_lru"], p["wt"], b_rg,
                                               p["b_ig"], p["lam"])
    g_wrg, g_wig = _gate_tile_grads(g_wt)
    d_rest, g_convw, g_convb = _conv_bwd(duc, proj, p["conv_w"], d_rest)
    dqkv = None
    for g in range(3):
        dqkv = _attn_bwd(proj, d_o, sv["o"], sv["lse"], g, dqkv)
    if hooks is not None:
        g_pre = g_pre + hooks[2]([dqkv[0]])
    parts = (dqkv[0], dqkv[1], dqkv[2], d_rest)
    big["w_in"] = _g_w_in(sv["h_t"], parts)
    nb = s // 1024
    if hooks is None:
        dx_in, d_shift, d_scale, d_gpre = _d_x("d_x", parts, gw["w_in"][l], sv["x"], dx, g_pre, p["scale"],
                                               (0, nb), None)
    else:
        first = _d_x("d_x_a", parts, gw["w_in"][l], sv["x"], dx, g_pre + hooks[3](big), p["scale"],
                     (0, nb // 2), None)
        second = _d_x("d_x_b", parts, gw["w_in"][l], sv["x"], dx, g_pre + hooks[4]([first[0]]), p["scale"],
                      (nb // 2, nb - nb // 2), first[0])
        dx_in = second[0]
        d_shift, d_scale, d_gpre = (a + b for a, b in zip(first[1:], second[1:]))
    small = dict(dmod=jnp.concatenate([d_shift, d_scale, d_gate], axis=1), g_pre=d_gpre, conv_w=g_convw,
                 conv_b=g_convb, w_rg=g_wrg, b_rg=d_brg, w_ig=g_wig, b_ig=d_big, lam=d_lam, g_post=d_gpost)
    return dx_in, small, big


_BIG = ("w_in", "w_pa", "w_pb", "w_o")


class _GradReduce:
    PAIR_CHUNKS = (2, 1, 1, 1)
    CHIP_CHUNKS = (2, 1, 1, 1)
    FILL_CHUNKS = (4, 1, 1, 1)

    def __init__(self, core, where):
        self.core, self.where = core, where
        self.finals = None

    def begin(self, l, big):
        n = len(_BIG)
        halves = [big[k].reshape(N_CHIPS, 2, big[k].shape[1] // 2, big[k].shape[2]) for k in _BIG]
        lands = [lax.empty((N_CHIPS,) + h.shape[2:], _F32) for h in halves]
        plan, nsem = _pair_plan(n, self.PAIR_CHUNKS)
        state = {}
        state["pair"] = _split_start("reduce_pair_start_%d" % l, halves + lands, plan, nsem, [])

        def started(after):
            return state["pair"][3][0, 0]

        def pair_done(after):
            send, recv, arrays, _ = state["pair"]
            arrays = _split_wait("reduce_pair_wait_%d" % l, send, recv, arrays, plan, after)
            sums = [_sum_pair("sum_pair_%s_%d" % (k, l), arrays[a], arrays[n + a], self.core, 128)
                    for a, k in enumerate(_BIG)]
            state["mine"] = [t[0] for t in sums]
            lands2 = [lax.empty(t[1].shape, _MXU) for t in sums]
            plan2, nsem2 = _chips_plan(n, self.CHIP_CHUNKS)
            state["plan2"] = plan2
            state["chips"] = _split_start("reduce_chips_start_%d" % l, [t[1] for t in sums] + lands2, plan2, nsem2, [])
            return state["chips"][3][0, 0]

        def chips_done(after):
            send, recv, arrays, _ = state["chips"]
            arrays = _split_wait("reduce_chips_wait_%d" % l, send, recv, arrays, state["plan2"], after)
            finals = [_sum_chips("sum_chips_%s_%d" % (k, l), state["mine"][a], arrays[n + a], self.where, l,
                                 None if self.finals is None else self.finals[a], 128)
                      for a, k in enumerate(_BIG)]
            plan3, nsem3 = _fill_plan(n, self.FILL_CHUNKS, l)
            state["plan3"] = plan3
            state["fill"] = _split_start("gather_halves_start_%d" % l, finals, plan3, nsem3, [])
            return state["fill"][3][0, 0]

        def finish(after):
            send, recv, arrays, _ = state["fill"]
            self.finals = _split_wait("gather_halves_wait_%d" % l, send, recv, arrays, state["plan3"], after)
            return self.finals

        self._finish = finish
        return [started, pair_done, chips_done]

    def finish(self, after):
        return self._finish(after)


def _local_step(x, target, small_p, w_in0, late, reducer, on_smalls):
    saved = []
    h = x
    gw = dict(w_in=[w_in0])
    h, sv = _layer_fwd(0, h, small_p[0], gw, late, None)
    saved.append(sv)
    (dy, sq), sv = _layer_fwd(1, h, small_p[1], gw, None, target)
    saved.append(sv)
    loss = 0.5 * jnp.sum(sq) / D
    smalls = [None, None]
    dx, smalls[1], big1 = _layer_bwd(1, dy, small_p[1], gw, saved[1], None)
    hooks1 = reducer.begin(1, big1)
    small_started = on_smalls(1, smalls[1])
    pair_started = hooks1[0]
    hooks1[0] = lambda after: pair_started(after) + small_started
    own = {}

    def layer0_ready(big0):
        reducer.finish([big0["w_in"]])
        own["hooks"] = reducer.begin(0, big0)
        return own["hooks"][0]([])

    dx, smalls[0], _ = _layer_bwd(0, dx, small_p[0], gw, saved[0],
                                  hooks1 + [layer0_ready, lambda after: own["hooks"][1](after)])
    on_smalls(0, smalls[0])

    def finish_reduce(after):
        own["hooks"][2](after)
        return reducer.finish(after)

    return loss, dx, smalls, finish_reduce


_SMALL_ROWS = 8 + 8 + 8 + 64 + 64
_SMALL_VECS = ("g_pre", "conv_b", "b_rg", "b_ig", "lam", "g_post")


def _pack_small(small):
    pad = lambda rows: jnp.zeros((rows, D), _F32)
    return jnp.concatenate(
        [small["dmod"].reshape(3, D), pad(5)] + [small[k] for k in _SMALL_VECS] + [pad(2)]
        + [small["conv_w"], pad(4), small["w_rg"].reshape(64, D), small["w_ig"].reshape(64, D)], axis=0)


def kernel(x, c, w_mod, b_mod, g_pre, w_in, conv_w, conv_b, w_rg, b_rg, w_ig, b_ig, lru_lambda, w_pa, w_pb, w_o, g_post, loss_target, m_w_mod, m_b_mod, m_g_pre, m_w_in, m_conv_w, m_conv_b, m_w_rg, m_b_rg, m_w_ig, m_b_ig, m_lru_lambda, m_w_pa, m_w_pb, m_w_o, m_g_post, v_w_mod, v_b_mod, v_g_pre, v_w_in, v_conv_w, v_conv_b, v_w_rg, v_b_rg, v_w_ig, v_b_ig, v_lru_lambda, v_w_pa, v_w_pb, v_w_o, v_g_post):
    xi, yi, ci = lax.axis_index("x"), lax.axis_index("y"), lax.axis_index("c")
    chip = 2 * xi + yi
    dev = 4 * xi + 2 * yi + ci
    mcols = w_mod.shape[2]

    pack1 = jnp.concatenate([jnp.broadcast_to(c, (8, D)),
                             jnp.pad(conv_w.reshape(8, 256), ((0, 0), (0, D - 256)))], axis=0)
    g1 = _exchange("gather_cond", [pack1], "xyc", False)[0]
    c_all = g1[:, 0, :]
    conv_w_full = jnp.transpose(g1[0::2, 8:16, 0:256], (1, 0, 2)).reshape(2, 4, D)

    b_cols = lax.dynamic_slice(b_mod, (0, chip * mcols), (2, mcols)).reshape(2, 1, mcols)
    mod_loc = _mod_fwd(c_all, w_mod, b_cols)
    g2 = _exchange("gather_mod", [mod_loc.reshape(16, mcols)], "xyc", False)[0]
    mod_full = jnp.transpose(g2[0::2], (1, 0, 2)).reshape(2, 8, 3 * D)
    mod_me = lax.dynamic_index_in_dim(mod_full, dev, axis=1, keepdims=False)

    wb_in = _cast("cast_w_in", w_in.reshape(2 * D, 2304), 256).reshape(2, D, 2304)
    late_src = [wb_in[1], _cast("cast_w_pa", w_pa.reshape(2 * ATT_W, 256), 256).reshape(2, ATT_W, 256),
                _cast("cast_w_pb", w_pb.reshape(512, D), 256).reshape(2, 256, D),
                _cast("cast_w_o", w_o.reshape(512, D), 256).reshape(2, 256, D)]
    late_chunks = [4, 2, 2, 2]
    w_in0 = _gather_weights([wb_in[0].reshape(2, D // 2, 2304)], [2])[0].reshape(N_CHIPS, D, 2304)
    chip1 = jnp.reshape(chip, (1,)).astype(jnp.int32)
    lands = [_own_slot("own_slot_" + k, a, chip1, 256) for k, a in zip(("w_in", "w_pa", "w_pb", "w_o"), late_src)]
    plan_a, nsem_a = _gather_plan(3, late_chunks[1:])
    send_a, recv_a, arrays_a, token_a = _split_start(
        "late_gather_start_a", late_src[1:] + lands[1:], plan_a, nsem_a, [w_in0, mod_me])
    plan_b, nsem_b = _gather_plan(1, late_chunks[:1])
    send_b, recv_b, arrays_b, token_b = _split_start(
        "late_gather_start_b", late_src[:1] + lands[:1], plan_b, nsem_b, [w_in0, mod_me, arrays_a[0]])
    token = token_a + token_b

    def late(after):
        got = _split_wait("late_gather_wait_a", send_a, recv_a, arrays_a, plan_a, [after])[3:]
        w_in1 = lambda later: _split_wait("late_gather_wait_b", send_b, recv_b, arrays_b, plan_b, [later])[1]
        return dict(w_in1=w_in1, w_pa=got[0], w_pb=got[1], w_o=got[2])

    small_p = []
    for l in range(2):
        gates = _gate_tiles(w_rg[l], w_ig[l]).astype(_MXU)
        small_p.append(dict(
            shift=mod_me[l:l + 1, 0:D], scale=mod_me[l:l + 1, D:2 * D], gate=mod_me[l:l + 1, 2 * D:3 * D],
            g_pre=g_pre[l:l + 1], conv_w=conv_w_full[l], conv_b=conv_b[l:l + 1], wt=gates,
            b_rg=b_rg[l:l + 1], b_ig=b_ig[l:l + 1], lam=lru_lambda[l:l + 1], g_post=g_post[l:l + 1]))

    small_p[0]["shift"] = small_p[0]["shift"] + token[0, 0]

    core = jnp.reshape(ci, (1,)).astype(jnp.int32)
    where = jnp.stack([chip, ci]).astype(jnp.int32)
    dev1 = jnp.reshape(dev, (1,)).astype(jnp.int32)
    small_plan, small_nsem = _all_plan()
    small_state = {}

    def on_smalls(l, small):
        pack = _pack_small(small)
        land = _own_slot("own_small_%d" % l, pack, dev1, _SMALL_ROWS, slots=8)
        small_state[l] = _split_start("gather_small_start_%d" % l, [pack, land], small_plan, small_nsem, [])
        return small_state[l][3][0, 0]

    def small_done(l, after):
        send, recv, arrays, _ = small_state[l]
        return _split_wait("gather_small_wait_%d" % l, send, recv, arrays, small_plan, after)[1]

    loss_loc, dx, _, finish_reduce = _local_step(x[0], loss_target[0], small_p, w_in0, late,
                                                 _GradReduce(core, where), on_smalls)
    loss = lax.psum(loss_loc, ("x", "y", "c"))
    grad_x = dx[None]
    reduced = finish_reduce([dx, small_state[0][2][0]])
    g_big ={k: a.reshape(2, 2 * a.shape[2], a.shape[3]) for k, a in zip(_BIG, reduced)}

    weights = dict(w_mod=w_mod, b_mod=b_mod, g_pre=g_pre, w_in=w_in, conv_w=conv_w, conv_b=conv_b, w_rg=w_rg,
                   b_rg=b_rg, w_ig=w_ig, b_ig=b_ig, lru_lambda=lru_lambda, w_pa=w_pa, w_pb=w_pb, w_o=w_o,
                   g_post=g_post)
    ms = dict(w_mod=m_w_mod, b_mod=m_b_mod, g_pre=m_g_pre, w_in=m_w_in, conv_w=m_conv_w, conv_b=m_conv_b,
              w_rg=m_w_rg, b_rg=m_b_rg, w_ig=m_w_ig, b_ig=m_b_ig, lru_lambda=m_lru_lambda, w_pa=m_w_pa,
              w_pb=m_w_pb, w_o=m_w_o, g_post=m_g_post)
    vs = dict(w_mod=v_w_mod, b_mod=v_b_mod, g_pre=v_g_pre, w_in=v_w_in, conv_w=v_conv_w, conv_b=v_conv_b,
              w_rg=v_w_rg, b_rg=v_b_rg, w_ig=v_w_ig, b_ig=v_b_ig, lru_lambda=v_lru_lambda, w_pa=v_w_pa,
              w_pb=v_w_pb, w_o=v_w_o, g_post=v_g_post)
    flat = dict(w_mod=(2 * D, mcols, 256), b_mod=(2, 3 * D, 2), g_pre=(2, D, 2), w_in=(2 * D, 2304, 256),
                conv_w=(8, 256, 8), conv_b=(2, D, 2), w_rg=(128, D, 128), b_rg=(2, D, 2), w_ig=(128, D, 128),
                b_ig=(2, D, 2), lru_lambda=(2, D, 2), w_pa=(2 * ATT_W, 256, 256), w_pb=(512, D, 256),
                w_o=(512, D, 256), g_post=(2, D, 2))
    order = ("w_mod", "b_mod", "g_pre", "w_in", "conv_w", "conv_b", "w_rg", "b_rg", "w_ig", "b_ig",
             "lru_lambda", "w_pa", "w_pb", "w_o", "g_post")

    def adam(k, g):
        rows, cols, tb = flat[k]
        return _adamw("adamw_" + k, weights[k].reshape(rows, cols), g.reshape(rows, cols),
                      ms[k].reshape(rows, cols), vs[k].reshape(rows, cols), tb)

    stepped = {k: adam(k, g_big[k]) for k in _BIG}

    g3 = [small_done(l, [stepped["w_in"][0]]) for l in range(2)]
    tot =[_sum_lead("sum_small_%d" % l, g3[l], _SMALL_ROWS) for l in range(2)]
    dmod_all = jnp.stack([g3[l][:, 0:3, :].reshape(8, 3 * D) for l in range(2)], axis=0)
    dm_cols = lax.dynamic_slice(dmod_all, (0, 0, chip * mcols), (2, 8, mcols))
    g_w_mod = _mod_bwd(jnp.transpose(c_all), dm_cols)
    both = lambda first, rows: jnp.stack([tot[l][first:first + rows] for l in range(2)], axis=0)
    vec = both(8, 6)
    grads = dict(
        w_mod=g_w_mod, b_mod=both(0, 3).reshape(2, 3 * D), g_pre=vec[:, 0], w_in=g_big["w_in"],
        conv_w=lax.dynamic_slice(both(16, 4), (0, 0, chip * 256), (2, 4, 256)), conv_b=vec[:, 1],
        w_rg=both(24, 64).reshape(2, 16, 64, 64), b_rg=vec[:, 2], w_ig=both(88, 64).reshape(2, 16, 64, 64),
        b_ig=vec[:, 3], lru_lambda=vec[:, 4], w_pa=g_big["w_pa"], w_pb=g_big["w_pb"], w_o=g_big["w_o"],
        g_post=vec[:, 5])
    for k in order:
        if k not in stepped:
            stepped[k] = adam(k, grads[k])
    deltas, new_m, new_v = ([stepped[k][t].reshape(weights[k].shape) for k in order] for t in range(3))
    return (loss, grad_x, *[grads[k].reshape(weights[k].shape) for k in order], *deltas, *new_m, *new_v)
```

```python
import functools

import jax
import jax.numpy as jnp
from jax import lax
from jax.experimental import pallas as pl
from jax.experimental.pallas import tpu as pltpu

_F32 = jnp.float32
_MXU = jnp.bfloat16
_VMEM_LIMIT = 56 * 1024 * 1024
_MESH = pl.DeviceIdType.MESH

D = 1024
HEAD = 128
HEADS = 4
ATT_W = 512
QKV_W = 1536
IN_W = 9216
DILATIONS = (1, 4, 16)
BAND = 128
QBLK = BAND * 16
NORM_EPS = 1e-6
NEG_INF = -1e30
LRU_C = 8.0
N_CHIPS = 4
CB_GATT = 4608 // 512
CB_U, CB_GLRU, CB_MA, CB_MB = 5, 6, 7, 8
R_U, R_GLRU, R_MA, R_MB, R_END = 512, 1536, 2560, 3584, 4608

ADAM_LR, ADAM_B1, ADAM_B2, ADAM_EPS, ADAM_WD, ADAM_STEP = 0.001, 0.9, 0.999, 1e-08, 0.01, 10


def _params(ngrid):
    return pltpu.CompilerParams(dimension_semantics=("arbitrary",) * ngrid, vmem_limit_bytes=_VMEM_LIMIT)


def _sigmoid(v):
    return 0.5 * jnp.tanh(0.5 * v) + 0.5


_GROUPS = {
    "c": [(0, 0, 1)],
    "xy": [(1, 0, 0), (0, 1, 0), (1, 1, 0)],
    "xyc": [(0, 0, 1), (0, 1, 0), (0, 1, 1), (1, 0, 0), (1, 0, 1), (1, 1, 0), (1, 1, 1)],
}


def _rank(group, px, py, pc):
    if group == "c":
        return pc
    if group == "xy":
        return 2 * px + py
    return 4 * px + 2 * py + pc


def _flip(rel, x, y, c):
    dx, dy, dc = rel
    return (1 - x if dx else x, 1 - y if dy else y, 1 - c if dc else c)


def _pieces(ref, nchunk):
    step = ref.shape[0] // nchunk
    return [ref.at[pl.ds(q * step, step)] for q in range(nchunk)]


def _exchange(name, srcs, group, scatter, *, local=True, nchunks=None):
    rels = _GROUPS[group]
    gsize = len(rels) + 1
    n = len(srcs)
    nchunks = nchunks or [1] * n
    blks = [s.shape[1:] if scatter else s.shape for s in srcs]
    slotted = local or gsize > 2
    base = [sum(nchunks[:a]) for a in range(n)]
    tot = sum(nchunks)

    def body(*refs):
        src_refs, out_refs = refs[:n], refs[n:2 * n]
        send_sems, recv_sems, loc_sems = refs[2 * n:]
        x, y, c = lax.axis_index("x"), lax.axis_index("y"), lax.axis_index("c")
        me = _rank(group, x, y, c)
        copies = []
        for a in range(n):
            def part(r, a=a):
                return src_refs[a].at[r] if scatter else src_refs[a]
            dst = out_refs[a].at[me] if slotted else out_refs[a]
            if local:
                for q, (s_, d_) in enumerate(zip(_pieces(part(me), nchunks[a]), _pieces(dst, nchunks[a]))):
                    loc = pltpu.make_async_copy(s_, d_, loc_sems.at[base[a] + q])
                    loc.start()
                    copies.append(loc)
            for k, rel in enumerate(rels):
                peer = _flip(rel, x, y, c)
                for q, (s_, d_) in enumerate(zip(_pieces(part(_rank(group, *peer)), nchunks[a]),
                                                 _pieces(dst, nchunks[a]))):
                    cp = pltpu.make_async_remote_copy(
                        src_ref=s_, dst_ref=d_, send_sem=send_sems.at[(base[a] + q) * len(rels) + k],
                        recv_sem=recv_sems.at[(base[a] + q) * len(rels) + k],
                        device_id=peer, device_id_type=_MESH)
                    cp.start()
                    copies.append(cp)
        for cp in copies:
            cp.wait()

    any_spec = pl.BlockSpec(memory_space=pl.ANY)
    lead = (gsize,) if slotted else ()
    return pl.pallas_call(
        body, name=name,
        out_shape=[jax.ShapeDtypeStruct(lead + tuple(b), s.dtype) for b, s in zip(blks, srcs)],
        in_specs=[any_spec] * n, out_specs=[any_spec] * n,
        scratch_shapes=[pltpu.SemaphoreType.DMA((tot * len(rels),)), pltpu.SemaphoreType.DMA((tot * len(rels),)),
                        pltpu.SemaphoreType.DMA((tot,))],
    )(*srcs)


def _gather_weights(wb, nchunks):
    n = len(wb)
    rels = _GROUPS["xy"]
    base = [sum(nchunks[:a]) for a in range(n)]
    tot = sum(nchunks)

    def body(*refs):
        src_refs, out_refs = refs[:n], refs[n:2 * n]
        ici_send, ici_recv, d2d_send, d2d_recv, loc_sems = refs[2 * n:]
        x, y, c = lax.axis_index("x"), lax.axis_index("y"), lax.axis_index("c")
        me = 2 * x + y
        waits = []
        for a in range(n):
            for l in range(2):
                for q, (s_, d_) in enumerate(zip(_pieces(src_refs[a].at[l], nchunks[a]),
                                                 _pieces(out_refs[a].at[me, l], nchunks[a]))):
                    loc = pltpu.make_async_copy(s_, d_, loc_sems.at[(base[a] + q) * 2 + l])
                    loc.start()
                    waits.append(loc)
        first = []
        for a in range(n):
            for k, rel in enumerate(rels):
                px, py, _ = _flip(rel, x, y, c)
                for q, (s_, d_) in enumerate(zip(_pieces(src_refs[a].at[c], nchunks[a]),
                                                 _pieces(out_refs[a].at[me, c], nchunks[a]))):
                    sem = (base[a] + q) * 3 + k
                    cp = pltpu.make_async_remote_copy(src_ref=s_, dst_ref=d_, send_sem=ici_send.at[sem],
                                                      recv_sem=ici_recv.at[sem], device_id=(px, py, c),
                                                      device_id_type=_MESH)
                    cp.start()
                    first.append(cp)
        second = []
        for a in range(n):
            for k, rel in enumerate(rels):
                px, py, _ = _flip(rel, x, y, c)
                for q, blk in enumerate(_pieces(out_refs[a].at[2 * px + py, c], nchunks[a])):
                    sem = (base[a] + q) * 3 + k
                    landed = pltpu.make_async_remote_copy(src_ref=blk, dst_ref=blk, send_sem=ici_send.at[sem],
                                                          recv_sem=ici_recv.at[sem], device_id=(px, py, c),
                                                          device_id_type=_MESH)
                    landed.wait_recv()
                    cp = pltpu.make_async_remote_copy(src_ref=blk, dst_ref=blk, send_sem=d2d_send.at[sem],
                                                      recv_sem=d2d_recv.at[sem], device_id=(x, y, 1 - c),
                                                      device_id_type=_MESH)
                    cp.start()
                    second.append(cp)
        for cp in first:
            cp.wait_send()
        for cp in second:
            cp.wait_send()
        for a in range(n):
            for k, rel in enumerate(rels):
                px, py, _ = _flip(rel, x, y, c)
                for q, blk in enumerate(_pieces(out_refs[a].at[2 * px + py, 1 - c], nchunks[a])):
                    sem = (base[a] + q) * 3 + k
                    pltpu.make_async_remote_copy(src_ref=blk, dst_ref=blk, send_sem=d2d_send.at[sem],
                                                 recv_sem=d2d_recv.at[sem], device_id=(x, y, 1 - c),
                                                 device_id_type=_MESH).wait_recv()
        for cp in waits:
            cp.wait()

    any_spec = pl.BlockSpec(memory_space=pl.ANY)
    return pl.pallas_call(
        body, name="gather_weights",
        out_shape=[jax.ShapeDtypeStruct((N_CHIPS,) + a.shape, a.dtype) for a in wb],
        in_specs=[any_spec] * n, out_specs=[any_spec] * n,
        scratch_shapes=[pltpu.SemaphoreType.DMA((tot * 3,))] * 4 + [pltpu.SemaphoreType.DMA((tot * 2,))],
    )(*wb)


_HBM = pl.BlockSpec(memory_space=pltpu.HBM)
_SEM = pl.BlockSpec(memory_space=pltpu.SEMAPHORE)
_EFFECT = pltpu.SideEffectType.DATAFLOW_SIDE_EFFECTING


def _own_slot(name, src, chip, tb, slots=N_CHIPS):
    rows, cols = src.shape[-2:]
    lead = src.shape[:-2]
    flat = src.reshape((-1, cols))

    def body(s_ref, a_ref, o_ref):
        o_ref[...] = a_ref[...]

    grid_spec = pltpu.PrefetchScalarGridSpec(
        num_scalar_prefetch=1, grid=(flat.shape[0] // tb,),
        in_specs=[pl.BlockSpec((tb, cols), lambda i, s: (i, 0))],
        out_specs=pl.BlockSpec((None, tb, cols), lambda i, s: (s[0], i, 0)))
    out = pl.pallas_call(body, name=name, grid_spec=grid_spec,
                         out_shape=jax.ShapeDtypeStruct((slots,) + flat.shape, src.dtype),
                         compiler_params=_params(1))(chip, flat)
    return out.reshape((slots,) + lead + (rows, cols))


def _numbered(pairs, peer, send_sems, recv_sems, first):
    return [pltpu.make_async_remote_copy(src_ref=s_, dst_ref=d_, send_sem=send_sems.at[first + q],
                                         recv_sem=recv_sems.at[first + q], device_id=peer, device_id_type=_MESH)
            for q, (s_, d_) in enumerate(pairs)]


def _gather_plan(n, nchunks):
    def plan(refs, send_sems, recv_sems):
        x, y, c = lax.axis_index("x"), lax.axis_index("y"), lax.axis_index("c")
        me = 2 * x + y
        copies = []
        for a in range(n):
            for rel in _GROUPS["xy"]:
                px, py, _ = _flip(rel, x, y, c)
                pairs = list(zip(_pieces(refs[a], nchunks[a]), _pieces(refs[n + a].at[me], nchunks[a])))
                copies += _numbered(pairs, (px, py, c), send_sems, recv_sems, len(copies))
        return copies
    return plan, 3 * sum(nchunks)


def _all_plan():
    def plan(refs, send_sems, recv_sems):
        x, y, c = lax.axis_index("x"), lax.axis_index("y"), lax.axis_index("c")
        me = 4 * x + 2 * y + c
        copies = []
        for rel in _GROUPS["xyc"]:
            copies += _numbered([(refs[0], refs[1].at[me])], _flip(rel, x, y, c), send_sems, recv_sems, len(copies))
        return copies
    return plan, len(_GROUPS["xyc"])


def _pair_plan(n, nchunks):
    def plan(refs, send_sems, recv_sems):
        x, y, c = lax.axis_index("x"), lax.axis_index("y"), lax.axis_index("c")
        copies = []
        for a in range(n):
            for j in range(N_CHIPS):
                pairs = list(zip(_pieces(refs[a].at[j, 1 - c], nchunks[a]), _pieces(refs[n + a].at[j], nchunks[a])))
                copies += _numbered(pairs, (x, y, 1 - c), send_sems, recv_sems, len(copies))
        return copies
    return plan, N_CHIPS * sum(nchunks)


def _chips_plan(n, nchunks):
    def plan(refs, send_sems, recv_sems):
        x, y, c = lax.axis_index("x"), lax.axis_index("y"), lax.axis_index("c")
        me = 2 * x + y
        copies = []
        for a in range(n):
            for rel in _GROUPS["xy"]:
                px, py, _ = _flip(rel, x, y, c)
                pairs = list(zip(_pieces(refs[a].at[2 * px + py], nchunks[a]), _pieces(refs[n + a].at[me], nchunks[a])))
                copies += _numbered(pairs, (px, py, c), send_sems, recv_sems, len(copies))
        return copies
    return plan, 3 * sum(nchunks)


def _fill_plan(n, nchunks, l):
    def plan(refs, send_sems, recv_sems):
        x, y, c = lax.axis_index("x"), lax.axis_index("y"), lax.axis_index("c")
        copies = []
        for a in range(n):
            blk = _pieces(refs[a].at[l, c], nchunks[a])
            copies += _numbered(list(zip(blk, blk)), (x, y, 1 - c), send_sems, recv_sems, len(copies))
        return copies
    return plan, sum(nchunks)


def _split_start(name, arrays, plan, nsem, after):
    n = len(arrays)
    na = len(after)

    def body(*refs):
        send_sems, recv_sems = refs[n + na], refs[n + na + 1]
        token = refs[-1]
        for cp in plan(refs[:n], send_sems, recv_sems):
            cp.start()
        token[...] = jnp.zeros_like(token)

    hbm = [pltpu.HBM(a.shape, a.dtype) for a in arrays]
    outs = pl.pallas_call(
        body, name=name,
        out_shape=(pltpu.SemaphoreType.DMA((nsem,)), pltpu.SemaphoreType.DMA((nsem,)), *hbm, _sds((8, 128))),
        in_specs=[_HBM] * n + [pl.BlockSpec(memory_space=pl.ANY)] * na,
        out_specs=(_SEM, _SEM, *([_HBM] * n), pl.BlockSpec(memory_space=pltpu.VMEM)),
        input_output_aliases={i: 2 + i for i in range(n)},
        compiler_params=pltpu.CompilerParams(has_side_effects=_EFFECT),
    )(*[pltpu.with_memory_space_constraint(a, pltpu.HBM) for a in arrays], *after)
    return outs[0], outs[1], list(outs[2:2 + n]), outs[-1]


def _split_wait(name, send_sems, recv_sems, arrays, plan, after):
    n = len(arrays)

    def body(*refs):
        for cp in plan(refs[:n], refs[n], refs[n + 1]):
            cp.wait_send()
            cp.wait_recv()

    hbm = [pltpu.HBM(a.shape, a.dtype) for a in arrays]
    return list(pl.pallas_call(
        body, name=name, out_shape=tuple(hbm),
        in_specs=[_HBM] * n + [_SEM, _SEM] + [pl.BlockSpec(memory_space=pl.ANY)] * len(after),
        out_specs=tuple([_HBM] * n), input_output_aliases={i: i for i in range(n)},
        compiler_params=pltpu.CompilerParams(has_side_effects=_EFFECT),
    )(*arrays, send_sems, recv_sems, *after))


def _mm(name, a, b, out_sds, *, grid, a_spec, b_spec, o_spec, dims, acc_shape, into=None):
    nk = grid[2]

    def body(*refs):
        a_ref, b_ref = refs[0], refs[1]
        o_ref, acc = refs[-2], refs[-1]
        k = pl.program_id(2)
        part = lax.dot_general(a_ref[...].astype(_MXU), b_ref[...].astype(_MXU), dims,
                               preferred_element_type=_F32)
        if nk == 1:
            o_ref[...] = part.astype(o_ref.dtype)
            return

        @pl.when(k == 0)
        def _():
            acc[...] = part

        @pl.when(k > 0)
        def _():
            acc[...] += part

        @pl.when(k == nk - 1)
        def _():
            o_ref[...] = acc[...].astype(o_ref.dtype).reshape(o_ref.shape)

    if nk == 1:
        acc_shape = (8, 128)
    in_specs = [a_spec, b_spec]
    args = [a, b]
    aliases = {}
    if into is not None:
        in_specs.append(pl.BlockSpec(memory_space=pl.ANY))
        args.append(into)
        aliases = {2: 0}
    return pl.pallas_call(
        body, name=name, grid=grid, in_specs=in_specs, out_specs=o_spec, out_shape=out_sds,
        scratch_shapes=[pltpu.VMEM(acc_shape, _F32)], input_output_aliases=aliases,
        compiler_params=_params(3))(*args)


_NN = (((1,), (0,)), ((), ()))
_NT = (((1,), (1,)), ((), ()))
_TN = (((0,), (0,)), ((), ()))


def _rowwise(name, body, *, grid, ins, outs, scratch=()):
    return pl.pallas_call(
        body, name=name, grid=(grid,), in_specs=[s for _, s in ins], out_specs=[s for _, s in outs],
        out_shape=[o for o, _ in outs], scratch_shapes=list(scratch),
        compiler_params=_params(1))(*[a for a, _ in ins])


def _rows(tb, w, cb=0, n=None):
    if n is None:
        return pl.BlockSpec((tb, w), lambda i: (i, cb))
    return pl.BlockSpec((tb, w), lambda i: (n - 1 - i, cb))


def _vec(shape):
    return pl.BlockSpec(shape, lambda i: (0,) * len(shape))


def _halo_prev(tb, w, cb=0, n=None, rows=8):
    if n is None:
        return pl.BlockSpec((rows, w), lambda i: (jnp.maximum(i * (tb // rows) - 1, 0), cb))
    return pl.BlockSpec((rows, w), lambda i: (jnp.maximum((n - 1 - i) * (tb // rows) - 1, 0), cb))


def _halo_next(tb, w, n, cb=0):
    return pl.BlockSpec((8, w), lambda i: (jnp.minimum((i + 1) * (tb // 8), n * (tb // 8) - 1), cb))


def _sds(shape, dtype=_F32):
    return jax.ShapeDtypeStruct(shape, dtype)


def _cast(name, a, tb):
    rows, cols = a.shape

    def body(a_ref, o_ref):
        o_ref[...] = a_ref[...].astype(o_ref.dtype)

    return _rowwise(name, body, grid=rows // tb, ins=[(a, _rows(tb, cols))],
                    outs=[(_sds((rows, cols), _MXU), _rows(tb, cols))])[0]


def _sum_lead(name, a, tb):
    g, rows, cols = a.shape

    def body(a_ref, o_ref):
        acc = a_ref[0]
        for k in range(1, g):
            acc = acc + a_ref[k]
        o_ref[...] = acc

    return _rowwise(name, body, grid=rows // tb,
                    ins=[(a, pl.BlockSpec((g, tb, cols), lambda i: (0, i, 0)))],
                    outs=[(_sds((rows, cols)), _rows(tb, cols))])[0]


def _sum_pair(name, mine, theirs, core, tb):
    nj, _, rows, cols = mine.shape

    def body(s_ref, a_ref, b_ref, o_ref, ob_ref):
        t = a_ref[...] + b_ref[...]
        o_ref[...] = t
        ob_ref[...] = t.astype(ob_ref.dtype)

    blk = pl.BlockSpec((None, tb, cols), lambda j, i, s: (j, i, 0))
    grid_spec = pltpu.PrefetchScalarGridSpec(
        num_scalar_prefetch=1, grid=(nj, rows // tb),
        in_specs=[pl.BlockSpec((None, None, tb, cols), lambda j, i, s: (j, s[0], i, 0)), blk],
        out_specs=[blk, blk])
    return pl.pallas_call(body, name=name, grid_spec=grid_spec,
                          out_shape=[_sds((nj, rows, cols)), _sds((nj, rows, cols), _MXU)],
                          compiler_params=_params(2))(core, mine, theirs)


def _sum_chips(name, mine, theirs, where, l, into, tb):
    _, rows, cols = mine.shape
    extra = [] if into is None else [into]

    def body(*refs):
        a_ref, b1_ref, b2_ref, b3_ref = refs[1:5]
        o_ref = refs[-1]
        o_ref[...] = ((a_ref[...] + b1_ref[...].astype(_F32)) + b2_ref[...].astype(_F32)) + b3_ref[...].astype(_F32)

    def slot(k):
        return pl.BlockSpec((None, tb, cols), lambda i, s: (jnp.bitwise_xor(s[0], k), i, 0))

    grid_spec = pltpu.PrefetchScalarGridSpec(
        num_scalar_prefetch=1, grid=(rows // tb,),
        in_specs=[slot(0), slot(1), slot(2), slot(3)] + [pl.BlockSpec(memory_space=pl.ANY)] * len(extra),
        out_specs=pl.BlockSpec((None, None, tb, cols), lambda i, s: (l, s[1], i, 0)))
    return pl.pallas_call(body, name=name, grid_spec=grid_spec, out_shape=_sds((2, 2, rows, cols)),
                          input_output_aliases={5: 0} if extra else {},
                          compiler_params=_params(1))(where, mine, theirs, theirs, theirs, *extra)


def _adamw(name, w, g, m, v, tb):
    rows, cols = w.shape
    c1 = 1.0 - ADAM_B1 ** ADAM_STEP
    c2 = 1.0 - ADAM_B2 ** ADAM_STEP

    def body(w_ref, g_ref, m_ref, v_ref, d_ref, nm_ref, nv_ref):
        gv = g_ref[...]
        nm = ADAM_B1 * m_ref[...] + (1.0 - ADAM_B1) * gv
        nv = ADAM_B2 * v_ref[...] + (1.0 - ADAM_B2) * (gv * gv)
        d_ref[...] = -ADAM_LR * ((nm / c1) / (jnp.sqrt(nv / c2) + ADAM_EPS) + ADAM_WD * w_ref[...])
        nm_ref[...] = nm
        nv_ref[...] = nv

    spec = _rows(tb, cols)
    return _rowwise(name, body, grid=rows // tb, ins=[(w, spec), (g, spec), (m, spec), (v, spec)],
                    outs=[(_sds((rows, cols)), spec)] * 3)


def _mod_fwd(c_all, w_mod, b_cols):
    cols = w_mod.shape[2]

    def body(c_ref, w_ref, b_ref, o_ref):
        cv = c_ref[...]
        sc = (cv * _sigmoid(cv)).astype(_MXU)
        o_ref[...] = jnp.dot(sc, w_ref[...].astype(_MXU), preferred_element_type=_F32) + b_ref[...]

    return pl.pallas_call(
        body, name="mod_fwd", grid=(2,),
        in_specs=[pl.BlockSpec((8, D), lambda l: (0, 0)), pl.BlockSpec((None, D, cols), lambda l: (l, 0, 0)),
                  pl.BlockSpec((None, 1, cols), lambda l: (l, 0, 0))],
        out_specs=pl.BlockSpec((None, 8, cols), lambda l: (l, 0, 0)),
        out_shape=_sds((2, 8, cols)), compiler_params=_params(1))(c_all, w_mod, b_cols)


def _mod_bwd(c_all_t, dm):
    cols = dm.shape[2]

    def body(c_ref, d_ref, o_ref):
        cv = c_ref[...]
        sc = (cv * _sigmoid(cv)).astype(_MXU)
        o_ref[...] = jnp.dot(sc, d_ref[...].astype(_MXU), preferred_element_type=_F32)

    return pl.pallas_call(
        body, name="mod_bwd", grid=(2,),
        in_specs=[pl.BlockSpec((D, 8), lambda l: (0, 0)), pl.BlockSpec((None, 8, cols), lambda l: (l, 0, 0))],
        out_specs=pl.BlockSpec((None, D, cols), lambda l: (l, 0, 0)),
        out_shape=_sds((2, D, cols)), compiler_params=_params(1))(c_all_t, dm)


def _proj(x, g_pre, shift, scale, w_in):
    s = x.shape[0]
    tm = 1024

    def body(x_ref, g_ref, sh_ref, sc_ref, w_ref, o_ref, ht_ref, h_s):
        @pl.when(pl.program_id(1) == 0)
        def _():
            xv = x_ref[...]
            rstd = lax.rsqrt(jnp.mean(xv * xv, axis=-1, keepdims=True) + NORM_EPS)
            hv = (xv * rstd) * g_ref[...] * (1.0 + sc_ref[...]) + sh_ref[...]
            h_s[...] = hv.astype(h_s.dtype)
            ht_ref[...] = hv.T.astype(ht_ref.dtype)

        o_ref[...] = jnp.dot(h_s[...], w_ref[...], preferred_element_type=_F32).astype(o_ref.dtype)

    vec = pl.BlockSpec((1, D), lambda m, n: (0, 0))
    return pl.pallas_call(
        body, name="proj", grid=(s // tm, N_CHIPS),
        in_specs=[pl.BlockSpec((tm, D), lambda m, n: (m, 0)), vec, vec, vec,
                  pl.BlockSpec((None, D, 2304), lambda m, n: (n, 0, 0))],
        out_specs=[pl.BlockSpec((tm, 2304), lambda m, n: (m, n)), pl.BlockSpec((D, tm), lambda m, n: (0, m))],
        out_shape=[_sds((s, IN_W), _MXU), _sds((D, s), _MXU)],
        scratch_shapes=[pltpu.VMEM((tm, D), _MXU)], compiler_params=_params(2))(x, g_pre, shift, scale, w_in)


def _shift_down(cur, halo, j, tb):
    ext = jnp.concatenate([halo, cur], axis=0)
    return pltpu.roll(ext, j, 0)[8:8 + tb]


def _shift_up(cur, halo, j, tb):
    ext = jnp.concatenate([cur, halo], axis=0)
    return pltpu.roll(ext, tb + 8 - j, 0)[0:tb]


def _conv(u_ref, halo_ref, w_ref, b_ref, first, tb):
    u = u_ref[...].astype(_F32)
    halo = jnp.where(first, 0.0, halo_ref[...].astype(_F32)[8:16])
    acc = b_ref[...] + u * w_ref[0:1, :]
    for j in range(1, 4):
        acc = acc + _shift_down(u, halo, j, tb) * w_ref[j:j + 1, :]
    return acc


def _lru_gates(pre_r, pre_i, uc, b_rg, b_ig, lam):
    r = _sigmoid(pre_r + b_rg)
    ig = _sigmoid(pre_i + b_ig)
    nl = -lam
    sp = jnp.maximum(nl, 0.0) + jnp.log(1.0 + jnp.exp(-jnp.abs(nl)))
    la = -LRU_C * r * sp
    a = jnp.exp(la)
    one_m_a2 = -jnp.tanh(la) * (a * a + 1.0)
    inv_sq = lax.rsqrt(jnp.maximum(one_m_a2, 1e-30))
    return r, ig, sp, a, one_m_a2 * inv_sq, inv_sq


GATE_TILES = 8


def _gate_tiles(w_rg, w_ig):
    eye = jnp.eye(2, dtype=w_rg.dtype)

    def tiles(w):
        return jnp.einsum("cpij,pq->cpiqj", w.reshape(GATE_TILES, 2, 64, 64), eye).reshape(GATE_TILES, 128, 128)

    return jnp.concatenate([tiles(w_rg), tiles(w_ig)], axis=2)


def _gate_tile_grads(gw):
    keep = jnp.eye(2, dtype=jnp.bool_)[None, :, None, :, None]

    def blocks(t):
        t5 = t.reshape(GATE_TILES, 2, 64, 2, 64)
        return jnp.sum(jnp.where(keep, t5, 0.0), axis=3).reshape(16, 64, 64)

    return blocks(gw[:, :, 0:128]), blocks(gw[:, :, 128:256])


def _gate_preacts(ucv, wt_ref):
    ucb = ucv.astype(_MXU)
    ps = [jnp.dot(ucb[:, 128 * c:128 * (c + 1)], wt_ref[c], preferred_element_type=_F32) for c in range(GATE_TILES)]
    pre_r = jnp.concatenate([p[:, 0:128] for p in ps], axis=1)
    pre_i = jnp.concatenate([p[:, 128:256] for p in ps], axis=1)
    return pre_r, pre_i


def _scan_fwd(proj, conv_w, conv_b, wt, b_rg, b_ig, lam):
    s = proj.shape[0]
    tb = 256

    def body(u_ref, up_ref, cw_ref, cb_ref, wt_ref, brg_ref, big_ref, lam_ref, h_ref, carry, a_s, b_s):
        i = pl.program_id(0)

        @pl.when(i == 0)
        def _():
            carry[...] = jnp.zeros_like(carry)

        ucv = _conv(u_ref, up_ref, cw_ref, cb_ref, i == 0, tb)
        pre_r, pre_i = _gate_preacts(ucv, wt_ref)
        _, ig, _, a, sq, _ = _lru_gates(pre_r, pre_i, ucv, brg_ref[...], big_ref[...], lam_ref[...])
        av = a
        bv = sq * (ig * ucv)
        av = av.reshape(tb // 8, 8, D)
        bv = bv.reshape(tb // 8, 8, D)
        row8 = lax.broadcasted_iota(jnp.int32, (1, 8, 1), 1)
        for sh in (1, 2, 4):
            m = row8 >= sh
            b_sh = pltpu.roll(bv, sh, 1)
            a_sh = pltpu.roll(av, sh, 1)
            bv = jnp.where(m, av * b_sh + bv, bv)
            av = jnp.where(m, av * a_sh, av)
        a_s[...] = av.reshape(tb, D)
        b_s[...] = bv.reshape(tb, D)

        def tile(t, state):
            rows = pl.ds(pl.multiple_of(t * 8, 8), 8)
            hv = b_s[rows, :] + a_s[rows, :] * state
            b_s[rows, :] = hv
            return jnp.broadcast_to(hv[7:8, :], (8, D))

        carry[...] = lax.fori_loop(0, tb // 8, tile, jnp.broadcast_to(carry[7:8, :], (8, D)), unroll=4)
        h_ref[...] = b_s[...].astype(h_ref.dtype)

    v = _vec((1, D))
    return _rowwise("scan_fwd", body, grid=s // tb,
                    ins=[(proj, _rows(tb, D, CB_U)), (proj, _halo_prev(tb, D, CB_U, rows=16)),
                         (conv_w, _vec((4, D))), (conv_b, v), (wt, _vec((GATE_TILES, 128, 256))),
                         (b_rg, v), (b_ig, v), (lam, v)],
                    outs=[(_sds((s, D), _MXU), _rows(tb, D))],
                    scratch=[pltpu.VMEM((8, D), _F32), pltpu.VMEM((tb, D), _F32), pltpu.VMEM((tb, D), _F32)])[0]


def _weight_specs(l):
    return [pl.BlockSpec((N_CHIPS, None, ATT_W, 256), lambda i: (0, l, 0, 0)),
            pl.BlockSpec((N_CHIPS, None, 256, D), lambda i: (0, l, 0, 0)),
            pl.BlockSpec((N_CHIPS, None, 256, D), lambda i: (0, l, 0, 0))]


def _tail_fwd(l, o, h_lru, proj, x, gate, g_post, gw, target):
    s = x.shape[0]
    tb = 512

    def body(*refs):
        o_ref, h_ref, ga_ref, gl_ref, ma_ref, mb_ref, x_ref, gt_ref, gp_ref, wpa_ref, wpb_ref, wo_ref = refs[0:12]
        aa_ref, ba_ref, ya_ref, yb_ref, z_ref, out_ref = refs[-8:-2] if target is not None else refs[-7:-1]
        ga = ga_ref[...].astype(_F32)
        aa32 = o_ref[...].astype(_F32) * (ga * _sigmoid(ga))
        aa = aa32.astype(_MXU)
        aa_ref[...] = aa32.T.astype(aa_ref.dtype)
        gl = gl_ref[...].astype(_F32)
        ba32 = h_ref[...].astype(_F32) * (gl * _sigmoid(gl))
        ba = ba32.astype(_MXU)
        ba_ref[...] = ba32.T.astype(ba_ref.dtype)
        ya = jnp.concatenate([jnp.dot(aa, wpa_ref[j], preferred_element_type=_F32) for j in range(N_CHIPS)], axis=1)
        ya_ref[...] = ya.astype(ya_ref.dtype)
        yb = jnp.dot(ba, wpb_ref[...].reshape(D, D), preferred_element_type=_F32)
        yb_ref[...] = yb.astype(yb_ref.dtype)
        z32 = _sigmoid(ma_ref[...].astype(_F32)) * ya + _sigmoid(mb_ref[...].astype(_F32)) * yb
        z = z32.astype(_MXU)
        z_ref[...] = z32.T.astype(z_ref.dtype)
        ov = jnp.dot(z, wo_ref[...].reshape(D, D), preferred_element_type=_F32)
        out_ref[...] = ov.astype(out_ref.dtype)
        rstd = lax.rsqrt(jnp.mean(ov * ov, axis=-1, keepdims=True) + NORM_EPS)
        xn =x_ref[...] + gt_ref[...] * ((ov * rstd) * gp_ref[...])
        if target is None:
            refs[-1][...] = xn
        else:
            dy_ref, acc_ref = refs[-2], refs[-1]
            err = xn - refs[12][...]
            dy_ref[...] = err * (1.0 / D)
            _zero_first(pl.program_id(0), acc_ref)
            acc_ref[...] += jnp.sum(err * err, axis=0, keepdims=True)

    v = _vec((1, D))
    r = _rows(tb, D)
    r5 = _rows(tb, ATT_W)
    weights = list(zip((gw["w_pa"], gw["w_pb"], gw["w_o"]), _weight_specs(l)))
    cols = pl.BlockSpec((D, tb), lambda i: (0, i))
    head_in = [] if target is None else [(target, r)]
    head_out = [] if target is None else [(_sds((1, D)), v)]
    return _rowwise("tail_fwd" if target is None else "tail_loss_fwd", body, grid=s // tb,
                    ins=[(o, r5), (h_lru, r), (proj, _rows(tb, ATT_W, CB_GATT)), (proj, _rows(tb, D, CB_GLRU)),
                         (proj, _rows(tb, D, CB_MA)), (proj, _rows(tb, D, CB_MB)), (x, r), (gate, v), (g_post, v)]
                    + weights + head_in,
                    outs=[(_sds((ATT_W, s), _MXU), pl.BlockSpec((ATT_W, tb), lambda i: (0, i))),
                          (_sds((D, s), _MXU), cols), (_sds((s, D), _MXU), r), (_sds((s, D), _MXU), r),
                          (_sds((D, s), _MXU), cols), (_sds((s, D), _MXU), r), (_sds((s, D)), r)]
                    + head_out)


def _zero_first(i, *refs):
    @pl.when(i == 0)
    def _():
        for ref in refs:
            ref[...] = jnp.zeros_like(ref)


def _tail_bwd(l, dx, out, y_a, y_b, proj, o, h_lru, gate, g_post, gw):
    s = dx.shape[0]
    tb = 256

    def body(dx_ref, out_ref, ya_ref, yb_ref, ma_ref, mb_ref, o_ref, ga_ref, h_ref, gl_ref, gt_ref, gp_ref,
             wpa_ref, wpb_ref, wo_ref,
             dout_ref, dya_ref, dyb_ref, rest_ref, do_ref, dh_ref, dgt_ref, dgp_ref):
        i = pl.program_id(0)
        ov = out_ref[...].astype(_F32)
        dxv = dx_ref[...]
        rstd = lax.rsqrt(jnp.mean(ov * ov, axis=-1, keepdims=True) + NORM_EPS)
        nv = ov * rstd
        s_dn = jnp.sum(dxv * nv, axis=0, keepdims=True)
        _zero_first(i, dgt_ref, dgp_ref)
        dgt_ref[...] += s_dn * gp_ref[...]
        dgp_ref[...] += s_dn * gt_ref[...]
        dn = dxv * (gt_ref[...] * gp_ref[...])
        d_out = (rstd * (dn - nv * jnp.mean(dn * nv, axis=-1, keepdims=True))).astype(_MXU)
        dout_ref[...] = d_out
        dz = lax.dot_general(d_out, wo_ref[...].reshape(D, D), _NT, preferred_element_type=_F32)
        ga = _sigmoid(ma_ref[...].astype(_F32))
        gb = _sigmoid(mb_ref[...].astype(_F32))
        dya = (dz * ga).astype(_MXU)
        dyb = (dz * gb).astype(_MXU)
        dya_ref[...] = dya
        dyb_ref[...] = dyb
        rest_ref[:, R_MA:R_MB] = (dz * ya_ref[...].astype(_F32) * ga * (1.0 - ga)).astype(rest_ref.dtype)
        rest_ref[:, R_MB:R_END] = (dz * yb_ref[...].astype(_F32) * gb * (1.0 - gb)).astype(rest_ref.dtype)
        daa = lax.dot_general(dya[:, 0:256], wpa_ref[0], _NT, preferred_element_type=_F32)
        for j in range(1, N_CHIPS):
            daa = daa + lax.dot_general(dya[:, j * 256:(j + 1) * 256], wpa_ref[j], _NT, preferred_element_type=_F32)
        dba = lax.dot_general(dyb, wpb_ref[...].reshape(D, D), _NT, preferred_element_type=_F32)
        gav = ga_ref[...].astype(_F32)
        sa = _sigmoid(gav)
        do_ref[...] = daa * (gav * sa)
        rest_ref[:, 0:R_U] = (daa * o_ref[...].astype(_F32) * (sa * (1.0 + gav * (1.0 - sa)))).astype(rest_ref.dtype)
        gl = gl_ref[...].astype(_F32)
        sl = _sigmoid(gl)
        dh_ref[...] = dba * (gl * sl)
        rest_ref[:, R_GLRU:R_MA] = (dba * h_ref[...].astype(_F32)
                                    * (sl * (1.0 + gl * (1.0 - sl)))).astype(rest_ref.dtype)

    v = _vec((1, D))
    r5, r10 = _rows(tb, ATT_W), _rows(tb, D)
    return _rowwise("tail_bwd", body, grid=s // tb,
                    ins=[(dx, r10), (out, r10), (y_a, r10), (y_b, r10), (proj, _rows(tb, D, CB_MA)),
                         (proj, _rows(tb, D, CB_MB)), (o, r5), (proj, _rows(tb, ATT_W, CB_GATT)), (h_lru, r10),
                         (proj, _rows(tb, D, CB_GLRU)), (gate, v), (g_post, v)]
                    + list(zip((gw["w_pa"], gw["w_pb"], gw["w_o"]), _weight_specs(l))),
                    outs=[(_sds((s, D), _MXU), r10), (_sds((s, D), _MXU), r10), (_sds((s, D), _MXU), r10),
                          (_sds((s, R_END), _MXU), _rows(tb, R_END)),
                          (_sds((s, ATT_W)), r5), (_sds((s, D)), r10), (_sds((1, D)), v), (_sds((1, D)), v)])


def _scan_bwd(dh, proj, conv_w, conv_b, h_lru, wt, b_rg, b_ig, lam):
    s = dh.shape[0]
    tb = 256
    n = s // tb

    def body(dh_ref, u_ref, up_ref, cw_ref, cb_ref, h_ref, hp_ref, wt_ref, brg_ref, big_ref, lam_ref,
             duc_ref, dwt_ref, dbrg_ref, dbig_ref, dlam_ref, carry, c_s, g_s):
        i = pl.program_id(0)

        @pl.when(i == 0)
        def _():
            carry[...] = jnp.zeros_like(carry)
            for acc_ref in (dwt_ref, dbrg_ref, dbig_ref, dlam_ref):
                acc_ref[...] = jnp.zeros_like(acc_ref)

        ucv = _conv(u_ref, up_ref, cw_ref, cb_ref, i == n - 1, tb)
        pre_r, pre_i = _gate_preacts(ucv, wt_ref)
        r, ig, sp, a, sq, inv_sq =_lru_gates(pre_r, pre_i, ucv, brg_ref[...], big_ref[...], lam_ref[...])
        row = lax.broadcasted_iota(jnp.int32, (tb, 1), 0)
        cv = jnp.where(row == tb - 1, 1.0, pltpu.roll(a, tb - 1, 0))
        gv = dh_ref[...]
        cv = cv.reshape(tb // 8, 8, D)
        gv = gv.reshape(tb // 8, 8, D)
        row8 = lax.broadcasted_iota(jnp.int32, (1, 8, 1), 1)
        for sh in (1, 2, 4):
            m = row8 < 8 - sh
            g_sh = pltpu.roll(gv, 8 - sh, 1)
            c_sh = pltpu.roll(cv, 8 - sh, 1)
            gv = jnp.where(m, gv + cv * g_sh, gv)
            cv = jnp.where(m, cv * c_sh, cv)
        c_s[...] = cv.reshape(tb, D)
        g_s[...] = gv.reshape(tb, D)

        def tile(k, state):
            rows = pl.ds(pl.multiple_of((tb // 8 - 1 - k) * 8, 8), 8)
            gt = g_s[rows, :] + c_s[rows, :] * state
            g_s[rows, :] = gt
            return jnp.broadcast_to(gt[0:1, :], (8, D))

        lax.fori_loop(0, tb // 8, tile, jnp.broadcast_to(carry[0:1, :], (8, D)), unroll=4)
        gv = g_s[...]
        carry[...] = (a * gv)[0:8]

        halo = jnp.where(i < n - 1, hp_ref[...].astype(_F32)[8:16], 0.0)
        h_prev = _shift_down(h_ref[...].astype(_F32), halo, 1, tb)
        d_a = gv * h_prev
        d_sq = gv * (ig * ucv)
        d_i = gv * sq * ucv
        d_la = d_a * a - d_sq * (a * a) * inv_sq
        d_r = d_la * (-LRU_C * sp)
        d_pre_r = d_r * r * (1.0 - r)
        d_pre_i = d_i * ig * (1.0 - ig)
        ucb = ucv.astype(_MXU)
        dpr = d_pre_r.astype(_MXU)
        dpi = d_pre_i.astype(_MXU)
        back = []
        for c in range(GATE_TILES):
            lanes = slice(128 * c, 128 * (c + 1))
            dp = jnp.concatenate([dpr[:, lanes], dpi[:, lanes]], axis=1)
            back.append(lax.dot_general(dp, wt_ref[c], _NT, preferred_element_type=_F32))
            dwt_ref[c] += lax.dot_general(ucb[:, lanes], dp, _TN, preferred_element_type=_F32)
        duc_ref[...] = gv * sq * ig + jnp.concatenate(back, axis=1)
        dbrg_ref[...] += jnp.sum(d_pre_r, axis=0, keepdims=True)
        dbig_ref[...] += jnp.sum(d_pre_i, axis=0, keepdims=True)
        lamv = lam_ref[...]
        dlam_ref[...] += jnp.sum(d_la * (-LRU_C * r), axis=0, keepdims=True) * (-_sigmoid(-lamv))

    v = _vec((1, D))
    rv = _rows(tb, D, 0, n)
    return _rowwise("scan_bwd", body, grid=n,
                    ins=[(dh, rv), (proj, _rows(tb, D, CB_U, n)), (proj, _halo_prev(tb, D, CB_U, n, rows=16)),
                         (conv_w, _vec((4, D))), (conv_b, v), (h_lru, rv), (h_lru, _halo_prev(tb, D, 0, n, rows=16)),
                         (wt, _vec((GATE_TILES, 128, 256))), (b_rg, v), (b_ig, v), (lam, v)],
                    outs=[(_sds((s, D)), rv), (_sds((GATE_TILES, 128, 256)), _vec((GATE_TILES, 128, 256))),
                          (_sds((1, D)), v), (_sds((1, D)), v), (_sds((1, D)), v)],
                    scratch=[pltpu.VMEM((8, D), _F32), pltpu.VMEM((tb, D), _F32), pltpu.VMEM((tb, D), _F32)])


def _conv_bwd(duc_a, proj, conv_w, rest):
    s = duc_a.shape[0]
    tb = 512
    n = s // tb
    hw = D // 2

    def body(da_ref, dan_ref, u_ref, up_ref, w_ref, rest_in, du_ref, dw_ref, dbias_ref):
        i = pl.program_id(1)
        duc = da_ref[...]
        nxt = jnp.where(i < n - 1, dan_ref[...], 0.0)
        u = u_ref[...].astype(_F32)
        halo = jnp.where(i > 0, up_ref[...].astype(_F32)[8:16], 0.0)
        du = duc * w_ref[0:1, :]
        dws = [jnp.sum(duc * u, axis=0, keepdims=True)]
        for j in range(1, 4):
            du = du + _shift_up(duc, nxt, j, tb) * w_ref[j:j + 1, :]
            dws.append(jnp.sum(duc * _shift_down(u, halo, j, tb), axis=0, keepdims=True))
        du_ref[...] = du.astype(du_ref.dtype)
        _zero_first(i, dw_ref, dbias_ref)
        for j in range(4):
            dw_ref[j:j + 1, :] += dws[j]
        dbias_ref[...] += jnp.sum(duc, axis=0, keepdims=True)

    r = pl.BlockSpec((tb, hw), lambda h, i: (i, h))
    nxt_spec = pl.BlockSpec((8, hw), lambda h, i: (jnp.minimum((i + 1) * (tb // 8), n * (tb // 8) - 1), h))
    return pl.pallas_call(
        body, name="conv_bwd", grid=(2, n),
        in_specs=[r, nxt_spec,
                  pl.BlockSpec((tb, hw), lambda h, i: (i, 2 * CB_U + h)),
                  pl.BlockSpec((16, hw), lambda h, i: (jnp.maximum(i * (tb // 16) - 1, 0), 2 * CB_U + h)),
                  pl.BlockSpec((4, hw), lambda h, i: (0, h)), pl.BlockSpec(memory_space=pl.ANY)],
        out_specs=[pl.BlockSpec((tb, hw), lambda h, i: (i, R_U // hw + h)),
                   pl.BlockSpec((4, hw), lambda h, i: (0, h)), pl.BlockSpec((1, hw), lambda h, i: (0, h))],
        out_shape=[_sds(rest.shape, rest.dtype), _sds((4, D)), _sds((1, D))],
        input_output_aliases={5: 0}, compiler_params=_params(2),
    )(duc_a, duc_a, proj, proj, conv_w, rest)


def _band_tiles(dil):
    tiles = []
    for rho in range(dil):
        for b in range(16 // dil):
            qs = rho + dil * BAND * b
            tiles.append((qs, QBLK + qs - dil * BAND, b))
    return tiles


def _strided(start, size, dil):
    return pl.ds(start, size, stride=dil) if dil > 1 else pl.ds(start, size)


def _band_mask(i, b):
    qi = lax.broadcasted_iota(jnp.int32, (BAND, 2 * BAND), 0)
    ki = lax.broadcasted_iota(jnp.int32, (BAND, 2 * BAND), 1)
    valid = (ki >= qi) & (ki <= qi + BAND)
    if b == 0:
        valid = valid & ((ki >= BAND) | (i > 0))
    return valid


def _attn_fwd(proj):
    s = proj.shape[0]
    n = s // QBLK
    scale = HEAD ** -0.5

    def body(*refs):
        q_refs, kp_refs, kc_refs, vp_refs, vc_refs = (refs[3 * t:3 * t + 3] for t in range(5))
        o_ref, lse_ref, qbuf, kbuf, vbuf = refs[15:20]
        accs, maxs, dens = refs[20:23], refs[23:26], refs[26:29]
        i = pl.program_id(1)
        for g, dil in enumerate(DILATIONS):
            qbuf[...] = q_refs[g][...].astype(_F32)
            kbuf[0:QBLK, :] = kp_refs[g][...].astype(_F32)
            kbuf[QBLK:2 * QBLK, :] = kc_refs[g][...].astype(_F32)
            vbuf[0:QBLK, :] = vp_refs[g][...].astype(_F32)
            vbuf[QBLK:2 * QBLK, :] = vc_refs[g][...].astype(_F32)
            for qs, ks, b in _band_tiles(dil):
                qsl = _strided(qs, BAND, dil)
                q = qbuf[qsl, :].astype(_MXU)
                kk = kbuf[_strided(ks, 2 * BAND, dil), :].astype(_MXU)
                vv = vbuf[_strided(ks, 2 * BAND, dil), :].astype(_MXU)
                sc = lax.dot_general(q, kk, _NT, preferred_element_type=_F32) * scale
                sc = jnp.where(_band_mask(i, b), sc, NEG_INF)
                m = jnp.max(sc, axis=-1, keepdims=True)
                p = jnp.exp(sc - m)
                accs[g][qsl, :] = jnp.dot(p.astype(_MXU), vv, preferred_element_type=_F32)
                maxs[g][qsl, :] = jnp.broadcast_to(m, (BAND, HEAD))
                dens[g][qsl, :] = jnp.broadcast_to(jnp.sum(p, axis=-1, keepdims=True), (BAND, HEAD))
        ms = [r[...] for r in maxs]
        mx = jnp.maximum(jnp.maximum(ms[0], ms[1]), ms[2])
        ws = [jnp.exp(m - mx) for m in ms]
        den = ws[0] * dens[0][...] + ws[1] * dens[1][...] + ws[2] * dens[2][...]
        o_ref[...] = ((ws[0] * accs[0][...] + ws[1] * accs[1][...] + ws[2] * accs[2][...]) / den).astype(o_ref.dtype)
        lse_ref[...] = mx + jnp.log(den)

    blk = (QBLK, HEAD)

    def spec(first_col, lag):
        specs = []
        for g in range(3):
            col = first_col + g * HEADS
            if lag:
                specs.append(pl.BlockSpec(blk, lambda j, i, col=col: (jnp.maximum(i - 1, 0), col + j)))
            else:
                specs.append(pl.BlockSpec(blk, lambda j, i, col=col: (i, col + j)))
        return specs

    out_spec = pl.BlockSpec(blk, lambda j, i: (i, j))
    return pl.pallas_call(
        body, name="attn_fwd", grid=(HEADS, n),
        in_specs=spec(0, False) + spec(12, True) + spec(12, False) + spec(24, True) + spec(24, False),
        out_specs=[out_spec] * 2, out_shape=[_sds((s, ATT_W), _MXU), _sds((s, ATT_W))],
        scratch_shapes=[pltpu.VMEM(blk, _F32)] + [pltpu.VMEM((2 * QBLK, HEAD), _F32)] * 2
        + [pltpu.VMEM(blk, _F32)] * 9,
        compiler_params=_params(2))(*([proj] * 15))


def _attn_bwd(proj, d_o, o, lse, g, into):
    s = proj.shape[0]
    dil = DILATIONS[g]
    n = s // QBLK
    scale = HEAD ** -0.5
    tiles = _band_tiles(dil)

    def body(*refs):
        q_ref, kp_ref, kc_ref, vp_ref, vc_ref, do_ref, o_ref, lse_ref = refs[0:8]
        dq_ref, dk_ref, dv_ref, kbuf, vbuf, dkbuf, dvbuf, dqbuf, qbuf, obuf = refs[-10:]
        i = pl.program_id(1)

        @pl.when(i == 0)
        def _():
            dkbuf[0:QBLK, :] = jnp.zeros((QBLK, HEAD), _F32)
            dvbuf[0:QBLK, :] = jnp.zeros((QBLK, HEAD), _F32)

        @pl.when(i < n)
        def _():
            qbuf[...] = q_ref[...].astype(_F32)
            obuf[...] = o_ref[...].astype(_F32)
            kbuf[0:QBLK, :] = kp_ref[...].astype(_F32)
            kbuf[QBLK:2 * QBLK, :] = kc_ref[...].astype(_F32)
            vbuf[0:QBLK, :] = vp_ref[...].astype(_F32)
            vbuf[QBLK:2 * QBLK, :] = vc_ref[...].astype(_F32)
            dkbuf[QBLK:2 * QBLK, :] = jnp.zeros((QBLK, HEAD), _F32)
            dvbuf[QBLK:2 * QBLK, :] = jnp.zeros((QBLK, HEAD), _F32)
            for qs, ks, b in tiles:
                qsl = _strided(qs, BAND, dil)
                ksl = _strided(ks, 2 * BAND, dil)
                q = qbuf[qsl, :].astype(_MXU)
                kk = kbuf[ksl, :].astype(_MXU)
                vv = vbuf[ksl, :].astype(_MXU)
                dov = do_ref[qsl, :]
                dd = jnp.sum(dov * obuf[qsl, :], axis=-1, keepdims=True)
                lse_t = lse_ref[qsl, :][:, 0:1]
                sc = lax.dot_general(q, kk, _NT, preferred_element_type=_F32) * scale
                p = jnp.where(_band_mask(i, b), jnp.exp(sc - lse_t), 0.0)
                dob = dov.astype(_MXU)
                dp = lax.dot_general(dob, vv, _NT, preferred_element_type=_F32)
                ds = (p * (dp - dd) * scale).astype(_MXU)
                dqbuf[qsl, :] = jnp.dot(ds, kk, preferred_element_type=_F32)
                dkbuf[ksl, :] += lax.dot_general(ds, q, _TN, preferred_element_type=_F32)
                dvbuf[ksl, :] += lax.dot_general(p.astype(_MXU), dob, _TN, preferred_element_type=_F32)
            dq_ref[...] = dqbuf[...].astype(dq_ref.dtype)

        dk_ref[...] = dkbuf[0:QBLK, :].astype(dk_ref.dtype)
        dv_ref[...] = dvbuf[0:QBLK, :].astype(dv_ref.dtype)
        dkbuf[0:QBLK, :] = dkbuf[QBLK:2 * QBLK, :]
        dvbuf[0:QBLK, :] = dvbuf[QBLK:2 * QBLK, :]

    blk = (QBLK, HEAD)
    cq, ck, cv = g * HEADS, 12 + g * HEADS, 24 + g * HEADS

    def cur(i):
        return jnp.minimum(i, n - 1)

    def prev(i):
        return jnp.maximum(jnp.minimum(i, n - 1) - 1, 0)

    own = pl.BlockSpec(blk, lambda j, i: (cur(i), j))
    own_out = pl.BlockSpec(blk, lambda j, i: (cur(i), cq + j))
    late_out = pl.BlockSpec(blk, lambda j, i: (jnp.maximum(i - 1, 0), cq + j))
    extra = [] if into is None else list(into)
    return pl.pallas_call(
        body, name="attn_bwd_d%d" % dil, grid=(HEADS, n + 1),
        in_specs=[pl.BlockSpec(blk, lambda j, i: (cur(i), cq + j)),
                  pl.BlockSpec(blk, lambda j, i: (prev(i), ck + j)),
                  pl.BlockSpec(blk, lambda j, i: (cur(i), ck + j)),
                  pl.BlockSpec(blk, lambda j, i: (prev(i), cv + j)),
                  pl.BlockSpec(blk, lambda j, i: (cur(i), cv + j)),
                  own, own, own] + [pl.BlockSpec(memory_space=pl.ANY)] * len(extra),
        out_specs=[own_out, late_out, late_out], out_shape=[_sds((s, QKV_W), _MXU)] * 3,
        input_output_aliases={8 + t: t for t in range(len(extra))},
        scratch_shapes=[pltpu.VMEM((2 * QBLK, HEAD), _F32)] * 4 + [pltpu.VMEM((QBLK, HEAD), _F32)] * 3,
        compiler_params=_params(2))(proj, proj, proj, proj, proj, d_o, o, lse, *extra)


_PARTS = ((0, 2), (2, 2), (4, 2), (6, 6))
_CHUNK = 768


def _d_x(name, parts, w_in, x, dx_out, g_pre, scale, blocks, into):
    s = parts[0].shape[0]
    nk = IN_W // _CHUNK
    first_block, n_blocks = blocks

    def body(*refs):
        p0, p1, p2, p3, w_ref, x_ref, dxo_ref, g_ref, sc_ref = refs[0:9]
        dx_ref, dsh_ref, dsc_ref, dg_ref, acc = refs[-5:]
        m = pl.program_id(0)
        k = pl.program_id(2)

        @pl.when(k == 0)
        def _():
            acc[...] = jnp.zeros_like(acc)

        @pl.when((k == 0) & (m == 0))
        def _():
            for ref in (dsh_ref, dsc_ref, dg_ref):
                ref[...] = jnp.zeros_like(ref)

        for p_ref, (first, cnt) in zip((p0, p1, p2, p3), _PARTS):
            @pl.when((k >= first) & (k < first + cnt))
            def _(p_ref=p_ref):
                acc[...] += lax.dot_general(p_ref[...].astype(_MXU), w_ref[...], _NT, preferred_element_type=_F32)

        @pl.when(k == nk - 1)
        def _():
            dhv = acc[...]
            xv = x_ref[...]
            rstd = lax.rsqrt(jnp.mean(xv * xv, axis=-1, keepdims=True) + NORM_EPS)
            xn = xv * rstd
            one_sc = 1.0 + sc_ref[...]
            s1 = jnp.sum(dhv * xn, axis=0, keepdims=True)
            dsh_ref[...] += jnp.sum(dhv, axis=0, keepdims=True)
            dsc_ref[...] += s1 * g_ref[...]
            dg_ref[...] += s1 * one_sc
            dxn = dhv * (g_ref[...] * one_sc)
            dx_ref[...] = dxo_ref[...] + rstd * (dxn - xn * jnp.mean(dxn * xn, axis=-1, keepdims=True))

    def part_spec(first, cnt):
        return pl.BlockSpec((1024, _CHUNK), lambda m, n, k: (first_block + m, jnp.clip(k - first, 0, cnt - 1)))

    rows = pl.BlockSpec((1024, D), lambda m, n, k: (first_block + m, 0))
    vec = pl.BlockSpec((1, D), lambda m, n, k: (0, 0))
    extra = [] if into is None else [into]
    return pl.pallas_call(
        body, name=name, grid=(n_blocks, 1, nk),
        in_specs=[part_spec(*p) for p in _PARTS]
        + [pl.BlockSpec((None, D, _CHUNK), lambda m, n, k: (k // 3, 0, k % 3)), rows, rows, vec, vec]
        + [pl.BlockSpec(memory_space=pl.ANY)] * len(extra),
        out_specs=[rows, vec, vec, vec], out_shape=[_sds((s, D)), _sds((1, D)), _sds((1, D)), _sds((1, D))],
        input_output_aliases={9: 0} if extra else {},
        scratch_shapes=[pltpu.VMEM((1024, D), _F32)], compiler_params=_params(3))(
            *parts, w_in, x, dx_out, g_pre, scale, *extra)


def _g_w_in(h_t, parts):
    s = h_t.shape[1]
    tk = 2048
    nk = s // tk

    def body(*refs):
        h_ref, p_refs = refs[0], refs[1:5]
        o_ref, acc = refs[-2], refs[-1]
        n = pl.program_id(1)
        k = pl.program_id(2)

        @pl.when(k == 0)
        def _():
            acc[...] = jnp.zeros_like(acc)

        for p_ref, (first, cnt) in zip(p_refs, _PARTS):
            @pl.when((n >= first) & (n < first + cnt))
            def _(p_ref=p_ref):
                acc[...] += jnp.dot(h_ref[...], p_ref[...].astype(_MXU), preferred_element_type=_F32)

        @pl.when(k == nk - 1)
        def _():
            o_ref[...] = acc[...]

    def part_spec(first, cnt):
        def index(m, n, k):
            row = jnp.where(n < first, 0, jnp.where(n >= first + cnt, nk - 1, k))
            return (row, jnp.clip(n - first, 0, cnt - 1))
        return pl.BlockSpec((tk, _CHUNK), index)

    return pl.pallas_call(
        body, name="g_w_in", grid=(1, IN_W // _CHUNK, nk),
        in_specs=[pl.BlockSpec((D, tk), lambda m, n, k: (0, k))] + [part_spec(*p) for p in _PARTS],
        out_specs=pl.BlockSpec((None, D, _CHUNK), lambda m, n, k: (n // 3, 0, n % 3)),
        out_shape=_sds((N_CHIPS, D, 2304)),
        scratch_shapes=[pltpu.VMEM((D, _CHUNK), _F32)], compiler_params=_params(3))(h_t, *parts)


def _layer_fwd(l, x, p, gw, late, target):
    if callable(gw["w_in"][l]):
        gw["w_in"][l] = gw["w_in"][l](x)
    proj, h_t = _proj(x, p["g_pre"], p["shift"], p["scale"], gw["w_in"][l])
    o, lse = _attn_fwd(proj)
    h_lru = _scan_fwd(proj, p["conv_w"], p["conv_b"], p["wt"], p["b_rg"], p["b_ig"], p["lam"])
    if late is not None:
        landed = dict(late(h_lru))
        gw["w_in"].append(landed.pop("w_in1"))
        gw.update(landed)
    a_att, b_act, y_a, y_b, z, out, *last = _tail_fwd(l, o, h_lru, proj, x, p["gate"], p["g_post"], gw, target)
    saved = dict(x=x, h_t=h_t, proj=proj, o=o, lse=lse, h_lru=h_lru, a_att=a_att, b_act=b_act,
                 y_a=y_a, y_b=y_b, z=z, out=out)
    return (last[0] if target is None else last), saved


def _layer_bwd(l, dx, p, gw, sv, hooks):
    s = dx.shape[0]
    nt = s // 2048
    proj = sv["proj"]
    gate, b_rg, g_pre = p["gate"], p["b_rg"], p["g_pre"]
    if hooks is not None:
        gate = gate + hooks[0]([dx])
    d_out, dy_a, dy_b, d_rest, d_o, dh_lru, d_gate, d_gpost = _tail_bwd(
        l, dx, sv["out"], sv["y_a"], sv["y_b"], proj, sv["o"], sv["h_lru"], gate, p["g_post"], gw)
    if hooks is not None:
        b_rg = b_rg + hooks[1]([d_out])

    def wgrad_rows(name, a, b):
        return _mm(name, a, b, _sds((N_CHIPS, 256, D)), grid=(1, 1, nt),
                   a_spec=pl.BlockSpec((D, 2048), lambda m, n, k: (0, k)),
                   b_spec=pl.BlockSpec((2048, D), lambda m, n, k: (k, 0)),
                   o_spec=pl.BlockSpec((N_CHIPS, 256, D), lambda m, n, k: (0, 0, 0)),
                   dims=_NN, acc_shape=(D, D))

    big = {}
    big["w_o"] = wgrad_rows("g_w_o", sv["z"], d_out)
    big["w_pa"] = _mm("g_w_pa", sv["a_att"], dy_a, _sds((N_CHIPS, ATT_W, 256)), grid=(1, 4, nt),
                      a_spec=pl.BlockSpec((ATT_W, 2048), lambda m, n, k: (0, k)),
                      b_spec=pl.BlockSpec((2048, 256), lambda m, n, k: (k, n)),
                      o_spec=pl.BlockSpec((None, ATT_W, 256), lambda m, n, k: (n, 0, 0)),
                      dims=_NN, acc_shape=(ATT_W, 256))
    big["w_pb"] = wgrad_rows("g_w_pb", sv["b_act"], dy_b)
    duc, g_wt, d_brg, d_big, d_lam = _scan_bwd(dh_lru, proj, p["conv_w"], p["conv_b"], sv["h_lru"], p["wt"], b_rg,
                                               p["b_ig"], p["lam"])
    g_wrg, g_wig = _gate_tile_grads(g_wt)
    d_rest, g_convw, g_convb = _conv_bwd(duc, proj, p["conv_w"], d_rest)
    dqkv = None
    for g in range(3):
        dqkv = _attn_bwd(proj, d_o, sv["o"], sv["lse"], g, dqkv)
    if hooks is not None:
        g_pre = g_pre + hooks[2]([dqkv[0]])
    parts = (dqkv[0], dqkv[1], dqkv[2], d_rest)
    big["w_in"] = _g_w_in(sv["h_t"], parts)
    nb = s // 1024
    if hooks is None:
        dx_in, d_shift, d_scale, d_gpre = _d_x("d_x", parts, gw["w_in"][l], sv["x"], dx, g_pre, p["scale"],
                                               (0, nb), None)
    else:
        first = _d_x("d_x_a", parts, gw["w_in"][l], sv["x"], dx, g_pre + hooks[3](big), p["scale"],
                     (0, nb // 2), None)
        second = _d_x("d_x_b", parts, gw["w_in"][l], sv["x"], dx, g_pre + hooks[4]([first[0]]), p["scale"],
                      (nb // 2, nb - nb // 2), first[0])
        dx_in = second[0]
        d_shift, d_scale, d_gpre = (a + b for a, b in zip(first[1:], second[1:]))
    small = dict(dmod=jnp.concatenate([d_shift, d_scale, d_gate], axis=1), g_pre=d_gpre, conv_w=g_convw,
                 conv_b=g_convb, w_rg=g_wrg, b_rg=d_brg, w_ig=g_wig, b_ig=d_big, lam=d_lam, g_post=d_gpost)
    return dx_in, small, big


_BIG = ("w_in", "w_pa", "w_pb", "w_o")


class _GradReduce:
    PAIR_CHUNKS = (2, 1, 1, 1)
    CHIP_CHUNKS = (2, 1, 1, 1)
    FILL_CHUNKS = (4, 1, 1, 1)

    def __init__(self, core, where):
        self.core, self.where = core, where
        self.finals = None

    def begin(self, l, big):
        n = len(_BIG)
        halves = [big[k].reshape(N_CHIPS, 2, big[k].shape[1] // 2, big[k].shape[2]) for k in _BIG]
        lands = [lax.empty((N_CHIPS,) + h.shape[2:], _F32) for h in halves]
        plan, nsem = _pair_plan(n, self.PAIR_CHUNKS)
        state = {}
        state["pair"] = _split_start("reduce_pair_start_%d" % l, halves + lands, plan, nsem, [])

        def started(after):
            return state["pair"][3][0, 0]

        def pair_done(after):
            send, recv, arrays, _ = state["pair"]
            arrays = _split_wait("reduce_pair_wait_%d" % l, send, recv, arrays, plan, after)
            sums = [_sum_pair("sum_pair_%s_%d" % (k, l), arrays[a], arrays[n + a], self.core, 128)
                    for a, k in enumerate(_BIG)]
            state["mine"] = [t[0] for t in sums]
            lands2 = [lax.empty(t[1].shape, _MXU) for t in sums]
            plan2, nsem2 = _chips_plan(n, self.CHIP_CHUNKS)
            state["plan2"] = plan2
            state["chips"] = _split_start("reduce_chips_start_%d" % l, [t[1] for t in sums] + lands2, plan2, nsem2, [])
            return state["chips"][3][0, 0]

        def chips_done(after):
            send, recv, arrays, _ = state["chips"]
            arrays = _split_wait("reduce_chips_wait_%d" % l, send, recv, arrays, state["plan2"], after)
            finals = [_sum_chips("sum_chips_%s_%d" % (k, l), state["mine"][a], arrays[n + a], self.where, l,
                                 None if self.finals is None else self.finals[a], 128)
                      for a, k in enumerate(_BIG)]
            plan3, nsem3 = _fill_plan(n, self.FILL_CHUNKS, l)
            state["plan3"] = plan3
            state["fill"] = _split_start("gather_halves_start_%d" % l, finals, plan3, nsem3, [])
            return state["fill"][3][0, 0]

        def finish(after):
            send, recv, arrays, _ = state["fill"]
            self.finals = _split_wait("gather_halves_wait_%d" % l, send, recv, arrays, state["plan3"], after)
            return self.finals

        self._finish = finish
        return [started, pair_done, chips_done]

    def finish(self, after):
        return self._finish(after)


def _local_step(x, target, small_p, w_in0, late, reducer, on_smalls):
    saved = []
    h = x
    gw = dict(w_in=[w_in0])
    h, sv = _layer_fwd(0, h, small_p[0], gw, late, None)
    saved.append(sv)
    (dy, sq), sv = _layer_fwd(1, h, small_p[1], gw, None, target)
    saved.append(sv)
    loss = 0.5 * jnp.sum(sq) / D
    smalls = [None, None]
    dx, smalls[1], big1 = _layer_bwd(1, dy, small_p[1], gw, saved[1], None)
    hooks1 = reducer.begin(1, big1)
    small_started = on_smalls(1, smalls[1])
    pair_started = hooks1[0]
    hooks1[0] = lambda after: pair_started(after) + small_started
    own = {}

    def layer0_ready(big0):
        reducer.finish([big0["w_in"]])
        own["hooks"] = reducer.begin(0, big0)
        return own["hooks"][0]([])

    dx, smalls[0], _ = _layer_bwd(0, dx, small_p[0], gw, saved[0],
                                  hooks1 + [layer0_ready, lambda after: own["hooks"][1](after)])
    on_smalls(0, smalls[0])

    def finish_reduce(after):
        own["hooks"][2](after)
        return reducer.finish(after)

    return loss, dx, smalls, finish_reduce


_SMALL_ROWS = 8 + 8 + 8 + 64 + 64
_SMALL_VECS = ("g_pre", "conv_b", "b_rg", "b_ig", "lam", "g_post")


def _pack_small(small):
    pad = lambda rows: jnp.zeros((rows, D), _F32)
    return jnp.concatenate(
        [small["dmod"].reshape(3, D), pad(5)] + [small[k] for k in _SMALL_VECS] + [pad(2)]
        + [small["conv_w"], pad(4), small["w_rg"].reshape(64, D), small["w_ig"].reshape(64, D)], axis=0)


def kernel(x, c, w_mod, b_mod, g_pre, w_in, conv_w, conv_b, w_rg, b_rg, w_ig, b_ig, lru_lambda, w_pa, w_pb, w_o, g_post, loss_target, m_w_mod, m_b_mod, m_g_pre, m_w_in, m_conv_w, m_conv_b, m_w_rg, m_b_rg, m_w_ig, m_b_ig, m_lru_lambda, m_w_pa, m_w_pb, m_w_o, m_g_post, v_w_mod, v_b_mod, v_g_pre, v_w_in, v_conv_w, v_conv_b, v_w_rg, v_b_rg, v_w_ig, v_b_ig, v_lru_lambda, v_w_pa, v_w_pb, v_w_o, v_g_post):
    xi, yi, ci = lax.axis_index("x"), lax.axis_index("y"), lax.axis_index("c")
    chip = 2 * xi + yi
    dev = 4 * xi + 2 * yi + ci
    mcols = w_mod.shape[2]

    pack1 = jnp.concatenate([jnp.broadcast_to(c, (8, D)),
                             jnp.pad(conv_w.reshape(8, 256), ((0, 0), (0, D - 256)))], axis=0)
    g1 = _exchange("gather_cond", [pack1], "xyc", False)[0]
    c_all = g1[:, 0, :]
    conv_w_full = jnp.transpose(g1[0::2, 8:16, 0:256], (1, 0, 2)).reshape(2, 4, D)

    b_cols = lax.dynamic_slice(b_mod, (0, chip * mcols), (2, mcols)).reshape(2, 1, mcols)
    mod_loc = _mod_fwd(c_all, w_mod, b_cols)
    g2 = _exchange("gather_mod", [mod_loc.reshape(16, mcols)], "xyc", False)[0]
    mod_full = jnp.transpose(g2[0::2], (1, 0, 2)).reshape(2, 8, 3 * D)
    mod_me = lax.dynamic_index_in_dim(mod_full, dev, axis=1, keepdims=False)

    wb_in = _cast("cast_w_in", w_in.reshape(2 * D, 2304), 256).reshape(2, D, 2304)
    late_src = [wb_in[1], _cast("cast_w_pa", w_pa.reshape(2 * ATT_W, 256), 256).reshape(2, ATT_W, 256),
                _cast("cast_w_pb", w_pb.reshape(512, D), 256).reshape(2, 256, D),
                _cast("cast_w_o", w_o.reshape(512, D), 256).reshape(2, 256, D)]
    late_chunks = [4, 2, 2, 2]
    w_in0 = _gather_weights([wb_in[0].reshape(2, D // 2, 2304)], [4])[0].reshape(N_CHIPS, D, 2304)
    chip1 = jnp.reshape(chip, (1,)).astype(jnp.int32)
    lands = [_own_slot("own_slot_" + k, a, chip1, 256) for k, a in zip(("w_in", "w_pa", "w_pb", "w_o"), late_src)]
    plan_a, nsem_a = _gather_plan(3, late_chunks[1:])
    send_a, recv_a, arrays_a, token_a = _split_start(
        "late_gather_start_a", late_src[1:] + lands[1:], plan_a, nsem_a, [w_in0, mod_me])
    plan_b, nsem_b = _gather_plan(1, late_chunks[:1])
    send_b, recv_b, arrays_b, token_b = _split_start(
        "late_gather_start_b", late_src[:1] + lands[:1], plan_b, nsem_b, [w_in0, mod_me, arrays_a[0]])
    token = token_a + token_b

    def late(after):
        got = _split_wait("late_gather_wait_a", send_a, recv_a, arrays_a, plan_a, [after])[3:]
        w_in1 = lambda later: _split_wait("late_gather_wait_b", send_b, recv_b, arrays_b, plan_b, [later])[1]
        return dict(w_in1=w_in1, w_pa=got[0], w_pb=got[1], w_o=got[2])

    small_p = []
    for l in range(2):
        gates = _gate_tiles(w_rg[l], w_ig[l]).astype(_MXU)
        small_p.append(dict(
            shift=mod_me[l:l + 1, 0:D], scale=mod_me[l:l + 1, D:2 * D], gate=mod_me[l:l + 1, 2 * D:3 * D],
            g_pre=g_pre[l:l + 1], conv_w=conv_w_full[l], conv_b=conv_b[l:l + 1], wt=gates,
            b_rg=b_rg[l:l + 1], b_ig=b_ig[l:l + 1], lam=lru_lambda[l:l + 1], g_post=g_post[l:l + 1]))

    small_p[0]["shift"] = small_p[0]["shift"] + token[0, 0]

    core = jnp.reshape(ci, (1,)).astype(jnp.int32)
    where = jnp.stack([chip, ci]).astype(jnp.int32)
    dev1 = jnp.reshape(dev, (1,)).astype(jnp.int32)
    small_plan, small_nsem = _all_plan()
    small_state = {}

    def on_smalls(l, small):
        pack = _pack_small(small)
        land = _own_slot("own_small_%d" % l, pack, dev1, _SMALL_ROWS, slots=8)
        small_state[l] = _split_start("gather_small_start_%d" % l, [pack, land], small_plan, small_nsem, [])
        return small_state[l][3][0, 0]

    def small_done(l, after):
        send, recv, arrays, _ = small_state[l]
        return _split_wait("gather_small_wait_%d" % l, send, recv, arrays, small_plan, after)[1]

    loss_loc, dx, _, finish_reduce = _local_step(x[0], loss_target[0], small_p, w_in0, late,
                                                 _GradReduce(core, where), on_smalls)
    loss = lax.psum(loss_loc, ("x", "y", "c"))
    grad_x = dx[None]
    reduced = finish_reduce([dx, small_state[0][2][0]])
    g_big ={k: a.reshape(2, 2 * a.shape[2], a.shape[3]) for k, a in zip(_BIG, reduced)}

    weights = dict(w_mod=w_mod, b_mod=b_mod, g_pre=g_pre, w_in=w_in, conv_w=conv_w, conv_b=conv_b, w_rg=w_rg,
                   b_rg=b_rg, w_ig=w_ig, b_ig=b_ig, lru_lambda=lru_lambda, w_pa=w_pa, w_pb=w_pb, w_o=w_o,
                   g_post=g_post)
    ms = dict(w_mod=m_w_mod, b_mod=m_b_mod, g_pre=m_g_pre, w_in=m_w_in, conv_w=m_conv_w, conv_b=m_conv_b,
              w_rg=m_w_rg, b_rg=m_b_rg, w_ig=m_w_ig, b_ig=m_b_ig, lru_lambda=m_lru_lambda, w_pa=m_w_pa,
              w_pb=m_w_pb, w_o=m_w_o, g_post=m_g_post)
    vs = dict(w_mod=v_w_mod, b_mod=v_b_mod, g_pre=v_g_pre, w_in=v_w_in, conv_w=v_conv_w, conv_b=v_conv_b,
              w_rg=v_w_rg, b_rg=v_b_rg, w_ig=v_w_ig, b_ig=v_b_ig, lru_lambda=v_lru_lambda, w_pa=v_w_pa,
              w_pb=v_w_pb, w_o=v_w_o, g_post=v_g_post)
    flat = dict(w_mod=(2 * D, mcols, 256), b_mod=(2, 3 * D, 2), g_pre=(2, D, 2), w_in=(2 * D, 2304, 256),
                conv_w=(8, 256, 8), conv_b=(2, D, 2), w_rg=(128, D, 128), b_rg=(2, D, 2), w_ig=(128, D, 128),
                b_ig=(2, D, 2), lru_lambda=(2, D, 2), w_pa=(2 * ATT_W, 256, 256), w_pb=(512, D, 256),
                w_o=(512, D, 256), g_post=(2, D, 2))
    order = ("w_mod", "b_mod", "g_pre", "w_in", "conv_w", "conv_b", "w_rg", "b_rg", "w_ig", "b_ig",
             "lru_lambda", "w_pa", "w_pb", "w_o", "g_post")

    def adam(k, g):
        rows, cols, tb = flat[k]
        return _adamw("adamw_" + k, weights[k].reshape(rows, cols), g.reshape(rows, cols),
                      ms[k].reshape(rows, cols), vs[k].reshape(rows, cols), tb)

    stepped = {k: adam(k, g_big[k]) for k in _BIG}

    g3 = [small_done(l, [stepped["w_in"][0]]) for l in range(2)]
    tot =[_sum_lead("sum_small_%d" % l, g3[l], _SMALL_ROWS) for l in range(2)]
    dmod_all = jnp.stack([g3[l][:, 0:3, :].reshape(8, 3 * D) for l in range(2)], axis=0)
    dm_cols = lax.dynamic_slice(dmod_all, (0, 0, chip * mcols), (2, 8, mcols))
    g_w_mod = _mod_bwd(jnp.transpose(c_all), dm_cols)
    both = lambda first, rows: jnp.stack([tot[l][first:first + rows] for l in range(2)], axis=0)
    vec = both(8, 6)
    grads = dict(
        w_mod=g_w_mod, b_mod=both(0, 3).reshape(2, 3 * D), g_pre=vec[:, 0], w_in=g_big["w_in"],
        conv_w=lax.dynamic_slice(both(16, 4), (0, 0, chip * 256), (2, 4, 256)), conv_b=vec[:, 1],
        w_rg=both(24, 64).reshape(2, 16, 64, 64), b_rg=vec[:, 2], w_ig=both(88, 64).reshape(2, 16, 64, 64),
        b_ig=vec[:, 3], lru_lambda=vec[:, 4], w_pa=g_big["w_pa"], w_pb=g_big["w_pb"], w_o=g_big["w_o"],
        g_post=vec[:, 5])
    for k in order:
        if k not in stepped:
            stepped[k] = adam(k, grads[k])
    deltas, new_m, new_v = ([stepped[k][t].reshape(weights[k].shape) for k in order] for t in range(3))
    return (loss, grad_x, *[grads[k].reshape(weights[k].shape) for k in order], *deltas, *new_m, *new_v)
```

```python
import functools

import jax
import jax.numpy as jnp
from jax import lax
from jax.experimental import pallas as pl
from jax.experimental.pallas import tpu as pltpu

_F32 = jnp.float32
_MXU = jnp.bfloat16
_VMEM_LIMIT = 56 * 1024 * 1024
_MESH = pl.DeviceIdType.MESH

D = 1024
HEAD = 128
HEADS = 4
ATT_W = 512
QKV_W = 1536
IN_W = 9216
DILATIONS = (1, 4, 16)
BAND = 128
QBLK = BAND * 16
NORM_EPS = 1e-6
NEG_INF = -1e30
LRU_C = 8.0
N_CHIPS = 4
CB_GATT = 4608 // 512
CB_U, CB_GLRU, CB_MA, CB_MB = 5, 6, 7, 8
R_U, R_GLRU, R_MA, R_MB, R_END = 512, 1536, 2560, 3584, 4608

ADAM_LR, ADAM_B1, ADAM_B2, ADAM_EPS, ADAM_WD, ADAM_STEP = 0.001, 0.9, 0.999, 1e-08, 0.01, 10


def _params(ngrid):
    return pltpu.CompilerParams(dimension_semantics=("arbitrary",) * ngrid, vmem_limit_bytes=_VMEM_LIMIT)


def _sigmoid(v):
    return 0.5 * jnp.tanh(0.5 * v) + 0.5


_GROUPS = {
    "c": [(0, 0, 1)],
    "xy": [(1, 0, 0), (0, 1, 0), (1, 1, 0)],
    "xyc": [(0, 0, 1), (0, 1, 0), (0, 1, 1), (1, 0, 0), (1, 0, 1), (1, 1, 0), (1, 1, 1)],
}


def _rank(group, px, py, pc):
    if group == "c":
        return pc
    if group == "xy":
        return 2 * px + py
    return 4 * px + 2 * py + pc


def _flip(rel, x, y, c):
    dx, dy, dc = rel
    return (1 - x if dx else x, 1 - y if dy else y, 1 - c if dc else c)


def _pieces(ref, nchunk):
    step = ref.shape[0] // nchunk
    return [ref.at[pl.ds(q * step, step)] for q in range(nchunk)]


def _exchange(name, srcs, group, scatter, *, local=True, nchunks=None):
    rels = _GROUPS[group]
    gsize = len(rels) + 1
    n = len(srcs)
    nchunks = nchunks or [1] * n
    blks = [s.shape[1:] if scatter else s.shape for s in srcs]
    slotted = local or gsize > 2
    base = [sum(nchunks[:a]) for a in range(n)]
    tot = sum(nchunks)

    def body(*refs):
        src_refs, out_refs = refs[:n], refs[n:2 * n]
        send_sems, recv_sems, loc_sems = refs[2 * n:]
        x, y, c = lax.axis_index("x"), lax.axis_index("y"), lax.axis_index("c")
        me = _rank(group, x, y, c)
        copies = []
        for a in range(n):
            def part(r, a=a):
                return src_refs[a].at[r] if scatter else src_refs[a]
            dst = out_refs[a].at[me] if slotted else out_refs[a]
            if local:
                for q, (s_, d_) in enumerate(zip(_pieces(part(me), nchunks[a]), _pieces(dst, nchunks[a]))):
                    loc = pltpu.make_async_copy(s_, d_, loc_sems.at[base[a] + q])
                    loc.start()
                    copies.append(loc)
            for k, rel in enumerate(rels):
                peer = _flip(rel, x, y, c)
                for q, (s_, d_) in enumerate(zip(_pieces(part(_rank(group, *peer)), nchunks[a]),
                                                 _pieces(dst, nchunks[a]))):
                    cp = pltpu.make_async_remote_copy(
                        src_ref=s_, dst_ref=d_, send_sem=send_sems.at[(base[a] + q) * len(rels) + k],
                        recv_sem=recv_sems.at[(base[a] + q) * len(rels) + k],
                        device_id=peer, device_id_type=_MESH)
                    cp.start()
                    copies.append(cp)
        for cp in copies:
            cp.wait()

    any_spec = pl.BlockSpec(memory_space=pl.ANY)
    lead = (gsize,) if slotted else ()
    return pl.pallas_call(
        body, name=name,
        out_shape=[jax.ShapeDtypeStruct(lead + tuple(b), s.dtype) for b, s in zip(blks, srcs)],
        in_specs=[any_spec] * n, out_specs=[any_spec] * n,
        scratch_shapes=[pltpu.SemaphoreType.DMA((tot * len(rels),)), pltpu.SemaphoreType.DMA((tot * len(rels),)),
                        pltpu.SemaphoreType.DMA((tot,))],
    )(*srcs)


def _gather_weights(wb, nchunks):
    n = len(wb)
    rels = _GROUPS["xy"]
    base = [sum(nchunks[:a]) for a in range(n)]
    tot = sum(nchunks)

    def body(*refs):
        src_refs, out_refs = refs[:n], refs[n:2 * n]
        ici_send, ici_recv, d2d_send, d2d_recv, loc_sems = refs[2 * n:]
        x, y, c = lax.axis_index("x"), lax.axis_index("y"), lax.axis_index("c")
        me = 2 * x + y
        waits = []
        for a in range(n):
            for l in range(2):
                for q, (s_, d_) in enumerate(zip(_pieces(src_refs[a].at[l], nchunks[a]),
                                                 _pieces(out_refs[a].at[me, l], nchunks[a]))):
                    loc = pltpu.make_async_copy(s_, d_, loc_sems.at[(base[a] + q) * 2 + l])
                    loc.start()
                    waits.append(loc)
        first = []
        for a in range(n):
            for k, rel in enumerate(rels):
                px, py, _ = _flip(rel, x, y, c)
                for q, (s_, d_) in enumerate(zip(_pieces(src_refs[a].at[c], nchunks[a]),
                                                 _pieces(out_refs[a].at[me, c], nchunks[a]))):
                    sem = (base[a] + q) * 3 + k
                    cp = pltpu.make_async_remote_copy(src_ref=s_, dst_ref=d_, send_sem=ici_send.at[sem],
                                                      recv_sem=ici_recv.at[sem], device_id=(px, py, c),
                                                      device_id_type=_MESH)
                    cp.start()
                    first.append(cp)
        second = []
        for a in range(n):
            for k, rel in enumerate(rels):
                px, py, _ = _flip(rel, x, y, c)
                for q, blk in enumerate(_pieces(out_refs[a].at[2 * px + py, c], nchunks[a])):
                    sem = (base[a] + q) * 3 + k
                    landed = pltpu.make_async_remote_copy(src_ref=blk, dst_ref=blk, send_sem=ici_send.at[sem],
                                                          recv_sem=ici_recv.at[sem], device_id=(px, py, c),
                                                          device_id_type=_MESH)
                    landed.wait_recv()
                    cp = pltpu.make_async_remote_copy(src_ref=blk, dst_ref=blk, send_sem=d2d_send.at[sem],
                                                      recv_sem=d2d_recv.at[sem], device_id=(x, y, 1 - c),
                                                      device_id_type=_MESH)
                    cp.start()
                    second.append(cp)
        for cp in first:
            cp.wait_send()
        for cp in second:
            cp.wait_send()
        for a in range(n):
            for k, rel in enumerate(rels):
                px, py, _ = _flip(rel, x, y, c)
                for q, blk in enumerate(_pieces(out_refs[a].at[2 * px + py, 1 - c], nchunks[a])):
                    sem = (base[a] + q) * 3 + k
                    pltpu.make_async_remote_copy(src_ref=blk, dst_ref=blk, send_sem=d2d_send.at[sem],
                                                 recv_sem=d2d_recv.at[sem], device_id=(x, y, 1 - c),
                                                 device_id_type=_MESH).wait_recv()
        for cp in waits:
            cp.wait()

    any_spec = pl.BlockSpec(memory_space=pl.ANY)
    return pl.pallas_call(
        body, name="gather_weights",
        out_shape=[jax.ShapeDtypeStruct((N_CHIPS,) + a.shape, a.dtype) for a in wb],
        in_specs=[any_spec] * n, out_specs=[any_spec] * n,
        scratch_shapes=[pltpu.SemaphoreType.DMA((tot * 3,))] * 4 + [pltpu.SemaphoreType.DMA((tot * 2,))],
    )(*wb)


_HBM = pl.BlockSpec(memory_space=pltpu.HBM)
_SEM = pl.BlockSpec(memory_space=pltpu.SEMAPHORE)
_EFFECT = pltpu.SideEffectType.DATAFLOW_SIDE_EFFECTING


def _own_slot(name, src, chip, tb, slots=N_CHIPS):
    rows, cols = src.shape[-2:]
    lead = src.shape[:-2]
    flat = src.reshape((-1, cols))

    def body(s_ref, a_ref, o_ref):
        o_ref[...] = a_ref[...]

    grid_spec = pltpu.PrefetchScalarGridSpec(
        num_scalar_prefetch=1, grid=(flat.shape[0] // tb,),
        in_specs=[pl.BlockSpec((tb, cols), lambda i, s: (i, 0))],
        out_specs=pl.BlockSpec((None, tb, cols), lambda i, s: (s[0], i, 0)))
    out = pl.pallas_call(body, name=name, grid_spec=grid_spec,
                         out_shape=jax.ShapeDtypeStruct((slots,) + flat.shape, src.dtype),
                         compiler_params=_params(1))(chip, flat)
    return out.reshape((slots,) + lead + (rows, cols))


def _numbered(pairs, peer, send_sems, recv_sems, first):
    return [pltpu.make_async_remote_copy(src_ref=s_, dst_ref=d_, send_sem=send_sems.at[first + q],
                                         recv_sem=recv_sems.at[first + q], device_id=peer, device_id_type=_MESH)
            for q, (s_, d_) in enumerate(pairs)]


def _gather_plan(n, nchunks):
    def plan(refs, send_sems, recv_sems):
        x, y, c = lax.axis_index("x"), lax.axis_index("y"), lax.axis_index("c")
        me = 2 * x + y
        copies = []
        for a in range(n):
            for rel in _GROUPS["xy"]:
                px, py, _ = _flip(rel, x, y, c)
                pairs = list(zip(_pieces(refs[a], nchunks[a]), _pieces(refs[n + a].at[me], nchunks[a])))
                copies += _numbered(pairs, (px, py, c), send_sems, recv_sems, len(copies))
        return copies
    return plan, 3 * sum(nchunks)


def _all_plan():
    def plan(refs, send_sems, recv_sems):
        x, y, c = lax.axis_index("x"), lax.axis_index("y"), lax.axis_index("c")
        me = 4 * x + 2 * y + c
        copies = []
        for rel in _GROUPS["xyc"]:
            copies += _numbered([(refs[0], refs[1].at[me])], _flip(rel, x, y, c), send_sems, recv_sems, len(copies))
        return copies
    return plan, len(_GROUPS["xyc"])


def _pair_plan(n, nchunks):
    def plan(refs, send_sems, recv_sems):
        x, y, c = lax.axis_index("x"), lax.axis_index("y"), lax.axis_index("c")
        copies = []
        for a in range(n):
            for j in range(N_CHIPS):
                pairs = list(zip(_pieces(refs[a].at[j, 1 - c], nchunks[a]), _pieces(refs[n + a].at[j], nchunks[a])))
                copies += _numbered(pairs, (x, y, 1 - c), send_sems, recv_sems, len(copies))
        return copies
    return plan, N_CHIPS * sum(nchunks)


def _chips_plan(n, nchunks):
    def plan(refs, send_sems, recv_sems):
        x, y, c = lax.axis_index("x"), lax.axis_index("y"), lax.axis_index("c")
        me = 2 * x + y
        copies = []
        for a in range(n):
            for rel in _GROUPS["xy"]:
                px, py, _ = _flip(rel, x, y, c)
                pairs = list(zip(_pieces(refs[a].at[2 * px + py], nchunks[a]), _pieces(refs[n + a].at[me], nchunks[a])))
                copies += _numbered(pairs, (px, py, c), send_sems, recv_sems, len(copies))
        return copies
    return plan, 3 * sum(nchunks)


def _fill_plan(n, nchunks, l):
    def plan(refs, send_sems, recv_sems):
        x, y, c = lax.axis_index("x"), lax.axis_index("y"), lax.axis_index("c")
        copies = []
        for a in range(n):
            blk = _pieces(refs[a].at[l, c], nchunks[a])
            copies += _numbered(list(zip(blk, blk)), (x, y, 1 - c), send_sems, recv_sems, len(copies))
        return copies
    return plan, sum(nchunks)


def _split_start(name, arrays, plan, nsem, after):
    n = len(arrays)
    na = len(after)

    def body(*refs):
        send_sems, recv_sems = refs[n + na], refs[n + na + 1]
        token = refs[-1]
        for cp in plan(refs[:n], send_sems, recv_sems):
            cp.start()
        token[...] = jnp.zeros_like(token)

    hbm = [pltpu.HBM(a.shape, a.dtype) for a in arrays]
    outs = pl.pallas_call(
        body, name=name,
        out_shape=(pltpu.SemaphoreType.DMA((nsem,)), pltpu.SemaphoreType.DMA((nsem,)), *hbm, _sds((8, 128))),
        in_specs=[_HBM] * n + [pl.BlockSpec(memory_space=pl.ANY)] * na,
        out_specs=(_SEM, _SEM, *([_HBM] * n), pl.BlockSpec(memory_space=pltpu.VMEM)),
        input_output_aliases={i: 2 + i for i in range(n)},
        compiler_params=pltpu.CompilerParams(has_side_effects=_EFFECT),
    )(*[pltpu.with_memory_space_constraint(a, pltpu.HBM) for a in arrays], *after)
    return outs[0], outs[1], list(outs[2:2 + n]), outs[-1]


def _split_wait(name, send_sems, recv_sems, arrays, plan, after):
    n = len(arrays)

    def body(*refs):
        for cp in plan(refs[:n], refs[n], refs[n + 1]):
            cp.wait_send()
            cp.wait_recv()

    hbm = [pltpu.HBM(a.shape, a.dtype) for a in arrays]
    return list(pl.pallas_call(
        body, name=name, out_shape=tuple(hbm),
        in_specs=[_HBM] * n + [_SEM, _SEM] + [pl.BlockSpec(memory_space=pl.ANY)] * len(after),
        out_specs=tuple([_HBM] * n), input_output_aliases={i: i for i in range(n)},
        compiler_params=pltpu.CompilerParams(has_side_effects=_EFFECT),
    )(*arrays, send_sems, recv_sems, *after))


def _mm(name, a, b, out_sds, *, grid, a_spec, b_spec, o_spec, dims, acc_shape, into=None):
    nk = grid[2]

    def body(*refs):
        a_ref, b_ref = refs[0], refs[1]
        o_ref, acc = refs[-2], refs[-1]
        k = pl.program_id(2)
        part = lax.dot_general(a_ref[...].astype(_MXU), b_ref[...].astype(_MXU), dims,
                               preferred_element_type=_F32)
        if nk == 1:
            o_ref[...] = part.astype(o_ref.dtype)
            return

        @pl.when(k == 0)
        def _():
            acc[...] = part

        @pl.when(k > 0)
        def _():
            acc[...] += part

        @pl.when(k == nk - 1)
        def _():
            o_ref[...] = acc[...].astype(o_ref.dtype).reshape(o_ref.shape)

    if nk == 1:
        acc_shape = (8, 128)
    in_specs = [a_spec, b_spec]
    args = [a, b]
    aliases = {}
    if into is not None:
        in_specs.append(pl.BlockSpec(memory_space=pl.ANY))
        args.append(into)
        aliases = {2: 0}
    return pl.pallas_call(
        body, name=name, grid=grid, in_specs=in_specs, out_specs=o_spec, out_shape=out_sds,
        scratch_shapes=[pltpu.VMEM(acc_shape, _F32)], input_output_aliases=aliases,
        compiler_params=_params(3))(*args)


_NN = (((1,), (0,)), ((), ()))
_NT = (((1,), (1,)), ((), ()))
_TN = (((0,), (0,)), ((), ()))


def _rowwise(name, body, *, grid, ins, outs, scratch=()):
    return pl.pallas_call(
        body, name=name, grid=(grid,), in_specs=[s for _, s in ins], out_specs=[s for _, s in outs],
        out_shape=[o for o, _ in outs], scratch_shapes=list(scratch),
        compiler_params=_params(1))(*[a for a, _ in ins])


def _rows(tb, w, cb=0, n=None):
    if n is None:
        return pl.BlockSpec((tb, w), lambda i: (i, cb))
    return pl.BlockSpec((tb, w), lambda i: (n - 1 - i, cb))


def _vec(shape):
    return pl.BlockSpec(shape, lambda i: (0,) * len(shape))


def _halo_prev(tb, w, cb=0, n=None, rows=8):
    if n is None:
        return pl.BlockSpec((rows, w), lambda i: (jnp.maximum(i * (tb // rows) - 1, 0), cb))
    return pl.BlockSpec((rows, w), lambda i: (jnp.maximum((n - 1 - i) * (tb // rows) - 1, 0), cb))


def _halo_next(tb, w, n, cb=0):
    return pl.BlockSpec((8, w), lambda i: (jnp.minimum((i + 1) * (tb // 8), n * (tb // 8) - 1), cb))


def _sds(shape, dtype=_F32):
    return jax.ShapeDtypeStruct(shape, dtype)


def _cast(name, a, tb):
    rows, cols = a.shape

    def body(a_ref, o_ref):
        o_ref[...] = a_ref[...].astype(o_ref.dtype)

    return _rowwise(name, body, grid=rows // tb, ins=[(a, _rows(tb, cols))],
                    outs=[(_sds((rows, cols), _MXU), _rows(tb, cols))])[0]


def _sum_lead(name, a, tb):
    g, rows, cols = a.shape

    def body(a_ref, o_ref):
        acc = a_ref[0]
        for k in range(1, g):
            acc = acc + a_ref[k]
        o_ref[...] = acc

    return _rowwise(name, body, grid=rows // tb,
                    ins=[(a, pl.BlockSpec((g, tb, cols), lambda i: (0, i, 0)))],
                    outs=[(_sds((rows, cols)), _rows(tb, cols))])[0]


def _sum_pair(name, mine, theirs, core, tb):
    nj, _, rows, cols = mine.shape

    def body(s_ref, a_ref, b_ref, o_ref, ob_ref):
        t = a_ref[...] + b_ref[...]
        o_ref[...] = t
        ob_ref[...] = t.astype(ob_ref.dtype)

    blk = pl.BlockSpec((None, tb, cols), lambda j, i, s: (j, i, 0))
    grid_spec = pltpu.PrefetchScalarGridSpec(
        num_scalar_prefetch=1, grid=(nj, rows // tb),
        in_specs=[pl.BlockSpec((None, None, tb, cols), lambda j, i, s: (j, s[0], i, 0)), blk],
        out_specs=[blk, blk])
    return pl.pallas_call(body, name=name, grid_spec=grid_spec,
                          out_shape=[_sds((nj, rows, cols)), _sds((nj, rows, cols), _MXU)],
                          compiler_params=_params(2))(core, mine, theirs)


def _sum_chips(name, mine, theirs, where, l, into, tb):
    _, rows, cols = mine.shape
    extra = [] if into is None else [into]

    def body(*refs):
        a_ref, b1_ref, b2_ref, b3_ref = refs[1:5]
        o_ref = refs[-1]
        o_ref[...] = ((a_ref[...] + b1_ref[...].astype(_F32)) + b2_ref[...].astype(_F32)) + b3_ref[...].astype(_F32)

    def slot(k):
        return pl.BlockSpec((None, tb, cols), lambda i, s: (jnp.bitwise_xor(s[0], k), i, 0))

    grid_spec = pltpu.PrefetchScalarGridSpec(
        num_scalar_prefetch=1, grid=(rows // tb,),
        in_specs=[slot(0), slot(1), slot(2), slot(3)] + [pl.BlockSpec(memory_space=pl.ANY)] * len(extra),
        out_specs=pl.BlockSpec((None, None, tb, cols), lambda i, s: (l, s[1], i, 0)))
    return pl.pallas_call(body, name=name, grid_spec=grid_spec, out_shape=_sds((2, 2, rows, cols)),
                          input_output_aliases={5: 0} if extra else {},
                          compiler_params=_params(1))(where, mine, theirs, theirs, theirs, *extra)


def _adamw(name, w, g, m, v, tb):
    rows, cols = w.shape
    c1 = 1.0 - ADAM_B1 ** ADAM_STEP
    c2 = 1.0 - ADAM_B2 ** ADAM_STEP

    def body(w_ref, g_ref, m_ref, v_ref, d_ref, nm_ref, nv_ref):
        gv = g_ref[...]
        nm = ADAM_B1 * m_ref[...] + (1.0 - ADAM_B1) * gv
        nv = ADAM_B2 * v_ref[...] + (1.0 - ADAM_B2) * (gv * gv)
        d_ref[...] = -ADAM_LR * ((nm / c1) / (jnp.sqrt(nv / c2) + ADAM_EPS) + ADAM_WD * w_ref[...])
        nm_ref[...] = nm
        nv_ref[...] = nv

    spec = _rows(tb, cols)
    return _rowwise(name, body, grid=rows // tb, ins=[(w, spec), (g, spec), (m, spec), (v, spec)],
                    outs=[(_sds((rows, cols)), spec)] * 3)


def _mod_fwd(c_all, w_mod, b_cols):
    cols = w_mod.shape[2]

    def body(c_ref, w_ref, b_ref, o_ref):
        cv = c_ref[...]
        sc = (cv * _sigmoid(cv)).astype(_MXU)
        o_ref[...] = jnp.dot(sc, w_ref[...].astype(_MXU), preferred_element_type=_F32) + b_ref[...]

    return pl.pallas_call(
        body, name="mod_fwd", grid=(2,),
        in_specs=[pl.BlockSpec((8, D), lambda l: (0, 0)), pl.BlockSpec((None, D, cols), lambda l: (l, 0, 0)),
                  pl.BlockSpec((None, 1, cols), lambda l: (l, 0, 0))],
        out_specs=pl.BlockSpec((None, 8, cols), lambda l: (l, 0, 0)),
        out_shape=_sds((2, 8, cols)), compiler_params=_params(1))(c_all, w_mod, b_cols)


def _mod_bwd(c_all_t, dm):
    cols = dm.shape[2]

    def body(c_ref, d_ref, o_ref):
        cv = c_ref[...]
        sc = (cv * _sigmoid(cv)).astype(_MXU)
        o_ref[...] = jnp.dot(sc, d_ref[...].astype(_MXU), preferred_element_type=_F32)

    return pl.pallas_call(
        body, name="mod_bwd", grid=(2,),
        in_specs=[pl.BlockSpec((D, 8), lambda l: (0, 0)), pl.BlockSpec((None, 8, cols), lambda l: (l, 0, 0))],
        out_specs=pl.BlockSpec((None, D, cols), lambda l: (l, 0, 0)),
        out_shape=_sds((2, D, cols)), compiler_params=_params(1))(c_all_t, dm)


def _proj(x, g_pre, shift, scale, w_in):
    s = x.shape[0]
    tm = 1024

    def body(x_ref, g_ref, sh_ref, sc_ref, w_ref, o_ref, ht_ref, h_s):
        @pl.when(pl.program_id(1) == 0)
        def _():
            xv = x_ref[...]
            rstd = lax.rsqrt(jnp.mean(xv * xv, axis=-1, keepdims=True) + NORM_EPS)
            hv = (xv * rstd) * g_ref[...] * (1.0 + sc_ref[...]) + sh_ref[...]
            h_s[...] = hv.astype(h_s.dtype)
            ht_ref[...] = hv.T.astype(ht_ref.dtype)

        o_ref[...] = jnp.dot(h_s[...], w_ref[...], preferred_element_type=_F32).astype(o_ref.dtype)

    vec = pl.BlockSpec((1, D), lambda m, n: (0, 0))
    return pl.pallas_call(
        body, name="proj", grid=(s // tm, N_CHIPS),
        in_specs=[pl.BlockSpec((tm, D), lambda m, n: (m, 0)), vec, vec, vec,
                  pl.BlockSpec((None, D, 2304), lambda m, n: (n, 0, 0))],
        out_specs=[pl.BlockSpec((tm, 2304), lambda m, n: (m, n)), pl.BlockSpec((D, tm), lambda m, n: (0, m))],
        out_shape=[_sds((s, IN_W), _MXU), _sds((D, s), _MXU)],
        scratch_shapes=[pltpu.VMEM((tm, D), _MXU)], compiler_params=_params(2))(x, g_pre, shift, scale, w_in)


def _shift_down(cur, halo, j, tb):
    ext = jnp.concatenate([halo, cur], axis=0)
    return pltpu.roll(ext, j, 0)[8:8 + tb]


def _shift_up(cur, halo, j, tb):
    ext = jnp.concatenate([cur, halo], axis=0)
    return pltpu.roll(ext, tb + 8 - j, 0)[0:tb]


def _conv(u_ref, halo_ref, w_ref, b_ref, first, tb):
    u = u_ref[...].astype(_F32)
    halo = jnp.where(first, 0.0, halo_ref[...].astype(_F32)[8:16])
    acc = b_ref[...] + u * w_ref[0:1, :]
    for j in range(1, 4):
        acc = acc + _shift_down(u, halo, j, tb) * w_ref[j:j + 1, :]
    return acc


def _lru_gates(pre_r, pre_i, uc, b_rg, b_ig, lam):
    r = _sigmoid(pre_r + b_rg)
    ig = _sigmoid(pre_i + b_ig)
    nl = -lam
    sp = jnp.maximum(nl, 0.0) + jnp.log(1.0 + jnp.exp(-jnp.abs(nl)))
    la = -LRU_C * r * sp
    a = jnp.exp(la)
    one_m_a2 = -jnp.tanh(la) * (a * a + 1.0)
    inv_sq = lax.rsqrt(jnp.maximum(one_m_a2, 1e-30))
    return r, ig, sp, a, one_m_a2 * inv_sq, inv_sq


GATE_TILES = 8


def _gate_tiles(w_rg, w_ig):
    eye = jnp.eye(2, dtype=w_rg.dtype)

    def tiles(w):
        return jnp.einsum("cpij,pq->cpiqj", w.reshape(GATE_TILES, 2, 64, 64), eye).reshape(GATE_TILES, 128, 128)

    return jnp.concatenate([tiles(w_rg), tiles(w_ig)], axis=2)


def _gate_tile_grads(gw):
    keep = jnp.eye(2, dtype=jnp.bool_)[None, :, None, :, None]

    def blocks(t):
        t5 = t.reshape(GATE_TILES, 2, 64, 2, 64)
        return jnp.sum(jnp.where(keep, t5, 0.0), axis=3).reshape(16, 64, 64)

    return blocks(gw[:, :, 0:128]), blocks(gw[:, :, 128:256])


def _gate_preacts(ucv, wt_ref):
    ucb = ucv.astype(_MXU)
    ps = [jnp.dot(ucb[:, 128 * c:128 * (c + 1)], wt_ref[c], preferred_element_type=_F32) for c in range(GATE_TILES)]
    pre_r = jnp.concatenate([p[:, 0:128] for p in ps], axis=1)
    pre_i = jnp.concatenate([p[:, 128:256] for p in ps], axis=1)
    return pre_r, pre_i


def _scan_fwd(proj, conv_w, conv_b, wt, b_rg, b_ig, lam):
    s = proj.shape[0]
    tb = 256

    def body(u_ref, up_ref, cw_ref, cb_ref, wt_ref, brg_ref, big_ref, lam_ref, h_ref, carry, a_s, b_s):
        i = pl.program_id(0)

        @pl.when(i == 0)
        def _():
            carry[...] = jnp.zeros_like(carry)

        ucv = _conv(u_ref, up_ref, cw_ref, cb_ref, i == 0, tb)
        pre_r, pre_i = _gate_preacts(ucv, wt_ref)
        _, ig, _, a, sq, _ = _lru_gates(pre_r, pre_i, ucv, brg_ref[...], big_ref[...], lam_ref[...])
        av = a
        bv = sq * (ig * ucv)
        av = av.reshape(tb // 8, 8, D)
        bv = bv.reshape(tb // 8, 8, D)
        row8 = lax.broadcasted_iota(jnp.int32, (1, 8, 1), 1)
        for sh in (1, 2, 4):
            m = row8 >= sh
            b_sh = pltpu.roll(bv, sh, 1)
            a_sh = pltpu.roll(av, sh, 1)
            bv = jnp.where(m, av * b_sh + bv, bv)
            av = jnp.where(m, av * a_sh, av)
        a_s[...] = av.reshape(tb, D)
        b_s[...] = bv.reshape(tb, D)

        def tile(t, state):
            rows = pl.ds(pl.multiple_of(t * 8, 8), 8)
            hv = b_s[rows, :] + a_s[rows, :] * state
            b_s[rows, :] = hv
            return jnp.broadcast_to(hv[7:8, :], (8, D))

        carry[...] = lax.fori_loop(0, tb // 8, tile, jnp.broadcast_to(carry[7:8, :], (8, D)), unroll=4)
        h_ref[...] = b_s[...].astype(h_ref.dtype)

    v = _vec((1, D))
    return _rowwise("scan_fwd", body, grid=s // tb,
                    ins=[(proj, _rows(tb, D, CB_U)), (proj, _halo_prev(tb, D, CB_U, rows=16)),
                         (conv_w, _vec((4, D))), (conv_b, v), (wt, _vec((GATE_TILES, 128, 256))),
                         (b_rg, v), (b_ig, v), (lam, v)],
                    outs=[(_sds((s, D), _MXU), _rows(tb, D))],
                    scratch=[pltpu.VMEM((8, D), _F32), pltpu.VMEM((tb, D), _F32), pltpu.VMEM((tb, D), _F32)])[0]


def _weight_specs(l):
    return [pl.BlockSpec((N_CHIPS, None, ATT_W, 256), lambda i: (0, l, 0, 0)),
            pl.BlockSpec((N_CHIPS, None, 256, D), lambda i: (0, l, 0, 0)),
            pl.BlockSpec((N_CHIPS, None, 256, D), lambda i: (0, l, 0, 0))]


def _tail_fwd(l, o, h_lru, proj, x, gate, g_post, gw, target):
    s = x.shape[0]
    tb = 512

    def body(*refs):
        o_ref, h_ref, ga_ref, gl_ref, ma_ref, mb_ref, x_ref, gt_ref, gp_ref, wpa_ref, wpb_ref, wo_ref = refs[0:12]
        aa_ref, ba_ref, ya_ref, yb_ref, z_ref, out_ref = refs[-8:-2] if target is not None else refs[-7:-1]
        ga = ga_ref[...].astype(_F32)
        aa32 = o_ref[...].astype(_F32) * (ga * _sigmoid(ga))
        aa = aa32.astype(_MXU)
        aa_ref[...] = aa32.T.astype(aa_ref.dtype)
        gl = gl_ref[...].astype(_F32)
        ba32 = h_ref[...].astype(_F32) * (gl * _sigmoid(gl))
        ba = ba32.astype(_MXU)
        ba_ref[...] = ba32.T.astype(ba_ref.dtype)
        ya = jnp.concatenate([jnp.dot(aa, wpa_ref[j], preferred_element_type=_F32) for j in range(N_CHIPS)], axis=1)
        ya_ref[...] = ya.astype(ya_ref.dtype)
        yb = jnp.dot(ba, wpb_ref[...].reshape(D, D), preferred_element_type=_F32)
        yb_ref[...] = yb.astype(yb_ref.dtype)
        z32 = _sigmoid(ma_ref[...].astype(_F32)) * ya + _sigmoid(mb_ref[...].astype(_F32)) * yb
        z = z32.astype(_MXU)
        z_ref[...] = z32.T.astype(z_ref.dtype)
        ov = jnp.dot(z, wo_ref[...].reshape(D, D), preferred_element_type=_F32)
        out_ref[...] = ov.astype(out_ref.dtype)
        rstd = lax.rsqrt(jnp.mean(ov * ov, axis=-1, keepdims=True) + NORM_EPS)
        xn =x_ref[...] + gt_ref[...] * ((ov * rstd) * gp_ref[...])
        if target is None:
            refs[-1][...] = xn
        else:
            dy_ref, acc_ref = refs[-2], refs[-1]
            err = xn - refs[12][...]
            dy_ref[...] = err * (1.0 / D)
            _zero_first(pl.program_id(0), acc_ref)
            acc_ref[...] += jnp.sum(err * err, axis=0, keepdims=True)

    v = _vec((1, D))
    r = _rows(tb, D)
    r5 = _rows(tb, ATT_W)
    weights = list(zip((gw["w_pa"], gw["w_pb"], gw["w_o"]), _weight_specs(l)))
    cols = pl.BlockSpec((D, tb), lambda i: (0, i))
    head_in = [] if target is None else [(target, r)]
    head_out = [] if target is None else [(_sds((1, D)), v)]
    return _rowwise("tail_fwd" if target is None else "tail_loss_fwd", body, grid=s // tb,
                    ins=[(o, r5), (h_lru, r), (proj, _rows(tb, ATT_W, CB_GATT)), (proj, _rows(tb, D, CB_GLRU)),
                         (proj, _rows(tb, D, CB_MA)), (proj, _rows(tb, D, CB_MB)), (x, r), (gate, v), (g_post, v)]
                    + weights + head_in,
                    outs=[(_sds((ATT_W, s), _MXU), pl.BlockSpec((ATT_W, tb), lambda i: (0, i))),
                          (_sds((D, s), _MXU), cols), (_sds((s, D), _MXU), r), (_sds((s, D), _MXU), r),
                          (_sds((D, s), _MXU), cols), (_sds((s, D), _MXU), r), (_sds((s, D)), r)]
                    + head_out)


def _zero_first(i, *refs):
    @pl.when(i == 0)
    def _():
        for ref in refs:
            ref[...] = jnp.zeros_like(ref)


def _tail_bwd(l, dx, out, y_a, y_b, proj, o, h_lru, gate, g_post, gw):
    s = dx.shape[0]
    tb = 256

    def body(dx_ref, out_ref, ya_ref, yb_ref, ma_ref, mb_ref, o_ref, ga_ref, h_ref, gl_ref, gt_ref, gp_ref,
             wpa_ref, wpb_ref, wo_ref,
             dout_ref, dya_ref, dyb_ref, rest_ref, do_ref, dh_ref, dgt_ref, dgp_ref):
        i = pl.program_id(0)
        ov = out_ref[...].astype(_F32)
        dxv = dx_ref[...]
        rstd = lax.rsqrt(jnp.mean(ov * ov, axis=-1, keepdims=True) + NORM_EPS)
        nv = ov * rstd
        s_dn = jnp.sum(dxv * nv, axis=0, keepdims=True)
        _zero_first(i, dgt_ref, dgp_ref)
        dgt_ref[...] += s_dn * gp_ref[...]
        dgp_ref[...] += s_dn * gt_ref[...]
        dn = dxv * (gt_ref[...] * gp_ref[...])
        d_out = (rstd * (dn - nv * jnp.mean(dn * nv, axis=-1, keepdims=True))).astype(_MXU)
        dout_ref[...] = d_out
        dz = lax.dot_general(d_out, wo_ref[...].reshape(D, D), _NT, preferred_element_type=_F32)
        ga = _sigmoid(ma_ref[...].astype(_F32))
        gb = _sigmoid(mb_ref[...].astype(_F32))
        dya = (dz * ga).astype(_MXU)
        dyb = (dz * gb).astype(_MXU)
        dya_ref[...] = dya
        dyb_ref[...] = dyb
        rest_ref[:, R_MA:R_MB] = (dz * ya_ref[...].astype(_F32) * ga * (1.0 - ga)).astype(rest_ref.dtype)
        rest_ref[:, R_MB:R_END] = (dz * yb_ref[...].astype(_F32) * gb * (1.0 - gb)).astype(rest_ref.dtype)
        daa = lax.dot_general(dya[:, 0:256], wpa_ref[0], _NT, preferred_element_type=_F32)
        for j in range(1, N_CHIPS):
            daa = daa + lax.dot_general(dya[:, j * 256:(j + 1) * 256], wpa_ref[j], _NT, preferred_element_type=_F32)
        dba = lax.dot_general(dyb, wpb_ref[...].reshape(D, D), _NT, preferred_element_type=_F32)
        gav = ga_ref[...].astype(_F32)
        sa = _sigmoid(gav)
        do_ref[...] = daa * (gav * sa)
        rest_ref[:, 0:R_U] = (daa * o_ref[...].astype(_F32) * (sa * (1.0 + gav * (1.0 - sa)))).astype(rest_ref.dtype)
        gl = gl_ref[...].astype(_F32)
        sl = _sigmoid(gl)
        dh_ref[...] = dba * (gl * sl)
        rest_ref[:, R_GLRU:R_MA] = (dba * h_ref[...].astype(_F32)
                                    * (sl * (1.0 + gl * (1.0 - sl)))).astype(rest_ref.dtype)

    v = _vec((1, D))
    r5, r10 = _rows(tb, ATT_W), _rows(tb, D)
    return _rowwise("tail_bwd", body, grid=s // tb,
                    ins=[(dx, r10), (out, r10), (y_a, r10), (y_b, r10), (proj, _rows(tb, D, CB_MA)),
                         (proj, _rows(tb, D, CB_MB)), (o, r5), (proj, _rows(tb, ATT_W, CB_GATT)), (h_lru, r10),
                         (proj, _rows(tb, D, CB_GLRU)), (gate, v), (g_post, v)]
                    + list(zip((gw["w_pa"], gw["w_pb"], gw["w_o"]), _weight_specs(l))),
                    outs=[(_sds((s, D), _MXU), r10), (_sds((s, D), _MXU), r10), (_sds((s, D), _MXU), r10),
                          (_sds((s, R_END), _MXU), _rows(tb, R_END)),
                          (_sds((s, ATT_W)), r5), (_sds((s, D)), r10), (_sds((1, D)), v), (_sds((1, D)), v)])


def _scan_bwd(dh, proj, conv_w, conv_b, h_lru, wt, b_rg, b_ig, lam):
    s = dh.shape[0]
    tb = 256
    n = s // tb

    def body(dh_ref, u_ref, up_ref, cw_ref, cb_ref, h_ref, hp_ref, wt_ref, brg_ref, big_ref, lam_ref,
             duc_ref, dwt_ref, dbrg_ref, dbig_ref, dlam_ref, carry, c_s, g_s):
        i = pl.program_id(0)

        @pl.when(i == 0)
        def _():
            carry[...] = jnp.zeros_like(carry)
            for acc_ref in (dwt_ref, dbrg_ref, dbig_ref, dlam_ref):
                acc_ref[...] = jnp.zeros_like(acc_ref)

        ucv = _conv(u_ref, up_ref, cw_ref, cb_ref, i == n - 1, tb)
        pre_r, pre_i = _gate_preacts(ucv, wt_ref)
        r, ig, sp, a, sq, inv_sq =_lru_gates(pre_r, pre_i, ucv, brg_ref[...], big_ref[...], lam_ref[...])
        row = lax.broadcasted_iota(jnp.int32, (tb, 1), 0)
        cv = jnp.where(row == tb - 1, 1.0, pltpu.roll(a, tb - 1, 0))
        gv = dh_ref[...]
        cv = cv.reshape(tb // 8, 8, D)
        gv = gv.reshape(tb // 8, 8, D)
        row8 = lax.broadcasted_iota(jnp.int32, (1, 8, 1), 1)
        for sh in (1, 2, 4):
            m = row8 < 8 - sh
            g_sh = pltpu.roll(gv, 8 - sh, 1)
            c_sh = pltpu.roll(cv, 8 - sh, 1)
            gv = jnp.where(m, gv + cv * g_sh, gv)
            cv = jnp.where(m, cv * c_sh, cv)
        c_s[...] = cv.reshape(tb, D)
        g_s[...] = gv.reshape(tb, D)

        def tile(k, state):
            rows = pl.ds(pl.multiple_of((tb // 8 - 1 - k) * 8, 8), 8)
            gt = g_s[rows, :] + c_s[rows, :] * state
            g_s[rows, :] = gt
            return jnp.broadcast_to(gt[0:1, :], (8, D))

        lax.fori_loop(0, tb // 8, tile, jnp.broadcast_to(carry[0:1, :], (8, D)), unroll=4)
        gv = g_s[...]
        carry[...] = (a * gv)[0:8]

        halo = jnp.where(i < n - 1, hp_ref[...].astype(_F32)[8:16], 0.0)
        h_prev = _shift_down(h_ref[...].astype(_F32), halo, 1, tb)
        d_a = gv * h_prev
        d_sq = gv * (ig * ucv)
        d_i = gv * sq * ucv
        d_la = d_a * a - d_sq * (a * a) * inv_sq
        d_r = d_la * (-LRU_C * sp)
        d_pre_r = d_r * r * (1.0 - r)
        d_pre_i = d_i * ig * (1.0 - ig)
        ucb = ucv.astype(_MXU)
        dpr = d_pre_r.astype(_MXU)
        dpi = d_pre_i.astype(_MXU)
        back = []
        for c in range(GATE_TILES):
            lanes = slice(128 * c, 128 * (c + 1))
            dp = jnp.concatenate([dpr[:, lanes], dpi[:, lanes]], axis=1)
            back.append(lax.dot_general(dp, wt_ref[c], _NT, preferred_element_type=_F32))
            dwt_ref[c] += lax.dot_general(ucb[:, lanes], dp, _TN, preferred_element_type=_F32)
        duc_ref[...] = gv * sq * ig + jnp.concatenate(back, axis=1)
        dbrg_ref[...] += jnp.sum(d_pre_r, axis=0, keepdims=True)
        dbig_ref[...] += jnp.sum(d_pre_i, axis=0, keepdims=True)
        lamv = lam_ref[...]
        dlam_ref[...] += jnp.sum(d_la * (-LRU_C * r), axis=0, keepdims=True) * (-_sigmoid(-lamv))

    v = _vec((1, D))
    rv = _rows(tb, D, 0, n)
    return _rowwise("scan_bwd", body, grid=n,
                    ins=[(dh, rv), (proj, _rows(tb, D, CB_U, n)), (proj, _halo_prev(tb, D, CB_U, n, rows=16)),
                         (conv_w, _vec((4, D))), (conv_b, v), (h_lru, rv), (h_lru, _halo_prev(tb, D, 0, n, rows=16)),
                         (wt, _vec((GATE_TILES, 128, 256))), (b_rg, v), (b_ig, v), (lam, v)],
                    outs=[(_sds((s, D)), rv), (_sds((GATE_TILES, 128, 256)), _vec((GATE_TILES, 128, 256))),
                          (_sds((1, D)), v), (_sds((1, D)), v), (_sds((1, D)), v)],
                    scratch=[pltpu.VMEM((8, D), _F32), pltpu.VMEM((tb, D), _F32), pltpu.VMEM((tb, D), _F32)])


def _conv_bwd(duc_a, proj, conv_w, rest):
    s = duc_a.shape[0]
    tb = 512
    n = s // tb
    hw = D // 2

    def body(da_ref, dan_ref, u_ref, up_ref, w_ref, rest_in, du_ref, dw_ref, dbias_ref):
        i = pl.program_id(1)
        duc = da_ref[...]
        nxt = jnp.where(i < n - 1, dan_ref[...], 0.0)
        u = u_ref[...].astype(_F32)
        halo = jnp.where(i > 0, up_ref[...].astype(_F32)[8:16], 0.0)
        du = duc * w_ref[0:1, :]
        dws = [jnp.sum(duc * u, axis=0, keepdims=True)]
        for j in range(1, 4):
            du = du + _shift_up(duc, nxt, j, tb) * w_ref[j:j + 1, :]
            dws.append(jnp.sum(duc * _shift_down(u, halo, j, tb), axis=0, keepdims=True))
        du_ref[...] = du.astype(du_ref.dtype)
        _zero_first(i, dw_ref, dbias_ref)
        for j in range(4):
            dw_ref[j:j + 1, :] += dws[j]
        dbias_ref[...] += jnp.sum(duc, axis=0, keepdims=True)

    r = pl.BlockSpec((tb, hw), lambda h, i: (i, h))
    nxt_spec = pl.BlockSpec((8, hw), lambda h, i: (jnp.minimum((i + 1) * (tb // 8), n * (tb // 8) - 1), h))
    return pl.pallas_call(
        body, name="conv_bwd", grid=(2, n),
        in_specs=[r, nxt_spec,
                  pl.BlockSpec((tb, hw), lambda h, i: (i, 2 * CB_U + h)),
                  pl.BlockSpec((16, hw), lambda h, i: (jnp.maximum(i * (tb // 16) - 1, 0), 2 * CB_U + h)),
                  pl.BlockSpec((4, hw), lambda h, i: (0, h)), pl.BlockSpec(memory_space=pl.ANY)],
        out_specs=[pl.BlockSpec((tb, hw), lambda h, i: (i, R_U // hw + h)),
                   pl.BlockSpec((4, hw), lambda h, i: (0, h)), pl.BlockSpec((1, hw), lambda h, i: (0, h))],
        out_shape=[_sds(rest.shape, rest.dtype), _sds((4, D)), _sds((1, D))],
        input_output_aliases={5: 0}, compiler_params=_params(2),
    )(duc_a, duc_a, proj, proj, conv_w, rest)


def _band_tiles(dil):
    tiles = []
    for rho in range(dil):
        for b in range(16 // dil):
            qs = rho + dil * BAND * b
            tiles.append((qs, QBLK + qs - dil * BAND, b))
    return tiles


def _strided(start, size, dil):
    return pl.ds(start, size, stride=dil) if dil > 1 else pl.ds(start, size)


def _band_mask(i, b):
    qi = lax.broadcasted_iota(jnp.int32, (BAND, 2 * BAND), 0)
    ki = lax.broadcasted_iota(jnp.int32, (BAND, 2 * BAND), 1)
    valid = (ki >= qi) & (ki <= qi + BAND)
    if b == 0:
        valid = valid & ((ki >= BAND) | (i > 0))
    return valid


def _attn_fwd(proj):
    s = proj.shape[0]
    n = s // QBLK
    scale = HEAD ** -0.5

    def body(*refs):
        q_refs, kp_refs, kc_refs, vp_refs, vc_refs = (refs[3 * t:3 * t + 3] for t in range(5))
        o_ref, lse_ref, qbuf, kbuf, vbuf = refs[15:20]
        accs, maxs, dens = refs[20:23], refs[23:26], refs[26:29]
        i = pl.program_id(1)
        for g, dil in enumerate(DILATIONS):
            qbuf[...] = q_refs[g][...].astype(_F32)
            kbuf[0:QBLK, :] = kp_refs[g][...].astype(_F32)
            kbuf[QBLK:2 * QBLK, :] = kc_refs[g][...].astype(_F32)
            vbuf[0:QBLK, :] = vp_refs[g][...].astype(_F32)
            vbuf[QBLK:2 * QBLK, :] = vc_refs[g][...].astype(_F32)
            for qs, ks, b in _band_tiles(dil):
                qsl = _strided(qs, BAND, dil)
                q = qbuf[qsl, :].astype(_MXU)
                kk = kbuf[_strided(ks, 2 * BAND, dil), :].astype(_MXU)
                vv = vbuf[_strided(ks, 2 * BAND, dil), :].astype(_MXU)
                sc = lax.dot_general(q, kk, _NT, preferred_element_type=_F32) * scale
                sc = jnp.where(_band_mask(i, b), sc, NEG_INF)
                m = jnp.max(sc, axis=-1, keepdims=True)
                p = jnp.exp(sc - m)
                accs[g][qsl, :] = jnp.dot(p.astype(_MXU), vv, preferred_element_type=_F32)
                maxs[g][qsl, :] = jnp.broadcast_to(m, (BAND, HEAD))
                dens[g][qsl, :] = jnp.broadcast_to(jnp.sum(p, axis=-1, keepdims=True), (BAND, HEAD))
        ms = [r[...] for r in maxs]
        mx = jnp.maximum(jnp.maximum(ms[0], ms[1]), ms[2])
        ws = [jnp.exp(m - mx) for m in ms]
        den = ws[0] * dens[0][...] + ws[1] * dens[1][...] + ws[2] * dens[2][...]
        o_ref[...] = ((ws[0] * accs[0][...] + ws[1] * accs[1][...] + ws[2] * accs[2][...]) / den).astype(o_ref.dtype)
        lse_ref[...] = mx + jnp.log(den)

    blk = (QBLK, HEAD)

    def spec(first_col, lag):
        specs = []
        for g in range(3):
            col = first_col + g * HEADS
            if lag:
                specs.append(pl.BlockSpec(blk, lambda j, i, col=col: (jnp.maximum(i - 1, 0), col + j)))
            else:
                specs.append(pl.BlockSpec(blk, lambda j, i, col=col: (i, col + j)))
        return specs

    out_spec = pl.BlockSpec(blk, lambda j, i: (i, j))
    return pl.pallas_call(
        body, name="attn_fwd", grid=(HEADS, n),
        in_specs=spec(0, False) + spec(12, True) + spec(12, False) + spec(24, True) + spec(24, False),
        out_specs=[out_spec] * 2, out_shape=[_sds((s, ATT_W), _MXU), _sds((s, ATT_W))],
        scratch_shapes=[pltpu.VMEM(blk, _F32)] + [pltpu.VMEM((2 * QBLK, HEAD), _F32)] * 2
        + [pltpu.VMEM(blk, _F32)] * 9,
        compiler_params=_params(2))(*([proj] * 15))


def _attn_bwd(proj, d_o, o, lse, g, into):
    s = proj.shape[0]
    dil = DILATIONS[g]
    n = s // QBLK
    scale = HEAD ** -0.5
    tiles = _band_tiles(dil)

    def body(*refs):
        q_ref, kp_ref, kc_ref, vp_ref, vc_ref, do_ref, o_ref, lse_ref = refs[0:8]
        dq_ref, dk_ref, dv_ref, kbuf, vbuf, dkbuf, dvbuf, dqbuf, qbuf, obuf = refs[-10:]
        i = pl.program_id(1)

        @pl.when(i == 0)
        def _():
            dkbuf[0:QBLK, :] = jnp.zeros((QBLK, HEAD), _F32)
            dvbuf[0:QBLK, :] = jnp.zeros((QBLK, HEAD), _F32)

        @pl.when(i < n)
        def _():
            qbuf[...] = q_ref[...].astype(_F32)
            obuf[...] = o_ref[...].astype(_F32)
            kbuf[0:QBLK, :] = kp_ref[...].astype(_F32)
            kbuf[QBLK:2 * QBLK, :] = kc_ref[...].astype(_F32)
            vbuf[0:QBLK, :] = vp_ref[...].astype(_F32)
            vbuf[QBLK:2 * QBLK, :] = vc_ref[...].astype(_F32)
            dkbuf[QBLK:2 * QBLK, :] = jnp.zeros((QBLK, HEAD), _F32)
            dvbuf[QBLK:2 * QBLK, :] = jnp.zeros((QBLK, HEAD), _F32)
            for qs, ks, b in tiles:
                qsl = _strided(qs, BAND, dil)
                ksl = _strided(ks, 2 * BAND, dil)
                q = qbuf[qsl, :].astype(_MXU)
                kk = kbuf[ksl, :].astype(_MXU)
                vv = vbuf[ksl, :].astype(_MXU)
                dov = do_ref[qsl, :]
                dd = jnp.sum(dov * obuf[qsl, :], axis=-1, keepdims=True)
                lse_t = lse_ref[qsl, :][:, 0:1]
                sc = lax.dot_general(q, kk, _NT, preferred_element_type=_F32) * scale
                p = jnp.where(_band_mask(i, b), jnp.exp(sc - lse_t), 0.0)
                dob = dov.astype(_MXU)
                dp = lax.dot_general(dob, vv, _NT, preferred_element_type=_F32)
                ds = (p * (dp - dd) * scale).astype(_MXU)
                dqbuf[qsl, :] = jnp.dot(ds, kk, preferred_element_type=_F32)
                dkbuf[ksl, :] += lax.dot_general(ds, q, _TN, preferred_element_type=_F32)
                dvbuf[ksl, :] += lax.dot_general(p.astype(_MXU), dob, _TN, preferred_element_type=_F32)
            dq_ref[...] = dqbuf[...].astype(dq_ref.dtype)

        dk_ref[...] = dkbuf[0:QBLK, :].astype(dk_ref.dtype)
        dv_ref[...] = dvbuf[0:QBLK, :].astype(dv_ref.dtype)
        dkbuf[0:QBLK, :] = dkbuf[QBLK:2 * QBLK, :]
        dvbuf[0:QBLK, :] = dvbuf[QBLK:2 * QBLK, :]

    blk = (QBLK, HEAD)
    cq, ck, cv = g * HEADS, 12 + g * HEADS, 24 + g * HEADS

    def cur(i):
        return jnp.minimum(i, n - 1)

    def prev(i):
        return jnp.maximum(jnp.minimum(i, n - 1) - 1, 0)

    own = pl.BlockSpec(blk, lambda j, i: (cur(i), j))
    own_out = pl.BlockSpec(blk, lambda j, i: (cur(i), cq + j))
    late_out = pl.BlockSpec(blk, lambda j, i: (jnp.maximum(i - 1, 0), cq + j))
    extra = [] if into is None else list(into)
    return pl.pallas_call(
        body, name="attn_bwd_d%d" % dil, grid=(HEADS, n + 1),
        in_specs=[pl.BlockSpec(blk, lambda j, i: (cur(i), cq + j)),
                  pl.BlockSpec(blk, lambda j, i: (prev(i), ck + j)),
                  pl.BlockSpec(blk, lambda j, i: (cur(i), ck + j)),
                  pl.BlockSpec(blk, lambda j, i: (prev(i), cv + j)),
                  pl.BlockSpec(blk, lambda j, i: (cur(i), cv + j)),
                  own, own, own] + [pl.BlockSpec(memory_space=pl.ANY)] * len(extra),
        out_specs=[own_out, late_out, late_out], out_shape=[_sds((s, QKV_W), _MXU)] * 3,
        input_output_aliases={8 + t: t for t in range(len(extra))},
        scratch_shapes=[pltpu.VMEM((2 * QBLK, HEAD), _F32)] * 4 + [pltpu.VMEM((QBLK, HEAD), _F32)] * 3,
        compiler_params=_params(2))(proj, proj, proj, proj, proj, d_o, o, lse, *extra)


_PARTS = ((0, 2), (2, 2), (4, 2), (6, 6))
_CHUNK = 768


def _d_x(name, parts, w_in, x, dx_out, g_pre, scale, blocks, into):
    s = parts[0].shape[0]
    nk = IN_W // _CHUNK
    first_block, n_blocks = blocks

    def body(*refs):
        p0, p1, p2, p3, w_ref, x_ref, dxo_ref, g_ref, sc_ref = refs[0:9]
        dx_ref, dsh_ref, dsc_ref, dg_ref, acc = refs[-5:]
        m = pl.program_id(0)
        k = pl.program_id(2)

        @pl.when(k == 0)
        def _():
            acc[...] = jnp.zeros_like(acc)

        @pl.when((k == 0) & (m == 0))
        def _():
            for ref in (dsh_ref, dsc_ref, dg_ref):
                ref[...] = jnp.zeros_like(ref)

        for p_ref, (first, cnt) in zip((p0, p1, p2, p3), _PARTS):
            @pl.when((k >= first) & (k < first + cnt))
            def _(p_ref=p_ref):
                acc[...] += lax.dot_general(p_ref[...].astype(_MXU), w_ref[...], _NT, preferred_element_type=_F32)

        @pl.when(k == nk - 1)
        def _():
            dhv = acc[...]
            xv = x_ref[...]
            rstd = lax.rsqrt(jnp.mean(xv * xv, axis=-1, keepdims=True) + NORM_EPS)
            xn = xv * rstd
            one_sc = 1.0 + sc_ref[...]
            s1 = jnp.sum(dhv * xn, axis=0, keepdims=True)
            dsh_ref[...] += jnp.sum(dhv, axis=0, keepdims=True)
            dsc_ref[...] += s1 * g_ref[...]
            dg_ref[...] += s1 * one_sc
            dxn = dhv * (g_ref[...] * one_sc)
            dx_ref[...] = dxo_ref[...] + rstd * (dxn - xn * jnp.mean(dxn * xn, axis=-1, keepdims=True))

    def part_spec(first, cnt):
        return pl.BlockSpec((1024, _CHUNK), lambda m, n, k: (first_block + m, jnp.clip(k - first, 0, cnt - 1)))

    rows = pl.BlockSpec((1024, D), lambda m, n, k: (first_block + m, 0))
    vec = pl.BlockSpec((1, D), lambda m, n, k: (0, 0))
    extra = [] if into is None else [into]
    return pl.pallas_call(
        body, name=name, grid=(n_blocks, 1, nk),
        in_specs=[part_spec(*p) for p in _PARTS]
        + [pl.BlockSpec((None, D, _CHUNK), lambda m, n, k: (k // 3, 0, k % 3)), rows, rows, vec, vec]
        + [pl.BlockSpec(memory_space=pl.ANY)] * len(extra),
        out_specs=[rows, vec, vec, vec], out_shape=[_sds((s, D)), _sds((1, D)), _sds((1, D)), _sds((1, D))],
        input_output_aliases={9: 0} if extra else {},
        scratch_shapes=[pltpu.VMEM((1024, D), _F32)], compiler_params=_params(3))(
            *parts, w_in, x, dx_out, g_pre, scale, *extra)


def _g_w_in(h_t, parts):
    s = h_t.shape[1]
    tk = 2048
    nk = s // tk

    def body(*refs):
        h_ref, p_refs = refs[0], refs[1:5]
        o_ref, acc = refs[-2], refs[-1]
        n = pl.program_id(1)
        k = pl.program_id(2)

        @pl.when(k == 0)
        def _():
            acc[...] = jnp.zeros_like(acc)

        for p_ref, (first, cnt) in zip(p_refs, _PARTS):
            @pl.when((n >= first) & (n < first + cnt))
            def _(p_ref=p_ref):
                acc[...] += jnp.dot(h_ref[...], p_ref[...].astype(_MXU), preferred_element_type=_F32)

        @pl.when(k == nk - 1)
        def _():
            o_ref[...] = acc[...]

    def part_spec(first, cnt):
        def index(m, n, k):
            row = jnp.where(n < first, 0, jnp.where(n >= first + cnt, nk - 1, k))
            return (row, jnp.clip(n - first, 0, cnt - 1))
        return pl.BlockSpec((tk, _CHUNK), index)

    return pl.pallas_call(
        body, name="g_w_in", grid=(1, IN_W // _CHUNK, nk),
        in_specs=[pl.BlockSpec((D, tk), lambda m, n, k: (0, k))] + [part_spec(*p) for p in _PARTS],
        out_specs=pl.BlockSpec((None, D, _CHUNK), lambda m, n, k: (n // 3, 0, n % 3)),
        out_shape=_sds((N_CHIPS, D, 2304)),
        scratch_shapes=[pltpu.VMEM((D, _CHUNK), _F32)], compiler_params=_params(3))(h_t, *parts)


def _layer_fwd(l, x, p, gw, late, target):
    if callable(gw["w_in"][l]):
        gw["w_in"][l] = gw["w_in"][l](x)
    proj, h_t = _proj(x, p["g_pre"], p["shift"], p["scale"], gw["w_in"][l])
    o, lse = _attn_fwd(proj)
    h_lru = _scan_fwd(proj, p["conv_w"], p["conv_b"], p["wt"], p["b_rg"], p["b_ig"], p["lam"])
    if late is not None:
        landed = dict(late(h_lru))
        gw["w_in"].append(landed.pop("w_in1"))
        gw.update(landed)
    a_att, b_act, y_a, y_b, z, out, *last = _tail_fwd(l, o, h_lru, proj, x, p["gate"], p["g_post"], gw, target)
    saved = dict(x=x, h_t=h_t, proj=proj, o=o, lse=lse, h_lru=h_lru, a_att=a_att, b_act=b_act,
                 y_a=y_a, y_b=y_b, z=z, out=out)
    return (last[0] if target is None else last), saved


def _layer_bwd(l, dx, p, gw, sv, hooks):
    s = dx.shape[0]
    nt = s // 2048
    proj = sv["proj"]
    gate, b_rg, g_pre = p["gate"], p["b_rg"], p["g_pre"]
    if hooks is not None:
        gate = gate + hooks[0]([dx])
    d_out, dy_a, dy_b, d_rest, d_o, dh_lru, d_gate, d_gpost = _tail_bwd(
        l, dx, sv["out"], sv["y_a"], sv["y_b"], proj, sv["o"], sv["h_lru"], gate, p["g_post"], gw)
    if hooks is not None:
        b_rg = b_rg + hooks[1]([d_out])

    def wgrad_rows(name, a, b):
        return _mm(name, a, b, _sds((N_CHIPS, 256, D)), grid=(1, 1, nt),
                   a_spec=pl.BlockSpec((D, 2048), lambda m, n, k: (0, k)),
                   b_spec=pl.BlockSpec((2048, D), lambda m, n, k: (k, 0)),
                   o_spec=pl.BlockSpec((N_CHIPS, 256, D), lambda m, n, k: (0, 0, 0)),
                   dims=_NN, acc_shape=(D, D))

    big = {}
    big["w_o"] = wgrad_rows("g_w_o", sv["z"], d_out)
    big["w_pa"] = _mm("g_w_pa", sv["a_att"], dy_a, _sds((N_CHIPS, ATT_W, 256)), grid=(1, 4, nt),
                      a_spec=pl.BlockSpec((ATT_W, 2048), lambda m, n, k: (0, k)),
                      b_spec=pl.BlockSpec((2048, 256), lambda m, n, k: (k, n)),
                      o_spec=pl.BlockSpec((None, ATT_W, 256), lambda m, n, k: (n, 0, 0)),
                      dims=_NN, acc_shape=(ATT_W, 256))
    big["w_pb"] = wgrad_rows("g_w_pb", sv["b_act"], dy_b)
    duc, g_wt, d_brg, d_big, d_lam = _scan_bwd(dh_lru, proj, p["conv_w"], p["conv_b"], sv["h_lru"], p["wt"], b_rg,
                                               p["b_ig"], p["lam"])
    g_wrg, g_wig = _gate_tile_grads(g_wt)
    d_rest, g_convw, g_convb = _conv_bwd(duc, proj, p["conv_w"], d_rest)
    dqkv = None
    for g in range(3):
        dqkv = _attn_bwd(proj, d_o, sv["o"], sv["lse"], g, dqkv)
    if hooks is not None:
        g_pre = g_pre + hooks[2]([dqkv[0]])
    parts = (dqkv[0], dqkv[1], dqkv[2], d_rest)
    big["w_in"] = _g_w_in(sv["h_t"], parts)
    nb = s // 1024
    if hooks is None:
        dx_in, d_shift, d_scale, d_gpre = _d_x("d_x", parts, gw["w_in"][l], sv["x"], dx, g_pre, p["scale"],
                                               (0, nb), None)
    else:
        first = _d_x("d_x_a", parts, gw["w_in"][l], sv["x"], dx, g_pre + hooks[3](big), p["scale"],
                     (0, nb // 2), None)
        second = _d_x("d_x_b", parts, gw["w_in"][l], sv["x"], dx, g_pre + hooks[4]([first[0]]), p["scale"],
                      (nb // 2, nb - nb // 2), first[0])
        dx_in = second[0]
        d_shift, d_scale, d_gpre = (a + b for a, b in zip(first[1:], second[1:]))
    small = dict(dmod=jnp.concatenate([d_shift, d_scale, d_gate], axis=1), g_pre=d_gpre, conv_w=g_convw,
                 conv_b=g_convb, w_rg=g_wrg, b_rg=d_brg, w_ig=g_wig, b_ig=d_big, lam=d_lam, g_post=d_gpost)
    return dx_in, small, big


_BIG = ("w_in", "w_pa", "w_pb", "w_o")


class _GradReduce:
    PAIR_CHUNKS = (2, 1, 1, 1)
    CHIP_CHUNKS = (2, 1, 1, 1)
    FILL_CHUNKS = (4, 1, 1, 1)

    def __init__(self, core, where):
        self.core, self.where = core, where
        self.finals = None

    def begin(self, l, big):
        n = len(_BIG)
        halves = [big[k].reshape(N_CHIPS, 2, big[k].shape[1] // 2, big[k].shape[2]) for k in _BIG]
        lands = [lax.empty((N_CHIPS,) + h.shape[2:], _F32) for h in halves]
        plan, nsem = _pair_plan(n, self.PAIR_CHUNKS)
        state = {}
        state["pair"] = _split_start("reduce_pair_start_%d" % l, halves + lands, plan, nsem, [])

        def started(after):
            return state["pair"][3][0, 0]

        def pair_done(after):
            send, recv, arrays, _ = state["pair"]
            arrays = _split_wait("reduce_pair_wait_%d" % l, send, recv, arrays, plan, after)
            sums = [_sum_pair("sum_pair_%s_%d" % (k, l), arrays[a], arrays[n + a], self.core,
                              min(256, arrays[a].shape[2]))
                    for a, k in enumerate(_BIG)]
            state["mine"] = [t[0] for t in sums]
            lands2 = [lax.empty(t[1].shape, _MXU) for t in sums]
            plan2, nsem2 = _chips_plan(n, self.CHIP_CHUNKS)
            state["plan2"] = plan2
            state["chips"] = _split_start("reduce_chips_start_%d" % l, [t[1] for t in sums] + lands2, plan2, nsem2, [])
            return state["chips"][3][0, 0]

        def chips_done(after):
            send, recv, arrays, _ = state["chips"]
            arrays = _split_wait("reduce_chips_wait_%d" % l, send, recv, arrays, state["plan2"], after)
            finals = [_sum_chips("sum_chips_%s_%d" % (k, l), state["mine"][a], arrays[n + a], self.where, l,
                                 None if self.finals is None else self.finals[a],
                                 min(256, state["mine"][a].shape[1]))
                      for a, k in enumerate(_BIG)]
            plan3, nsem3 = _fill_plan(n, self.FILL_CHUNKS, l)
            state["plan3"] = plan3
            state["fill"] = _split_start("gather_halves_start_%d" % l, finals, plan3, nsem3, [])
            return state["fill"][3][0, 0]

        def finish(after):
            send, recv, arrays, _ = state["fill"]
            self.finals = _split_wait("gather_halves_wait_%d" % l, send, recv, arrays, state["plan3"], after)
            return self.finals

        self._finish = finish
        return [started, pair_done, chips_done]

    def finish(self, after):
        return self._finish(after)


def _local_step(x, target, small_p, w_in0, late, reducer, on_smalls):
    saved = []
    h = x
    gw = dict(w_in=[w_in0])
    h, sv = _layer_fwd(0, h, small_p[0], gw, late, None)
    saved.append(sv)
    (dy, sq), sv = _layer_fwd(1, h, small_p[1], gw, None, target)
    saved.append(sv)
    loss = 0.5 * jnp.sum(sq) / D
    smalls = [None, None]
    dx, smalls[1], big1 = _layer_bwd(1, dy, small_p[1], gw, saved[1], None)
    hooks1 = reducer.begin(1, big1)
    small_started = on_smalls(1, smalls[1])
    pair_started = hooks1[0]
    hooks1[0] = lambda after: pair_started(after) + small_started
    own = {}

    def layer0_ready(big0):
        reducer.finish([big0["w_in"]])
        own["hooks"] = reducer.begin(0, big0)
        return own["hooks"][0]([])

    dx, smalls[0], _ = _layer_bwd(0, dx, small_p[0], gw, saved[0],
                                  hooks1 + [layer0_ready, lambda after: own["hooks"][1](after)])
    on_smalls(0, smalls[0])

    def finish_reduce(after):
        own["hooks"][2](after)
        return reducer.finish(after)

    return loss, dx, smalls, finish_reduce


_SMALL_ROWS = 8 + 8 + 8 + 64 + 64
_SMALL_VECS = ("g_pre", "conv_b", "b_rg", "b_ig", "lam", "g_post")


def _pack_small(small):
    pad = lambda rows: jnp.zeros((rows, D), _F32)
    return jnp.concatenate(
        [small["dmod"].reshape(3, D), pad(5)] + [small[k] for k in _SMALL_VECS] + [pad(2)]
        + [small["conv_w"], pad(4), small["w_rg"].reshape(64, D), small["w_ig"].reshape(64, D)], axis=0)


def kernel(x, c, w_mod, b_mod, g_pre, w_in, conv_w, conv_b, w_rg, b_rg, w_ig, b_ig, lru_lambda, w_pa, w_pb, w_o, g_post, loss_target, m_w_mod, m_b_mod, m_g_pre, m_w_in, m_conv_w, m_conv_b, m_w_rg, m_b_rg, m_w_ig, m_b_ig, m_lru_lambda, m_w_pa, m_w_pb, m_w_o, m_g_post, v_w_mod, v_b_mod, v_g_pre, v_w_in, v_conv_w, v_conv_b, v_w_rg, v_b_rg, v_w_ig, v_b_ig, v_lru_lambda, v_w_pa, v_w_pb, v_w_o, v_g_post):
    xi, yi, ci = lax.axis_index("x"), lax.axis_index("y"), lax.axis_index("c")
    chip = 2 * xi + yi
    dev = 4 * xi + 2 * yi + ci
    mcols = w_mod.shape[2]

    pack1 = jnp.concatenate([jnp.broadcast_to(c, (8, D)),
                             jnp.pad(conv_w.reshape(8, 256), ((0, 0), (0, D - 256)))], axis=0)
    g1 = _exchange("gather_cond", [pack1], "xyc", False)[0]
    c_all = g1[:, 0, :]
    conv_w_full = jnp.transpose(g1[0::2, 8:16, 0:256], (1, 0, 2)).reshape(2, 4, D)

    b_cols = lax.dynamic_slice(b_mod, (0, chip * mcols), (2, mcols)).reshape(2, 1, mcols)
    mod_loc = _mod_fwd(c_all, w_mod, b_cols)
    g2 = _exchange("gather_mod", [mod_loc.reshape(16, mcols)], "xyc", False)[0]
    mod_full = jnp.transpose(g2[0::2], (1, 0, 2)).reshape(2, 8, 3 * D)
    mod_me = lax.dynamic_index_in_dim(mod_full, dev, axis=1, keepdims=False)

    wb_in = _cast("cast_w_in", w_in.reshape(2 * D, 2304), 256).reshape(2, D, 2304)
    late_src = [wb_in[1], _cast("cast_w_pa", w_pa.reshape(2 * ATT_W, 256), 256).reshape(2, ATT_W, 256),
                _cast("cast_w_pb", w_pb.reshape(512, D), 256).reshape(2, 256, D),
                _cast("cast_w_o", w_o.reshape(512, D), 256).reshape(2, 256, D)]
    late_chunks = [4, 2, 2, 2]
    w_in0 = _gather_weights([wb_in[0].reshape(2, D // 2, 2304)], [2])[0].reshape(N_CHIPS, D, 2304)
    chip1 = jnp.reshape(chip, (1,)).astype(jnp.int32)
    lands = [_own_slot("own_slot_" + k, a, chip1, 256) for k, a in zip(("w_in", "w_pa", "w_pb", "w_o"), late_src)]
    plan_a, nsem_a = _gather_plan(3, late_chunks[1:])
    send_a, recv_a, arrays_a, token_a = _split_start(
        "late_gather_start_a", late_src[1:] + lands[1:], plan_a, nsem_a, [w_in0, mod_me])
    plan_b, nsem_b = _gather_plan(1, late_chunks[:1])
    send_b, recv_b, arrays_b, token_b = _split_start(
        "late_gather_start_b", late_src[:1] + lands[:1], plan_b, nsem_b, [w_in0, mod_me, arrays_a[0]])
    token = token_a + token_b

    def late(after):
        got = _split_wait("late_gather_wait_a", send_a, recv_a, arrays_a, plan_a, [after])[3:]
        w_in1 = lambda later: _split_wait("late_gather_wait_b", send_b, recv_b, arrays_b, plan_b, [later])[1]
        return dict(w_in1=w_in1, w_pa=got[0], w_pb=got[1], w_o=got[2])

    small_p = []
    for l in range(2):
        gates = _gate_tiles(w_rg[l], w_ig[l]).astype(_MXU)
        small_p.append(dict(
            shift=mod_me[l:l + 1, 0:D], scale=mod_me[l:l + 1, D:2 * D], gate=mod_me[l:l + 1, 2 * D:3 * D],
            g_pre=g_pre[l:l + 1], conv_w=conv_w_full[l], conv_b=conv_b[l:l + 1], wt=gates,
            b_rg=b_rg[l:l + 1], b_ig=b_ig[l:l + 1], lam=lru_lambda[l:l + 1], g_post=g_post[l:l + 1]))

    small_p[0]["shift"] = small_p[0]["shift"] + token[0, 0]

    core = jnp.reshape(ci, (1,)).astype(jnp.int32)
    where = jnp.stack([chip, ci]).astype(jnp.int32)
    dev1 = jnp.reshape(dev, (1,)).astype(jnp.int32)
    small_plan, small_nsem = _all_plan()
    small_state = {}

    def on_smalls(l, small):
        pack = _pack_small(small)
        land = _own_slot("own_small_%d" % l, pack, dev1, _SMALL_ROWS, slots=8)
        small_state[l] = _split_start("gather_small_start_%d" % l, [pack, land], small_plan, small_nsem, [])
        return small_state[l][3][0, 0]

    def small_done(l, after):
        send, recv, arrays, _ = small_state[l]
        return _split_wait("gather_small_wait_%d" % l, send, recv, arrays, small_plan, after)[1]

    loss_loc, dx, _, finish_reduce = _local_step(x[0], loss_target[0], small_p, w_in0, late,
                                                 _GradReduce(core, where), on_smalls)
    loss = lax.psum(loss_loc, ("x", "y", "c"))
    grad_x = dx[None]
    reduced = finish_reduce([dx, small_state[0][2][0]])
    g_big ={k: a.reshape(2, 2 * a.shape[2], a.shape[3]) for k, a in zip(_BIG, reduced)}

    weights = dict(w_mod=w_mod, b_mod=b_mod, g_pre=g_pre, w_in=w_in, conv_w=conv_w, conv_b=conv_b, w_rg=w_rg,
                   b_rg=b_rg, w_ig=w_ig, b_ig=b_ig, lru_lambda=lru_lambda, w_pa=w_pa, w_pb=w_pb, w_o=w_o,
                   g_post=g_post)
    ms = dict(w_mod=m_w_mod, b_mod=m_b_mod, g_pre=m_g_pre, w_in=m_w_in, conv_w=m_conv_w, conv_b=m_conv_b,
              w_rg=m_w_rg, b_rg=m_b_rg, w_ig=m_w_ig, b_ig=m_b_ig, lru_lambda=m_lru_lambda, w_pa=m_w_pa,
              w_pb=m_w_pb, w_o=m_w_o, g_post=m_g_post)
    vs = dict(w_mod=v_w_mod, b_mod=v_b_mod, g_pre=v_g_pre, w_in=v_w_in, conv_w=v_conv_w, conv_b=v_conv_b,
              w_rg=v_w_rg, b_rg=v_b_rg, w_ig=v_w_ig, b_ig=v_b_ig, lru_lambda=v_lru_lambda, w_pa=v_w_pa,
              w_pb=v_w_pb, w_o=v_w_o, g_post=v_g_post)
    flat = dict(w_mod=(2 * D, mcols, 512), b_mod=(2, 3 * D, 2), g_pre=(2, D, 2), w_in=(2 * D, 2304, 256),
                conv_w=(8, 256, 8), conv_b=(2, D, 2), w_rg=(128, D, 128), b_rg=(2, D, 2), w_ig=(128, D, 128),
                b_ig=(2, D, 2), lru_lambda=(2, D, 2), w_pa=(2 * ATT_W, 256, 256), w_pb=(512, D, 256),
                w_o=(512, D, 256), g_post=(2, D, 2))
    order = ("w_mod", "b_mod", "g_pre", "w_in", "conv_w", "conv_b", "w_rg", "b_rg", "w_ig", "b_ig",
             "lru_lambda", "w_pa", "w_pb", "w_o", "g_post")

    def adam(k, g):
        rows, cols, tb = flat[k]
        return _adamw("adamw_" + k, weights[k].reshape(rows, cols), g.reshape(rows, cols),
                      ms[k].reshape(rows, cols), vs[k].reshape(rows, cols), tb)

    stepped = {k: adam(k, g_big[k]) for k in _BIG}

    g3 = [small_done(l, [stepped["w_in"][0]]) for l in range(2)]
    tot =[_sum_lead("sum_small_%d" % l, g3[l], _SMALL_ROWS) for l in range(2)]
    dmod_all = jnp.stack([g3[l][:, 0:3, :].reshape(8, 3 * D) for l in range(2)], axis=0)
    dm_cols = lax.dynamic_slice(dmod_all, (0, 0, chip * mcols), (2, 8, mcols))
    g_w_mod = _mod_bwd(jnp.transpose(c_all), dm_cols)
    both = lambda first, rows: jnp.stack([tot[l][first:first + rows] for l in range(2)], axis=0)
    vec = both(8, 6)
    grads = dict(
        w_mod=g_w_mod, b_mod=both(0, 3).reshape(2, 3 * D), g_pre=vec[:, 0], w_in=g_big["w_in"],
        conv_w=lax.dynamic_slice(both(16, 4), (0, 0, chip * 256), (2, 4, 256)), conv_b=vec[:, 1],
        w_rg=both(24, 64).reshape(2, 16, 64, 64), b_rg=vec[:, 2], w_ig=both(88, 64).reshape(2, 16, 64, 64),
        b_ig=vec[:, 3], lru_lambda=vec[:, 4], w_pa=g_big["w_pa"], w_pb=g_big["w_pb"], w_o=g_big["w_o"],
        g_post=vec[:, 5])
    for k in order:
        if k not in stepped:
            stepped[k] = adam(k, grads[k])
    deltas, new_m, new_v = ([stepped[k][t].reshape(weights[k].shape) for k in order] for t in range(3))
    return (loss, grad_x, *[grads[k].reshape(weights[k].shape) for k in order], *deltas, *new_m, *new_v)
```
